```python
import math
import jax
import jax.numpy as jnp
from jax import lax
import numpy as np

D_MODEL = 2048
BATCH = 2
SEQ = 8192
DEPTH = 2

GRID_W = 64
CTX_LEN = 256
N_EVEN = (DEPTH + 1) // 2
N_ODD = DEPTH // 2
EPS = 1e-6

DIFF_QK = 64
DIFF_HEADS = D_MODEL // (4 * DIFF_QK)
DIFF_V = 2 * DIFF_QK
DIFF_WIDTH = DIFF_HEADS * DIFF_V
ROPE_BASE = 10000.0
Q_BLOCK = 128
POOL_WINDOWS = (2, 4, 8, 16)
POOL_WIDTH = D_MODEL - DIFF_WIDTH
POOL_GROUP = POOL_WIDTH // len(POOL_WINDOWS)
AB_IN = 2 * DIFF_HEADS * 2 * DIFF_QK + DIFF_WIDTH + POOL_WIDTH

HY_WIDTH = D_MODEL // 2
HY_ORDER = 2
HY_EMB = 33
HY_HIDDEN = 64
HY_FAST_DECAY = 0.3
HY_SLOW_DECAY = 1.5
HY_TARGET = 1e-2
S5_WIDTH = D_MODEL - HY_WIDTH
S5_GROUP = 16
S5_GROUPS = S5_WIDTH // S5_GROUP
S5_STATE = 64
CD_IN = (HY_ORDER + 1) * HY_WIDTH + S5_WIDTH

MOE_GROUPS = 8
MOE_PER_GROUP = 8
MOE_EXPERTS = MOE_GROUPS * MOE_PER_GROUP
MOE_TOPK = 2
D_EXPERT = 512
MOE_BLOCK = 256

kernel_name = "hybrid_diffattn_pool_hyena_s5_hmoe_dit"


def rmsnorm(x, g):
    xf = x.astype(jnp.float32)
    y = xf * lax.rsqrt(jnp.mean(xf * xf, axis=-1, keepdims=True) + EPS)
    return (y * g.astype(jnp.float32)).astype(x.dtype)


def modulate(x, shift, scale):
    return x * (1 + scale) + shift


def axial_rope_angles(L):
    rows = L // GRID_W
    row = jnp.repeat(jnp.arange(rows, dtype=jnp.float32), GRID_W)
    col = jnp.tile(jnp.arange(GRID_W, dtype=jnp.float32), rows)
    n_freq = DIFF_QK // 4
    inv = ROPE_BASE ** (-jnp.arange(n_freq, dtype=jnp.float32) / n_freq)
    return row[:, None] * inv, col[:, None] * inv


def apply_axial_rope(x, ang_r, ang_c):
    def rot(xp, ang):
        ang = ang[None, :, None, None, :]
        cos = jnp.cos(ang).astype(x.dtype)
        sin = jnp.sin(ang).astype(x.dtype)
        a, b = jnp.split(xp, 2, axis=-1)
        return jnp.concatenate([a * cos - b * sin, a * sin + b * cos], axis=-1)
    half = x.shape[-1] // 2
    return jnp.concatenate([rot(x[..., :half], ang_r), rot(x[..., half:], ang_c)], axis=-1)


def diff_attend(q, k, v, lam):
    s = jnp.einsum('bhmqd,bhmkd->bhmqk', q, k).astype(jnp.float32) * (DIFF_QK ** -0.5)
    p = jax.nn.softmax(s, axis=-1)
    a = p[:, :, 0] - lam * p[:, :, 1]
    return jnp.einsum('bhqk,bhkd->bhqd', a.astype(v.dtype), v)


def multiscale_pool(p, pool_w, pool_scale):
    B, L, _ = p.shape
    pg = p.reshape(B, L, len(POOL_WINDOWS), POOL_GROUP).astype(jnp.float32)
    cs = jnp.concatenate([jnp.zeros_like(pg[:, :1]), jnp.cumsum(pg, axis=1)], axis=1)
    t = jnp.arange(L)
    means = []
    for gi, w in enumerate(POOL_WINDOWS):
        lo = jnp.clip(t - w // 2, 0, L)
        hi = jnp.clip(t - w // 2 + w, 0, L)
        csg = cs[:, :, gi]
        means.append((csg[:, hi] - csg[:, lo]) / (hi - lo).astype(jnp.float32)[None, :, None])
    d = (jnp.stack(means, axis=2) - pg).astype(p.dtype)
    y = jnp.einsum('blgc,gcd->blgd', d, pool_w).reshape(B, L, POOL_WIDTH)
    return y * pool_scale


def diff_pool_mixer(h, hc, w_in, w_out, lam_vecs, subln_g, pool_w, pool_scale, lam_init, ctx_out):
    B, L, _ = h.shape
    C = hc.shape[1]
    H, d, dv = DIFF_HEADS, DIFF_QK, DIFF_V
    qw = H * 2 * d
    vw = H * dv
    proj = h @ w_in
    ang_r, ang_c = axial_rope_angles(L)
    q = apply_axial_rope(proj[..., :qw].reshape(B, L, H, 2, d), ang_r, ang_c)
    k = apply_axial_rope(proj[..., qw:2 * qw].reshape(B, L, H, 2, d), ang_r, ang_c)
    v = proj[..., 2 * qw:2 * qw + vw].reshape(B, L, H, dv)
    p = proj[..., 2 * qw + vw:]
    if ctx_out:
        projc = hc @ w_in
        kvc = projc[..., qw:2 * qw + vw]
    else:
        kvc = hc @ w_in[:, qw:2 * qw + vw]
    kc = kvc[..., :qw].reshape(B, C, H, 2, d)
    vc = kvc[..., qw:].reshape(B, C, H, dv)
    lv = lam_vecs.astype(jnp.float32)
    lam = jnp.exp(jnp.dot(lv[0], lv[1])) - jnp.exp(jnp.dot(lv[2], lv[3])) + lam_init
    k_all = jnp.concatenate([kc, k], axis=1).transpose(0, 2, 3, 1, 4)
    v_all = jnp.concatenate([vc, v], axis=1).transpose(0, 2, 1, 3)
    nb = L // Q_BLOCK
    qb = q.reshape(B, nb, Q_BLOCK, H, 2, d).transpose(1, 0, 3, 4, 2, 5)
    o = lax.map(lambda qq: diff_attend(qq, k_all, v_all, lam), qb)
    o = o.transpose(1, 0, 3, 2, 4).reshape(B, L, H, dv)

    def head_out(oh):
        return (rmsnorm(oh, subln_g) * (1 - lam_init)).reshape(oh.shape[0], oh.shape[1], H * dv)

    y = jnp.concatenate([head_out(o), multiscale_pool(p, pool_w, pool_scale)], axis=-1) @ w_out
    if not ctx_out:
        return y, None
    qc = projc[..., :qw].reshape(B, C, H, 2, d).transpose(0, 2, 3, 1, 4)
    oc = diff_attend(qc, kc.transpose(0, 2, 3, 1, 4), vc.transpose(0, 2, 1, 3), lam).transpose(0, 2, 1, 3)
    yc = jnp.concatenate([head_out(oc), multiscale_pool(projc[..., 2 * qw + vw:], pool_w, pool_scale)], axis=-1) @ w_out
    return y, yc


def short_conv(u, w, b):
    up = jnp.pad(u, ((0, 0), (1, 1), (0, 0)))
    return up[:, :-2] * w[0] + up[:, 1:-1] * w[1] + up[:, 2:] * w[2] + b


def hyena_filters(L, w1, b1, freq, w2, b2, w3):
    f32 = jnp.float32
    t = jnp.linspace(0.0, 1.0, L, dtype=f32)[:, None]
    bands = (HY_EMB - 1) // 2
    w = 2 * math.pi * jnp.arange(L, dtype=f32)[:, None] / L
    f = jnp.linspace(1e-4, bands - 1, bands, dtype=f32)[None, :]
    z = jnp.concatenate([t, jnp.cos(f * w), -jnp.sin(f * w)], axis=-1)
    fr = freq.astype(f32)
    hid = jnp.sin(fr[0] * (z @ w1.astype(f32) + b1.astype(f32)))
    hid = jnp.sin(fr[1] * (hid @ w2.astype(f32) + b2.astype(f32)))
    k = (hid @ w3.astype(f32)).reshape(L, HY_ORDER, 2, HY_WIDTH)
    max_decay = math.log(HY_TARGET) / HY_FAST_DECAY
    min_decay = math.log(HY_TARGET) / HY_SLOW_DECAY
    deltas = jnp.linspace(min_decay, max_decay, HY_WIDTH, dtype=f32)
    k = k * jnp.exp(-t * jnp.abs(deltas))[:, None, None, :]
    norm = jnp.sum(jnp.abs(k[:, :, 0]), axis=0) + jnp.sum(jnp.abs(k[1:, :, 1]), axis=0)
    return k / norm[None, :, None, :]


def long_conv(u, k_fwd, k_bwd, bias):
    L = u.shape[1]
    uf = u.astype(jnp.float32)
    k_full = jnp.concatenate([k_fwd, jnp.zeros_like(k_fwd[:1]), k_bwd[:0:-1]], axis=0)
    spec = jnp.fft.rfft(uf, n=2 * L, axis=1) * jnp.fft.rfft(k_full, axis=0)[None]
    y = jnp.fft.irfft(spec, n=2 * L, axis=1)[:, :L]
    return (y + uf * bias.astype(jnp.float32)).astype(u.dtype)


def hyena(u, conv_w, conv_b, f_w1, f_b1, f_freq, f_w2, f_b2, f_w3, hy_bias):
    L = u.shape[1]
    g = short_conv(u, conv_w, conv_b)
    v, x1, x2 = jnp.split(g, HY_ORDER + 1, axis=-1)
    k = hyena_filters(L, f_w1, f_b1, f_freq, f_w2, f_b2, f_w3)
    z = x1 * long_conv(v, k[:, 0, 0], k[:, 0, 1], hy_bias[0])
    return x2 * long_conv(z, k[:, 1, 0], k[:, 1, 1], hy_bias[1])


def s5_discretize(a_re, a_im, log_dt, b_re, b_im):
    f32 = jnp.float32
    lam = lax.complex(a_re.astype(f32), a_im.astype(f32))
    dt = jnp.exp(log_dt.astype(f32))[:, None]
    lam_bar = jnp.exp(lam * dt)
    b = lax.complex(b_re.astype(f32), b_im.astype(f32))
    b_bar = ((lam_bar - 1) / lam)[..., None] * b
    return lam_bar, b_bar


def s5_scan(u, lam_bar, b_bar, h0, reverse):
    bu = jnp.einsum('blgc,gpc->blgp', u.astype(jnp.complex64), b_bar)
    if h0 is not None:
        bu = bu.at[:, -1 if reverse else 0].add(lam_bar * h0)
    a = jnp.broadcast_to(lam_bar, (1,) + bu.shape[1:])

    def combine(e1, e2):
        a1, b1 = e1
        a2, b2 = e2
        return a1 * a2, a2 * b1 + b2

    _, s = lax.associative_scan(combine, (a, bu), reverse=reverse, axis=1)
    return s


def s5_bidir(u_lat, u_ctx, a_re, a_im, log_dt, b_re, b_im, c_re, c_im, s5_d, glu_w, glu_b, ctx_out):
    B, L, _ = u_lat.shape
    C = u_ctx.shape[1]
    ul = u_lat.astype(jnp.float32).reshape(B, L, S5_GROUPS, S5_GROUP)
    uc = u_ctx.astype(jnp.float32).reshape(B, C, S5_GROUPS, S5_GROUP)
    dd = s5_d.astype(jnp.float32).reshape(S5_GROUPS, S5_GROUP)
    y_lat = ul * dd
    y_ctx = uc * dd
    for di in range(2):
        rev = di == 1
        lam_bar, b_bar = s5_discretize(a_re[di], a_im[di], log_dt[di], b_re[di], b_im[di])
        cm = lax.complex(c_re[di].astype(jnp.float32), c_im[di].astype(jnp.float32))
        sc = s5_scan(uc, lam_bar, b_bar, None, rev)
        h0 = sc[:, 0] if rev else sc[:, -1]
        sl = s5_scan(ul, lam_bar, b_bar, h0, rev)
        y_lat = y_lat + jnp.real(jnp.einsum('blgp,gcp->blgc', sl, cm))
        if ctx_out:
            y_ctx = y_ctx + jnp.real(jnp.einsum('blgp,gcp->blgc', sc, cm))

    def glu(y, n):
        g = jax.nn.gelu(y.reshape(B, n, S5_WIDTH))
        return (g * jax.nn.sigmoid(g @ glu_w.astype(jnp.float32) + glu_b.astype(jnp.float32))).astype(u_lat.dtype)

    return glu(y_lat, L), (glu(y_ctx, C) if ctx_out else None)


def hyena_s5_mixer(h, hc, w_in, w_out, conv_w, conv_b, f_w1, f_b1, f_freq, f_w2, f_b2, f_w3, hy_bias,
                   a_re, a_im, log_dt, b_re, b_im, c_re, c_im, s5_d, glu_w, glu_b, ctx_out):
    hw = (HY_ORDER + 1) * HY_WIDTH
    hy_args = (conv_w, conv_b, f_w1, f_b1, f_freq, f_w2, f_b2, f_w3, hy_bias)
    proj = h @ w_in
    y_h = hyena(proj[..., :hw], *hy_args)
    if ctx_out:
        projc = hc @ w_in
        uc = projc[..., hw:]
    else:
        uc = hc @ w_in[:, hw:]
    y_s, y_sc = s5_bidir(proj[..., hw:], uc, a_re, a_im, log_dt, b_re, b_im, c_re, c_im, s5_d, glu_w, glu_b, ctx_out)
    y = jnp.concatenate([y_h, y_s], axis=-1) @ w_out
    if not ctx_out:
        return y, None
    yc = jnp.concatenate([hyena(projc[..., :hw], *hy_args), y_sc], axis=-1) @ w_out
    return y, yc


def hier_moe(h, w_rg, b_rg, w_re, b_re, w1, w3, w2):
    T, D = h.shape
    pg = jax.nn.softmax((h @ w_rg + b_rg).astype(jnp.float32), axis=-1)
    pg_top, g_idx = lax.top_k(pg, 1)
    el = (h @ w_re + b_re).astype(jnp.float32).reshape(T, MOE_GROUPS, MOE_PER_GROUP)
    el = el[jnp.arange(T), g_idx[:, 0]]
    pe_top, e_idx = lax.top_k(jax.nn.softmax(el, axis=-1), MOE_TOPK)
    gate = pg_top * pe_top / jnp.sum(pe_top, axis=-1, keepdims=True)
    expert = g_idx * MOE_PER_GROUP + e_idx
    S = T * MOE_TOPK
    M = MOE_BLOCK
    NB = -(-S // M) + MOE_EXPERTS
    flat_e = expert.reshape(S)
    order = jnp.argsort(flat_e)
    se = flat_e[order]
    stok = (order // MOE_TOPK).astype(jnp.int32)
    sw = gate.reshape(S)[order]
    counts = jnp.bincount(flat_e, length=MOE_EXPERTS)
    padded = ((counts + M - 1) // M) * M
    pad_end = jnp.cumsum(padded)
    dest = (pad_end - padded)[se] + jnp.arange(S) - (jnp.cumsum(counts) - counts)[se]
    buf_tok = jnp.full((NB * M,), T, dtype=jnp.int32).at[dest].set(stok)
    buf_w = jnp.zeros((NB * M,), jnp.float32).at[dest].set(sw)
    block_e = jnp.minimum(jnp.searchsorted(pad_end, jnp.arange(NB) * M, side='right'), MOE_EXPERTS - 1)
    h_pad = jnp.concatenate([h, jnp.zeros((1, D), h.dtype)], axis=0)

    def run(args):
        e, tok, w = args
        xb = h_pad[tok]
        y = (jax.nn.silu(xb @ w1[e]) * (xb @ w3[e])) @ w2[e]
        return y * w[:, None].astype(y.dtype)

    ys = lax.map(run, (block_e, buf_tok.reshape(NB, M), buf_w.reshape(NB, M)))
    out = jnp.zeros((T + 1, D), ys.dtype).at[buf_tok].add(ys.reshape(NB * M, D))
    return out[:T]


def setup_inputs(seed: int = 0) -> dict:
    key = jax.random.key(seed)
    ks = iter(jax.random.split(key, 64))
    D = D_MODEL

    def nrm(shape, std):
        return jax.random.normal(next(ks), shape, jnp.float32) * std

    G, P = S5_GROUPS, S5_STATE
    return {
        "x": nrm((BATCH, SEQ, D), 1.0),
        "c": nrm((BATCH, D), 1.0),
        "ctx": nrm((BATCH, CTX_LEN, D), 1.0),
        "c_ctx": nrm((D,), 1.0),
        "ada_w": nrm((DEPTH, D, 6 * D), 0.5 * D ** -0.5),
        "ada_b": nrm((DEPTH, 6 * D), 0.02),
        "norm_mix_g": 1.0 + nrm((DEPTH, D), 0.02),
        "norm_ffn_g": 1.0 + nrm((DEPTH, D), 0.02),
        "norm_out_g": 1.0 + nrm((D,), 0.02),
        "ab_w_in": nrm((N_EVEN, D, AB_IN), D ** -0.5),
        "ab_w_out": nrm((N_EVEN, DIFF_WIDTH + POOL_WIDTH, D), (DIFF_WIDTH + POOL_WIDTH) ** -0.5),
        "diff_lam": nrm((N_EVEN, 4, DIFF_QK), 0.1),
        "diff_subln_g": 1.0 + nrm((N_EVEN, DIFF_V), 0.02),
        "pool_w": nrm((N_EVEN, len(POOL_WINDOWS), POOL_GROUP, POOL_GROUP), POOL_GROUP ** -0.5),
        "pool_scale": 1.0 + nrm((N_EVEN, POOL_WIDTH), 0.02),
        "cd_w_in": nrm((N_ODD, D, CD_IN), D ** -0.5),
        "cd_w_out": nrm((N_ODD, HY_WIDTH + S5_WIDTH, D), (HY_WIDTH + S5_WIDTH) ** -0.5),
        "hy_conv_w": nrm((N_ODD, 3, (HY_ORDER + 1) * HY_WIDTH), 3 ** -0.5),
        "hy_conv_b": nrm((N_ODD, (HY_ORDER + 1) * HY_WIDTH), 0.02),
        "hy_f_w1": nrm((N_ODD, HY_EMB, HY_HIDDEN), 1.0),
        "hy_f_b1": nrm((N_ODD, HY_HIDDEN), 0.1),
        "hy_f_freq": 1.0 + nrm((N_ODD, 2, HY_HIDDEN), 0.1),
        "hy_f_w2": nrm((N_ODD, HY_HIDDEN, HY_HIDDEN), HY_HIDDEN ** -0.5),
        "hy_f_b2": nrm((N_ODD, HY_HIDDEN), 0.1),
        "hy_f_w3": nrm((N_ODD, HY_HIDDEN, HY_ORDER * 2 * HY_WIDTH), HY_HIDDEN ** -0.5),
        "hy_bias": nrm((N_ODD, HY_ORDER, HY_WIDTH), 1.0),
        "s5_a_re": -0.5 + nrm((N_ODD, 2, G, P), 0.01),
        "s5_a_im": jnp.pi * jnp.arange(P, dtype=jnp.float32) + nrm((N_ODD, 2, G, P), 0.01),
        "s5_log_dt": jax.random.uniform(next(ks), (N_ODD, 2, G), jnp.float32, math.log(1e-3), math.log(1e-1)),
        "s5_b_re": nrm((N_ODD, 2, G, P, S5_GROUP), (2 * S5_GROUP) ** -0.5),
        "s5_b_im": nrm((N_ODD, 2, G, P, S5_GROUP), (2 * S5_GROUP) ** -0.5),
        "s5_c_re": nrm((N_ODD, 2, G, S5_GROUP, P), P ** -0.5),
        "s5_c_im": nrm((N_ODD, 2, G, S5_GROUP, P), P ** -0.5),
        "s5_d": nrm((N_ODD, S5_WIDTH), 1.0),
        "glu_w": nrm((N_ODD, S5_WIDTH, S5_WIDTH), S5_WIDTH ** -0.5),
        "glu_b": nrm((N_ODD, S5_WIDTH), 0.02),
        "moe_w_rg": nrm((DEPTH, D, MOE_GROUPS), D ** -0.5),
        "moe_b_rg": nrm((DEPTH, MOE_GROUPS), 0.01),
        "moe_w_re": nrm((DEPTH, D, MOE_EXPERTS), D ** -0.5),
        "moe_b_re": nrm((DEPTH, MOE_EXPERTS), 0.01),
        "moe_w1": nrm((DEPTH, MOE_EXPERTS, D, D_EXPERT), D ** -0.5),
        "moe_w3": nrm((DEPTH, MOE_EXPERTS, D, D_EXPERT), D ** -0.5),
        "moe_w2": nrm((DEPTH, MOE_EXPERTS, D_EXPERT, D), D_EXPERT ** -0.5),
    }


def reference(x, c, ctx, c_ctx, ada_w, ada_b, norm_mix_g, norm_ffn_g, norm_out_g,
              ab_w_in, ab_w_out, diff_lam, diff_subln_g, pool_w, pool_scale,
              cd_w_in, cd_w_out, hy_conv_w, hy_conv_b, hy_f_w1, hy_f_b1, hy_f_freq, hy_f_w2, hy_f_b2, hy_f_w3, hy_bias,
              s5_a_re, s5_a_im, s5_log_dt, s5_b_re, s5_b_im, s5_c_re, s5_c_im, s5_d, glu_w, glu_b,
              moe_w_rg, moe_b_rg, moe_w_re, moe_b_re, moe_w1, moe_w3, moe_w2):
    D = D_MODEL
    B, L, _ = x.shape
    xc = ctx
    sc = jax.nn.silu(c)
    scc = jax.nn.silu(c_ctx)
    for l in range(DEPTH):
        last = l == DEPTH - 1
        mod = sc @ ada_w[l] + ada_b[l]
        sh1, sc1, g1, sh2, sc2, g2 = jnp.split(mod[:, None, :], 6, axis=-1)
        ncols = 2 * D if last else 6 * D
        modc = jnp.split(scc @ ada_w[l][:, :ncols] + ada_b[l][:ncols], ncols // D)
        h = modulate(rmsnorm(x, norm_mix_g[l]), sh1, sc1)
        hc = modulate(rmsnorm(xc, norm_mix_g[l]), modc[0], modc[1])
        i = l // 2
        if l % 2 == 0:
            y, yc = diff_pool_mixer(h, hc, ab_w_in[i], ab_w_out[i], diff_lam[i], diff_subln_g[i],
                                    pool_w[i], pool_scale[i], 0.8 - 0.6 * math.exp(-0.3 * l), not last)
        else:
            y, yc = hyena_s5_mixer(h, hc, cd_w_in[i], cd_w_out[i], hy_conv_w[i], hy_conv_b[i], hy_f_w1[i],
                                   hy_f_b1[i], hy_f_freq[i], hy_f_w2[i], hy_f_b2[i], hy_f_w3[i], hy_bias[i],
                                   s5_a_re[i], s5_a_im[i], s5_log_dt[i], s5_b_re[i], s5_b_im[i], s5_c_re[i],
                                   s5_c_im[i], s5_d[i], glu_w[i], glu_b[i], not last)
        x = x + g1 * y
        h2 = modulate(rmsnorm(x, norm_ffn_g[l]), sh2, sc2)
        moe_args = (moe_w_rg[l], moe_b_rg[l], moe_w_re[l], moe_b_re[l], moe_w1[l], moe_w3[l], moe_w2[l])
        if last:
            x = x + g2 * hier_moe(h2.reshape(B * L, D), *moe_args).reshape(B, L, D)
        else:
            xc = xc + modc[2] * yc
            h2c = modulate(rmsnorm(xc, norm_ffn_g[l]), modc[3], modc[4])
            out = hier_moe(jnp.concatenate([h2.reshape(B * L, D), h2c.reshape(-1, D)], axis=0), *moe_args)
            x = x + g2 * out[:B * L].reshape(B, L, D)
            xc = xc + modc[5] * out[B * L:].reshape(xc.shape)
    return rmsnorm(x, norm_out_g)
```

```python
import functools
import math

import jax
import jax.numpy as jnp
from jax import lax
from jax.experimental import pallas as pl
from jax.experimental.pallas import tpu as pltpu

f32, bf16, i32 = jnp.float32, jnp.bfloat16, jnp.int32

EPS = 1e-6
GRID_W = 64
DIFF_QK = 64
DIFF_V = 128
ROPE_BASE = 10000.0
POOL_WINDOWS = (2, 4, 8, 16)
POOL_GROUP = 256
HY_WIDTH = 1024
HY_ORDER = 2
HY_EMB = 33
HY_FAST_DECAY = 0.3
HY_SLOW_DECAY = 1.5
HY_TARGET = 1e-2
S5_GROUP = 16
S5_GROUPS = 64
S5_STATE = 64
MOE_GROUPS = 8
MOE_PER_GROUP = 8
MOE_EXPERTS = 64
MOE_TOPK = 2
MOE_BLOCK = 256

VMEM_LIMIT = 56 * 1024 * 1024
TM = 512
LOG2E = 1.4426950408889634


def _cparams(sem):
    return pltpu.CompilerParams(dimension_semantics=sem, vmem_limit_bytes=VMEM_LIMIT)


def _norm_mod(x, g, shift, scale):
    ms = jnp.mean(x * x, axis=-1, keepdims=True)
    return (x * lax.rsqrt(ms + EPS) * g) * (1.0 + scale) + shift


def _tile_segment(i, lat_tiles_per_batch, batch):
    seg = jnp.zeros((), i32)
    for b in range(1, batch + 1):
        seg = seg + (i >= b * lat_tiles_per_batch).astype(i32)
    return seg


def _ada_kernel(s_ref, w_ref, b_ref, o_ref):
    s = s_ref[...]
    a = (s * jax.nn.sigmoid(s)).astype(bf16)
    o_ref[...] = jnp.dot(a, w_ref[...].astype(bf16), preferred_element_type=f32) + b_ref[...]


def _ada_mods(s, ada_w, ada_b):
    depth, d, n = ada_w.shape
    tn = 1024
    return pl.pallas_call(
        _ada_kernel,
        grid=(depth, n // tn),
        in_specs=[pl.BlockSpec((8, d), lambda l, j: (0, 0)),
                  pl.BlockSpec((None, d, tn), lambda l, j: (l, 0, j)),
                  pl.BlockSpec((None, 1, tn), lambda l, j: (l, 0, j))],
        out_specs=pl.BlockSpec((None, 8, tn), lambda l, j: (l, 0, j)),
        out_shape=jax.ShapeDtypeStruct((depth, 8, n), f32),
        compiler_params=_cparams(("arbitrary", "arbitrary")),
        name="ada_mods",
    )(s, ada_w, ada_b.reshape(depth, 1, n))


def _proj_rope_kernel(x_ref, g_ref, sh_ref, sc_ref, w_ref, cos_ref, sin_ref,
                      q_ref, k_ref, v_ref, p_ref, h_scr, *, tiles_per_batch, batch, qscale):
    seg = _tile_segment(pl.program_id(0), tiles_per_batch, batch)
    h = _norm_mod(x_ref[...], g_ref[...], sh_ref[pl.ds(seg, 1), :], sc_ref[pl.ds(seg, 1), :])
    h_scr[...] = h.astype(bf16)
    cos = cos_ref[...]
    sin = sin_ref[...]
    tm = cos.shape[0]
    lane = lax.broadcasted_iota(i32, (tm, 128), 1)
    first = (lane % 32) < 16

    def rope(r):
        sw = jnp.where(first, pltpu.roll(r, 112, 1), pltpu.roll(r, 16, 1))
        return r * cos + sw * sin

    heads = q_ref.shape[0]
    nc = 512
    for c in range(w_ref.shape[1] // nc):
        res = jnp.dot(h_scr[...], w_ref[:, c * nc:(c + 1) * nc], preferred_element_type=f32)
        for s in range(nc // 128):
            r = res[:, s * 128:(s + 1) * 128]
            col = (c * nc) // 128 + s
            if col < heads:
                q_ref[col] = (rope(r) * qscale).astype(bf16)
            elif col < 2 * heads:
                k_ref[col - heads] = rope(r).astype(bf16)
            elif col < 3 * heads:
                v_ref[col - 2 * heads] = r.astype(bf16)
            else:
                c0 = (col - 3 * heads) * 128
                p_ref[:, c0:c0 + 128] = r


def _proj_rope(x, g, mods, w, cos, sin, *, batch, seq):
    r, d = x.shape
    heads = 8
    lat_tiles = seq // TM
    n_tiles = r // TM

    def tab_map(i):
        return (jnp.where(i < batch * lat_tiles, i % lat_tiles, lat_tiles), 0)

    hm = jax.ShapeDtypeStruct((heads, r, 128), bf16)
    hm_spec = pl.BlockSpec((heads, TM, 128), lambda i: (0, i, 0))
    return pl.pallas_call(
        functools.partial(_proj_rope_kernel, tiles_per_batch=lat_tiles, batch=batch,
                          qscale=LOG2E * DIFF_QK ** -0.5),
        grid=(n_tiles,),
        in_specs=[pl.BlockSpec((TM, d), lambda i: (i, 0)),
                  pl.BlockSpec((1, d), lambda i: (0, 0)),
                  pl.BlockSpec((8, d), lambda i: (0, 0)),
                  pl.BlockSpec((8, d), lambda i: (0, 1)),
                  pl.BlockSpec(w.shape, lambda i: (0, 0), pipeline_mode=pl.Buffered(1)),
                  pl.BlockSpec((TM, 128), tab_map),
                  pl.BlockSpec((TM, 128), tab_map)],
        out_specs=[hm_spec, hm_spec, hm_spec, pl.BlockSpec((TM, 1024), lambda i: (i, 0))],
        out_shape=[hm, hm, hm, jax.ShapeDtypeStruct((r, 1024), f32)],
        scratch_shapes=[pltpu.VMEM((TM, d), bf16)],
        compiler_params=_cparams(("arbitrary",)),
        name="proj_rope",
    )(x, g, mods, mods, w, cos, sin)


def _proj_kernel(x_ref, g_ref, sh_ref, sc_ref, w_ref, o_ref, h_scr, *, tiles_per_batch, batch):
    seg = _tile_segment(pl.program_id(0), tiles_per_batch, batch)
    h = _norm_mod(x_ref[...], g_ref[...], sh_ref[pl.ds(seg, 1), :], sc_ref[pl.ds(seg, 1), :])
    h_scr[...] = h.astype(bf16)
    nc = 512
    for c in range(w_ref.shape[1] // nc):
        o_ref[:, c * nc:(c + 1) * nc] = jnp.dot(h_scr[...], w_ref[:, c * nc:(c + 1) * nc],
                                                preferred_element_type=f32)


def _proj(x, g, mods, w, *, batch, seq):
    r, d = x.shape
    n = w.shape[1]
    return pl.pallas_call(
        functools.partial(_proj_kernel, tiles_per_batch=seq // TM, batch=batch),
        grid=(r // TM,),
        in_specs=[pl.BlockSpec((TM, d), lambda i: (i, 0)),
                  pl.BlockSpec((1, d), lambda i: (0, 0)),
                  pl.BlockSpec((8, d), lambda i: (0, 0)),
                  pl.BlockSpec((8, d), lambda i: (0, 1)),
                  pl.BlockSpec(w.shape, lambda i: (0, 0), pipeline_mode=pl.Buffered(1))],
        out_specs=pl.BlockSpec((TM, n), lambda i: (i, 0)),
        out_shape=jax.ShapeDtypeStruct((r, n), f32),
        scratch_shapes=[pltpu.VMEM((TM, d), bf16)],
        compiler_params=_cparams(("arbitrary",)),
        name="proj",
    )(x, g, mods, mods, w)


def _attn_kernel(lv_ref, g_ref, q_ref, kc_ref, vc_ref, *rest, n_lat, tk, lam_init):
    if n_lat:
        kl_ref, vl_ref, o_ref = rest
    else:
        (o_ref,) = rest
    lv = lv_ref[...]
    lam = (jnp.exp(jnp.sum(lv[0:1] * lv[1:2], axis=-1, keepdims=True))
           - jnp.exp(jnp.sum(lv[2:3] * lv[3:4], axis=-1, keepdims=True)) + lam_init)
    q = q_ref[...]
    tq = q.shape[0]
    lane = lax.broadcasted_iota(i32, q.shape, 1)
    zero = jnp.zeros_like(q)
    qs = (jnp.where(lane < DIFF_QK, q, zero), jnp.where(lane >= DIFF_QK, q, zero))
    nt = (((1,), (1,)), ((), ()))

    def update(carry, k, v):
        new = []
        for m in range(2):
            mx, l, acc = carry[3 * m:3 * m + 3]
            s = lax.dot_general(qs[m], k, nt, preferred_element_type=f32)
            mn = jnp.maximum(mx, jnp.max(s, axis=-1, keepdims=True))
            alpha = jnp.exp2(mx - mn)
            p = jnp.exp2(s - mn)
            l = alpha * l + jnp.sum(p, axis=-1, keepdims=True)
            acc = alpha * acc + jnp.dot(p.astype(bf16), v, preferred_element_type=f32)
            new += [mn, l, acc]
        return tuple(new)

    init = (jnp.full((tq, 1), -jnp.inf, f32), jnp.zeros((tq, 1), f32), jnp.zeros((tq, DIFF_V), f32)) * 2
    carry = update(init, kc_ref[...], vc_ref[...])
    if n_lat:
        def body(j, carry):
            off = pl.multiple_of(j * tk, tk)
            return update(carry, kl_ref[pl.ds(off, tk), :], vl_ref[pl.ds(off, tk), :])
        carry = lax.fori_loop(0, n_lat, body, carry)
    _, l0, a0, _, l1, a1 = carry
    o = a0 / l0 - lam * (a1 / l1)
    ms = jnp.mean(o * o, axis=-1, keepdims=True)
    o_ref[...] = ((o * lax.rsqrt(ms + EPS) * g_ref[...]) * (1.0 - lam_init)).astype(o_ref.dtype)


def _diff_attention(q, k, v, lam_vecs, subln_g, lam_init, *, batch, seq, ctx_len):
    heads, r, _ = q.shape
    tq, tk = 512, 512
    ctx_blk0 = (batch * seq) // ctx_len
    small = [pl.BlockSpec((4, DIFF_QK), lambda *_: (0, 0)), pl.BlockSpec((1, DIFF_V), lambda *_: (0, 0))]
    kc_spec = pl.BlockSpec((None, ctx_len, 128), lambda b, h, *_: (h, ctx_blk0 + b, 0))
    g2 = subln_g.reshape(1, DIFF_V)
    lat = pl.pallas_call(
        functools.partial(_attn_kernel, n_lat=seq // tk, tk=tk, lam_init=lam_init),
        grid=(batch, heads, seq // tq),
        in_specs=small + [pl.BlockSpec((None, tq, 128), lambda b, h, i: (h, b * (seq // tq) + i, 0)),
                          kc_spec, kc_spec,
                          pl.BlockSpec((None, seq, 128), lambda b, h, i: (h, b, 0)),
                          pl.BlockSpec((None, seq, 128), lambda b, h, i: (h, b, 0))],
        out_specs=pl.BlockSpec((tq, 128), lambda b, h, i: (b * (seq // tq) + i, h)),
        out_shape=jax.ShapeDtypeStruct((r, heads * DIFF_V), bf16),
        compiler_params=_cparams(("arbitrary",) * 3),
        name="diff_attn_lat",
    )(lam_vecs, g2, q, k, v, k, v)
    ctx = pl.pallas_call(
        functools.partial(_attn_kernel, n_lat=0, tk=tk, lam_init=lam_init),
        grid=(batch, heads),
        in_specs=small + [kc_spec, kc_spec, kc_spec],
        out_specs=pl.BlockSpec((ctx_len, 128), lambda b, h: (b, h)),
        out_shape=jax.ShapeDtypeStruct((batch * ctx_len, heads * DIFF_V), bf16),
        compiler_params=_cparams(("arbitrary",) * 2),
        name="diff_attn_ctx",
    )(lam_vecs, g2, q, k, v)
    return lax.dynamic_update_slice(lat, ctx, (batch * seq, 0))


def _pool_kernel(p_ref, prev_ref, next_ref, w_ref, scale_ref, o_ref, ext_scr, *,
                 tiles_per_batch, batch, seq, ctx_len):
    i = pl.program_id(0)
    tp = p_ref.shape[0]
    is_ctx = i >= batch * tiles_per_batch
    j = jnp.where(is_ctx, 0, i % tiles_per_batch)
    seg_len = jnp.where(is_ctx, ctx_len, seq)
    last = jnp.where(is_ctx, 0, tiles_per_batch - 1)
    ext_scr[0:8, :] = jnp.where(j > 0, prev_ref[...], 0.0)
    ext_scr[8:8 + tp, :] = p_ref[...]
    ext_scr[8 + tp:16 + tp, :] = jnp.where(j < last, next_ref[...], 0.0)
    pos = j * tp + lax.broadcasted_iota(i32, (tp, 1), 0)
    for gi, w in enumerate(POOL_WINDOWS):
        c0 = gi * POOL_GROUP
        half = w // 2
        tot = ext_scr[8 - half:8 - half + tp, c0:c0 + POOL_GROUP]
        for dlt in range(1 - half, half):
            tot = tot + ext_scr[8 + dlt:8 + dlt + tp, c0:c0 + POOL_GROUP]
        cnt = (jnp.minimum(pos - half + w, seg_len) - jnp.maximum(pos - half, 0)).astype(f32)
        dlt_mean = (tot / cnt - ext_scr[8:8 + tp, c0:c0 + POOL_GROUP]).astype(bf16)
        y = jnp.dot(dlt_mean, w_ref[gi].astype(bf16), preferred_element_type=f32)
        o_ref[:, c0:c0 + POOL_GROUP] = (y * scale_ref[:, c0:c0 + POOL_GROUP]).astype(o_ref.dtype)


def _pool(p, pool_w, pool_scale, *, batch, seq, ctx_len):
    r, width = p.shape
    tp = ctx_len
    tpb = seq // tp
    return pl.pallas_call(
        functools.partial(_pool_kernel, tiles_per_batch=tpb, batch=batch, seq=seq, ctx_len=ctx_len),
        grid=(r // tp,),
        in_specs=[pl.BlockSpec((tp, width), lambda i: (i, 0)),
                  pl.BlockSpec((8, width), lambda i: (jnp.maximum(i * (tp // 8) - 1, 0), 0)),
                  pl.BlockSpec((8, width), lambda i: (jnp.minimum((i + 1) * (tp // 8), r // 8 - 1), 0)),
                  pl.BlockSpec(pool_w.shape, lambda i: (0, 0, 0)),
                  pl.BlockSpec((1, width), lambda i: (0, 0))],
        out_specs=pl.BlockSpec((tp, width), lambda i: (i, 0)),
        out_shape=jax.ShapeDtypeStruct((r, width), bf16),
        scratch_shapes=[pltpu.VMEM((tp + 16, width), f32)],
        compiler_params=_cparams(("arbitrary",)),
        name="pool",
    )(p, p, p, pool_w, pool_scale.reshape(1, width))


def _outproj_kernel(a_ref, b_ref, x_ref, w_ref, g1_ref, sh_ref, sc_ref, gn_ref, wr_ref, br_ref,
                    xn_ref, h2_ref, lg_ref, *, tiles_per_batch, batch):
    seg = _tile_segment(pl.program_id(0), tiles_per_batch, batch)
    half = a_ref.shape[1]
    y = (jnp.dot(a_ref[...], w_ref[0:half, :], preferred_element_type=f32)
         + jnp.dot(b_ref[...], w_ref[half:, :], preferred_element_type=f32))
    xn = x_ref[...] + g1_ref[pl.ds(seg, 1), :] * y
    xn_ref[...] = xn
    h2 = _norm_mod(xn, gn_ref[...], sh_ref[pl.ds(seg, 1), :], sc_ref[pl.ds(seg, 1), :])
    h2_ref[...] = h2.astype(bf16)
    lg_ref[...] = jnp.dot(h2, wr_ref[...], preferred_element_type=f32,
                          precision=lax.Precision.HIGHEST) + br_ref[...]


def _outproj(a, b, x, w, mods, g_ffn, w_r, b_r, *, n_tiles, batch, seq):
    r, d = x.shape
    half = a.shape[1]
    row = lambda i: (i, 0)
    return pl.pallas_call(
        functools.partial(_outproj_kernel, tiles_per_batch=seq // TM, batch=batch),
        grid=(n_tiles,),
        in_specs=[pl.BlockSpec((TM, half), row), pl.BlockSpec((TM, half), row), pl.BlockSpec((TM, d), row),
                  pl.BlockSpec(w.shape, lambda i: (0, 0), pipeline_mode=pl.Buffered(1)),
                  pl.BlockSpec((8, d), lambda i: (0, 2)),
                  pl.BlockSpec((8, d), lambda i: (0, 3)),
                  pl.BlockSpec((8, d), lambda i: (0, 4)),
                  pl.BlockSpec((1, d), lambda i: (0, 0)),
                  pl.BlockSpec(w_r.shape, lambda i: (0, 0)),
                  pl.BlockSpec((1, 128), lambda i: (0, 0))],
        out_specs=[pl.BlockSpec((TM, d), row), pl.BlockSpec((TM, d), row), pl.BlockSpec((TM, 128), row)],
        out_shape=[jax.ShapeDtypeStruct((r, d), f32), jax.ShapeDtypeStruct((r, d), bf16),
                   jax.ShapeDtypeStruct((r, 128), f32)],
        compiler_params=_cparams(("arbitrary",)),
        name="outproj",
    )(a, b, x, w, mods, mods, mods, g_ffn, w_r, b_r)


def _moe_kernel(be_ref, nu_ref, xs_ref, w1_ref, w3_ref, w2_ref, ys_ref, w1_scr, w3_scr, w2_scr):
    i = pl.program_id(0)
    e = be_ref[i]
    prev = be_ref[jnp.maximum(i - 1, 0)]

    @pl.when((i == 0) | (e != prev))
    def _():
        w1_scr[...] = w1_ref[...].astype(bf16)
        w3_scr[...] = w3_ref[...].astype(bf16)
        w2_scr[...] = w2_ref[...].astype(bf16)

    @pl.when(i < nu_ref[0])
    def _():
        xb = xs_ref[...]
        a = jnp.dot(xb, w1_scr[...], preferred_element_type=f32)
        b = jnp.dot(xb, w3_scr[...], preferred_element_type=f32)
        hid = (a * jax.nn.sigmoid(a) * b).astype(bf16)
        ys_ref[...] = jnp.dot(hid, w2_scr[...], preferred_element_type=f32)

    @pl.when(i >= nu_ref[0])
    def _():
        ys_ref[...] = jnp.zeros_like(ys_ref)


def _moe_experts(xs, block_e, n_used, w1, w3, w2, nb):
    m = MOE_BLOCK
    _, d, de = w1.shape
    grid_spec = pltpu.PrefetchScalarGridSpec(
        num_scalar_prefetch=2,
        grid=(nb,),
        in_specs=[pl.BlockSpec((m, d), lambda i, be, nu: (jnp.minimum(i, nu[0] - 1), 0)),
                  pl.BlockSpec((None, d, de), lambda i, be, nu: (be[i], 0, 0)),
                  pl.BlockSpec((None, d, de), lambda i, be, nu: (be[i], 0, 0)),
                  pl.BlockSpec((None, de, d), lambda i, be, nu: (be[i], 0, 0))],
        out_specs=pl.BlockSpec((m, d), lambda i, be, nu: (jnp.where(i < nu[0], i, nb), 0)),
        scratch_shapes=[pltpu.VMEM((d, de), bf16), pltpu.VMEM((d, de), bf16), pltpu.VMEM((de, d), bf16)],
    )
    return pl.pallas_call(
        _moe_kernel,
        grid_spec=grid_spec,
        out_shape=jax.ShapeDtypeStruct(((nb + 1) * m, d), f32),
        compiler_params=_cparams(("arbitrary",)),
        name="moe_experts",
    )(block_e, n_used, xs, w1, w3, w2)


def _route(logits):
    t = logits.shape[0]
    pg = jax.nn.softmax(logits[:, :MOE_GROUPS], axis=-1)
    pg_top, g_idx = lax.top_k(pg, 1)
    el = logits[:, MOE_GROUPS:MOE_GROUPS + MOE_EXPERTS].reshape(t, MOE_GROUPS, MOE_PER_GROUP)
    el = jnp.take_along_axis(el, g_idx[:, :, None], axis=1)[:, 0]
    pe_top, e_idx = lax.top_k(jax.nn.softmax(el, axis=-1), MOE_TOPK)
    gate = pg_top * pe_top / jnp.sum(pe_top, axis=-1, keepdims=True)
    return (g_idx * MOE_PER_GROUP + e_idx).astype(i32), gate


def _combine_kernel(x_ref, y0_ref, y1_ref, gt_ref, g2_ref, gn_ref, o_ref, *, tiles_per_batch, batch, final_norm):
    seg = _tile_segment(pl.program_id(0), tiles_per_batch, batch)
    gt = gt_ref[...]
    x = x_ref[...] + g2_ref[pl.ds(seg, 1), :] * (gt[:, 0:1] * y0_ref[...] + gt[:, 1:2] * y1_ref[...])
    if final_norm:
        ms = jnp.mean(x * x, axis=-1, keepdims=True)
        x = x * lax.rsqrt(ms + EPS) * gn_ref[...]
    o_ref[...] = x


def _combine(x, y0, y1, gate, mods, g_out, *, n_tiles, batch, seq, final_norm):
    r, d = x.shape
    row = lambda i: (i, 0)
    return pl.pallas_call(
        functools.partial(_combine_kernel, tiles_per_batch=seq // TM, batch=batch, final_norm=final_norm),
        grid=(n_tiles,),
        in_specs=[pl.BlockSpec((TM, d), row), pl.BlockSpec((TM, d), row), pl.BlockSpec((TM, d), row),
                  pl.BlockSpec((TM, MOE_TOPK), row),
                  pl.BlockSpec((8, d), lambda i: (0, 5)),
                  pl.BlockSpec((1, d), lambda i: (0, 0))],
        out_specs=pl.BlockSpec((TM, d), row),
        out_shape=jax.ShapeDtypeStruct((n_tiles * TM, d), f32),
        compiler_params=_cparams(("arbitrary",)),
        name="combine",
    )(x, y0, y1, gate, mods, g_out)


def _moe(xn, h2, logits, mods, w1, w3, w2, g_out, *, n_tiles, batch, seq, final_norm):
    t = n_tiles * TM
    m = MOE_BLOCK
    expert, gate = _route(logits[:t])
    s = t * MOE_TOPK
    nb = s // m + MOE_EXPERTS
    flat_e = expert.reshape(s)
    onehot = (flat_e[:, None] == jnp.arange(MOE_EXPERTS, dtype=i32)[None, :]).astype(i32)
    csum = jnp.cumsum(onehot, axis=0)
    rank = jnp.sum((csum - onehot) * onehot, axis=1)
    counts = csum[-1]
    padded = ((counts + m - 1) // m) * m
    pad_end = jnp.cumsum(padded)
    dest = (pad_end - padded)[flat_e] + rank
    n_used = (pad_end[-1] // m).astype(i32)
    blk = jnp.arange(nb, dtype=i32)
    block_e = jnp.minimum(jnp.searchsorted(pad_end, jnp.minimum(blk, n_used - 1) * m, side='right'),
                          MOE_EXPERTS - 1).astype(i32)
    tok = (jnp.arange(s, dtype=i32) // MOE_TOPK)
    buf_tok = jnp.zeros((nb * m,), i32).at[dest].set(tok)
    xs = jnp.take(h2, buf_tok, axis=0)
    ys = _moe_experts(xs, block_e, n_used.reshape(1), w1, w3, w2, nb)
    d2 = dest.reshape(t, MOE_TOPK)
    y0 = jnp.take(ys, d2[:, 0], axis=0)
    y1 = jnp.take(ys, d2[:, 1], axis=0)
    return _combine(xn, y0, y1, gate, mods, g_out, n_tiles=n_tiles, batch=batch, seq=seq, final_norm=final_norm)


def _short_conv(u, w, b):
    up = jnp.pad(u, ((0, 0), (1, 1), (0, 0)))
    return up[:, :-2] * w[0] + up[:, 1:-1] * w[1] + up[:, 2:] * w[2] + b


def _hyena_filters(n, w1, b1, freq, w2, b2, w3):
    t = jnp.linspace(0.0, 1.0, n, dtype=f32)[:, None]
    bands = (HY_EMB - 1) // 2
    w = 2 * math.pi * jnp.arange(n, dtype=f32)[:, None] / n
    f = jnp.linspace(1e-4, bands - 1, bands, dtype=f32)[None, :]
    z = jnp.concatenate([t, jnp.cos(f * w), -jnp.sin(f * w)], axis=-1)
    hid = jnp.sin(freq[0] * (z @ w1 + b1))
    hid = jnp.sin(freq[1] * (hid @ w2 + b2))
    k = (hid @ w3).reshape(n, HY_ORDER, 2, HY_WIDTH)
    max_decay = math.log(HY_TARGET) / HY_FAST_DECAY
    min_decay = math.log(HY_TARGET) / HY_SLOW_DECAY
    deltas = jnp.linspace(min_decay, max_decay, HY_WIDTH, dtype=f32)
    k = k * jnp.exp(-t * jnp.abs(deltas))[:, None, None, :]
    norm = jnp.sum(jnp.abs(k[:, :, 0]), axis=0) + jnp.sum(jnp.abs(k[1:, :, 1]), axis=0)
    return k / norm[None, :, None, :]


def _long_conv(u, k_fwd, k_bwd, bias):
    n = u.shape[1]
    k_full = jnp.concatenate([k_fwd, jnp.zeros_like(k_fwd[:1]), k_bwd[:0:-1]], axis=0)
    spec = jnp.fft.rfft(u, n=2 * n, axis=1) * jnp.fft.rfft(k_full, axis=0)[None]
    y = jnp.fft.irfft(spec, n=2 * n, axis=1)[:, :n]
    return y + u * bias


def _hyena(u, conv_w, conv_b, f_w1, f_b1, f_freq, f_w2, f_b2, f_w3, hy_bias):
    n = u.shape[1]
    g = _short_conv(u, conv_w, conv_b)
    v, x1, x2 = jnp.split(g, HY_ORDER + 1, axis=-1)
    k = _hyena_filters(n, f_w1, f_b1, f_freq, f_w2, f_b2, f_w3)
    z = x1 * _long_conv(v, k[:, 0, 0], k[:, 0, 1], hy_bias[0])
    return x2 * _long_conv(z, k[:, 1, 0], k[:, 1, 1], hy_bias[1])


def _s5_scan(u, lam_bar, b_bar, h0, reverse):
    bu = jnp.einsum('blgc,gpc->blgp', u.astype(jnp.complex64), b_bar)
    if h0 is not None:
        bu = bu.at[:, -1 if reverse else 0].add(lam_bar * h0)
    a = jnp.broadcast_to(lam_bar, (1,) + bu.shape[1:])

    def comb(e1, e2):
        return e1[0] * e2[0], e2[0] * e1[1] + e2[1]

    return lax.associative_scan(comb, (a, bu), reverse=reverse, axis=1)[1]


def _s5(u_lat, u_ctx, a_re, a_im, log_dt, b_re, b_im, c_re, c_im, s5_d, glu_w, glu_b):
    bsz, n, _ = u_lat.shape
    cl = u_ctx.shape[1]
    ul = u_lat.reshape(bsz, n, S5_GROUPS, S5_GROUP)
    uc = u_ctx.reshape(bsz, cl, S5_GROUPS, S5_GROUP)
    y = ul * s5_d.reshape(S5_GROUPS, S5_GROUP)
    for di in range(2):
        rev = di == 1
        lam = lax.complex(a_re[di], a_im[di])
        dt = jnp.exp(log_dt[di])[:, None]
        lam_bar = jnp.exp(lam * dt)
        b_bar = ((lam_bar - 1) / lam)[..., None] * lax.complex(b_re[di], b_im[di])
        cm = lax.complex(c_re[di], c_im[di])
        sc = _s5_scan(uc, lam_bar, b_bar, None, rev)
        h0 = sc[:, 0] if rev else sc[:, -1]
        sl = _s5_scan(ul, lam_bar, b_bar, h0, rev)
        y = y + jnp.real(jnp.einsum('blgp,gcp->blgc', sl, cm))
    g = jax.nn.gelu(y.reshape(bsz, n, S5_GROUPS * S5_GROUP))
    return g * jax.nn.sigmoid(g @ glu_w + glu_b)


def _rope_tables(seq):
    rows = seq // GRID_W
    row = jnp.repeat(jnp.arange(rows, dtype=f32), GRID_W)
    col = jnp.tile(jnp.arange(GRID_W, dtype=f32), rows)
    n_freq = DIFF_QK // 4
    inv = ROPE_BASE ** (-jnp.arange(n_freq, dtype=f32) / n_freq)
    ar, ac = row[:, None] * inv, col[:, None] * inv
    cos = jnp.concatenate([jnp.cos(ar), jnp.cos(ar), jnp.cos(ac), jnp.cos(ac)], axis=-1)
    sin = jnp.concatenate([-jnp.sin(ar), jnp.sin(ar), -jnp.sin(ac), jnp.sin(ac)], axis=-1)
    cos = jnp.concatenate([jnp.tile(cos, (1, 2)), jnp.ones((TM, 128), f32)], axis=0)
    sin = jnp.concatenate([jnp.tile(sin, (1, 2)), jnp.zeros((TM, 128), f32)], axis=0)
    return cos, sin


def kernel(x, c, ctx, c_ctx, ada_w, ada_b, norm_mix_g, norm_ffn_g, norm_out_g, ab_w_in, ab_w_out, diff_lam, diff_subln_g, pool_w, pool_scale, cd_w_in, cd_w_out, hy_conv_w, hy_conv_b, hy_f_w1, hy_f_b1, hy_f_freq, hy_f_w2, hy_f_b2, hy_f_w3, hy_bias, s5_a_re, s5_a_im, s5_log_dt, s5_b_re, s5_b_im, s5_c_re, s5_c_im, s5_d, glu_w, glu_b, moe_w_rg, moe_b_rg, moe_w_re, moe_b_re, moe_w1, moe_w3, moe_w2):
    bsz, seq, d = x.shape
    cl = ctx.shape[1]
    depth = ada_w.shape[0]
    n_lat = bsz * seq
    assert seq % TM == 0 and (bsz * cl) % TM == 0 and bsz + 1 <= 8
    xs = jnp.concatenate([x.reshape(n_lat, d), ctx.reshape(bsz * cl, d)], axis=0)
    r = xs.shape[0]
    cond = jnp.zeros((8, d), f32).at[0:bsz].set(c).at[bsz].set(c_ctx)
    mods = _ada_mods(cond, ada_w, ada_b)
    geom = dict(batch=bsz, seq=seq)

    for l in range(depth):
        last = l == depth - 1
        i = l // 2
        n_tiles = (n_lat if last else r) // TM
        g_mix = norm_mix_g[l].reshape(1, d)
        if l % 2 == 0:
            lam_init = 0.8 - 0.6 * math.exp(-0.3 * l)
            cos, sin = _rope_tables(seq)
            q, k, v, p = _proj_rope(xs, g_mix, mods[l], ab_w_in[i].astype(bf16), cos, sin, **geom)
            mix_a = _diff_attention(q, k, v, diff_lam[i], diff_subln_g[i], lam_init, ctx_len=cl, **geom)
            mix_b = _pool(p, pool_w[i], pool_scale[i], ctx_len=cl, **geom)
            w_out = ab_w_out[i]
        else:
            hw = (HY_ORDER + 1) * HY_WIDTH
            proj = _proj(xs, g_mix, mods[l], cd_w_in[i].astype(bf16), **geom)
            lat = proj[:n_lat].reshape(bsz, seq, -1)
            u_ctx = proj[n_lat:, hw:].reshape(bsz, cl, -1)
            y_h = _hyena(lat[..., :hw], hy_conv_w[i], hy_conv_b[i], hy_f_w1[i], hy_f_b1[i], hy_f_freq[i],
                         hy_f_w2[i], hy_f_b2[i], hy_f_w3[i], hy_bias[i])
            y_s = _s5(lat[..., hw:], u_ctx, s5_a_re[i], s5_a_im[i], s5_log_dt[i], s5_b_re[i], s5_b_im[i],
                      s5_c_re[i], s5_c_im[i], s5_d[i], glu_w[i], glu_b[i])
            pad = ((0, r - n_lat), (0, 0))
            mix_a = jnp.pad(y_h.reshape(n_lat, -1).astype(bf16), pad)
            mix_b = jnp.pad(y_s.reshape(n_lat, -1).astype(bf16), pad)
            w_out = cd_w_out[i]
        w_r = jnp.zeros((d, 128), f32).at[:, :MOE_GROUPS].set(moe_w_rg[l]) \
            .at[:, MOE_GROUPS:MOE_GROUPS + MOE_EXPERTS].set(moe_w_re[l])
        b_r = jnp.zeros((1, 128), f32).at[0, :MOE_GROUPS].set(moe_b_rg[l]) \
            .at[0, MOE_GROUPS:MOE_GROUPS + MOE_EXPERTS].set(moe_b_re[l])
        xn, h2, logits = _outproj(mix_a, mix_b, xs, w_out.astype(bf16), mods[l], norm_ffn_g[l].reshape(1, d),
                                  w_r, b_r, n_tiles=n_tiles, **geom)
        xs = _moe(xn, h2, logits, mods[l], moe_w1[l], moe_w3[l], moe_w2[l], norm_out_g.reshape(1, d),
                  n_tiles=n_tiles, final_norm=last, **geom)
    return xs[:n_lat].reshape(bsz, seq, d)
```

```python
import functools
import math

import jax
import jax.numpy as jnp
from jax import lax
from jax.experimental import pallas as pl
from jax.experimental.pallas import tpu as pltpu

f32, bf16, i32 = jnp.float32, jnp.bfloat16, jnp.int32

EPS = 1e-6
GRID_W = 64
DIFF_QK = 64
DIFF_V = 128
ROPE_BASE = 10000.0
POOL_WINDOWS = (2, 4, 8, 16)
POOL_GROUP = 256
HY_WIDTH = 1024
HY_ORDER = 2
HY_EMB = 33
HY_FAST_DECAY = 0.3
HY_SLOW_DECAY = 1.5
HY_TARGET = 1e-2
S5_GROUP = 16
S5_GROUPS = 64
S5_STATE = 64
MOE_GROUPS = 8
MOE_PER_GROUP = 8
MOE_EXPERTS = 64
MOE_TOPK = 2
MOE_BLOCK = 256

VMEM_LIMIT = 56 * 1024 * 1024
TM = 512
LOG2E = 1.4426950408889634


def _cparams(sem):
    return pltpu.CompilerParams(dimension_semantics=sem, vmem_limit_bytes=VMEM_LIMIT)


def _norm_mod(x, g, shift, scale):
    ms = jnp.mean(x * x, axis=-1, keepdims=True)
    return (x * lax.rsqrt(ms + EPS) * g) * (1.0 + scale) + shift


def _tile_segment(i, lat_tiles_per_batch, batch):
    seg = jnp.zeros((), i32)
    for b in range(1, batch + 1):
        seg = seg + (i >= b * lat_tiles_per_batch).astype(i32)
    return seg


def _ada_kernel(s_ref, w_ref, b_ref, o_ref):
    s = s_ref[...]
    a = (s * jax.nn.sigmoid(s)).astype(bf16)
    o_ref[...] = jnp.dot(a, w_ref[...].astype(bf16), preferred_element_type=f32) + b_ref[...]


def _ada_mods(s, ada_w, ada_b):
    depth, d, n = ada_w.shape
    tn = 1024
    return pl.pallas_call(
        _ada_kernel,
        grid=(depth, n // tn),
        in_specs=[pl.BlockSpec((8, d), lambda l, j: (0, 0)),
                  pl.BlockSpec((None, d, tn), lambda l, j: (l, 0, j)),
                  pl.BlockSpec((None, 1, tn), lambda l, j: (l, 0, j))],
        out_specs=pl.BlockSpec((None, 8, tn), lambda l, j: (l, 0, j)),
        out_shape=jax.ShapeDtypeStruct((depth, 8, n), f32),
        compiler_params=_cparams(("arbitrary", "arbitrary")),
        name="ada_mods",
    )(s, ada_w, ada_b.reshape(depth, 1, n))


def _proj_rope_kernel(x_ref, g_ref, sh_ref, sc_ref, w_ref, cos_ref, sin_ref,
                      q_ref, k_ref, v_ref, p_ref, h_scr, *, tiles_per_batch, batch, qscale):
    seg = _tile_segment(pl.program_id(0), tiles_per_batch, batch)
    h = _norm_mod(x_ref[...], g_ref[...], sh_ref[pl.ds(seg, 1), :], sc_ref[pl.ds(seg, 1), :])
    h_scr[...] = h.astype(bf16)
    cos = cos_ref[...]
    sin = sin_ref[...]
    tm = cos.shape[0]
    lane = lax.broadcasted_iota(i32, (tm, 128), 1)
    first = (lane % 32) < 16

    def rope(r):
        sw = jnp.where(first, pltpu.roll(r, 112, 1), pltpu.roll(r, 16, 1))
        return r * cos + sw * sin

    heads = q_ref.shape[0]
    nc = 512
    for c in range(w_ref.shape[1] // nc):
        res = jnp.dot(h_scr[...], w_ref[:, c * nc:(c + 1) * nc], preferred_element_type=f32)
        for s in range(nc // 128):
            r = res[:, s * 128:(s + 1) * 128]
            col = (c * nc) // 128 + s
            if col < heads:
                q_ref[col] = (rope(r) * qscale).astype(bf16)
            elif col < 2 * heads:
                k_ref[col - heads] = rope(r).astype(bf16)
            elif col < 3 * heads:
                v_ref[col - 2 * heads] = r.astype(bf16)
            else:
                c0 = (col - 3 * heads) * 128
                p_ref[:, c0:c0 + 128] = r


def _proj_rope(x, g, mods, w, cos, sin, *, batch, seq):
    r, d = x.shape
    heads = 8
    lat_tiles = seq // TM
    n_tiles = r // TM

    def tab_map(i):
        return (jnp.where(i < batch * lat_tiles, i % lat_tiles, lat_tiles), 0)

    hm = jax.ShapeDtypeStruct((heads, r, 128), bf16)
    hm_spec = pl.BlockSpec((heads, TM, 128), lambda i: (0, i, 0))
    return pl.pallas_call(
        functools.partial(_proj_rope_kernel, tiles_per_batch=lat_tiles, batch=batch,
                          qscale=LOG2E * DIFF_QK ** -0.5),
        grid=(n_tiles,),
        in_specs=[pl.BlockSpec((TM, d), lambda i: (i, 0)),
                  pl.BlockSpec((1, d), lambda i: (0, 0)),
                  pl.BlockSpec((8, d), lambda i: (0, 0)),
                  pl.BlockSpec((8, d), lambda i: (0, 1)),
                  pl.BlockSpec(w.shape, lambda i: (0, 0), pipeline_mode=pl.Buffered(1)),
                  pl.BlockSpec((TM, 128), tab_map),
                  pl.BlockSpec((TM, 128), tab_map)],
        out_specs=[hm_spec, hm_spec, hm_spec, pl.BlockSpec((TM, 1024), lambda i: (i, 0))],
        out_shape=[hm, hm, hm, jax.ShapeDtypeStruct((r, 1024), f32)],
        scratch_shapes=[pltpu.VMEM((TM, d), bf16)],
        compiler_params=_cparams(("arbitrary",)),
        name="proj_rope",
    )(x, g, mods, mods, w, cos, sin)


def _proj_kernel(x_ref, g_ref, sh_ref, sc_ref, w_ref, hy_ref, u_ref, h_scr, *, tiles_per_batch, batch):
    seg = _tile_segment(pl.program_id(0), tiles_per_batch, batch)
    h = _norm_mod(x_ref[...], g_ref[...], sh_ref[pl.ds(seg, 1), :], sc_ref[pl.ds(seg, 1), :])
    h_scr[...] = h.astype(bf16)
    nc = 512
    n_hy = hy_ref.shape[1]
    for c in range(w_ref.shape[1] // nc):
        res = jnp.dot(h_scr[...], w_ref[:, c * nc:(c + 1) * nc], preferred_element_type=f32)
        if c * nc < n_hy:
            hy_ref[:, c * nc:(c + 1) * nc] = res
        else:
            u_ref[:, c * nc - n_hy:(c + 1) * nc - n_hy] = res


def _proj(x, g, mods, w, n_hy, *, batch, seq):
    r, d = x.shape
    n = w.shape[1]
    return pl.pallas_call(
        functools.partial(_proj_kernel, tiles_per_batch=seq // TM, batch=batch),
        grid=(r // TM,),
        in_specs=[pl.BlockSpec((TM, d), lambda i: (i, 0)),
                  pl.BlockSpec((1, d), lambda i: (0, 0)),
                  pl.BlockSpec((8, d), lambda i: (0, 0)),
                  pl.BlockSpec((8, d), lambda i: (0, 1)),
                  pl.BlockSpec(w.shape, lambda i: (0, 0), pipeline_mode=pl.Buffered(1))],
        out_specs=[pl.BlockSpec((TM, n_hy), lambda i: (i, 0)), pl.BlockSpec((TM, n - n_hy), lambda i: (i, 0))],
        out_shape=[jax.ShapeDtypeStruct((r, n_hy), f32), jax.ShapeDtypeStruct((r, n - n_hy), f32)],
        scratch_shapes=[pltpu.VMEM((TM, d), bf16)],
        compiler_params=_cparams(("arbitrary",)),
        name="proj",
    )(x, g, mods, mods, w)


def _attn_kernel(lv_ref, g_ref, q_ref, kc_ref, vc_ref, *rest, n_lat, tk, lam_init):
    if n_lat:
        kl_ref, vl_ref, o_ref = rest
    else:
        (o_ref,) = rest
    lv = lv_ref[...]
    lam = (jnp.exp(jnp.sum(lv[0:1] * lv[1:2], axis=-1, keepdims=True))
           - jnp.exp(jnp.sum(lv[2:3] * lv[3:4], axis=-1, keepdims=True)) + lam_init)
    q = q_ref[...]
    tq = q.shape[0]
    lane = lax.broadcasted_iota(i32, q.shape, 1)
    zero = jnp.zeros_like(q)
    qs = (jnp.where(lane < DIFF_QK, q, zero), jnp.where(lane >= DIFF_QK, q, zero))
    nt = (((1,), (1,)), ((), ()))

    def update(carry, k, v):
        new = []
        for m in range(2):
            mx, l, acc = carry[3 * m:3 * m + 3]
            s = lax.dot_general(qs[m], k, nt, preferred_element_type=f32)
            mn = jnp.maximum(mx, jnp.max(s, axis=-1, keepdims=True))
            alpha = jnp.exp2(mx - mn)
            p = jnp.exp2(s - mn)
            l = alpha * l + jnp.sum(p, axis=-1, keepdims=True)
            acc = alpha * acc + jnp.dot(p.astype(bf16), v, preferred_element_type=f32)
            new += [mn, l, acc]
        return tuple(new)

    init = (jnp.full((tq, 1), -jnp.inf, f32), jnp.zeros((tq, 1), f32), jnp.zeros((tq, DIFF_V), f32)) * 2
    carry = update(init, kc_ref[...], vc_ref[...])
    if n_lat:
        def body(j, carry):
            off = pl.multiple_of(j * tk, tk)
            return update(carry, kl_ref[pl.ds(off, tk), :], vl_ref[pl.ds(off, tk), :])
        carry = lax.fori_loop(0, n_lat, body, carry)
    _, l0, a0, _, l1, a1 = carry
    o = a0 / l0 - lam * (a1 / l1)
    ms = jnp.mean(o * o, axis=-1, keepdims=True)
    o_ref[...] = ((o * lax.rsqrt(ms + EPS) * g_ref[...]) * (1.0 - lam_init)).astype(o_ref.dtype)


def _diff_attention(q, k, v, lam_vecs, subln_g, lam_init, *, batch, seq, ctx_len):
    heads, r, _ = q.shape
    tq, tk = 512, 512
    ctx_blk0 = (batch * seq) // ctx_len
    small = [pl.BlockSpec((4, DIFF_QK), lambda *_: (0, 0)), pl.BlockSpec((1, DIFF_V), lambda *_: (0, 0))]
    kc_spec = pl.BlockSpec((None, ctx_len, 128), lambda b, h, *_: (h, ctx_blk0 + b, 0))
    g2 = subln_g.reshape(1, DIFF_V)
    lat = pl.pallas_call(
        functools.partial(_attn_kernel, n_lat=seq // tk, tk=tk, lam_init=lam_init),
        grid=(batch, heads, seq // tq),
        in_specs=small + [pl.BlockSpec((None, tq, 128), lambda b, h, i: (h, b * (seq // tq) + i, 0)),
                          kc_spec, kc_spec,
                          pl.BlockSpec((None, seq, 128), lambda b, h, i: (h, b, 0)),
                          pl.BlockSpec((None, seq, 128), lambda b, h, i: (h, b, 0))],
        out_specs=pl.BlockSpec((tq, 128), lambda b, h, i: (b * (seq // tq) + i, h)),
        out_shape=jax.ShapeDtypeStruct((r, heads * DIFF_V), bf16),
        compiler_params=_cparams(("arbitrary",) * 3),
        name="diff_attn_lat",
    )(lam_vecs, g2, q, k, v, k, v)
    ctx = pl.pallas_call(
        functools.partial(_attn_kernel, n_lat=0, tk=tk, lam_init=lam_init),
        grid=(batch, heads),
        in_specs=small + [kc_spec, kc_spec, kc_spec],
        out_specs=pl.BlockSpec((ctx_len, 128), lambda b, h: (b, h)),
        out_shape=jax.ShapeDtypeStruct((batch * ctx_len, heads * DIFF_V), bf16),
        compiler_params=_cparams(("arbitrary",) * 2),
        name="diff_attn_ctx",
    )(lam_vecs, g2, q, k, v)
    return lax.dynamic_update_slice(lat, ctx, (batch * seq, 0))


def _pool_kernel(p_ref, prev_ref, next_ref, w_ref, scale_ref, o_ref, ext_scr, *,
                 tiles_per_batch, batch, seq, ctx_len):
    i = pl.program_id(0)
    tp = p_ref.shape[0]
    is_ctx = i >= batch * tiles_per_batch
    j = jnp.where(is_ctx, 0, i % tiles_per_batch)
    seg_len = jnp.where(is_ctx, ctx_len, seq)
    last = jnp.where(is_ctx, 0, tiles_per_batch - 1)
    ext_scr[0:8, :] = jnp.where(j > 0, prev_ref[...], 0.0)
    ext_scr[8:8 + tp, :] = p_ref[...]
    ext_scr[8 + tp:16 + tp, :] = jnp.where(j < last, next_ref[...], 0.0)
    pos = j * tp + lax.broadcasted_iota(i32, (tp, 1), 0)
    for gi, w in enumerate(POOL_WINDOWS):
        c0 = gi * POOL_GROUP
        half = w // 2
        tot = ext_scr[8 - half:8 - half + tp, c0:c0 + POOL_GROUP]
        for dlt in range(1 - half, half):
            tot = tot + ext_scr[8 + dlt:8 + dlt + tp, c0:c0 + POOL_GROUP]
        cnt = (jnp.minimum(pos - half + w, seg_len) - jnp.maximum(pos - half, 0)).astype(f32)
        dlt_mean = (tot / cnt - ext_scr[8:8 + tp, c0:c0 + POOL_GROUP]).astype(bf16)
        y = jnp.dot(dlt_mean, w_ref[gi].astype(bf16), preferred_element_type=f32)
        o_ref[:, c0:c0 + POOL_GROUP] = (y * scale_ref[:, c0:c0 + POOL_GROUP]).astype(o_ref.dtype)


def _pool(p, pool_w, pool_scale, *, batch, seq, ctx_len):
    r, width = p.shape
    tp = ctx_len
    tpb = seq // tp
    return pl.pallas_call(
        functools.partial(_pool_kernel, tiles_per_batch=tpb, batch=batch, seq=seq, ctx_len=ctx_len),
        grid=(r // tp,),
        in_specs=[pl.BlockSpec((tp, width), lambda i: (i, 0)),
                  pl.BlockSpec((8, width), lambda i: (jnp.maximum(i * (tp // 8) - 1, 0), 0)),
                  pl.BlockSpec((8, width), lambda i: (jnp.minimum((i + 1) * (tp // 8), r // 8 - 1), 0)),
                  pl.BlockSpec(pool_w.shape, lambda i: (0, 0, 0)),
                  pl.BlockSpec((1, width), lambda i: (0, 0))],
        out_specs=pl.BlockSpec((tp, width), lambda i: (i, 0)),
        out_shape=jax.ShapeDtypeStruct((r, width), bf16),
        scratch_shapes=[pltpu.VMEM((tp + 16, width), f32)],
        compiler_params=_cparams(("arbitrary",)),
        name="pool",
    )(p, p, p, pool_w, pool_scale.reshape(1, width))


def _outproj_kernel(a_ref, b_ref, x_ref, w_ref, g1_ref, sh_ref, sc_ref, gn_ref, wr_ref, br_ref,
                    xn_ref, h2_ref, lg_ref, *, tiles_per_batch, batch):
    seg = _tile_segment(pl.program_id(0), tiles_per_batch, batch)
    half = a_ref.shape[1]
    y = (jnp.dot(a_ref[...], w_ref[0:half, :], preferred_element_type=f32)
         + jnp.dot(b_ref[...], w_ref[half:, :], preferred_element_type=f32))
    xn = x_ref[...] + g1_ref[pl.ds(seg, 1), :] * y
    xn_ref[...] = xn
    h2 = _norm_mod(xn, gn_ref[...], sh_ref[pl.ds(seg, 1), :], sc_ref[pl.ds(seg, 1), :])
    h2_ref[...] = h2.astype(bf16)
    lg_ref[...] = jnp.dot(h2, wr_ref[...], preferred_element_type=f32,
                          precision=lax.Precision.HIGHEST) + br_ref[...]


def _outproj(a, b, x, w, mods, g_ffn, w_r, b_r, *, n_tiles, batch, seq):
    r, d = x.shape
    half = a.shape[1]
    row = lambda i: (i, 0)
    return pl.pallas_call(
        functools.partial(_outproj_kernel, tiles_per_batch=seq // TM, batch=batch),
        grid=(n_tiles,),
        in_specs=[pl.BlockSpec((TM, half), row), pl.BlockSpec((TM, half), row), pl.BlockSpec((TM, d), row),
                  pl.BlockSpec(w.shape, lambda i: (0, 0), pipeline_mode=pl.Buffered(1)),
                  pl.BlockSpec((8, d), lambda i: (0, 2)),
                  pl.BlockSpec((8, d), lambda i: (0, 3)),
                  pl.BlockSpec((8, d), lambda i: (0, 4)),
                  pl.BlockSpec((1, d), lambda i: (0, 0)),
                  pl.BlockSpec(w_r.shape, lambda i: (0, 0)),
                  pl.BlockSpec((1, 128), lambda i: (0, 0))],
        out_specs=[pl.BlockSpec((TM, d), row), pl.BlockSpec((TM, d), row), pl.BlockSpec((TM, 128), row)],
        out_shape=[jax.ShapeDtypeStruct((r, d), f32), jax.ShapeDtypeStruct((r, d), bf16),
                   jax.ShapeDtypeStruct((r, 128), f32)],
        compiler_params=_cparams(("arbitrary",)),
        name="outproj",
    )(a, b, x, w, mods, mods, mods, g_ffn, w_r, b_r)


def _moe_kernel(be_ref, nu_ref, xs_ref, w1_ref, w3_ref, w2_ref, ys_ref, w1_scr, w3_scr, w2_scr):
    i = pl.program_id(0)
    e = be_ref[i]
    prev = be_ref[jnp.maximum(i - 1, 0)]

    @pl.when((i == 0) | (e != prev))
    def _():
        w1_scr[...] = w1_ref[...].astype(bf16)
        w3_scr[...] = w3_ref[...].astype(bf16)
        w2_scr[...] = w2_ref[...].astype(bf16)

    @pl.when(i < nu_ref[0])
    def _():
        xb = xs_ref[...]
        a = jnp.dot(xb, w1_scr[...], preferred_element_type=f32)
        b = jnp.dot(xb, w3_scr[...], preferred_element_type=f32)
        hid = (a * jax.nn.sigmoid(a) * b).astype(bf16)
        ys_ref[...] = jnp.dot(hid, w2_scr[...], preferred_element_type=f32)

    @pl.when(i >= nu_ref[0])
    def _():
        ys_ref[...] = jnp.zeros_like(ys_ref)


def _moe_experts(xs, block_e, n_used, w1, w3, w2, nb):
    m = MOE_BLOCK
    _, d, de = w1.shape
    grid_spec = pltpu.PrefetchScalarGridSpec(
        num_scalar_prefetch=2,
        grid=(nb,),
        in_specs=[pl.BlockSpec((m, d), lambda i, be, nu: (jnp.minimum(i, nu[0] - 1), 0)),
                  pl.BlockSpec((None, d, de), lambda i, be, nu: (be[i], 0, 0)),
                  pl.BlockSpec((None, d, de), lambda i, be, nu: (be[i], 0, 0)),
                  pl.BlockSpec((None, de, d), lambda i, be, nu: (be[i], 0, 0))],
        out_specs=pl.BlockSpec((m, d), lambda i, be, nu: (jnp.where(i < nu[0], i, nb), 0)),
        scratch_shapes=[pltpu.VMEM((d, de), bf16), pltpu.VMEM((d, de), bf16), pltpu.VMEM((de, d), bf16)],
    )
    return pl.pallas_call(
        _moe_kernel,
        grid_spec=grid_spec,
        out_shape=jax.ShapeDtypeStruct(((nb + 1) * m, d), f32),
        compiler_params=_cparams(("arbitrary",)),
        name="moe_experts",
    )(block_e, n_used, xs, w1, w3, w2)


def _route(logits):
    t = logits.shape[0]
    pg = jax.nn.softmax(logits[:, :MOE_GROUPS], axis=-1)
    pg_top, g_idx = lax.top_k(pg, 1)
    el = logits[:, MOE_GROUPS:MOE_GROUPS + MOE_EXPERTS].reshape(t, MOE_GROUPS, MOE_PER_GROUP)
    el = jnp.take_along_axis(el, g_idx[:, :, None], axis=1)[:, 0]
    pe_top, e_idx = lax.top_k(jax.nn.softmax(el, axis=-1), MOE_TOPK)
    gate = pg_top * pe_top / jnp.sum(pe_top, axis=-1, keepdims=True)
    return (g_idx * MOE_PER_GROUP + e_idx).astype(i32), gate


def _combine_kernel(x_ref, y0_ref, y1_ref, gt_ref, g2_ref, gn_ref, o_ref, *, tiles_per_batch, batch, final_norm):
    seg = _tile_segment(pl.program_id(0), tiles_per_batch, batch)
    gt = gt_ref[...]
    x = x_ref[...] + g2_ref[pl.ds(seg, 1), :] * (gt[:, 0:1] * y0_ref[...] + gt[:, 1:2] * y1_ref[...])
    if final_norm:
        ms = jnp.mean(x * x, axis=-1, keepdims=True)
        x = x * lax.rsqrt(ms + EPS) * gn_ref[...]
    o_ref[...] = x


def _combine(x, y0, y1, gate, mods, g_out, *, n_tiles, batch, seq, final_norm):
    r, d = x.shape
    row = lambda i: (i, 0)
    return pl.pallas_call(
        functools.partial(_combine_kernel, tiles_per_batch=seq // TM, batch=batch, final_norm=final_norm),
        grid=(n_tiles,),
        in_specs=[pl.BlockSpec((TM, d), row), pl.BlockSpec((TM, d), row), pl.BlockSpec((TM, d), row),
                  pl.BlockSpec((TM, MOE_TOPK), row),
                  pl.BlockSpec((8, d), lambda i: (0, 5)),
                  pl.BlockSpec((1, d), lambda i: (0, 0))],
        out_specs=pl.BlockSpec((TM, d), row),
        out_shape=jax.ShapeDtypeStruct((n_tiles * TM, d), f32),
        compiler_params=_cparams(("arbitrary",)),
        name="combine",
    )(x, y0, y1, gate, mods, g_out)


def _moe(xn, h2, logits, mods, w1, w3, w2, g_out, *, n_tiles, batch, seq, final_norm):
    t = n_tiles * TM
    m = MOE_BLOCK
    expert, gate = _route(logits[:t])
    s = t * MOE_TOPK
    nb = s // m + MOE_EXPERTS
    flat_e = expert.reshape(s)
    onehot = (flat_e[:, None] == jnp.arange(MOE_EXPERTS, dtype=i32)[None, :]).astype(i32)
    csum = jnp.cumsum(onehot, axis=0)
    rank = jnp.sum((csum - onehot) * onehot, axis=1)
    counts = csum[-1]
    padded = ((counts + m - 1) // m) * m
    pad_end = jnp.cumsum(padded)
    dest = (pad_end - padded)[flat_e] + rank
    n_used = (pad_end[-1] // m).astype(i32)
    blk = jnp.arange(nb, dtype=i32)
    block_e = jnp.minimum(jnp.searchsorted(pad_end, jnp.minimum(blk, n_used - 1) * m, side='right'),
                          MOE_EXPERTS - 1).astype(i32)
    tok = (jnp.arange(s, dtype=i32) // MOE_TOPK)
    buf_tok = jnp.zeros((nb * m,), i32).at[dest].set(tok)
    xs = jnp.take(h2, buf_tok, axis=0)
    ys = _moe_experts(xs, block_e, n_used.reshape(1), w1, w3, w2, nb)
    d2 = dest.reshape(t, MOE_TOPK)
    y0 = jnp.take(ys, d2[:, 0], axis=0)
    y1 = jnp.take(ys, d2[:, 1], axis=0)
    return _combine(xn, y0, y1, gate, mods, g_out, n_tiles=n_tiles, batch=batch, seq=seq, final_norm=final_norm)


def _short_conv(u, w, b):
    up = jnp.pad(u, ((0, 0), (1, 1), (0, 0)))
    return up[:, :-2] * w[0] + up[:, 1:-1] * w[1] + up[:, 2:] * w[2] + b


def _hyena_filters(n, w1, b1, freq, w2, b2, w3):
    t = jnp.linspace(0.0, 1.0, n, dtype=f32)[:, None]
    bands = (HY_EMB - 1) // 2
    w = 2 * math.pi * jnp.arange(n, dtype=f32)[:, None] / n
    f = jnp.linspace(1e-4, bands - 1, bands, dtype=f32)[None, :]
    z = jnp.concatenate([t, jnp.cos(f * w), -jnp.sin(f * w)], axis=-1)
    hid = jnp.sin(freq[0] * (z @ w1 + b1))
    hid = jnp.sin(freq[1] * (hid @ w2 + b2))
    k = (hid @ w3).reshape(n, HY_ORDER, 2, HY_WIDTH)
    max_decay = math.log(HY_TARGET) / HY_FAST_DECAY
    min_decay = math.log(HY_TARGET) / HY_SLOW_DECAY
    deltas = jnp.linspace(min_decay, max_decay, HY_WIDTH, dtype=f32)
    k = k * jnp.exp(-t * jnp.abs(deltas))[:, None, None, :]
    norm = jnp.sum(jnp.abs(k[:, :, 0]), axis=0) + jnp.sum(jnp.abs(k[1:, :, 1]), axis=0)
    return k / norm[None, :, None, :]


def _long_conv(u, k_fwd, k_bwd, bias):
    n = u.shape[1]
    k_full = jnp.concatenate([k_fwd, jnp.zeros_like(k_fwd[:1]), k_bwd[:0:-1]], axis=0)
    spec = jnp.fft.rfft(u, n=2 * n, axis=1) * jnp.fft.rfft(k_full, axis=0)[None]
    y = jnp.fft.irfft(spec, n=2 * n, axis=1)[:, :n]
    return y + u * bias


def _hyena(u, conv_w, conv_b, f_w1, f_b1, f_freq, f_w2, f_b2, f_w3, hy_bias):
    n = u.shape[1]
    g = _short_conv(u, conv_w, conv_b)
    v, x1, x2 = jnp.split(g, HY_ORDER + 1, axis=-1)
    k = _hyena_filters(n, f_w1, f_b1, f_freq, f_w2, f_b2, f_w3)
    z = x1 * _long_conv(v, k[:, 0, 0], k[:, 0, 1], hy_bias[0])
    return x2 * _long_conv(z, k[:, 1, 0], k[:, 1, 1], hy_bias[1])


S5_TT = 256
S5_GB = 8


def _gelu_tanh(x):
    return 0.5 * x * (1.0 + jnp.tanh(0.7978845608028654 * (x + 0.044715 * x * x * x)))


def _s5_kernel(u_ref, perm_ref, permt_ref, wb_ref, wc_ref, are_ref, aim_ref, pre_ref, pim_ref, *rest, fuse_glu):
    if fuse_glu:
        yf_ref, d_ref, gw_ref, gb_ref, o_ref, st_scr, bu_scr, y_scr = rest
    else:
        o_ref, st_scr, bu_scr, y_scr = rest
    tt = u_ref.shape[0]
    steps = tt // 8
    half = bu_scr.shape[1] // 2

    @pl.when(pl.program_id(1) == 0)
    def _():
        st_scr[...] = jnp.zeros_like(st_scr)

    up = jnp.dot(perm_ref[...], u_ref[...].astype(bf16), preferred_element_type=f32).astype(bf16)
    for gb in range(wb_ref.shape[0]):
        bu_scr[...] = jnp.dot(up[:, gb * 128:(gb + 1) * 128], wb_ref[gb], preferred_element_type=f32)
        a_re = jnp.broadcast_to(are_ref[gb], (8, half))
        a_im = jnp.broadcast_to(aim_ref[gb], (8, half))

        def scan_step(k, h):
            hr, hi = h
            rows = pl.ds(pl.multiple_of(k * 8, 8), 8)
            nr = a_re * hr - a_im * hi + bu_scr[rows, 0:half]
            ni = a_re * hi + a_im * hr + bu_scr[rows, half:]
            bu_scr[rows, 0:half] = nr
            bu_scr[rows, half:] = ni
            return nr, ni

        zero = jnp.zeros((8, half), f32)
        hr, hi = lax.fori_loop(0, steps, scan_step, (zero, zero), unroll=4)

        pw_r = pre_ref[gb, steps - 1:steps, :]
        pw_i = pim_ref[gb, steps - 1:steps, :]
        cur_r = st_scr[gb, :, 0:half]
        cur_i = st_scr[gb, :, half:]
        car_r, car_i = [], []
        for j in range(8):
            car_r.append(cur_r)
            car_i.append(cur_i)
            cur_r, cur_i = (hr[j:j + 1] + pw_r * cur_r - pw_i * cur_i,
                            hi[j:j + 1] + pw_r * cur_i + pw_i * cur_r)
        st_scr[gb, :, 0:half] = cur_r
        st_scr[gb, :, half:] = cur_i
        cr = jnp.concatenate(car_r, axis=0)
        ci = jnp.concatenate(car_i, axis=0)

        def fix_step(k, carry):
            rows = pl.ds(pl.multiple_of(k * 8, 8), 8)
            pr = pre_ref[gb, pl.ds(k, 1), :]
            pi = pim_ref[gb, pl.ds(k, 1), :]
            bu_scr[rows, 0:half] = bu_scr[rows, 0:half] + (pr * cr - pi * ci)
            bu_scr[rows, half:] = bu_scr[rows, half:] + (pr * ci + pi * cr)
            return carry

        lax.fori_loop(0, steps, fix_step, 0, unroll=4)
        y_scr[:, gb * 128:(gb + 1) * 128] = jnp.dot(bu_scr[...].astype(bf16), wc_ref[gb],
                                                      preferred_element_type=f32)
    y = jnp.dot(permt_ref[...], y_scr[...], preferred_element_type=f32, precision=lax.Precision.HIGHEST)
    if fuse_glu:
        g = _gelu_tanh(y + yf_ref[...] + u_ref[...] * d_ref[...])
        z = jnp.dot(g.astype(bf16), gw_ref[...], preferred_element_type=f32) + gb_ref[...]
        o_ref[...] = (g * jax.nn.sigmoid(z)).astype(o_ref.dtype)
    else:
        o_ref[...] = y


def _s5_tables(a_re, a_im, log_dt, b_re, b_im, c_re, c_im, steps):
    ng, ns = a_re.shape
    nb = ng // S5_GB
    lam = lax.complex(a_re, a_im)
    dt = jnp.exp(log_dt)[:, None]
    lam_bar = jnp.exp(lam * dt)
    b_bar = ((lam_bar - 1) / lam)[..., None] * lax.complex(b_re, b_im)
    cm = lax.complex(c_re, c_im)
    pw = jnp.exp((lam * dt)[None] * jnp.arange(1, steps + 1, dtype=f32)[:, None, None])
    eye = jnp.eye(S5_GB, dtype=f32)
    bb = b_bar.reshape(nb, S5_GB, ns, S5_GROUP)
    wb = jnp.concatenate([jnp.einsum('bgpc,gh->bgchp', part, eye).reshape(nb, S5_GB * S5_GROUP, S5_GB * ns)
                          for part in (jnp.real(bb), jnp.imag(bb))], axis=-1)
    cc = cm.reshape(nb, S5_GB, S5_GROUP, ns)
    wc = jnp.concatenate([jnp.einsum('bgcp,gh->bgphc', part, eye).reshape(nb, S5_GB * ns, S5_GB * S5_GROUP)
                          for part in (jnp.real(cc), -jnp.imag(cc))], axis=1)
    lb = lam_bar.reshape(nb, 1, S5_GB * ns)
    pwb = pw.reshape(steps, nb, S5_GB * ns).transpose(1, 0, 2)
    return (wb.astype(bf16), wc.astype(bf16), jnp.real(lb), jnp.imag(lb), jnp.real(pwb), jnp.imag(pwb))


def _s5_perm(tt, reverse):
    steps = tt // 8
    dst = jnp.arange(tt)
    t = (dst % 8) * steps + dst // 8
    src = tt - 1 - t if reverse else t
    return jnp.arange(tt)[None, :] == src[:, None]


def _s5(u, a_re, a_im, log_dt, b_re, b_im, c_re, c_im, s5_d, glu_w, glu_b, *, batch, seq, ctx_len):
    r, width = u.shape
    tt = S5_TT
    assert ctx_len == tt and seq % tt == 0
    n_lat = seq // tt
    ctx_blk0 = (batch * seq) // tt
    y = None
    for di in range(2):
        rev = di == 1
        wb, wc, lr, li, pr, pi = _s5_tables(a_re[di], a_im[di], log_dt[di], b_re[di], b_im[di],
                                            c_re[di], c_im[di], tt // 8)
        perm = _s5_perm(tt, rev)

        def tile_map(b, i, rev=rev):
            lat = b * n_lat + (n_lat - i if rev else i - 1)
            return (jnp.where(i == 0, ctx_blk0 + b, lat), 0)

        row_spec = pl.BlockSpec((tt, width), tile_map)
        const = lambda a: pl.BlockSpec(a.shape, lambda b, i: (0,) * a.ndim)
        ins = [u, perm.astype(bf16), perm.T.astype(f32), wb, wc, lr, li, pr, pi]
        in_specs = [row_spec] + [const(a) for a in ins[1:]]
        if rev:
            extra = [y, s5_d.reshape(1, width), glu_w.astype(bf16), glu_b.reshape(1, width)]
            in_specs += [row_spec] + [const(a) for a in extra[1:]]
            ins += extra
        y = pl.pallas_call(
            functools.partial(_s5_kernel, fuse_glu=rev),
            grid=(batch, n_lat + 1),
            in_specs=in_specs,
            out_specs=row_spec,
            out_shape=jax.ShapeDtypeStruct((r, width), bf16 if rev else f32),
            scratch_shapes=[pltpu.VMEM((wb.shape[0], 1, wb.shape[2]), f32),
                            pltpu.VMEM((tt, wb.shape[2]), f32),
                            pltpu.VMEM((tt, width), f32)],
            compiler_params=_cparams(("arbitrary", "arbitrary")),
            name="s5_bwd_glu" if rev else "s5_fwd",
        )(*ins)
    return y


def _rope_tables(seq):
    rows = seq // GRID_W
    row = jnp.repeat(jnp.arange(rows, dtype=f32), GRID_W)
    col = jnp.tile(jnp.arange(GRID_W, dtype=f32), rows)
    n_freq = DIFF_QK // 4
    inv = ROPE_BASE ** (-jnp.arange(n_freq, dtype=f32) / n_freq)
    ar, ac = row[:, None] * inv, col[:, None] * inv
    cos = jnp.concatenate([jnp.cos(ar), jnp.cos(ar), jnp.cos(ac), jnp.cos(ac)], axis=-1)
    sin = jnp.concatenate([-jnp.sin(ar), jnp.sin(ar), -jnp.sin(ac), jnp.sin(ac)], axis=-1)
    cos = jnp.concatenate([jnp.tile(cos, (1, 2)), jnp.ones((TM, 128), f32)], axis=0)
    sin = jnp.concatenate([jnp.tile(sin, (1, 2)), jnp.zeros((TM, 128), f32)], axis=0)
    return cos, sin


def kernel(x, c, ctx, c_ctx, ada_w, ada_b, norm_mix_g, norm_ffn_g, norm_out_g, ab_w_in, ab_w_out, diff_lam, diff_subln_g, pool_w, pool_scale, cd_w_in, cd_w_out, hy_conv_w, hy_conv_b, hy_f_w1, hy_f_b1, hy_f_freq, hy_f_w2, hy_f_b2, hy_f_w3, hy_bias, s5_a_re, s5_a_im, s5_log_dt, s5_b_re, s5_b_im, s5_c_re, s5_c_im, s5_d, glu_w, glu_b, moe_w_rg, moe_b_rg, moe_w_re, moe_b_re, moe_w1, moe_w3, moe_w2):
    bsz, seq, d = x.shape
    cl = ctx.shape[1]
    depth = ada_w.shape[0]
    n_lat = bsz * seq
    assert seq % TM == 0 and (bsz * cl) % TM == 0 and bsz + 1 <= 8
    xs = jnp.concatenate([x.reshape(n_lat, d), ctx.reshape(bsz * cl, d)], axis=0)
    r = xs.shape[0]
    cond = jnp.zeros((8, d), f32).at[0:bsz].set(c).at[bsz].set(c_ctx)
    mods = _ada_mods(cond, ada_w, ada_b)
    geom = dict(batch=bsz, seq=seq)

    for l in range(depth):
        last = l == depth - 1
        i = l // 2
        n_tiles = (n_lat if last else r) // TM
        g_mix = norm_mix_g[l].reshape(1, d)
        if l % 2 == 0:
            lam_init = 0.8 - 0.6 * math.exp(-0.3 * l)
            cos, sin = _rope_tables(seq)
            q, k, v, p = _proj_rope(xs, g_mix, mods[l], ab_w_in[i].astype(bf16), cos, sin, **geom)
            mix_a = _diff_attention(q, k, v, diff_lam[i], diff_subln_g[i], lam_init, ctx_len=cl, **geom)
            mix_b = _pool(p, pool_w[i], pool_scale[i], ctx_len=cl, **geom)
            w_out = ab_w_out[i]
        else:
            hw = (HY_ORDER + 1) * HY_WIDTH
            hy, u = _proj(xs, g_mix, mods[l], cd_w_in[i].astype(bf16), hw, **geom)
            y_h = _hyena(hy[:n_lat].reshape(bsz, seq, hw), hy_conv_w[i], hy_conv_b[i], hy_f_w1[i], hy_f_b1[i],
                         hy_f_freq[i], hy_f_w2[i], hy_f_b2[i], hy_f_w3[i], hy_bias[i])
            mix_a = jnp.pad(y_h.reshape(n_lat, -1).astype(bf16), ((0, r - n_lat), (0, 0)))
            mix_b = _s5(u, s5_a_re[i], s5_a_im[i], s5_log_dt[i], s5_b_re[i], s5_b_im[i],
                        s5_c_re[i], s5_c_im[i], s5_d[i], glu_w[i], glu_b[i], ctx_len=cl, **geom)
            w_out = cd_w_out[i]
        w_r = jnp.zeros((d, 128), f32).at[:, :MOE_GROUPS].set(moe_w_rg[l]) \
            .at[:, MOE_GROUPS:MOE_GROUPS + MOE_EXPERTS].set(moe_w_re[l])
        b_r = jnp.zeros((1, 128), f32).at[0, :MOE_GROUPS].set(moe_b_rg[l]) \
            .at[0, MOE_GROUPS:MOE_GROUPS + MOE_EXPERTS].set(moe_b_re[l])
        xn, h2, logits = _outproj(mix_a, mix_b, xs, w_out.astype(bf16), mods[l], norm_ffn_g[l].reshape(1, d),
                                  w_r, b_r, n_tiles=n_tiles, **geom)
        xs = _moe(xn, h2, logits, mods[l], moe_w1[l], moe_w3[l], moe_w2[l], norm_out_g.reshape(1, d),
                  n_tiles=n_tiles, final_norm=last, **geom)
    return xs[:n_lat].reshape(bsz, seq, d)
```

```python
import functools
import math

import jax
import jax.numpy as jnp
from jax import lax
from jax.experimental import pallas as pl
from jax.experimental.pallas import tpu as pltpu

f32, bf16, i32 = jnp.float32, jnp.bfloat16, jnp.int32

EPS = 1e-6
GRID_W = 64
DIFF_QK = 64
DIFF_V = 128
ROPE_BASE = 10000.0
POOL_WINDOWS = (2, 4, 8, 16)
POOL_GROUP = 256
HY_WIDTH = 1024
HY_ORDER = 2
HY_EMB = 33
HY_FAST_DECAY = 0.3
HY_SLOW_DECAY = 1.5
HY_TARGET = 1e-2
S5_GROUP = 16
S5_GROUPS = 64
S5_STATE = 64
MOE_GROUPS = 8
MOE_PER_GROUP = 8
MOE_EXPERTS = 64
MOE_TOPK = 2
MOE_BLOCK = 256

VMEM_LIMIT = 56 * 1024 * 1024
TM = 512
LOG2E = 1.4426950408889634


def _cparams(sem):
    return pltpu.CompilerParams(dimension_semantics=sem, vmem_limit_bytes=VMEM_LIMIT)


def _norm_mod(x, g, shift, scale):
    ms = jnp.mean(x * x, axis=-1, keepdims=True)
    return (x * lax.rsqrt(ms + EPS) * g) * (1.0 + scale) + shift


def _tile_segment(i, lat_tiles_per_batch, batch):
    seg = jnp.zeros((), i32)
    for b in range(1, batch + 1):
        seg = seg + (i >= b * lat_tiles_per_batch).astype(i32)
    return seg


def _ada_kernel(s_ref, w_ref, b_ref, o_ref):
    s = s_ref[...]
    a = (s * jax.nn.sigmoid(s)).astype(bf16)
    o_ref[...] = jnp.dot(a, w_ref[...].astype(bf16), preferred_element_type=f32) + b_ref[...]


def _ada_mods(s, ada_w, ada_b):
    depth, d, n = ada_w.shape
    tn = 1024
    return pl.pallas_call(
        _ada_kernel,
        grid=(depth, n // tn),
        in_specs=[pl.BlockSpec((8, d), lambda l, j: (0, 0)),
                  pl.BlockSpec((None, d, tn), lambda l, j: (l, 0, j)),
                  pl.BlockSpec((None, 1, tn), lambda l, j: (l, 0, j))],
        out_specs=pl.BlockSpec((None, 8, tn), lambda l, j: (l, 0, j)),
        out_shape=jax.ShapeDtypeStruct((depth, 8, n), f32),
        compiler_params=_cparams(("arbitrary", "arbitrary")),
        name="ada_mods",
    )(s, ada_w, ada_b.reshape(depth, 1, n))


def _proj_rope_kernel(x_ref, g_ref, sh_ref, sc_ref, w_ref, cos_ref, sin_ref,
                      q_ref, k_ref, v_ref, p_ref, h_scr, *, tiles_per_batch, batch, qscale):
    seg = _tile_segment(pl.program_id(0), tiles_per_batch, batch)
    h = _norm_mod(x_ref[...], g_ref[...], sh_ref[pl.ds(seg, 1), :], sc_ref[pl.ds(seg, 1), :])
    h_scr[...] = h.astype(bf16)
    cos = cos_ref[...]
    sin = sin_ref[...]
    tm = cos.shape[0]
    lane = lax.broadcasted_iota(i32, (tm, 128), 1)
    first = (lane % 32) < 16

    def rope(r):
        sw = jnp.where(first, pltpu.roll(r, 112, 1), pltpu.roll(r, 16, 1))
        return r * cos + sw * sin

    heads = q_ref.shape[0]
    nc = 512
    for c in range(w_ref.shape[1] // nc):
        res = jnp.dot(h_scr[...], w_ref[:, c * nc:(c + 1) * nc], preferred_element_type=f32)
        for s in range(nc // 128):
            r = res[:, s * 128:(s + 1) * 128]
            col = (c * nc) // 128 + s
            if col < heads:
                q_ref[col] = (rope(r) * qscale).astype(bf16)
            elif col < 2 * heads:
                k_ref[col - heads] = rope(r).astype(bf16)
            elif col < 3 * heads:
                v_ref[col - 2 * heads] = r.astype(bf16)
            else:
                c0 = (col - 3 * heads) * 128
                p_ref[:, c0:c0 + 128] = r


def _proj_rope(x, g, mods, w, cos, sin, *, batch, seq):
    r, d = x.shape
    heads = 8
    lat_tiles = seq // TM
    n_tiles = r // TM

    def tab_map(i):
        return (jnp.where(i < batch * lat_tiles, i % lat_tiles, lat_tiles), 0)

    hm = jax.ShapeDtypeStruct((heads, r, 128), bf16)
    hm_spec = pl.BlockSpec((heads, TM, 128), lambda i: (0, i, 0))
    return pl.pallas_call(
        functools.partial(_proj_rope_kernel, tiles_per_batch=lat_tiles, batch=batch,
                          qscale=LOG2E * DIFF_QK ** -0.5),
        grid=(n_tiles,),
        in_specs=[pl.BlockSpec((TM, d), lambda i: (i, 0)),
                  pl.BlockSpec((1, d), lambda i: (0, 0)),
                  pl.BlockSpec((8, d), lambda i: (0, 0)),
                  pl.BlockSpec((8, d), lambda i: (0, 1)),
                  pl.BlockSpec(w.shape, lambda i: (0, 0), pipeline_mode=pl.Buffered(1)),
                  pl.BlockSpec((TM, 128), tab_map),
                  pl.BlockSpec((TM, 128), tab_map)],
        out_specs=[hm_spec, hm_spec, hm_spec, pl.BlockSpec((TM, 1024), lambda i: (i, 0))],
        out_shape=[hm, hm, hm, jax.ShapeDtypeStruct((r, 1024), f32)],
        scratch_shapes=[pltpu.VMEM((TM, d), bf16)],
        compiler_params=_cparams(("arbitrary",)),
        name="proj_rope",
    )(x, g, mods, mods, w, cos, sin)


def _proj_kernel(x_ref, g_ref, sh_ref, sc_ref, w_ref, hy_ref, u_ref, h_scr, *, tiles_per_batch, batch):
    seg = _tile_segment(pl.program_id(0), tiles_per_batch, batch)
    h = _norm_mod(x_ref[...], g_ref[...], sh_ref[pl.ds(seg, 1), :], sc_ref[pl.ds(seg, 1), :])
    h_scr[...] = h.astype(bf16)
    nc = 512
    n_hy = hy_ref.shape[1]
    for c in range(w_ref.shape[1] // nc):
        res = jnp.dot(h_scr[...], w_ref[:, c * nc:(c + 1) * nc], preferred_element_type=f32)
        if c * nc < n_hy:
            hy_ref[:, c * nc:(c + 1) * nc] = res
        else:
            u_ref[:, c * nc - n_hy:(c + 1) * nc - n_hy] = res


def _proj(x, g, mods, w, n_hy, *, batch, seq):
    r, d = x.shape
    n = w.shape[1]
    return pl.pallas_call(
        functools.partial(_proj_kernel, tiles_per_batch=seq // TM, batch=batch),
        grid=(r // TM,),
        in_specs=[pl.BlockSpec((TM, d), lambda i: (i, 0)),
                  pl.BlockSpec((1, d), lambda i: (0, 0)),
                  pl.BlockSpec((8, d), lambda i: (0, 0)),
                  pl.BlockSpec((8, d), lambda i: (0, 1)),
                  pl.BlockSpec(w.shape, lambda i: (0, 0), pipeline_mode=pl.Buffered(1))],
        out_specs=[pl.BlockSpec((TM, n_hy), lambda i: (i, 0)), pl.BlockSpec((TM, n - n_hy), lambda i: (i, 0))],
        out_shape=[jax.ShapeDtypeStruct((r, n_hy), f32), jax.ShapeDtypeStruct((r, n - n_hy), f32)],
        scratch_shapes=[pltpu.VMEM((TM, d), bf16)],
        compiler_params=_cparams(("arbitrary",)),
        name="proj",
    )(x, g, mods, mods, w)


def _attn_kernel(lv_ref, g_ref, q_ref, kc_ref, vc_ref, *rest, n_lat, tk, lam_init):
    if n_lat:
        kl_ref, vl_ref, o_ref = rest
    else:
        (o_ref,) = rest
    lv = lv_ref[...]
    lam = (jnp.exp(jnp.sum(lv[0:1] * lv[1:2], axis=-1, keepdims=True))
           - jnp.exp(jnp.sum(lv[2:3] * lv[3:4], axis=-1, keepdims=True)) + lam_init)
    q = q_ref[...]
    tq = q.shape[0]
    lane = lax.broadcasted_iota(i32, q.shape, 1)
    zero = jnp.zeros_like(q)
    qs = (jnp.where(lane < DIFF_QK, q, zero), jnp.where(lane >= DIFF_QK, q, zero))
    nt = (((1,), (1,)), ((), ()))

    def update(carry, k, v):
        new = []
        for m in range(2):
            mx, l, acc = carry[3 * m:3 * m + 3]
            s = lax.dot_general(qs[m], k, nt, preferred_element_type=f32)
            mn = jnp.maximum(mx, jnp.max(s, axis=-1, keepdims=True))
            alpha = jnp.exp2(mx - mn)
            p = jnp.exp2(s - mn)
            l = alpha * l + jnp.sum(p, axis=-1, keepdims=True)
            acc = alpha * acc + jnp.dot(p.astype(bf16), v, preferred_element_type=f32)
            new += [mn, l, acc]
        return tuple(new)

    init = (jnp.full((tq, 1), -jnp.inf, f32), jnp.zeros((tq, 1), f32), jnp.zeros((tq, DIFF_V), f32)) * 2
    carry = update(init, kc_ref[...], vc_ref[...])
    if n_lat:
        def body(j, carry):
            off = pl.multiple_of(j * tk, tk)
            return update(carry, kl_ref[pl.ds(off, tk), :], vl_ref[pl.ds(off, tk), :])
        carry = lax.fori_loop(0, n_lat, body, carry)
    _, l0, a0, _, l1, a1 = carry
    o = a0 / l0 - lam * (a1 / l1)
    ms = jnp.mean(o * o, axis=-1, keepdims=True)
    o_ref[...] = ((o * lax.rsqrt(ms + EPS) * g_ref[...]) * (1.0 - lam_init)).astype(o_ref.dtype)


def _diff_attention(q, k, v, lam_vecs, subln_g, lam_init, *, batch, seq, ctx_len):
    heads, r, _ = q.shape
    tq, tk = 512, 512
    ctx_blk0 = (batch * seq) // ctx_len
    small = [pl.BlockSpec((4, DIFF_QK), lambda *_: (0, 0)), pl.BlockSpec((1, DIFF_V), lambda *_: (0, 0))]
    kc_spec = pl.BlockSpec((None, ctx_len, 128), lambda b, h, *_: (h, ctx_blk0 + b, 0))
    g2 = subln_g.reshape(1, DIFF_V)
    lat = pl.pallas_call(
        functools.partial(_attn_kernel, n_lat=seq // tk, tk=tk, lam_init=lam_init),
        grid=(batch, heads, seq // tq),
        in_specs=small + [pl.BlockSpec((None, tq, 128), lambda b, h, i: (h, b * (seq // tq) + i, 0)),
                          kc_spec, kc_spec,
                          pl.BlockSpec((None, seq, 128), lambda b, h, i: (h, b, 0)),
                          pl.BlockSpec((None, seq, 128), lambda b, h, i: (h, b, 0))],
        out_specs=pl.BlockSpec((tq, 128), lambda b, h, i: (b * (seq // tq) + i, h)),
        out_shape=jax.ShapeDtypeStruct((batch * seq, heads * DIFF_V), bf16),
        compiler_params=_cparams(("arbitrary",) * 3),
        name="diff_attn_lat",
    )(lam_vecs, g2, q, k, v, k, v)
    ctx = pl.pallas_call(
        functools.partial(_attn_kernel, n_lat=0, tk=tk, lam_init=lam_init),
        grid=(batch, heads),
        in_specs=small + [kc_spec, kc_spec, kc_spec],
        out_specs=pl.BlockSpec((ctx_len, 128), lambda b, h: (b, h)),
        out_shape=jax.ShapeDtypeStruct((batch * ctx_len, heads * DIFF_V), bf16),
        compiler_params=_cparams(("arbitrary",) * 2),
        name="diff_attn_ctx",
    )(lam_vecs, g2, q, k, v)
    return jnp.concatenate([lat, ctx], axis=0)


def _pool_kernel(p_ref, prev_ref, next_ref, w_ref, scale_ref, o_ref, ext_scr, *,
                 tiles_per_batch, batch, seq, ctx_len):
    i = pl.program_id(0)
    tp = p_ref.shape[0]
    is_ctx = i >= batch * tiles_per_batch
    j = jnp.where(is_ctx, 0, i % tiles_per_batch)
    seg_len = jnp.where(is_ctx, ctx_len, seq)
    last = jnp.where(is_ctx, 0, tiles_per_batch - 1)
    ext_scr[0:8, :] = jnp.where(j > 0, prev_ref[...], 0.0)
    ext_scr[8:8 + tp, :] = p_ref[...]
    ext_scr[8 + tp:16 + tp, :] = jnp.where(j < last, next_ref[...], 0.0)
    pos = j * tp + lax.broadcasted_iota(i32, (tp, 1), 0)
    for gi, w in enumerate(POOL_WINDOWS):
        c0 = gi * POOL_GROUP
        half = w // 2
        tot = ext_scr[8 - half:8 - half + tp, c0:c0 + POOL_GROUP]
        for dlt in range(1 - half, half):
            tot = tot + ext_scr[8 + dlt:8 + dlt + tp, c0:c0 + POOL_GROUP]
        cnt = (jnp.minimum(pos - half + w, seg_len) - jnp.maximum(pos - half, 0)).astype(f32)
        dlt_mean = (tot / cnt - ext_scr[8:8 + tp, c0:c0 + POOL_GROUP]).astype(bf16)
        y = jnp.dot(dlt_mean, w_ref[gi].astype(bf16), preferred_element_type=f32)
        o_ref[:, c0:c0 + POOL_GROUP] = (y * scale_ref[:, c0:c0 + POOL_GROUP]).astype(o_ref.dtype)


def _pool(p, pool_w, pool_scale, *, batch, seq, ctx_len):
    r, width = p.shape
    tp = ctx_len
    tpb = seq // tp
    return pl.pallas_call(
        functools.partial(_pool_kernel, tiles_per_batch=tpb, batch=batch, seq=seq, ctx_len=ctx_len),
        grid=(r // tp,),
        in_specs=[pl.BlockSpec((tp, width), lambda i: (i, 0)),
                  pl.BlockSpec((8, width), lambda i: (jnp.maximum(i * (tp // 8) - 1, 0), 0)),
                  pl.BlockSpec((8, width), lambda i: (jnp.minimum((i + 1) * (tp // 8), r // 8 - 1), 0)),
                  pl.BlockSpec(pool_w.shape, lambda i: (0, 0, 0)),
                  pl.BlockSpec((1, width), lambda i: (0, 0))],
        out_specs=pl.BlockSpec((tp, width), lambda i: (i, 0)),
        out_shape=jax.ShapeDtypeStruct((r, width), bf16),
        scratch_shapes=[pltpu.VMEM((tp + 16, width), f32)],
        compiler_params=_cparams(("arbitrary",)),
        name="pool",
    )(p, p, p, pool_w, pool_scale.reshape(1, width))


def _outproj_kernel(a_ref, b_ref, x_ref, w_ref, g1_ref, sh_ref, sc_ref, gn_ref, wr_ref, br_ref,
                    xn_ref, h2_ref, lg_ref, *, tiles_per_batch, batch):
    seg = _tile_segment(pl.program_id(0), tiles_per_batch, batch)
    half = a_ref.shape[1]
    y = (jnp.dot(a_ref[...], w_ref[0:half, :], preferred_element_type=f32)
         + jnp.dot(b_ref[...], w_ref[half:, :], preferred_element_type=f32))
    xn = x_ref[...] + g1_ref[pl.ds(seg, 1), :] * y
    xn_ref[...] = xn
    h2 = _norm_mod(xn, gn_ref[...], sh_ref[pl.ds(seg, 1), :], sc_ref[pl.ds(seg, 1), :])
    h2_ref[...] = h2.astype(bf16)
    lg_ref[...] = jnp.dot(h2, wr_ref[...], preferred_element_type=f32,
                          precision=lax.Precision.HIGHEST) + br_ref[...]


def _outproj(a, b, x, w, mods, g_ffn, w_r, b_r, *, n_tiles, batch, seq):
    r, d = n_tiles * TM, x.shape[1]
    half = a.shape[1]
    row = lambda i: (i, 0)
    return pl.pallas_call(
        functools.partial(_outproj_kernel, tiles_per_batch=seq // TM, batch=batch),
        grid=(n_tiles,),
        in_specs=[pl.BlockSpec((TM, half), row), pl.BlockSpec((TM, half), row), pl.BlockSpec((TM, d), row),
                  pl.BlockSpec(w.shape, lambda i: (0, 0), pipeline_mode=pl.Buffered(1)),
                  pl.BlockSpec((8, d), lambda i: (0, 2)),
                  pl.BlockSpec((8, d), lambda i: (0, 3)),
                  pl.BlockSpec((8, d), lambda i: (0, 4)),
                  pl.BlockSpec((1, d), lambda i: (0, 0)),
                  pl.BlockSpec(w_r.shape, lambda i: (0, 0)),
                  pl.BlockSpec((1, 128), lambda i: (0, 0))],
        out_specs=[pl.BlockSpec((TM, d), row), pl.BlockSpec((TM, d), row), pl.BlockSpec((TM, 128), row)],
        out_shape=[jax.ShapeDtypeStruct((r, d), f32), jax.ShapeDtypeStruct((r, d), bf16),
                   jax.ShapeDtypeStruct((r, 128), f32)],
        compiler_params=_cparams(("arbitrary",)),
        name="outproj",
    )(a, b, x, w, mods, mods, mods, g_ffn, w_r, b_r)


def _moe_kernel(be_ref, nu_ref, xs_ref, w1_ref, w3_ref, w2_ref, ys_ref, w1_scr, w3_scr, w2_scr):
    i = pl.program_id(0)
    e = be_ref[i]
    prev = be_ref[jnp.maximum(i - 1, 0)]

    @pl.when((i == 0) | (e != prev))
    def _():
        w1_scr[...] = w1_ref[...].astype(bf16)
        w3_scr[...] = w3_ref[...].astype(bf16)
        w2_scr[...] = w2_ref[...].astype(bf16)

    @pl.when(i < nu_ref[0])
    def _():
        xb = xs_ref[...]
        a = jnp.dot(xb, w1_scr[...], preferred_element_type=f32)
        b = jnp.dot(xb, w3_scr[...], preferred_element_type=f32)
        hid = (a * jax.nn.sigmoid(a) * b).astype(bf16)
        ys_ref[...] = jnp.dot(hid, w2_scr[...], preferred_element_type=f32)

    @pl.when(i >= nu_ref[0])
    def _():
        ys_ref[...] = jnp.zeros_like(ys_ref)


def _moe_experts(xs, block_e, n_used, w1, w3, w2, nb):
    m = MOE_BLOCK
    _, d, de = w1.shape
    grid_spec = pltpu.PrefetchScalarGridSpec(
        num_scalar_prefetch=2,
        grid=(nb,),
        in_specs=[pl.BlockSpec((m, d), lambda i, be, nu: (jnp.minimum(i, nu[0] - 1), 0)),
                  pl.BlockSpec((None, d, de), lambda i, be, nu: (be[i], 0, 0)),
                  pl.BlockSpec((None, d, de), lambda i, be, nu: (be[i], 0, 0)),
                  pl.BlockSpec((None, de, d), lambda i, be, nu: (be[i], 0, 0))],
        out_specs=pl.BlockSpec((m, d), lambda i, be, nu: (i, 0)),
        scratch_shapes=[pltpu.VMEM((d, de), bf16), pltpu.VMEM((d, de), bf16), pltpu.VMEM((de, d), bf16)],
    )
    return pl.pallas_call(
        _moe_kernel,
        grid_spec=grid_spec,
        out_shape=jax.ShapeDtypeStruct((nb * m, d), f32),
        compiler_params=_cparams(("arbitrary",)),
        name="moe_experts",
    )(block_e, n_used, xs, w1, w3, w2)


def _route(logits):
    t = logits.shape[0]
    pg = jax.nn.softmax(logits[:, :MOE_GROUPS], axis=-1)
    pg_top, g_idx = lax.top_k(pg, 1)
    el = logits[:, MOE_GROUPS:MOE_GROUPS + MOE_EXPERTS].reshape(t, MOE_GROUPS, MOE_PER_GROUP)
    el = jnp.take_along_axis(el, g_idx[:, :, None], axis=1)[:, 0]
    pe_top, e_idx = lax.top_k(jax.nn.softmax(el, axis=-1), MOE_TOPK)
    gate = pg_top * pe_top / jnp.sum(pe_top, axis=-1, keepdims=True)
    return (g_idx * MOE_PER_GROUP + e_idx).astype(i32), gate


def _combine_kernel(x_ref, y0_ref, y1_ref, gt_ref, g2_ref, gn_ref, o_ref, *, tiles_per_batch, batch, final_norm):
    seg = _tile_segment(pl.program_id(0), tiles_per_batch, batch)
    gt = gt_ref[...]
    x = x_ref[...] + g2_ref[pl.ds(seg, 1), :] * (gt[:, 0:1] * y0_ref[...] + gt[:, 1:2] * y1_ref[...])
    if final_norm:
        ms = jnp.mean(x * x, axis=-1, keepdims=True)
        x = x * lax.rsqrt(ms + EPS) * gn_ref[...]
    o_ref[...] = x


def _combine(x, y0, y1, gate, mods, g_out, *, n_tiles, batch, seq, final_norm):
    r, d = x.shape
    row = lambda i: (i, 0)
    return pl.pallas_call(
        functools.partial(_combine_kernel, tiles_per_batch=seq // TM, batch=batch, final_norm=final_norm),
        grid=(n_tiles,),
        in_specs=[pl.BlockSpec((TM, d), row), pl.BlockSpec((TM, d), row), pl.BlockSpec((TM, d), row),
                  pl.BlockSpec((TM, MOE_TOPK), row),
                  pl.BlockSpec((8, d), lambda i: (0, 5)),
                  pl.BlockSpec((1, d), lambda i: (0, 0))],
        out_specs=pl.BlockSpec((TM, d), row),
        out_shape=jax.ShapeDtypeStruct((n_tiles * TM, d), f32),
        compiler_params=_cparams(("arbitrary",)),
        name="combine",
    )(x, y0, y1, gate, mods, g_out)


def _moe(xn, h2, logits, mods, w1, w3, w2, g_out, *, n_tiles, batch, seq, final_norm):
    t = n_tiles * TM
    m = MOE_BLOCK
    expert, gate = _route(logits[:t])
    s = t * MOE_TOPK
    nb = s // m + MOE_EXPERTS
    flat_e = expert.reshape(s)
    onehot = (flat_e[:, None] == jnp.arange(MOE_EXPERTS, dtype=i32)[None, :]).astype(i32)
    csum = jnp.cumsum(onehot, axis=0)
    rank = jnp.sum((csum - onehot) * onehot, axis=1)
    counts = csum[-1]
    padded = ((counts + m - 1) // m) * m
    pad_end = jnp.cumsum(padded)
    dest = (pad_end - padded)[flat_e] + rank
    n_used = (pad_end[-1] // m).astype(i32)
    blk = jnp.arange(nb, dtype=i32)
    block_e = jnp.minimum(jnp.searchsorted(pad_end, jnp.minimum(blk, n_used - 1) * m, side='right'),
                          MOE_EXPERTS - 1).astype(i32)
    tok = (jnp.arange(s, dtype=i32) // MOE_TOPK)
    buf_tok = jnp.zeros((nb * m,), i32).at[dest].set(tok)
    xs = jnp.take(h2, buf_tok, axis=0)
    ys = _moe_experts(xs, block_e, n_used.reshape(1), w1, w3, w2, nb)
    d2 = dest.reshape(t, MOE_TOPK)
    y0 = jnp.take(ys, d2[:, 0], axis=0)
    y1 = jnp.take(ys, d2[:, 1], axis=0)
    return _combine(xn, y0, y1, gate, mods, g_out, n_tiles=n_tiles, batch=batch, seq=seq, final_norm=final_norm)


HY_N2 = 128
HY_WB = 128


def _short_conv_kernel(x_ref, prev_ref, next_ref, w_ref, b_ref, v_ref, x1_ref, x2_ref, ext_scr, *, tiles_per_batch):
    j = pl.program_id(0) % tiles_per_batch
    tt = x_ref.shape[0]
    ext_scr[0:8, :] = jnp.where(j > 0, prev_ref[...], 0.0)
    ext_scr[8:8 + tt, :] = x_ref[...]
    ext_scr[8 + tt:16 + tt, :] = jnp.where(j < tiles_per_batch - 1, next_ref[...], 0.0)
    w = w_ref[...]
    width = v_ref.shape[1]
    for part, o_ref in enumerate((v_ref, x1_ref, x2_ref)):
        cols = slice(part * width, (part + 1) * width)
        o_ref[...] = (ext_scr[7:7 + tt, cols] * w[0:1, cols] + ext_scr[8:8 + tt, cols] * w[1:2, cols]
                      + ext_scr[9:9 + tt, cols] * w[2:3, cols] + b_ref[:, cols])


def _short_conv(hy, w, b, *, batch, seq):
    n_lat = batch * seq
    tt = 256
    width = hy.shape[1]
    out = jax.ShapeDtypeStruct((n_lat, width // 3), f32)
    o_spec = pl.BlockSpec((tt, width // 3), lambda i: (i, 0))
    return pl.pallas_call(
        functools.partial(_short_conv_kernel, tiles_per_batch=seq // tt),
        grid=(n_lat // tt,),
        in_specs=[pl.BlockSpec((tt, width), lambda i: (i, 0)),
                  pl.BlockSpec((8, width), lambda i: (jnp.maximum(i * (tt // 8) - 1, 0), 0)),
                  pl.BlockSpec((8, width), lambda i: ((i + 1) * (tt // 8), 0)),
                  pl.BlockSpec((3, width), lambda i: (0, 0)),
                  pl.BlockSpec((1, width), lambda i: (0, 0))],
        out_specs=[o_spec, o_spec, o_spec],
        out_shape=[out, out, out],
        scratch_shapes=[pltpu.VMEM((tt + 16, width), f32)],
        compiler_params=_cparams(("arbitrary",)),
        name="short_conv",
    )(hy, hy, hy, w, b.reshape(1, width))


def _hy_filter_kernel(z_ref, w1_ref, b1_ref, fr_ref, w2_ref, b2_ref, w3_ref, dl_ref, k_ref, nrm_ref):
    i = pl.program_id(0)
    z = z_ref[...]
    fr = fr_ref[...]
    h = jnp.sin(fr[0:1] * (jnp.dot(z.astype(bf16), w1_ref[...], preferred_element_type=f32) + b1_ref[...]))
    h = jnp.sin(fr[1:2] * (jnp.dot(h.astype(bf16), w2_ref[...], preferred_element_type=f32) + b2_ref[...]))
    kk = jnp.dot(h.astype(bf16), w3_ref[...], preferred_element_type=f32)
    width = dl_ref.shape[1]
    decay = jnp.exp(-z[:, 0:1] * jnp.abs(dl_ref[...]))
    tl = z.shape[0]
    row = i * tl + lax.broadcasted_iota(i32, (tl, 1), 0)
    tot = []
    for blk in range(kk.shape[1] // width):
        kb = kk[:, blk * width:(blk + 1) * width] * decay
        if blk % 2 == 1:
            kb = jnp.where(row == 0, 0.0, kb)
        k_ref[:, blk * width:(blk + 1) * width] = kb
        tot.append(jnp.sum(jnp.abs(kb), axis=0, keepdims=True))
    s = jnp.concatenate(tot, axis=1)

    @pl.when(i == 0)
    def _():
        nrm_ref[...] = s

    @pl.when(i > 0)
    def _():
        nrm_ref[...] = nrm_ref[...] + s


def _hy_filters(seq, w1, b1, freq, w2, b2, w3):
    t = jnp.linspace(0.0, 1.0, seq, dtype=f32)[:, None]
    bands = (HY_EMB - 1) // 2
    w = 2 * math.pi * jnp.arange(seq, dtype=f32)[:, None] / seq
    f = jnp.linspace(1e-4, bands - 1, bands, dtype=f32)[None, :]
    z = jnp.concatenate([t, jnp.cos(f * w), -jnp.sin(f * w), jnp.zeros((seq, 128 - HY_EMB), f32)], axis=-1)
    w1p = jnp.concatenate([w1, jnp.zeros((128 - HY_EMB, w1.shape[1]), f32)], axis=0).astype(bf16)
    hid = w1.shape[1]
    n_out = w3.shape[1]
    deltas = jnp.linspace(math.log(HY_TARGET) / HY_SLOW_DECAY, math.log(HY_TARGET) / HY_FAST_DECAY,
                          HY_WIDTH, dtype=f32).reshape(1, HY_WIDTH)
    tl = 512
    const = lambda a: pl.BlockSpec(a.shape, lambda i: (0,) * a.ndim)
    ins = [z, w1p, b1.reshape(1, hid), freq, w2.astype(bf16), b2.reshape(1, hid), w3.astype(bf16), deltas]
    return pl.pallas_call(
        _hy_filter_kernel,
        grid=(seq // tl,),
        in_specs=[pl.BlockSpec((tl, 128), lambda i: (i, 0))] + [const(a) for a in ins[1:]],
        out_specs=[pl.BlockSpec((tl, n_out), lambda i: (i, 0)), pl.BlockSpec((1, n_out), lambda i: (0, 0))],
        out_shape=[jax.ShapeDtypeStruct((seq, n_out), f32), jax.ShapeDtypeStruct((1, n_out), f32)],
        compiler_params=_cparams(("arbitrary",)),
        name="hy_filter",
    )(*ins)


def _dft_tables(seq):
    n2 = HY_N2
    n_tot = 2 * seq
    n1 = n_tot // n2
    nf = n1 // 2 + 1
    nfp = -(-nf // 8) * 8
    f1 = jnp.arange(nfp, dtype=i32)
    k1 = jnp.arange(n1 // 2, dtype=i32)
    th = (2 * math.pi / n1) * ((f1[:, None] * k1[None, :]) % n1).astype(f32)
    valid = (f1 < nf)[:, None]
    fwd = jnp.concatenate([jnp.where(valid, jnp.cos(th), 0.0), jnp.where(valid, -jnp.sin(th), 0.0)], axis=0)
    eye = jnp.eye(8, dtype=f32)
    fk = jnp.einsum('rk,jh->rjkh', fwd, eye).reshape(2 * nfp * 8, (n1 // 2) * 8)
    cw = jnp.where(valid, jnp.where((f1 == 0) | (f1 == n1 // 2), 1.0, 2.0)[:, None], 0.0) / n_tot
    inv = jnp.concatenate([(cw * jnp.cos(th)).T, (-cw * jnp.sin(th)).T], axis=1)
    ck = jnp.einsum('kr,jh->kjrh', inv, eye).reshape((n1 // 2) * 8, 2 * nfp * 8)
    fa = jnp.arange(nf, dtype=i32)[:, None, None]
    f2 = jnp.arange(n2, dtype=i32)[None, :, None]
    m2 = jnp.arange(n2, dtype=i32)[None, None, :]
    ph = (2 * math.pi / n_tot) * ((f2 * m2 * n1 + m2 * fa) % n_tot).astype(f32)
    gr, gi = jnp.cos(ph), -jnp.sin(ph)
    g = jnp.concatenate([jnp.concatenate([gr, -gi], axis=2), jnp.concatenate([gi, gr], axis=2)], axis=1)
    hr, hi = jnp.swapaxes(jnp.cos(ph), 1, 2), jnp.swapaxes(jnp.sin(ph), 1, 2)
    h = jnp.concatenate([jnp.concatenate([hr, -hi], axis=2), jnp.concatenate([hi, hr], axis=2)], axis=1)
    return fk.astype(bf16), ck.astype(bf16), g.astype(bf16), h.astype(bf16), nf, nfp


def _hy_rows_per_step(nf):
    return max(d for d in (5, 4, 3, 2, 1) if nf % d == 0)


def _dft_stage1(x_ref, fk_ref, a_scr):
    n1h, n2, wb = x_ref.shape

    def slab(s, carry):
        rows = pl.ds(pl.multiple_of(s * 8, 8), 8)
        xs = x_ref[:, rows, :].reshape(n1h * 8, wb).astype(bf16)
        a = jnp.dot(fk_ref[...], xs, preferred_element_type=f32)
        a_scr[:, rows, :] = a.reshape(a_scr.shape[0], 8, wb)
        return carry

    lax.fori_loop(0, n2 // 8, slab, 0)


def _dft_stage2(a_scr, g, f1, nfp):
    a = jnp.concatenate([a_scr[f1], a_scr[nfp + f1]], axis=0).astype(bf16)
    return jnp.dot(g, a, preferred_element_type=f32)


def _hy_spectrum_kernel(kf_ref, kb_ref, fk_ref, g_ref, o_ref, af_scr, ab_scr, *, nfp):
    fs = pl.program_id(2)

    @pl.when(fs == 0)
    def _():
        _dft_stage1(kf_ref, fk_ref, af_scr)
        _dft_stage1(kb_ref, fk_ref, ab_scr)

    fb = g_ref.shape[0]
    n2 = kf_ref.shape[1]
    for j in range(fb):
        f1 = fs * fb + j
        sf = _dft_stage2(af_scr, g_ref[j], f1, nfp)
        sb = _dft_stage2(ab_scr, g_ref[j], f1, nfp)
        o_ref[0, j] = sf[:n2] + sb[:n2]
        o_ref[1, j] = sf[n2:] - sb[n2:]


def _hy_spectrum(k, fk, g, nf, nfp, seq):
    n2, wb = HY_N2, HY_WB
    n1h = seq // n2
    fb = _hy_rows_per_step(nf)
    nwb = HY_WIDTH // wb
    k3 = k.reshape(n1h, n2, k.shape[1])
    return pl.pallas_call(
        functools.partial(_hy_spectrum_kernel, nfp=nfp),
        grid=(HY_ORDER, nwb, nf // fb),
        in_specs=[pl.BlockSpec((n1h, n2, wb), lambda o, w, f: (0, 0, (2 * o) * nwb + w)),
                  pl.BlockSpec((n1h, n2, wb), lambda o, w, f: (0, 0, (2 * o + 1) * nwb + w)),
                  pl.BlockSpec(fk.shape, lambda o, w, f: (0, 0)),
                  pl.BlockSpec((fb, 2 * n2, 2 * n2), lambda o, w, f: (f, 0, 0))],
        out_specs=pl.BlockSpec((None, 2, fb, n2, wb), lambda o, w, f: (o, 0, f, 0, w)),
        out_shape=jax.ShapeDtypeStruct((HY_ORDER, 2, nf, n2, HY_WIDTH), f32),
        scratch_shapes=[pltpu.VMEM((2 * nfp, n2, wb), f32), pltpu.VMEM((2 * nfp, n2, wb), f32)],
        compiler_params=_cparams(("arbitrary",) * 3),
        name="hy_spectrum",
    )(k3, k3, fk, g)


def _hy_conv_kernel(x_ref, gate_ref, kf_ref, g_ref, h_ref, fk_ref, ck_ref, bias_ref, inv_ref, o_ref, a_scr, *, nfp):
    fs = pl.program_id(2)
    n1h, n2, wb = x_ref.shape

    @pl.when(fs == 0)
    def _():
        _dft_stage1(x_ref, fk_ref, a_scr)

    fb = g_ref.shape[0]
    for j in range(fb):
        f1 = fs * fb + j
        xs = _dft_stage2(a_scr, g_ref[j], f1, nfp)
        xr, xi = xs[:n2], xs[n2:]
        kr, ki = kf_ref[0, j], kf_ref[1, j]
        y = jnp.concatenate([xr * kr - xi * ki, xr * ki + xi * kr], axis=0).astype(bf16)
        bb = jnp.dot(h_ref[j], y, preferred_element_type=f32)
        a_scr[f1] = bb[:n2]
        a_scr[nfp + f1] = bb[n2:]

    @pl.when(fs == pl.num_programs(2) - 1)
    def _():
        bias = bias_ref[...].reshape(1, 1, wb)
        inv = inv_ref[...].reshape(1, 1, wb)

        def slab(s, carry):
            rows = pl.ds(pl.multiple_of(s * 8, 8), 8)
            bs = a_scr[:, rows, :].reshape(2 * nfp * 8, wb).astype(bf16)
            y = jnp.dot(ck_ref[...], bs, preferred_element_type=f32).reshape(n1h, 8, wb)
            o_ref[:, rows, :] = (gate_ref[:, rows, :] * (y * inv + x_ref[:, rows, :] * bias)).astype(o_ref.dtype)
            return carry

        lax.fori_loop(0, n2 // 8, slab, 0)


def _hy_conv(x, gate, kf, order, tabs, bias, inv_norm, out_dtype, *, batch, seq):
    fk, ck, g, h, nf, nfp = tabs
    n2, wb = HY_N2, HY_WB
    n1h = seq // n2
    fb = _hy_rows_per_step(nf)
    width = x.shape[1]
    big = pl.BlockSpec((n1h, n2, wb), lambda b, w, f: (b, 0, w))
    out = pl.pallas_call(
        functools.partial(_hy_conv_kernel, nfp=nfp),
        grid=(batch, width // wb, nf // fb),
        in_specs=[big, big,
                  pl.BlockSpec((None, 2, fb, n2, wb), lambda b, w, f: (order, 0, f, 0, w)),
                  pl.BlockSpec((fb, 2 * n2, 2 * n2), lambda b, w, f: (f, 0, 0)),
                  pl.BlockSpec((fb, 2 * n2, 2 * n2), lambda b, w, f: (f, 0, 0)),
                  pl.BlockSpec(fk.shape, lambda b, w, f: (0, 0)),
                  pl.BlockSpec(ck.shape, lambda b, w, f: (0, 0)),
                  pl.BlockSpec((1, wb), lambda b, w, f: (0, w)),
                  pl.BlockSpec((1, wb), lambda b, w, f: (0, w))],
        out_specs=big,
        out_shape=jax.ShapeDtypeStruct((batch * n1h, n2, width), out_dtype),
        scratch_shapes=[pltpu.VMEM((2 * nfp, n2, wb), f32)],
        compiler_params=_cparams(("arbitrary",) * 3),
        name="hy_conv",
    )(x.reshape(batch * n1h, n2, width), gate.reshape(batch * n1h, n2, width), kf, g, h, fk, ck,
      bias.reshape(1, width), inv_norm.reshape(1, width))
    return out.reshape(batch * seq, width)


def _hyena(hy, conv_w, conv_b, f_w1, f_b1, f_freq, f_w2, f_b2, f_w3, hy_bias, *, batch, seq):
    v, x1, x2 = _short_conv(hy, conv_w, conv_b, batch=batch, seq=seq)
    k, sums = _hy_filters(seq, f_w1, f_b1, f_freq, f_w2, f_b2, f_w3)
    sums = sums.reshape(HY_ORDER, 2, HY_WIDTH)
    inv_norm = 1.0 / (sums[:, 0] + sums[:, 1])
    tabs = _dft_tables(seq)
    kf = _hy_spectrum(k, tabs[0], tabs[2], tabs[4], tabs[5], seq)
    z = _hy_conv(v, x1, kf, 0, tabs, hy_bias[0], inv_norm[0], f32, batch=batch, seq=seq)
    return _hy_conv(z, x2, kf, 1, tabs, hy_bias[1], inv_norm[1], bf16, batch=batch, seq=seq)


S5_TT = 256
S5_GB = 8


def _gelu_tanh(x):
    return 0.5 * x * (1.0 + jnp.tanh(0.7978845608028654 * (x + 0.044715 * x * x * x)))


def _s5_kernel(u_ref, perm_ref, permt_ref, wb_ref, wc_ref, are_ref, aim_ref, pre_ref, pim_ref, *rest, fuse_glu):
    if fuse_glu:
        yf_ref, d_ref, gw_ref, gb_ref, o_ref, st_scr, bu_scr, y_scr = rest
    else:
        o_ref, st_scr, bu_scr, y_scr = rest
    tt = u_ref.shape[0]
    steps = tt // 8
    half = bu_scr.shape[1] // 2

    @pl.when(pl.program_id(1) == 0)
    def _():
        st_scr[...] = jnp.zeros_like(st_scr)

    up = jnp.dot(perm_ref[...], u_ref[...].astype(bf16), preferred_element_type=f32).astype(bf16)
    for gb in range(wb_ref.shape[0]):
        bu_scr[...] = jnp.dot(up[:, gb * 128:(gb + 1) * 128], wb_ref[gb], preferred_element_type=f32)
        a_re = jnp.broadcast_to(are_ref[gb], (8, half))
        a_im = jnp.broadcast_to(aim_ref[gb], (8, half))

        def scan_step(k, h):
            hr, hi = h
            rows = pl.ds(pl.multiple_of(k * 8, 8), 8)
            nr = a_re * hr - a_im * hi + bu_scr[rows, 0:half]
            ni = a_re * hi + a_im * hr + bu_scr[rows, half:]
            bu_scr[rows, 0:half] = nr
            bu_scr[rows, half:] = ni
            return nr, ni

        zero = jnp.zeros((8, half), f32)
        hr, hi = lax.fori_loop(0, steps, scan_step, (zero, zero), unroll=4)

        pw_r = pre_ref[gb, steps - 1:steps, :]
        pw_i = pim_ref[gb, steps - 1:steps, :]
        cur_r = st_scr[gb, :, 0:half]
        cur_i = st_scr[gb, :, half:]
        car_r, car_i = [], []
        for j in range(8):
            car_r.append(cur_r)
            car_i.append(cur_i)
            cur_r, cur_i = (hr[j:j + 1] + pw_r * cur_r - pw_i * cur_i,
                            hi[j:j + 1] + pw_r * cur_i + pw_i * cur_r)
        st_scr[gb, :, 0:half] = cur_r
        st_scr[gb, :, half:] = cur_i
        cr = jnp.concatenate(car_r, axis=0)
        ci = jnp.concatenate(car_i, axis=0)

        def fix_step(k, carry):
            rows = pl.ds(pl.multiple_of(k * 8, 8), 8)
            pr = pre_ref[gb, pl.ds(k, 1), :]
            pi = pim_ref[gb, pl.ds(k, 1), :]
            bu_scr[rows, 0:half] = bu_scr[rows, 0:half] + (pr * cr - pi * ci)
            bu_scr[rows, half:] = bu_scr[rows, half:] + (pr * ci + pi * cr)
            return carry

        lax.fori_loop(0, steps, fix_step, 0, unroll=4)
        y_scr[:, gb * 128:(gb + 1) * 128] = jnp.dot(bu_scr[...].astype(bf16), wc_ref[gb],
                                                      preferred_element_type=f32)
    y = jnp.dot(permt_ref[...], y_scr[...], preferred_element_type=f32, precision=lax.Precision.HIGHEST)
    if fuse_glu:
        g = _gelu_tanh(y + yf_ref[...] + u_ref[...] * d_ref[...])
        z = jnp.dot(g.astype(bf16), gw_ref[...], preferred_element_type=f32) + gb_ref[...]
        o_ref[...] = (g * jax.nn.sigmoid(z)).astype(o_ref.dtype)
    else:
        o_ref[...] = y


def _s5_tables(a_re, a_im, log_dt, b_re, b_im, c_re, c_im, steps):
    ng, ns = a_re.shape
    nb = ng // S5_GB
    lam = lax.complex(a_re, a_im)
    dt = jnp.exp(log_dt)[:, None]
    lam_bar = jnp.exp(lam * dt)
    b_bar = ((lam_bar - 1) / lam)[..., None] * lax.complex(b_re, b_im)
    cm = lax.complex(c_re, c_im)
    pw = jnp.exp((lam * dt)[None] * jnp.arange(1, steps + 1, dtype=f32)[:, None, None])
    eye = jnp.eye(S5_GB, dtype=f32)
    bb = b_bar.reshape(nb, S5_GB, ns, S5_GROUP)
    wb = jnp.concatenate([jnp.einsum('bgpc,gh->bgchp', part, eye).reshape(nb, S5_GB * S5_GROUP, S5_GB * ns)
                          for part in (jnp.real(bb), jnp.imag(bb))], axis=-1)
    cc = cm.reshape(nb, S5_GB, S5_GROUP, ns)
    wc = jnp.concatenate([jnp.einsum('bgcp,gh->bgphc', part, eye).reshape(nb, S5_GB * ns, S5_GB * S5_GROUP)
                          for part in (jnp.real(cc), -jnp.imag(cc))], axis=1)
    lb = lam_bar.reshape(nb, 1, S5_GB * ns)
    pwb = pw.reshape(steps, nb, S5_GB * ns).transpose(1, 0, 2)
    return (wb.astype(bf16), wc.astype(bf16), jnp.real(lb), jnp.imag(lb), jnp.real(pwb), jnp.imag(pwb))


def _s5_perm(tt, reverse):
    steps = tt // 8
    dst = jnp.arange(tt)
    t = (dst % 8) * steps + dst // 8
    src = tt - 1 - t if reverse else t
    return jnp.arange(tt)[None, :] == src[:, None]


def _s5(u, a_re, a_im, log_dt, b_re, b_im, c_re, c_im, s5_d, glu_w, glu_b, *, batch, seq, ctx_len):
    r, width = u.shape
    tt = S5_TT
    assert ctx_len == tt and seq % tt == 0
    n_lat = seq // tt
    ctx_blk0 = (batch * seq) // tt
    y = None
    for di in range(2):
        rev = di == 1
        wb, wc, lr, li, pr, pi = _s5_tables(a_re[di], a_im[di], log_dt[di], b_re[di], b_im[di],
                                            c_re[di], c_im[di], tt // 8)
        perm = _s5_perm(tt, rev)

        def tile_map(b, i, rev=rev):
            lat = b * n_lat + (n_lat - i if rev else i - 1)
            return (jnp.where(i == 0, ctx_blk0 + b, lat), 0)

        row_spec = pl.BlockSpec((tt, width), tile_map)
        const = lambda a: pl.BlockSpec(a.shape, lambda b, i: (0,) * a.ndim)
        ins = [u, perm.astype(bf16), perm.T.astype(f32), wb, wc, lr, li, pr, pi]
        in_specs = [row_spec] + [const(a) for a in ins[1:]]
        if rev:
            extra = [y, s5_d.reshape(1, width), glu_w.astype(bf16), glu_b.reshape(1, width)]
            in_specs += [row_spec] + [const(a) for a in extra[1:]]
            ins += extra
        y = pl.pallas_call(
            functools.partial(_s5_kernel, fuse_glu=rev),
            grid=(batch, n_lat + 1),
            in_specs=in_specs,
            out_specs=row_spec,
            out_shape=jax.ShapeDtypeStruct((r, width), bf16 if rev else f32),
            scratch_shapes=[pltpu.VMEM((wb.shape[0], 1, wb.shape[2]), f32),
                            pltpu.VMEM((tt, wb.shape[2]), f32),
                            pltpu.VMEM((tt, width), f32)],
            compiler_params=_cparams(("arbitrary", "arbitrary")),
            name="s5_bwd_glu" if rev else "s5_fwd",
        )(*ins)
    return y


def _rope_tables(seq):
    rows = seq // GRID_W
    row = jnp.repeat(jnp.arange(rows, dtype=f32), GRID_W)
    col = jnp.tile(jnp.arange(GRID_W, dtype=f32), rows)
    n_freq = DIFF_QK // 4
    inv = ROPE_BASE ** (-jnp.arange(n_freq, dtype=f32) / n_freq)
    ar, ac = row[:, None] * inv, col[:, None] * inv
    cos = jnp.concatenate([jnp.cos(ar), jnp.cos(ar), jnp.cos(ac), jnp.cos(ac)], axis=-1)
    sin = jnp.concatenate([-jnp.sin(ar), jnp.sin(ar), -jnp.sin(ac), jnp.sin(ac)], axis=-1)
    cos = jnp.concatenate([jnp.tile(cos, (1, 2)), jnp.ones((TM, 128), f32)], axis=0)
    sin = jnp.concatenate([jnp.tile(sin, (1, 2)), jnp.zeros((TM, 128), f32)], axis=0)
    return cos, sin


def kernel(x, c, ctx, c_ctx, ada_w, ada_b, norm_mix_g, norm_ffn_g, norm_out_g, ab_w_in, ab_w_out, diff_lam, diff_subln_g, pool_w, pool_scale, cd_w_in, cd_w_out, hy_conv_w, hy_conv_b, hy_f_w1, hy_f_b1, hy_f_freq, hy_f_w2, hy_f_b2, hy_f_w3, hy_bias, s5_a_re, s5_a_im, s5_log_dt, s5_b_re, s5_b_im, s5_c_re, s5_c_im, s5_d, glu_w, glu_b, moe_w_rg, moe_b_rg, moe_w_re, moe_b_re, moe_w1, moe_w3, moe_w2):
    bsz, seq, d = x.shape
    cl = ctx.shape[1]
    depth = ada_w.shape[0]
    n_lat = bsz * seq
    assert seq % TM == 0 and (bsz * cl) % TM == 0 and bsz + 1 <= 8
    xs = jnp.concatenate([x.reshape(n_lat, d), ctx.reshape(bsz * cl, d)], axis=0)
    r = xs.shape[0]
    cond = jnp.zeros((8, d), f32).at[0:bsz].set(c).at[bsz].set(c_ctx)
    mods = _ada_mods(cond, ada_w, ada_b)
    geom = dict(batch=bsz, seq=seq)

    for l in range(depth):
        last = l == depth - 1
        i = l // 2
        n_tiles = (n_lat if last else r) // TM
        g_mix = norm_mix_g[l].reshape(1, d)
        if l % 2 == 0:
            lam_init = 0.8 - 0.6 * math.exp(-0.3 * l)
            cos, sin = _rope_tables(seq)
            q, k, v, p = _proj_rope(xs, g_mix, mods[l], ab_w_in[i].astype(bf16), cos, sin, **geom)
            mix_a = _diff_attention(q, k, v, diff_lam[i], diff_subln_g[i], lam_init, ctx_len=cl, **geom)
            mix_b = _pool(p, pool_w[i], pool_scale[i], ctx_len=cl, **geom)
            w_out = ab_w_out[i]
        else:
            hw = (HY_ORDER + 1) * HY_WIDTH
            hy, u = _proj(xs, g_mix, mods[l], cd_w_in[i].astype(bf16), hw, **geom)
            mix_a = _hyena(hy, hy_conv_w[i], hy_conv_b[i], hy_f_w1[i], hy_f_b1[i], hy_f_freq[i],
                           hy_f_w2[i], hy_f_b2[i], hy_f_w3[i], hy_bias[i], **geom)
            mix_b = _s5(u, s5_a_re[i], s5_a_im[i], s5_log_dt[i], s5_b_re[i], s5_b_im[i],
                        s5_c_re[i], s5_c_im[i], s5_d[i], glu_w[i], glu_b[i], ctx_len=cl, **geom)
            w_out = cd_w_out[i]
        w_r = jnp.zeros((d, 128), f32).at[:, :MOE_GROUPS].set(moe_w_rg[l]) \
            .at[:, MOE_GROUPS:MOE_GROUPS + MOE_EXPERTS].set(moe_w_re[l])
        b_r = jnp.zeros((1, 128), f32).at[0, :MOE_GROUPS].set(moe_b_rg[l]) \
            .at[0, MOE_GROUPS:MOE_GROUPS + MOE_EXPERTS].set(moe_b_re[l])
        xn, h2, logits = _outproj(mix_a, mix_b, xs, w_out.astype(bf16), mods[l], norm_ffn_g[l].reshape(1, d),
                                  w_r, b_r, n_tiles=n_tiles, **geom)
        xs = _moe(xn, h2, logits, mods[l], moe_w1[l], moe_w3[l], moe_w2[l], norm_out_g.reshape(1, d),
                  n_tiles=n_tiles, final_norm=last, **geom)
    return xs[:n_lat].reshape(bsz, seq, d)
```

```python
import functools
import math

import jax
import jax.numpy as jnp
from jax import lax
from jax.experimental import pallas as pl
from jax.experimental.pallas import tpu as pltpu

f32, bf16, i32 = jnp.float32, jnp.bfloat16, jnp.int32

EPS = 1e-6
GRID_W = 64
DIFF_QK = 64
DIFF_V = 128
ROPE_BASE = 10000.0
POOL_WINDOWS = (2, 4, 8, 16)
POOL_GROUP = 256
HY_WIDTH = 1024
HY_ORDER = 2
HY_EMB = 33
HY_FAST_DECAY = 0.3
HY_SLOW_DECAY = 1.5
HY_TARGET = 1e-2
S5_GROUP = 16
S5_GROUPS = 64
S5_STATE = 64
MOE_GROUPS = 8
MOE_PER_GROUP = 8
MOE_EXPERTS = 64
MOE_TOPK = 2
MOE_BLOCK = 256

VMEM_LIMIT = 56 * 1024 * 1024
TM = 512
LOG2E = 1.4426950408889634


def _cparams(sem):
    return pltpu.CompilerParams(dimension_semantics=sem, vmem_limit_bytes=VMEM_LIMIT)


def _norm_mod(x, g, shift, scale):
    ms = jnp.mean(x * x, axis=-1, keepdims=True)
    return (x * lax.rsqrt(ms + EPS) * g) * (1.0 + scale) + shift


def _tile_segment(i, lat_tiles_per_batch, batch):
    seg = jnp.zeros((), i32)
    for b in range(1, batch + 1):
        seg = seg + (i >= b * lat_tiles_per_batch).astype(i32)
    return seg


def _ada_kernel(s_ref, w_ref, b_ref, o_ref):
    s = s_ref[...]
    a = (s * jax.nn.sigmoid(s)).astype(bf16)
    o_ref[...] = jnp.dot(a, w_ref[...].astype(bf16), preferred_element_type=f32) + b_ref[...]


def _ada_mods(s, ada_w, ada_b):
    depth, d, n = ada_w.shape
    tn = 1024
    return pl.pallas_call(
        _ada_kernel,
        grid=(depth, n // tn),
        in_specs=[pl.BlockSpec((8, d), lambda l, j: (0, 0)),
                  pl.BlockSpec((None, d, tn), lambda l, j: (l, 0, j)),
                  pl.BlockSpec((None, 1, tn), lambda l, j: (l, 0, j))],
        out_specs=pl.BlockSpec((None, 8, tn), lambda l, j: (l, 0, j)),
        out_shape=jax.ShapeDtypeStruct((depth, 8, n), f32),
        compiler_params=_cparams(("arbitrary", "arbitrary")),
        name="ada_mods",
    )(s, ada_w, ada_b.reshape(depth, 1, n))


def _proj_rope_kernel(x_ref, g_ref, sh_ref, sc_ref, w_ref, cos_ref, sin_ref,
                      q_ref, k_ref, v_ref, p_ref, h_scr, *, tiles_per_batch, batch, qscale):
    seg = _tile_segment(pl.program_id(0), tiles_per_batch, batch)
    h = _norm_mod(x_ref[...], g_ref[...], sh_ref[pl.ds(seg, 1), :], sc_ref[pl.ds(seg, 1), :])
    h_scr[...] = h.astype(bf16)
    cos = cos_ref[...]
    sin = sin_ref[...]
    tm = cos.shape[0]
    lane = lax.broadcasted_iota(i32, (tm, 128), 1)
    first = (lane % 32) < 16

    def rope(r):
        sw = jnp.where(first, pltpu.roll(r, 112, 1), pltpu.roll(r, 16, 1))
        return r * cos + sw * sin

    heads = q_ref.shape[0]
    nc = 512
    for c in range(w_ref.shape[1] // nc):
        res = jnp.dot(h_scr[...], w_ref[:, c * nc:(c + 1) * nc], preferred_element_type=f32)
        for s in range(nc // 128):
            r = res[:, s * 128:(s + 1) * 128]
            col = (c * nc) // 128 + s
            if col < heads:
                q_ref[col] = (rope(r) * qscale).astype(bf16)
            elif col < 2 * heads:
                k_ref[col - heads] = rope(r).astype(bf16)
            elif col < 3 * heads:
                v_ref[col - 2 * heads] = r.astype(bf16)
            else:
                c0 = (col - 3 * heads) * 128
                p_ref[:, c0:c0 + 128] = r


def _proj_rope(x, g, mods, w, cos, sin, *, batch, seq):
    r, d = x.shape
    heads = 8
    lat_tiles = seq // TM
    n_tiles = r // TM

    def tab_map(i):
        return (jnp.where(i < batch * lat_tiles, i % lat_tiles, lat_tiles), 0)

    hm = jax.ShapeDtypeStruct((heads, r, 128), bf16)
    hm_spec = pl.BlockSpec((heads, TM, 128), lambda i: (0, i, 0))
    return pl.pallas_call(
        functools.partial(_proj_rope_kernel, tiles_per_batch=lat_tiles, batch=batch,
                          qscale=LOG2E * DIFF_QK ** -0.5),
        grid=(n_tiles,),
        in_specs=[pl.BlockSpec((TM, d), lambda i: (i, 0)),
                  pl.BlockSpec((1, d), lambda i: (0, 0)),
                  pl.BlockSpec((8, d), lambda i: (0, 0)),
                  pl.BlockSpec((8, d), lambda i: (0, 1)),
                  pl.BlockSpec(w.shape, lambda i: (0, 0), pipeline_mode=pl.Buffered(1)),
                  pl.BlockSpec((TM, 128), tab_map),
                  pl.BlockSpec((TM, 128), tab_map)],
        out_specs=[hm_spec, hm_spec, hm_spec, pl.BlockSpec((TM, 1024), lambda i: (i, 0))],
        out_shape=[hm, hm, hm, jax.ShapeDtypeStruct((r, 1024), f32)],
        scratch_shapes=[pltpu.VMEM((TM, d), bf16)],
        compiler_params=_cparams(("arbitrary",)),
        name="proj_rope",
    )(x, g, mods, mods, w, cos, sin)


def _proj_kernel(x_ref, g_ref, sh_ref, sc_ref, w_ref, hy_ref, u_ref, h_scr, *, tiles_per_batch, batch):
    seg = _tile_segment(pl.program_id(0), tiles_per_batch, batch)
    h = _norm_mod(x_ref[...], g_ref[...], sh_ref[pl.ds(seg, 1), :], sc_ref[pl.ds(seg, 1), :])
    h_scr[...] = h.astype(bf16)
    nc = 512
    n_hy = hy_ref.shape[1]
    for c in range(w_ref.shape[1] // nc):
        res = jnp.dot(h_scr[...], w_ref[:, c * nc:(c + 1) * nc], preferred_element_type=f32)
        if c * nc < n_hy:
            hy_ref[:, c * nc:(c + 1) * nc] = res
        else:
            u_ref[:, c * nc - n_hy:(c + 1) * nc - n_hy] = res


def _proj(x, g, mods, w, n_hy, *, batch, seq):
    r, d = x.shape
    n = w.shape[1]
    return pl.pallas_call(
        functools.partial(_proj_kernel, tiles_per_batch=seq // TM, batch=batch),
        grid=(r // TM,),
        in_specs=[pl.BlockSpec((TM, d), lambda i: (i, 0)),
                  pl.BlockSpec((1, d), lambda i: (0, 0)),
                  pl.BlockSpec((8, d), lambda i: (0, 0)),
                  pl.BlockSpec((8, d), lambda i: (0, 1)),
                  pl.BlockSpec(w.shape, lambda i: (0, 0), pipeline_mode=pl.Buffered(1))],
        out_specs=[pl.BlockSpec((TM, n_hy), lambda i: (i, 0)), pl.BlockSpec((TM, n - n_hy), lambda i: (i, 0))],
        out_shape=[jax.ShapeDtypeStruct((r, n_hy), f32), jax.ShapeDtypeStruct((r, n - n_hy), f32)],
        scratch_shapes=[pltpu.VMEM((TM, d), bf16)],
        compiler_params=_cparams(("arbitrary",)),
        name="proj",
    )(x, g, mods, mods, w)


def _attn_kernel(lv_ref, g_ref, q_ref, kc_ref, vc_ref, *rest, n_lat, tk, lam_init):
    if n_lat:
        kl_ref, vl_ref, o_ref = rest
    else:
        (o_ref,) = rest
    lv = lv_ref[...]
    lam = (jnp.exp(jnp.sum(lv[0:1] * lv[1:2], axis=-1, keepdims=True))
           - jnp.exp(jnp.sum(lv[2:3] * lv[3:4], axis=-1, keepdims=True)) + lam_init)
    q = q_ref[...]
    tq = q.shape[0]
    lane = lax.broadcasted_iota(i32, q.shape, 1)
    zero = jnp.zeros_like(q)
    qs = (jnp.where(lane < DIFF_QK, q, zero), jnp.where(lane >= DIFF_QK, q, zero))
    nt = (((1,), (1,)), ((), ()))

    def update(carry, k, v):
        new = []
        for m in range(2):
            mx, l, acc = carry[3 * m:3 * m + 3]
            s = lax.dot_general(qs[m], k, nt, preferred_element_type=f32)
            mn = jnp.maximum(mx, jnp.max(s, axis=-1, keepdims=True))
            alpha = jnp.exp2(mx - mn)
            p = jnp.exp2(s - mn)
            l = alpha * l + jnp.sum(p, axis=-1, keepdims=True)
            acc = alpha * acc + jnp.dot(p.astype(bf16), v, preferred_element_type=f32)
            new += [mn, l, acc]
        return tuple(new)

    init = (jnp.full((tq, 1), -jnp.inf, f32), jnp.zeros((tq, 1), f32), jnp.zeros((tq, DIFF_V), f32)) * 2
    carry = update(init, kc_ref[...], vc_ref[...])
    if n_lat:
        def body(j, carry):
            off = pl.multiple_of(j * tk, tk)
            return update(carry, kl_ref[pl.ds(off, tk), :], vl_ref[pl.ds(off, tk), :])
        carry = lax.fori_loop(0, n_lat, body, carry)
    _, l0, a0, _, l1, a1 = carry
    o = a0 / l0 - lam * (a1 / l1)
    ms = jnp.mean(o * o, axis=-1, keepdims=True)
    o_ref[...] = ((o * lax.rsqrt(ms + EPS) * g_ref[...]) * (1.0 - lam_init)).astype(o_ref.dtype)


def _diff_attention(q, k, v, lam_vecs, subln_g, lam_init, *, batch, seq, ctx_len):
    heads, r, _ = q.shape
    tq, tk = 512, 512
    ctx_blk0 = (batch * seq) // ctx_len
    small = [pl.BlockSpec((4, DIFF_QK), lambda *_: (0, 0)), pl.BlockSpec((1, DIFF_V), lambda *_: (0, 0))]
    kc_spec = pl.BlockSpec((None, ctx_len, 128), lambda b, h, *_: (h, ctx_blk0 + b, 0))
    g2 = subln_g.reshape(1, DIFF_V)
    lat = pl.pallas_call(
        functools.partial(_attn_kernel, n_lat=seq // tk, tk=tk, lam_init=lam_init),
        grid=(batch, heads, seq // tq),
        in_specs=small + [pl.BlockSpec((None, tq, 128), lambda b, h, i: (h, b * (seq // tq) + i, 0)),
                          kc_spec, kc_spec,
                          pl.BlockSpec((None, seq, 128), lambda b, h, i: (h, b, 0)),
                          pl.BlockSpec((None, seq, 128), lambda b, h, i: (h, b, 0))],
        out_specs=pl.BlockSpec((tq, 128), lambda b, h, i: (b * (seq // tq) + i, h)),
        out_shape=jax.ShapeDtypeStruct((batch * seq, heads * DIFF_V), bf16),
        compiler_params=_cparams(("arbitrary",) * 3),
        name="diff_attn_lat",
    )(lam_vecs, g2, q, k, v, k, v)
    ctx = pl.pallas_call(
        functools.partial(_attn_kernel, n_lat=0, tk=tk, lam_init=lam_init),
        grid=(batch, heads),
        in_specs=small + [kc_spec, kc_spec, kc_spec],
        out_specs=pl.BlockSpec((ctx_len, 128), lambda b, h: (b, h)),
        out_shape=jax.ShapeDtypeStruct((batch * ctx_len, heads * DIFF_V), bf16),
        compiler_params=_cparams(("arbitrary",) * 2),
        name="diff_attn_ctx",
    )(lam_vecs, g2, q, k, v)
    return jnp.concatenate([lat, ctx], axis=0)


def _pool_kernel(p_ref, prev_ref, next_ref, w_ref, scale_ref, o_ref, ext_scr, *,
                 tiles_per_batch, batch, seq, ctx_len):
    i = pl.program_id(0)
    tp = p_ref.shape[0]
    is_ctx = i >= batch * tiles_per_batch
    j = jnp.where(is_ctx, 0, i % tiles_per_batch)
    seg_len = jnp.where(is_ctx, ctx_len, seq)
    last = jnp.where(is_ctx, 0, tiles_per_batch - 1)
    ext_scr[0:8, :] = jnp.where(j > 0, prev_ref[...], 0.0)
    ext_scr[8:8 + tp, :] = p_ref[...]
    ext_scr[8 + tp:16 + tp, :] = jnp.where(j < last, next_ref[...], 0.0)
    pos = j * tp + lax.broadcasted_iota(i32, (tp, 1), 0)
    for gi, w in enumerate(POOL_WINDOWS):
        c0 = gi * POOL_GROUP
        half = w // 2
        tot = ext_scr[8 - half:8 - half + tp, c0:c0 + POOL_GROUP]
        for dlt in range(1 - half, half):
            tot = tot + ext_scr[8 + dlt:8 + dlt + tp, c0:c0 + POOL_GROUP]
        cnt = (jnp.minimum(pos - half + w, seg_len) - jnp.maximum(pos - half, 0)).astype(f32)
        dlt_mean = (tot / cnt - ext_scr[8:8 + tp, c0:c0 + POOL_GROUP]).astype(bf16)
        y = jnp.dot(dlt_mean, w_ref[gi].astype(bf16), preferred_element_type=f32)
        o_ref[:, c0:c0 + POOL_GROUP] = (y * scale_ref[:, c0:c0 + POOL_GROUP]).astype(o_ref.dtype)


def _pool(p, pool_w, pool_scale, *, batch, seq, ctx_len):
    r, width = p.shape
    tp = ctx_len
    tpb = seq // tp
    return pl.pallas_call(
        functools.partial(_pool_kernel, tiles_per_batch=tpb, batch=batch, seq=seq, ctx_len=ctx_len),
        grid=(r // tp,),
        in_specs=[pl.BlockSpec((tp, width), lambda i: (i, 0)),
                  pl.BlockSpec((8, width), lambda i: (jnp.maximum(i * (tp // 8) - 1, 0), 0)),
                  pl.BlockSpec((8, width), lambda i: (jnp.minimum((i + 1) * (tp // 8), r // 8 - 1), 0)),
                  pl.BlockSpec(pool_w.shape, lambda i: (0, 0, 0)),
                  pl.BlockSpec((1, width), lambda i: (0, 0))],
        out_specs=pl.BlockSpec((tp, width), lambda i: (i, 0)),
        out_shape=jax.ShapeDtypeStruct((r, width), bf16),
        scratch_shapes=[pltpu.VMEM((tp + 16, width), f32)],
        compiler_params=_cparams(("arbitrary",)),
        name="pool",
    )(p, p, p, pool_w, pool_scale.reshape(1, width))


def _outproj_kernel(a_ref, b_ref, x_ref, w_ref, g1_ref, sh_ref, sc_ref, gn_ref, wr_ref, br_ref,
                    xn_ref, h2_ref, lg_ref, *, tiles_per_batch, batch):
    seg = _tile_segment(pl.program_id(0), tiles_per_batch, batch)
    half = a_ref.shape[1]
    y = (jnp.dot(a_ref[...], w_ref[0:half, :], preferred_element_type=f32)
         + jnp.dot(b_ref[...], w_ref[half:, :], preferred_element_type=f32))
    xn = x_ref[...] + g1_ref[pl.ds(seg, 1), :] * y
    xn_ref[...] = xn
    h2 = _norm_mod(xn, gn_ref[...], sh_ref[pl.ds(seg, 1), :], sc_ref[pl.ds(seg, 1), :])
    h2_ref[...] = h2.astype(bf16)
    lg_ref[...] = jnp.dot(h2, wr_ref[...], preferred_element_type=f32,
                          precision=lax.Precision.HIGHEST) + br_ref[...]


def _outproj(a, b, x, w, mods, g_ffn, w_r, b_r, *, n_tiles, batch, seq):
    r, d = n_tiles * TM, x.shape[1]
    half = a.shape[1]
    row = lambda i: (i, 0)
    return pl.pallas_call(
        functools.partial(_outproj_kernel, tiles_per_batch=seq // TM, batch=batch),
        grid=(n_tiles,),
        in_specs=[pl.BlockSpec((TM, half), row), pl.BlockSpec((TM, half), row), pl.BlockSpec((TM, d), row),
                  pl.BlockSpec(w.shape, lambda i: (0, 0), pipeline_mode=pl.Buffered(1)),
                  pl.BlockSpec((8, d), lambda i: (0, 2)),
                  pl.BlockSpec((8, d), lambda i: (0, 3)),
                  pl.BlockSpec((8, d), lambda i: (0, 4)),
                  pl.BlockSpec((1, d), lambda i: (0, 0)),
                  pl.BlockSpec(w_r.shape, lambda i: (0, 0)),
                  pl.BlockSpec((1, 128), lambda i: (0, 0))],
        out_specs=[pl.BlockSpec((TM, d), row), pl.BlockSpec((TM, d), row), pl.BlockSpec((TM, 128), row)],
        out_shape=[jax.ShapeDtypeStruct((r, d), f32), jax.ShapeDtypeStruct((r, d), bf16),
                   jax.ShapeDtypeStruct((r, 128), f32)],
        compiler_params=_cparams(("arbitrary",)),
        name="outproj",
    )(a, b, x, w, mods, mods, mods, g_ffn, w_r, b_r)


def _moe_kernel(be_ref, nu_ref, xs_ref, w1_ref, w3_ref, w2_ref, ys_ref, w1_scr, w3_scr, w2_scr):
    i = pl.program_id(0)
    e = be_ref[i]
    prev = be_ref[jnp.maximum(i - 1, 0)]

    @pl.when((i == 0) | (e != prev))
    def _():
        w1_scr[...] = w1_ref[...].astype(bf16)
        w3_scr[...] = w3_ref[...].astype(bf16)
        w2_scr[...] = w2_ref[...].astype(bf16)

    @pl.when(i < nu_ref[0])
    def _():
        xb = xs_ref[...]
        a = jnp.dot(xb, w1_scr[...], preferred_element_type=f32)
        b = jnp.dot(xb, w3_scr[...], preferred_element_type=f32)
        hid = (a * jax.nn.sigmoid(a) * b).astype(bf16)
        ys_ref[...] = jnp.dot(hid, w2_scr[...], preferred_element_type=f32)

    @pl.when(i >= nu_ref[0])
    def _():
        ys_ref[...] = jnp.zeros_like(ys_ref)


def _moe_experts(xs, block_e, n_used, w1, w3, w2, nb):
    m = MOE_BLOCK
    _, d, de = w1.shape
    grid_spec = pltpu.PrefetchScalarGridSpec(
        num_scalar_prefetch=2,
        grid=(nb,),
        in_specs=[pl.BlockSpec((m, d), lambda i, be, nu: (jnp.minimum(i, nu[0] - 1), 0)),
                  pl.BlockSpec((None, d, de), lambda i, be, nu: (be[i], 0, 0)),
                  pl.BlockSpec((None, d, de), lambda i, be, nu: (be[i], 0, 0)),
                  pl.BlockSpec((None, de, d), lambda i, be, nu: (be[i], 0, 0))],
        out_specs=pl.BlockSpec((m, d), lambda i, be, nu: (i, 0)),
        scratch_shapes=[pltpu.VMEM((d, de), bf16), pltpu.VMEM((d, de), bf16), pltpu.VMEM((de, d), bf16)],
    )
    return pl.pallas_call(
        _moe_kernel,
        grid_spec=grid_spec,
        out_shape=jax.ShapeDtypeStruct((nb * m, d), f32),
        compiler_params=_cparams(("arbitrary",)),
        name="moe_experts",
    )(block_e, n_used, xs, w1, w3, w2)


def _route_kernel(lg_ref, tri_ref, o_ref, cnt_ref, run_scr):
    @pl.when(pl.program_id(0) == 0)
    def _():
        run_scr[...] = jnp.zeros_like(run_scr)

    lg = lg_ref[...]
    lane = lax.broadcasted_iota(i32, lg.shape, 1)
    ninf = -jnp.inf

    def first_lane(mask):
        return jnp.min(jnp.where(mask, lane, lg.shape[1]), axis=-1, keepdims=True)

    gl = jnp.where(lane < MOE_GROUPS, lg, ninf)
    gmax = jnp.max(gl, axis=-1, keepdims=True)
    pg_top = 1.0 / jnp.sum(jnp.exp(gl - gmax), axis=-1, keepdims=True)
    lo = MOE_GROUPS + first_lane(gl == gmax) * MOE_PER_GROUP
    sel = (lane >= lo) & (lane < lo + MOE_PER_GROUP)
    el = jnp.where(sel, lg, ninf)
    ee = jnp.exp(el - jnp.max(el, axis=-1, keepdims=True))
    pe = jnp.where(sel, ee / jnp.sum(ee, axis=-1, keepdims=True), -1.0)
    p1 = jnp.max(pe, axis=-1, keepdims=True)
    i1 = first_lane(pe == p1)
    pe2 = jnp.where(lane == i1, -1.0, pe)
    p2 = jnp.max(pe2, axis=-1, keepdims=True)
    i2 = first_lane(pe2 == p2)
    den = p1 + p2
    e1 = i1 - MOE_GROUPS
    e2 = i2 - MOE_GROUPS
    hit1 = lane == e1
    hit2 = lane == e2
    onehot = jnp.where(hit1, 1.0, jnp.where(hit2, 1.0, 0.0))
    base = jnp.dot(tri_ref[...], onehot.astype(bf16), preferred_element_type=f32) + run_scr[...]
    r1 = jnp.sum(jnp.where(hit1, base, 0.0), axis=-1, keepdims=True)
    r2 = jnp.sum(jnp.where(hit2, base, 0.0), axis=-1, keepdims=True)
    run_scr[...] = run_scr[...] + jnp.sum(onehot, axis=0, keepdims=True)
    cnt_ref[...] = run_scr[...]
    vals = (e1.astype(f32), e2.astype(f32), pg_top * p1 / den, pg_top * p2 / den, r1, r2)
    out = jnp.zeros(lg.shape, f32)
    for j, v in enumerate(vals):
        out = jnp.where(lane == j, v, out)
    o_ref[...] = out


def _route(logits):
    t, width = logits.shape
    tri = jnp.tril(jnp.ones((TM, TM), f32), -1).astype(bf16)
    return pl.pallas_call(
        _route_kernel,
        grid=(t // TM,),
        in_specs=[pl.BlockSpec((TM, width), lambda i: (i, 0)), pl.BlockSpec((TM, TM), lambda i: (0, 0))],
        out_specs=[pl.BlockSpec((TM, width), lambda i: (i, 0)), pl.BlockSpec((1, width), lambda i: (0, 0))],
        out_shape=[jax.ShapeDtypeStruct((t, width), f32), jax.ShapeDtypeStruct((1, width), f32)],
        scratch_shapes=[pltpu.VMEM((1, width), f32)],
        compiler_params=_cparams(("arbitrary",)),
        name="route",
    )(logits, tri)


def _combine_kernel(x_ref, y0_ref, y1_ref, gt_ref, g2_ref, gn_ref, o_ref, *, tiles_per_batch, batch, final_norm):
    seg = _tile_segment(pl.program_id(0), tiles_per_batch, batch)
    gt = gt_ref[...]
    x = x_ref[...] + g2_ref[pl.ds(seg, 1), :] * (gt[:, 0:1] * y0_ref[...] + gt[:, 1:2] * y1_ref[...])
    if final_norm:
        ms = jnp.mean(x * x, axis=-1, keepdims=True)
        x = x * lax.rsqrt(ms + EPS) * gn_ref[...]
    o_ref[...] = x


def _combine(x, y0, y1, gate, mods, g_out, *, n_tiles, batch, seq, final_norm):
    r, d = x.shape
    row = lambda i: (i, 0)
    return pl.pallas_call(
        functools.partial(_combine_kernel, tiles_per_batch=seq // TM, batch=batch, final_norm=final_norm),
        grid=(n_tiles,),
        in_specs=[pl.BlockSpec((TM, d), row), pl.BlockSpec((TM, d), row), pl.BlockSpec((TM, d), row),
                  pl.BlockSpec((TM, MOE_TOPK), row),
                  pl.BlockSpec((8, d), lambda i: (0, 5)),
                  pl.BlockSpec((1, d), lambda i: (0, 0))],
        out_specs=pl.BlockSpec((TM, d), row),
        out_shape=jax.ShapeDtypeStruct((n_tiles * TM, d), f32),
        compiler_params=_cparams(("arbitrary",)),
        name="combine",
    )(x, y0, y1, gate, mods, g_out)


def _moe(xn, h2, logits, mods, w1, w3, w2, g_out, *, n_tiles, batch, seq, final_norm):
    t = n_tiles * TM
    m = MOE_BLOCK
    routed, cnt = _route(logits)
    s = t * MOE_TOPK
    nb = s // m + MOE_EXPERTS
    expert = routed[:, 0:MOE_TOPK].astype(i32)
    gate = routed[:, MOE_TOPK:2 * MOE_TOPK]
    rank = routed[:, 2 * MOE_TOPK:3 * MOE_TOPK].astype(i32)
    counts = cnt[0, :MOE_EXPERTS].astype(i32)
    padded = ((counts + m - 1) // m) * m
    pad_end = jnp.cumsum(padded)
    onehot = expert[:, :, None] == jnp.arange(MOE_EXPERTS, dtype=i32)[None, None, :]
    dest = (jnp.sum(jnp.where(onehot, (pad_end - padded)[None, None, :], 0), axis=-1) + rank).reshape(s)
    n_used = (pad_end[-1] // m).astype(i32)
    blk = jnp.arange(nb, dtype=i32)
    block_e = jnp.minimum(jnp.searchsorted(pad_end, jnp.minimum(blk, n_used - 1) * m, side='right'),
                          MOE_EXPERTS - 1).astype(i32)
    tok = (jnp.arange(s, dtype=i32) // MOE_TOPK)
    buf_tok = jnp.zeros((nb * m,), i32).at[dest].set(tok)
    xs = jnp.take(h2, buf_tok, axis=0)
    ys = _moe_experts(xs, block_e, n_used.reshape(1), w1, w3, w2, nb)
    d2 = dest.reshape(t, MOE_TOPK)
    y0 = jnp.take(ys, d2[:, 0], axis=0)
    y1 = jnp.take(ys, d2[:, 1], axis=0)
    return _combine(xn, y0, y1, gate, mods, g_out, n_tiles=n_tiles, batch=batch, seq=seq, final_norm=final_norm)


HY_N2 = 128
HY_WB = 128


def _short_conv_kernel(x_ref, prev_ref, next_ref, w_ref, b_ref, v_ref, x1_ref, x2_ref, ext_scr, *, tiles_per_batch):
    j = pl.program_id(0) % tiles_per_batch
    tt = x_ref.shape[0]
    ext_scr[0:8, :] = jnp.where(j > 0, prev_ref[...], 0.0)
    ext_scr[8:8 + tt, :] = x_ref[...]
    ext_scr[8 + tt:16 + tt, :] = jnp.where(j < tiles_per_batch - 1, next_ref[...], 0.0)
    w = w_ref[...]
    width = v_ref.shape[1]
    for part, o_ref in enumerate((v_ref, x1_ref, x2_ref)):
        cols = slice(part * width, (part + 1) * width)
        o_ref[...] = (ext_scr[7:7 + tt, cols] * w[0:1, cols] + ext_scr[8:8 + tt, cols] * w[1:2, cols]
                      + ext_scr[9:9 + tt, cols] * w[2:3, cols] + b_ref[:, cols])


def _short_conv(hy, w, b, *, batch, seq):
    n_lat = batch * seq
    tt = 256
    width = hy.shape[1]
    out = jax.ShapeDtypeStruct((n_lat, width // 3), f32)
    o_spec = pl.BlockSpec((tt, width // 3), lambda i: (i, 0))
    return pl.pallas_call(
        functools.partial(_short_conv_kernel, tiles_per_batch=seq // tt),
        grid=(n_lat // tt,),
        in_specs=[pl.BlockSpec((tt, width), lambda i: (i, 0)),
                  pl.BlockSpec((8, width), lambda i: (jnp.maximum(i * (tt // 8) - 1, 0), 0)),
                  pl.BlockSpec((8, width), lambda i: ((i + 1) * (tt // 8), 0)),
                  pl.BlockSpec((3, width), lambda i: (0, 0)),
                  pl.BlockSpec((1, width), lambda i: (0, 0))],
        out_specs=[o_spec, o_spec, o_spec],
        out_shape=[out, out, out],
        scratch_shapes=[pltpu.VMEM((tt + 16, width), f32)],
        compiler_params=_cparams(("arbitrary",)),
        name="short_conv",
    )(hy, hy, hy, w, b.reshape(1, width))


def _hy_filter_kernel(z_ref, w1_ref, b1_ref, fr_ref, w2_ref, b2_ref, w3_ref, dl_ref, k_ref, nrm_ref):
    i = pl.program_id(0)
    z = z_ref[...]
    fr = fr_ref[...]
    h = jnp.sin(fr[0:1] * (jnp.dot(z.astype(bf16), w1_ref[...], preferred_element_type=f32) + b1_ref[...]))
    h = jnp.sin(fr[1:2] * (jnp.dot(h.astype(bf16), w2_ref[...], preferred_element_type=f32) + b2_ref[...]))
    kk = jnp.dot(h.astype(bf16), w3_ref[...], preferred_element_type=f32)
    width = dl_ref.shape[1]
    decay = jnp.exp(-z[:, 0:1] * jnp.abs(dl_ref[...]))
    tl = z.shape[0]
    row = i * tl + lax.broadcasted_iota(i32, (tl, 1), 0)
    tot = []
    for blk in range(kk.shape[1] // width):
        kb = kk[:, blk * width:(blk + 1) * width] * decay
        if blk % 2 == 1:
            kb = jnp.where(row == 0, 0.0, kb)
        k_ref[:, blk * width:(blk + 1) * width] = kb
        tot.append(jnp.sum(jnp.abs(kb), axis=0, keepdims=True))
    s = jnp.concatenate(tot, axis=1)

    @pl.when(i == 0)
    def _():
        nrm_ref[...] = s

    @pl.when(i > 0)
    def _():
        nrm_ref[...] = nrm_ref[...] + s


def _hy_filters(seq, w1, b1, freq, w2, b2, w3):
    t = jnp.linspace(0.0, 1.0, seq, dtype=f32)[:, None]
    bands = (HY_EMB - 1) // 2
    w = 2 * math.pi * jnp.arange(seq, dtype=f32)[:, None] / seq
    f = jnp.linspace(1e-4, bands - 1, bands, dtype=f32)[None, :]
    z = jnp.concatenate([t, jnp.cos(f * w), -jnp.sin(f * w), jnp.zeros((seq, 128 - HY_EMB), f32)], axis=-1)
    w1p = jnp.concatenate([w1, jnp.zeros((128 - HY_EMB, w1.shape[1]), f32)], axis=0).astype(bf16)
    hid = w1.shape[1]
    n_out = w3.shape[1]
    deltas = jnp.linspace(math.log(HY_TARGET) / HY_SLOW_DECAY, math.log(HY_TARGET) / HY_FAST_DECAY,
                          HY_WIDTH, dtype=f32).reshape(1, HY_WIDTH)
    tl = 512
    const = lambda a: pl.BlockSpec(a.shape, lambda i: (0,) * a.ndim)
    ins = [z, w1p, b1.reshape(1, hid), freq, w2.astype(bf16), b2.reshape(1, hid), w3.astype(bf16), deltas]
    return pl.pallas_call(
        _hy_filter_kernel,
        grid=(seq // tl,),
        in_specs=[pl.BlockSpec((tl, 128), lambda i: (i, 0))] + [const(a) for a in ins[1:]],
        out_specs=[pl.BlockSpec((tl, n_out), lambda i: (i, 0)), pl.BlockSpec((1, n_out), lambda i: (0, 0))],
        out_shape=[jax.ShapeDtypeStruct((seq, n_out), f32), jax.ShapeDtypeStruct((1, n_out), f32)],
        compiler_params=_cparams(("arbitrary",)),
        name="hy_filter",
    )(*ins)


def _dft_tables(seq):
    n2 = HY_N2
    n_tot = 2 * seq
    n1 = n_tot // n2
    nf = n1 // 2 + 1
    nfp = -(-nf // 8) * 8
    f1 = jnp.arange(nfp, dtype=i32)
    k1 = jnp.arange(n1 // 2, dtype=i32)
    th = (2 * math.pi / n1) * ((f1[:, None] * k1[None, :]) % n1).astype(f32)
    valid = (f1 < nf)[:, None]
    fwd = jnp.concatenate([jnp.where(valid, jnp.cos(th), 0.0), jnp.where(valid, -jnp.sin(th), 0.0)], axis=0)
    eye = jnp.eye(8, dtype=f32)
    fk = jnp.einsum('rk,jh->rjkh', fwd, eye).reshape(2 * nfp * 8, (n1 // 2) * 8)
    cw = jnp.where(valid, jnp.where((f1 == 0) | (f1 == n1 // 2), 1.0, 2.0)[:, None], 0.0) / n_tot
    inv = jnp.concatenate([(cw * jnp.cos(th)).T, (-cw * jnp.sin(th)).T], axis=1)
    ck = jnp.einsum('kr,jh->kjrh', inv, eye).reshape((n1 // 2) * 8, 2 * nfp * 8)
    fa = jnp.arange(nf, dtype=i32)[:, None, None]
    f2 = jnp.arange(n2, dtype=i32)[None, :, None]
    m2 = jnp.arange(n2, dtype=i32)[None, None, :]
    ph = (2 * math.pi / n_tot) * ((f2 * m2 * n1 + m2 * fa) % n_tot).astype(f32)
    gr, gi = jnp.cos(ph), -jnp.sin(ph)
    g = jnp.concatenate([jnp.concatenate([gr, -gi], axis=2), jnp.concatenate([gi, gr], axis=2)], axis=1)
    hr, hi = jnp.swapaxes(jnp.cos(ph), 1, 2), jnp.swapaxes(jnp.sin(ph), 1, 2)
    h = jnp.concatenate([jnp.concatenate([hr, -hi], axis=2), jnp.concatenate([hi, hr], axis=2)], axis=1)
    return fk.astype(bf16), ck.astype(bf16), g.astype(bf16), h.astype(bf16), nf, nfp


def _hy_rows_per_step(nf):
    return max(d for d in (5, 4, 3, 2, 1) if nf % d == 0)


def _dft_stage1(x_ref, fk_ref, a_scr):
    n1h, n2, wb = x_ref.shape

    def slab(s, carry):
        rows = pl.ds(pl.multiple_of(s * 8, 8), 8)
        xs = x_ref[:, rows, :].reshape(n1h * 8, wb).astype(bf16)
        a = jnp.dot(fk_ref[...], xs, preferred_element_type=f32)
        a_scr[:, rows, :] = a.reshape(a_scr.shape[0], 8, wb)
        return carry

    lax.fori_loop(0, n2 // 8, slab, 0)


def _dft_stage2(a_scr, g, f1, nfp):
    a = jnp.concatenate([a_scr[f1], a_scr[nfp + f1]], axis=0).astype(bf16)
    return jnp.dot(g, a, preferred_element_type=f32)


def _hy_spectrum_kernel(kf_ref, kb_ref, fk_ref, g_ref, o_ref, af_scr, ab_scr, *, nfp):
    fs = pl.program_id(2)

    @pl.when(fs == 0)
    def _():
        _dft_stage1(kf_ref, fk_ref, af_scr)
        _dft_stage1(kb_ref, fk_ref, ab_scr)

    fb = g_ref.shape[0]
    n2 = kf_ref.shape[1]
    for j in range(fb):
        f1 = fs * fb + j
        sf = _dft_stage2(af_scr, g_ref[j], f1, nfp)
        sb = _dft_stage2(ab_scr, g_ref[j], f1, nfp)
        o_ref[0, j] = sf[:n2] + sb[:n2]
        o_ref[1, j] = sf[n2:] - sb[n2:]


def _hy_spectrum(k, fk, g, nf, nfp, seq):
    n2, wb = HY_N2, HY_WB
    n1h = seq // n2
    fb = _hy_rows_per_step(nf)
    nwb = HY_WIDTH // wb
    k3 = k.reshape(n1h, n2, k.shape[1])
    return pl.pallas_call(
        functools.partial(_hy_spectrum_kernel, nfp=nfp),
        grid=(HY_ORDER, nwb, nf // fb),
        in_specs=[pl.BlockSpec((n1h, n2, wb), lambda o, w, f: (0, 0, (2 * o) * nwb + w)),
                  pl.BlockSpec((n1h, n2, wb), lambda o, w, f: (0, 0, (2 * o + 1) * nwb + w)),
                  pl.BlockSpec(fk.shape, lambda o, w, f: (0, 0)),
                  pl.BlockSpec((fb, 2 * n2, 2 * n2), lambda o, w, f: (f, 0, 0))],
        out_specs=pl.BlockSpec((None, 2, fb, n2, wb), lambda o, w, f: (o, 0, f, 0, w)),
        out_shape=jax.ShapeDtypeStruct((HY_ORDER, 2, nf, n2, HY_WIDTH), f32),
        scratch_shapes=[pltpu.VMEM((2 * nfp, n2, wb), f32), pltpu.VMEM((2 * nfp, n2, wb), f32)],
        compiler_params=_cparams(("arbitrary",) * 3),
        name="hy_spectrum",
    )(k3, k3, fk, g)


def _hy_conv_kernel(x_ref, gate_ref, kf_ref, g_ref, h_ref, fk_ref, ck_ref, bias_ref, inv_ref, o_ref, a_scr, *, nfp):
    fs = pl.program_id(2)
    n1h, n2, wb = x_ref.shape

    @pl.when(fs == 0)
    def _():
        _dft_stage1(x_ref, fk_ref, a_scr)

    fb = g_ref.shape[0]
    for j in range(fb):
        f1 = fs * fb + j
        xs = _dft_stage2(a_scr, g_ref[j], f1, nfp)
        xr, xi = xs[:n2], xs[n2:]
        kr, ki = kf_ref[0, j], kf_ref[1, j]
        y = jnp.concatenate([xr * kr - xi * ki, xr * ki + xi * kr], axis=0).astype(bf16)
        bb = jnp.dot(h_ref[j], y, preferred_element_type=f32)
        a_scr[f1] = bb[:n2]
        a_scr[nfp + f1] = bb[n2:]

    @pl.when(fs == pl.num_programs(2) - 1)
    def _():
        bias = bias_ref[...].reshape(1, 1, wb)
        inv = inv_ref[...].reshape(1, 1, wb)

        def slab(s, carry):
            rows = pl.ds(pl.multiple_of(s * 8, 8), 8)
            bs = a_scr[:, rows, :].reshape(2 * nfp * 8, wb).astype(bf16)
            y = jnp.dot(ck_ref[...], bs, preferred_element_type=f32).reshape(n1h, 8, wb)
            o_ref[:, rows, :] = (gate_ref[:, rows, :] * (y * inv + x_ref[:, rows, :] * bias)).astype(o_ref.dtype)
            return carry

        lax.fori_loop(0, n2 // 8, slab, 0)


def _hy_conv(x, gate, kf, order, tabs, bias, inv_norm, out_dtype, *, batch, seq):
    fk, ck, g, h, nf, nfp = tabs
    n2, wb = HY_N2, HY_WB
    n1h = seq // n2
    fb = _hy_rows_per_step(nf)
    width = x.shape[1]
    big = pl.BlockSpec((n1h, n2, wb), lambda b, w, f: (b, 0, w))
    out = pl.pallas_call(
        functools.partial(_hy_conv_kernel, nfp=nfp),
        grid=(batch, width // wb, nf // fb),
        in_specs=[big, big,
                  pl.BlockSpec((None, 2, fb, n2, wb), lambda b, w, f: (order, 0, f, 0, w)),
                  pl.BlockSpec((fb, 2 * n2, 2 * n2), lambda b, w, f: (f, 0, 0)),
                  pl.BlockSpec((fb, 2 * n2, 2 * n2), lambda b, w, f: (f, 0, 0)),
                  pl.BlockSpec(fk.shape, lambda b, w, f: (0, 0)),
                  pl.BlockSpec(ck.shape, lambda b, w, f: (0, 0)),
                  pl.BlockSpec((1, wb), lambda b, w, f: (0, w)),
                  pl.BlockSpec((1, wb), lambda b, w, f: (0, w))],
        out_specs=big,
        out_shape=jax.ShapeDtypeStruct((batch * n1h, n2, width), out_dtype),
        scratch_shapes=[pltpu.VMEM((2 * nfp, n2, wb), f32)],
        compiler_params=_cparams(("arbitrary",) * 3),
        name="hy_conv",
    )(x.reshape(batch * n1h, n2, width), gate.reshape(batch * n1h, n2, width), kf, g, h, fk, ck,
      bias.reshape(1, width), inv_norm.reshape(1, width))
    return out.reshape(batch * seq, width)


def _hyena(hy, conv_w, conv_b, f_w1, f_b1, f_freq, f_w2, f_b2, f_w3, hy_bias, *, batch, seq):
    v, x1, x2 = _short_conv(hy, conv_w, conv_b, batch=batch, seq=seq)
    k, sums = _hy_filters(seq, f_w1, f_b1, f_freq, f_w2, f_b2, f_w3)
    sums = sums.reshape(HY_ORDER, 2, HY_WIDTH)
    inv_norm = 1.0 / (sums[:, 0] + sums[:, 1])
    tabs = _dft_tables(seq)
    kf = _hy_spectrum(k, tabs[0], tabs[2], tabs[4], tabs[5], seq)
    z = _hy_conv(v, x1, kf, 0, tabs, hy_bias[0], inv_norm[0], f32, batch=batch, seq=seq)
    return _hy_conv(z, x2, kf, 1, tabs, hy_bias[1], inv_norm[1], bf16, batch=batch, seq=seq)


S5_TT = 256
S5_GB = 8


def _gelu_tanh(x):
    return 0.5 * x * (1.0 + jnp.tanh(0.7978845608028654 * (x + 0.044715 * x * x * x)))


def _s5_kernel(u_ref, perm_ref, permt_ref, wb_ref, wc_ref, are_ref, aim_ref, pre_ref, pim_ref, *rest, fuse_glu):
    if fuse_glu:
        yf_ref, d_ref, gw_ref, gb_ref, o_ref, st_scr, bu_scr, y_scr = rest
    else:
        o_ref, st_scr, bu_scr, y_scr = rest
    tt = u_ref.shape[0]
    steps = tt // 8
    half = bu_scr.shape[1] // 2

    @pl.when(pl.program_id(1) == 0)
    def _():
        st_scr[...] = jnp.zeros_like(st_scr)

    up = jnp.dot(perm_ref[...], u_ref[...].astype(bf16), preferred_element_type=f32).astype(bf16)
    for gb in range(wb_ref.shape[0]):
        bu_scr[...] = jnp.dot(up[:, gb * 128:(gb + 1) * 128], wb_ref[gb], preferred_element_type=f32)
        a_re = jnp.broadcast_to(are_ref[gb], (8, half))
        a_im = jnp.broadcast_to(aim_ref[gb], (8, half))

        def scan_step(k, h):
            hr, hi = h
            rows = pl.ds(pl.multiple_of(k * 8, 8), 8)
            nr = a_re * hr - a_im * hi + bu_scr[rows, 0:half]
            ni = a_re * hi + a_im * hr + bu_scr[rows, half:]
            bu_scr[rows, 0:half] = nr
            bu_scr[rows, half:] = ni
            return nr, ni

        zero = jnp.zeros((8, half), f32)
        hr, hi = lax.fori_loop(0, steps, scan_step, (zero, zero), unroll=4)

        pw_r = pre_ref[gb, steps - 1:steps, :]
        pw_i = pim_ref[gb, steps - 1:steps, :]
        cur_r = st_scr[gb, :, 0:half]
        cur_i = st_scr[gb, :, half:]
        car_r, car_i = [], []
        for j in range(8):
            car_r.append(cur_r)
            car_i.append(cur_i)
            cur_r, cur_i = (hr[j:j + 1] + pw_r * cur_r - pw_i * cur_i,
                            hi[j:j + 1] + pw_r * cur_i + pw_i * cur_r)
        st_scr[gb, :, 0:half] = cur_r
        st_scr[gb, :, half:] = cur_i
        cr = jnp.concatenate(car_r, axis=0)
        ci = jnp.concatenate(car_i, axis=0)

        def fix_step(k, carry):
            rows = pl.ds(pl.multiple_of(k * 8, 8), 8)
            pr = pre_ref[gb, pl.ds(k, 1), :]
            pi = pim_ref[gb, pl.ds(k, 1), :]
            bu_scr[rows, 0:half] = bu_scr[rows, 0:half] + (pr * cr - pi * ci)
            bu_scr[rows, half:] = bu_scr[rows, half:] + (pr * ci + pi * cr)
            return carry

        lax.fori_loop(0, steps, fix_step, 0, unroll=4)
        y_scr[:, gb * 128:(gb + 1) * 128] = jnp.dot(bu_scr[...].astype(bf16), wc_ref[gb],
                                                      preferred_element_type=f32)
    y = jnp.dot(permt_ref[...], y_scr[...], preferred_element_type=f32, precision=lax.Precision.HIGHEST)
    if fuse_glu:
        g = _gelu_tanh(y + yf_ref[...] + u_ref[...] * d_ref[...])
        z = jnp.dot(g.astype(bf16), gw_ref[...], preferred_element_type=f32) + gb_ref[...]
        o_ref[...] = (g * jax.nn.sigmoid(z)).astype(o_ref.dtype)
    else:
        o_ref[...] = y


def _s5_tables(a_re, a_im, log_dt, b_re, b_im, c_re, c_im, steps):
    ng, ns = a_re.shape
    nb = ng // S5_GB
    lam = lax.complex(a_re, a_im)
    dt = jnp.exp(log_dt)[:, None]
    lam_bar = jnp.exp(lam * dt)
    b_bar = ((lam_bar - 1) / lam)[..., None] * lax.complex(b_re, b_im)
    cm = lax.complex(c_re, c_im)
    pw = jnp.exp((lam * dt)[None] * jnp.arange(1, steps + 1, dtype=f32)[:, None, None])
    eye = jnp.eye(S5_GB, dtype=f32)
    bb = b_bar.reshape(nb, S5_GB, ns, S5_GROUP)
    wb = jnp.concatenate([jnp.einsum('bgpc,gh->bgchp', part, eye).reshape(nb, S5_GB * S5_GROUP, S5_GB * ns)
                          for part in (jnp.real(bb), jnp.imag(bb))], axis=-1)
    cc = cm.reshape(nb, S5_GB, S5_GROUP, ns)
    wc = jnp.concatenate([jnp.einsum('bgcp,gh->bgphc', part, eye).reshape(nb, S5_GB * ns, S5_GB * S5_GROUP)
                          for part in (jnp.real(cc), -jnp.imag(cc))], axis=1)
    lb = lam_bar.reshape(nb, 1, S5_GB * ns)
    pwb = pw.reshape(steps, nb, S5_GB * ns).transpose(1, 0, 2)
    return (wb.astype(bf16), wc.astype(bf16), jnp.real(lb), jnp.imag(lb), jnp.real(pwb), jnp.imag(pwb))


def _s5_perm(tt, reverse):
    steps = tt // 8
    dst = jnp.arange(tt)
    t = (dst % 8) * steps + dst // 8
    src = tt - 1 - t if reverse else t
    return jnp.arange(tt)[None, :] == src[:, None]


def _s5(u, a_re, a_im, log_dt, b_re, b_im, c_re, c_im, s5_d, glu_w, glu_b, *, batch, seq, ctx_len):
    r, width = u.shape
    tt = S5_TT
    assert ctx_len == tt and seq % tt == 0
    n_lat = seq // tt
    ctx_blk0 = (batch * seq) // tt
    y = None
    for di in range(2):
        rev = di == 1
        wb, wc, lr, li, pr, pi = _s5_tables(a_re[di], a_im[di], log_dt[di], b_re[di], b_im[di],
                                            c_re[di], c_im[di], tt // 8)
        perm = _s5_perm(tt, rev)

        def tile_map(b, i, rev=rev):
            lat = b * n_lat + (n_lat - i if rev else i - 1)
            return (jnp.where(i == 0, ctx_blk0 + b, lat), 0)

        row_spec = pl.BlockSpec((tt, width), tile_map)
        const = lambda a: pl.BlockSpec(a.shape, lambda b, i: (0,) * a.ndim)
        ins = [u, perm.astype(bf16), perm.T.astype(f32), wb, wc, lr, li, pr, pi]
        in_specs = [row_spec] + [const(a) for a in ins[1:]]
        if rev:
            extra = [y, s5_d.reshape(1, width), glu_w.astype(bf16), glu_b.reshape(1, width)]
            in_specs += [row_spec] + [const(a) for a in extra[1:]]
            ins += extra
        y = pl.pallas_call(
            functools.partial(_s5_kernel, fuse_glu=rev),
            grid=(batch, n_lat + 1),
            in_specs=in_specs,
            out_specs=row_spec,
            out_shape=jax.ShapeDtypeStruct((r, width), bf16 if rev else f32),
            scratch_shapes=[pltpu.VMEM((wb.shape[0], 1, wb.shape[2]), f32),
                            pltpu.VMEM((tt, wb.shape[2]), f32),
                            pltpu.VMEM((tt, width), f32)],
            compiler_params=_cparams(("arbitrary", "arbitrary")),
            name="s5_bwd_glu" if rev else "s5_fwd",
        )(*ins)
    return y


def _rope_tables(seq):
    rows = seq // GRID_W
    row = jnp.repeat(jnp.arange(rows, dtype=f32), GRID_W)
    col = jnp.tile(jnp.arange(GRID_W, dtype=f32), rows)
    n_freq = DIFF_QK // 4
    inv = ROPE_BASE ** (-jnp.arange(n_freq, dtype=f32) / n_freq)
    ar, ac = row[:, None] * inv, col[:, None] * inv
    cos = jnp.concatenate([jnp.cos(ar), jnp.cos(ar), jnp.cos(ac), jnp.cos(ac)], axis=-1)
    sin = jnp.concatenate([-jnp.sin(ar), jnp.sin(ar), -jnp.sin(ac), jnp.sin(ac)], axis=-1)
    cos = jnp.concatenate([jnp.tile(cos, (1, 2)), jnp.ones((TM, 128), f32)], axis=0)
    sin = jnp.concatenate([jnp.tile(sin, (1, 2)), jnp.zeros((TM, 128), f32)], axis=0)
    return cos, sin


def kernel(x, c, ctx, c_ctx, ada_w, ada_b, norm_mix_g, norm_ffn_g, norm_out_g, ab_w_in, ab_w_out, diff_lam, diff_subln_g, pool_w, pool_scale, cd_w_in, cd_w_out, hy_conv_w, hy_conv_b, hy_f_w1, hy_f_b1, hy_f_freq, hy_f_w2, hy_f_b2, hy_f_w3, hy_bias, s5_a_re, s5_a_im, s5_log_dt, s5_b_re, s5_b_im, s5_c_re, s5_c_im, s5_d, glu_w, glu_b, moe_w_rg, moe_b_rg, moe_w_re, moe_b_re, moe_w1, moe_w3, moe_w2):
    bsz, seq, d = x.shape
    cl = ctx.shape[1]
    depth = ada_w.shape[0]
    n_lat = bsz * seq
    assert seq % TM == 0 and (bsz * cl) % TM == 0 and bsz + 1 <= 8
    xs = jnp.concatenate([x.reshape(n_lat, d), ctx.reshape(bsz * cl, d)], axis=0)
    r = xs.shape[0]
    cond = jnp.zeros((8, d), f32).at[0:bsz].set(c).at[bsz].set(c_ctx)
    mods = _ada_mods(cond, ada_w, ada_b)
    geom = dict(batch=bsz, seq=seq)

    for l in range(depth):
        last = l == depth - 1
        i = l // 2
        n_tiles = (n_lat if last else r) // TM
        g_mix = norm_mix_g[l].reshape(1, d)
        if l % 2 == 0:
            lam_init = 0.8 - 0.6 * math.exp(-0.3 * l)
            cos, sin = _rope_tables(seq)
            q, k, v, p = _proj_rope(xs, g_mix, mods[l], ab_w_in[i].astype(bf16), cos, sin, **geom)
            mix_a = _diff_attention(q, k, v, diff_lam[i], diff_subln_g[i], lam_init, ctx_len=cl, **geom)
            mix_b = _pool(p, pool_w[i], pool_scale[i], ctx_len=cl, **geom)
            w_out = ab_w_out[i]
        else:
            hw = (HY_ORDER + 1) * HY_WIDTH
            hy, u = _proj(xs, g_mix, mods[l], cd_w_in[i].astype(bf16), hw, **geom)
            mix_a = _hyena(hy, hy_conv_w[i], hy_conv_b[i], hy_f_w1[i], hy_f_b1[i], hy_f_freq[i],
                           hy_f_w2[i], hy_f_b2[i], hy_f_w3[i], hy_bias[i], **geom)
            mix_b = _s5(u, s5_a_re[i], s5_a_im[i], s5_log_dt[i], s5_b_re[i], s5_b_im[i],
                        s5_c_re[i], s5_c_im[i], s5_d[i], glu_w[i], glu_b[i], ctx_len=cl, **geom)
            w_out = cd_w_out[i]
        w_r = jnp.zeros((d, 128), f32).at[:, :MOE_GROUPS].set(moe_w_rg[l]) \
            .at[:, MOE_GROUPS:MOE_GROUPS + MOE_EXPERTS].set(moe_w_re[l])
        b_r = jnp.zeros((1, 128), f32).at[0, :MOE_GROUPS].set(moe_b_rg[l]) \
            .at[0, MOE_GROUPS:MOE_GROUPS + MOE_EXPERTS].set(moe_b_re[l])
        xn, h2, logits = _outproj(mix_a, mix_b, xs, w_out.astype(bf16), mods[l], norm_ffn_g[l].reshape(1, d),
                                  w_r, b_r, n_tiles=n_tiles, **geom)
        xs = _moe(xn, h2, logits, mods[l], moe_w1[l], moe_w3[l], moe_w2[l], norm_out_g.reshape(1, d),
                  n_tiles=n_tiles, final_norm=last, **geom)
    return xs[:n_lat].reshape(bsz, seq, d)
```

```python
import functools
import math

import jax
import jax.numpy as jnp
from jax import lax
from jax.experimental import pallas as pl
from jax.experimental.pallas import tpu as pltpu

f32, bf16, i32 = jnp.float32, jnp.bfloat16, jnp.int32

EPS = 1e-6
GRID_W = 64
DIFF_QK = 64
DIFF_V = 128
ROPE_BASE = 10000.0
POOL_WINDOWS = (2, 4, 8, 16)
POOL_GROUP = 256
HY_WIDTH = 1024
HY_ORDER = 2
HY_EMB = 33
HY_FAST_DECAY = 0.3
HY_SLOW_DECAY = 1.5
HY_TARGET = 1e-2
S5_GROUP = 16
S5_GROUPS = 64
S5_STATE = 64
MOE_GROUPS = 8
MOE_PER_GROUP = 8
MOE_EXPERTS = 64
MOE_TOPK = 2
MOE_BLOCK = 256

VMEM_LIMIT = 56 * 1024 * 1024
TM = 512
LOG2E = 1.4426950408889634


def _cparams(sem):
    return pltpu.CompilerParams(dimension_semantics=sem, vmem_limit_bytes=VMEM_LIMIT)


def _norm_mod(x, g, shift, scale):
    ms = jnp.mean(x * x, axis=-1, keepdims=True)
    return (x * lax.rsqrt(ms + EPS) * g) * (1.0 + scale) + shift


def _tile_segment(i, lat_tiles_per_batch, batch):
    seg = jnp.zeros((), i32)
    for b in range(1, batch + 1):
        seg = seg + (i >= b * lat_tiles_per_batch).astype(i32)
    return seg


def _ada_kernel(s_ref, w_ref, b_ref, o_ref):
    s = s_ref[...]
    a = (s * jax.nn.sigmoid(s)).astype(bf16)
    o_ref[...] = jnp.dot(a, w_ref[...].astype(bf16), preferred_element_type=f32) + b_ref[...]


def _ada_mods(s, ada_w, ada_b):
    depth, d, n = ada_w.shape
    tn = 1024
    return pl.pallas_call(
        _ada_kernel,
        grid=(depth, n // tn),
        in_specs=[pl.BlockSpec((8, d), lambda l, j: (0, 0)),
                  pl.BlockSpec((None, d, tn), lambda l, j: (l, 0, j)),
                  pl.BlockSpec((None, 1, tn), lambda l, j: (l, 0, j))],
        out_specs=pl.BlockSpec((None, 8, tn), lambda l, j: (l, 0, j)),
        out_shape=jax.ShapeDtypeStruct((depth, 8, n), f32),
        compiler_params=_cparams(("arbitrary", "arbitrary")),
        name="ada_mods",
    )(s, ada_w, ada_b.reshape(depth, 1, n))


def _proj_rope_kernel(x_ref, g_ref, sh_ref, sc_ref, w_ref, cos_ref, sin_ref,
                      q_ref, k_ref, v_ref, p_ref, h_scr, *, tiles_per_batch, batch, qscale):
    seg = _tile_segment(pl.program_id(0), tiles_per_batch, batch)
    h = _norm_mod(x_ref[...], g_ref[...], sh_ref[pl.ds(seg, 1), :], sc_ref[pl.ds(seg, 1), :])
    h_scr[...] = h.astype(bf16)
    cos = cos_ref[...]
    sin = sin_ref[...]
    tm = cos.shape[0]
    lane = lax.broadcasted_iota(i32, (tm, 128), 1)
    first = (lane % 32) < 16

    def rope(r):
        sw = jnp.where(first, pltpu.roll(r, 112, 1), pltpu.roll(r, 16, 1))
        return r * cos + sw * sin

    heads = q_ref.shape[0]
    nc = 512
    for c in range(w_ref.shape[1] // nc):
        res = jnp.dot(h_scr[...], w_ref[:, c * nc:(c + 1) * nc], preferred_element_type=f32)
        for s in range(nc // 128):
            r = res[:, s * 128:(s + 1) * 128]
            col = (c * nc) // 128 + s
            if col < heads:
                q_ref[col] = (rope(r) * qscale).astype(bf16)
            elif col < 2 * heads:
                k_ref[col - heads] = rope(r).astype(bf16)
            elif col < 3 * heads:
                v_ref[col - 2 * heads] = r.astype(bf16)
            else:
                c0 = (col - 3 * heads) * 128
                p_ref[:, c0:c0 + 128] = r


def _proj_rope(x, g, mods, w, cos, sin, *, batch, seq):
    r, d = x.shape
    heads = 8
    lat_tiles = seq // TM
    n_tiles = r // TM

    def tab_map(i):
        return (jnp.where(i < batch * lat_tiles, i % lat_tiles, lat_tiles), 0)

    hm = jax.ShapeDtypeStruct((heads, r, 128), bf16)
    hm_spec = pl.BlockSpec((heads, TM, 128), lambda i: (0, i, 0))
    return pl.pallas_call(
        functools.partial(_proj_rope_kernel, tiles_per_batch=lat_tiles, batch=batch,
                          qscale=LOG2E * DIFF_QK ** -0.5),
        grid=(n_tiles,),
        in_specs=[pl.BlockSpec((TM, d), lambda i: (i, 0)),
                  pl.BlockSpec((1, d), lambda i: (0, 0)),
                  pl.BlockSpec((8, d), lambda i: (0, 0)),
                  pl.BlockSpec((8, d), lambda i: (0, 1)),
                  pl.BlockSpec(w.shape, lambda i: (0, 0), pipeline_mode=pl.Buffered(1)),
                  pl.BlockSpec((TM, 128), tab_map),
                  pl.BlockSpec((TM, 128), tab_map)],
        out_specs=[hm_spec, hm_spec, hm_spec, pl.BlockSpec((TM, 1024), lambda i: (i, 0))],
        out_shape=[hm, hm, hm, jax.ShapeDtypeStruct((r, 1024), f32)],
        scratch_shapes=[pltpu.VMEM((TM, d), bf16)],
        compiler_params=_cparams(("arbitrary",)),
        name="proj_rope",
    )(x, g, mods, mods, w, cos, sin)


def _proj_kernel(x_ref, g_ref, sh_ref, sc_ref, w_ref, hy_ref, u_ref, h_scr, *, tiles_per_batch, batch):
    seg = _tile_segment(pl.program_id(0), tiles_per_batch, batch)
    h = _norm_mod(x_ref[...], g_ref[...], sh_ref[pl.ds(seg, 1), :], sc_ref[pl.ds(seg, 1), :])
    h_scr[...] = h.astype(bf16)
    nc = 512
    n_hy = hy_ref.shape[1]
    for c in range(w_ref.shape[1] // nc):
        res = jnp.dot(h_scr[...], w_ref[:, c * nc:(c + 1) * nc], preferred_element_type=f32)
        if c * nc < n_hy:
            hy_ref[:, c * nc:(c + 1) * nc] = res
        else:
            u_ref[:, c * nc - n_hy:(c + 1) * nc - n_hy] = res


def _proj(x, g, mods, w, n_hy, *, batch, seq):
    r, d = x.shape
    n = w.shape[1]
    return pl.pallas_call(
        functools.partial(_proj_kernel, tiles_per_batch=seq // TM, batch=batch),
        grid=(r // TM,),
        in_specs=[pl.BlockSpec((TM, d), lambda i: (i, 0)),
                  pl.BlockSpec((1, d), lambda i: (0, 0)),
                  pl.BlockSpec((8, d), lambda i: (0, 0)),
                  pl.BlockSpec((8, d), lambda i: (0, 1)),
                  pl.BlockSpec(w.shape, lambda i: (0, 0), pipeline_mode=pl.Buffered(1))],
        out_specs=[pl.BlockSpec((TM, n_hy), lambda i: (i, 0)), pl.BlockSpec((TM, n - n_hy), lambda i: (i, 0))],
        out_shape=[jax.ShapeDtypeStruct((r, n_hy), f32), jax.ShapeDtypeStruct((r, n - n_hy), f32)],
        scratch_shapes=[pltpu.VMEM((TM, d), bf16)],
        compiler_params=_cparams(("arbitrary",)),
        name="proj",
    )(x, g, mods, mods, w)


def _attn_kernel(lv_ref, g_ref, q_ref, kc_ref, vc_ref, *rest, n_lat, tk, lam_init):
    if n_lat:
        kl_ref, vl_ref, o_ref = rest
    else:
        (o_ref,) = rest
    lv = lv_ref[...]
    lam = (jnp.exp(jnp.sum(lv[0:1] * lv[1:2], axis=-1, keepdims=True))
           - jnp.exp(jnp.sum(lv[2:3] * lv[3:4], axis=-1, keepdims=True)) + lam_init)
    q = q_ref[...]
    tq = q.shape[0]
    lane = lax.broadcasted_iota(i32, q.shape, 1)
    zero = jnp.zeros_like(q)
    qs = (jnp.where(lane < DIFF_QK, q, zero), jnp.where(lane >= DIFF_QK, q, zero))
    nt = (((1,), (1,)), ((), ()))

    def update(carry, k, v):
        new = []
        for m in range(2):
            mx, l, acc = carry[3 * m:3 * m + 3]
            s = lax.dot_general(qs[m], k, nt, preferred_element_type=f32)
            mn = jnp.maximum(mx, jnp.max(s, axis=-1, keepdims=True))
            alpha = jnp.exp2(mx - mn)
            p = jnp.exp2(s - mn)
            l = alpha * l + jnp.sum(p, axis=-1, keepdims=True)
            acc = alpha * acc + jnp.dot(p.astype(bf16), v, preferred_element_type=f32)
            new += [mn, l, acc]
        return tuple(new)

    init = (jnp.full((tq, 1), -jnp.inf, f32), jnp.zeros((tq, 1), f32), jnp.zeros((tq, DIFF_V), f32)) * 2
    carry = update(init, kc_ref[...], vc_ref[...])
    if n_lat:
        def body(j, carry):
            off = pl.multiple_of(j * tk, tk)
            return update(carry, kl_ref[pl.ds(off, tk), :], vl_ref[pl.ds(off, tk), :])
        carry = lax.fori_loop(0, n_lat, body, carry)
    _, l0, a0, _, l1, a1 = carry
    o = a0 / l0 - lam * (a1 / l1)
    ms = jnp.mean(o * o, axis=-1, keepdims=True)
    o_ref[...] = ((o * lax.rsqrt(ms + EPS) * g_ref[...]) * (1.0 - lam_init)).astype(o_ref.dtype)


def _diff_attention(q, k, v, lam_vecs, subln_g, lam_init, *, batch, seq, ctx_len):
    heads, r, _ = q.shape
    tq, tk = 512, 512
    ctx_blk0 = (batch * seq) // ctx_len
    small = [pl.BlockSpec((4, DIFF_QK), lambda *_: (0, 0)), pl.BlockSpec((1, DIFF_V), lambda *_: (0, 0))]
    kc_spec = pl.BlockSpec((None, ctx_len, 128), lambda b, h, *_: (h, ctx_blk0 + b, 0))
    g2 = subln_g.reshape(1, DIFF_V)
    lat = pl.pallas_call(
        functools.partial(_attn_kernel, n_lat=seq // tk, tk=tk, lam_init=lam_init),
        grid=(batch, heads, seq // tq),
        in_specs=small + [pl.BlockSpec((None, tq, 128), lambda b, h, i: (h, b * (seq // tq) + i, 0)),
                          kc_spec, kc_spec,
                          pl.BlockSpec((None, seq, 128), lambda b, h, i: (h, b, 0)),
                          pl.BlockSpec((None, seq, 128), lambda b, h, i: (h, b, 0))],
        out_specs=pl.BlockSpec((tq, 128), lambda b, h, i: (b * (seq // tq) + i, h)),
        out_shape=jax.ShapeDtypeStruct((batch * seq, heads * DIFF_V), bf16),
        compiler_params=_cparams(("arbitrary",) * 3),
        name="diff_attn_lat",
    )(lam_vecs, g2, q, k, v, k, v)
    ctx = pl.pallas_call(
        functools.partial(_attn_kernel, n_lat=0, tk=tk, lam_init=lam_init),
        grid=(batch, heads),
        in_specs=small + [kc_spec, kc_spec, kc_spec],
        out_specs=pl.BlockSpec((ctx_len, 128), lambda b, h: (b, h)),
        out_shape=jax.ShapeDtypeStruct((batch * ctx_len, heads * DIFF_V), bf16),
        compiler_params=_cparams(("arbitrary",) * 2),
        name="diff_attn_ctx",
    )(lam_vecs, g2, q, k, v)
    return jnp.concatenate([lat, ctx], axis=0)


def _pool_kernel(p_ref, prev_ref, next_ref, w_ref, scale_ref, o_ref, ext_scr, *,
                 tiles_per_batch, batch, seq, ctx_len):
    i = pl.program_id(0)
    tp = p_ref.shape[0]
    is_ctx = i >= batch * tiles_per_batch
    j = jnp.where(is_ctx, 0, i % tiles_per_batch)
    seg_len = jnp.where(is_ctx, ctx_len, seq)
    last = jnp.where(is_ctx, 0, tiles_per_batch - 1)
    ext_scr[0:8, :] = jnp.where(j > 0, prev_ref[...], 0.0)
    ext_scr[8:8 + tp, :] = p_ref[...]
    ext_scr[8 + tp:16 + tp, :] = jnp.where(j < last, next_ref[...], 0.0)
    pos = j * tp + lax.broadcasted_iota(i32, (tp, 1), 0)
    for gi, w in enumerate(POOL_WINDOWS):
        c0 = gi * POOL_GROUP
        half = w // 2
        tot = ext_scr[8 - half:8 - half + tp, c0:c0 + POOL_GROUP]
        for dlt in range(1 - half, half):
            tot = tot + ext_scr[8 + dlt:8 + dlt + tp, c0:c0 + POOL_GROUP]
        cnt = (jnp.minimum(pos - half + w, seg_len) - jnp.maximum(pos - half, 0)).astype(f32)
        dlt_mean = (tot / cnt - ext_scr[8:8 + tp, c0:c0 + POOL_GROUP]).astype(bf16)
        y = jnp.dot(dlt_mean, w_ref[gi].astype(bf16), preferred_element_type=f32)
        o_ref[:, c0:c0 + POOL_GROUP] = (y * scale_ref[:, c0:c0 + POOL_GROUP]).astype(o_ref.dtype)


def _pool(p, pool_w, pool_scale, *, batch, seq, ctx_len):
    r, width = p.shape
    tp = ctx_len
    tpb = seq // tp
    return pl.pallas_call(
        functools.partial(_pool_kernel, tiles_per_batch=tpb, batch=batch, seq=seq, ctx_len=ctx_len),
        grid=(r // tp,),
        in_specs=[pl.BlockSpec((tp, width), lambda i: (i, 0)),
                  pl.BlockSpec((8, width), lambda i: (jnp.maximum(i * (tp // 8) - 1, 0), 0)),
                  pl.BlockSpec((8, width), lambda i: (jnp.minimum((i + 1) * (tp // 8), r // 8 - 1), 0)),
                  pl.BlockSpec(pool_w.shape, lambda i: (0, 0, 0)),
                  pl.BlockSpec((1, width), lambda i: (0, 0))],
        out_specs=pl.BlockSpec((tp, width), lambda i: (i, 0)),
        out_shape=jax.ShapeDtypeStruct((r, width), bf16),
        scratch_shapes=[pltpu.VMEM((tp + 16, width), f32)],
        compiler_params=_cparams(("arbitrary",)),
        name="pool",
    )(p, p, p, pool_w, pool_scale.reshape(1, width))


def _outproj_kernel(a_ref, b_ref, x_ref, w_ref, g1_ref, sh_ref, sc_ref, gn_ref, wr_ref, br_ref,
                    xn_ref, h2_ref, lg_ref, *, tiles_per_batch, batch):
    seg = _tile_segment(pl.program_id(0), tiles_per_batch, batch)
    half = a_ref.shape[1]
    y = (jnp.dot(a_ref[...], w_ref[0:half, :], preferred_element_type=f32)
         + jnp.dot(b_ref[...], w_ref[half:, :], preferred_element_type=f32))
    xn = x_ref[...] + g1_ref[pl.ds(seg, 1), :] * y
    xn_ref[...] = xn
    h2 = _norm_mod(xn, gn_ref[...], sh_ref[pl.ds(seg, 1), :], sc_ref[pl.ds(seg, 1), :])
    h2_ref[...] = h2
    lg_ref[...] = jnp.dot(h2, wr_ref[...], preferred_element_type=f32,
                          precision=lax.Precision.HIGHEST) + br_ref[...]


def _outproj(a, b, x, w, mods, g_ffn, w_r, b_r, *, n_tiles, batch, seq):
    r, d = n_tiles * TM, x.shape[1]
    half = a.shape[1]
    row = lambda i: (i, 0)
    return pl.pallas_call(
        functools.partial(_outproj_kernel, tiles_per_batch=seq // TM, batch=batch),
        grid=(n_tiles,),
        in_specs=[pl.BlockSpec((TM, half), row), pl.BlockSpec((TM, half), row), pl.BlockSpec((TM, d), row),
                  pl.BlockSpec(w.shape, lambda i: (0, 0), pipeline_mode=pl.Buffered(1)),
                  pl.BlockSpec((8, d), lambda i: (0, 2)),
                  pl.BlockSpec((8, d), lambda i: (0, 3)),
                  pl.BlockSpec((8, d), lambda i: (0, 4)),
                  pl.BlockSpec((1, d), lambda i: (0, 0)),
                  pl.BlockSpec(w_r.shape, lambda i: (0, 0)),
                  pl.BlockSpec((1, 128), lambda i: (0, 0))],
        out_specs=[pl.BlockSpec((TM, d), row), pl.BlockSpec((TM, d), row), pl.BlockSpec((TM, 128), row)],
        out_shape=[jax.ShapeDtypeStruct((r, d), f32), jax.ShapeDtypeStruct((r, d), f32),
                   jax.ShapeDtypeStruct((r, 128), f32)],
        compiler_params=_cparams(("arbitrary",)),
        name="outproj",
    )(a, b, x, w, mods, mods, mods, g_ffn, w_r, b_r)


def _moe_kernel(be_ref, nu_ref, xs_ref, w1_ref, w3_ref, w2_ref, ys_ref, w1_scr, w3_scr, w2_scr):
    i = pl.program_id(0)
    e = be_ref[i]
    prev = be_ref[jnp.maximum(i - 1, 0)]

    @pl.when((i == 0) | (e != prev))
    def _():
        w1_scr[...] = w1_ref[...].astype(bf16)
        w3_scr[...] = w3_ref[...].astype(bf16)
        w2_scr[...] = w2_ref[...].astype(bf16)

    @pl.when(i < nu_ref[0])
    def _():
        xb = xs_ref[...].astype(bf16)
        a = jnp.dot(xb, w1_scr[...], preferred_element_type=f32)
        b = jnp.dot(xb, w3_scr[...], preferred_element_type=f32)
        hid = (a * jax.nn.sigmoid(a) * b).astype(bf16)
        ys_ref[...] = jnp.dot(hid, w2_scr[...], preferred_element_type=f32)

    @pl.when(i >= nu_ref[0])
    def _():
        ys_ref[...] = jnp.zeros_like(ys_ref)


def _moe_experts(xs, block_e, n_used, w1, w3, w2, nb):
    m = MOE_BLOCK
    _, d, de = w1.shape
    grid_spec = pltpu.PrefetchScalarGridSpec(
        num_scalar_prefetch=2,
        grid=(nb,),
        in_specs=[pl.BlockSpec((m, d), lambda i, be, nu: (jnp.minimum(i, nu[0] - 1), 0)),
                  pl.BlockSpec((None, d, de), lambda i, be, nu: (be[i], 0, 0)),
                  pl.BlockSpec((None, d, de), lambda i, be, nu: (be[i], 0, 0)),
                  pl.BlockSpec((None, de, d), lambda i, be, nu: (be[i], 0, 0))],
        out_specs=pl.BlockSpec((m, d), lambda i, be, nu: (i, 0)),
        scratch_shapes=[pltpu.VMEM((d, de), bf16), pltpu.VMEM((d, de), bf16), pltpu.VMEM((de, d), bf16)],
    )
    return pl.pallas_call(
        _moe_kernel,
        grid_spec=grid_spec,
        out_shape=jax.ShapeDtypeStruct((nb * m, d), f32),
        compiler_params=_cparams(("arbitrary",)),
        name="moe_experts",
    )(block_e, n_used, xs, w1, w3, w2)


def _route_kernel(lg_ref, tri_ref, o_ref, cnt_ref, run_scr):
    @pl.when(pl.program_id(0) == 0)
    def _():
        run_scr[...] = jnp.zeros_like(run_scr)

    lg = lg_ref[...]
    lane = lax.broadcasted_iota(i32, lg.shape, 1)
    ninf = -jnp.inf

    def first_lane(mask):
        return jnp.min(jnp.where(mask, lane, lg.shape[1]), axis=-1, keepdims=True)

    gl = jnp.where(lane < MOE_GROUPS, lg, ninf)
    gmax = jnp.max(gl, axis=-1, keepdims=True)
    pg_top = 1.0 / jnp.sum(jnp.exp(gl - gmax), axis=-1, keepdims=True)
    lo = MOE_GROUPS + first_lane(gl == gmax) * MOE_PER_GROUP
    sel = (lane >= lo) & (lane < lo + MOE_PER_GROUP)
    el = jnp.where(sel, lg, ninf)
    ee = jnp.exp(el - jnp.max(el, axis=-1, keepdims=True))
    pe = jnp.where(sel, ee / jnp.sum(ee, axis=-1, keepdims=True), -1.0)
    p1 = jnp.max(pe, axis=-1, keepdims=True)
    i1 = first_lane(pe == p1)
    pe2 = jnp.where(lane == i1, -1.0, pe)
    p2 = jnp.max(pe2, axis=-1, keepdims=True)
    i2 = first_lane(pe2 == p2)
    den = p1 + p2
    e1 = i1 - MOE_GROUPS
    e2 = i2 - MOE_GROUPS
    hit1 = lane == e1
    hit2 = lane == e2
    onehot = jnp.where(hit1, 1.0, jnp.where(hit2, 1.0, 0.0))
    base = jnp.dot(tri_ref[...], onehot.astype(bf16), preferred_element_type=f32) + run_scr[...]
    r1 = jnp.sum(jnp.where(hit1, base, 0.0), axis=-1, keepdims=True)
    r2 = jnp.sum(jnp.where(hit2, base, 0.0), axis=-1, keepdims=True)
    run_scr[...] = run_scr[...] + jnp.sum(onehot, axis=0, keepdims=True)
    cnt_ref[...] = run_scr[...]
    vals = (e1.astype(f32), e2.astype(f32), pg_top * p1 / den, pg_top * p2 / den, r1, r2)
    out = jnp.zeros(lg.shape, f32)
    for j, v in enumerate(vals):
        out = jnp.where(lane == j, v, out)
    o_ref[...] = out


def _route(logits):
    t, width = logits.shape
    tri = jnp.tril(jnp.ones((TM, TM), f32), -1).astype(bf16)
    return pl.pallas_call(
        _route_kernel,
        grid=(t // TM,),
        in_specs=[pl.BlockSpec((TM, width), lambda i: (i, 0)), pl.BlockSpec((TM, TM), lambda i: (0, 0))],
        out_specs=[pl.BlockSpec((TM, width), lambda i: (i, 0)), pl.BlockSpec((1, width), lambda i: (0, 0))],
        out_shape=[jax.ShapeDtypeStruct((t, width), f32), jax.ShapeDtypeStruct((1, width), f32)],
        scratch_shapes=[pltpu.VMEM((1, width), f32)],
        compiler_params=_cparams(("arbitrary",)),
        name="route",
    )(logits, tri)


def _dispatch_kernel(dest_ref, h_ref, init_ref, xs_ref, sem):
    del init_ref
    i = pl.program_id(0)
    tm = h_ref.shape[0]

    def row_copy(r, k):
        slot = dest_ref[(i * tm + r) * MOE_TOPK + k]
        return pltpu.make_async_copy(h_ref.at[pl.ds(r, 1)], xs_ref.at[pl.ds(slot, 1)], sem)

    def issue(r, carry):
        for k in range(MOE_TOPK):
            row_copy(r, k).start()
        return carry

    lax.fori_loop(0, tm, issue, 0)

    def drain(r, carry):
        for k in range(MOE_TOPK):
            row_copy(r, k).wait()
        return carry

    lax.fori_loop(0, tm, drain, 0)


def _dispatch(h2, dest, nb):
    t, d = h2.shape
    rows = nb * MOE_BLOCK
    grid_spec = pltpu.PrefetchScalarGridSpec(
        num_scalar_prefetch=1,
        grid=(t // TM,),
        in_specs=[pl.BlockSpec((TM, d), lambda i, dst: (i, 0)), pl.BlockSpec(memory_space=pl.ANY)],
        out_specs=pl.BlockSpec(memory_space=pl.ANY),
        scratch_shapes=[pltpu.SemaphoreType.DMA(())],
    )
    return pl.pallas_call(
        _dispatch_kernel,
        grid_spec=grid_spec,
        out_shape=jax.ShapeDtypeStruct((rows, d), h2.dtype),
        input_output_aliases={2: 0},
        compiler_params=_cparams(("arbitrary",)),
        name="moe_dispatch",
    )(dest, h2, jnp.zeros((rows, d), h2.dtype))


def _combine_kernel(dest_ref, x_ref, rt_ref, g2_ref, gn_ref, ys_ref, o_ref, y_scr, sem, *,
                    tiles_per_batch, batch, final_norm):
    i = pl.program_id(0)
    tm = x_ref.shape[0]

    def row_copy(r, k):
        slot = dest_ref[(i * tm + r) * MOE_TOPK + k]
        return pltpu.make_async_copy(ys_ref.at[pl.ds(slot, 1)], y_scr.at[k, pl.ds(r, 1)], sem)

    def issue(r, carry):
        for k in range(MOE_TOPK):
            row_copy(r, k).start()
        return carry

    lax.fori_loop(0, tm, issue, 0)

    def drain(r, carry):
        for k in range(MOE_TOPK):
            row_copy(r, k).wait()
        return carry

    lax.fori_loop(0, tm, drain, 0)
    seg = _tile_segment(i, tiles_per_batch, batch)
    rt = rt_ref[...]
    y = rt[:, MOE_TOPK:MOE_TOPK + 1] * y_scr[0]
    for k in range(1, MOE_TOPK):
        y = y + rt[:, MOE_TOPK + k:MOE_TOPK + k + 1] * y_scr[k]
    x = x_ref[...] + g2_ref[pl.ds(seg, 1), :] * y
    if final_norm:
        ms = jnp.mean(x * x, axis=-1, keepdims=True)
        x = x * lax.rsqrt(ms + EPS) * gn_ref[...]
    o_ref[...] = x


def _combine(x, ys, dest, routed, mods, g_out, *, n_tiles, batch, seq, final_norm):
    d = x.shape[1]
    row = lambda i, dst: (i, 0)
    grid_spec = pltpu.PrefetchScalarGridSpec(
        num_scalar_prefetch=1,
        grid=(n_tiles,),
        in_specs=[pl.BlockSpec((TM, d), row),
                  pl.BlockSpec((TM, routed.shape[1]), row),
                  pl.BlockSpec((8, d), lambda i, dst: (0, 5)),
                  pl.BlockSpec((1, d), lambda i, dst: (0, 0)),
                  pl.BlockSpec(memory_space=pl.ANY)],
        out_specs=pl.BlockSpec((TM, d), row),
        scratch_shapes=[pltpu.VMEM((MOE_TOPK, TM, d), f32), pltpu.SemaphoreType.DMA(())],
    )
    return pl.pallas_call(
        functools.partial(_combine_kernel, tiles_per_batch=seq // TM, batch=batch, final_norm=final_norm),
        grid_spec=grid_spec,
        out_shape=jax.ShapeDtypeStruct((n_tiles * TM, d), f32),
        compiler_params=_cparams(("arbitrary",)),
        name="moe_combine",
    )(dest, x, routed, mods, g_out, ys)


def _moe(xn, h2, logits, mods, w1, w3, w2, g_out, *, n_tiles, batch, seq, final_norm):
    t = n_tiles * TM
    m = MOE_BLOCK
    routed, cnt = _route(logits)
    s = t * MOE_TOPK
    nb = s // m + MOE_EXPERTS
    expert = routed[:, 0:MOE_TOPK].astype(i32)
    rank = routed[:, 2 * MOE_TOPK:3 * MOE_TOPK].astype(i32)
    counts = cnt[0, :MOE_EXPERTS].astype(i32)
    padded = ((counts + m - 1) // m) * m
    pad_end = jnp.cumsum(padded)
    onehot = expert[:, :, None] == jnp.arange(MOE_EXPERTS, dtype=i32)[None, None, :]
    dest = (jnp.sum(jnp.where(onehot, (pad_end - padded)[None, None, :], 0), axis=-1) + rank).reshape(s)
    n_used = (pad_end[-1] // m).astype(i32)
    blk = jnp.arange(nb, dtype=i32)
    block_e = jnp.minimum(jnp.searchsorted(pad_end, jnp.minimum(blk, n_used - 1) * m, side='right'),
                          MOE_EXPERTS - 1).astype(i32)
    xs = _dispatch(h2, dest, nb)
    ys = _moe_experts(xs, block_e, n_used.reshape(1), w1, w3, w2, nb)
    return _combine(xn, ys, dest, routed, mods, g_out, n_tiles=n_tiles, batch=batch, seq=seq,
                    final_norm=final_norm)


HY_N2 = 128
HY_WB = 128


def _short_conv_kernel(x_ref, prev_ref, next_ref, w_ref, b_ref, v_ref, x1_ref, x2_ref, ext_scr, *, tiles_per_batch):
    j = pl.program_id(0) % tiles_per_batch
    tt = x_ref.shape[0]
    ext_scr[0:8, :] = jnp.where(j > 0, prev_ref[...], 0.0)
    ext_scr[8:8 + tt, :] = x_ref[...]
    ext_scr[8 + tt:16 + tt, :] = jnp.where(j < tiles_per_batch - 1, next_ref[...], 0.0)
    w = w_ref[...]
    width = v_ref.shape[1]
    for part, o_ref in enumerate((v_ref, x1_ref, x2_ref)):
        cols = slice(part * width, (part + 1) * width)
        o_ref[...] = (ext_scr[7:7 + tt, cols] * w[0:1, cols] + ext_scr[8:8 + tt, cols] * w[1:2, cols]
                      + ext_scr[9:9 + tt, cols] * w[2:3, cols] + b_ref[:, cols])


def _short_conv(hy, w, b, *, batch, seq):
    n_lat = batch * seq
    tt = 256
    width = hy.shape[1]
    out = jax.ShapeDtypeStruct((n_lat, width // 3), f32)
    o_spec = pl.BlockSpec((tt, width // 3), lambda i: (i, 0))
    return pl.pallas_call(
        functools.partial(_short_conv_kernel, tiles_per_batch=seq // tt),
        grid=(n_lat // tt,),
        in_specs=[pl.BlockSpec((tt, width), lambda i: (i, 0)),
                  pl.BlockSpec((8, width), lambda i: (jnp.maximum(i * (tt // 8) - 1, 0), 0)),
                  pl.BlockSpec((8, width), lambda i: ((i + 1) * (tt // 8), 0)),
                  pl.BlockSpec((3, width), lambda i: (0, 0)),
                  pl.BlockSpec((1, width), lambda i: (0, 0))],
        out_specs=[o_spec, o_spec, o_spec],
        out_shape=[out, out, out],
        scratch_shapes=[pltpu.VMEM((tt + 16, width), f32)],
        compiler_params=_cparams(("arbitrary",)),
        name="short_conv",
    )(hy, hy, hy, w, b.reshape(1, width))


def _hy_filter_kernel(z_ref, w1_ref, b1_ref, fr_ref, w2_ref, b2_ref, w3_ref, dl_ref, k_ref, nrm_ref):
    i = pl.program_id(0)
    z = z_ref[...]
    fr = fr_ref[...]
    h = jnp.sin(fr[0:1] * (jnp.dot(z.astype(bf16), w1_ref[...], preferred_element_type=f32) + b1_ref[...]))
    h = jnp.sin(fr[1:2] * (jnp.dot(h.astype(bf16), w2_ref[...], preferred_element_type=f32) + b2_ref[...]))
    kk = jnp.dot(h.astype(bf16), w3_ref[...], preferred_element_type=f32)
    width = dl_ref.shape[1]
    decay = jnp.exp(-z[:, 0:1] * jnp.abs(dl_ref[...]))
    tl = z.shape[0]
    row = i * tl + lax.broadcasted_iota(i32, (tl, 1), 0)
    tot = []
    for blk in range(kk.shape[1] // width):
        kb = kk[:, blk * width:(blk + 1) * width] * decay
        if blk % 2 == 1:
            kb = jnp.where(row == 0, 0.0, kb)
        k_ref[:, blk * width:(blk + 1) * width] = kb
        tot.append(jnp.sum(jnp.abs(kb), axis=0, keepdims=True))
    s = jnp.concatenate(tot, axis=1)

    @pl.when(i == 0)
    def _():
        nrm_ref[...] = s

    @pl.when(i > 0)
    def _():
        nrm_ref[...] = nrm_ref[...] + s


def _hy_filters(seq, w1, b1, freq, w2, b2, w3):
    t = jnp.linspace(0.0, 1.0, seq, dtype=f32)[:, None]
    bands = (HY_EMB - 1) // 2
    w = 2 * math.pi * jnp.arange(seq, dtype=f32)[:, None] / seq
    f = jnp.linspace(1e-4, bands - 1, bands, dtype=f32)[None, :]
    z = jnp.concatenate([t, jnp.cos(f * w), -jnp.sin(f * w), jnp.zeros((seq, 128 - HY_EMB), f32)], axis=-1)
    w1p = jnp.concatenate([w1, jnp.zeros((128 - HY_EMB, w1.shape[1]), f32)], axis=0).astype(bf16)
    hid = w1.shape[1]
    n_out = w3.shape[1]
    deltas = jnp.linspace(math.log(HY_TARGET) / HY_SLOW_DECAY, math.log(HY_TARGET) / HY_FAST_DECAY,
                          HY_WIDTH, dtype=f32).reshape(1, HY_WIDTH)
    tl = 512
    const = lambda a: pl.BlockSpec(a.shape, lambda i: (0,) * a.ndim)
    ins = [z, w1p, b1.reshape(1, hid), freq, w2.astype(bf16), b2.reshape(1, hid), w3.astype(bf16), deltas]
    return pl.pallas_call(
        _hy_filter_kernel,
        grid=(seq // tl,),
        in_specs=[pl.BlockSpec((tl, 128), lambda i: (i, 0))] + [const(a) for a in ins[1:]],
        out_specs=[pl.BlockSpec((tl, n_out), lambda i: (i, 0)), pl.BlockSpec((1, n_out), lambda i: (0, 0))],
        out_shape=[jax.ShapeDtypeStruct((seq, n_out), f32), jax.ShapeDtypeStruct((1, n_out), f32)],
        compiler_params=_cparams(("arbitrary",)),
        name="hy_filter",
    )(*ins)


def _dft_tables(seq):
    n2 = HY_N2
    n_tot = 2 * seq
    n1 = n_tot // n2
    nf = n1 // 2 + 1
    nfp = -(-nf // 8) * 8
    f1 = jnp.arange(nfp, dtype=i32)
    k1 = jnp.arange(n1 // 2, dtype=i32)
    th = (2 * math.pi / n1) * ((f1[:, None] * k1[None, :]) % n1).astype(f32)
    valid = (f1 < nf)[:, None]
    fwd = jnp.concatenate([jnp.where(valid, jnp.cos(th), 0.0), jnp.where(valid, -jnp.sin(th), 0.0)], axis=0)
    eye = jnp.eye(8, dtype=f32)
    fk = jnp.einsum('rk,jh->rjkh', fwd, eye).reshape(2 * nfp * 8, (n1 // 2) * 8)
    cw = jnp.where(valid, jnp.where((f1 == 0) | (f1 == n1 // 2), 1.0, 2.0)[:, None], 0.0) / n_tot
    inv = jnp.concatenate([(cw * jnp.cos(th)).T, (-cw * jnp.sin(th)).T], axis=1)
    ck = jnp.einsum('kr,jh->kjrh', inv, eye).reshape((n1 // 2) * 8, 2 * nfp * 8)
    fa = jnp.arange(nf, dtype=i32)[:, None, None]
    f2 = jnp.arange(n2, dtype=i32)[None, :, None]
    m2 = jnp.arange(n2, dtype=i32)[None, None, :]
    ph = (2 * math.pi / n_tot) * ((f2 * m2 * n1 + m2 * fa) % n_tot).astype(f32)
    gr, gi = jnp.cos(ph), -jnp.sin(ph)
    g = jnp.concatenate([jnp.concatenate([gr, -gi], axis=2), jnp.concatenate([gi, gr], axis=2)], axis=1)
    hr, hi = jnp.swapaxes(jnp.cos(ph), 1, 2), jnp.swapaxes(jnp.sin(ph), 1, 2)
    h = jnp.concatenate([jnp.concatenate([hr, -hi], axis=2), jnp.concatenate([hi, hr], axis=2)], axis=1)
    return fk.astype(bf16), ck.astype(bf16), g.astype(bf16), h.astype(bf16), nf, nfp


def _hy_rows_per_step(nf):
    return max(d for d in (5, 4, 3, 2, 1) if nf % d == 0)


def _dft_stage1(x_ref, fk_ref, a_scr):
    n1h, n2, wb = x_ref.shape

    def slab(s, carry):
        rows = pl.ds(pl.multiple_of(s * 8, 8), 8)
        xs = x_ref[:, rows, :].reshape(n1h * 8, wb).astype(bf16)
        a = jnp.dot(fk_ref[...], xs, preferred_element_type=f32)
        a_scr[:, rows, :] = a.reshape(a_scr.shape[0], 8, wb)
        return carry

    lax.fori_loop(0, n2 // 8, slab, 0)


def _dft_stage2(a_scr, g, f1, nfp):
    a = jnp.concatenate([a_scr[f1], a_scr[nfp + f1]], axis=0).astype(bf16)
    return jnp.dot(g, a, preferred_element_type=f32)


def _hy_spectrum_kernel(kf_ref, kb_ref, fk_ref, g_ref, o_ref, af_scr, ab_scr, *, nfp):
    fs = pl.program_id(2)

    @pl.when(fs == 0)
    def _():
        _dft_stage1(kf_ref, fk_ref, af_scr)
        _dft_stage1(kb_ref, fk_ref, ab_scr)

    fb = g_ref.shape[0]
    n2 = kf_ref.shape[1]
    for j in range(fb):
        f1 = fs * fb + j
        sf = _dft_stage2(af_scr, g_ref[j], f1, nfp)
        sb = _dft_stage2(ab_scr, g_ref[j], f1, nfp)
        o_ref[0, j] = sf[:n2] + sb[:n2]
        o_ref[1, j] = sf[n2:] - sb[n2:]


def _hy_spectrum(k, fk, g, nf, nfp, seq):
    n2, wb = HY_N2, HY_WB
    n1h = seq // n2
    fb = _hy_rows_per_step(nf)
    nwb = HY_WIDTH // wb
    k3 = k.reshape(n1h, n2, k.shape[1])
    return pl.pallas_call(
        functools.partial(_hy_spectrum_kernel, nfp=nfp),
        grid=(HY_ORDER, nwb, nf // fb),
        in_specs=[pl.BlockSpec((n1h, n2, wb), lambda o, w, f: (0, 0, (2 * o) * nwb + w)),
                  pl.BlockSpec((n1h, n2, wb), lambda o, w, f: (0, 0, (2 * o + 1) * nwb + w)),
                  pl.BlockSpec(fk.shape, lambda o, w, f: (0, 0)),
                  pl.BlockSpec((fb, 2 * n2, 2 * n2), lambda o, w, f: (f, 0, 0))],
        out_specs=pl.BlockSpec((None, 2, fb, n2, wb), lambda o, w, f: (o, 0, f, 0, w)),
        out_shape=jax.ShapeDtypeStruct((HY_ORDER, 2, nf, n2, HY_WIDTH), f32),
        scratch_shapes=[pltpu.VMEM((2 * nfp, n2, wb), f32), pltpu.VMEM((2 * nfp, n2, wb), f32)],
        compiler_params=_cparams(("arbitrary",) * 3),
        name="hy_spectrum",
    )(k3, k3, fk, g)


def _hy_conv_kernel(x_ref, gate_ref, kf_ref, g_ref, h_ref, fk_ref, ck_ref, bias_ref, inv_ref, o_ref, a_scr, *, nfp):
    fs = pl.program_id(2)
    n1h, n2, wb = x_ref.shape

    @pl.when(fs == 0)
    def _():
        _dft_stage1(x_ref, fk_ref, a_scr)

    fb = g_ref.shape[0]
    for j in range(fb):
        f1 = fs * fb + j
        xs = _dft_stage2(a_scr, g_ref[j], f1, nfp)
        xr, xi = xs[:n2], xs[n2:]
        kr, ki = kf_ref[0, j], kf_ref[1, j]
        y = jnp.concatenate([xr * kr - xi * ki, xr * ki + xi * kr], axis=0).astype(bf16)
        bb = jnp.dot(h_ref[j], y, preferred_element_type=f32)
        a_scr[f1] = bb[:n2]
        a_scr[nfp + f1] = bb[n2:]

    @pl.when(fs == pl.num_programs(2) - 1)
    def _():
        bias = bias_ref[...].reshape(1, 1, wb)
        inv = inv_ref[...].reshape(1, 1, wb)

        def slab(s, carry):
            rows = pl.ds(pl.multiple_of(s * 8, 8), 8)
            bs = a_scr[:, rows, :].reshape(2 * nfp * 8, wb).astype(bf16)
            y = jnp.dot(ck_ref[...], bs, preferred_element_type=f32).reshape(n1h, 8, wb)
            o_ref[:, rows, :] = (gate_ref[:, rows, :] * (y * inv + x_ref[:, rows, :] * bias)).astype(o_ref.dtype)
            return carry

        lax.fori_loop(0, n2 // 8, slab, 0)


def _hy_conv(x, gate, kf, order, tabs, bias, inv_norm, out_dtype, *, batch, seq):
    fk, ck, g, h, nf, nfp = tabs
    n2, wb = HY_N2, HY_WB
    n1h = seq // n2
    fb = _hy_rows_per_step(nf)
    width = x.shape[1]
    big = pl.BlockSpec((n1h, n2, wb), lambda b, w, f: (b, 0, w))
    out = pl.pallas_call(
        functools.partial(_hy_conv_kernel, nfp=nfp),
        grid=(batch, width // wb, nf // fb),
        in_specs=[big, big,
                  pl.BlockSpec((None, 2, fb, n2, wb), lambda b, w, f: (order, 0, f, 0, w)),
                  pl.BlockSpec((fb, 2 * n2, 2 * n2), lambda b, w, f: (f, 0, 0)),
                  pl.BlockSpec((fb, 2 * n2, 2 * n2), lambda b, w, f: (f, 0, 0)),
                  pl.BlockSpec(fk.shape, lambda b, w, f: (0, 0)),
                  pl.BlockSpec(ck.shape, lambda b, w, f: (0, 0)),
                  pl.BlockSpec((1, wb), lambda b, w, f: (0, w)),
                  pl.BlockSpec((1, wb), lambda b, w, f: (0, w))],
        out_specs=big,
        out_shape=jax.ShapeDtypeStruct((batch * n1h, n2, width), out_dtype),
        scratch_shapes=[pltpu.VMEM((2 * nfp, n2, wb), f32)],
        compiler_params=_cparams(("arbitrary",) * 3),
        name="hy_conv",
    )(x.reshape(batch * n1h, n2, width), gate.reshape(batch * n1h, n2, width), kf, g, h, fk, ck,
      bias.reshape(1, width), inv_norm.reshape(1, width))
    return out.reshape(batch * seq, width)


def _hyena(hy, conv_w, conv_b, f_w1, f_b1, f_freq, f_w2, f_b2, f_w3, hy_bias, *, batch, seq):
    v, x1, x2 = _short_conv(hy, conv_w, conv_b, batch=batch, seq=seq)
    k, sums = _hy_filters(seq, f_w1, f_b1, f_freq, f_w2, f_b2, f_w3)
    sums = sums.reshape(HY_ORDER, 2, HY_WIDTH)
    inv_norm = 1.0 / (sums[:, 0] + sums[:, 1])
    tabs = _dft_tables(seq)
    kf = _hy_spectrum(k, tabs[0], tabs[2], tabs[4], tabs[5], seq)
    z = _hy_conv(v, x1, kf, 0, tabs, hy_bias[0], inv_norm[0], f32, batch=batch, seq=seq)
    return _hy_conv(z, x2, kf, 1, tabs, hy_bias[1], inv_norm[1], bf16, batch=batch, seq=seq)


S5_TT = 256
S5_GB = 8


def _gelu_tanh(x):
    return 0.5 * x * (1.0 + jnp.tanh(0.7978845608028654 * (x + 0.044715 * x * x * x)))


def _s5_kernel(u_ref, perm_ref, permt_ref, wb_ref, wc_ref, are_ref, aim_ref, pre_ref, pim_ref, *rest, fuse_glu):
    if fuse_glu:
        yf_ref, d_ref, gw_ref, gb_ref, o_ref, st_scr, bu_scr, y_scr = rest
    else:
        o_ref, st_scr, bu_scr, y_scr = rest
    tt = u_ref.shape[0]
    steps = tt // 8
    half = bu_scr.shape[1] // 2

    @pl.when(pl.program_id(1) == 0)
    def _():
        st_scr[...] = jnp.zeros_like(st_scr)

    up = jnp.dot(perm_ref[...], u_ref[...].astype(bf16), preferred_element_type=f32).astype(bf16)
    for gb in range(wb_ref.shape[0]):
        bu_scr[...] = jnp.dot(up[:, gb * 128:(gb + 1) * 128], wb_ref[gb], preferred_element_type=f32)
        a_re = jnp.broadcast_to(are_ref[gb], (8, half))
        a_im = jnp.broadcast_to(aim_ref[gb], (8, half))

        def scan_step(k, h):
            hr, hi = h
            rows = pl.ds(pl.multiple_of(k * 8, 8), 8)
            nr = a_re * hr - a_im * hi + bu_scr[rows, 0:half]
            ni = a_re * hi + a_im * hr + bu_scr[rows, half:]
            bu_scr[rows, 0:half] = nr
            bu_scr[rows, half:] = ni
            return nr, ni

        zero = jnp.zeros((8, half), f32)
        hr, hi = lax.fori_loop(0, steps, scan_step, (zero, zero), unroll=4)

        pw_r = pre_ref[gb, steps - 1:steps, :]
        pw_i = pim_ref[gb, steps - 1:steps, :]
        cur_r = st_scr[gb, :, 0:half]
        cur_i = st_scr[gb, :, half:]
        car_r, car_i = [], []
        for j in range(8):
            car_r.append(cur_r)
            car_i.append(cur_i)
            cur_r, cur_i = (hr[j:j + 1] + pw_r * cur_r - pw_i * cur_i,
                            hi[j:j + 1] + pw_r * cur_i + pw_i * cur_r)
        st_scr[gb, :, 0:half] = cur_r
        st_scr[gb, :, half:] = cur_i
        cr = jnp.concatenate(car_r, axis=0)
        ci = jnp.concatenate(car_i, axis=0)

        def fix_step(k, carry):
            rows = pl.ds(pl.multiple_of(k * 8, 8), 8)
            pr = pre_ref[gb, pl.ds(k, 1), :]
            pi = pim_ref[gb, pl.ds(k, 1), :]
            bu_scr[rows, 0:half] = bu_scr[rows, 0:half] + (pr * cr - pi * ci)
            bu_scr[rows, half:] = bu_scr[rows, half:] + (pr * ci + pi * cr)
            return carry

        lax.fori_loop(0, steps, fix_step, 0, unroll=4)
        y_scr[:, gb * 128:(gb + 1) * 128] = jnp.dot(bu_scr[...].astype(bf16), wc_ref[gb],
                                                      preferred_element_type=f32)
    y = jnp.dot(permt_ref[...], y_scr[...], preferred_element_type=f32, precision=lax.Precision.HIGHEST)
    if fuse_glu:
        g = _gelu_tanh(y + yf_ref[...] + u_ref[...] * d_ref[...])
        z = jnp.dot(g.astype(bf16), gw_ref[...], preferred_element_type=f32) + gb_ref[...]
        o_ref[...] = (g * jax.nn.sigmoid(z)).astype(o_ref.dtype)
    else:
        o_ref[...] = y


def _s5_tables(a_re, a_im, log_dt, b_re, b_im, c_re, c_im, steps):
    ng, ns = a_re.shape
    nb = ng // S5_GB
    lam = lax.complex(a_re, a_im)
    dt = jnp.exp(log_dt)[:, None]
    lam_bar = jnp.exp(lam * dt)
    b_bar = ((lam_bar - 1) / lam)[..., None] * lax.complex(b_re, b_im)
    cm = lax.complex(c_re, c_im)
    pw = jnp.exp((lam * dt)[None] * jnp.arange(1, steps + 1, dtype=f32)[:, None, None])
    eye = jnp.eye(S5_GB, dtype=f32)
    bb = b_bar.reshape(nb, S5_GB, ns, S5_GROUP)
    wb = jnp.concatenate([jnp.einsum('bgpc,gh->bgchp', part, eye).reshape(nb, S5_GB * S5_GROUP, S5_GB * ns)
                          for part in (jnp.real(bb), jnp.imag(bb))], axis=-1)
    cc = cm.reshape(nb, S5_GB, S5_GROUP, ns)
    wc = jnp.concatenate([jnp.einsum('bgcp,gh->bgphc', part, eye).reshape(nb, S5_GB * ns, S5_GB * S5_GROUP)
                          for part in (jnp.real(cc), -jnp.imag(cc))], axis=1)
    lb = lam_bar.reshape(nb, 1, S5_GB * ns)
    pwb = pw.reshape(steps, nb, S5_GB * ns).transpose(1, 0, 2)
    return (wb.astype(bf16), wc.astype(bf16), jnp.real(lb), jnp.imag(lb), jnp.real(pwb), jnp.imag(pwb))


def _s5_perm(tt, reverse):
    steps = tt // 8
    dst = jnp.arange(tt)
    t = (dst % 8) * steps + dst // 8
    src = tt - 1 - t if reverse else t
    return jnp.arange(tt)[None, :] == src[:, None]


def _s5(u, a_re, a_im, log_dt, b_re, b_im, c_re, c_im, s5_d, glu_w, glu_b, *, batch, seq, ctx_len):
    r, width = u.shape
    tt = S5_TT
    assert ctx_len == tt and seq % tt == 0
    n_lat = seq // tt
    ctx_blk0 = (batch * seq) // tt
    y = None
    for di in range(2):
        rev = di == 1
        wb, wc, lr, li, pr, pi = _s5_tables(a_re[di], a_im[di], log_dt[di], b_re[di], b_im[di],
                                            c_re[di], c_im[di], tt // 8)
        perm = _s5_perm(tt, rev)

        def tile_map(b, i, rev=rev):
            lat = b * n_lat + (n_lat - i if rev else i - 1)
            return (jnp.where(i == 0, ctx_blk0 + b, lat), 0)

        row_spec = pl.BlockSpec((tt, width), tile_map)
        const = lambda a: pl.BlockSpec(a.shape, lambda b, i: (0,) * a.ndim)
        ins = [u, perm.astype(bf16), perm.T.astype(f32), wb, wc, lr, li, pr, pi]
        in_specs = [row_spec] + [const(a) for a in ins[1:]]
        if rev:
            extra = [y, s5_d.reshape(1, width), glu_w.astype(bf16), glu_b.reshape(1, width)]
            in_specs += [row_spec] + [const(a) for a in extra[1:]]
            ins += extra
        y = pl.pallas_call(
            functools.partial(_s5_kernel, fuse_glu=rev),
            grid=(batch, n_lat + 1),
            in_specs=in_specs,
            out_specs=row_spec,
            out_shape=jax.ShapeDtypeStruct((r, width), bf16 if rev else f32),
            scratch_shapes=[pltpu.VMEM((wb.shape[0], 1, wb.shape[2]), f32),
                            pltpu.VMEM((tt, wb.shape[2]), f32),
                            pltpu.VMEM((tt, width), f32)],
            compiler_params=_cparams(("arbitrary", "arbitrary")),
            name="s5_bwd_glu" if rev else "s5_fwd",
        )(*ins)
    return y


def _rope_tables(seq):
    rows = seq // GRID_W
    row = jnp.repeat(jnp.arange(rows, dtype=f32), GRID_W)
    col = jnp.tile(jnp.arange(GRID_W, dtype=f32), rows)
    n_freq = DIFF_QK // 4
    inv = ROPE_BASE ** (-jnp.arange(n_freq, dtype=f32) / n_freq)
    ar, ac = row[:, None] * inv, col[:, None] * inv
    cos = jnp.concatenate([jnp.cos(ar), jnp.cos(ar), jnp.cos(ac), jnp.cos(ac)], axis=-1)
    sin = jnp.concatenate([-jnp.sin(ar), jnp.sin(ar), -jnp.sin(ac), jnp.sin(ac)], axis=-1)
    cos = jnp.concatenate([jnp.tile(cos, (1, 2)), jnp.ones((TM, 128), f32)], axis=0)
    sin = jnp.concatenate([jnp.tile(sin, (1, 2)), jnp.zeros((TM, 128), f32)], axis=0)
    return cos, sin


def kernel(x, c, ctx, c_ctx, ada_w, ada_b, norm_mix_g, norm_ffn_g, norm_out_g, ab_w_in, ab_w_out, diff_lam, diff_subln_g, pool_w, pool_scale, cd_w_in, cd_w_out, hy_conv_w, hy_conv_b, hy_f_w1, hy_f_b1, hy_f_freq, hy_f_w2, hy_f_b2, hy_f_w3, hy_bias, s5_a_re, s5_a_im, s5_log_dt, s5_b_re, s5_b_im, s5_c_re, s5_c_im, s5_d, glu_w, glu_b, moe_w_rg, moe_b_rg, moe_w_re, moe_b_re, moe_w1, moe_w3, moe_w2):
    bsz, seq, d = x.shape
    cl = ctx.shape[1]
    depth = ada_w.shape[0]
    n_lat = bsz * seq
    assert seq % TM == 0 and (bsz * cl) % TM == 0 and bsz + 1 <= 8
    xs = jnp.concatenate([x.reshape(n_lat, d), ctx.reshape(bsz * cl, d)], axis=0)
    r = xs.shape[0]
    cond = jnp.zeros((8, d), f32).at[0:bsz].set(c).at[bsz].set(c_ctx)
    mods = _ada_mods(cond, ada_w, ada_b)
    geom = dict(batch=bsz, seq=seq)

    for l in range(depth):
        last = l == depth - 1
        i = l // 2
        n_tiles = (n_lat if last else r) // TM
        g_mix = norm_mix_g[l].reshape(1, d)
        if l % 2 == 0:
            lam_init = 0.8 - 0.6 * math.exp(-0.3 * l)
            cos, sin = _rope_tables(seq)
            q, k, v, p = _proj_rope(xs, g_mix, mods[l], ab_w_in[i].astype(bf16), cos, sin, **geom)
            mix_a = _diff_attention(q, k, v, diff_lam[i], diff_subln_g[i], lam_init, ctx_len=cl, **geom)
            mix_b = _pool(p, pool_w[i], pool_scale[i], ctx_len=cl, **geom)
            w_out = ab_w_out[i]
        else:
            hw = (HY_ORDER + 1) * HY_WIDTH
            hy, u = _proj(xs, g_mix, mods[l], cd_w_in[i].astype(bf16), hw, **geom)
            mix_a = _hyena(hy, hy_conv_w[i], hy_conv_b[i], hy_f_w1[i], hy_f_b1[i], hy_f_freq[i],
                           hy_f_w2[i], hy_f_b2[i], hy_f_w3[i], hy_bias[i], **geom)
            mix_b = _s5(u, s5_a_re[i], s5_a_im[i], s5_log_dt[i], s5_b_re[i], s5_b_im[i],
                        s5_c_re[i], s5_c_im[i], s5_d[i], glu_w[i], glu_b[i], ctx_len=cl, **geom)
            w_out = cd_w_out[i]
        w_r = jnp.zeros((d, 128), f32).at[:, :MOE_GROUPS].set(moe_w_rg[l]) \
            .at[:, MOE_GROUPS:MOE_GROUPS + MOE_EXPERTS].set(moe_w_re[l])
        b_r = jnp.zeros((1, 128), f32).at[0, :MOE_GROUPS].set(moe_b_rg[l]) \
            .at[0, MOE_GROUPS:MOE_GROUPS + MOE_EXPERTS].set(moe_b_re[l])
        xn, h2, logits = _outproj(mix_a, mix_b, xs, w_out.astype(bf16), mods[l], norm_ffn_g[l].reshape(1, d),
                                  w_r, b_r, n_tiles=n_tiles, **geom)
        xs = _moe(xn, h2, logits, mods[l], moe_w1[l], moe_w3[l], moe_w2[l], norm_out_g.reshape(1, d),
                  n_tiles=n_tiles, final_norm=last, **geom)
    return xs[:n_lat].reshape(bsz, seq, d)
```

```python
import functools
import math

import jax
import jax.numpy as jnp
from jax import lax
from jax.experimental import pallas as pl
from jax.experimental.pallas import tpu as pltpu

f32, bf16, i32 = jnp.float32, jnp.bfloat16, jnp.int32

EPS = 1e-6
GRID_W = 64
DIFF_QK = 64
DIFF_V = 128
ROPE_BASE = 10000.0
POOL_WINDOWS = (2, 4, 8, 16)
POOL_GROUP = 256
HY_WIDTH = 1024
HY_ORDER = 2
HY_EMB = 33
HY_FAST_DECAY = 0.3
HY_SLOW_DECAY = 1.5
HY_TARGET = 1e-2
S5_GROUP = 16
S5_GROUPS = 64
S5_STATE = 64
MOE_GROUPS = 8
MOE_PER_GROUP = 8
MOE_EXPERTS = 64
MOE_TOPK = 2
MOE_BLOCK = 256

VMEM_LIMIT = 56 * 1024 * 1024
TM = 512
LOG2E = 1.4426950408889634


def _cparams(sem):
    return pltpu.CompilerParams(dimension_semantics=sem, vmem_limit_bytes=VMEM_LIMIT)


def _norm_mod(x, g, shift, scale):
    ms = jnp.mean(x * x, axis=-1, keepdims=True)
    return (x * lax.rsqrt(ms + EPS) * g) * (1.0 + scale) + shift


def _tile_segment(i, lat_tiles_per_batch, batch):
    seg = jnp.zeros((), i32)
    for b in range(1, batch + 1):
        seg = seg + (i >= b * lat_tiles_per_batch).astype(i32)
    return seg


def _ada_kernel(s_ref, w_ref, b_ref, o_ref):
    s = s_ref[...]
    a = (s * jax.nn.sigmoid(s)).astype(bf16)
    o_ref[...] = jnp.dot(a, w_ref[...].astype(bf16), preferred_element_type=f32) + b_ref[...]


def _ada_mods(s, ada_w, ada_b):
    depth, d, n = ada_w.shape
    tn = 1024
    return pl.pallas_call(
        _ada_kernel,
        grid=(depth, n // tn),
        in_specs=[pl.BlockSpec((8, d), lambda l, j: (0, 0)),
                  pl.BlockSpec((None, d, tn), lambda l, j: (l, 0, j)),
                  pl.BlockSpec((None, 1, tn), lambda l, j: (l, 0, j))],
        out_specs=pl.BlockSpec((None, 8, tn), lambda l, j: (l, 0, j)),
        out_shape=jax.ShapeDtypeStruct((depth, 8, n), f32),
        compiler_params=_cparams(("arbitrary", "arbitrary")),
        name="ada_mods",
    )(s, ada_w, ada_b.reshape(depth, 1, n))


def _proj_rope_kernel(x_ref, g_ref, sh_ref, sc_ref, w_ref, cos_ref, sin_ref,
                      q_ref, k_ref, v_ref, p_ref, h_scr, *, tiles_per_batch, batch, qscale):
    seg = _tile_segment(pl.program_id(0), tiles_per_batch, batch)
    h = _norm_mod(x_ref[...], g_ref[...], sh_ref[pl.ds(seg, 1), :], sc_ref[pl.ds(seg, 1), :])
    h_scr[...] = h.astype(bf16)
    cos = cos_ref[...]
    sin = sin_ref[...]
    tm = cos.shape[0]
    lane = lax.broadcasted_iota(i32, (tm, 128), 1)
    first = (lane % 32) < 16

    def rope(r):
        sw = jnp.where(first, pltpu.roll(r, 112, 1), pltpu.roll(r, 16, 1))
        return r * cos + sw * sin

    heads = q_ref.shape[0]
    nc = 512
    for c in range(w_ref.shape[1] // nc):
        res = jnp.dot(h_scr[...], w_ref[:, c * nc:(c + 1) * nc], preferred_element_type=f32)
        for s in range(nc // 128):
            r = res[:, s * 128:(s + 1) * 128]
            col = (c * nc) // 128 + s
            if col < heads:
                q_ref[col] = (rope(r) * qscale).astype(bf16)
            elif col < 2 * heads:
                k_ref[col - heads] = rope(r).astype(bf16)
            elif col < 3 * heads:
                v_ref[col - 2 * heads] = r.astype(bf16)
            else:
                c0 = (col - 3 * heads) * 128
                p_ref[:, c0:c0 + 128] = r


def _proj_rope(x, g, mods, w, cos, sin, *, batch, seq):
    r, d = x.shape
    heads = 8
    lat_tiles = seq // TM
    n_tiles = r // TM

    def tab_map(i):
        return (jnp.where(i < batch * lat_tiles, i % lat_tiles, lat_tiles), 0)

    hm = jax.ShapeDtypeStruct((heads, r, 128), bf16)
    hm_spec = pl.BlockSpec((heads, TM, 128), lambda i: (0, i, 0))
    return pl.pallas_call(
        functools.partial(_proj_rope_kernel, tiles_per_batch=lat_tiles, batch=batch,
                          qscale=LOG2E * DIFF_QK ** -0.5),
        grid=(n_tiles,),
        in_specs=[pl.BlockSpec((TM, d), lambda i: (i, 0)),
                  pl.BlockSpec((1, d), lambda i: (0, 0)),
                  pl.BlockSpec((8, d), lambda i: (0, 0)),
                  pl.BlockSpec((8, d), lambda i: (0, 1)),
                  pl.BlockSpec(w.shape, lambda i: (0, 0), pipeline_mode=pl.Buffered(1)),
                  pl.BlockSpec((TM, 128), tab_map),
                  pl.BlockSpec((TM, 128), tab_map)],
        out_specs=[hm_spec, hm_spec, hm_spec, pl.BlockSpec((TM, 1024), lambda i: (i, 0))],
        out_shape=[hm, hm, hm, jax.ShapeDtypeStruct((r, 1024), f32)],
        scratch_shapes=[pltpu.VMEM((TM, d), bf16)],
        compiler_params=_cparams(("arbitrary",)),
        name="proj_rope",
    )(x, g, mods, mods, w, cos, sin)


def _proj_kernel(x_ref, g_ref, sh_ref, sc_ref, w_ref, hy_ref, u_ref, h_scr, *, tiles_per_batch, batch):
    seg = _tile_segment(pl.program_id(0), tiles_per_batch, batch)
    h = _norm_mod(x_ref[...], g_ref[...], sh_ref[pl.ds(seg, 1), :], sc_ref[pl.ds(seg, 1), :])
    h_scr[...] = h.astype(bf16)
    nc = 512
    n_hy = hy_ref.shape[1]
    for c in range(w_ref.shape[1] // nc):
        res = jnp.dot(h_scr[...], w_ref[:, c * nc:(c + 1) * nc], preferred_element_type=f32)
        if c * nc < n_hy:
            hy_ref[:, c * nc:(c + 1) * nc] = res
        else:
            u_ref[:, c * nc - n_hy:(c + 1) * nc - n_hy] = res


def _proj(x, g, mods, w, n_hy, *, batch, seq):
    r, d = x.shape
    n = w.shape[1]
    return pl.pallas_call(
        functools.partial(_proj_kernel, tiles_per_batch=seq // TM, batch=batch),
        grid=(r // TM,),
        in_specs=[pl.BlockSpec((TM, d), lambda i: (i, 0)),
                  pl.BlockSpec((1, d), lambda i: (0, 0)),
                  pl.BlockSpec((8, d), lambda i: (0, 0)),
                  pl.BlockSpec((8, d), lambda i: (0, 1)),
                  pl.BlockSpec(w.shape, lambda i: (0, 0), pipeline_mode=pl.Buffered(1))],
        out_specs=[pl.BlockSpec((TM, n_hy), lambda i: (i, 0)), pl.BlockSpec((TM, n - n_hy), lambda i: (i, 0))],
        out_shape=[jax.ShapeDtypeStruct((r, n_hy), f32), jax.ShapeDtypeStruct((r, n - n_hy), f32)],
        scratch_shapes=[pltpu.VMEM((TM, d), bf16)],
        compiler_params=_cparams(("arbitrary",)),
        name="proj",
    )(x, g, mods, mods, w)


def _attn_kernel(lv_ref, g_ref, q_ref, kc_ref, vc_ref, *rest, n_lat, tk, lam_init):
    if n_lat:
        kl_ref, vl_ref, o_ref = rest
    else:
        (o_ref,) = rest
    lv = lv_ref[...]
    lam = (jnp.exp(jnp.sum(lv[0:1] * lv[1:2], axis=-1, keepdims=True))
           - jnp.exp(jnp.sum(lv[2:3] * lv[3:4], axis=-1, keepdims=True)) + lam_init)
    q = q_ref[...]
    tq = q.shape[0]
    lane = lax.broadcasted_iota(i32, q.shape, 1)
    zero = jnp.zeros_like(q)
    qs = (jnp.where(lane < DIFF_QK, q, zero), jnp.where(lane >= DIFF_QK, q, zero))
    nt = (((1,), (1,)), ((), ()))

    def update(carry, k, v):
        new = []
        for m in range(2):
            mx, l, acc = carry[3 * m:3 * m + 3]
            s = lax.dot_general(qs[m], k, nt, preferred_element_type=f32)
            mn = jnp.maximum(mx, jnp.max(s, axis=-1, keepdims=True))
            alpha = jnp.exp2(mx - mn)
            p = jnp.exp2(s - mn)
            l = alpha * l + jnp.sum(p, axis=-1, keepdims=True)
            acc = alpha * acc + jnp.dot(p.astype(bf16), v, preferred_element_type=f32)
            new += [mn, l, acc]
        return tuple(new)

    init = (jnp.full((tq, 1), -jnp.inf, f32), jnp.zeros((tq, 1), f32), jnp.zeros((tq, DIFF_V), f32)) * 2
    carry = update(init, kc_ref[...], vc_ref[...])
    if n_lat:
        def body(j, carry):
            off = pl.multiple_of(j * tk, tk)
            return update(carry, kl_ref[pl.ds(off, tk), :], vl_ref[pl.ds(off, tk), :])
        carry = lax.fori_loop(0, n_lat, body, carry)
    _, l0, a0, _, l1, a1 = carry
    o = a0 / l0 - lam * (a1 / l1)
    ms = jnp.mean(o * o, axis=-1, keepdims=True)
    o_ref[...] = ((o * lax.rsqrt(ms + EPS) * g_ref[...]) * (1.0 - lam_init)).astype(o_ref.dtype)


def _diff_attention(q, k, v, lam_vecs, subln_g, lam_init, *, batch, seq, ctx_len):
    heads, r, _ = q.shape
    tq, tk = 512, 512
    ctx_blk0 = (batch * seq) // ctx_len
    small = [pl.BlockSpec((4, DIFF_QK), lambda *_: (0, 0)), pl.BlockSpec((1, DIFF_V), lambda *_: (0, 0))]
    kc_spec = pl.BlockSpec((None, ctx_len, 128), lambda b, h, *_: (h, ctx_blk0 + b, 0))
    g2 = subln_g.reshape(1, DIFF_V)
    lat = pl.pallas_call(
        functools.partial(_attn_kernel, n_lat=seq // tk, tk=tk, lam_init=lam_init),
        grid=(batch, heads, seq // tq),
        in_specs=small + [pl.BlockSpec((None, tq, 128), lambda b, h, i: (h, b * (seq // tq) + i, 0)),
                          kc_spec, kc_spec,
                          pl.BlockSpec((None, seq, 128), lambda b, h, i: (h, b, 0)),
                          pl.BlockSpec((None, seq, 128), lambda b, h, i: (h, b, 0))],
        out_specs=pl.BlockSpec((tq, 128), lambda b, h, i: (b * (seq // tq) + i, h)),
        out_shape=jax.ShapeDtypeStruct((batch * seq, heads * DIFF_V), bf16),
        compiler_params=_cparams(("arbitrary",) * 3),
        name="diff_attn_lat",
    )(lam_vecs, g2, q, k, v, k, v)
    ctx = pl.pallas_call(
        functools.partial(_attn_kernel, n_lat=0, tk=tk, lam_init=lam_init),
        grid=(batch, heads),
        in_specs=small + [kc_spec, kc_spec, kc_spec],
        out_specs=pl.BlockSpec((ctx_len, 128), lambda b, h: (b, h)),
        out_shape=jax.ShapeDtypeStruct((batch * ctx_len, heads * DIFF_V), bf16),
        compiler_params=_cparams(("arbitrary",) * 2),
        name="diff_attn_ctx",
    )(lam_vecs, g2, q, k, v)
    return jnp.concatenate([lat, ctx], axis=0)


def _pool_kernel(p_ref, prev_ref, next_ref, w_ref, scale_ref, o_ref, ext_scr, *,
                 tiles_per_batch, batch, seq, ctx_len):
    i = pl.program_id(0)
    tp = p_ref.shape[0]
    is_ctx = i >= batch * tiles_per_batch
    j = jnp.where(is_ctx, 0, i % tiles_per_batch)
    seg_len = jnp.where(is_ctx, ctx_len, seq)
    last = jnp.where(is_ctx, 0, tiles_per_batch - 1)
    ext_scr[0:8, :] = jnp.where(j > 0, prev_ref[...], 0.0)
    ext_scr[8:8 + tp, :] = p_ref[...]
    ext_scr[8 + tp:16 + tp, :] = jnp.where(j < last, next_ref[...], 0.0)
    pos = j * tp + lax.broadcasted_iota(i32, (tp, 1), 0)
    for gi, w in enumerate(POOL_WINDOWS):
        c0 = gi * POOL_GROUP
        half = w // 2
        tot = ext_scr[8 - half:8 - half + tp, c0:c0 + POOL_GROUP]
        for dlt in range(1 - half, half):
            tot = tot + ext_scr[8 + dlt:8 + dlt + tp, c0:c0 + POOL_GROUP]
        cnt = (jnp.minimum(pos - half + w, seg_len) - jnp.maximum(pos - half, 0)).astype(f32)
        dlt_mean = (tot / cnt - ext_scr[8:8 + tp, c0:c0 + POOL_GROUP]).astype(bf16)
        y = jnp.dot(dlt_mean, w_ref[gi].astype(bf16), preferred_element_type=f32)
        o_ref[:, c0:c0 + POOL_GROUP] = (y * scale_ref[:, c0:c0 + POOL_GROUP]).astype(o_ref.dtype)


def _pool(p, pool_w, pool_scale, *, batch, seq, ctx_len):
    r, width = p.shape
    tp = ctx_len
    tpb = seq // tp
    return pl.pallas_call(
        functools.partial(_pool_kernel, tiles_per_batch=tpb, batch=batch, seq=seq, ctx_len=ctx_len),
        grid=(r // tp,),
        in_specs=[pl.BlockSpec((tp, width), lambda i: (i, 0)),
                  pl.BlockSpec((8, width), lambda i: (jnp.maximum(i * (tp // 8) - 1, 0), 0)),
                  pl.BlockSpec((8, width), lambda i: (jnp.minimum((i + 1) * (tp // 8), r // 8 - 1), 0)),
                  pl.BlockSpec(pool_w.shape, lambda i: (0, 0, 0)),
                  pl.BlockSpec((1, width), lambda i: (0, 0))],
        out_specs=pl.BlockSpec((tp, width), lambda i: (i, 0)),
        out_shape=jax.ShapeDtypeStruct((r, width), bf16),
        scratch_shapes=[pltpu.VMEM((tp + 16, width), f32)],
        compiler_params=_cparams(("arbitrary",)),
        name="pool",
    )(p, p, p, pool_w, pool_scale.reshape(1, width))


def _outproj_kernel(a_ref, b_ref, x_ref, w_ref, g1_ref, sh_ref, sc_ref, gn_ref, wr_ref, br_ref,
                    xn_ref, h2_ref, lg_ref, *, tiles_per_batch, batch):
    seg = _tile_segment(pl.program_id(0), tiles_per_batch, batch)
    half = a_ref.shape[1]
    y = (jnp.dot(a_ref[...], w_ref[0:half, :], preferred_element_type=f32)
         + jnp.dot(b_ref[...], w_ref[half:, :], preferred_element_type=f32))
    xn = x_ref[...] + g1_ref[pl.ds(seg, 1), :] * y
    xn_ref[...] = xn
    h2 = _norm_mod(xn, gn_ref[...], sh_ref[pl.ds(seg, 1), :], sc_ref[pl.ds(seg, 1), :])
    h2_ref[...] = h2
    lg_ref[...] = jnp.dot(h2.astype(bf16), wr_ref[...], preferred_element_type=f32) + br_ref[...]


def _outproj(a, b, x, w, mods, g_ffn, w_r, b_r, *, n_tiles, batch, seq):
    r, d = n_tiles * TM, x.shape[1]
    half = a.shape[1]
    row = lambda i: (i, 0)
    return pl.pallas_call(
        functools.partial(_outproj_kernel, tiles_per_batch=seq // TM, batch=batch),
        grid=(n_tiles,),
        in_specs=[pl.BlockSpec((TM, half), row), pl.BlockSpec((TM, half), row), pl.BlockSpec((TM, d), row),
                  pl.BlockSpec(w.shape, lambda i: (0, 0), pipeline_mode=pl.Buffered(1)),
                  pl.BlockSpec((8, d), lambda i: (0, 2)),
                  pl.BlockSpec((8, d), lambda i: (0, 3)),
                  pl.BlockSpec((8, d), lambda i: (0, 4)),
                  pl.BlockSpec((1, d), lambda i: (0, 0)),
                  pl.BlockSpec(w_r.shape, lambda i: (0, 0)),
                  pl.BlockSpec((1, 128), lambda i: (0, 0))],
        out_specs=[pl.BlockSpec((TM, d), row), pl.BlockSpec((TM, d), row), pl.BlockSpec((TM, 128), row)],
        out_shape=[jax.ShapeDtypeStruct((r, d), f32), jax.ShapeDtypeStruct((r, d), f32),
                   jax.ShapeDtypeStruct((r, 128), f32)],
        compiler_params=_cparams(("arbitrary",)),
        name="outproj",
    )(a, b, x, w, mods, mods, mods, g_ffn, w_r, b_r)


def _moe_kernel(be_ref, nu_ref, xs_ref, w1_ref, w3_ref, w2_ref, ys_ref, w1_scr, w3_scr, w2_scr):
    i = pl.program_id(0)
    e = be_ref[i]
    prev = be_ref[jnp.maximum(i - 1, 0)]

    @pl.when((i == 0) | (e != prev))
    def _():
        w1_scr[...] = w1_ref[...].astype(bf16)
        w3_scr[...] = w3_ref[...].astype(bf16)
        w2_scr[...] = w2_ref[...].astype(bf16)

    @pl.when(i < nu_ref[0])
    def _():
        xb = xs_ref[...].astype(bf16)
        a = jnp.dot(xb, w1_scr[...], preferred_element_type=f32)
        b = jnp.dot(xb, w3_scr[...], preferred_element_type=f32)
        hid = (a * jax.nn.sigmoid(a) * b).astype(bf16)
        ys_ref[...] = jnp.dot(hid, w2_scr[...], preferred_element_type=f32)

    @pl.when(i >= nu_ref[0])
    def _():
        ys_ref[...] = jnp.zeros_like(ys_ref)


def _moe_experts(xs, block_e, n_used, w1, w3, w2, layer, nb):
    m = MOE_BLOCK
    _, _, d, de = w1.shape
    grid_spec = pltpu.PrefetchScalarGridSpec(
        num_scalar_prefetch=2,
        grid=(nb,),
        in_specs=[pl.BlockSpec((m, d), lambda i, be, nu: (jnp.minimum(i, nu[0] - 1), 0)),
                  pl.BlockSpec((None, None, d, de), lambda i, be, nu: (layer, be[i], 0, 0)),
                  pl.BlockSpec((None, None, d, de), lambda i, be, nu: (layer, be[i], 0, 0)),
                  pl.BlockSpec((None, None, de, d), lambda i, be, nu: (layer, be[i], 0, 0))],
        out_specs=pl.BlockSpec((m, d), lambda i, be, nu: (i, 0)),
        scratch_shapes=[pltpu.VMEM((d, de), bf16), pltpu.VMEM((d, de), bf16), pltpu.VMEM((de, d), bf16)],
    )
    return pl.pallas_call(
        _moe_kernel,
        grid_spec=grid_spec,
        out_shape=jax.ShapeDtypeStruct((nb * m, d), f32),
        compiler_params=_cparams(("arbitrary",)),
        name="moe_experts",
    )(block_e, n_used, xs, w1, w3, w2)


def _route_kernel(lg_ref, tri_ref, o_ref, cnt_ref, run_scr):
    @pl.when(pl.program_id(0) == 0)
    def _():
        run_scr[...] = jnp.zeros_like(run_scr)

    lg = lg_ref[...]
    lane = lax.broadcasted_iota(i32, lg.shape, 1)
    ninf = -jnp.inf

    def first_lane(mask):
        return jnp.min(jnp.where(mask, lane, lg.shape[1]), axis=-1, keepdims=True)

    gl = jnp.where(lane < MOE_GROUPS, lg, ninf)
    gmax = jnp.max(gl, axis=-1, keepdims=True)
    pg_top = 1.0 / jnp.sum(jnp.exp(gl - gmax), axis=-1, keepdims=True)
    lo = MOE_GROUPS + first_lane(gl == gmax) * MOE_PER_GROUP
    sel = (lane >= lo) & (lane < lo + MOE_PER_GROUP)
    el = jnp.where(sel, lg, ninf)
    ee = jnp.exp(el - jnp.max(el, axis=-1, keepdims=True))
    pe = jnp.where(sel, ee / jnp.sum(ee, axis=-1, keepdims=True), -1.0)
    p1 = jnp.max(pe, axis=-1, keepdims=True)
    i1 = first_lane(pe == p1)
    pe2 = jnp.where(lane == i1, -1.0, pe)
    p2 = jnp.max(pe2, axis=-1, keepdims=True)
    i2 = first_lane(pe2 == p2)
    den = p1 + p2
    e1 = i1 - MOE_GROUPS
    e2 = i2 - MOE_GROUPS
    hit1 = lane == e1
    hit2 = lane == e2
    onehot = jnp.where(hit1, 1.0, jnp.where(hit2, 1.0, 0.0))
    base = jnp.dot(tri_ref[...], onehot.astype(bf16), preferred_element_type=f32) + run_scr[...]
    r1 = jnp.sum(jnp.where(hit1, base, 0.0), axis=-1, keepdims=True)
    r2 = jnp.sum(jnp.where(hit2, base, 0.0), axis=-1, keepdims=True)
    run_scr[...] = run_scr[...] + jnp.sum(onehot, axis=0, keepdims=True)
    cnt_ref[...] = run_scr[...]
    vals = (e1.astype(f32), e2.astype(f32), pg_top * p1 / den, pg_top * p2 / den, r1, r2)
    out = jnp.zeros(lg.shape, f32)
    for j, v in enumerate(vals):
        out = jnp.where(lane == j, v, out)
    o_ref[...] = out


def _route(logits):
    t, width = logits.shape
    tri = jnp.tril(jnp.ones((TM, TM), f32), -1).astype(bf16)
    return pl.pallas_call(
        _route_kernel,
        grid=(t // TM,),
        in_specs=[pl.BlockSpec((TM, width), lambda i: (i, 0)), pl.BlockSpec((TM, TM), lambda i: (0, 0))],
        out_specs=[pl.BlockSpec((TM, width), lambda i: (i, 0)), pl.BlockSpec((1, width), lambda i: (0, 0))],
        out_shape=[jax.ShapeDtypeStruct((t, width), f32), jax.ShapeDtypeStruct((1, width), f32)],
        scratch_shapes=[pltpu.VMEM((1, width), f32)],
        compiler_params=_cparams(("arbitrary",)),
        name="route",
    )(logits, tri)


def _dispatch_kernel(dest_ref, h_ref, init_ref, xs_ref, sem):
    del init_ref
    i = pl.program_id(0)
    tm = h_ref.shape[0]

    def row_copy(r, k):
        slot = dest_ref[(i * tm + r) * MOE_TOPK + k]
        return pltpu.make_async_copy(h_ref.at[pl.ds(r, 1)], xs_ref.at[pl.ds(slot, 1)], sem)

    def issue(r, carry):
        for k in range(MOE_TOPK):
            row_copy(r, k).start()
        return carry

    lax.fori_loop(0, tm, issue, 0, unroll=8)
    for k in range(MOE_TOPK):
        pltpu.make_async_copy(h_ref, xs_ref.at[pl.ds(0, tm)], sem).wait()


def _dispatch(h2, dest, nb):
    t, d = h2.shape
    rows = nb * MOE_BLOCK
    grid_spec = pltpu.PrefetchScalarGridSpec(
        num_scalar_prefetch=1,
        grid=(t // TM,),
        in_specs=[pl.BlockSpec((TM, d), lambda i, dst: (i, 0)), pl.BlockSpec(memory_space=pl.ANY)],
        out_specs=pl.BlockSpec(memory_space=pl.ANY),
        scratch_shapes=[pltpu.SemaphoreType.DMA(())],
    )
    return pl.pallas_call(
        _dispatch_kernel,
        grid_spec=grid_spec,
        out_shape=jax.ShapeDtypeStruct((rows, d), h2.dtype),
        input_output_aliases={2: 0},
        compiler_params=_cparams(("arbitrary",)),
        name="moe_dispatch",
    )(dest, h2, jnp.zeros((rows, d), h2.dtype))


def _combine_kernel(dest_ref, x_ref, rt_ref, g2_ref, gn_ref, ys_ref, o_ref, y_scr, sem, *,
                    tiles_per_batch, batch, final_norm):
    i = pl.program_id(0)
    tm = x_ref.shape[0]

    def row_copy(r, k):
        slot = dest_ref[(i * tm + r) * MOE_TOPK + k]
        return pltpu.make_async_copy(ys_ref.at[pl.ds(slot, 1)], y_scr.at[k, pl.ds(r, 1)], sem)

    def issue(r, carry):
        for k in range(MOE_TOPK):
            row_copy(r, k).start()
        return carry

    lax.fori_loop(0, tm, issue, 0, unroll=8)
    for k in range(MOE_TOPK):
        pltpu.make_async_copy(ys_ref.at[pl.ds(0, tm)], y_scr.at[k], sem).wait()
    seg = _tile_segment(i, tiles_per_batch, batch)
    rt = rt_ref[...]
    y = rt[:, MOE_TOPK:MOE_TOPK + 1] * y_scr[0]
    for k in range(1, MOE_TOPK):
        y = y + rt[:, MOE_TOPK + k:MOE_TOPK + k + 1] * y_scr[k]
    x = x_ref[...] + g2_ref[pl.ds(seg, 1), :] * y
    if final_norm:
        ms = jnp.mean(x * x, axis=-1, keepdims=True)
        x = x * lax.rsqrt(ms + EPS) * gn_ref[...]
    o_ref[...] = x


def _combine(x, ys, dest, routed, mods, g_out, *, n_tiles, batch, seq, final_norm):
    d = x.shape[1]
    row = lambda i, dst: (i, 0)
    grid_spec = pltpu.PrefetchScalarGridSpec(
        num_scalar_prefetch=1,
        grid=(n_tiles,),
        in_specs=[pl.BlockSpec((TM, d), row),
                  pl.BlockSpec((TM, routed.shape[1]), row),
                  pl.BlockSpec((8, d), lambda i, dst: (0, 5)),
                  pl.BlockSpec((1, d), lambda i, dst: (0, 0)),
                  pl.BlockSpec(memory_space=pl.ANY)],
        out_specs=pl.BlockSpec((TM, d), row),
        scratch_shapes=[pltpu.VMEM((MOE_TOPK, TM, d), f32), pltpu.SemaphoreType.DMA(())],
    )
    return pl.pallas_call(
        functools.partial(_combine_kernel, tiles_per_batch=seq // TM, batch=batch, final_norm=final_norm),
        grid_spec=grid_spec,
        out_shape=jax.ShapeDtypeStruct((n_tiles * TM, d), f32),
        compiler_params=_cparams(("arbitrary",)),
        name="moe_combine",
    )(dest, x, routed, mods, g_out, ys)


def _moe(xn, h2, logits, mods, w1, w3, w2, layer, g_out, *, n_tiles, batch, seq, final_norm):
    t = n_tiles * TM
    m = MOE_BLOCK
    routed, cnt = _route(logits)
    s = t * MOE_TOPK
    nb = s // m + MOE_EXPERTS
    expert = routed[:, 0:MOE_TOPK].astype(i32)
    rank = routed[:, 2 * MOE_TOPK:3 * MOE_TOPK].astype(i32)
    counts = cnt[0, :MOE_EXPERTS].astype(i32)
    padded = ((counts + m - 1) // m) * m
    pad_end = jnp.cumsum(padded)
    onehot = expert[:, :, None] == jnp.arange(MOE_EXPERTS, dtype=i32)[None, None, :]
    dest = (jnp.sum(jnp.where(onehot, (pad_end - padded)[None, None, :], 0), axis=-1) + rank).reshape(s)
    n_used = (pad_end[-1] // m).astype(i32)
    blk = jnp.arange(nb, dtype=i32)
    block_e = jnp.minimum(jnp.searchsorted(pad_end, jnp.minimum(blk, n_used - 1) * m, side='right'),
                          MOE_EXPERTS - 1).astype(i32)
    xs = _dispatch(h2, dest, nb)
    ys = _moe_experts(xs, block_e, n_used.reshape(1), w1, w3, w2, layer, nb)
    return _combine(xn, ys, dest, routed, mods, g_out, n_tiles=n_tiles, batch=batch, seq=seq,
                    final_norm=final_norm)


HY_N2 = 128
HY_WB = 256
HY_SPEC_WB = 128


def _short_conv_kernel(x_ref, prev_ref, next_ref, w_ref, b_ref, v_ref, x1_ref, x2_ref, ext_scr, *, tiles_per_batch):
    j = pl.program_id(0) % tiles_per_batch
    tt = x_ref.shape[0]
    ext_scr[0:8, :] = jnp.where(j > 0, prev_ref[...], 0.0)
    ext_scr[8:8 + tt, :] = x_ref[...]
    ext_scr[8 + tt:16 + tt, :] = jnp.where(j < tiles_per_batch - 1, next_ref[...], 0.0)
    w = w_ref[...]
    width = v_ref.shape[1]
    for part, o_ref in enumerate((v_ref, x1_ref, x2_ref)):
        cols = slice(part * width, (part + 1) * width)
        o_ref[...] = (ext_scr[7:7 + tt, cols] * w[0:1, cols] + ext_scr[8:8 + tt, cols] * w[1:2, cols]
                      + ext_scr[9:9 + tt, cols] * w[2:3, cols] + b_ref[:, cols])


def _short_conv(hy, w, b, *, batch, seq):
    n_lat = batch * seq
    tt = 256
    width = hy.shape[1]
    out = jax.ShapeDtypeStruct((n_lat, width // 3), f32)
    o_spec = pl.BlockSpec((tt, width // 3), lambda i: (i, 0))
    return pl.pallas_call(
        functools.partial(_short_conv_kernel, tiles_per_batch=seq // tt),
        grid=(n_lat // tt,),
        in_specs=[pl.BlockSpec((tt, width), lambda i: (i, 0)),
                  pl.BlockSpec((8, width), lambda i: (jnp.maximum(i * (tt // 8) - 1, 0), 0)),
                  pl.BlockSpec((8, width), lambda i: ((i + 1) * (tt // 8), 0)),
                  pl.BlockSpec((3, width), lambda i: (0, 0)),
                  pl.BlockSpec((1, width), lambda i: (0, 0))],
        out_specs=[o_spec, o_spec, o_spec],
        out_shape=[out, out, out],
        scratch_shapes=[pltpu.VMEM((tt + 16, width), f32)],
        compiler_params=_cparams(("arbitrary",)),
        name="short_conv",
    )(hy, hy, hy, w, b.reshape(1, width))


def _hy_filter_kernel(z_ref, w1_ref, b1_ref, fr_ref, w2_ref, b2_ref, w3_ref, dl_ref, k_ref, nrm_ref):
    i = pl.program_id(0)
    z = z_ref[...]
    fr = fr_ref[...]
    h = jnp.sin(fr[0:1] * (jnp.dot(z.astype(bf16), w1_ref[...], preferred_element_type=f32) + b1_ref[...]))
    h = jnp.sin(fr[1:2] * (jnp.dot(h.astype(bf16), w2_ref[...], preferred_element_type=f32) + b2_ref[...]))
    kk = jnp.dot(h.astype(bf16), w3_ref[...], preferred_element_type=f32)
    width = dl_ref.shape[1]
    decay = jnp.exp(-z[:, 0:1] * jnp.abs(dl_ref[...]))
    tl = z.shape[0]
    row = i * tl + lax.broadcasted_iota(i32, (tl, 1), 0)
    tot = []
    for blk in range(kk.shape[1] // width):
        kb = kk[:, blk * width:(blk + 1) * width] * decay
        if blk % 2 == 1:
            kb = jnp.where(row == 0, 0.0, kb)
        k_ref[:, blk * width:(blk + 1) * width] = kb
        tot.append(jnp.sum(jnp.abs(kb), axis=0, keepdims=True))
    s = jnp.concatenate(tot, axis=1)

    @pl.when(i == 0)
    def _():
        nrm_ref[...] = s

    @pl.when(i > 0)
    def _():
        nrm_ref[...] = nrm_ref[...] + s


def _hy_filters(seq, w1, b1, freq, w2, b2, w3):
    t = jnp.linspace(0.0, 1.0, seq, dtype=f32)[:, None]
    bands = (HY_EMB - 1) // 2
    w = 2 * math.pi * jnp.arange(seq, dtype=f32)[:, None] / seq
    f = jnp.linspace(1e-4, bands - 1, bands, dtype=f32)[None, :]
    z = jnp.concatenate([t, jnp.cos(f * w), -jnp.sin(f * w), jnp.zeros((seq, 128 - HY_EMB), f32)], axis=-1)
    w1p = jnp.concatenate([w1, jnp.zeros((128 - HY_EMB, w1.shape[1]), f32)], axis=0).astype(bf16)
    hid = w1.shape[1]
    n_out = w3.shape[1]
    deltas = jnp.linspace(math.log(HY_TARGET) / HY_SLOW_DECAY, math.log(HY_TARGET) / HY_FAST_DECAY,
                          HY_WIDTH, dtype=f32).reshape(1, HY_WIDTH)
    tl = 512
    const = lambda a: pl.BlockSpec(a.shape, lambda i: (0,) * a.ndim)
    ins = [z, w1p, b1.reshape(1, hid), freq, w2.astype(bf16), b2.reshape(1, hid), w3.astype(bf16), deltas]
    return pl.pallas_call(
        _hy_filter_kernel,
        grid=(seq // tl,),
        in_specs=[pl.BlockSpec((tl, 128), lambda i: (i, 0))] + [const(a) for a in ins[1:]],
        out_specs=[pl.BlockSpec((tl, n_out), lambda i: (i, 0)), pl.BlockSpec((1, n_out), lambda i: (0, 0))],
        out_shape=[jax.ShapeDtypeStruct((seq, n_out), f32), jax.ShapeDtypeStruct((1, n_out), f32)],
        compiler_params=_cparams(("arbitrary",)),
        name="hy_filter",
    )(*ins)


def _dft_tables(seq):
    n2 = HY_N2
    n_tot = 2 * seq
    n1 = n_tot // n2
    nf = n1 // 2 + 1
    nfp = -(-nf // 8) * 8
    f1 = jnp.arange(nfp, dtype=i32)
    k1 = jnp.arange(n1 // 2, dtype=i32)
    th = (2 * math.pi / n1) * ((f1[:, None] * k1[None, :]) % n1).astype(f32)
    valid = (f1 < nf)[:, None]
    fwd = jnp.concatenate([jnp.where(valid, jnp.cos(th), 0.0), jnp.where(valid, -jnp.sin(th), 0.0)], axis=0)
    eye = jnp.eye(8, dtype=f32)
    fk = jnp.einsum('rk,jh->rjkh', fwd, eye).reshape(2 * nfp * 8, (n1 // 2) * 8)
    cw = jnp.where(valid, jnp.where((f1 == 0) | (f1 == n1 // 2), 1.0, 2.0)[:, None], 0.0) / n_tot
    inv = jnp.concatenate([(cw * jnp.cos(th)).T, (-cw * jnp.sin(th)).T], axis=1)
    ck = jnp.einsum('kr,jh->kjrh', inv, eye).reshape((n1 // 2) * 8, 2 * nfp * 8)
    fa = jnp.arange(nf, dtype=i32)[:, None, None]
    f2 = jnp.arange(n2, dtype=i32)[None, :, None]
    m2 = jnp.arange(n2, dtype=i32)[None, None, :]
    ph = (2 * math.pi / n_tot) * ((f2 * m2 * n1 + m2 * fa) % n_tot).astype(f32)
    gr, gi = jnp.cos(ph), -jnp.sin(ph)
    g = jnp.concatenate([jnp.concatenate([gr, -gi], axis=2), jnp.concatenate([gi, gr], axis=2)], axis=1)
    hr, hi = jnp.swapaxes(jnp.cos(ph), 1, 2), jnp.swapaxes(jnp.sin(ph), 1, 2)
    h = jnp.concatenate([jnp.concatenate([hr, -hi], axis=2), jnp.concatenate([hi, hr], axis=2)], axis=1)
    return fk.astype(bf16), ck.astype(bf16), g.astype(bf16), h.astype(bf16), nf, nfp


def _hy_rows_per_step(nf):
    return max(d for d in (5, 4, 3, 2, 1) if nf % d == 0)


def _dft_stage1(x_ref, fk_ref, a_scr):
    n1h, n2, wb = x_ref.shape

    def slab(s, carry):
        rows = pl.ds(pl.multiple_of(s * 8, 8), 8)
        xs = x_ref[:, rows, :].reshape(n1h * 8, wb).astype(bf16)
        a = jnp.dot(fk_ref[...], xs, preferred_element_type=f32)
        a_scr[:, rows, :] = a.reshape(a_scr.shape[0], 8, wb)
        return carry

    lax.fori_loop(0, n2 // 8, slab, 0)


def _dft_stage2(a_scr, g, f1, nfp):
    a = jnp.concatenate([a_scr[f1], a_scr[nfp + f1]], axis=0).astype(bf16)
    return jnp.dot(g, a, preferred_element_type=f32)


def _hy_spectrum_kernel(kf_ref, kb_ref, fk_ref, g_ref, o_ref, af_scr, ab_scr, *, nfp):
    fs = pl.program_id(2)

    @pl.when(fs == 0)
    def _():
        _dft_stage1(kf_ref, fk_ref, af_scr)
        _dft_stage1(kb_ref, fk_ref, ab_scr)

    fb = g_ref.shape[0]
    n2 = kf_ref.shape[1]
    for j in range(fb):
        f1 = fs * fb + j
        sf = _dft_stage2(af_scr, g_ref[j], f1, nfp)
        sb = _dft_stage2(ab_scr, g_ref[j], f1, nfp)
        o_ref[0, j] = sf[:n2] + sb[:n2]
        o_ref[1, j] = sf[n2:] - sb[n2:]


def _hy_spectrum(k, fk, g, nf, nfp, seq):
    n2, wb = HY_N2, HY_SPEC_WB
    n1h = seq // n2
    fb = _hy_rows_per_step(nf)
    nwb = HY_WIDTH // wb
    k3 = k.reshape(n1h, n2, k.shape[1])
    return pl.pallas_call(
        functools.partial(_hy_spectrum_kernel, nfp=nfp),
        grid=(HY_ORDER, nwb, nf // fb),
        in_specs=[pl.BlockSpec((n1h, n2, wb), lambda o, w, f: (0, 0, (2 * o) * nwb + w)),
                  pl.BlockSpec((n1h, n2, wb), lambda o, w, f: (0, 0, (2 * o + 1) * nwb + w)),
                  pl.BlockSpec(fk.shape, lambda o, w, f: (0, 0)),
                  pl.BlockSpec((fb, 2 * n2, 2 * n2), lambda o, w, f: (f, 0, 0))],
        out_specs=pl.BlockSpec((None, 2, fb, n2, wb), lambda o, w, f: (o, 0, f, 0, w)),
        out_shape=jax.ShapeDtypeStruct((HY_ORDER, 2, nf, n2, HY_WIDTH), f32),
        scratch_shapes=[pltpu.VMEM((2 * nfp, n2, wb), f32), pltpu.VMEM((2 * nfp, n2, wb), f32)],
        compiler_params=_cparams(("arbitrary",) * 3),
        name="hy_spectrum",
    )(k3, k3, fk, g)


def _hy_conv_kernel(x_ref, gate_ref, kf_ref, g_ref, h_ref, fk_ref, ck_ref, bias_ref, inv_ref, o_ref, a_scr, *, nfp):
    fs = pl.program_id(2)
    n1h, n2, wb = x_ref.shape

    @pl.when(fs == 0)
    def _():
        _dft_stage1(x_ref, fk_ref, a_scr)

    fb = g_ref.shape[0]
    for j in range(fb):
        f1 = fs * fb + j
        xs = _dft_stage2(a_scr, g_ref[j], f1, nfp)
        xr, xi = xs[:n2], xs[n2:]
        kr, ki = kf_ref[0, j], kf_ref[1, j]
        y = jnp.concatenate([xr * kr - xi * ki, xr * ki + xi * kr], axis=0).astype(bf16)
        bb = jnp.dot(h_ref[j], y, preferred_element_type=f32)
        a_scr[f1] = bb[:n2]
        a_scr[nfp + f1] = bb[n2:]

    @pl.when(fs == pl.num_programs(2) - 1)
    def _():
        bias = bias_ref[...].reshape(1, 1, wb)
        inv = inv_ref[...].reshape(1, 1, wb)

        def slab(s, carry):
            rows = pl.ds(pl.multiple_of(s * 8, 8), 8)
            bs = a_scr[:, rows, :].reshape(2 * nfp * 8, wb).astype(bf16)
            y = jnp.dot(ck_ref[...], bs, preferred_element_type=f32).reshape(n1h, 8, wb)
            o_ref[:, rows, :] = (gate_ref[:, rows, :] * (y * inv + x_ref[:, rows, :] * bias)).astype(o_ref.dtype)
            return carry

        lax.fori_loop(0, n2 // 8, slab, 0)


def _hy_conv(x, gate, kf, order, tabs, bias, inv_norm, out_dtype, *, batch, seq):
    fk, ck, g, h, nf, nfp = tabs
    n2, wb = HY_N2, HY_WB
    n1h = seq // n2
    fb = _hy_rows_per_step(nf)
    width = x.shape[1]
    big = pl.BlockSpec((n1h, n2, wb), lambda b, w, f: (b, 0, w), pipeline_mode=pl.Buffered(1))
    out = pl.pallas_call(
        functools.partial(_hy_conv_kernel, nfp=nfp),
        grid=(batch, width // wb, nf // fb),
        in_specs=[big, big,
                  pl.BlockSpec((None, 2, fb, n2, wb), lambda b, w, f: (order, 0, f, 0, w)),
                  pl.BlockSpec((fb, 2 * n2, 2 * n2), lambda b, w, f: (f, 0, 0)),
                  pl.BlockSpec((fb, 2 * n2, 2 * n2), lambda b, w, f: (f, 0, 0)),
                  pl.BlockSpec(fk.shape, lambda b, w, f: (0, 0)),
                  pl.BlockSpec(ck.shape, lambda b, w, f: (0, 0)),
                  pl.BlockSpec((1, wb), lambda b, w, f: (0, w)),
                  pl.BlockSpec((1, wb), lambda b, w, f: (0, w))],
        out_specs=big,
        out_shape=jax.ShapeDtypeStruct((batch * n1h, n2, width), out_dtype),
        scratch_shapes=[pltpu.VMEM((2 * nfp, n2, wb), f32)],
        compiler_params=_cparams(("arbitrary",) * 3),
        name="hy_conv",
    )(x.reshape(batch * n1h, n2, width), gate.reshape(batch * n1h, n2, width), kf, g, h, fk, ck,
      bias.reshape(1, width), inv_norm.reshape(1, width))
    return out.reshape(batch * seq, width)


def _hyena(hy, conv_w, conv_b, f_w1, f_b1, f_freq, f_w2, f_b2, f_w3, hy_bias, *, batch, seq):
    v, x1, x2 = _short_conv(hy, conv_w, conv_b, batch=batch, seq=seq)
    k, sums = _hy_filters(seq, f_w1, f_b1, f_freq, f_w2, f_b2, f_w3)
    sums = sums.reshape(HY_ORDER, 2, HY_WIDTH)
    inv_norm = 1.0 / (sums[:, 0] + sums[:, 1])
    tabs = _dft_tables(seq)
    kf = _hy_spectrum(k, tabs[0], tabs[2], tabs[4], tabs[5], seq)
    z = _hy_conv(v, x1, kf, 0, tabs, hy_bias[0], inv_norm[0], f32, batch=batch, seq=seq)
    return _hy_conv(z, x2, kf, 1, tabs, hy_bias[1], inv_norm[1], bf16, batch=batch, seq=seq)


S5_TT = 256
S5_GB = 8


def _gelu_tanh(x):
    return 0.5 * x * (1.0 + jnp.tanh(0.7978845608028654 * (x + 0.044715 * x * x * x)))


def _s5_kernel(u_ref, perm_ref, permt_ref, wb_ref, wc_ref, are_ref, aim_ref, pre_ref, pim_ref, *rest, fuse_glu):
    if fuse_glu:
        yf_ref, d_ref, gw_ref, gb_ref, o_ref, st_scr, bu_scr, y_scr = rest
    else:
        o_ref, st_scr, bu_scr, y_scr = rest
    tt = u_ref.shape[0]
    steps = tt // 8
    half = bu_scr.shape[1] // 2

    @pl.when(pl.program_id(1) == 0)
    def _():
        st_scr[...] = jnp.zeros_like(st_scr)

    up = jnp.dot(perm_ref[...], u_ref[...].astype(bf16), preferred_element_type=f32).astype(bf16)
    for gb in range(wb_ref.shape[0]):
        bu_scr[...] = jnp.dot(up[:, gb * 128:(gb + 1) * 128], wb_ref[gb], preferred_element_type=f32)
        a_re = jnp.broadcast_to(are_ref[gb], (8, half))
        a_im = jnp.broadcast_to(aim_ref[gb], (8, half))

        def scan_step(k, h):
            hr, hi = h
            rows = pl.ds(pl.multiple_of(k * 8, 8), 8)
            nr = a_re * hr - a_im * hi + bu_scr[rows, 0:half]
            ni = a_re * hi + a_im * hr + bu_scr[rows, half:]
            bu_scr[rows, 0:half] = nr
            bu_scr[rows, half:] = ni
            return nr, ni

        zero = jnp.zeros((8, half), f32)
        hr, hi = lax.fori_loop(0, steps, scan_step, (zero, zero), unroll=4)

        pw_r = pre_ref[gb, steps - 1:steps, :]
        pw_i = pim_ref[gb, steps - 1:steps, :]
        cur_r = st_scr[gb, :, 0:half]
        cur_i = st_scr[gb, :, half:]
        car_r, car_i = [], []
        for j in range(8):
            car_r.append(cur_r)
            car_i.append(cur_i)
            cur_r, cur_i = (hr[j:j + 1] + pw_r * cur_r - pw_i * cur_i,
                            hi[j:j + 1] + pw_r * cur_i + pw_i * cur_r)
        st_scr[gb, :, 0:half] = cur_r
        st_scr[gb, :, half:] = cur_i
        cr = jnp.concatenate(car_r, axis=0)
        ci = jnp.concatenate(car_i, axis=0)

        def fix_step(k, carry):
            rows = pl.ds(pl.multiple_of(k * 8, 8), 8)
            pr = pre_ref[gb, pl.ds(k, 1), :]
            pi = pim_ref[gb, pl.ds(k, 1), :]
            bu_scr[rows, 0:half] = bu_scr[rows, 0:half] + (pr * cr - pi * ci)
            bu_scr[rows, half:] = bu_scr[rows, half:] + (pr * ci + pi * cr)
            return carry

        lax.fori_loop(0, steps, fix_step, 0, unroll=4)
        y_scr[:, gb * 128:(gb + 1) * 128] = jnp.dot(bu_scr[...].astype(bf16), wc_ref[gb],
                                                      preferred_element_type=f32)
    y = jnp.dot(permt_ref[...], y_scr[...], preferred_element_type=f32, precision=lax.Precision.HIGHEST)
    if fuse_glu:
        g = _gelu_tanh(y + yf_ref[...] + u_ref[...] * d_ref[...])
        z = jnp.dot(g.astype(bf16), gw_ref[...], preferred_element_type=f32) + gb_ref[...]
        o_ref[...] = (g * jax.nn.sigmoid(z)).astype(o_ref.dtype)
    else:
        o_ref[...] = y


def _s5_tables(a_re, a_im, log_dt, b_re, b_im, c_re, c_im, steps):
    ng, ns = a_re.shape
    nb = ng // S5_GB
    lam = lax.complex(a_re, a_im)
    dt = jnp.exp(log_dt)[:, None]
    lam_bar = jnp.exp(lam * dt)
    b_bar = ((lam_bar - 1) / lam)[..., None] * lax.complex(b_re, b_im)
    cm = lax.complex(c_re, c_im)
    pw = jnp.exp((lam * dt)[None] * jnp.arange(1, steps + 1, dtype=f32)[:, None, None])
    eye = jnp.eye(S5_GB, dtype=f32)
    bb = b_bar.reshape(nb, S5_GB, ns, S5_GROUP)
    wb = jnp.concatenate([jnp.einsum('bgpc,gh->bgchp', part, eye).reshape(nb, S5_GB * S5_GROUP, S5_GB * ns)
                          for part in (jnp.real(bb), jnp.imag(bb))], axis=-1)
    cc = cm.reshape(nb, S5_GB, S5_GROUP, ns)
    wc = jnp.concatenate([jnp.einsum('bgcp,gh->bgphc', part, eye).reshape(nb, S5_GB * ns, S5_GB * S5_GROUP)
                          for part in (jnp.real(cc), -jnp.imag(cc))], axis=1)
    lb = lam_bar.reshape(nb, 1, S5_GB * ns)
    pwb = pw.reshape(steps, nb, S5_GB * ns).transpose(1, 0, 2)
    return (wb.astype(bf16), wc.astype(bf16), jnp.real(lb), jnp.imag(lb), jnp.real(pwb), jnp.imag(pwb))


def _s5_perm(tt, reverse):
    steps = tt // 8
    dst = jnp.arange(tt)
    t = (dst % 8) * steps + dst // 8
    src = tt - 1 - t if reverse else t
    return jnp.arange(tt)[None, :] == src[:, None]


def _s5(u, a_re, a_im, log_dt, b_re, b_im, c_re, c_im, s5_d, glu_w, glu_b, *, batch, seq, ctx_len):
    r, width = u.shape
    tt = S5_TT
    assert ctx_len == tt and seq % tt == 0
    n_lat = seq // tt
    ctx_blk0 = (batch * seq) // tt
    y = None
    for di in range(2):
        rev = di == 1
        wb, wc, lr, li, pr, pi = _s5_tables(a_re[di], a_im[di], log_dt[di], b_re[di], b_im[di],
                                            c_re[di], c_im[di], tt // 8)
        perm = _s5_perm(tt, rev)

        def tile_map(b, i, rev=rev):
            lat = b * n_lat + (n_lat - i if rev else i - 1)
            return (jnp.where(i == 0, ctx_blk0 + b, lat), 0)

        row_spec = pl.BlockSpec((tt, width), tile_map)
        const = lambda a: pl.BlockSpec(a.shape, lambda b, i: (0,) * a.ndim)
        ins = [u, perm.astype(bf16), perm.T.astype(f32), wb, wc, lr, li, pr, pi]
        in_specs = [row_spec] + [const(a) for a in ins[1:]]
        if rev:
            extra = [y, s5_d.reshape(1, width), glu_w.astype(bf16), glu_b.reshape(1, width)]
            in_specs += [row_spec] + [const(a) for a in extra[1:]]
            ins += extra
        y = pl.pallas_call(
            functools.partial(_s5_kernel, fuse_glu=rev),
            grid=(batch, n_lat + 1),
            in_specs=in_specs,
            out_specs=row_spec,
            out_shape=jax.ShapeDtypeStruct((r, width), bf16 if rev else f32),
            scratch_shapes=[pltpu.VMEM((wb.shape[0], 1, wb.shape[2]), f32),
                            pltpu.VMEM((tt, wb.shape[2]), f32),
                            pltpu.VMEM((tt, width), f32)],
            compiler_params=_cparams(("arbitrary", "arbitrary")),
            name="s5_bwd_glu" if rev else "s5_fwd",
        )(*ins)
    return y


def _rope_tables(seq):
    rows = seq // GRID_W
    row = jnp.repeat(jnp.arange(rows, dtype=f32), GRID_W)
    col = jnp.tile(jnp.arange(GRID_W, dtype=f32), rows)
    n_freq = DIFF_QK // 4
    inv = ROPE_BASE ** (-jnp.arange(n_freq, dtype=f32) / n_freq)
    ar, ac = row[:, None] * inv, col[:, None] * inv
    cos = jnp.concatenate([jnp.cos(ar), jnp.cos(ar), jnp.cos(ac), jnp.cos(ac)], axis=-1)
    sin = jnp.concatenate([-jnp.sin(ar), jnp.sin(ar), -jnp.sin(ac), jnp.sin(ac)], axis=-1)
    cos = jnp.concatenate([jnp.tile(cos, (1, 2)), jnp.ones((TM, 128), f32)], axis=0)
    sin = jnp.concatenate([jnp.tile(sin, (1, 2)), jnp.zeros((TM, 128), f32)], axis=0)
    return cos, sin


def kernel(x, c, ctx, c_ctx, ada_w, ada_b, norm_mix_g, norm_ffn_g, norm_out_g, ab_w_in, ab_w_out, diff_lam, diff_subln_g, pool_w, pool_scale, cd_w_in, cd_w_out, hy_conv_w, hy_conv_b, hy_f_w1, hy_f_b1, hy_f_freq, hy_f_w2, hy_f_b2, hy_f_w3, hy_bias, s5_a_re, s5_a_im, s5_log_dt, s5_b_re, s5_b_im, s5_c_re, s5_c_im, s5_d, glu_w, glu_b, moe_w_rg, moe_b_rg, moe_w_re, moe_b_re, moe_w1, moe_w3, moe_w2):
    bsz, seq, d = x.shape
    cl = ctx.shape[1]
    depth = ada_w.shape[0]
    n_lat = bsz * seq
    assert seq % TM == 0 and (bsz * cl) % TM == 0 and bsz + 1 <= 8
    xs = jnp.concatenate([x.reshape(n_lat, d), ctx.reshape(bsz * cl, d)], axis=0)
    r = xs.shape[0]
    cond = jnp.zeros((8, d), f32).at[0:bsz].set(c).at[bsz].set(c_ctx)
    mods = _ada_mods(cond, ada_w, ada_b)
    geom = dict(batch=bsz, seq=seq)

    for l in range(depth):
        last = l == depth - 1
        i = l // 2
        n_tiles = (n_lat if last else r) // TM
        g_mix = norm_mix_g[l].reshape(1, d)
        if l % 2 == 0:
            lam_init = 0.8 - 0.6 * math.exp(-0.3 * l)
            cos, sin = _rope_tables(seq)
            q, k, v, p = _proj_rope(xs, g_mix, mods[l], ab_w_in[i].astype(bf16), cos, sin, **geom)
            mix_a = _diff_attention(q, k, v, diff_lam[i], diff_subln_g[i], lam_init, ctx_len=cl, **geom)
            mix_b = _pool(p, pool_w[i], pool_scale[i], ctx_len=cl, **geom)
            w_out = ab_w_out[i]
        else:
            hw = (HY_ORDER + 1) * HY_WIDTH
            hy, u = _proj(xs, g_mix, mods[l], cd_w_in[i].astype(bf16), hw, **geom)
            mix_a = _hyena(hy, hy_conv_w[i], hy_conv_b[i], hy_f_w1[i], hy_f_b1[i], hy_f_freq[i],
                           hy_f_w2[i], hy_f_b2[i], hy_f_w3[i], hy_bias[i], **geom)
            mix_b = _s5(u, s5_a_re[i], s5_a_im[i], s5_log_dt[i], s5_b_re[i], s5_b_im[i],
                        s5_c_re[i], s5_c_im[i], s5_d[i], glu_w[i], glu_b[i], ctx_len=cl, **geom)
            w_out = cd_w_out[i]
        w_r = jnp.zeros((d, 128), f32).at[:, :MOE_GROUPS].set(moe_w_rg[l]) \
            .at[:, MOE_GROUPS:MOE_GROUPS + MOE_EXPERTS].set(moe_w_re[l]).astype(bf16)
        b_r = jnp.zeros((1, 128), f32).at[0, :MOE_GROUPS].set(moe_b_rg[l]) \
            .at[0, MOE_GROUPS:MOE_GROUPS + MOE_EXPERTS].set(moe_b_re[l])
        xn, h2, logits = _outproj(mix_a, mix_b, xs, w_out.astype(bf16), mods[l], norm_ffn_g[l].reshape(1, d),
                                  w_r, b_r, n_tiles=n_tiles, **geom)
        xs = _moe(xn, h2, logits, mods[l], moe_w1, moe_w3, moe_w2, l, norm_out_g.reshape(1, d),
                  n_tiles=n_tiles, final_norm=last, **geom)
    return xs[:n_lat].reshape(bsz, seq, d)
```

```python
import functools
import math

import jax
import jax.numpy as jnp
from jax import lax
from jax.experimental import pallas as pl
from jax.experimental.pallas import tpu as pltpu

f32, bf16, i32 = jnp.float32, jnp.bfloat16, jnp.int32

EPS = 1e-6
GRID_W = 64
DIFF_QK = 64
DIFF_V = 128
ROPE_BASE = 10000.0
POOL_WINDOWS = (2, 4, 8, 16)
POOL_GROUP = 256
HY_WIDTH = 1024
HY_ORDER = 2
HY_EMB = 33
HY_FAST_DECAY = 0.3
HY_SLOW_DECAY = 1.5
HY_TARGET = 1e-2
S5_GROUP = 16
S5_GROUPS = 64
S5_STATE = 64
MOE_GROUPS = 8
MOE_PER_GROUP = 8
MOE_EXPERTS = 64
MOE_TOPK = 2
MOE_BLOCK = 256

VMEM_LIMIT = 56 * 1024 * 1024
TM = 512
LOG2E = 1.4426950408889634


def _cparams(sem):
    return pltpu.CompilerParams(dimension_semantics=sem, vmem_limit_bytes=VMEM_LIMIT)


def _norm_mod(x, g, shift, scale):
    ms = jnp.mean(x * x, axis=-1, keepdims=True)
    return (x * lax.rsqrt(ms + EPS) * g) * (1.0 + scale) + shift


def _tile_segment(i, lat_tiles_per_batch, batch):
    seg = jnp.zeros((), i32)
    for b in range(1, batch + 1):
        seg = seg + (i >= b * lat_tiles_per_batch).astype(i32)
    return seg


def _ada_kernel(s_ref, w_ref, b_ref, o_ref):
    s = s_ref[...]
    a = (s * jax.nn.sigmoid(s)).astype(bf16)
    o_ref[...] = jnp.dot(a, w_ref[...].astype(bf16), preferred_element_type=f32) + b_ref[...]


def _ada_mods(s, ada_w, ada_b):
    depth, d, n = ada_w.shape
    tn = 1024
    return pl.pallas_call(
        _ada_kernel,
        grid=(depth, n // tn),
        in_specs=[pl.BlockSpec((8, d), lambda l, j: (0, 0)),
                  pl.BlockSpec((None, d, tn), lambda l, j: (l, 0, j)),
                  pl.BlockSpec((None, 1, tn), lambda l, j: (l, 0, j))],
        out_specs=pl.BlockSpec((None, 8, tn), lambda l, j: (l, 0, j)),
        out_shape=jax.ShapeDtypeStruct((depth, 8, n), f32),
        compiler_params=_cparams(("arbitrary", "arbitrary")),
        name="ada_mods",
    )(s, ada_w, ada_b.reshape(depth, 1, n))


def _proj_rope_kernel(x_ref, g_ref, sh_ref, sc_ref, w_ref, cos_ref, sin_ref,
                      q_ref, k_ref, v_ref, p_ref, h_scr, *, tiles_per_batch, batch, qscale):
    seg = _tile_segment(pl.program_id(0), tiles_per_batch, batch)
    h = _norm_mod(x_ref[...], g_ref[...], sh_ref[pl.ds(seg, 1), :], sc_ref[pl.ds(seg, 1), :])
    h_scr[...] = h.astype(bf16)
    cos = cos_ref[...]
    sin = sin_ref[...]
    tm = cos.shape[0]
    lane = lax.broadcasted_iota(i32, (tm, 128), 1)
    first = (lane % 32) < 16

    def rope(r):
        sw = jnp.where(first, pltpu.roll(r, 112, 1), pltpu.roll(r, 16, 1))
        return r * cos + sw * sin

    heads = q_ref.shape[0]
    nc = 512
    for c in range(w_ref.shape[1] // nc):
        res = jnp.dot(h_scr[...], w_ref[:, c * nc:(c + 1) * nc], preferred_element_type=f32)
        for s in range(nc // 128):
            r = res[:, s * 128:(s + 1) * 128]
            col = (c * nc) // 128 + s
            if col < heads:
                q_ref[col] = (rope(r) * qscale).astype(bf16)
            elif col < 2 * heads:
                k_ref[col - heads] = rope(r).astype(bf16)
            elif col < 3 * heads:
                v_ref[col - 2 * heads] = r.T.astype(bf16)
            else:
                c0 = (col - 3 * heads) * 128
                p_ref[:, c0:c0 + 128] = r


def _proj_rope(x, g, mods, w, cos, sin, *, batch, seq):
    r, d = x.shape
    heads = 8
    lat_tiles = seq // TM
    n_tiles = r // TM

    def tab_map(i):
        return (jnp.where(i < batch * lat_tiles, i % lat_tiles, lat_tiles), 0)

    hm = jax.ShapeDtypeStruct((heads, r, 128), bf16)
    hm_spec = pl.BlockSpec((heads, TM, 128), lambda i: (0, i, 0))
    return pl.pallas_call(
        functools.partial(_proj_rope_kernel, tiles_per_batch=lat_tiles, batch=batch,
                          qscale=LOG2E * DIFF_QK ** -0.5),
        grid=(n_tiles,),
        in_specs=[pl.BlockSpec((TM, d), lambda i: (i, 0)),
                  pl.BlockSpec((1, d), lambda i: (0, 0)),
                  pl.BlockSpec((8, d), lambda i: (0, 0)),
                  pl.BlockSpec((8, d), lambda i: (0, 1)),
                  pl.BlockSpec(w.shape, lambda i: (0, 0), pipeline_mode=pl.Buffered(1)),
                  pl.BlockSpec((TM, 128), tab_map),
                  pl.BlockSpec((TM, 128), tab_map)],
        out_specs=[hm_spec, hm_spec, pl.BlockSpec((heads, 128, TM), lambda i: (0, 0, i)),
                   pl.BlockSpec((TM, 1024), lambda i: (i, 0))],
        out_shape=[hm, hm, jax.ShapeDtypeStruct((heads, 128, r), bf16), jax.ShapeDtypeStruct((r, 1024), f32)],
        scratch_shapes=[pltpu.VMEM((TM, d), bf16)],
        compiler_params=_cparams(("arbitrary",)),
        name="proj_rope",
    )(x, g, mods, mods, w, cos, sin)


def _proj_kernel(x_ref, g_ref, sh_ref, sc_ref, w_ref, hy_ref, u_ref, h_scr, *, tiles_per_batch, batch):
    seg = _tile_segment(pl.program_id(0), tiles_per_batch, batch)
    h = _norm_mod(x_ref[...], g_ref[...], sh_ref[pl.ds(seg, 1), :], sc_ref[pl.ds(seg, 1), :])
    h_scr[...] = h.astype(bf16)
    nc = 512
    n_hy = hy_ref.shape[1]
    for c in range(w_ref.shape[1] // nc):
        res = jnp.dot(h_scr[...], w_ref[:, c * nc:(c + 1) * nc], preferred_element_type=f32)
        if c * nc < n_hy:
            hy_ref[:, c * nc:(c + 1) * nc] = res
        else:
            u_ref[:, c * nc - n_hy:(c + 1) * nc - n_hy] = res


def _proj(x, g, mods, w, n_hy, *, batch, seq):
    r, d = x.shape
    n = w.shape[1]
    return pl.pallas_call(
        functools.partial(_proj_kernel, tiles_per_batch=seq // TM, batch=batch),
        grid=(r // TM,),
        in_specs=[pl.BlockSpec((TM, d), lambda i: (i, 0)),
                  pl.BlockSpec((1, d), lambda i: (0, 0)),
                  pl.BlockSpec((8, d), lambda i: (0, 0)),
                  pl.BlockSpec((8, d), lambda i: (0, 1)),
                  pl.BlockSpec(w.shape, lambda i: (0, 0), pipeline_mode=pl.Buffered(1))],
        out_specs=[pl.BlockSpec((TM, n_hy), lambda i: (i, 0)), pl.BlockSpec((TM, n - n_hy), lambda i: (i, 0))],
        out_shape=[jax.ShapeDtypeStruct((r, n_hy), f32), jax.ShapeDtypeStruct((r, n - n_hy), f32)],
        scratch_shapes=[pltpu.VMEM((TM, d), bf16)],
        compiler_params=_cparams(("arbitrary",)),
        name="proj",
    )(x, g, mods, mods, w)


def _attn_kernel(lv_ref, g_ref, q_ref, kc_ref, vc_ref, *rest, n_lat, tk, lam_init):
    if n_lat:
        kl_ref, vl_ref, o_ref, s_scr = rest
    else:
        o_ref, s_scr = rest
    lv = lv_ref[...]
    lam = (jnp.exp(jnp.sum(lv[0:1] * lv[1:2], axis=-1, keepdims=True))
           - jnp.exp(jnp.sum(lv[2:3] * lv[3:4], axis=-1, keepdims=True)) + lam_init)
    q = q_ref[...]
    tq = q.shape[0]
    lane = lax.broadcasted_iota(i32, q.shape, 1)
    zero = jnp.zeros_like(q)
    qs = (jnp.where(lane < DIFF_QK, q, zero), jnp.where(lane >= DIFF_QK, q, zero))
    nt = (((1,), (1,)), ((), ()))

    def scores(slot, k):
        for m in range(2):
            s_scr[slot, m, 0:k.shape[0], :] = lax.dot_general(k, qs[m], nt, preferred_element_type=f32)

    def update(carry, slot, nk, vt):
        new = []
        for m in range(2):
            mx, l, acc = carry[3 * m:3 * m + 3]
            s = s_scr[slot, m, 0:nk, :]
            mn = jnp.maximum(mx, jnp.max(s, axis=0, keepdims=True))
            alpha = jnp.exp2(mx - mn)
            p = jnp.exp2(s - mn)
            l = alpha * l + jnp.sum(p, axis=0, keepdims=True)
            acc = alpha * acc + jnp.dot(vt, p.astype(bf16), preferred_element_type=f32)
            new += [mn, l, acc]
        return tuple(new)

    def k_block(j):
        return kl_ref[pl.ds(pl.multiple_of(j * tk, tk), tk), :]

    def vt_block(j):
        return vl_ref[:, pl.ds(pl.multiple_of(j * tk, tk), tk)]

    init = (jnp.full((1, tq), -jnp.inf, f32), jnp.zeros((1, tq), f32), jnp.zeros((DIFF_V, tq), f32)) * 2
    nc = kc_ref.shape[0]
    scores(1, kc_ref[...])
    if n_lat:
        scores(0, k_block(0))
        carry = update(init, 1, nc, vc_ref[...])

        def body(jj, carry):
            j = 2 * jj
            scores(1, k_block(j + 1))
            carry = update(carry, 0, tk, vt_block(j))
            scores(0, k_block(jnp.minimum(j + 2, n_lat - 1)))
            return update(carry, 1, tk, vt_block(j + 1))

        carry = lax.fori_loop(0, n_lat // 2, body, carry)
    else:
        carry = update(init, 1, nc, vc_ref[...])
    _, l0, a0, _, l1, a1 = carry
    o = a0 / l0 - lam * (a1 / l1)
    ms = jnp.mean(o * o, axis=0, keepdims=True)
    y = (o * lax.rsqrt(ms + EPS) * g_ref[...]) * (1.0 - lam_init)
    o_ref[...] = y.T.astype(o_ref.dtype)


def _diff_attention(q, k, vt, lam_vecs, subln_g, lam_init, *, batch, seq, ctx_len):
    heads, r, _ = q.shape
    tq, tk = 512, 512
    ctx_blk0 = (batch * seq) // ctx_len
    assert (seq // tk) % 2 == 0
    small = [pl.BlockSpec((4, DIFF_QK), lambda *_: (0, 0)), pl.BlockSpec((DIFF_V, 1), lambda *_: (0, 0))]
    kc_spec = pl.BlockSpec((None, ctx_len, 128), lambda b, h, *_: (h, ctx_blk0 + b, 0))
    vc_spec = pl.BlockSpec((None, 128, ctx_len), lambda b, h, *_: (h, 0, ctx_blk0 + b))
    g2 = subln_g.reshape(DIFF_V, 1)
    lat = pl.pallas_call(
        functools.partial(_attn_kernel, n_lat=seq // tk, tk=tk, lam_init=lam_init),
        grid=(batch, heads, seq // tq),
        in_specs=small + [pl.BlockSpec((None, tq, 128), lambda b, h, i: (h, b * (seq // tq) + i, 0)),
                          kc_spec, vc_spec,
                          pl.BlockSpec((None, seq, 128), lambda b, h, i: (h, b, 0)),
                          pl.BlockSpec((None, 128, seq), lambda b, h, i: (h, 0, b))],
        out_specs=pl.BlockSpec((tq, 128), lambda b, h, i: (b * (seq // tq) + i, h)),
        out_shape=jax.ShapeDtypeStruct((batch * seq, heads * DIFF_V), bf16),
        scratch_shapes=[pltpu.VMEM((2, 2, tk, tq), f32)],
        compiler_params=_cparams(("arbitrary",) * 3),
        name="diff_attn_lat",
    )(lam_vecs, g2, q, k, vt, k, vt)
    ctx = pl.pallas_call(
        functools.partial(_attn_kernel, n_lat=0, tk=tk, lam_init=lam_init),
        grid=(batch, heads),
        in_specs=small + [kc_spec, kc_spec, vc_spec],
        out_specs=pl.BlockSpec((ctx_len, 128), lambda b, h: (b, h)),
        out_shape=jax.ShapeDtypeStruct((batch * ctx_len, heads * DIFF_V), bf16),
        scratch_shapes=[pltpu.VMEM((2, 2, ctx_len, ctx_len), f32)],
        compiler_params=_cparams(("arbitrary",) * 2),
        name="diff_attn_ctx",
    )(lam_vecs, g2, q, k, vt)
    return jnp.concatenate([lat, ctx], axis=0)


def _pool_kernel(p_ref, prev_ref, next_ref, w_ref, scale_ref, o_ref, ext_scr, *,
                 tiles_per_batch, batch, seq, ctx_len):
    i = pl.program_id(0)
    tp = p_ref.shape[0]
    is_ctx = i >= batch * tiles_per_batch
    j = jnp.where(is_ctx, 0, i % tiles_per_batch)
    seg_len = jnp.where(is_ctx, ctx_len, seq)
    last = jnp.where(is_ctx, 0, tiles_per_batch - 1)
    ext_scr[0:8, :] = jnp.where(j > 0, prev_ref[...], 0.0)
    ext_scr[8:8 + tp, :] = p_ref[...]
    ext_scr[8 + tp:16 + tp, :] = jnp.where(j < last, next_ref[...], 0.0)
    pos = j * tp + lax.broadcasted_iota(i32, (tp, 1), 0)
    for gi, w in enumerate(POOL_WINDOWS):
        c0 = gi * POOL_GROUP
        half = w // 2
        tot = ext_scr[8 - half:8 - half + tp, c0:c0 + POOL_GROUP]
        for dlt in range(1 - half, half):
            tot = tot + ext_scr[8 + dlt:8 + dlt + tp, c0:c0 + POOL_GROUP]
        cnt = (jnp.minimum(pos - half + w, seg_len) - jnp.maximum(pos - half, 0)).astype(f32)
        dlt_mean = (tot / cnt - ext_scr[8:8 + tp, c0:c0 + POOL_GROUP]).astype(bf16)
        y = jnp.dot(dlt_mean, w_ref[gi].astype(bf16), preferred_element_type=f32)
        o_ref[:, c0:c0 + POOL_GROUP] = (y * scale_ref[:, c0:c0 + POOL_GROUP]).astype(o_ref.dtype)


def _pool(p, pool_w, pool_scale, *, batch, seq, ctx_len):
    r, width = p.shape
    tp = ctx_len
    tpb = seq // tp
    return pl.pallas_call(
        functools.partial(_pool_kernel, tiles_per_batch=tpb, batch=batch, seq=seq, ctx_len=ctx_len),
        grid=(r // tp,),
        in_specs=[pl.BlockSpec((tp, width), lambda i: (i, 0)),
                  pl.BlockSpec((8, width), lambda i: (jnp.maximum(i * (tp // 8) - 1, 0), 0)),
                  pl.BlockSpec((8, width), lambda i: (jnp.minimum((i + 1) * (tp // 8), r // 8 - 1), 0)),
                  pl.BlockSpec(pool_w.shape, lambda i: (0, 0, 0)),
                  pl.BlockSpec((1, width), lambda i: (0, 0))],
        out_specs=pl.BlockSpec((tp, width), lambda i: (i, 0)),
        out_shape=jax.ShapeDtypeStruct((r, width), bf16),
        scratch_shapes=[pltpu.VMEM((tp + 16, width), f32)],
        compiler_params=_cparams(("arbitrary",)),
        name="pool",
    )(p, p, p, pool_w, pool_scale.reshape(1, width))


def _outproj_kernel(a_ref, b_ref, x_ref, w_ref, g1_ref, sh_ref, sc_ref, gn_ref, wr_ref, br_ref,
                    xn_ref, h2_ref, lg_ref, *, tiles_per_batch, batch):
    seg = _tile_segment(pl.program_id(0), tiles_per_batch, batch)
    half = a_ref.shape[1]
    y = (jnp.dot(a_ref[...], w_ref[0:half, :], preferred_element_type=f32)
         + jnp.dot(b_ref[...], w_ref[half:, :], preferred_element_type=f32))
    xn = x_ref[...] + g1_ref[pl.ds(seg, 1), :] * y
    xn_ref[...] = xn
    h2 = _norm_mod(xn, gn_ref[...], sh_ref[pl.ds(seg, 1), :], sc_ref[pl.ds(seg, 1), :])
    h2_ref[...] = h2
    lg_ref[...] = jnp.dot(h2.astype(bf16), wr_ref[...], preferred_element_type=f32) + br_ref[...]


def _outproj(a, b, x, w, mods, g_ffn, w_r, b_r, *, n_tiles, batch, seq):
    r, d = n_tiles * TM, x.shape[1]
    half = a.shape[1]
    row = lambda i: (i, 0)
    return pl.pallas_call(
        functools.partial(_outproj_kernel, tiles_per_batch=seq // TM, batch=batch),
        grid=(n_tiles,),
        in_specs=[pl.BlockSpec((TM, half), row), pl.BlockSpec((TM, half), row), pl.BlockSpec((TM, d), row),
                  pl.BlockSpec(w.shape, lambda i: (0, 0), pipeline_mode=pl.Buffered(1)),
                  pl.BlockSpec((8, d), lambda i: (0, 2)),
                  pl.BlockSpec((8, d), lambda i: (0, 3)),
                  pl.BlockSpec((8, d), lambda i: (0, 4)),
                  pl.BlockSpec((1, d), lambda i: (0, 0)),
                  pl.BlockSpec(w_r.shape, lambda i: (0, 0)),
                  pl.BlockSpec((1, 128), lambda i: (0, 0))],
        out_specs=[pl.BlockSpec((TM, d), row), pl.BlockSpec((TM, d), row), pl.BlockSpec((TM, 128), row)],
        out_shape=[jax.ShapeDtypeStruct((r, d), f32), jax.ShapeDtypeStruct((r, d), f32),
                   jax.ShapeDtypeStruct((r, 128), f32)],
        compiler_params=_cparams(("arbitrary",)),
        name="outproj",
    )(a, b, x, w, mods, mods, mods, g_ffn, w_r, b_r)


def _moe_kernel(be_ref, nu_ref, xs_ref, w1_ref, w3_ref, w2_ref, ys_ref, w1_scr, w3_scr, w2_scr):
    i = pl.program_id(0)
    e = be_ref[i]
    prev = be_ref[jnp.maximum(i - 1, 0)]

    @pl.when((i == 0) | (e != prev))
    def _():
        w1_scr[...] = w1_ref[...].astype(bf16)
        w3_scr[...] = w3_ref[...].astype(bf16)
        w2_scr[...] = w2_ref[...].astype(bf16)

    @pl.when(i < nu_ref[0])
    def _():
        xb = xs_ref[...].astype(bf16)
        a = jnp.dot(xb, w1_scr[...], preferred_element_type=f32)
        b = jnp.dot(xb, w3_scr[...], preferred_element_type=f32)
        hid = (a * jax.nn.sigmoid(a) * b).astype(bf16)
        ys_ref[...] = jnp.dot(hid, w2_scr[...], preferred_element_type=f32)

    @pl.when(i >= nu_ref[0])
    def _():
        ys_ref[...] = jnp.zeros_like(ys_ref)


def _moe_experts(xs, block_e, n_used, w1, w3, w2, layer, nb):
    m = MOE_BLOCK
    _, _, d, de = w1.shape
    grid_spec = pltpu.PrefetchScalarGridSpec(
        num_scalar_prefetch=2,
        grid=(nb,),
        in_specs=[pl.BlockSpec((m, d), lambda i, be, nu: (jnp.minimum(i, nu[0] - 1), 0)),
                  pl.BlockSpec((None, None, d, de), lambda i, be, nu: (layer, be[i], 0, 0)),
                  pl.BlockSpec((None, None, d, de), lambda i, be, nu: (layer, be[i], 0, 0)),
                  pl.BlockSpec((None, None, de, d), lambda i, be, nu: (layer, be[i], 0, 0))],
        out_specs=pl.BlockSpec((m, d), lambda i, be, nu: (i, 0)),
        scratch_shapes=[pltpu.VMEM((d, de), bf16), pltpu.VMEM((d, de), bf16), pltpu.VMEM((de, d), bf16)],
    )
    return pl.pallas_call(
        _moe_kernel,
        grid_spec=grid_spec,
        out_shape=jax.ShapeDtypeStruct((nb * m, d), f32),
        compiler_params=_cparams(("arbitrary",)),
        name="moe_experts",
    )(block_e, n_used, xs, w1, w3, w2)


def _route_kernel(lg_ref, tri_ref, o_ref, cnt_ref, run_scr):
    @pl.when(pl.program_id(0) == 0)
    def _():
        run_scr[...] = jnp.zeros_like(run_scr)

    lg = lg_ref[...]
    lane = lax.broadcasted_iota(i32, lg.shape, 1)
    ninf = -jnp.inf

    def first_lane(mask):
        return jnp.min(jnp.where(mask, lane, lg.shape[1]), axis=-1, keepdims=True)

    gl = jnp.where(lane < MOE_GROUPS, lg, ninf)
    gmax = jnp.max(gl, axis=-1, keepdims=True)
    pg_top = 1.0 / jnp.sum(jnp.exp(gl - gmax), axis=-1, keepdims=True)
    lo = MOE_GROUPS + first_lane(gl == gmax) * MOE_PER_GROUP
    sel = (lane >= lo) & (lane < lo + MOE_PER_GROUP)
    el = jnp.where(sel, lg, ninf)
    ee = jnp.exp(el - jnp.max(el, axis=-1, keepdims=True))
    pe = jnp.where(sel, ee / jnp.sum(ee, axis=-1, keepdims=True), -1.0)
    p1 = jnp.max(pe, axis=-1, keepdims=True)
    i1 = first_lane(pe == p1)
    pe2 = jnp.where(lane == i1, -1.0, pe)
    p2 = jnp.max(pe2, axis=-1, keepdims=True)
    i2 = first_lane(pe2 == p2)
    den = p1 + p2
    e1 = i1 - MOE_GROUPS
    e2 = i2 - MOE_GROUPS
    hit1 = lane == e1
    hit2 = lane == e2
    onehot = jnp.where(hit1, 1.0, jnp.where(hit2, 1.0, 0.0))
    base = jnp.dot(tri_ref[...], onehot.astype(bf16), preferred_element_type=f32) + run_scr[...]
    r1 = jnp.sum(jnp.where(hit1, base, 0.0), axis=-1, keepdims=True)
    r2 = jnp.sum(jnp.where(hit2, base, 0.0), axis=-1, keepdims=True)
    run_scr[...] = run_scr[...] + jnp.sum(onehot, axis=0, keepdims=True)
    cnt_ref[...] = run_scr[...]
    vals = (e1.astype(f32), e2.astype(f32), pg_top * p1 / den, pg_top * p2 / den, r1, r2)
    out = jnp.zeros(lg.shape, f32)
    for j, v in enumerate(vals):
        out = jnp.where(lane == j, v, out)
    o_ref[...] = out


def _route(logits):
    t, width = logits.shape
    tri = jnp.tril(jnp.ones((TM, TM), f32), -1).astype(bf16)
    return pl.pallas_call(
        _route_kernel,
        grid=(t // TM,),
        in_specs=[pl.BlockSpec((TM, width), lambda i: (i, 0)), pl.BlockSpec((TM, TM), lambda i: (0, 0))],
        out_specs=[pl.BlockSpec((TM, width), lambda i: (i, 0)), pl.BlockSpec((1, width), lambda i: (0, 0))],
        out_shape=[jax.ShapeDtypeStruct((t, width), f32), jax.ShapeDtypeStruct((1, width), f32)],
        scratch_shapes=[pltpu.VMEM((1, width), f32)],
        compiler_params=_cparams(("arbitrary",)),
        name="route",
    )(logits, tri)


def _dispatch_kernel(dest_ref, h_ref, init_ref, xs_ref, sem):
    del init_ref
    i = pl.program_id(0)
    tm = h_ref.shape[0]

    def row_copy(r, k):
        slot = dest_ref[(i * tm + r) * MOE_TOPK + k]
        return pltpu.make_async_copy(h_ref.at[pl.ds(r, 1)], xs_ref.at[pl.ds(slot, 1)], sem)

    def issue(r, carry):
        for k in range(MOE_TOPK):
            row_copy(r, k).start()
        return carry

    lax.fori_loop(0, tm, issue, 0, unroll=8)
    for k in range(MOE_TOPK):
        pltpu.make_async_copy(h_ref, xs_ref.at[pl.ds(0, tm)], sem).wait()


def _dispatch(h2, dest, nb):
    t, d = h2.shape
    rows = nb * MOE_BLOCK
    grid_spec = pltpu.PrefetchScalarGridSpec(
        num_scalar_prefetch=1,
        grid=(t // TM,),
        in_specs=[pl.BlockSpec((TM, d), lambda i, dst: (i, 0)), pl.BlockSpec(memory_space=pl.ANY)],
        out_specs=pl.BlockSpec(memory_space=pl.ANY),
        scratch_shapes=[pltpu.SemaphoreType.DMA(())],
    )
    return pl.pallas_call(
        _dispatch_kernel,
        grid_spec=grid_spec,
        out_shape=jax.ShapeDtypeStruct((rows, d), h2.dtype),
        input_output_aliases={2: 0},
        compiler_params=_cparams(("arbitrary",)),
        name="moe_dispatch",
    )(dest, h2, jnp.zeros((rows, d), h2.dtype))


def _combine_kernel(dest_ref, x_ref, rt_ref, g2_ref, gn_ref, ys_ref, o_ref, y_scr, sem, *,
                    tiles_per_batch, batch, final_norm):
    i = pl.program_id(0)
    tm = x_ref.shape[0]

    def row_copy(r, k):
        slot = dest_ref[(i * tm + r) * MOE_TOPK + k]
        return pltpu.make_async_copy(ys_ref.at[pl.ds(slot, 1)], y_scr.at[k, pl.ds(r, 1)], sem)

    def issue(r, carry):
        for k in range(MOE_TOPK):
            row_copy(r, k).start()
        return carry

    lax.fori_loop(0, tm, issue, 0, unroll=8)
    for k in range(MOE_TOPK):
        pltpu.make_async_copy(ys_ref.at[pl.ds(0, tm)], y_scr.at[k], sem).wait()
    seg = _tile_segment(i, tiles_per_batch, batch)
    rt = rt_ref[...]
    y = rt[:, MOE_TOPK:MOE_TOPK + 1] * y_scr[0]
    for k in range(1, MOE_TOPK):
        y = y + rt[:, MOE_TOPK + k:MOE_TOPK + k + 1] * y_scr[k]
    x = x_ref[...] + g2_ref[pl.ds(seg, 1), :] * y
    if final_norm:
        ms = jnp.mean(x * x, axis=-1, keepdims=True)
        x = x * lax.rsqrt(ms + EPS) * gn_ref[...]
    o_ref[...] = x


def _combine(x, ys, dest, routed, mods, g_out, *, n_tiles, batch, seq, final_norm):
    d = x.shape[1]
    row = lambda i, dst: (i, 0)
    grid_spec = pltpu.PrefetchScalarGridSpec(
        num_scalar_prefetch=1,
        grid=(n_tiles,),
        in_specs=[pl.BlockSpec((TM, d), row),
                  pl.BlockSpec((TM, routed.shape[1]), row),
                  pl.BlockSpec((8, d), lambda i, dst: (0, 5)),
                  pl.BlockSpec((1, d), lambda i, dst: (0, 0)),
                  pl.BlockSpec(memory_space=pl.ANY)],
        out_specs=pl.BlockSpec((TM, d), row),
        scratch_shapes=[pltpu.VMEM((MOE_TOPK, TM, d), f32), pltpu.SemaphoreType.DMA(())],
    )
    return pl.pallas_call(
        functools.partial(_combine_kernel, tiles_per_batch=seq // TM, batch=batch, final_norm=final_norm),
        grid_spec=grid_spec,
        out_shape=jax.ShapeDtypeStruct((n_tiles * TM, d), f32),
        compiler_params=_cparams(("arbitrary",)),
        name="moe_combine",
    )(dest, x, routed, mods, g_out, ys)


def _moe(xn, h2, logits, mods, w1, w3, w2, layer, g_out, *, n_tiles, batch, seq, final_norm):
    t = n_tiles * TM
    m = MOE_BLOCK
    routed, cnt = _route(logits)
    s = t * MOE_TOPK
    nb = s // m + MOE_EXPERTS
    expert = routed[:, 0:MOE_TOPK].astype(i32)
    rank = routed[:, 2 * MOE_TOPK:3 * MOE_TOPK].astype(i32)
    counts = cnt[0, :MOE_EXPERTS].astype(i32)
    padded = ((counts + m - 1) // m) * m
    pad_end = jnp.cumsum(padded)
    onehot = expert[:, :, None] == jnp.arange(MOE_EXPERTS, dtype=i32)[None, None, :]
    dest = (jnp.sum(jnp.where(onehot, (pad_end - padded)[None, None, :], 0), axis=-1) + rank).reshape(s)
    n_used = (pad_end[-1] // m).astype(i32)
    blk = jnp.arange(nb, dtype=i32)
    block_e = jnp.minimum(jnp.searchsorted(pad_end, jnp.minimum(blk, n_used - 1) * m, side='right'),
                          MOE_EXPERTS - 1).astype(i32)
    xs = _dispatch(h2, dest, nb)
    ys = _moe_experts(xs, block_e, n_used.reshape(1), w1, w3, w2, layer, nb)
    return _combine(xn, ys, dest, routed, mods, g_out, n_tiles=n_tiles, batch=batch, seq=seq,
                    final_norm=final_norm)


HY_N2 = 128
HY_WB = 256
HY_SPEC_WB = 128


def _short_conv_kernel(x_ref, prev_ref, next_ref, w_ref, b_ref, v_ref, x1_ref, x2_ref, ext_scr, *, tiles_per_batch):
    j = pl.program_id(0) % tiles_per_batch
    tt = x_ref.shape[0]
    ext_scr[0:8, :] = jnp.where(j > 0, prev_ref[...], 0.0)
    ext_scr[8:8 + tt, :] = x_ref[...]
    ext_scr[8 + tt:16 + tt, :] = jnp.where(j < tiles_per_batch - 1, next_ref[...], 0.0)
    w = w_ref[...]
    width = v_ref.shape[1]
    for part, o_ref in enumerate((v_ref, x1_ref, x2_ref)):
        cols = slice(part * width, (part + 1) * width)
        o_ref[...] = (ext_scr[7:7 + tt, cols] * w[0:1, cols] + ext_scr[8:8 + tt, cols] * w[1:2, cols]
                      + ext_scr[9:9 + tt, cols] * w[2:3, cols] + b_ref[:, cols])


def _short_conv(hy, w, b, *, batch, seq):
    n_lat = batch * seq
    tt = 256
    width = hy.shape[1]
    out = jax.ShapeDtypeStruct((n_lat, width // 3), f32)
    o_spec = pl.BlockSpec((tt, width // 3), lambda i: (i, 0))
    return pl.pallas_call(
        functools.partial(_short_conv_kernel, tiles_per_batch=seq // tt),
        grid=(n_lat // tt,),
        in_specs=[pl.BlockSpec((tt, width), lambda i: (i, 0)),
                  pl.BlockSpec((8, width), lambda i: (jnp.maximum(i * (tt // 8) - 1, 0), 0)),
                  pl.BlockSpec((8, width), lambda i: ((i + 1) * (tt // 8), 0)),
                  pl.BlockSpec((3, width), lambda i: (0, 0)),
                  pl.BlockSpec((1, width), lambda i: (0, 0))],
        out_specs=[o_spec, o_spec, o_spec],
        out_shape=[out, out, out],
        scratch_shapes=[pltpu.VMEM((tt + 16, width), f32)],
        compiler_params=_cparams(("arbitrary",)),
        name="short_conv",
    )(hy, hy, hy, w, b.reshape(1, width))


def _hy_filter_kernel(z_ref, w1_ref, b1_ref, fr_ref, w2_ref, b2_ref, w3_ref, dl_ref, k_ref, nrm_ref):
    i = pl.program_id(0)
    z = z_ref[...]
    fr = fr_ref[...]
    h = jnp.sin(fr[0:1] * (jnp.dot(z.astype(bf16), w1_ref[...], preferred_element_type=f32) + b1_ref[...]))
    h = jnp.sin(fr[1:2] * (jnp.dot(h.astype(bf16), w2_ref[...], preferred_element_type=f32) + b2_ref[...]))
    kk = jnp.dot(h.astype(bf16), w3_ref[...], preferred_element_type=f32)
    width = dl_ref.shape[1]
    decay = jnp.exp(-z[:, 0:1] * jnp.abs(dl_ref[...]))
    tl = z.shape[0]
    row = i * tl + lax.broadcasted_iota(i32, (tl, 1), 0)
    tot = []
    for blk in range(kk.shape[1] // width):
        kb = kk[:, blk * width:(blk + 1) * width] * decay
        if blk % 2 == 1:
            kb = jnp.where(row == 0, 0.0, kb)
        k_ref[:, blk * width:(blk + 1) * width] = kb
        tot.append(jnp.sum(jnp.abs(kb), axis=0, keepdims=True))
    s = jnp.concatenate(tot, axis=1)

    @pl.when(i == 0)
    def _():
        nrm_ref[...] = s

    @pl.when(i > 0)
    def _():
        nrm_ref[...] = nrm_ref[...] + s


def _hy_filters(seq, w1, b1, freq, w2, b2, w3):
    t = jnp.linspace(0.0, 1.0, seq, dtype=f32)[:, None]
    bands = (HY_EMB - 1) // 2
    w = 2 * math.pi * jnp.arange(seq, dtype=f32)[:, None] / seq
    f = jnp.linspace(1e-4, bands - 1, bands, dtype=f32)[None, :]
    z = jnp.concatenate([t, jnp.cos(f * w), -jnp.sin(f * w), jnp.zeros((seq, 128 - HY_EMB), f32)], axis=-1)
    w1p = jnp.concatenate([w1, jnp.zeros((128 - HY_EMB, w1.shape[1]), f32)], axis=0).astype(bf16)
    hid = w1.shape[1]
    n_out = w3.shape[1]
    deltas = jnp.linspace(math.log(HY_TARGET) / HY_SLOW_DECAY, math.log(HY_TARGET) / HY_FAST_DECAY,
                          HY_WIDTH, dtype=f32).reshape(1, HY_WIDTH)
    tl = 512
    const = lambda a: pl.BlockSpec(a.shape, lambda i: (0,) * a.ndim)
    ins = [z, w1p, b1.reshape(1, hid), freq, w2.astype(bf16), b2.reshape(1, hid), w3.astype(bf16), deltas]
    return pl.pallas_call(
        _hy_filter_kernel,
        grid=(seq // tl,),
        in_specs=[pl.BlockSpec((tl, 128), lambda i: (i, 0))] + [const(a) for a in ins[1:]],
        out_specs=[pl.BlockSpec((tl, n_out), lambda i: (i, 0)), pl.BlockSpec((1, n_out), lambda i: (0, 0))],
        out_shape=[jax.ShapeDtypeStruct((seq, n_out), f32), jax.ShapeDtypeStruct((1, n_out), f32)],
        compiler_params=_cparams(("arbitrary",)),
        name="hy_filter",
    )(*ins)


def _dft_tables(seq):
    n2 = HY_N2
    n_tot = 2 * seq
    n1 = n_tot // n2
    nf = n1 // 2 + 1
    nfp = -(-nf // 8) * 8
    f1 = jnp.arange(nfp, dtype=i32)
    k1 = jnp.arange(n1 // 2, dtype=i32)
    th = (2 * math.pi / n1) * ((f1[:, None] * k1[None, :]) % n1).astype(f32)
    valid = (f1 < nf)[:, None]
    fwd = jnp.concatenate([jnp.where(valid, jnp.cos(th), 0.0), jnp.where(valid, -jnp.sin(th), 0.0)], axis=0)
    eye = jnp.eye(8, dtype=f32)
    fk = jnp.einsum('rk,jh->rjkh', fwd, eye).reshape(2 * nfp * 8, (n1 // 2) * 8)
    cw = jnp.where(valid, jnp.where((f1 == 0) | (f1 == n1 // 2), 1.0, 2.0)[:, None], 0.0) / n_tot
    inv = jnp.concatenate([(cw * jnp.cos(th)).T, (-cw * jnp.sin(th)).T], axis=1)
    ck = jnp.einsum('kr,jh->kjrh', inv, eye).reshape((n1 // 2) * 8, 2 * nfp * 8)
    fa = jnp.arange(nf, dtype=i32)[:, None, None]
    f2 = jnp.arange(n2, dtype=i32)[None, :, None]
    m2 = jnp.arange(n2, dtype=i32)[None, None, :]
    ph = (2 * math.pi / n_tot) * ((f2 * m2 * n1 + m2 * fa) % n_tot).astype(f32)
    gr, gi = jnp.cos(ph), -jnp.sin(ph)
    g = jnp.concatenate([jnp.concatenate([gr, -gi], axis=2), jnp.concatenate([gi, gr], axis=2)], axis=1)
    hr, hi = jnp.swapaxes(jnp.cos(ph), 1, 2), jnp.swapaxes(jnp.sin(ph), 1, 2)
    h = jnp.concatenate([jnp.concatenate([hr, -hi], axis=2), jnp.concatenate([hi, hr], axis=2)], axis=1)
    return fk.astype(bf16), ck.astype(bf16), g.astype(bf16), h.astype(bf16), nf, nfp


def _hy_rows_per_step(nf):
    return max(d for d in (5, 4, 3, 2, 1) if nf % d == 0)


def _dft_stage1(x_ref, fk_ref, a_scr):
    n1h, n2, wb = x_ref.shape

    def slab(s, carry):
        rows = pl.ds(pl.multiple_of(s * 8, 8), 8)
        xs = x_ref[:, rows, :].reshape(n1h * 8, wb).astype(bf16)
        a = jnp.dot(fk_ref[...], xs, preferred_element_type=f32)
        a_scr[:, rows, :] = a.reshape(a_scr.shape[0], 8, wb)
        return carry

    lax.fori_loop(0, n2 // 8, slab, 0)


def _dft_stage2(a_scr, g, f1, nfp):
    a = jnp.concatenate([a_scr[f1], a_scr[nfp + f1]], axis=0).astype(bf16)
    return jnp.dot(g, a, preferred_element_type=f32)


def _hy_spectrum_kernel(kf_ref, kb_ref, fk_ref, g_ref, o_ref, af_scr, ab_scr, *, nfp):
    fs = pl.program_id(2)

    @pl.when(fs == 0)
    def _():
        _dft_stage1(kf_ref, fk_ref, af_scr)
        _dft_stage1(kb_ref, fk_ref, ab_scr)

    fb = g_ref.shape[0]
    n2 = kf_ref.shape[1]
    for j in range(fb):
        f1 = fs * fb + j
        sf = _dft_stage2(af_scr, g_ref[j], f1, nfp)
        sb = _dft_stage2(ab_scr, g_ref[j], f1, nfp)
        o_ref[0, j] = sf[:n2] + sb[:n2]
        o_ref[1, j] = sf[n2:] - sb[n2:]


def _hy_spectrum(k, fk, g, nf, nfp, seq):
    n2, wb = HY_N2, HY_SPEC_WB
    n1h = seq // n2
    fb = _hy_rows_per_step(nf)
    nwb = HY_WIDTH // wb
    k3 = k.reshape(n1h, n2, k.shape[1])
    return pl.pallas_call(
        functools.partial(_hy_spectrum_kernel, nfp=nfp),
        grid=(HY_ORDER, nwb, nf // fb),
        in_specs=[pl.BlockSpec((n1h, n2, wb), lambda o, w, f: (0, 0, (2 * o) * nwb + w)),
                  pl.BlockSpec((n1h, n2, wb), lambda o, w, f: (0, 0, (2 * o + 1) * nwb + w)),
                  pl.BlockSpec(fk.shape, lambda o, w, f: (0, 0)),
                  pl.BlockSpec((fb, 2 * n2, 2 * n2), lambda o, w, f: (f, 0, 0))],
        out_specs=pl.BlockSpec((None, 2, fb, n2, wb), lambda o, w, f: (o, 0, f, 0, w)),
        out_shape=jax.ShapeDtypeStruct((HY_ORDER, 2, nf, n2, HY_WIDTH), f32),
        scratch_shapes=[pltpu.VMEM((2 * nfp, n2, wb), f32), pltpu.VMEM((2 * nfp, n2, wb), f32)],
        compiler_params=_cparams(("arbitrary",) * 3),
        name="hy_spectrum",
    )(k3, k3, fk, g)


def _hy_conv_kernel(x_ref, gate_ref, kf_ref, g_ref, h_ref, fk_ref, ck_ref, bias_ref, inv_ref, o_ref, a_scr, *, nfp):
    fs = pl.program_id(2)
    n1h, n2, wb = x_ref.shape

    @pl.when(fs == 0)
    def _():
        _dft_stage1(x_ref, fk_ref, a_scr)

    fb = g_ref.shape[0]
    for j in range(fb):
        f1 = fs * fb + j
        xs = _dft_stage2(a_scr, g_ref[j], f1, nfp)
        xr, xi = xs[:n2], xs[n2:]
        kr, ki = kf_ref[0, j], kf_ref[1, j]
        y = jnp.concatenate([xr * kr - xi * ki, xr * ki + xi * kr], axis=0).astype(bf16)
        bb = jnp.dot(h_ref[j], y, preferred_element_type=f32)
        a_scr[f1] = bb[:n2]
        a_scr[nfp + f1] = bb[n2:]

    @pl.when(fs == pl.num_programs(2) - 1)
    def _():
        bias = bias_ref[...].reshape(1, 1, wb)
        inv = inv_ref[...].reshape(1, 1, wb)

        def slab(s, carry):
            rows = pl.ds(pl.multiple_of(s * 8, 8), 8)
            bs = a_scr[:, rows, :].reshape(2 * nfp * 8, wb).astype(bf16)
            y = jnp.dot(ck_ref[...], bs, preferred_element_type=f32).reshape(n1h, 8, wb)
            o_ref[:, rows, :] = (gate_ref[:, rows, :] * (y * inv + x_ref[:, rows, :] * bias)).astype(o_ref.dtype)
            return carry

        lax.fori_loop(0, n2 // 8, slab, 0)


def _hy_conv(x, gate, kf, order, tabs, bias, inv_norm, out_dtype, *, batch, seq):
    fk, ck, g, h, nf, nfp = tabs
    n2, wb = HY_N2, HY_WB
    n1h = seq // n2
    fb = _hy_rows_per_step(nf)
    width = x.shape[1]
    big = pl.BlockSpec((n1h, n2, wb), lambda b, w, f: (b, 0, w), pipeline_mode=pl.Buffered(1))
    out = pl.pallas_call(
        functools.partial(_hy_conv_kernel, nfp=nfp),
        grid=(batch, width // wb, nf // fb),
        in_specs=[big, big,
                  pl.BlockSpec((None, 2, fb, n2, wb), lambda b, w, f: (order, 0, f, 0, w)),
                  pl.BlockSpec((fb, 2 * n2, 2 * n2), lambda b, w, f: (f, 0, 0)),
                  pl.BlockSpec((fb, 2 * n2, 2 * n2), lambda b, w, f: (f, 0, 0)),
                  pl.BlockSpec(fk.shape, lambda b, w, f: (0, 0)),
                  pl.BlockSpec(ck.shape, lambda b, w, f: (0, 0)),
                  pl.BlockSpec((1, wb), lambda b, w, f: (0, w)),
                  pl.BlockSpec((1, wb), lambda b, w, f: (0, w))],
        out_specs=big,
        out_shape=jax.ShapeDtypeStruct((batch * n1h, n2, width), out_dtype),
        scratch_shapes=[pltpu.VMEM((2 * nfp, n2, wb), f32)],
        compiler_params=_cparams(("arbitrary",) * 3),
        name="hy_conv",
    )(x.reshape(batch * n1h, n2, width), gate.reshape(batch * n1h, n2, width), kf, g, h, fk, ck,
      bias.reshape(1, width), inv_norm.reshape(1, width))
    return out.reshape(batch * seq, width)


def _hyena(hy, conv_w, conv_b, f_w1, f_b1, f_freq, f_w2, f_b2, f_w3, hy_bias, *, batch, seq):
    v, x1, x2 = _short_conv(hy, conv_w, conv_b, batch=batch, seq=seq)
    k, sums = _hy_filters(seq, f_w1, f_b1, f_freq, f_w2, f_b2, f_w3)
    sums = sums.reshape(HY_ORDER, 2, HY_WIDTH)
    inv_norm = 1.0 / (sums[:, 0] + sums[:, 1])
    tabs = _dft_tables(seq)
    kf = _hy_spectrum(k, tabs[0], tabs[2], tabs[4], tabs[5], seq)
    z = _hy_conv(v, x1, kf, 0, tabs, hy_bias[0], inv_norm[0], f32, batch=batch, seq=seq)
    return _hy_conv(z, x2, kf, 1, tabs, hy_bias[1], inv_norm[1], bf16, batch=batch, seq=seq)


S5_TT = 256
S5_GB = 8


def _gelu_tanh(x):
    return 0.5 * x * (1.0 + jnp.tanh(0.7978845608028654 * (x + 0.044715 * x * x * x)))


def _s5_kernel(u_ref, perm_ref, permt_ref, wb_ref, wc_ref, are_ref, aim_ref, pre_ref, pim_ref, *rest, fuse_glu):
    if fuse_glu:
        yf_ref, d_ref, gw_ref, gb_ref, o_ref, st_scr, bu_scr, y_scr = rest
    else:
        o_ref, st_scr, bu_scr, y_scr = rest
    tt = u_ref.shape[0]
    steps = tt // 8
    half = bu_scr.shape[1] // 2

    @pl.when(pl.program_id(1) == 0)
    def _():
        st_scr[...] = jnp.zeros_like(st_scr)

    up = jnp.dot(perm_ref[...], u_ref[...].astype(bf16), preferred_element_type=f32).astype(bf16)
    for gb in range(wb_ref.shape[0]):
        bu_scr[...] = jnp.dot(up[:, gb * 128:(gb + 1) * 128], wb_ref[gb], preferred_element_type=f32)
        a_re = jnp.broadcast_to(are_ref[gb], (8, half))
        a_im = jnp.broadcast_to(aim_ref[gb], (8, half))

        def scan_step(k, h):
            hr, hi = h
            rows = pl.ds(pl.multiple_of(k * 8, 8), 8)
            nr = a_re * hr - a_im * hi + bu_scr[rows, 0:half]
            ni = a_re * hi + a_im * hr + bu_scr[rows, half:]
            bu_scr[rows, 0:half] = nr
            bu_scr[rows, half:] = ni
            return nr, ni

        zero = jnp.zeros((8, half), f32)
        hr, hi = lax.fori_loop(0, steps, scan_step, (zero, zero), unroll=4)

        pw_r = pre_ref[gb, steps - 1:steps, :]
        pw_i = pim_ref[gb, steps - 1:steps, :]
        cur_r = st_scr[gb, :, 0:half]
        cur_i = st_scr[gb, :, half:]
        car_r, car_i = [], []
        for j in range(8):
            car_r.append(cur_r)
            car_i.append(cur_i)
            cur_r, cur_i = (hr[j:j + 1] + pw_r * cur_r - pw_i * cur_i,
                            hi[j:j + 1] + pw_r * cur_i + pw_i * cur_r)
        st_scr[gb, :, 0:half] = cur_r
        st_scr[gb, :, half:] = cur_i
        cr = jnp.concatenate(car_r, axis=0)
        ci = jnp.concatenate(car_i, axis=0)

        def fix_step(k, carry):
            rows = pl.ds(pl.multiple_of(k * 8, 8), 8)
            pr = pre_ref[gb, pl.ds(k, 1), :]
            pi = pim_ref[gb, pl.ds(k, 1), :]
            bu_scr[rows, 0:half] = bu_scr[rows, 0:half] + (pr * cr - pi * ci)
            bu_scr[rows, half:] = bu_scr[rows, half:] + (pr * ci + pi * cr)
            return carry

        lax.fori_loop(0, steps, fix_step, 0, unroll=4)
        y_scr[:, gb * 128:(gb + 1) * 128] = jnp.dot(bu_scr[...].astype(bf16), wc_ref[gb],
                                                      preferred_element_type=f32)
    y = jnp.dot(permt_ref[...], y_scr[...], preferred_element_type=f32, precision=lax.Precision.HIGHEST)
    if fuse_glu:
        g = _gelu_tanh(y + yf_ref[...] + u_ref[...] * d_ref[...])
        z = jnp.dot(g.astype(bf16), gw_ref[...], preferred_element_type=f32) + gb_ref[...]
        o_ref[...] = (g * jax.nn.sigmoid(z)).astype(o_ref.dtype)
    else:
        o_ref[...] = y


def _s5_tables(a_re, a_im, log_dt, b_re, b_im, c_re, c_im, steps):
    ng, ns = a_re.shape
    nb = ng // S5_GB
    lam = lax.complex(a_re, a_im)
    dt = jnp.exp(log_dt)[:, None]
    lam_bar = jnp.exp(lam * dt)
    b_bar = ((lam_bar - 1) / lam)[..., None] * lax.complex(b_re, b_im)
    cm = lax.complex(c_re, c_im)
    pw = jnp.exp((lam * dt)[None] * jnp.arange(1, steps + 1, dtype=f32)[:, None, None])
    eye = jnp.eye(S5_GB, dtype=f32)
    bb = b_bar.reshape(nb, S5_GB, ns, S5_GROUP)
    wb = jnp.concatenate([jnp.einsum('bgpc,gh->bgchp', part, eye).reshape(nb, S5_GB * S5_GROUP, S5_GB * ns)
                          for part in (jnp.real(bb), jnp.imag(bb))], axis=-1)
    cc = cm.reshape(nb, S5_GB, S5_GROUP, ns)
    wc = jnp.concatenate([jnp.einsum('bgcp,gh->bgphc', part, eye).reshape(nb, S5_GB * ns, S5_GB * S5_GROUP)
                          for part in (jnp.real(cc), -jnp.imag(cc))], axis=1)
    lb = lam_bar.reshape(nb, 1, S5_GB * ns)
    pwb = pw.reshape(steps, nb, S5_GB * ns).transpose(1, 0, 2)
    return (wb.astype(bf16), wc.astype(bf16), jnp.real(lb), jnp.imag(lb), jnp.real(pwb), jnp.imag(pwb))


def _s5_perm(tt, reverse):
    steps = tt // 8
    dst = jnp.arange(tt)
    t = (dst % 8) * steps + dst // 8
    src = tt - 1 - t if reverse else t
    return jnp.arange(tt)[None, :] == src[:, None]


def _s5(u, a_re, a_im, log_dt, b_re, b_im, c_re, c_im, s5_d, glu_w, glu_b, *, batch, seq, ctx_len):
    r, width = u.shape
    tt = S5_TT
    assert ctx_len == tt and seq % tt == 0
    n_lat = seq // tt
    ctx_blk0 = (batch * seq) // tt
    y = None
    for di in range(2):
        rev = di == 1
        wb, wc, lr, li, pr, pi = _s5_tables(a_re[di], a_im[di], log_dt[di], b_re[di], b_im[di],
                                            c_re[di], c_im[di], tt // 8)
        perm = _s5_perm(tt, rev)

        def tile_map(b, i, rev=rev):
            lat = b * n_lat + (n_lat - i if rev else i - 1)
            return (jnp.where(i == 0, ctx_blk0 + b, lat), 0)

        row_spec = pl.BlockSpec((tt, width), tile_map)
        const = lambda a: pl.BlockSpec(a.shape, lambda b, i: (0,) * a.ndim)
        ins = [u, perm.astype(bf16), perm.T.astype(f32), wb, wc, lr, li, pr, pi]
        in_specs = [row_spec] + [const(a) for a in ins[1:]]
        if rev:
            extra = [y, s5_d.reshape(1, width), glu_w.astype(bf16), glu_b.reshape(1, width)]
            in_specs += [row_spec] + [const(a) for a in extra[1:]]
            ins += extra
        y = pl.pallas_call(
            functools.partial(_s5_kernel, fuse_glu=rev),
            grid=(batch, n_lat + 1),
            in_specs=in_specs,
            out_specs=row_spec,
            out_shape=jax.ShapeDtypeStruct((r, width), bf16 if rev else f32),
            scratch_shapes=[pltpu.VMEM((wb.shape[0], 1, wb.shape[2]), f32),
                            pltpu.VMEM((tt, wb.shape[2]), f32),
                            pltpu.VMEM((tt, width), f32)],
            compiler_params=_cparams(("arbitrary", "arbitrary")),
            name="s5_bwd_glu" if rev else "s5_fwd",
        )(*ins)
    return y


def _rope_tables(seq):
    rows = seq // GRID_W
    row = jnp.repeat(jnp.arange(rows, dtype=f32), GRID_W)
    col = jnp.tile(jnp.arange(GRID_W, dtype=f32), rows)
    n_freq = DIFF_QK // 4
    inv = ROPE_BASE ** (-jnp.arange(n_freq, dtype=f32) / n_freq)
    ar, ac = row[:, None] * inv, col[:, None] * inv
    cos = jnp.concatenate([jnp.cos(ar), jnp.cos(ar), jnp.cos(ac), jnp.cos(ac)], axis=-1)
    sin = jnp.concatenate([-jnp.sin(ar), jnp.sin(ar), -jnp.sin(ac), jnp.sin(ac)], axis=-1)
    cos = jnp.concatenate([jnp.tile(cos, (1, 2)), jnp.ones((TM, 128), f32)], axis=0)
    sin = jnp.concatenate([jnp.tile(sin, (1, 2)), jnp.zeros((TM, 128), f32)], axis=0)
    return cos, sin


def kernel(x, c, ctx, c_ctx, ada_w, ada_b, norm_mix_g, norm_ffn_g, norm_out_g, ab_w_in, ab_w_out, diff_lam, diff_subln_g, pool_w, pool_scale, cd_w_in, cd_w_out, hy_conv_w, hy_conv_b, hy_f_w1, hy_f_b1, hy_f_freq, hy_f_w2, hy_f_b2, hy_f_w3, hy_bias, s5_a_re, s5_a_im, s5_log_dt, s5_b_re, s5_b_im, s5_c_re, s5_c_im, s5_d, glu_w, glu_b, moe_w_rg, moe_b_rg, moe_w_re, moe_b_re, moe_w1, moe_w3, moe_w2):
    bsz, seq, d = x.shape
    cl = ctx.shape[1]
    depth = ada_w.shape[0]
    n_lat = bsz * seq
    assert seq % TM == 0 and (bsz * cl) % TM == 0 and bsz + 1 <= 8
    xs = jnp.concatenate([x.reshape(n_lat, d), ctx.reshape(bsz * cl, d)], axis=0)
    r = xs.shape[0]
    cond = jnp.zeros((8, d), f32).at[0:bsz].set(c).at[bsz].set(c_ctx)
    mods = _ada_mods(cond, ada_w, ada_b)
    geom = dict(batch=bsz, seq=seq)

    for l in range(depth):
        last = l == depth - 1
        i = l // 2
        n_tiles = (n_lat if last else r) // TM
        g_mix = norm_mix_g[l].reshape(1, d)
        if l % 2 == 0:
            lam_init = 0.8 - 0.6 * math.exp(-0.3 * l)
            cos, sin = _rope_tables(seq)
            q, k, vt, p = _proj_rope(xs, g_mix, mods[l], ab_w_in[i].astype(bf16), cos, sin, **geom)
            mix_a = _diff_attention(q, k, vt, diff_lam[i], diff_subln_g[i], lam_init, ctx_len=cl, **geom)
            mix_b = _pool(p, pool_w[i], pool_scale[i], ctx_len=cl, **geom)
            w_out = ab_w_out[i]
        else:
            hw = (HY_ORDER + 1) * HY_WIDTH
            hy, u = _proj(xs, g_mix, mods[l], cd_w_in[i].astype(bf16), hw, **geom)
            mix_a = _hyena(hy, hy_conv_w[i], hy_conv_b[i], hy_f_w1[i], hy_f_b1[i], hy_f_freq[i],
                           hy_f_w2[i], hy_f_b2[i], hy_f_w3[i], hy_bias[i], **geom)
            mix_b = _s5(u, s5_a_re[i], s5_a_im[i], s5_log_dt[i], s5_b_re[i], s5_b_im[i],
                        s5_c_re[i], s5_c_im[i], s5_d[i], glu_w[i], glu_b[i], ctx_len=cl, **geom)
            w_out = cd_w_out[i]
        w_r = jnp.zeros((d, 128), f32).at[:, :MOE_GROUPS].set(moe_w_rg[l]) \
            .at[:, MOE_GROUPS:MOE_GROUPS + MOE_EXPERTS].set(moe_w_re[l]).astype(bf16)
        b_r = jnp.zeros((1, 128), f32).at[0, :MOE_GROUPS].set(moe_b_rg[l]) \
            .at[0, MOE_GROUPS:MOE_GROUPS + MOE_EXPERTS].set(moe_b_re[l])
        xn, h2, logits = _outproj(mix_a, mix_b, xs, w_out.astype(bf16), mods[l], norm_ffn_g[l].reshape(1, d),
                                  w_r, b_r, n_tiles=n_tiles, **geom)
        xs = _moe(xn, h2, logits, mods[l], moe_w1, moe_w3, moe_w2, l, norm_out_g.reshape(1, d),
                  n_tiles=n_tiles, final_norm=last, **geom)
    return xs[:n_lat].reshape(bsz, seq, d)
```

```python
import functools
import math

import jax
import jax.numpy as jnp
from jax import lax
from jax.experimental import pallas as pl
from jax.experimental.pallas import tpu as pltpu

f32, bf16, i32 = jnp.float32, jnp.bfloat16, jnp.int32

EPS = 1e-6
GRID_W = 64
DIFF_QK = 64
DIFF_V = 128
ROPE_BASE = 10000.0
POOL_WINDOWS = (2, 4, 8, 16)
POOL_GROUP = 256
HY_WIDTH = 1024
HY_ORDER = 2
HY_EMB = 33
HY_FAST_DECAY = 0.3
HY_SLOW_DECAY = 1.5
HY_TARGET = 1e-2
S5_GROUP = 16
S5_GROUPS = 64
S5_STATE = 64
MOE_GROUPS = 8
MOE_PER_GROUP = 8
MOE_EXPERTS = 64
MOE_TOPK = 2
MOE_BLOCK = 256

VMEM_LIMIT = 56 * 1024 * 1024
TM = 512
LOG2E = 1.4426950408889634


def _cparams(sem):
    return pltpu.CompilerParams(dimension_semantics=sem, vmem_limit_bytes=VMEM_LIMIT)


def _norm_mod(x, g, shift, scale):
    ms = jnp.mean(x * x, axis=-1, keepdims=True)
    return (x * lax.rsqrt(ms + EPS) * g) * (1.0 + scale) + shift


def _tile_segment(i, lat_tiles_per_batch, batch):
    seg = jnp.zeros((), i32)
    for b in range(1, batch + 1):
        seg = seg + (i >= b * lat_tiles_per_batch).astype(i32)
    return seg


def _ada_kernel(s_ref, w_ref, b_ref, o_ref):
    s = s_ref[...]
    a = (s * jax.nn.sigmoid(s)).astype(bf16)
    o_ref[...] = jnp.dot(a, w_ref[...].astype(bf16), preferred_element_type=f32) + b_ref[...]


def _ada_mods(s, ada_w, ada_b):
    depth, d, n = ada_w.shape
    tn = 1024
    return pl.pallas_call(
        _ada_kernel,
        grid=(depth, n // tn),
        in_specs=[pl.BlockSpec((8, d), lambda l, j: (0, 0)),
                  pl.BlockSpec((None, d, tn), lambda l, j: (l, 0, j)),
                  pl.BlockSpec((None, 1, tn), lambda l, j: (l, 0, j))],
        out_specs=pl.BlockSpec((None, 8, tn), lambda l, j: (l, 0, j)),
        out_shape=jax.ShapeDtypeStruct((depth, 8, n), f32),
        compiler_params=_cparams(("arbitrary", "arbitrary")),
        name="ada_mods",
    )(s, ada_w, ada_b.reshape(depth, 1, n))


def _proj_rope_kernel(x_ref, g_ref, sh_ref, sc_ref, w_ref, cos_ref, sin_ref,
                      q_ref, k_ref, v_ref, p_ref, h_scr, *, tiles_per_batch, batch, qscale):
    seg = _tile_segment(pl.program_id(0), tiles_per_batch, batch)
    h = _norm_mod(x_ref[...], g_ref[...], sh_ref[pl.ds(seg, 1), :], sc_ref[pl.ds(seg, 1), :])
    h_scr[...] = h.astype(bf16)
    cos = cos_ref[...]
    sin = sin_ref[...]
    tm = cos.shape[0]
    lane = lax.broadcasted_iota(i32, (tm, 128), 1)
    first = (lane % 32) < 16

    def rope(r):
        sw = jnp.where(first, pltpu.roll(r, 112, 1), pltpu.roll(r, 16, 1))
        return r * cos + sw * sin

    heads = q_ref.shape[0]
    nc = 512
    for c in range(w_ref.shape[1] // nc):
        res = jnp.dot(h_scr[...], w_ref[:, c * nc:(c + 1) * nc], preferred_element_type=f32)
        for s in range(nc // 128):
            r = res[:, s * 128:(s + 1) * 128]
            col = (c * nc) // 128 + s
            if col < heads:
                q_ref[col] = (rope(r) * qscale).astype(bf16)
            elif col < 2 * heads:
                k_ref[col - heads] = rope(r).astype(bf16)
            elif col < 3 * heads:
                v_ref[col - 2 * heads] = r.T.astype(bf16)
            else:
                c0 = (col - 3 * heads) * 128
                p_ref[:, c0:c0 + 128] = r


def _proj_rope(x, g, mods, w, cos, sin, *, batch, seq):
    r, d = x.shape
    heads = 8
    lat_tiles = seq // TM
    n_tiles = r // TM

    def tab_map(i):
        return (jnp.where(i < batch * lat_tiles, i % lat_tiles, lat_tiles), 0)

    hm = jax.ShapeDtypeStruct((heads, r, 128), bf16)
    hm_spec = pl.BlockSpec((heads, TM, 128), lambda i: (0, i, 0))
    return pl.pallas_call(
        functools.partial(_proj_rope_kernel, tiles_per_batch=lat_tiles, batch=batch,
                          qscale=LOG2E * DIFF_QK ** -0.5),
        grid=(n_tiles,),
        in_specs=[pl.BlockSpec((TM, d), lambda i: (i, 0)),
                  pl.BlockSpec((1, d), lambda i: (0, 0)),
                  pl.BlockSpec((8, d), lambda i: (0, 0)),
                  pl.BlockSpec((8, d), lambda i: (0, 1)),
                  pl.BlockSpec(w.shape, lambda i: (0, 0), pipeline_mode=pl.Buffered(1)),
                  pl.BlockSpec((TM, 128), tab_map),
                  pl.BlockSpec((TM, 128), tab_map)],
        out_specs=[hm_spec, hm_spec, pl.BlockSpec((heads, 128, TM), lambda i: (0, 0, i)),
                   pl.BlockSpec((TM, 1024), lambda i: (i, 0))],
        out_shape=[hm, hm, jax.ShapeDtypeStruct((heads, 128, r), bf16), jax.ShapeDtypeStruct((r, 1024), f32)],
        scratch_shapes=[pltpu.VMEM((TM, d), bf16)],
        compiler_params=_cparams(("arbitrary",)),
        name="proj_rope",
    )(x, g, mods, mods, w, cos, sin)


def _proj_kernel(x_ref, g_ref, sh_ref, sc_ref, w_ref, hy_ref, u_ref, h_scr, *, tiles_per_batch, batch):
    seg = _tile_segment(pl.program_id(0), tiles_per_batch, batch)
    h = _norm_mod(x_ref[...], g_ref[...], sh_ref[pl.ds(seg, 1), :], sc_ref[pl.ds(seg, 1), :])
    h_scr[...] = h.astype(bf16)
    nc = 512
    n_hy = hy_ref.shape[1]
    for c in range(w_ref.shape[1] // nc):
        res = jnp.dot(h_scr[...], w_ref[:, c * nc:(c + 1) * nc], preferred_element_type=f32)
        if c * nc < n_hy:
            hy_ref[:, c * nc:(c + 1) * nc] = res
        else:
            u_ref[:, c * nc - n_hy:(c + 1) * nc - n_hy] = res


def _proj(x, g, mods, w, n_hy, *, batch, seq):
    r, d = x.shape
    n = w.shape[1]
    return pl.pallas_call(
        functools.partial(_proj_kernel, tiles_per_batch=seq // TM, batch=batch),
        grid=(r // TM,),
        in_specs=[pl.BlockSpec((TM, d), lambda i: (i, 0)),
                  pl.BlockSpec((1, d), lambda i: (0, 0)),
                  pl.BlockSpec((8, d), lambda i: (0, 0)),
                  pl.BlockSpec((8, d), lambda i: (0, 1)),
                  pl.BlockSpec(w.shape, lambda i: (0, 0), pipeline_mode=pl.Buffered(1))],
        out_specs=[pl.BlockSpec((TM, n_hy), lambda i: (i, 0)), pl.BlockSpec((TM, n - n_hy), lambda i: (i, 0))],
        out_shape=[jax.ShapeDtypeStruct((r, n_hy), f32), jax.ShapeDtypeStruct((r, n - n_hy), f32)],
        scratch_shapes=[pltpu.VMEM((TM, d), bf16)],
        compiler_params=_cparams(("arbitrary",)),
        name="proj",
    )(x, g, mods, mods, w)


def _attn_kernel(lv_ref, g_ref, q_ref, kc_ref, vc_ref, *rest, n_lat, tk, lam_init):
    if n_lat:
        kl_ref, vl_ref, o_ref, s_scr = rest
    else:
        o_ref, s_scr = rest
    lv = lv_ref[...]
    lam = (jnp.exp(jnp.sum(lv[0:1] * lv[1:2], axis=-1, keepdims=True))
           - jnp.exp(jnp.sum(lv[2:3] * lv[3:4], axis=-1, keepdims=True)) + lam_init)
    q = q_ref[...]
    tq = q.shape[0]
    lane = lax.broadcasted_iota(i32, q.shape, 1)
    zero = jnp.zeros_like(q)
    qs = (jnp.where(lane < DIFF_QK, q, zero), jnp.where(lane >= DIFF_QK, q, zero))
    nt = (((1,), (1,)), ((), ()))

    def scores(slot, k):
        for m in range(2):
            s_scr[slot, m, 0:k.shape[0], :] = lax.dot_general(k, qs[m], nt, preferred_element_type=f32)

    def update(carry, slot, nk, vt):
        new = []
        for m in range(2):
            mx, l, acc = carry[3 * m:3 * m + 3]
            s = s_scr[slot, m, 0:nk, :]
            mn = jnp.maximum(mx, jnp.max(s, axis=0, keepdims=True))
            alpha = jnp.exp2(mx - mn)
            p = jnp.exp2(s - mn)
            l = alpha * l + jnp.sum(p, axis=0, keepdims=True)
            acc = alpha * acc + jnp.dot(vt, p.astype(bf16), preferred_element_type=f32)
            new += [mn, l, acc]
        return tuple(new)

    def k_block(j):
        return kl_ref[pl.ds(pl.multiple_of(j * tk, tk), tk), :]

    def vt_block(j):
        return vl_ref[:, pl.ds(pl.multiple_of(j * tk, tk), tk)]

    init = (jnp.full((1, tq), -jnp.inf, f32), jnp.zeros((1, tq), f32), jnp.zeros((DIFF_V, tq), f32)) * 2
    nc = kc_ref.shape[0]
    scores(1, kc_ref[...])
    if n_lat:
        scores(0, k_block(0))
        carry = update(init, 1, nc, vc_ref[...])

        def body(jj, carry):
            j = 2 * jj
            scores(1, k_block(j + 1))
            carry = update(carry, 0, tk, vt_block(j))
            scores(0, k_block(jnp.minimum(j + 2, n_lat - 1)))
            return update(carry, 1, tk, vt_block(j + 1))

        carry = lax.fori_loop(0, n_lat // 2, body, carry)
    else:
        carry = update(init, 1, nc, vc_ref[...])
    _, l0, a0, _, l1, a1 = carry
    o = a0 / l0 - lam * (a1 / l1)
    ms = jnp.mean(o * o, axis=0, keepdims=True)
    y = (o * lax.rsqrt(ms + EPS) * g_ref[...]) * (1.0 - lam_init)
    o_ref[...] = y.T.astype(o_ref.dtype)


def _diff_attention(q, k, vt, lam_vecs, subln_g, lam_init, *, batch, seq, ctx_len):
    heads, r, _ = q.shape
    tq, tk = 512, 512
    ctx_blk0 = (batch * seq) // ctx_len
    assert (seq // tk) % 2 == 0
    small = [pl.BlockSpec((4, DIFF_QK), lambda *_: (0, 0)), pl.BlockSpec((DIFF_V, 1), lambda *_: (0, 0))]
    kc_spec = pl.BlockSpec((None, ctx_len, 128), lambda b, h, *_: (h, ctx_blk0 + b, 0))
    vc_spec = pl.BlockSpec((None, 128, ctx_len), lambda b, h, *_: (h, 0, ctx_blk0 + b))
    g2 = subln_g.reshape(DIFF_V, 1)
    lat = pl.pallas_call(
        functools.partial(_attn_kernel, n_lat=seq // tk, tk=tk, lam_init=lam_init),
        grid=(batch, heads, seq // tq),
        in_specs=small + [pl.BlockSpec((None, tq, 128), lambda b, h, i: (h, b * (seq // tq) + i, 0)),
                          kc_spec, vc_spec,
                          pl.BlockSpec((None, seq, 128), lambda b, h, i: (h, b, 0)),
                          pl.BlockSpec((None, 128, seq), lambda b, h, i: (h, 0, b))],
        out_specs=pl.BlockSpec((tq, 128), lambda b, h, i: (b * (seq // tq) + i, h)),
        out_shape=jax.ShapeDtypeStruct((batch * seq, heads * DIFF_V), bf16),
        scratch_shapes=[pltpu.VMEM((2, 2, tk, tq), f32)],
        compiler_params=_cparams(("arbitrary",) * 3),
        name="diff_attn_lat",
    )(lam_vecs, g2, q, k, vt, k, vt)
    ctx = pl.pallas_call(
        functools.partial(_attn_kernel, n_lat=0, tk=tk, lam_init=lam_init),
        grid=(batch, heads),
        in_specs=small + [kc_spec, kc_spec, vc_spec],
        out_specs=pl.BlockSpec((ctx_len, 128), lambda b, h: (b, h)),
        out_shape=jax.ShapeDtypeStruct((batch * ctx_len, heads * DIFF_V), bf16),
        scratch_shapes=[pltpu.VMEM((2, 2, ctx_len, ctx_len), f32)],
        compiler_params=_cparams(("arbitrary",) * 2),
        name="diff_attn_ctx",
    )(lam_vecs, g2, q, k, vt)
    return jnp.concatenate([lat, ctx], axis=0)


def _pool_kernel(p_ref, prev_ref, next_ref, w_ref, scale_ref, o_ref, ext_scr, *,
                 tiles_per_batch, batch, seq, ctx_len):
    i = pl.program_id(0)
    tp = p_ref.shape[0]
    is_ctx = i >= batch * tiles_per_batch
    j = jnp.where(is_ctx, 0, i % tiles_per_batch)
    seg_len = jnp.where(is_ctx, ctx_len, seq)
    last = jnp.where(is_ctx, 0, tiles_per_batch - 1)
    ext_scr[0:8, :] = jnp.where(j > 0, prev_ref[...], 0.0)
    ext_scr[8:8 + tp, :] = p_ref[...]
    ext_scr[8 + tp:16 + tp, :] = jnp.where(j < last, next_ref[...], 0.0)
    pos = j * tp + lax.broadcasted_iota(i32, (tp, 1), 0)
    for gi, w in enumerate(POOL_WINDOWS):
        c0 = gi * POOL_GROUP
        half = w // 2
        tot = ext_scr[8 - half:8 - half + tp, c0:c0 + POOL_GROUP]
        for dlt in range(1 - half, half):
            tot = tot + ext_scr[8 + dlt:8 + dlt + tp, c0:c0 + POOL_GROUP]
        cnt = (jnp.minimum(pos - half + w, seg_len) - jnp.maximum(pos - half, 0)).astype(f32)
        dlt_mean = (tot / cnt - ext_scr[8:8 + tp, c0:c0 + POOL_GROUP]).astype(bf16)
        y = jnp.dot(dlt_mean, w_ref[gi].astype(bf16), preferred_element_type=f32)
        o_ref[:, c0:c0 + POOL_GROUP] = (y * scale_ref[:, c0:c0 + POOL_GROUP]).astype(o_ref.dtype)


def _pool(p, pool_w, pool_scale, *, batch, seq, ctx_len):
    r, width = p.shape
    tp = ctx_len
    tpb = seq // tp
    return pl.pallas_call(
        functools.partial(_pool_kernel, tiles_per_batch=tpb, batch=batch, seq=seq, ctx_len=ctx_len),
        grid=(r // tp,),
        in_specs=[pl.BlockSpec((tp, width), lambda i: (i, 0)),
                  pl.BlockSpec((8, width), lambda i: (jnp.maximum(i * (tp // 8) - 1, 0), 0)),
                  pl.BlockSpec((8, width), lambda i: (jnp.minimum((i + 1) * (tp // 8), r // 8 - 1), 0)),
                  pl.BlockSpec(pool_w.shape, lambda i: (0, 0, 0)),
                  pl.BlockSpec((1, width), lambda i: (0, 0))],
        out_specs=pl.BlockSpec((tp, width), lambda i: (i, 0)),
        out_shape=jax.ShapeDtypeStruct((r, width), bf16),
        scratch_shapes=[pltpu.VMEM((tp + 16, width), f32)],
        compiler_params=_cparams(("arbitrary",)),
        name="pool",
    )(p, p, p, pool_w, pool_scale.reshape(1, width))


def _outproj_kernel(a_ref, b_ref, x_ref, w_ref, g1_ref, sh_ref, sc_ref, gn_ref, wr_ref, br_ref,
                    xn_ref, h2_ref, lg_ref, *, tiles_per_batch, batch):
    seg = _tile_segment(pl.program_id(0), tiles_per_batch, batch)
    half = a_ref.shape[1]
    y = (jnp.dot(a_ref[...], w_ref[0:half, :], preferred_element_type=f32)
         + jnp.dot(b_ref[...], w_ref[half:, :], preferred_element_type=f32))
    xn = x_ref[...] + g1_ref[pl.ds(seg, 1), :] * y
    xn_ref[...] = xn
    h2 = _norm_mod(xn, gn_ref[...], sh_ref[pl.ds(seg, 1), :], sc_ref[pl.ds(seg, 1), :])
    h2_ref[...] = h2
    lg_ref[...] = jnp.dot(h2.astype(bf16), wr_ref[...], preferred_element_type=f32) + br_ref[...]


def _outproj(a, b, x, w, mods, g_ffn, w_r, b_r, *, n_tiles, batch, seq):
    r, d = n_tiles * TM, x.shape[1]
    half = a.shape[1]
    row = lambda i: (i, 0)
    return pl.pallas_call(
        functools.partial(_outproj_kernel, tiles_per_batch=seq // TM, batch=batch),
        grid=(n_tiles,),
        in_specs=[pl.BlockSpec((TM, half), row), pl.BlockSpec((TM, half), row), pl.BlockSpec((TM, d), row),
                  pl.BlockSpec(w.shape, lambda i: (0, 0), pipeline_mode=pl.Buffered(1)),
                  pl.BlockSpec((8, d), lambda i: (0, 2)),
                  pl.BlockSpec((8, d), lambda i: (0, 3)),
                  pl.BlockSpec((8, d), lambda i: (0, 4)),
                  pl.BlockSpec((1, d), lambda i: (0, 0)),
                  pl.BlockSpec(w_r.shape, lambda i: (0, 0)),
                  pl.BlockSpec((1, 128), lambda i: (0, 0))],
        out_specs=[pl.BlockSpec((TM, d), row), pl.BlockSpec((TM, d), row), pl.BlockSpec((TM, 128), row)],
        out_shape=[jax.ShapeDtypeStruct((r, d), f32), jax.ShapeDtypeStruct((r, d), f32),
                   jax.ShapeDtypeStruct((r, 128), f32)],
        compiler_params=_cparams(("arbitrary",)),
        name="outproj",
    )(a, b, x, w, mods, mods, mods, g_ffn, w_r, b_r)


def _moe_kernel(be_ref, nu_ref, nxt_ref, xs_ref, w1_hbm, w3_hbm, w2_hbm, ys_ref,
                w1_f32, w3_f32, w2_f32, w1_scr, w3_scr, w2_scr, sems, *, layer):
    i = pl.program_id(0)
    e = be_ref[i]
    prev = be_ref[jnp.maximum(i - 1, 0)]
    used = i < nu_ref[0]

    def weight_copies(expert):
        return (pltpu.make_async_copy(w1_hbm.at[layer, expert], w1_f32, sems.at[0]),
                pltpu.make_async_copy(w3_hbm.at[layer, expert], w3_f32, sems.at[1]),
                pltpu.make_async_copy(w2_hbm.at[layer, expert], w2_f32, sems.at[2]))

    @pl.when(i == 0)
    def _():
        for cp in weight_copies(e):
            cp.start()

    @pl.when(used & ((i == 0) | (e != prev)))
    def _():
        for cp in weight_copies(e):
            cp.wait()
        w1_scr[...] = w1_f32[...].astype(bf16)
        w3_scr[...] = w3_f32[...].astype(bf16)
        w2_scr[...] = w2_f32[...].astype(bf16)
        nxt = nxt_ref[e]

        @pl.when(nxt >= 0)
        def _():
            for cp in weight_copies(nxt):
                cp.start()

    @pl.when(used)
    def _():
        xb = xs_ref[...].astype(bf16)
        a = jnp.dot(xb, w1_scr[...], preferred_element_type=f32)
        b = jnp.dot(xb, w3_scr[...], preferred_element_type=f32)
        hid = (a * jax.nn.sigmoid(a) * b).astype(bf16)
        ys_ref[...] = jnp.dot(hid, w2_scr[...], preferred_element_type=f32)

    @pl.when(jnp.logical_not(used))
    def _():
        ys_ref[...] = jnp.zeros_like(ys_ref)


def _moe_experts(xs, block_e, n_used, next_expert, w1, w3, w2, layer, nb):
    m = MOE_BLOCK
    _, _, d, de = w1.shape
    hbm = pl.BlockSpec(memory_space=pl.ANY)
    grid_spec = pltpu.PrefetchScalarGridSpec(
        num_scalar_prefetch=3,
        grid=(nb,),
        in_specs=[pl.BlockSpec((m, d), lambda i, be, nu, nx: (jnp.minimum(i, nu[0] - 1), 0)), hbm, hbm, hbm],
        out_specs=pl.BlockSpec((m, d), lambda i, be, nu, nx: (i, 0)),
        scratch_shapes=[pltpu.VMEM((d, de), f32), pltpu.VMEM((d, de), f32), pltpu.VMEM((de, d), f32),
                        pltpu.VMEM((d, de), bf16), pltpu.VMEM((d, de), bf16), pltpu.VMEM((de, d), bf16),
                        pltpu.SemaphoreType.DMA((3,))],
    )
    return pl.pallas_call(
        functools.partial(_moe_kernel, layer=layer),
        grid_spec=grid_spec,
        out_shape=jax.ShapeDtypeStruct((nb * m, d), f32),
        compiler_params=_cparams(("arbitrary",)),
        name="moe_experts",
    )(block_e, n_used, next_expert, xs, w1, w3, w2)


def _route_kernel(lg_ref, tri_ref, o_ref, cnt_ref, run_scr):
    @pl.when(pl.program_id(0) == 0)
    def _():
        run_scr[...] = jnp.zeros_like(run_scr)

    lg = lg_ref[...]
    lane = lax.broadcasted_iota(i32, lg.shape, 1)
    ninf = -jnp.inf

    def first_lane(mask):
        return jnp.min(jnp.where(mask, lane, lg.shape[1]), axis=-1, keepdims=True)

    gl = jnp.where(lane < MOE_GROUPS, lg, ninf)
    gmax = jnp.max(gl, axis=-1, keepdims=True)
    pg_top = 1.0 / jnp.sum(jnp.exp(gl - gmax), axis=-1, keepdims=True)
    lo = MOE_GROUPS + first_lane(gl == gmax) * MOE_PER_GROUP
    sel = (lane >= lo) & (lane < lo + MOE_PER_GROUP)
    el = jnp.where(sel, lg, ninf)
    ee = jnp.exp(el - jnp.max(el, axis=-1, keepdims=True))
    pe = jnp.where(sel, ee / jnp.sum(ee, axis=-1, keepdims=True), -1.0)
    p1 = jnp.max(pe, axis=-1, keepdims=True)
    i1 = first_lane(pe == p1)
    pe2 = jnp.where(lane == i1, -1.0, pe)
    p2 = jnp.max(pe2, axis=-1, keepdims=True)
    i2 = first_lane(pe2 == p2)
    den = p1 + p2
    e1 = i1 - MOE_GROUPS
    e2 = i2 - MOE_GROUPS
    hit1 = lane == e1
    hit2 = lane == e2
    onehot = jnp.where(hit1, 1.0, jnp.where(hit2, 1.0, 0.0))
    base = jnp.dot(tri_ref[...], onehot.astype(bf16), preferred_element_type=f32) + run_scr[...]
    r1 = jnp.sum(jnp.where(hit1, base, 0.0), axis=-1, keepdims=True)
    r2 = jnp.sum(jnp.where(hit2, base, 0.0), axis=-1, keepdims=True)
    run_scr[...] = run_scr[...] + jnp.sum(onehot, axis=0, keepdims=True)
    cnt_ref[...] = run_scr[...]
    vals = (e1.astype(f32), e2.astype(f32), pg_top * p1 / den, pg_top * p2 / den, r1, r2)
    out = jnp.zeros(lg.shape, f32)
    for j, v in enumerate(vals):
        out = jnp.where(lane == j, v, out)
    o_ref[...] = out


def _route(logits):
    t, width = logits.shape
    tri = jnp.tril(jnp.ones((TM, TM), f32), -1).astype(bf16)
    return pl.pallas_call(
        _route_kernel,
        grid=(t // TM,),
        in_specs=[pl.BlockSpec((TM, width), lambda i: (i, 0)), pl.BlockSpec((TM, TM), lambda i: (0, 0))],
        out_specs=[pl.BlockSpec((TM, width), lambda i: (i, 0)), pl.BlockSpec((1, width), lambda i: (0, 0))],
        out_shape=[jax.ShapeDtypeStruct((t, width), f32), jax.ShapeDtypeStruct((1, width), f32)],
        scratch_shapes=[pltpu.VMEM((1, width), f32)],
        compiler_params=_cparams(("arbitrary",)),
        name="route",
    )(logits, tri)


def _dispatch_kernel(dest_ref, h_ref, init_ref, xs_ref, sem):
    del init_ref
    i = pl.program_id(0)
    tm = h_ref.shape[0]

    def row_copy(r, k):
        slot = dest_ref[(i * tm + r) * MOE_TOPK + k]
        return pltpu.make_async_copy(h_ref.at[pl.ds(r, 1)], xs_ref.at[pl.ds(slot, 1)], sem)

    def issue(r, carry):
        for k in range(MOE_TOPK):
            row_copy(r, k).start()
        return carry

    lax.fori_loop(0, tm, issue, 0, unroll=8)
    for k in range(MOE_TOPK):
        pltpu.make_async_copy(h_ref, xs_ref.at[pl.ds(0, tm)], sem).wait()


def _dispatch(h2, dest, nb):
    t, d = h2.shape
    rows = nb * MOE_BLOCK
    grid_spec = pltpu.PrefetchScalarGridSpec(
        num_scalar_prefetch=1,
        grid=(t // TM,),
        in_specs=[pl.BlockSpec((TM, d), lambda i, dst: (i, 0)), pl.BlockSpec(memory_space=pl.ANY)],
        out_specs=pl.BlockSpec(memory_space=pl.ANY),
        scratch_shapes=[pltpu.SemaphoreType.DMA(())],
    )
    return pl.pallas_call(
        _dispatch_kernel,
        grid_spec=grid_spec,
        out_shape=jax.ShapeDtypeStruct((rows, d), h2.dtype),
        input_output_aliases={2: 0},
        compiler_params=_cparams(("arbitrary",)),
        name="moe_dispatch",
    )(dest, h2, jnp.zeros((rows, d), h2.dtype))


def _combine_kernel(dest_ref, x_ref, rt_ref, g2_ref, gn_ref, ys_ref, o_ref, y_scr, sem, *,
                    tiles_per_batch, batch, final_norm):
    i = pl.program_id(0)
    tm = x_ref.shape[0]

    def row_copy(r, k):
        slot = dest_ref[(i * tm + r) * MOE_TOPK + k]
        return pltpu.make_async_copy(ys_ref.at[pl.ds(slot, 1)], y_scr.at[k, pl.ds(r, 1)], sem)

    def issue(r, carry):
        for k in range(MOE_TOPK):
            row_copy(r, k).start()
        return carry

    lax.fori_loop(0, tm, issue, 0, unroll=8)
    for k in range(MOE_TOPK):
        pltpu.make_async_copy(ys_ref.at[pl.ds(0, tm)], y_scr.at[k], sem).wait()
    seg = _tile_segment(i, tiles_per_batch, batch)
    rt = rt_ref[...]
    y = rt[:, MOE_TOPK:MOE_TOPK + 1] * y_scr[0]
    for k in range(1, MOE_TOPK):
        y = y + rt[:, MOE_TOPK + k:MOE_TOPK + k + 1] * y_scr[k]
    x = x_ref[...] + g2_ref[pl.ds(seg, 1), :] * y
    if final_norm:
        ms = jnp.mean(x * x, axis=-1, keepdims=True)
        x = x * lax.rsqrt(ms + EPS) * gn_ref[...]
    o_ref[...] = x


def _combine(x, ys, dest, routed, mods, g_out, *, n_tiles, batch, seq, final_norm):
    d = x.shape[1]
    row = lambda i, dst: (i, 0)
    grid_spec = pltpu.PrefetchScalarGridSpec(
        num_scalar_prefetch=1,
        grid=(n_tiles,),
        in_specs=[pl.BlockSpec((TM, d), row),
                  pl.BlockSpec((TM, routed.shape[1]), row),
                  pl.BlockSpec((8, d), lambda i, dst: (0, 5)),
                  pl.BlockSpec((1, d), lambda i, dst: (0, 0)),
                  pl.BlockSpec(memory_space=pl.ANY)],
        out_specs=pl.BlockSpec((TM, d), row),
        scratch_shapes=[pltpu.VMEM((MOE_TOPK, TM, d), f32), pltpu.SemaphoreType.DMA(())],
    )
    return pl.pallas_call(
        functools.partial(_combine_kernel, tiles_per_batch=seq // TM, batch=batch, final_norm=final_norm),
        grid_spec=grid_spec,
        out_shape=jax.ShapeDtypeStruct((n_tiles * TM, d), f32),
        compiler_params=_cparams(("arbitrary",)),
        name="moe_combine",
    )(dest, x, routed, mods, g_out, ys)


def _moe(xn, h2, logits, mods, w1, w3, w2, layer, g_out, *, n_tiles, batch, seq, final_norm):
    t = n_tiles * TM
    m = MOE_BLOCK
    routed, cnt = _route(logits)
    s = t * MOE_TOPK
    nb = s // m + MOE_EXPERTS
    expert = routed[:, 0:MOE_TOPK].astype(i32)
    rank = routed[:, 2 * MOE_TOPK:3 * MOE_TOPK].astype(i32)
    counts = cnt[0, :MOE_EXPERTS].astype(i32)
    padded = ((counts + m - 1) // m) * m
    pad_end = jnp.cumsum(padded)
    onehot = expert[:, :, None] == jnp.arange(MOE_EXPERTS, dtype=i32)[None, None, :]
    dest = (jnp.sum(jnp.where(onehot, (pad_end - padded)[None, None, :], 0), axis=-1) + rank).reshape(s)
    n_used = (pad_end[-1] // m).astype(i32)
    blk = jnp.arange(nb, dtype=i32)
    block_e = jnp.minimum(jnp.searchsorted(pad_end, jnp.minimum(blk, n_used - 1) * m, side='right'),
                          MOE_EXPERTS - 1).astype(i32)
    ids = jnp.arange(MOE_EXPERTS, dtype=i32)
    nonempty_from = lax.cummin(jnp.where(counts > 0, ids, MOE_EXPERTS), reverse=True)
    nxt = jnp.concatenate([nonempty_from[1:], jnp.full((1,), MOE_EXPERTS, i32)])
    next_expert = jnp.where(nxt < MOE_EXPERTS, nxt, -1).astype(i32)
    xs = _dispatch(h2, dest, nb)
    ys = _moe_experts(xs, block_e, n_used.reshape(1), next_expert, w1, w3, w2, layer, nb)
    return _combine(xn, ys, dest, routed, mods, g_out, n_tiles=n_tiles, batch=batch, seq=seq,
                    final_norm=final_norm)


HY_N2 = 128
HY_WB = 256
HY_SPEC_WB = 128


def _short_conv_kernel(x_ref, prev_ref, next_ref, w_ref, b_ref, v_ref, x1_ref, x2_ref, ext_scr, *, tiles_per_batch):
    j = pl.program_id(0) % tiles_per_batch
    tt = x_ref.shape[0]
    ext_scr[0:8, :] = jnp.where(j > 0, prev_ref[...], 0.0)
    ext_scr[8:8 + tt, :] = x_ref[...]
    ext_scr[8 + tt:16 + tt, :] = jnp.where(j < tiles_per_batch - 1, next_ref[...], 0.0)
    w = w_ref[...]
    width = v_ref.shape[1]
    for part, o_ref in enumerate((v_ref, x1_ref, x2_ref)):
        cols = slice(part * width, (part + 1) * width)
        o_ref[...] = (ext_scr[7:7 + tt, cols] * w[0:1, cols] + ext_scr[8:8 + tt, cols] * w[1:2, cols]
                      + ext_scr[9:9 + tt, cols] * w[2:3, cols] + b_ref[:, cols])


def _short_conv(hy, w, b, *, batch, seq):
    n_lat = batch * seq
    tt = 256
    width = hy.shape[1]
    out = jax.ShapeDtypeStruct((n_lat, width // 3), f32)
    o_spec = pl.BlockSpec((tt, width // 3), lambda i: (i, 0))
    return pl.pallas_call(
        functools.partial(_short_conv_kernel, tiles_per_batch=seq // tt),
        grid=(n_lat // tt,),
        in_specs=[pl.BlockSpec((tt, width), lambda i: (i, 0)),
                  pl.BlockSpec((8, width), lambda i: (jnp.maximum(i * (tt // 8) - 1, 0), 0)),
                  pl.BlockSpec((8, width), lambda i: ((i + 1) * (tt // 8), 0)),
                  pl.BlockSpec((3, width), lambda i: (0, 0)),
                  pl.BlockSpec((1, width), lambda i: (0, 0))],
        out_specs=[o_spec, o_spec, o_spec],
        out_shape=[out, out, out],
        scratch_shapes=[pltpu.VMEM((tt + 16, width), f32)],
        compiler_params=_cparams(("arbitrary",)),
        name="short_conv",
    )(hy, hy, hy, w, b.reshape(1, width))


def _hy_filter_kernel(z_ref, w1_ref, b1_ref, fr_ref, w2_ref, b2_ref, w3_ref, dl_ref, k_ref, nrm_ref):
    i = pl.program_id(0)
    z = z_ref[...]
    fr = fr_ref[...]
    h = jnp.sin(fr[0:1] * (jnp.dot(z.astype(bf16), w1_ref[...], preferred_element_type=f32) + b1_ref[...]))
    h = jnp.sin(fr[1:2] * (jnp.dot(h.astype(bf16), w2_ref[...], preferred_element_type=f32) + b2_ref[...]))
    kk = jnp.dot(h.astype(bf16), w3_ref[...], preferred_element_type=f32)
    width = dl_ref.shape[1]
    decay = jnp.exp(-z[:, 0:1] * jnp.abs(dl_ref[...]))
    tl = z.shape[0]
    row = i * tl + lax.broadcasted_iota(i32, (tl, 1), 0)
    tot = []
    for blk in range(kk.shape[1] // width):
        kb = kk[:, blk * width:(blk + 1) * width] * decay
        if blk % 2 == 1:
            kb = jnp.where(row == 0, 0.0, kb)
        k_ref[:, blk * width:(blk + 1) * width] = kb
        tot.append(jnp.sum(jnp.abs(kb), axis=0, keepdims=True))
    s = jnp.concatenate(tot, axis=1)

    @pl.when(i == 0)
    def _():
        nrm_ref[...] = s

    @pl.when(i > 0)
    def _():
        nrm_ref[...] = nrm_ref[...] + s


def _hy_filters(seq, w1, b1, freq, w2, b2, w3):
    t = jnp.linspace(0.0, 1.0, seq, dtype=f32)[:, None]
    bands = (HY_EMB - 1) // 2
    w = 2 * math.pi * jnp.arange(seq, dtype=f32)[:, None] / seq
    f = jnp.linspace(1e-4, bands - 1, bands, dtype=f32)[None, :]
    z = jnp.concatenate([t, jnp.cos(f * w), -jnp.sin(f * w), jnp.zeros((seq, 128 - HY_EMB), f32)], axis=-1)
    w1p = jnp.concatenate([w1, jnp.zeros((128 - HY_EMB, w1.shape[1]), f32)], axis=0).astype(bf16)
    hid = w1.shape[1]
    n_out = w3.shape[1]
    deltas = jnp.linspace(math.log(HY_TARGET) / HY_SLOW_DECAY, math.log(HY_TARGET) / HY_FAST_DECAY,
                          HY_WIDTH, dtype=f32).reshape(1, HY_WIDTH)
    tl = 512
    const = lambda a: pl.BlockSpec(a.shape, lambda i: (0,) * a.ndim)
    ins = [z, w1p, b1.reshape(1, hid), freq, w2.astype(bf16), b2.reshape(1, hid), w3.astype(bf16), deltas]
    return pl.pallas_call(
        _hy_filter_kernel,
        grid=(seq // tl,),
        in_specs=[pl.BlockSpec((tl, 128), lambda i: (i, 0))] + [const(a) for a in ins[1:]],
        out_specs=[pl.BlockSpec((tl, n_out), lambda i: (i, 0)), pl.BlockSpec((1, n_out), lambda i: (0, 0))],
        out_shape=[jax.ShapeDtypeStruct((seq, n_out), f32), jax.ShapeDtypeStruct((1, n_out), f32)],
        compiler_params=_cparams(("arbitrary",)),
        name="hy_filter",
    )(*ins)


def _dft_tables(seq):
    n2 = HY_N2
    n_tot = 2 * seq
    n1 = n_tot // n2
    nf = n1 // 2 + 1
    nfp = -(-nf // 8) * 8
    f1 = jnp.arange(nfp, dtype=i32)
    k1 = jnp.arange(n1 // 2, dtype=i32)
    th = (2 * math.pi / n1) * ((f1[:, None] * k1[None, :]) % n1).astype(f32)
    valid = (f1 < nf)[:, None]
    fwd = jnp.concatenate([jnp.where(valid, jnp.cos(th), 0.0), jnp.where(valid, -jnp.sin(th), 0.0)], axis=0)
    eye = jnp.eye(8, dtype=f32)
    fk = jnp.einsum('rk,jh->rjkh', fwd, eye).reshape(2 * nfp * 8, (n1 // 2) * 8)
    cw = jnp.where(valid, jnp.where((f1 == 0) | (f1 == n1 // 2), 1.0, 2.0)[:, None], 0.0) / n_tot
    inv = jnp.concatenate([(cw * jnp.cos(th)).T, (-cw * jnp.sin(th)).T], axis=1)
    ck = jnp.einsum('kr,jh->kjrh', inv, eye).reshape((n1 // 2) * 8, 2 * nfp * 8)
    fa = jnp.arange(nf, dtype=i32)[:, None, None]
    f2 = jnp.arange(n2, dtype=i32)[None, :, None]
    m2 = jnp.arange(n2, dtype=i32)[None, None, :]
    ph = (2 * math.pi / n_tot) * ((f2 * m2 * n1 + m2 * fa) % n_tot).astype(f32)
    gr, gi = jnp.cos(ph), -jnp.sin(ph)
    g = jnp.concatenate([jnp.concatenate([gr, -gi], axis=2), jnp.concatenate([gi, gr], axis=2)], axis=1)
    hr, hi = jnp.swapaxes(jnp.cos(ph), 1, 2), jnp.swapaxes(jnp.sin(ph), 1, 2)
    h = jnp.concatenate([jnp.concatenate([hr, -hi], axis=2), jnp.concatenate([hi, hr], axis=2)], axis=1)
    return fk.astype(bf16), ck.astype(bf16), g.astype(bf16), h.astype(bf16), nf, nfp


def _hy_rows_per_step(nf):
    return max(d for d in (5, 4, 3, 2, 1) if nf % d == 0)


def _dft_stage1(x_ref, fk_ref, a_scr):
    n1h, n2, wb = x_ref.shape

    def slab(s, carry):
        rows = pl.ds(pl.multiple_of(s * 8, 8), 8)
        xs = x_ref[:, rows, :].reshape(n1h * 8, wb).astype(bf16)
        a = jnp.dot(fk_ref[...], xs, preferred_element_type=f32)
        a_scr[:, rows, :] = a.reshape(a_scr.shape[0], 8, wb)
        return carry

    lax.fori_loop(0, n2 // 8, slab, 0)


def _dft_stage2(a_scr, g, f1, nfp):
    a = jnp.concatenate([a_scr[f1], a_scr[nfp + f1]], axis=0).astype(bf16)
    return jnp.dot(g, a, preferred_element_type=f32)


def _hy_spectrum_kernel(kf_ref, kb_ref, fk_ref, g_ref, o_ref, af_scr, ab_scr, *, nfp):
    fs = pl.program_id(2)

    @pl.when(fs == 0)
    def _():
        _dft_stage1(kf_ref, fk_ref, af_scr)
        _dft_stage1(kb_ref, fk_ref, ab_scr)

    fb = g_ref.shape[0]
    n2 = kf_ref.shape[1]
    for j in range(fb):
        f1 = fs * fb + j
        sf = _dft_stage2(af_scr, g_ref[j], f1, nfp)
        sb = _dft_stage2(ab_scr, g_ref[j], f1, nfp)
        o_ref[0, j] = sf[:n2] + sb[:n2]
        o_ref[1, j] = sf[n2:] - sb[n2:]


def _hy_spectrum(k, fk, g, nf, nfp, seq):
    n2, wb = HY_N2, HY_SPEC_WB
    n1h = seq // n2
    fb = _hy_rows_per_step(nf)
    nwb = HY_WIDTH // wb
    k3 = k.reshape(n1h, n2, k.shape[1])
    return pl.pallas_call(
        functools.partial(_hy_spectrum_kernel, nfp=nfp),
        grid=(HY_ORDER, nwb, nf // fb),
        in_specs=[pl.BlockSpec((n1h, n2, wb), lambda o, w, f: (0, 0, (2 * o) * nwb + w)),
                  pl.BlockSpec((n1h, n2, wb), lambda o, w, f: (0, 0, (2 * o + 1) * nwb + w)),
                  pl.BlockSpec(fk.shape, lambda o, w, f: (0, 0)),
                  pl.BlockSpec((fb, 2 * n2, 2 * n2), lambda o, w, f: (f, 0, 0))],
        out_specs=pl.BlockSpec((None, 2, fb, n2, wb), lambda o, w, f: (o, 0, f, 0, w)),
        out_shape=jax.ShapeDtypeStruct((HY_ORDER, 2, nf, n2, HY_WIDTH), f32),
        scratch_shapes=[pltpu.VMEM((2 * nfp, n2, wb), f32), pltpu.VMEM((2 * nfp, n2, wb), f32)],
        compiler_params=_cparams(("arbitrary",) * 3),
        name="hy_spectrum",
    )(k3, k3, fk, g)


def _hy_conv_kernel(x_ref, gate_ref, kf_ref, g_ref, h_ref, fk_ref, ck_ref, bias_ref, inv_ref, o_ref, a_scr, *, nfp):
    fs = pl.program_id(2)
    n1h, n2, wb = x_ref.shape

    @pl.when(fs == 0)
    def _():
        _dft_stage1(x_ref, fk_ref, a_scr)

    fb = g_ref.shape[0]
    for j in range(fb):
        f1 = fs * fb + j
        xs = _dft_stage2(a_scr, g_ref[j], f1, nfp)
        xr, xi = xs[:n2], xs[n2:]
        kr, ki = kf_ref[0, j], kf_ref[1, j]
        y = jnp.concatenate([xr * kr - xi * ki, xr * ki + xi * kr], axis=0).astype(bf16)
        bb = jnp.dot(h_ref[j], y, preferred_element_type=f32)
        a_scr[f1] = bb[:n2]
        a_scr[nfp + f1] = bb[n2:]

    @pl.when(fs == pl.num_programs(2) - 1)
    def _():
        bias = bias_ref[...].reshape(1, 1, wb)
        inv = inv_ref[...].reshape(1, 1, wb)

        def slab(s, carry):
            rows = pl.ds(pl.multiple_of(s * 8, 8), 8)
            bs = a_scr[:, rows, :].reshape(2 * nfp * 8, wb).astype(bf16)
            y = jnp.dot(ck_ref[...], bs, preferred_element_type=f32).reshape(n1h, 8, wb)
            o_ref[:, rows, :] = (gate_ref[:, rows, :] * (y * inv + x_ref[:, rows, :] * bias)).astype(o_ref.dtype)
            return carry

        lax.fori_loop(0, n2 // 8, slab, 0)


def _hy_conv(x, gate, kf, order, tabs, bias, inv_norm, out_dtype, *, batch, seq):
    fk, ck, g, h, nf, nfp = tabs
    n2, wb = HY_N2, HY_WB
    n1h = seq // n2
    fb = _hy_rows_per_step(nf)
    width = x.shape[1]
    big = pl.BlockSpec((n1h, n2, wb), lambda b, w, f: (b, 0, w), pipeline_mode=pl.Buffered(1))
    out = pl.pallas_call(
        functools.partial(_hy_conv_kernel, nfp=nfp),
        grid=(batch, width // wb, nf // fb),
        in_specs=[big, big,
                  pl.BlockSpec((None, 2, fb, n2, wb), lambda b, w, f: (order, 0, f, 0, w)),
                  pl.BlockSpec((fb, 2 * n2, 2 * n2), lambda b, w, f: (f, 0, 0)),
                  pl.BlockSpec((fb, 2 * n2, 2 * n2), lambda b, w, f: (f, 0, 0)),
                  pl.BlockSpec(fk.shape, lambda b, w, f: (0, 0)),
                  pl.BlockSpec(ck.shape, lambda b, w, f: (0, 0)),
                  pl.BlockSpec((1, wb), lambda b, w, f: (0, w)),
                  pl.BlockSpec((1, wb), lambda b, w, f: (0, w))],
        out_specs=big,
        out_shape=jax.ShapeDtypeStruct((batch * n1h, n2, width), out_dtype),
        scratch_shapes=[pltpu.VMEM((2 * nfp, n2, wb), f32)],
        compiler_params=_cparams(("arbitrary",) * 3),
        name="hy_conv",
    )(x.reshape(batch * n1h, n2, width), gate.reshape(batch * n1h, n2, width), kf, g, h, fk, ck,
      bias.reshape(1, width), inv_norm.reshape(1, width))
    return out.reshape(batch * seq, width)


def _hyena(hy, conv_w, conv_b, f_w1, f_b1, f_freq, f_w2, f_b2, f_w3, hy_bias, *, batch, seq):
    v, x1, x2 = _short_conv(hy, conv_w, conv_b, batch=batch, seq=seq)
    k, sums = _hy_filters(seq, f_w1, f_b1, f_freq, f_w2, f_b2, f_w3)
    sums = sums.reshape(HY_ORDER, 2, HY_WIDTH)
    inv_norm = 1.0 / (sums[:, 0] + sums[:, 1])
    tabs = _dft_tables(seq)
    kf = _hy_spectrum(k, tabs[0], tabs[2], tabs[4], tabs[5], seq)
    z = _hy_conv(v, x1, kf, 0, tabs, hy_bias[0], inv_norm[0], f32, batch=batch, seq=seq)
    return _hy_conv(z, x2, kf, 1, tabs, hy_bias[1], inv_norm[1], bf16, batch=batch, seq=seq)


S5_TT = 256
S5_GB = 8


def _gelu_tanh(x):
    return 0.5 * x * (1.0 + jnp.tanh(0.7978845608028654 * (x + 0.044715 * x * x * x)))


def _s5_kernel(u_ref, perm_ref, permt_ref, wb_ref, wc_ref, are_ref, aim_ref, pre_ref, pim_ref, *rest, fuse_glu):
    if fuse_glu:
        yf_ref, d_ref, gw_ref, gb_ref, o_ref, st_scr, bu_scr, y_scr = rest
    else:
        o_ref, st_scr, bu_scr, y_scr = rest
    tt = u_ref.shape[0]
    steps = tt // 8
    half = bu_scr.shape[1] // 2

    @pl.when(pl.program_id(1) == 0)
    def _():
        st_scr[...] = jnp.zeros_like(st_scr)

    up = jnp.dot(perm_ref[...], u_ref[...].astype(bf16), preferred_element_type=f32).astype(bf16)
    for gb in range(wb_ref.shape[0]):
        bu_scr[...] = jnp.dot(up[:, gb * 128:(gb + 1) * 128], wb_ref[gb], preferred_element_type=f32)
        a_re = jnp.broadcast_to(are_ref[gb], (8, half))
        a_im = jnp.broadcast_to(aim_ref[gb], (8, half))

        def scan_step(k, h):
            hr, hi = h
            rows = pl.ds(pl.multiple_of(k * 8, 8), 8)
            nr = a_re * hr - a_im * hi + bu_scr[rows, 0:half]
            ni = a_re * hi + a_im * hr + bu_scr[rows, half:]
            bu_scr[rows, 0:half] = nr
            bu_scr[rows, half:] = ni
            return nr, ni

        zero = jnp.zeros((8, half), f32)
        hr, hi = lax.fori_loop(0, steps, scan_step, (zero, zero), unroll=4)

        pw_r = pre_ref[gb, steps - 1:steps, :]
        pw_i = pim_ref[gb, steps - 1:steps, :]
        cur_r = st_scr[gb, :, 0:half]
        cur_i = st_scr[gb, :, half:]
        car_r, car_i = [], []
        for j in range(8):
            car_r.append(cur_r)
            car_i.append(cur_i)
            cur_r, cur_i = (hr[j:j + 1] + pw_r * cur_r - pw_i * cur_i,
                            hi[j:j + 1] + pw_r * cur_i + pw_i * cur_r)
        st_scr[gb, :, 0:half] = cur_r
        st_scr[gb, :, half:] = cur_i
        cr = jnp.concatenate(car_r, axis=0)
        ci = jnp.concatenate(car_i, axis=0)

        def fix_step(k, carry):
            rows = pl.ds(pl.multiple_of(k * 8, 8), 8)
            pr = pre_ref[gb, pl.ds(k, 1), :]
            pi = pim_ref[gb, pl.ds(k, 1), :]
            bu_scr[rows, 0:half] = bu_scr[rows, 0:half] + (pr * cr - pi * ci)
            bu_scr[rows, half:] = bu_scr[rows, half:] + (pr * ci + pi * cr)
            return carry

        lax.fori_loop(0, steps, fix_step, 0, unroll=4)
        y_scr[:, gb * 128:(gb + 1) * 128] = jnp.dot(bu_scr[...].astype(bf16), wc_ref[gb],
                                                      preferred_element_type=f32)
    yp = y_scr[...]
    y_hi = yp.astype(bf16)
    y_lo = (yp - y_hi.astype(f32)).astype(bf16)
    y = (jnp.dot(permt_ref[...], y_hi, preferred_element_type=f32)
         + jnp.dot(permt_ref[...], y_lo, preferred_element_type=f32))
    if fuse_glu:
        g = _gelu_tanh(y + yf_ref[...] + u_ref[...] * d_ref[...])
        z = jnp.dot(g.astype(bf16), gw_ref[...], preferred_element_type=f32) + gb_ref[...]
        o_ref[...] = (g * jax.nn.sigmoid(z)).astype(o_ref.dtype)
    else:
        o_ref[...] = y


def _s5_tables(a_re, a_im, log_dt, b_re, b_im, c_re, c_im, steps):
    ng, ns = a_re.shape
    nb = ng // S5_GB
    lam = lax.complex(a_re, a_im)
    dt = jnp.exp(log_dt)[:, None]
    lam_bar = jnp.exp(lam * dt)
    b_bar = ((lam_bar - 1) / lam)[..., None] * lax.complex(b_re, b_im)
    cm = lax.complex(c_re, c_im)
    pw = jnp.exp((lam * dt)[None] * jnp.arange(1, steps + 1, dtype=f32)[:, None, None])
    eye = jnp.eye(S5_GB, dtype=f32)
    bb = b_bar.reshape(nb, S5_GB, ns, S5_GROUP)
    wb = jnp.concatenate([jnp.einsum('bgpc,gh->bgchp', part, eye).reshape(nb, S5_GB * S5_GROUP, S5_GB * ns)
                          for part in (jnp.real(bb), jnp.imag(bb))], axis=-1)
    cc = cm.reshape(nb, S5_GB, S5_GROUP, ns)
    wc = jnp.concatenate([jnp.einsum('bgcp,gh->bgphc', part, eye).reshape(nb, S5_GB * ns, S5_GB * S5_GROUP)
                          for part in (jnp.real(cc), -jnp.imag(cc))], axis=1)
    lb = lam_bar.reshape(nb, 1, S5_GB * ns)
    pwb = pw.reshape(steps, nb, S5_GB * ns).transpose(1, 0, 2)
    return (wb.astype(bf16), wc.astype(bf16), jnp.real(lb), jnp.imag(lb), jnp.real(pwb), jnp.imag(pwb))


def _s5_perm(tt, reverse):
    steps = tt // 8
    dst = jnp.arange(tt)
    t = (dst % 8) * steps + dst // 8
    src = tt - 1 - t if reverse else t
    return jnp.arange(tt)[None, :] == src[:, None]


def _s5(u, a_re, a_im, log_dt, b_re, b_im, c_re, c_im, s5_d, glu_w, glu_b, *, batch, seq, ctx_len):
    r, width = u.shape
    tt = S5_TT
    assert ctx_len == tt and seq % tt == 0
    n_lat = seq // tt
    ctx_blk0 = (batch * seq) // tt
    y = None
    for di in range(2):
        rev = di == 1
        wb, wc, lr, li, pr, pi = _s5_tables(a_re[di], a_im[di], log_dt[di], b_re[di], b_im[di],
                                            c_re[di], c_im[di], tt // 8)
        perm = _s5_perm(tt, rev)

        def tile_map(b, i, rev=rev):
            lat = b * n_lat + (n_lat - i if rev else i - 1)
            return (jnp.where(i == 0, ctx_blk0 + b, lat), 0)

        row_spec = pl.BlockSpec((tt, width), tile_map)
        const = lambda a: pl.BlockSpec(a.shape, lambda b, i: (0,) * a.ndim)
        ins = [u, perm.astype(bf16), perm.T.astype(bf16), wb, wc, lr, li, pr, pi]
        in_specs = [row_spec] + [const(a) for a in ins[1:]]
        if rev:
            extra = [y, s5_d.reshape(1, width), glu_w.astype(bf16), glu_b.reshape(1, width)]
            in_specs += [row_spec] + [const(a) for a in extra[1:]]
            ins += extra
        y = pl.pallas_call(
            functools.partial(_s5_kernel, fuse_glu=rev),
            grid=(batch, n_lat + 1),
            in_specs=in_specs,
            out_specs=row_spec,
            out_shape=jax.ShapeDtypeStruct((r, width), bf16 if rev else f32),
            scratch_shapes=[pltpu.VMEM((wb.shape[0], 1, wb.shape[2]), f32),
                            pltpu.VMEM((tt, wb.shape[2]), f32),
                            pltpu.VMEM((tt, width), f32)],
            compiler_params=_cparams(("arbitrary", "arbitrary")),
            name="s5_bwd_glu" if rev else "s5_fwd",
        )(*ins)
    return y


def _rope_tables(seq):
    rows = seq // GRID_W
    row = jnp.repeat(jnp.arange(rows, dtype=f32), GRID_W)
    col = jnp.tile(jnp.arange(GRID_W, dtype=f32), rows)
    n_freq = DIFF_QK // 4
    inv = ROPE_BASE ** (-jnp.arange(n_freq, dtype=f32) / n_freq)
    ar, ac = row[:, None] * inv, col[:, None] * inv
    cos = jnp.concatenate([jnp.cos(ar), jnp.cos(ar), jnp.cos(ac), jnp.cos(ac)], axis=-1)
    sin = jnp.concatenate([-jnp.sin(ar), jnp.sin(ar), -jnp.sin(ac), jnp.sin(ac)], axis=-1)
    cos = jnp.concatenate([jnp.tile(cos, (1, 2)), jnp.ones((TM, 128), f32)], axis=0)
    sin = jnp.concatenate([jnp.tile(sin, (1, 2)), jnp.zeros((TM, 128), f32)], axis=0)
    return cos, sin


def kernel(x, c, ctx, c_ctx, ada_w, ada_b, norm_mix_g, norm_ffn_g, norm_out_g, ab_w_in, ab_w_out, diff_lam, diff_subln_g, pool_w, pool_scale, cd_w_in, cd_w_out, hy_conv_w, hy_conv_b, hy_f_w1, hy_f_b1, hy_f_freq, hy_f_w2, hy_f_b2, hy_f_w3, hy_bias, s5_a_re, s5_a_im, s5_log_dt, s5_b_re, s5_b_im, s5_c_re, s5_c_im, s5_d, glu_w, glu_b, moe_w_rg, moe_b_rg, moe_w_re, moe_b_re, moe_w1, moe_w3, moe_w2):
    bsz, seq, d = x.shape
    cl = ctx.shape[1]
    depth = ada_w.shape[0]
    n_lat = bsz * seq
    assert seq % TM == 0 and (bsz * cl) % TM == 0 and bsz + 1 <= 8
    xs = jnp.concatenate([x.reshape(n_lat, d), ctx.reshape(bsz * cl, d)], axis=0)
    r = xs.shape[0]
    cond = jnp.zeros((8, d), f32).at[0:bsz].set(c).at[bsz].set(c_ctx)
    mods = _ada_mods(cond, ada_w, ada_b)
    geom = dict(batch=bsz, seq=seq)

    for l in range(depth):
        last = l == depth - 1
        i = l // 2
        n_tiles = (n_lat if last else r) // TM
        g_mix = norm_mix_g[l].reshape(1, d)
        if l % 2 == 0:
            lam_init = 0.8 - 0.6 * math.exp(-0.3 * l)
            cos, sin = _rope_tables(seq)
            q, k, vt, p = _proj_rope(xs, g_mix, mods[l], ab_w_in[i].astype(bf16), cos, sin, **geom)
            mix_a = _diff_attention(q, k, vt, diff_lam[i], diff_subln_g[i], lam_init, ctx_len=cl, **geom)
            mix_b = _pool(p, pool_w[i], pool_scale[i], ctx_len=cl, **geom)
            w_out = ab_w_out[i]
        else:
            hw = (HY_ORDER + 1) * HY_WIDTH
            hy, u = _proj(xs, g_mix, mods[l], cd_w_in[i].astype(bf16), hw, **geom)
            mix_a = _hyena(hy, hy_conv_w[i], hy_conv_b[i], hy_f_w1[i], hy_f_b1[i], hy_f_freq[i],
                           hy_f_w2[i], hy_f_b2[i], hy_f_w3[i], hy_bias[i], **geom)
            mix_b = _s5(u, s5_a_re[i], s5_a_im[i], s5_log_dt[i], s5_b_re[i], s5_b_im[i],
                        s5_c_re[i], s5_c_im[i], s5_d[i], glu_w[i], glu_b[i], ctx_len=cl, **geom)
            w_out = cd_w_out[i]
        w_r = jnp.zeros((d, 128), f32).at[:, :MOE_GROUPS].set(moe_w_rg[l]) \
            .at[:, MOE_GROUPS:MOE_GROUPS + MOE_EXPERTS].set(moe_w_re[l]).astype(bf16)
        b_r = jnp.zeros((1, 128), f32).at[0, :MOE_GROUPS].set(moe_b_rg[l]) \
            .at[0, MOE_GROUPS:MOE_GROUPS + MOE_EXPERTS].set(moe_b_re[l])
        xn, h2, logits = _outproj(mix_a, mix_b, xs, w_out.astype(bf16), mods[l], norm_ffn_g[l].reshape(1, d),
                                  w_r, b_r, n_tiles=n_tiles, **geom)
        xs = _moe(xn, h2, logits, mods[l], moe_w1, moe_w3, moe_w2, l, norm_out_g.reshape(1, d),
                  n_tiles=n_tiles, final_norm=last, **geom)
    return xs[:n_lat].reshape(bsz, seq, d)
```

```python
import functools
import math

import jax
import jax.numpy as jnp
from jax import lax
from jax.experimental import pallas as pl
from jax.experimental.pallas import tpu as pltpu

f32, bf16, i32 = jnp.float32, jnp.bfloat16, jnp.int32

EPS = 1e-6
GRID_W = 64
DIFF_QK = 64
DIFF_V = 128
ROPE_BASE = 10000.0
POOL_WINDOWS = (2, 4, 8, 16)
POOL_GROUP = 256
HY_WIDTH = 1024
HY_ORDER = 2
HY_EMB = 33
HY_FAST_DECAY = 0.3
HY_SLOW_DECAY = 1.5
HY_TARGET = 1e-2
S5_GROUP = 16
S5_GROUPS = 64
S5_STATE = 64
MOE_GROUPS = 8
MOE_PER_GROUP = 8
MOE_EXPERTS = 64
MOE_TOPK = 2
MOE_BLOCK = 256

VMEM_LIMIT = 56 * 1024 * 1024
TM = 512
LOG2E = 1.4426950408889634


def _cparams(sem):
    return pltpu.CompilerParams(dimension_semantics=sem, vmem_limit_bytes=VMEM_LIMIT)


def _norm_mod(x, g, shift, scale):
    ms = jnp.mean(x * x, axis=-1, keepdims=True)
    return (x * lax.rsqrt(ms + EPS) * g) * (1.0 + scale) + shift


def _tile_segment(i, lat_tiles_per_batch, batch):
    seg = jnp.zeros((), i32)
    for b in range(1, batch + 1):
        seg = seg + (i >= b * lat_tiles_per_batch).astype(i32)
    return seg


def _ada_kernel(s_ref, w_ref, b_ref, o_ref):
    s = s_ref[...]
    a = (s * jax.nn.sigmoid(s)).astype(bf16)
    o_ref[...] = jnp.dot(a, w_ref[...].astype(bf16), preferred_element_type=f32) + b_ref[...]


def _ada_mods(s, ada_w, ada_b):
    depth, d, n = ada_w.shape
    tn = 1024
    return pl.pallas_call(
        _ada_kernel,
        grid=(depth, n // tn),
        in_specs=[pl.BlockSpec((8, d), lambda l, j: (0, 0)),
                  pl.BlockSpec((None, d, tn), lambda l, j: (l, 0, j)),
                  pl.BlockSpec((None, 1, tn), lambda l, j: (l, 0, j))],
        out_specs=pl.BlockSpec((None, 8, tn), lambda l, j: (l, 0, j)),
        out_shape=jax.ShapeDtypeStruct((depth, 8, n), f32),
        compiler_params=_cparams(("arbitrary", "arbitrary")),
        name="ada_mods",
    )(s, ada_w, ada_b.reshape(depth, 1, n))


def _proj_rope_kernel(x_ref, g_ref, sh_ref, sc_ref, w_ref, cos_ref, sin_ref,
                      q_ref, k_ref, v_ref, p_ref, h_scr, *, tiles_per_batch, batch, qscale):
    seg = _tile_segment(pl.program_id(0), tiles_per_batch, batch)
    h = _norm_mod(x_ref[...], g_ref[...], sh_ref[pl.ds(seg, 1), :], sc_ref[pl.ds(seg, 1), :])
    h_scr[...] = h.astype(bf16)
    cos = cos_ref[...]
    sin = sin_ref[...]
    tm = cos.shape[0]
    lane = lax.broadcasted_iota(i32, (tm, 128), 1)
    first = (lane % 32) < 16

    def rope(r):
        sw = jnp.where(first, pltpu.roll(r, 112, 1), pltpu.roll(r, 16, 1))
        return r * cos + sw * sin

    heads = q_ref.shape[0]
    nc = 512
    for c in range(w_ref.shape[1] // nc):
        res = jnp.dot(h_scr[...], w_ref[:, c * nc:(c + 1) * nc], preferred_element_type=f32)
        for s in range(nc // 128):
            r = res[:, s * 128:(s + 1) * 128]
            col = (c * nc) // 128 + s
            if col < heads:
                q_ref[col] = (rope(r) * qscale).astype(bf16)
            elif col < 2 * heads:
                k_ref[col - heads] = rope(r).astype(bf16)
            elif col < 3 * heads:
                v_ref[col - 2 * heads] = r.T.astype(bf16)
            else:
                c0 = (col - 3 * heads) * 128
                p_ref[:, c0:c0 + 128] = r


def _proj_rope(x, g, mods, w, cos, sin, *, batch, seq):
    r, d = x.shape
    heads = 8
    lat_tiles = seq // TM
    n_tiles = r // TM

    def tab_map(i):
        return (jnp.where(i < batch * lat_tiles, i % lat_tiles, lat_tiles), 0)

    hm = jax.ShapeDtypeStruct((heads, r, 128), bf16)
    hm_spec = pl.BlockSpec((heads, TM, 128), lambda i: (0, i, 0))
    return pl.pallas_call(
        functools.partial(_proj_rope_kernel, tiles_per_batch=lat_tiles, batch=batch,
                          qscale=LOG2E * DIFF_QK ** -0.5),
        grid=(n_tiles,),
        in_specs=[pl.BlockSpec((TM, d), lambda i: (i, 0)),
                  pl.BlockSpec((1, d), lambda i: (0, 0)),
                  pl.BlockSpec((8, d), lambda i: (0, 0)),
                  pl.BlockSpec((8, d), lambda i: (0, 1)),
                  pl.BlockSpec(w.shape, lambda i: (0, 0), pipeline_mode=pl.Buffered(1)),
                  pl.BlockSpec((TM, 128), tab_map),
                  pl.BlockSpec((TM, 128), tab_map)],
        out_specs=[hm_spec, hm_spec, pl.BlockSpec((heads, 128, TM), lambda i: (0, 0, i)),
                   pl.BlockSpec((TM, 1024), lambda i: (i, 0))],
        out_shape=[hm, hm, jax.ShapeDtypeStruct((heads, 128, r), bf16), jax.ShapeDtypeStruct((r, 1024), f32)],
        scratch_shapes=[pltpu.VMEM((TM, d), bf16)],
        compiler_params=_cparams(("arbitrary",)),
        name="proj_rope",
    )(x, g, mods, mods, w, cos, sin)


def _proj_kernel(x_ref, g_ref, sh_ref, sc_ref, w_ref, hy_ref, u_ref, h_scr, *, tiles_per_batch, batch):
    seg = _tile_segment(pl.program_id(0), tiles_per_batch, batch)
    h = _norm_mod(x_ref[...], g_ref[...], sh_ref[pl.ds(seg, 1), :], sc_ref[pl.ds(seg, 1), :])
    h_scr[...] = h.astype(bf16)
    nc = 512
    n_hy = hy_ref.shape[1]
    for c in range(w_ref.shape[1] // nc):
        res = jnp.dot(h_scr[...], w_ref[:, c * nc:(c + 1) * nc], preferred_element_type=f32)
        if c * nc < n_hy:
            hy_ref[:, c * nc:(c + 1) * nc] = res
        else:
            u_ref[:, c * nc - n_hy:(c + 1) * nc - n_hy] = res


def _proj(x, g, mods, w, n_hy, *, batch, seq):
    r, d = x.shape
    n = w.shape[1]
    return pl.pallas_call(
        functools.partial(_proj_kernel, tiles_per_batch=seq // TM, batch=batch),
        grid=(r // TM,),
        in_specs=[pl.BlockSpec((TM, d), lambda i: (i, 0)),
                  pl.BlockSpec((1, d), lambda i: (0, 0)),
                  pl.BlockSpec((8, d), lambda i: (0, 0)),
                  pl.BlockSpec((8, d), lambda i: (0, 1)),
                  pl.BlockSpec(w.shape, lambda i: (0, 0), pipeline_mode=pl.Buffered(1))],
        out_specs=[pl.BlockSpec((TM, n_hy), lambda i: (i, 0)), pl.BlockSpec((TM, n - n_hy), lambda i: (i, 0))],
        out_shape=[jax.ShapeDtypeStruct((r, n_hy), f32), jax.ShapeDtypeStruct((r, n - n_hy), f32)],
        scratch_shapes=[pltpu.VMEM((TM, d), bf16)],
        compiler_params=_cparams(("arbitrary",)),
        name="proj",
    )(x, g, mods, mods, w)


def _attn_kernel(lv_ref, g_ref, q_ref, kc_ref, vc_ref, *rest, n_lat, tk, lam_init):
    if n_lat:
        kl_ref, vl_ref, o_ref, s_scr = rest
    else:
        o_ref, s_scr = rest
    lv = lv_ref[...]
    lam = (jnp.exp(jnp.sum(lv[0:1] * lv[1:2], axis=-1, keepdims=True))
           - jnp.exp(jnp.sum(lv[2:3] * lv[3:4], axis=-1, keepdims=True)) + lam_init)
    q = q_ref[...]
    tq = q.shape[0]
    lane = lax.broadcasted_iota(i32, q.shape, 1)
    zero = jnp.zeros_like(q)
    qs = (jnp.where(lane < DIFF_QK, q, zero), jnp.where(lane >= DIFF_QK, q, zero))
    nt = (((1,), (1,)), ((), ()))

    def scores(slot, k):
        for m in range(2):
            s_scr[slot, m, 0:k.shape[0], :] = lax.dot_general(k, qs[m], nt, preferred_element_type=f32)

    def update(carry, slot, nk, vt):
        new = []
        for m in range(2):
            mx, l, acc = carry[3 * m:3 * m + 3]
            s = s_scr[slot, m, 0:nk, :]
            mn = jnp.maximum(mx, jnp.max(s, axis=0, keepdims=True))
            alpha = jnp.exp2(mx - mn)
            p = jnp.exp2(s - mn)
            l = alpha * l + jnp.sum(p, axis=0, keepdims=True)
            acc = alpha * acc + jnp.dot(vt, p.astype(bf16), preferred_element_type=f32)
            new += [mn, l, acc]
        return tuple(new)

    def k_block(j):
        return kl_ref[pl.ds(pl.multiple_of(j * tk, tk), tk), :]

    def vt_block(j):
        return vl_ref[:, pl.ds(pl.multiple_of(j * tk, tk), tk)]

    init = (jnp.full((1, tq), -jnp.inf, f32), jnp.zeros((1, tq), f32), jnp.zeros((DIFF_V, tq), f32)) * 2
    nc = kc_ref.shape[0]
    scores(1, kc_ref[...])
    if n_lat:
        scores(0, k_block(0))
        carry = update(init, 1, nc, vc_ref[...])

        def body(jj, carry):
            j = 4 * jj
            scores(1, k_block(j + 1))
            carry = update(carry, 0, tk, vt_block(j))
            scores(0, k_block(j + 2))
            carry = update(carry, 1, tk, vt_block(j + 1))
            scores(1, k_block(j + 3))
            carry = update(carry, 0, tk, vt_block(j + 2))
            scores(0, k_block(jnp.minimum(j + 4, n_lat - 1)))
            return update(carry, 1, tk, vt_block(j + 3))

        carry = lax.fori_loop(0, n_lat // 4, body, carry)
    else:
        carry = update(init, 1, nc, vc_ref[...])
    _, l0, a0, _, l1, a1 = carry
    o = a0 / l0 - lam * (a1 / l1)
    ms = jnp.mean(o * o, axis=0, keepdims=True)
    y = (o * lax.rsqrt(ms + EPS) * g_ref[...]) * (1.0 - lam_init)
    o_ref[...] = y.T.astype(o_ref.dtype)


def _diff_attention(q, k, vt, lam_vecs, subln_g, lam_init, *, batch, seq, ctx_len):
    heads, r, _ = q.shape
    tq, tk = 512, 512
    ctx_blk0 = (batch * seq) // ctx_len
    assert (seq // tk) % 4 == 0
    small = [pl.BlockSpec((4, DIFF_QK), lambda *_: (0, 0)), pl.BlockSpec((DIFF_V, 1), lambda *_: (0, 0))]
    kc_spec = pl.BlockSpec((None, ctx_len, 128), lambda b, h, *_: (h, ctx_blk0 + b, 0))
    vc_spec = pl.BlockSpec((None, 128, ctx_len), lambda b, h, *_: (h, 0, ctx_blk0 + b))
    g2 = subln_g.reshape(DIFF_V, 1)
    lat = pl.pallas_call(
        functools.partial(_attn_kernel, n_lat=seq // tk, tk=tk, lam_init=lam_init),
        grid=(batch, heads, seq // tq),
        in_specs=small + [pl.BlockSpec((None, tq, 128), lambda b, h, i: (h, b * (seq // tq) + i, 0)),
                          kc_spec, vc_spec,
                          pl.BlockSpec((None, seq, 128), lambda b, h, i: (h, b, 0)),
                          pl.BlockSpec((None, 128, seq), lambda b, h, i: (h, 0, b))],
        out_specs=pl.BlockSpec((tq, 128), lambda b, h, i: (b * (seq // tq) + i, h)),
        out_shape=jax.ShapeDtypeStruct((batch * seq, heads * DIFF_V), bf16),
        scratch_shapes=[pltpu.VMEM((2, 2, tk, tq), f32)],
        compiler_params=_cparams(("arbitrary",) * 3),
        name="diff_attn_lat",
    )(lam_vecs, g2, q, k, vt, k, vt)
    ctx = pl.pallas_call(
        functools.partial(_attn_kernel, n_lat=0, tk=tk, lam_init=lam_init),
        grid=(batch, heads),
        in_specs=small + [kc_spec, kc_spec, vc_spec],
        out_specs=pl.BlockSpec((ctx_len, 128), lambda b, h: (b, h)),
        out_shape=jax.ShapeDtypeStruct((batch * ctx_len, heads * DIFF_V), bf16),
        scratch_shapes=[pltpu.VMEM((2, 2, ctx_len, ctx_len), f32)],
        compiler_params=_cparams(("arbitrary",) * 2),
        name="diff_attn_ctx",
    )(lam_vecs, g2, q, k, vt)
    return jnp.concatenate([lat, ctx], axis=0)


def _pool_kernel(p_ref, prev_ref, next_ref, w_ref, scale_ref, o_ref, ext_scr, *,
                 tiles_per_batch, batch, seq, ctx_len):
    i = pl.program_id(0)
    tp = p_ref.shape[0]
    is_ctx = i >= batch * tiles_per_batch
    j = jnp.where(is_ctx, 0, i % tiles_per_batch)
    seg_len = jnp.where(is_ctx, ctx_len, seq)
    last = jnp.where(is_ctx, 0, tiles_per_batch - 1)
    ext_scr[0:8, :] = jnp.where(j > 0, prev_ref[...], 0.0)
    ext_scr[8:8 + tp, :] = p_ref[...]
    ext_scr[8 + tp:16 + tp, :] = jnp.where(j < last, next_ref[...], 0.0)
    pos = j * tp + lax.broadcasted_iota(i32, (tp, 1), 0)
    for gi, w in enumerate(POOL_WINDOWS):
        c0 = gi * POOL_GROUP
        half = w // 2
        tot = ext_scr[8 - half:8 - half + tp, c0:c0 + POOL_GROUP]
        for dlt in range(1 - half, half):
            tot = tot + ext_scr[8 + dlt:8 + dlt + tp, c0:c0 + POOL_GROUP]
        cnt = (jnp.minimum(pos - half + w, seg_len) - jnp.maximum(pos - half, 0)).astype(f32)
        dlt_mean = (tot / cnt - ext_scr[8:8 + tp, c0:c0 + POOL_GROUP]).astype(bf16)
        y = jnp.dot(dlt_mean, w_ref[gi].astype(bf16), preferred_element_type=f32)
        o_ref[:, c0:c0 + POOL_GROUP] = (y * scale_ref[:, c0:c0 + POOL_GROUP]).astype(o_ref.dtype)


def _pool(p, pool_w, pool_scale, *, batch, seq, ctx_len):
    r, width = p.shape
    tp = ctx_len
    tpb = seq // tp
    return pl.pallas_call(
        functools.partial(_pool_kernel, tiles_per_batch=tpb, batch=batch, seq=seq, ctx_len=ctx_len),
        grid=(r // tp,),
        in_specs=[pl.BlockSpec((tp, width), lambda i: (i, 0)),
                  pl.BlockSpec((8, width), lambda i: (jnp.maximum(i * (tp // 8) - 1, 0), 0)),
                  pl.BlockSpec((8, width), lambda i: (jnp.minimum((i + 1) * (tp // 8), r // 8 - 1), 0)),
                  pl.BlockSpec(pool_w.shape, lambda i: (0, 0, 0)),
                  pl.BlockSpec((1, width), lambda i: (0, 0))],
        out_specs=pl.BlockSpec((tp, width), lambda i: (i, 0)),
        out_shape=jax.ShapeDtypeStruct((r, width), bf16),
        scratch_shapes=[pltpu.VMEM((tp + 16, width), f32)],
        compiler_params=_cparams(("arbitrary",)),
        name="pool",
    )(p, p, p, pool_w, pool_scale.reshape(1, width))


def _outproj_kernel(a_ref, b_ref, x_ref, w_ref, g1_ref, sh_ref, sc_ref, gn_ref, wr_ref, br_ref,
                    xn_ref, h2_ref, lg_ref, *, tiles_per_batch, batch):
    seg = _tile_segment(pl.program_id(0), tiles_per_batch, batch)
    half = a_ref.shape[1]
    y = (jnp.dot(a_ref[...], w_ref[0:half, :], preferred_element_type=f32)
         + jnp.dot(b_ref[...], w_ref[half:, :], preferred_element_type=f32))
    xn = x_ref[...] + g1_ref[pl.ds(seg, 1), :] * y
    xn_ref[...] = xn
    h2 = _norm_mod(xn, gn_ref[...], sh_ref[pl.ds(seg, 1), :], sc_ref[pl.ds(seg, 1), :])
    h2_ref[...] = h2
    lg_ref[...] = jnp.dot(h2.astype(bf16), wr_ref[...], preferred_element_type=f32) + br_ref[...]


def _outproj(a, b, x, w, mods, g_ffn, w_r, b_r, *, n_tiles, batch, seq):
    r, d = n_tiles * TM, x.shape[1]
    half = a.shape[1]
    row = lambda i: (i, 0)
    return pl.pallas_call(
        functools.partial(_outproj_kernel, tiles_per_batch=seq // TM, batch=batch),
        grid=(n_tiles,),
        in_specs=[pl.BlockSpec((TM, half), row), pl.BlockSpec((TM, half), row), pl.BlockSpec((TM, d), row),
                  pl.BlockSpec(w.shape, lambda i: (0, 0), pipeline_mode=pl.Buffered(1)),
                  pl.BlockSpec((8, d), lambda i: (0, 2)),
                  pl.BlockSpec((8, d), lambda i: (0, 3)),
                  pl.BlockSpec((8, d), lambda i: (0, 4)),
                  pl.BlockSpec((1, d), lambda i: (0, 0)),
                  pl.BlockSpec(w_r.shape, lambda i: (0, 0)),
                  pl.BlockSpec((1, 128), lambda i: (0, 0))],
        out_specs=[pl.BlockSpec((TM, d), row), pl.BlockSpec((TM, d), row), pl.BlockSpec((TM, 128), row)],
        out_shape=[jax.ShapeDtypeStruct((r, d), f32), jax.ShapeDtypeStruct((r, d), f32),
                   jax.ShapeDtypeStruct((r, 128), f32)],
        compiler_params=_cparams(("arbitrary",)),
        name="outproj",
    )(a, b, x, w, mods, mods, mods, g_ffn, w_r, b_r)


def _moe_kernel(be_ref, nu_ref, nxt_ref, xs_ref, w1_hbm, w3_hbm, w2_hbm, ys_ref,
                w1_f32, w3_f32, w2_f32, w1_scr, w3_scr, w2_scr, sems, *, layer):
    i = pl.program_id(0)
    e = be_ref[i]
    prev = be_ref[jnp.maximum(i - 1, 0)]
    used = i < nu_ref[0]

    def weight_copies(expert):
        return (pltpu.make_async_copy(w1_hbm.at[layer, expert], w1_f32, sems.at[0]),
                pltpu.make_async_copy(w3_hbm.at[layer, expert], w3_f32, sems.at[1]),
                pltpu.make_async_copy(w2_hbm.at[layer, expert], w2_f32, sems.at[2]))

    @pl.when(i == 0)
    def _():
        for cp in weight_copies(e):
            cp.start()

    @pl.when(used & ((i == 0) | (e != prev)))
    def _():
        for cp in weight_copies(e):
            cp.wait()
        w1_scr[...] = w1_f32[...].astype(bf16)
        w3_scr[...] = w3_f32[...].astype(bf16)
        w2_scr[...] = w2_f32[...].astype(bf16)
        nxt = nxt_ref[e]

        @pl.when(nxt >= 0)
        def _():
            for cp in weight_copies(nxt):
                cp.start()

    @pl.when(used)
    def _():
        xb = xs_ref[...].astype(bf16)
        a = jnp.dot(xb, w1_scr[...], preferred_element_type=f32)
        b = jnp.dot(xb, w3_scr[...], preferred_element_type=f32)
        hid = (a * jax.nn.sigmoid(a) * b).astype(bf16)
        ys_ref[...] = jnp.dot(hid, w2_scr[...], preferred_element_type=f32)

    @pl.when(jnp.logical_not(used))
    def _():
        ys_ref[...] = jnp.zeros_like(ys_ref)


def _moe_experts(xs, block_e, n_used, next_expert, w1, w3, w2, layer, nb):
    m = MOE_BLOCK
    _, _, d, de = w1.shape
    hbm = pl.BlockSpec(memory_space=pl.ANY)
    grid_spec = pltpu.PrefetchScalarGridSpec(
        num_scalar_prefetch=3,
        grid=(nb,),
        in_specs=[pl.BlockSpec((m, d), lambda i, be, nu, nx: (jnp.minimum(i, nu[0] - 1), 0)), hbm, hbm, hbm],
        out_specs=pl.BlockSpec((m, d), lambda i, be, nu, nx: (i, 0)),
        scratch_shapes=[pltpu.VMEM((d, de), f32), pltpu.VMEM((d, de), f32), pltpu.VMEM((de, d), f32),
                        pltpu.VMEM((d, de), bf16), pltpu.VMEM((d, de), bf16), pltpu.VMEM((de, d), bf16),
                        pltpu.SemaphoreType.DMA((3,))],
    )
    return pl.pallas_call(
        functools.partial(_moe_kernel, layer=layer),
        grid_spec=grid_spec,
        out_shape=jax.ShapeDtypeStruct((nb * m, d), f32),
        compiler_params=_cparams(("arbitrary",)),
        name="moe_experts",
    )(block_e, n_used, next_expert, xs, w1, w3, w2)


def _route_kernel(lg_ref, tri_ref, o_ref, cnt_ref, run_scr):
    @pl.when(pl.program_id(0) == 0)
    def _():
        run_scr[...] = jnp.zeros_like(run_scr)

    lg = lg_ref[...]
    lane = lax.broadcasted_iota(i32, lg.shape, 1)
    ninf = -jnp.inf

    def first_lane(mask):
        return jnp.min(jnp.where(mask, lane, lg.shape[1]), axis=-1, keepdims=True)

    gl = jnp.where(lane < MOE_GROUPS, lg, ninf)
    gmax = jnp.max(gl, axis=-1, keepdims=True)
    pg_top = 1.0 / jnp.sum(jnp.exp(gl - gmax), axis=-1, keepdims=True)
    lo = MOE_GROUPS + first_lane(gl == gmax) * MOE_PER_GROUP
    sel = (lane >= lo) & (lane < lo + MOE_PER_GROUP)
    el = jnp.where(sel, lg, ninf)
    ee = jnp.exp(el - jnp.max(el, axis=-1, keepdims=True))
    pe = jnp.where(sel, ee / jnp.sum(ee, axis=-1, keepdims=True), -1.0)
    p1 = jnp.max(pe, axis=-1, keepdims=True)
    i1 = first_lane(pe == p1)
    pe2 = jnp.where(lane == i1, -1.0, pe)
    p2 = jnp.max(pe2, axis=-1, keepdims=True)
    i2 = first_lane(pe2 == p2)
    den = p1 + p2
    e1 = i1 - MOE_GROUPS
    e2 = i2 - MOE_GROUPS
    hit1 = lane == e1
    hit2 = lane == e2
    onehot = jnp.where(hit1, 1.0, jnp.where(hit2, 1.0, 0.0))
    base = jnp.dot(tri_ref[...], onehot.astype(bf16), preferred_element_type=f32) + run_scr[...]
    r1 = jnp.sum(jnp.where(hit1, base, 0.0), axis=-1, keepdims=True)
    r2 = jnp.sum(jnp.where(hit2, base, 0.0), axis=-1, keepdims=True)
    run_scr[...] = run_scr[...] + jnp.sum(onehot, axis=0, keepdims=True)
    cnt_ref[...] = run_scr[...]
    vals = (e1.astype(f32), e2.astype(f32), pg_top * p1 / den, pg_top * p2 / den, r1, r2)
    out = jnp.zeros(lg.shape, f32)
    for j, v in enumerate(vals):
        out = jnp.where(lane == j, v, out)
    o_ref[...] = out


def _route(logits):
    t, width = logits.shape
    tri = jnp.tril(jnp.ones((TM, TM), f32), -1).astype(bf16)
    return pl.pallas_call(
        _route_kernel,
        grid=(t // TM,),
        in_specs=[pl.BlockSpec((TM, width), lambda i: (i, 0)), pl.BlockSpec((TM, TM), lambda i: (0, 0))],
        out_specs=[pl.BlockSpec((TM, width), lambda i: (i, 0)), pl.BlockSpec((1, width), lambda i: (0, 0))],
        out_shape=[jax.ShapeDtypeStruct((t, width), f32), jax.ShapeDtypeStruct((1, width), f32)],
        scratch_shapes=[pltpu.VMEM((1, width), f32)],
        compiler_params=_cparams(("arbitrary",)),
        name="route",
    )(logits, tri)


def _dispatch_kernel(dest_ref, h_ref, init_ref, xs_ref, sem):
    del init_ref
    i = pl.program_id(0)
    tm = h_ref.shape[0]

    def row_copy(r, k):
        slot = dest_ref[(i * tm + r) * MOE_TOPK + k]
        return pltpu.make_async_copy(h_ref.at[pl.ds(r, 1)], xs_ref.at[pl.ds(slot, 1)], sem)

    def issue(r, carry):
        for k in range(MOE_TOPK):
            row_copy(r, k).start()
        return carry

    lax.fori_loop(0, tm, issue, 0, unroll=8)
    for k in range(MOE_TOPK):
        pltpu.make_async_copy(h_ref, xs_ref.at[pl.ds(0, tm)], sem).wait()


def _dispatch(h2, dest, nb):
    t, d = h2.shape
    rows = nb * MOE_BLOCK
    grid_spec = pltpu.PrefetchScalarGridSpec(
        num_scalar_prefetch=1,
        grid=(t // TM,),
        in_specs=[pl.BlockSpec((TM, d), lambda i, dst: (i, 0)), pl.BlockSpec(memory_space=pl.ANY)],
        out_specs=pl.BlockSpec(memory_space=pl.ANY),
        scratch_shapes=[pltpu.SemaphoreType.DMA(())],
    )
    return pl.pallas_call(
        _dispatch_kernel,
        grid_spec=grid_spec,
        out_shape=jax.ShapeDtypeStruct((rows, d), h2.dtype),
        input_output_aliases={2: 0},
        compiler_params=_cparams(("arbitrary",)),
        name="moe_dispatch",
    )(dest, h2, jnp.zeros((rows, d), h2.dtype))


def _combine_kernel(dest_ref, x_ref, rt_ref, g2_ref, gn_ref, ys_ref, o_ref, y_scr, sem, *,
                    tiles_per_batch, batch, final_norm):
    i = pl.program_id(0)
    tm = x_ref.shape[0]

    def row_copy(r, k):
        slot = dest_ref[(i * tm + r) * MOE_TOPK + k]
        return pltpu.make_async_copy(ys_ref.at[pl.ds(slot, 1)], y_scr.at[k, pl.ds(r, 1)], sem)

    def issue(r, carry):
        for k in range(MOE_TOPK):
            row_copy(r, k).start()
        return carry

    lax.fori_loop(0, tm, issue, 0, unroll=8)
    for k in range(MOE_TOPK):
        pltpu.make_async_copy(ys_ref.at[pl.ds(0, tm)], y_scr.at[k], sem).wait()
    seg = _tile_segment(i, tiles_per_batch, batch)
    rt = rt_ref[...]
    y = rt[:, MOE_TOPK:MOE_TOPK + 1] * y_scr[0]
    for k in range(1, MOE_TOPK):
        y = y + rt[:, MOE_TOPK + k:MOE_TOPK + k + 1] * y_scr[k]
    x = x_ref[...] + g2_ref[pl.ds(seg, 1), :] * y
    if final_norm:
        ms = jnp.mean(x * x, axis=-1, keepdims=True)
        x = x * lax.rsqrt(ms + EPS) * gn_ref[...]
    o_ref[...] = x


def _combine(x, ys, dest, routed, mods, g_out, *, n_tiles, batch, seq, final_norm):
    d = x.shape[1]
    row = lambda i, dst: (i, 0)
    grid_spec = pltpu.PrefetchScalarGridSpec(
        num_scalar_prefetch=1,
        grid=(n_tiles,),
        in_specs=[pl.BlockSpec((TM, d), row),
                  pl.BlockSpec((TM, routed.shape[1]), row),
                  pl.BlockSpec((8, d), lambda i, dst: (0, 5)),
                  pl.BlockSpec((1, d), lambda i, dst: (0, 0)),
                  pl.BlockSpec(memory_space=pl.ANY)],
        out_specs=pl.BlockSpec((TM, d), row),
        scratch_shapes=[pltpu.VMEM((MOE_TOPK, TM, d), f32), pltpu.SemaphoreType.DMA(())],
    )
    return pl.pallas_call(
        functools.partial(_combine_kernel, tiles_per_batch=seq // TM, batch=batch, final_norm=final_norm),
        grid_spec=grid_spec,
        out_shape=jax.ShapeDtypeStruct((n_tiles * TM, d), f32),
        compiler_params=_cparams(("arbitrary",)),
        name="moe_combine",
    )(dest, x, routed, mods, g_out, ys)


def _moe(xn, h2, logits, mods, w1, w3, w2, layer, g_out, *, n_tiles, batch, seq, final_norm):
    t = n_tiles * TM
    m = MOE_BLOCK
    routed, cnt = _route(logits)
    s = t * MOE_TOPK
    nb = s // m + MOE_EXPERTS
    expert = routed[:, 0:MOE_TOPK].astype(i32)
    rank = routed[:, 2 * MOE_TOPK:3 * MOE_TOPK].astype(i32)
    counts = cnt[0, :MOE_EXPERTS].astype(i32)
    padded = ((counts + m - 1) // m) * m
    pad_end = jnp.cumsum(padded)
    onehot = expert[:, :, None] == jnp.arange(MOE_EXPERTS, dtype=i32)[None, None, :]
    dest = (jnp.sum(jnp.where(onehot, (pad_end - padded)[None, None, :], 0), axis=-1) + rank).reshape(s)
    n_used = (pad_end[-1] // m).astype(i32)
    blk = jnp.arange(nb, dtype=i32)
    block_e = jnp.minimum(jnp.searchsorted(pad_end, jnp.minimum(blk, n_used - 1) * m, side='right'),
                          MOE_EXPERTS - 1).astype(i32)
    ids = jnp.arange(MOE_EXPERTS, dtype=i32)
    nonempty_from = lax.cummin(jnp.where(counts > 0, ids, MOE_EXPERTS), reverse=True)
    nxt = jnp.concatenate([nonempty_from[1:], jnp.full((1,), MOE_EXPERTS, i32)])
    next_expert = jnp.where(nxt < MOE_EXPERTS, nxt, -1).astype(i32)
    xs = _dispatch(h2, dest, nb)
    ys = _moe_experts(xs, block_e, n_used.reshape(1), next_expert, w1, w3, w2, layer, nb)
    return _combine(xn, ys, dest, routed, mods, g_out, n_tiles=n_tiles, batch=batch, seq=seq,
                    final_norm=final_norm)


HY_N2 = 128
HY_WB = 256


def _short_conv_kernel(x_ref, prev_ref, next_ref, w_ref, b_ref, v_ref, x1_ref, x2_ref, ext_scr, *, tiles_per_batch):
    j = pl.program_id(0) % tiles_per_batch
    tt = x_ref.shape[0]
    ext_scr[0:8, :] = jnp.where(j > 0, prev_ref[...], 0.0)
    ext_scr[8:8 + tt, :] = x_ref[...]
    ext_scr[8 + tt:16 + tt, :] = jnp.where(j < tiles_per_batch - 1, next_ref[...], 0.0)
    w = w_ref[...]
    width = v_ref.shape[1]
    for part, o_ref in enumerate((v_ref, x1_ref, x2_ref)):
        cols = slice(part * width, (part + 1) * width)
        o_ref[...] = (ext_scr[7:7 + tt, cols] * w[0:1, cols] + ext_scr[8:8 + tt, cols] * w[1:2, cols]
                      + ext_scr[9:9 + tt, cols] * w[2:3, cols] + b_ref[:, cols])


def _short_conv(hy, w, b, *, batch, seq):
    n_lat = batch * seq
    tt = 256
    width = hy.shape[1]
    out = jax.ShapeDtypeStruct((n_lat, width // 3), f32)
    o_spec = pl.BlockSpec((tt, width // 3), lambda i: (i, 0))
    return pl.pallas_call(
        functools.partial(_short_conv_kernel, tiles_per_batch=seq // tt),
        grid=(n_lat // tt,),
        in_specs=[pl.BlockSpec((tt, width), lambda i: (i, 0)),
                  pl.BlockSpec((8, width), lambda i: (jnp.maximum(i * (tt // 8) - 1, 0), 0)),
                  pl.BlockSpec((8, width), lambda i: ((i + 1) * (tt // 8), 0)),
                  pl.BlockSpec((3, width), lambda i: (0, 0)),
                  pl.BlockSpec((1, width), lambda i: (0, 0))],
        out_specs=[o_spec, o_spec, o_spec],
        out_shape=[out, out, out],
        scratch_shapes=[pltpu.VMEM((tt + 16, width), f32)],
        compiler_params=_cparams(("arbitrary",)),
        name="short_conv",
    )(hy, hy, hy, w, b.reshape(1, width))


def _hy_filter_kernel(z_ref, w1_ref, b1_ref, fr_ref, w2_ref, b2_ref, w3_ref, dl_ref, k_ref, nrm_ref):
    i = pl.program_id(0)
    z = z_ref[...]
    fr = fr_ref[...]
    h = jnp.sin(fr[0:1] * (jnp.dot(z.astype(bf16), w1_ref[...], preferred_element_type=f32) + b1_ref[...]))
    h = jnp.sin(fr[1:2] * (jnp.dot(h.astype(bf16), w2_ref[...], preferred_element_type=f32) + b2_ref[...]))
    kk = jnp.dot(h.astype(bf16), w3_ref[...], preferred_element_type=f32)
    width = dl_ref.shape[1]
    decay = jnp.exp(-z[:, 0:1] * jnp.abs(dl_ref[...]))
    tl = z.shape[0]
    row = i * tl + lax.broadcasted_iota(i32, (tl, 1), 0)
    tot = []
    for blk in range(kk.shape[1] // width):
        kb = kk[:, blk * width:(blk + 1) * width] * decay
        if blk % 2 == 1:
            kb = jnp.where(row == 0, 0.0, kb)
        k_ref[:, blk * width:(blk + 1) * width] = kb
        tot.append(jnp.sum(jnp.abs(kb), axis=0, keepdims=True))
    s = jnp.concatenate(tot, axis=1)

    @pl.when(i == 0)
    def _():
        nrm_ref[...] = s

    @pl.when(i > 0)
    def _():
        nrm_ref[...] = nrm_ref[...] + s


def _hy_filters(seq, w1, b1, freq, w2, b2, w3):
    t = jnp.linspace(0.0, 1.0, seq, dtype=f32)[:, None]
    bands = (HY_EMB - 1) // 2
    w = 2 * math.pi * jnp.arange(seq, dtype=f32)[:, None] / seq
    f = jnp.linspace(1e-4, bands - 1, bands, dtype=f32)[None, :]
    z = jnp.concatenate([t, jnp.cos(f * w), -jnp.sin(f * w), jnp.zeros((seq, 128 - HY_EMB), f32)], axis=-1)
    w1p = jnp.concatenate([w1, jnp.zeros((128 - HY_EMB, w1.shape[1]), f32)], axis=0).astype(bf16)
    hid = w1.shape[1]
    n_out = w3.shape[1]
    deltas = jnp.linspace(math.log(HY_TARGET) / HY_SLOW_DECAY, math.log(HY_TARGET) / HY_FAST_DECAY,
                          HY_WIDTH, dtype=f32).reshape(1, HY_WIDTH)
    tl = 512
    const = lambda a: pl.BlockSpec(a.shape, lambda i: (0,) * a.ndim)
    ins = [z, w1p, b1.reshape(1, hid), freq, w2.astype(bf16), b2.reshape(1, hid), w3.astype(bf16), deltas]
    return pl.pallas_call(
        _hy_filter_kernel,
        grid=(seq // tl,),
        in_specs=[pl.BlockSpec((tl, 128), lambda i: (i, 0))] + [const(a) for a in ins[1:]],
        out_specs=[pl.BlockSpec((tl, n_out), lambda i: (i, 0)), pl.BlockSpec((1, n_out), lambda i: (0, 0))],
        out_shape=[jax.ShapeDtypeStruct((seq, n_out), f32), jax.ShapeDtypeStruct((1, n_out), f32)],
        compiler_params=_cparams(("arbitrary",)),
        name="hy_filter",
    )(*ins)


def _dft_tables(seq):
    n2 = HY_N2
    n_tot = 2 * seq
    n1 = n_tot // n2
    nf = n1 // 2 + 1
    nfp = -(-nf // 8) * 8
    f1 = jnp.arange(nfp, dtype=i32)
    k1 = jnp.arange(n1 // 2, dtype=i32)
    th = (2 * math.pi / n1) * ((f1[:, None] * k1[None, :]) % n1).astype(f32)
    valid = (f1 < nf)[:, None]
    fwd = jnp.concatenate([jnp.where(valid, jnp.cos(th), 0.0), jnp.where(valid, -jnp.sin(th), 0.0)], axis=0)
    eye = jnp.eye(8, dtype=f32)
    fk = jnp.einsum('rk,jh->rjkh', fwd, eye).reshape(2 * nfp * 8, (n1 // 2) * 8)
    cw = jnp.where(valid, jnp.where((f1 == 0) | (f1 == n1 // 2), 1.0, 2.0)[:, None], 0.0) / n_tot
    inv = jnp.concatenate([(cw * jnp.cos(th)).T, (-cw * jnp.sin(th)).T], axis=1)
    ck = jnp.einsum('kr,jh->kjrh', inv, eye).reshape((n1 // 2) * 8, 2 * nfp * 8)
    fa = jnp.arange(nf, dtype=i32)[:, None, None]
    f2 = jnp.arange(n2, dtype=i32)[None, :, None]
    m2 = jnp.arange(n2, dtype=i32)[None, None, :]
    ph = (2 * math.pi / n_tot) * ((f2 * m2 * n1 + m2 * fa) % n_tot).astype(f32)
    gr, gi = jnp.cos(ph), -jnp.sin(ph)
    g = jnp.concatenate([jnp.concatenate([gr, -gi], axis=2), jnp.concatenate([gi, gr], axis=2)], axis=1)
    hr, hi = jnp.swapaxes(jnp.cos(ph), 1, 2), jnp.swapaxes(jnp.sin(ph), 1, 2)
    h = jnp.concatenate([jnp.concatenate([hr, -hi], axis=2), jnp.concatenate([hi, hr], axis=2)], axis=1)
    return fk.astype(bf16), ck.astype(bf16), g.astype(bf16), h.astype(bf16), nf, nfp


def _hy_rows_per_step(nf):
    return max(d for d in (5, 4, 3, 2, 1) if nf % d == 0)


def _dft_stage1(x_ref, fk_ref, a_scr):
    n1h, n2, wb = x_ref.shape

    def slab(s, carry):
        rows = pl.ds(pl.multiple_of(s * 8, 8), 8)
        xs = x_ref[:, rows, :].reshape(n1h * 8, wb).astype(bf16)
        a = jnp.dot(fk_ref[...], xs, preferred_element_type=f32)
        a_scr[:, rows, :] = a.reshape(a_scr.shape[0], 8, wb)
        return carry

    lax.fori_loop(0, n2 // 8, slab, 0)


def _dft_stage2(a_scr, g, f1, nfp):
    a = jnp.concatenate([a_scr[f1], a_scr[nfp + f1]], axis=0).astype(bf16)
    return jnp.dot(g, a, preferred_element_type=f32)


def _hy_spectrum_kernel(k_ref, fk_ref, g_ref, o_ref, a_scr, *, nfp):
    fs = pl.program_id(2)

    @pl.when(fs == 0)
    def _():
        _dft_stage1(k_ref, fk_ref, a_scr)

    fb = g_ref.shape[0]
    n2 = k_ref.shape[1]
    for j in range(fb):
        s = _dft_stage2(a_scr, g_ref[j], fs * fb + j, nfp)
        o_ref[0, j] = s[:n2]
        o_ref[1, j] = s[n2:]


def _hy_spectrum(k, fk, g, nf, nfp, seq):
    n2, wb = HY_N2, HY_WB
    n1h = seq // n2
    fb = _hy_rows_per_step(nf)
    nwb = HY_WIDTH // wb
    n_blk = k.shape[1] // HY_WIDTH
    k3 = k.reshape(n1h, n2, k.shape[1])
    return pl.pallas_call(
        functools.partial(_hy_spectrum_kernel, nfp=nfp),
        grid=(n_blk, nwb, nf // fb),
        in_specs=[pl.BlockSpec((n1h, n2, wb), lambda o, w, f: (0, 0, o * nwb + w), pipeline_mode=pl.Buffered(1)),
                  pl.BlockSpec(fk.shape, lambda o, w, f: (0, 0)),
                  pl.BlockSpec((fb, 2 * n2, 2 * n2), lambda o, w, f: (f, 0, 0))],
        out_specs=pl.BlockSpec((None, 2, fb, n2, wb), lambda o, w, f: (o, 0, f, 0, w)),
        out_shape=jax.ShapeDtypeStruct((n_blk, 2, nf, n2, HY_WIDTH), f32),
        scratch_shapes=[pltpu.VMEM((2 * nfp, n2, wb), f32)],
        compiler_params=_cparams(("arbitrary",) * 3),
        name="hy_spectrum",
    )(k3, fk, g)


def _hy_conv_kernel(x_ref, gate_ref, kf_ref, kb_ref, g_ref, h_ref, fk_ref, ck_ref, bias_ref, inv_ref, o_ref, a_scr, *,
                    nfp):
    fs = pl.program_id(2)
    n1h, n2, wb = x_ref.shape

    @pl.when(fs == 0)
    def _():
        _dft_stage1(x_ref, fk_ref, a_scr)

    fb = g_ref.shape[0]
    for j in range(fb):
        f1 = fs * fb + j
        xs = _dft_stage2(a_scr, g_ref[j], f1, nfp)
        xr, xi = xs[:n2], xs[n2:]
        kr = kf_ref[0, j] + kb_ref[0, j]
        ki = kf_ref[1, j] - kb_ref[1, j]
        y = jnp.concatenate([xr * kr - xi * ki, xr * ki + xi * kr], axis=0).astype(bf16)
        bb = jnp.dot(h_ref[j], y, preferred_element_type=f32)
        a_scr[f1] = bb[:n2]
        a_scr[nfp + f1] = bb[n2:]

    @pl.when(fs == pl.num_programs(2) - 1)
    def _():
        bias = bias_ref[...].reshape(1, 1, wb)
        inv = inv_ref[...].reshape(1, 1, wb)

        def slab(s, carry):
            rows = pl.ds(pl.multiple_of(s * 8, 8), 8)
            bs = a_scr[:, rows, :].reshape(2 * nfp * 8, wb).astype(bf16)
            y = jnp.dot(ck_ref[...], bs, preferred_element_type=f32).reshape(n1h, 8, wb)
            o_ref[:, rows, :] = (gate_ref[:, rows, :] * (y * inv + x_ref[:, rows, :] * bias)).astype(o_ref.dtype)
            return carry

        lax.fori_loop(0, n2 // 8, slab, 0)


def _hy_conv(x, gate, kf, order, tabs, bias, inv_norm, out_dtype, *, batch, seq):
    fk, ck, g, h, nf, nfp = tabs
    n2, wb = HY_N2, HY_WB
    n1h = seq // n2
    fb = _hy_rows_per_step(nf)
    width = x.shape[1]
    big = pl.BlockSpec((n1h, n2, wb), lambda b, w, f: (b, 0, w), pipeline_mode=pl.Buffered(1))
    out = pl.pallas_call(
        functools.partial(_hy_conv_kernel, nfp=nfp),
        grid=(batch, width // wb, nf // fb),
        in_specs=[big, big,
                  pl.BlockSpec((None, 2, fb, n2, wb), lambda b, w, f: (2 * order, 0, f, 0, w)),
                  pl.BlockSpec((None, 2, fb, n2, wb), lambda b, w, f: (2 * order + 1, 0, f, 0, w)),
                  pl.BlockSpec((fb, 2 * n2, 2 * n2), lambda b, w, f: (f, 0, 0)),
                  pl.BlockSpec((fb, 2 * n2, 2 * n2), lambda b, w, f: (f, 0, 0)),
                  pl.BlockSpec(fk.shape, lambda b, w, f: (0, 0)),
                  pl.BlockSpec(ck.shape, lambda b, w, f: (0, 0)),
                  pl.BlockSpec((1, wb), lambda b, w, f: (0, w)),
                  pl.BlockSpec((1, wb), lambda b, w, f: (0, w))],
        out_specs=big,
        out_shape=jax.ShapeDtypeStruct((batch * n1h, n2, width), out_dtype),
        scratch_shapes=[pltpu.VMEM((2 * nfp, n2, wb), f32)],
        compiler_params=_cparams(("arbitrary",) * 3),
        name="hy_conv",
    )(x.reshape(batch * n1h, n2, width), gate.reshape(batch * n1h, n2, width), kf, kf, g, h, fk, ck,
      bias.reshape(1, width), inv_norm.reshape(1, width))
    return out.reshape(batch * seq, width)


def _hyena(hy, conv_w, conv_b, f_w1, f_b1, f_freq, f_w2, f_b2, f_w3, hy_bias, *, batch, seq):
    v, x1, x2 = _short_conv(hy, conv_w, conv_b, batch=batch, seq=seq)
    k, sums = _hy_filters(seq, f_w1, f_b1, f_freq, f_w2, f_b2, f_w3)
    sums = sums.reshape(HY_ORDER, 2, HY_WIDTH)
    inv_norm = 1.0 / (sums[:, 0] + sums[:, 1])
    tabs = _dft_tables(seq)
    kf = _hy_spectrum(k, tabs[0], tabs[2], tabs[4], tabs[5], seq)
    z = _hy_conv(v, x1, kf, 0, tabs, hy_bias[0], inv_norm[0], f32, batch=batch, seq=seq)
    return _hy_conv(z, x2, kf, 1, tabs, hy_bias[1], inv_norm[1], bf16, batch=batch, seq=seq)


S5_TT = 256
S5_GB = 8


def _gelu_tanh(x):
    return 0.5 * x * (1.0 + jnp.tanh(0.7978845608028654 * (x + 0.044715 * x * x * x)))


def _s5_kernel(u_ref, perm_ref, permt_ref, wb_ref, wc_ref, are_ref, aim_ref, pre_ref, pim_ref, *rest, fuse_glu):
    if fuse_glu:
        yf_ref, d_ref, gw_ref, gb_ref, o_ref, st_scr, bu_scr, y_scr = rest
    else:
        o_ref, st_scr, bu_scr, y_scr = rest
    tt = u_ref.shape[0]
    steps = tt // 8
    half = bu_scr.shape[1] // 2

    @pl.when(pl.program_id(1) == 0)
    def _():
        st_scr[...] = jnp.zeros_like(st_scr)

    up = jnp.dot(perm_ref[...], u_ref[...].astype(bf16), preferred_element_type=f32).astype(bf16)
    for gb in range(wb_ref.shape[0]):
        bu_scr[...] = jnp.dot(up[:, gb * 128:(gb + 1) * 128], wb_ref[gb], preferred_element_type=f32)
        a_re = jnp.broadcast_to(are_ref[gb], (8, half))
        a_im = jnp.broadcast_to(aim_ref[gb], (8, half))

        def scan_step(k, h):
            hr, hi = h
            rows = pl.ds(k * 8, 8)
            nr = a_re * hr - a_im * hi + bu_scr[rows, 0:half]
            ni = a_re * hi + a_im * hr + bu_scr[rows, half:]
            bu_scr[rows, 0:half] = nr
            bu_scr[rows, half:] = ni
            return nr, ni

        zero = jnp.zeros((8, half), f32)
        h = (zero, zero)
        for k in range(steps):
            h = scan_step(k, h)
        hr, hi = h

        pw_r = pre_ref[gb, steps - 1:steps, :]
        pw_i = pim_ref[gb, steps - 1:steps, :]
        cur_r = st_scr[gb, :, 0:half]
        cur_i = st_scr[gb, :, half:]
        car_r, car_i = [], []
        for j in range(8):
            car_r.append(cur_r)
            car_i.append(cur_i)
            cur_r, cur_i = (hr[j:j + 1] + pw_r * cur_r - pw_i * cur_i,
                            hi[j:j + 1] + pw_r * cur_i + pw_i * cur_r)
        st_scr[gb, :, 0:half] = cur_r
        st_scr[gb, :, half:] = cur_i
        cr = jnp.concatenate(car_r, axis=0)
        ci = jnp.concatenate(car_i, axis=0)

        def fix_step(k, carry):
            rows = pl.ds(k * 8, 8)
            pr = pre_ref[gb, pl.ds(k, 1), :]
            pi = pim_ref[gb, pl.ds(k, 1), :]
            bu_scr[rows, 0:half] = bu_scr[rows, 0:half] + (pr * cr - pi * ci)
            bu_scr[rows, half:] = bu_scr[rows, half:] + (pr * ci + pi * cr)
            return carry

        for k in range(steps):
            fix_step(k, 0)
        y_scr[:, gb * 128:(gb + 1) * 128] = jnp.dot(bu_scr[...].astype(bf16), wc_ref[gb],
                                                      preferred_element_type=f32)
    yp = y_scr[...]
    y_hi = yp.astype(bf16)
    y_lo = (yp - y_hi.astype(f32)).astype(bf16)
    y = (jnp.dot(permt_ref[...], y_hi, preferred_element_type=f32)
         + jnp.dot(permt_ref[...], y_lo, preferred_element_type=f32))
    if fuse_glu:
        g = _gelu_tanh(y + yf_ref[...] + u_ref[...] * d_ref[...])
        z = jnp.dot(g.astype(bf16), gw_ref[...], preferred_element_type=f32) + gb_ref[...]
        o_ref[...] = (g * jax.nn.sigmoid(z)).astype(o_ref.dtype)
    else:
        o_ref[...] = y


def _s5_tables(a_re, a_im, log_dt, b_re, b_im, c_re, c_im, steps):
    ng, ns = a_re.shape
    nb = ng // S5_GB
    lam = lax.complex(a_re, a_im)
    dt = jnp.exp(log_dt)[:, None]
    lam_bar = jnp.exp(lam * dt)
    b_bar = ((lam_bar - 1) / lam)[..., None] * lax.complex(b_re, b_im)
    cm = lax.complex(c_re, c_im)
    pw = jnp.exp((lam * dt)[None] * jnp.arange(1, steps + 1, dtype=f32)[:, None, None])
    eye = jnp.eye(S5_GB, dtype=f32)
    bb = b_bar.reshape(nb, S5_GB, ns, S5_GROUP)
    wb = jnp.concatenate([jnp.einsum('bgpc,gh->bgchp', part, eye).reshape(nb, S5_GB * S5_GROUP, S5_GB * ns)
                          for part in (jnp.real(bb), jnp.imag(bb))], axis=-1)
    cc = cm.reshape(nb, S5_GB, S5_GROUP, ns)
    wc = jnp.concatenate([jnp.einsum('bgcp,gh->bgphc', part, eye).reshape(nb, S5_GB * ns, S5_GB * S5_GROUP)
                          for part in (jnp.real(cc), -jnp.imag(cc))], axis=1)
    lb = lam_bar.reshape(nb, 1, S5_GB * ns)
    pwb = pw.reshape(steps, nb, S5_GB * ns).transpose(1, 0, 2)
    return (wb.astype(bf16), wc.astype(bf16), jnp.real(lb), jnp.imag(lb), jnp.real(pwb), jnp.imag(pwb))


def _s5_perm(tt, reverse):
    steps = tt // 8
    dst = jnp.arange(tt)
    t = (dst % 8) * steps + dst // 8
    src = tt - 1 - t if reverse else t
    return jnp.arange(tt)[None, :] == src[:, None]


def _s5(u, a_re, a_im, log_dt, b_re, b_im, c_re, c_im, s5_d, glu_w, glu_b, *, batch, seq, ctx_len):
    r, width = u.shape
    tt = S5_TT
    assert ctx_len == tt and seq % tt == 0
    n_lat = seq // tt
    ctx_blk0 = (batch * seq) // tt
    y = None
    for di in range(2):
        rev = di == 1
        wb, wc, lr, li, pr, pi = _s5_tables(a_re[di], a_im[di], log_dt[di], b_re[di], b_im[di],
                                            c_re[di], c_im[di], tt // 8)
        perm = _s5_perm(tt, rev)

        def tile_map(b, i, rev=rev):
            lat = b * n_lat + (n_lat - i if rev else i - 1)
            return (jnp.where(i == 0, ctx_blk0 + b, lat), 0)

        row_spec = pl.BlockSpec((tt, width), tile_map)
        const = lambda a: pl.BlockSpec(a.shape, lambda b, i: (0,) * a.ndim)
        ins = [u, perm.astype(bf16), perm.T.astype(bf16), wb, wc, lr, li, pr, pi]
        in_specs = [row_spec] + [const(a) for a in ins[1:]]
        if rev:
            extra = [y, s5_d.reshape(1, width), glu_w.astype(bf16), glu_b.reshape(1, width)]
            in_specs += [row_spec] + [const(a) for a in extra[1:]]
            ins += extra
        y = pl.pallas_call(
            functools.partial(_s5_kernel, fuse_glu=rev),
            grid=(batch, n_lat + 1),
            in_specs=in_specs,
            out_specs=row_spec,
            out_shape=jax.ShapeDtypeStruct((r, width), bf16 if rev else f32),
            scratch_shapes=[pltpu.VMEM((wb.shape[0], 1, wb.shape[2]), f32),
                            pltpu.VMEM((tt, wb.shape[2]), f32),
                            pltpu.VMEM((tt, width), f32)],
            compiler_params=_cparams(("arbitrary", "arbitrary")),
            name="s5_bwd_glu" if rev else "s5_fwd",
        )(*ins)
    return y


def _rope_tables(seq):
    rows = seq // GRID_W
    row = jnp.repeat(jnp.arange(rows, dtype=f32), GRID_W)
    col = jnp.tile(jnp.arange(GRID_W, dtype=f32), rows)
    n_freq = DIFF_QK // 4
    inv = ROPE_BASE ** (-jnp.arange(n_freq, dtype=f32) / n_freq)
    ar, ac = row[:, None] * inv, col[:, None] * inv
    cos = jnp.concatenate([jnp.cos(ar), jnp.cos(ar), jnp.cos(ac), jnp.cos(ac)], axis=-1)
    sin = jnp.concatenate([-jnp.sin(ar), jnp.sin(ar), -jnp.sin(ac), jnp.sin(ac)], axis=-1)
    cos = jnp.concatenate([jnp.tile(cos, (1, 2)), jnp.ones((TM, 128), f32)], axis=0)
    sin = jnp.concatenate([jnp.tile(sin, (1, 2)), jnp.zeros((TM, 128), f32)], axis=0)
    return cos, sin


def kernel(x, c, ctx, c_ctx, ada_w, ada_b, norm_mix_g, norm_ffn_g, norm_out_g, ab_w_in, ab_w_out, diff_lam, diff_subln_g, pool_w, pool_scale, cd_w_in, cd_w_out, hy_conv_w, hy_conv_b, hy_f_w1, hy_f_b1, hy_f_freq, hy_f_w2, hy_f_b2, hy_f_w3, hy_bias, s5_a_re, s5_a_im, s5_log_dt, s5_b_re, s5_b_im, s5_c_re, s5_c_im, s5_d, glu_w, glu_b, moe_w_rg, moe_b_rg, moe_w_re, moe_b_re, moe_w1, moe_w3, moe_w2):
    bsz, seq, d = x.shape
    cl = ctx.shape[1]
    depth = ada_w.shape[0]
    n_lat = bsz * seq
    assert seq % TM == 0 and (bsz * cl) % TM == 0 and bsz + 1 <= 8
    xs = jnp.concatenate([x.reshape(n_lat, d), ctx.reshape(bsz * cl, d)], axis=0)
    r = xs.shape[0]
    cond = jnp.zeros((8, d), f32).at[0:bsz].set(c).at[bsz].set(c_ctx)
    mods = _ada_mods(cond, ada_w, ada_b)
    geom = dict(batch=bsz, seq=seq)

    for l in range(depth):
        last = l == depth - 1
        i = l // 2
        n_tiles = (n_lat if last else r) // TM
        g_mix = norm_mix_g[l].reshape(1, d)
        if l % 2 == 0:
            lam_init = 0.8 - 0.6 * math.exp(-0.3 * l)
            cos, sin = _rope_tables(seq)
            q, k, vt, p = _proj_rope(xs, g_mix, mods[l], ab_w_in[i].astype(bf16), cos, sin, **geom)
            mix_a = _diff_attention(q, k, vt, diff_lam[i], diff_subln_g[i], lam_init, ctx_len=cl, **geom)
            mix_b = _pool(p, pool_w[i], pool_scale[i], ctx_len=cl, **geom)
            w_out = ab_w_out[i]
        else:
            hw = (HY_ORDER + 1) * HY_WIDTH
            hy, u = _proj(xs, g_mix, mods[l], cd_w_in[i].astype(bf16), hw, **geom)
            mix_a = _hyena(hy, hy_conv_w[i], hy_conv_b[i], hy_f_w1[i], hy_f_b1[i], hy_f_freq[i],
                           hy_f_w2[i], hy_f_b2[i], hy_f_w3[i], hy_bias[i], **geom)
            mix_b = _s5(u, s5_a_re[i], s5_a_im[i], s5_log_dt[i], s5_b_re[i], s5_b_im[i],
                        s5_c_re[i], s5_c_im[i], s5_d[i], glu_w[i], glu_b[i], ctx_len=cl, **geom)
            w_out = cd_w_out[i]
        w_r = jnp.zeros((d, 128), f32).at[:, :MOE_GROUPS].set(moe_w_rg[l]) \
            .at[:, MOE_GROUPS:MOE_GROUPS + MOE_EXPERTS].set(moe_w_re[l]).astype(bf16)
        b_r = jnp.zeros((1, 128), f32).at[0, :MOE_GROUPS].set(moe_b_rg[l]) \
            .at[0, MOE_GROUPS:MOE_GROUPS + MOE_EXPERTS].set(moe_b_re[l])
        xn, h2, logits = _outproj(mix_a, mix_b, xs, w_out.astype(bf16), mods[l], norm_ffn_g[l].reshape(1, d),
                                  w_r, b_r, n_tiles=n_tiles, **geom)
        xs = _moe(xn, h2, logits, mods[l], moe_w1, moe_w3, moe_w2, l, norm_out_g.reshape(1, d),
                  n_tiles=n_tiles, final_norm=last, **geom)
    return xs[:n_lat].reshape(bsz, seq, d)
```

```python
import functools
import math

import jax
import jax.numpy as jnp
from jax import lax
from jax.experimental import pallas as pl
from jax.experimental.pallas import tpu as pltpu

f32, bf16, i32 = jnp.float32, jnp.bfloat16, jnp.int32

EPS = 1e-6
GRID_W = 64
DIFF_QK = 64
DIFF_V = 128
ROPE_BASE = 10000.0
POOL_WINDOWS = (2, 4, 8, 16)
POOL_GROUP = 256
HY_WIDTH = 1024
HY_ORDER = 2
HY_EMB = 33
HY_FAST_DECAY = 0.3
HY_SLOW_DECAY = 1.5
HY_TARGET = 1e-2
S5_GROUP = 16
S5_GROUPS = 64
S5_STATE = 64
MOE_GROUPS = 8
MOE_PER_GROUP = 8
MOE_EXPERTS = 64
MOE_TOPK = 2
MOE_BLOCK = 256

VMEM_LIMIT = 56 * 1024 * 1024
TM = 512
LOG2E = 1.4426950408889634


def _cparams(sem):
    return pltpu.CompilerParams(dimension_semantics=sem, vmem_limit_bytes=VMEM_LIMIT)


def _norm_mod(x, g, shift, scale):
    ms = jnp.mean(x * x, axis=-1, keepdims=True)
    return (x * lax.rsqrt(ms + EPS) * g) * (1.0 + scale) + shift


def _tile_segment(i, lat_tiles_per_batch, batch):
    seg = jnp.zeros((), i32)
    for b in range(1, batch + 1):
        seg = seg + (i >= b * lat_tiles_per_batch).astype(i32)
    return seg


def _pack_rows(x):
    half = x.shape[1] // 2
    bits = lax.bitcast_convert_type(x.astype(bf16).astype(f32), jnp.uint32)
    return (bits[:, :half] >> 16) | (bits[:, half:] & jnp.uint32(0xFFFF0000))


def _unpack_rows(words):
    return (lax.bitcast_convert_type(words << 16, f32),
            lax.bitcast_convert_type(words & jnp.uint32(0xFFFF0000), f32))


def _ada_kernel(s_ref, w_ref, b_ref, o_ref):
    s = s_ref[...]
    a = (s * jax.nn.sigmoid(s)).astype(bf16)
    o_ref[...] = jnp.dot(a, w_ref[...].astype(bf16), preferred_element_type=f32) + b_ref[...]


def _ada_mods(s, ada_w, ada_b):
    depth, d, n = ada_w.shape
    tn = 1024
    return pl.pallas_call(
        _ada_kernel,
        grid=(depth, n // tn),
        in_specs=[pl.BlockSpec((8, d), lambda l, j: (0, 0)),
                  pl.BlockSpec((None, d, tn), lambda l, j: (l, 0, j)),
                  pl.BlockSpec((None, 1, tn), lambda l, j: (l, 0, j))],
        out_specs=pl.BlockSpec((None, 8, tn), lambda l, j: (l, 0, j)),
        out_shape=jax.ShapeDtypeStruct((depth, 8, n), f32),
        compiler_params=_cparams(("arbitrary", "arbitrary")),
        name="ada_mods",
    )(s, ada_w, ada_b.reshape(depth, 1, n))


def _proj_rope_kernel(x_ref, g_ref, sh_ref, sc_ref, w_ref, cos_ref, sin_ref,
                      q_ref, k_ref, v_ref, p_ref, h_scr, *, tiles_per_batch, batch, qscale):
    seg = _tile_segment(pl.program_id(0), tiles_per_batch, batch)
    h = _norm_mod(x_ref[...], g_ref[...], sh_ref[pl.ds(seg, 1), :], sc_ref[pl.ds(seg, 1), :])
    h_scr[...] = h.astype(bf16)
    cos = cos_ref[...]
    sin = sin_ref[...]
    tm = cos.shape[0]
    lane = lax.broadcasted_iota(i32, (tm, 128), 1)
    first = (lane % 32) < 16

    def rope(r):
        sw = jnp.where(first, pltpu.roll(r, 112, 1), pltpu.roll(r, 16, 1))
        return r * cos + sw * sin

    heads = q_ref.shape[0]
    nc = 512
    for c in range(w_ref.shape[1] // nc):
        res = jnp.dot(h_scr[...], w_ref[:, c * nc:(c + 1) * nc], preferred_element_type=f32)
        for s in range(nc // 128):
            r = res[:, s * 128:(s + 1) * 128]
            col = (c * nc) // 128 + s
            if col < heads:
                q_ref[col] = (rope(r) * qscale).astype(bf16)
            elif col < 2 * heads:
                k_ref[col - heads] = rope(r).astype(bf16)
            elif col < 3 * heads:
                v_ref[col - 2 * heads] = r.T.astype(bf16)
            else:
                c0 = (col - 3 * heads) * 128
                p_ref[:, c0:c0 + 128] = r


def _proj_rope(x, g, mods, w, cos, sin, *, batch, seq):
    r, d = x.shape
    heads = 8
    lat_tiles = seq // TM
    n_tiles = r // TM

    def tab_map(i):
        return (jnp.where(i < batch * lat_tiles, i % lat_tiles, lat_tiles), 0)

    hm = jax.ShapeDtypeStruct((heads, r, 128), bf16)
    hm_spec = pl.BlockSpec((heads, TM, 128), lambda i: (0, i, 0))
    return pl.pallas_call(
        functools.partial(_proj_rope_kernel, tiles_per_batch=lat_tiles, batch=batch,
                          qscale=LOG2E * DIFF_QK ** -0.5),
        grid=(n_tiles,),
        in_specs=[pl.BlockSpec((TM, d), lambda i: (i, 0)),
                  pl.BlockSpec((1, d), lambda i: (0, 0)),
                  pl.BlockSpec((8, d), lambda i: (0, 0)),
                  pl.BlockSpec((8, d), lambda i: (0, 1)),
                  pl.BlockSpec(w.shape, lambda i: (0, 0), pipeline_mode=pl.Buffered(1)),
                  pl.BlockSpec((TM, 128), tab_map),
                  pl.BlockSpec((TM, 128), tab_map)],
        out_specs=[hm_spec, hm_spec, pl.BlockSpec((heads, 128, TM), lambda i: (0, 0, i)),
                   pl.BlockSpec((TM, 1024), lambda i: (i, 0))],
        out_shape=[hm, hm, jax.ShapeDtypeStruct((heads, 128, r), bf16), jax.ShapeDtypeStruct((r, 1024), f32)],
        scratch_shapes=[pltpu.VMEM((TM, d), bf16)],
        compiler_params=_cparams(("arbitrary",)),
        name="proj_rope",
    )(x, g, mods, mods, w, cos, sin)


def _proj_kernel(x_ref, g_ref, sh_ref, sc_ref, w_ref, hy_ref, u_ref, h_scr, *, tiles_per_batch, batch):
    seg = _tile_segment(pl.program_id(0), tiles_per_batch, batch)
    h = _norm_mod(x_ref[...], g_ref[...], sh_ref[pl.ds(seg, 1), :], sc_ref[pl.ds(seg, 1), :])
    h_scr[...] = h.astype(bf16)
    nc = 512
    n_hy = hy_ref.shape[1]
    for c in range(w_ref.shape[1] // nc):
        res = jnp.dot(h_scr[...], w_ref[:, c * nc:(c + 1) * nc], preferred_element_type=f32)
        if c * nc < n_hy:
            hy_ref[:, c * nc:(c + 1) * nc] = res
        else:
            u_ref[:, c * nc - n_hy:(c + 1) * nc - n_hy] = res


def _proj(x, g, mods, w, n_hy, *, batch, seq):
    r, d = x.shape
    n = w.shape[1]
    return pl.pallas_call(
        functools.partial(_proj_kernel, tiles_per_batch=seq // TM, batch=batch),
        grid=(r // TM,),
        in_specs=[pl.BlockSpec((TM, d), lambda i: (i, 0)),
                  pl.BlockSpec((1, d), lambda i: (0, 0)),
                  pl.BlockSpec((8, d), lambda i: (0, 0)),
                  pl.BlockSpec((8, d), lambda i: (0, 1)),
                  pl.BlockSpec(w.shape, lambda i: (0, 0), pipeline_mode=pl.Buffered(1))],
        out_specs=[pl.BlockSpec((TM, n_hy), lambda i: (i, 0)), pl.BlockSpec((TM, n - n_hy), lambda i: (i, 0))],
        out_shape=[jax.ShapeDtypeStruct((r, n_hy), f32), jax.ShapeDtypeStruct((r, n - n_hy), f32)],
        scratch_shapes=[pltpu.VMEM((TM, d), bf16)],
        compiler_params=_cparams(("arbitrary",)),
        name="proj",
    )(x, g, mods, mods, w)


def _attn_kernel(lv_ref, g_ref, q_ref, kc_ref, vc_ref, *rest, n_lat, tk, lam_init):
    if n_lat:
        kl_ref, vl_ref, o_ref, s_scr = rest
    else:
        o_ref, s_scr = rest
    lv = lv_ref[...]
    lam = (jnp.exp(jnp.sum(lv[0:1] * lv[1:2], axis=-1, keepdims=True))
           - jnp.exp(jnp.sum(lv[2:3] * lv[3:4], axis=-1, keepdims=True)) + lam_init)
    q = q_ref[...]
    tq = q.shape[0]
    lane = lax.broadcasted_iota(i32, q.shape, 1)
    zero = jnp.zeros_like(q)
    qs = (jnp.where(lane < DIFF_QK, q, zero), jnp.where(lane >= DIFF_QK, q, zero))
    nt = (((1,), (1,)), ((), ()))

    def scores(slot, k):
        for m in range(2):
            s_scr[slot, m, 0:k.shape[0], :] = lax.dot_general(k, qs[m], nt, preferred_element_type=f32)

    def update(carry, slot, nk, vt):
        new = []
        for m in range(2):
            mx, l, acc = carry[3 * m:3 * m + 3]
            s = s_scr[slot, m, 0:nk, :]
            mn = jnp.maximum(mx, jnp.max(s, axis=0, keepdims=True))
            alpha = jnp.exp2(mx - mn)
            p = jnp.exp2(s - mn)
            l = alpha * l + jnp.sum(p, axis=0, keepdims=True)
            acc = alpha * acc + jnp.dot(vt, p.astype(bf16), preferred_element_type=f32)
            new += [mn, l, acc]
        return tuple(new)

    def k_block(j):
        return kl_ref[pl.ds(pl.multiple_of(j * tk, tk), tk), :]

    def vt_block(j):
        return vl_ref[:, pl.ds(pl.multiple_of(j * tk, tk), tk)]

    init = (jnp.full((1, tq), -jnp.inf, f32), jnp.zeros((1, tq), f32), jnp.zeros((DIFF_V, tq), f32)) * 2
    nc = kc_ref.shape[0]
    scores(1, kc_ref[...])
    if n_lat:
        scores(0, k_block(0))
        carry = update(init, 1, nc, vc_ref[...])

        def body(jj, carry):
            j = 4 * jj
            scores(1, k_block(j + 1))
            carry = update(carry, 0, tk, vt_block(j))
            scores(0, k_block(j + 2))
            carry = update(carry, 1, tk, vt_block(j + 1))
            scores(1, k_block(j + 3))
            carry = update(carry, 0, tk, vt_block(j + 2))
            scores(0, k_block(jnp.minimum(j + 4, n_lat - 1)))
            return update(carry, 1, tk, vt_block(j + 3))

        carry = lax.fori_loop(0, n_lat // 4, body, carry)
    else:
        carry = update(init, 1, nc, vc_ref[...])
    _, l0, a0, _, l1, a1 = carry
    o = a0 / l0 - lam * (a1 / l1)
    ms = jnp.mean(o * o, axis=0, keepdims=True)
    y = (o * lax.rsqrt(ms + EPS) * g_ref[...]) * (1.0 - lam_init)
    o_ref[...] = y.T.astype(o_ref.dtype)


def _diff_attention(q, k, vt, lam_vecs, subln_g, lam_init, *, batch, seq, ctx_len):
    heads, r, _ = q.shape
    tq, tk = 512, 512
    ctx_blk0 = (batch * seq) // ctx_len
    assert (seq // tk) % 4 == 0
    small = [pl.BlockSpec((4, DIFF_QK), lambda *_: (0, 0)), pl.BlockSpec((DIFF_V, 1), lambda *_: (0, 0))]
    kc_spec = pl.BlockSpec((None, ctx_len, 128), lambda b, h, *_: (h, ctx_blk0 + b, 0))
    vc_spec = pl.BlockSpec((None, 128, ctx_len), lambda b, h, *_: (h, 0, ctx_blk0 + b))
    g2 = subln_g.reshape(DIFF_V, 1)
    lat = pl.pallas_call(
        functools.partial(_attn_kernel, n_lat=seq // tk, tk=tk, lam_init=lam_init),
        grid=(batch, heads, seq // tq),
        in_specs=small + [pl.BlockSpec((None, tq, 128), lambda b, h, i: (h, b * (seq // tq) + i, 0)),
                          kc_spec, vc_spec,
                          pl.BlockSpec((None, seq, 128), lambda b, h, i: (h, b, 0)),
                          pl.BlockSpec((None, 128, seq), lambda b, h, i: (h, 0, b))],
        out_specs=pl.BlockSpec((tq, 128), lambda b, h, i: (b * (seq // tq) + i, h)),
        out_shape=jax.ShapeDtypeStruct((batch * seq, heads * DIFF_V), bf16),
        scratch_shapes=[pltpu.VMEM((2, 2, tk, tq), f32)],
        compiler_params=_cparams(("arbitrary",) * 3),
        name="diff_attn_lat",
    )(lam_vecs, g2, q, k, vt, k, vt)
    ctx = pl.pallas_call(
        functools.partial(_attn_kernel, n_lat=0, tk=tk, lam_init=lam_init),
        grid=(batch, heads),
        in_specs=small + [kc_spec, kc_spec, vc_spec],
        out_specs=pl.BlockSpec((ctx_len, 128), lambda b, h: (b, h)),
        out_shape=jax.ShapeDtypeStruct((batch * ctx_len, heads * DIFF_V), bf16),
        scratch_shapes=[pltpu.VMEM((2, 2, ctx_len, ctx_len), f32)],
        compiler_params=_cparams(("arbitrary",) * 2),
        name="diff_attn_ctx",
    )(lam_vecs, g2, q, k, vt)
    return jnp.concatenate([lat, ctx], axis=0)


def _pool_kernel(p_ref, prev_ref, next_ref, w_ref, scale_ref, o_ref, ext_scr, *,
                 tiles_per_batch, batch, seq, ctx_len):
    i = pl.program_id(0)
    tp = p_ref.shape[0]
    is_ctx = i >= batch * tiles_per_batch
    j = jnp.where(is_ctx, 0, i % tiles_per_batch)
    seg_len = jnp.where(is_ctx, ctx_len, seq)
    last = jnp.where(is_ctx, 0, tiles_per_batch - 1)
    ext_scr[0:8, :] = jnp.where(j > 0, prev_ref[...], 0.0)
    ext_scr[8:8 + tp, :] = p_ref[...]
    ext_scr[8 + tp:16 + tp, :] = jnp.where(j < last, next_ref[...], 0.0)
    pos = j * tp + lax.broadcasted_iota(i32, (tp, 1), 0)
    for gi, w in enumerate(POOL_WINDOWS):
        c0 = gi * POOL_GROUP
        half = w // 2
        tot = ext_scr[8 - half:8 - half + tp, c0:c0 + POOL_GROUP]
        for dlt in range(1 - half, half):
            tot = tot + ext_scr[8 + dlt:8 + dlt + tp, c0:c0 + POOL_GROUP]
        cnt = (jnp.minimum(pos - half + w, seg_len) - jnp.maximum(pos - half, 0)).astype(f32)
        dlt_mean = (tot / cnt - ext_scr[8:8 + tp, c0:c0 + POOL_GROUP]).astype(bf16)
        y = jnp.dot(dlt_mean, w_ref[gi].astype(bf16), preferred_element_type=f32)
        o_ref[:, c0:c0 + POOL_GROUP] = (y * scale_ref[:, c0:c0 + POOL_GROUP]).astype(o_ref.dtype)


def _pool(p, pool_w, pool_scale, *, batch, seq, ctx_len):
    r, width = p.shape
    tp = ctx_len
    tpb = seq // tp
    return pl.pallas_call(
        functools.partial(_pool_kernel, tiles_per_batch=tpb, batch=batch, seq=seq, ctx_len=ctx_len),
        grid=(r // tp,),
        in_specs=[pl.BlockSpec((tp, width), lambda i: (i, 0)),
                  pl.BlockSpec((8, width), lambda i: (jnp.maximum(i * (tp // 8) - 1, 0), 0)),
                  pl.BlockSpec((8, width), lambda i: (jnp.minimum((i + 1) * (tp // 8), r // 8 - 1), 0)),
                  pl.BlockSpec(pool_w.shape, lambda i: (0, 0, 0)),
                  pl.BlockSpec((1, width), lambda i: (0, 0))],
        out_specs=pl.BlockSpec((tp, width), lambda i: (i, 0)),
        out_shape=jax.ShapeDtypeStruct((r, width), bf16),
        scratch_shapes=[pltpu.VMEM((tp + 16, width), f32)],
        compiler_params=_cparams(("arbitrary",)),
        name="pool",
    )(p, p, p, pool_w, pool_scale.reshape(1, width))


def _outproj_kernel(a_ref, b_ref, x_ref, w_ref, g1_ref, sh_ref, sc_ref, gn_ref, wr_ref, br_ref,
                    xn_ref, h2_ref, lg_ref, *, tiles_per_batch, batch):
    seg = _tile_segment(pl.program_id(0), tiles_per_batch, batch)
    half = a_ref.shape[1]
    y = (jnp.dot(a_ref[...], w_ref[0:half, :], preferred_element_type=f32)
         + jnp.dot(b_ref[...], w_ref[half:, :], preferred_element_type=f32))
    xn = x_ref[...] + g1_ref[pl.ds(seg, 1), :] * y
    xn_ref[...] = xn
    h2 = _norm_mod(xn, gn_ref[...], sh_ref[pl.ds(seg, 1), :], sc_ref[pl.ds(seg, 1), :])
    h2_ref[...] = _pack_rows(h2)
    lg_ref[...] = jnp.dot(h2.astype(bf16), wr_ref[...], preferred_element_type=f32) + br_ref[...]


def _outproj(a, b, x, w, mods, g_ffn, w_r, b_r, *, n_tiles, batch, seq):
    r, d = n_tiles * TM, x.shape[1]
    half = a.shape[1]
    row = lambda i: (i, 0)
    return pl.pallas_call(
        functools.partial(_outproj_kernel, tiles_per_batch=seq // TM, batch=batch),
        grid=(n_tiles,),
        in_specs=[pl.BlockSpec((TM, half), row), pl.BlockSpec((TM, half), row), pl.BlockSpec((TM, d), row),
                  pl.BlockSpec(w.shape, lambda i: (0, 0), pipeline_mode=pl.Buffered(1)),
                  pl.BlockSpec((8, d), lambda i: (0, 2)),
                  pl.BlockSpec((8, d), lambda i: (0, 3)),
                  pl.BlockSpec((8, d), lambda i: (0, 4)),
                  pl.BlockSpec((1, d), lambda i: (0, 0)),
                  pl.BlockSpec(w_r.shape, lambda i: (0, 0)),
                  pl.BlockSpec((1, 128), lambda i: (0, 0))],
        out_specs=[pl.BlockSpec((TM, d), row), pl.BlockSpec((TM, d // 2), row), pl.BlockSpec((TM, 128), row)],
        out_shape=[jax.ShapeDtypeStruct((r, d), f32), jax.ShapeDtypeStruct((r, d // 2), jnp.uint32),
                   jax.ShapeDtypeStruct((r, 128), f32)],
        compiler_params=_cparams(("arbitrary",)),
        name="outproj",
    )(a, b, x, w, mods, mods, mods, g_ffn, w_r, b_r)


def _moe_kernel(be_ref, nu_ref, nxt_ref, xs_ref, w1_hbm, w3_hbm, w2_hbm, ys_ref,
                w1_f32, w3_f32, w2_f32, w1_scr, w3_scr, w2_scr, sems, *, layer):
    i = pl.program_id(0)
    e = be_ref[i]
    prev = be_ref[jnp.maximum(i - 1, 0)]
    used = i < nu_ref[0]

    def weight_copies(expert):
        return (pltpu.make_async_copy(w1_hbm.at[layer, expert], w1_f32, sems.at[0]),
                pltpu.make_async_copy(w3_hbm.at[layer, expert], w3_f32, sems.at[1]),
                pltpu.make_async_copy(w2_hbm.at[layer, expert], w2_f32, sems.at[2]))

    @pl.when(i == 0)
    def _():
        for cp in weight_copies(e):
            cp.start()

    @pl.when(used & ((i == 0) | (e != prev)))
    def _():
        for cp in weight_copies(e):
            cp.wait()
        w1_scr[...] = w1_f32[...].astype(bf16)
        w3_scr[...] = w3_f32[...].astype(bf16)
        w2_scr[...] = w2_f32[...].astype(bf16)
        nxt = nxt_ref[e]

        @pl.when(nxt >= 0)
        def _():
            for cp in weight_copies(nxt):
                cp.start()

    @pl.when(used)
    def _():
        lo, hi = _unpack_rows(xs_ref[...])
        half = lo.shape[1]
        lo, hi = lo.astype(bf16), hi.astype(bf16)
        a = (jnp.dot(lo, w1_scr[0:half, :], preferred_element_type=f32)
             + jnp.dot(hi, w1_scr[half:, :], preferred_element_type=f32))
        b = (jnp.dot(lo, w3_scr[0:half, :], preferred_element_type=f32)
             + jnp.dot(hi, w3_scr[half:, :], preferred_element_type=f32))
        hid = (a * jax.nn.sigmoid(a) * b).astype(bf16)
        ys_ref[...] = _pack_rows(jnp.dot(hid, w2_scr[...], preferred_element_type=f32))

    @pl.when(jnp.logical_not(used))
    def _():
        ys_ref[...] = jnp.zeros_like(ys_ref)


def _moe_experts(xs, block_e, n_used, next_expert, w1, w3, w2, layer, nb):
    m = MOE_BLOCK
    _, _, d, de = w1.shape
    hbm = pl.BlockSpec(memory_space=pl.ANY)
    grid_spec = pltpu.PrefetchScalarGridSpec(
        num_scalar_prefetch=3,
        grid=(nb,),
        in_specs=[pl.BlockSpec((m, d // 2), lambda i, be, nu, nx: (jnp.minimum(i, nu[0] - 1), 0)), hbm, hbm, hbm],
        out_specs=pl.BlockSpec((m, d // 2), lambda i, be, nu, nx: (i, 0)),
        scratch_shapes=[pltpu.VMEM((d, de), f32), pltpu.VMEM((d, de), f32), pltpu.VMEM((de, d), f32),
                        pltpu.VMEM((d, de), bf16), pltpu.VMEM((d, de), bf16), pltpu.VMEM((de, d), bf16),
                        pltpu.SemaphoreType.DMA((3,))],
    )
    return pl.pallas_call(
        functools.partial(_moe_kernel, layer=layer),
        grid_spec=grid_spec,
        out_shape=jax.ShapeDtypeStruct((nb * m, d // 2), jnp.uint32),
        compiler_params=_cparams(("arbitrary",)),
        name="moe_experts",
    )(block_e, n_used, next_expert, xs, w1, w3, w2)


def _route_kernel(lg_ref, tri_ref, o_ref, cnt_ref, run_scr):
    @pl.when(pl.program_id(0) == 0)
    def _():
        run_scr[...] = jnp.zeros_like(run_scr)

    lg = lg_ref[...]
    lane = lax.broadcasted_iota(i32, lg.shape, 1)
    ninf = -jnp.inf

    def first_lane(mask):
        return jnp.min(jnp.where(mask, lane, lg.shape[1]), axis=-1, keepdims=True)

    gl = jnp.where(lane < MOE_GROUPS, lg, ninf)
    gmax = jnp.max(gl, axis=-1, keepdims=True)
    pg_top = 1.0 / jnp.sum(jnp.exp(gl - gmax), axis=-1, keepdims=True)
    lo = MOE_GROUPS + first_lane(gl == gmax) * MOE_PER_GROUP
    sel = (lane >= lo) & (lane < lo + MOE_PER_GROUP)
    el = jnp.where(sel, lg, ninf)
    ee = jnp.exp(el - jnp.max(el, axis=-1, keepdims=True))
    pe = jnp.where(sel, ee / jnp.sum(ee, axis=-1, keepdims=True), -1.0)
    p1 = jnp.max(pe, axis=-1, keepdims=True)
    i1 = first_lane(pe == p1)
    pe2 = jnp.where(lane == i1, -1.0, pe)
    p2 = jnp.max(pe2, axis=-1, keepdims=True)
    i2 = first_lane(pe2 == p2)
    den = p1 + p2
    e1 = i1 - MOE_GROUPS
    e2 = i2 - MOE_GROUPS
    hit1 = lane == e1
    hit2 = lane == e2
    onehot = jnp.where(hit1, 1.0, jnp.where(hit2, 1.0, 0.0))
    base = jnp.dot(tri_ref[...], onehot.astype(bf16), preferred_element_type=f32) + run_scr[...]
    r1 = jnp.sum(jnp.where(hit1, base, 0.0), axis=-1, keepdims=True)
    r2 = jnp.sum(jnp.where(hit2, base, 0.0), axis=-1, keepdims=True)
    run_scr[...] = run_scr[...] + jnp.sum(onehot, axis=0, keepdims=True)
    cnt_ref[...] = run_scr[...]
    vals = (e1.astype(f32), e2.astype(f32), pg_top * p1 / den, pg_top * p2 / den, r1, r2)
    out = jnp.zeros(lg.shape, f32)
    for j, v in enumerate(vals):
        out = jnp.where(lane == j, v, out)
    o_ref[...] = out


def _route(logits):
    t, width = logits.shape
    tri = jnp.tril(jnp.ones((TM, TM), f32), -1).astype(bf16)
    return pl.pallas_call(
        _route_kernel,
        grid=(t // TM,),
        in_specs=[pl.BlockSpec((TM, width), lambda i: (i, 0)), pl.BlockSpec((TM, TM), lambda i: (0, 0))],
        out_specs=[pl.BlockSpec((TM, width), lambda i: (i, 0)), pl.BlockSpec((1, width), lambda i: (0, 0))],
        out_shape=[jax.ShapeDtypeStruct((t, width), f32), jax.ShapeDtypeStruct((1, width), f32)],
        scratch_shapes=[pltpu.VMEM((1, width), f32)],
        compiler_params=_cparams(("arbitrary",)),
        name="route",
    )(logits, tri)


def _dispatch_kernel(dest_ref, h_ref, init_ref, xs_ref, sem):
    del init_ref
    i = pl.program_id(0)
    tm = h_ref.shape[0]

    def row_copy(r, k):
        slot = dest_ref[(i * tm + r) * MOE_TOPK + k]
        return pltpu.make_async_copy(h_ref.at[pl.ds(r, 1)], xs_ref.at[pl.ds(slot, 1)], sem)

    def issue(r, carry):
        for k in range(MOE_TOPK):
            row_copy(r, k).start()
        return carry

    lax.fori_loop(0, tm, issue, 0, unroll=8)
    for k in range(MOE_TOPK):
        pltpu.make_async_copy(h_ref, xs_ref.at[pl.ds(0, tm)], sem).wait()


def _dispatch(h2, dest, nb):
    t, d = h2.shape
    rows = nb * MOE_BLOCK
    grid_spec = pltpu.PrefetchScalarGridSpec(
        num_scalar_prefetch=1,
        grid=(t // TM,),
        in_specs=[pl.BlockSpec((TM, d), lambda i, dst: (i, 0)), pl.BlockSpec(memory_space=pl.ANY)],
        out_specs=pl.BlockSpec(memory_space=pl.ANY),
        scratch_shapes=[pltpu.SemaphoreType.DMA(())],
    )
    return pl.pallas_call(
        _dispatch_kernel,
        grid_spec=grid_spec,
        out_shape=jax.ShapeDtypeStruct((rows, d), h2.dtype),
        input_output_aliases={2: 0},
        compiler_params=_cparams(("arbitrary",)),
        name="moe_dispatch",
    )(dest, h2, jnp.zeros((rows, d), h2.dtype))


def _combine_kernel(dest_ref, x_ref, rt_ref, g2_ref, gn_ref, ys_ref, o_ref, y_scr, sem, *,
                    tiles_per_batch, batch, final_norm):
    i = pl.program_id(0)
    tm = x_ref.shape[0]

    def row_copy(r, k):
        slot = dest_ref[(i * tm + r) * MOE_TOPK + k]
        return pltpu.make_async_copy(ys_ref.at[pl.ds(slot, 1)], y_scr.at[k, pl.ds(r, 1)], sem)

    def issue(r, carry):
        for k in range(MOE_TOPK):
            row_copy(r, k).start()
        return carry

    lax.fori_loop(0, tm, issue, 0, unroll=8)
    for k in range(MOE_TOPK):
        pltpu.make_async_copy(ys_ref.at[pl.ds(0, tm)], y_scr.at[k], sem).wait()
    seg = _tile_segment(i, tiles_per_batch, batch)
    rt = rt_ref[...]
    y_lo, y_hi = 0.0, 0.0
    for k in range(MOE_TOPK):
        lo, hi = _unpack_rows(y_scr[k])
        gate = rt[:, MOE_TOPK + k:MOE_TOPK + k + 1]
        y_lo = y_lo + gate * lo
        y_hi = y_hi + gate * hi
    x = x_ref[...] + g2_ref[pl.ds(seg, 1), :] * jnp.concatenate([y_lo, y_hi], axis=1)
    if final_norm:
        ms = jnp.mean(x * x, axis=-1, keepdims=True)
        x = x * lax.rsqrt(ms + EPS) * gn_ref[...]
    o_ref[...] = x


def _combine(x, ys, dest, routed, mods, g_out, *, n_tiles, batch, seq, final_norm):
    d = x.shape[1]
    row = lambda i, dst: (i, 0)
    grid_spec = pltpu.PrefetchScalarGridSpec(
        num_scalar_prefetch=1,
        grid=(n_tiles,),
        in_specs=[pl.BlockSpec((TM, d), row),
                  pl.BlockSpec((TM, routed.shape[1]), row),
                  pl.BlockSpec((8, d), lambda i, dst: (0, 5)),
                  pl.BlockSpec((1, d), lambda i, dst: (0, 0)),
                  pl.BlockSpec(memory_space=pl.ANY)],
        out_specs=pl.BlockSpec((TM, d), row),
        scratch_shapes=[pltpu.VMEM((MOE_TOPK, TM, d // 2), jnp.uint32), pltpu.SemaphoreType.DMA(())],
    )
    return pl.pallas_call(
        functools.partial(_combine_kernel, tiles_per_batch=seq // TM, batch=batch, final_norm=final_norm),
        grid_spec=grid_spec,
        out_shape=jax.ShapeDtypeStruct((n_tiles * TM, d), f32),
        compiler_params=_cparams(("arbitrary",)),
        name="moe_combine",
    )(dest, x, routed, mods, g_out, ys)


def _moe(xn, h2, logits, mods, w1, w3, w2, layer, g_out, *, n_tiles, batch, seq, final_norm):
    t = n_tiles * TM
    m = MOE_BLOCK
    routed, cnt = _route(logits)
    s = t * MOE_TOPK
    nb = s // m + MOE_EXPERTS
    expert = routed[:, 0:MOE_TOPK].astype(i32)
    rank = routed[:, 2 * MOE_TOPK:3 * MOE_TOPK].astype(i32)
    counts = cnt[0, :MOE_EXPERTS].astype(i32)
    padded = ((counts + m - 1) // m) * m
    pad_end = jnp.cumsum(padded)
    onehot = expert[:, :, None] == jnp.arange(MOE_EXPERTS, dtype=i32)[None, None, :]
    dest = (jnp.sum(jnp.where(onehot, (pad_end - padded)[None, None, :], 0), axis=-1) + rank).reshape(s)
    n_used = (pad_end[-1] // m).astype(i32)
    blk = jnp.arange(nb, dtype=i32)
    block_e = jnp.minimum(jnp.searchsorted(pad_end, jnp.minimum(blk, n_used - 1) * m, side='right'),
                          MOE_EXPERTS - 1).astype(i32)
    ids = jnp.arange(MOE_EXPERTS, dtype=i32)
    nonempty_from = lax.cummin(jnp.where(counts > 0, ids, MOE_EXPERTS), reverse=True)
    nxt = jnp.concatenate([nonempty_from[1:], jnp.full((1,), MOE_EXPERTS, i32)])
    next_expert = jnp.where(nxt < MOE_EXPERTS, nxt, -1).astype(i32)
    xs = _dispatch(h2, dest, nb)
    ys = _moe_experts(xs, block_e, n_used.reshape(1), next_expert, w1, w3, w2, layer, nb)
    return _combine(xn, ys, dest, routed, mods, g_out, n_tiles=n_tiles, batch=batch, seq=seq,
                    final_norm=final_norm)


HY_N2 = 128
HY_WB = 256


def _short_conv_kernel(x_ref, prev_ref, next_ref, w_ref, b_ref, v_ref, x1_ref, x2_ref, ext_scr, *, tiles_per_batch):
    j = pl.program_id(0) % tiles_per_batch
    tt = x_ref.shape[0]
    ext_scr[0:8, :] = jnp.where(j > 0, prev_ref[...], 0.0)
    ext_scr[8:8 + tt, :] = x_ref[...]
    ext_scr[8 + tt:16 + tt, :] = jnp.where(j < tiles_per_batch - 1, next_ref[...], 0.0)
    w = w_ref[...]
    width = v_ref.shape[1]
    for part, o_ref in enumerate((v_ref, x1_ref, x2_ref)):
        cols = slice(part * width, (part + 1) * width)
        o_ref[...] = (ext_scr[7:7 + tt, cols] * w[0:1, cols] + ext_scr[8:8 + tt, cols] * w[1:2, cols]
                      + ext_scr[9:9 + tt, cols] * w[2:3, cols] + b_ref[:, cols])


def _short_conv(hy, w, b, *, batch, seq):
    n_lat = batch * seq
    tt = 256
    width = hy.shape[1]
    out = jax.ShapeDtypeStruct((n_lat, width // 3), f32)
    o_spec = pl.BlockSpec((tt, width // 3), lambda i: (i, 0))
    return pl.pallas_call(
        functools.partial(_short_conv_kernel, tiles_per_batch=seq // tt),
        grid=(n_lat // tt,),
        in_specs=[pl.BlockSpec((tt, width), lambda i: (i, 0)),
                  pl.BlockSpec((8, width), lambda i: (jnp.maximum(i * (tt // 8) - 1, 0), 0)),
                  pl.BlockSpec((8, width), lambda i: ((i + 1) * (tt // 8), 0)),
                  pl.BlockSpec((3, width), lambda i: (0, 0)),
                  pl.BlockSpec((1, width), lambda i: (0, 0))],
        out_specs=[o_spec, o_spec, o_spec],
        out_shape=[out, out, out],
        scratch_shapes=[pltpu.VMEM((tt + 16, width), f32)],
        compiler_params=_cparams(("arbitrary",)),
        name="short_conv",
    )(hy, hy, hy, w, b.reshape(1, width))


def _hy_filter_kernel(z_ref, w1_ref, b1_ref, fr_ref, w2_ref, b2_ref, w3_ref, dl_ref, k_ref, nrm_ref):
    i = pl.program_id(0)
    z = z_ref[...]
    fr = fr_ref[...]
    h = jnp.sin(fr[0:1] * (jnp.dot(z.astype(bf16), w1_ref[...], preferred_element_type=f32) + b1_ref[...]))
    h = jnp.sin(fr[1:2] * (jnp.dot(h.astype(bf16), w2_ref[...], preferred_element_type=f32) + b2_ref[...]))
    kk = jnp.dot(h.astype(bf16), w3_ref[...], preferred_element_type=f32)
    width = dl_ref.shape[1]
    decay = jnp.exp(-z[:, 0:1] * jnp.abs(dl_ref[...]))
    tl = z.shape[0]
    row = i * tl + lax.broadcasted_iota(i32, (tl, 1), 0)
    tot = []
    for blk in range(kk.shape[1] // width):
        kb = kk[:, blk * width:(blk + 1) * width] * decay
        if blk % 2 == 1:
            kb = jnp.where(row == 0, 0.0, kb)
        k_ref[:, blk * width:(blk + 1) * width] = kb
        tot.append(jnp.sum(jnp.abs(kb), axis=0, keepdims=True))
    s = jnp.concatenate(tot, axis=1)

    @pl.when(i == 0)
    def _():
        nrm_ref[...] = s

    @pl.when(i > 0)
    def _():
        nrm_ref[...] = nrm_ref[...] + s


def _hy_filters(seq, w1, b1, freq, w2, b2, w3):
    t = jnp.linspace(0.0, 1.0, seq, dtype=f32)[:, None]
    bands = (HY_EMB - 1) // 2
    w = 2 * math.pi * jnp.arange(seq, dtype=f32)[:, None] / seq
    f = jnp.linspace(1e-4, bands - 1, bands, dtype=f32)[None, :]
    z = jnp.concatenate([t, jnp.cos(f * w), -jnp.sin(f * w), jnp.zeros((seq, 128 - HY_EMB), f32)], axis=-1)
    w1p = jnp.concatenate([w1, jnp.zeros((128 - HY_EMB, w1.shape[1]), f32)], axis=0).astype(bf16)
    hid = w1.shape[1]
    n_out = w3.shape[1]
    deltas = jnp.linspace(math.log(HY_TARGET) / HY_SLOW_DECAY, math.log(HY_TARGET) / HY_FAST_DECAY,
                          HY_WIDTH, dtype=f32).reshape(1, HY_WIDTH)
    tl = 512
    const = lambda a: pl.BlockSpec(a.shape, lambda i: (0,) * a.ndim)
    ins = [z, w1p, b1.reshape(1, hid), freq, w2.astype(bf16), b2.reshape(1, hid), w3.astype(bf16), deltas]
    return pl.pallas_call(
        _hy_filter_kernel,
        grid=(seq // tl,),
        in_specs=[pl.BlockSpec((tl, 128), lambda i: (i, 0))] + [const(a) for a in ins[1:]],
        out_specs=[pl.BlockSpec((tl, n_out), lambda i: (i, 0)), pl.BlockSpec((1, n_out), lambda i: (0, 0))],
        out_shape=[jax.ShapeDtypeStruct((seq, n_out), f32), jax.ShapeDtypeStruct((1, n_out), f32)],
        compiler_params=_cparams(("arbitrary",)),
        name="hy_filter",
    )(*ins)


def _dft_tables(seq):
    n2 = HY_N2
    n_tot = 2 * seq
    n1 = n_tot // n2
    nf = n1 // 2 + 1
    nfp = -(-nf // 8) * 8
    f1 = jnp.arange(nfp, dtype=i32)
    k1 = jnp.arange(n1 // 2, dtype=i32)
    th = (2 * math.pi / n1) * ((f1[:, None] * k1[None, :]) % n1).astype(f32)
    valid = (f1 < nf)[:, None]
    fwd = jnp.concatenate([jnp.where(valid, jnp.cos(th), 0.0), jnp.where(valid, -jnp.sin(th), 0.0)], axis=0)
    eye = jnp.eye(8, dtype=f32)
    fk = jnp.einsum('rk,jh->rjkh', fwd, eye).reshape(2 * nfp * 8, (n1 // 2) * 8)
    cw = jnp.where(valid, jnp.where((f1 == 0) | (f1 == n1 // 2), 1.0, 2.0)[:, None], 0.0) / n_tot
    inv = jnp.concatenate([(cw * jnp.cos(th)).T, (-cw * jnp.sin(th)).T], axis=1)
    ck = jnp.einsum('kr,jh->kjrh', inv, eye).reshape((n1 // 2) * 8, 2 * nfp * 8)
    fa = jnp.arange(nf, dtype=i32)[:, None, None]
    f2 = jnp.arange(n2, dtype=i32)[None, :, None]
    m2 = jnp.arange(n2, dtype=i32)[None, None, :]
    ph = (2 * math.pi / n_tot) * ((f2 * m2 * n1 + m2 * fa) % n_tot).astype(f32)
    gr, gi = jnp.cos(ph), -jnp.sin(ph)
    g = jnp.concatenate([jnp.concatenate([gr, -gi], axis=2), jnp.concatenate([gi, gr], axis=2)], axis=1)
    hr, hi = jnp.swapaxes(jnp.cos(ph), 1, 2), jnp.swapaxes(jnp.sin(ph), 1, 2)
    h = jnp.concatenate([jnp.concatenate([hr, -hi], axis=2), jnp.concatenate([hi, hr], axis=2)], axis=1)
    return fk.astype(bf16), ck.astype(bf16), g.astype(bf16), h.astype(bf16), nf, nfp


def _hy_rows_per_step(nf):
    return max(d for d in (5, 4, 3, 2, 1) if nf % d == 0)


def _dft_stage1(x_ref, fk_ref, a_scr):
    n1h, n2, wb = x_ref.shape

    def slab(s, carry):
        rows = pl.ds(pl.multiple_of(s * 8, 8), 8)
        xs = x_ref[:, rows, :].reshape(n1h * 8, wb).astype(bf16)
        a = jnp.dot(fk_ref[...], xs, preferred_element_type=f32)
        a_scr[:, rows, :] = a.reshape(a_scr.shape[0], 8, wb)
        return carry

    lax.fori_loop(0, n2 // 8, slab, 0)


def _dft_stage2(a_scr, g, f1, nfp):
    a = jnp.concatenate([a_scr[f1], a_scr[nfp + f1]], axis=0).astype(bf16)
    return jnp.dot(g, a, preferred_element_type=f32)


def _hy_spectrum_kernel(k_ref, fk_ref, g_ref, o_ref, a_scr, *, nfp):
    fs = pl.program_id(2)

    @pl.when(fs == 0)
    def _():
        _dft_stage1(k_ref, fk_ref, a_scr)

    fb = g_ref.shape[0]
    n2 = k_ref.shape[1]
    for j in range(fb):
        s = _dft_stage2(a_scr, g_ref[j], fs * fb + j, nfp)
        o_ref[0, j] = s[:n2]
        o_ref[1, j] = s[n2:]


def _hy_spectrum(k, fk, g, nf, nfp, seq):
    n2, wb = HY_N2, HY_WB
    n1h = seq // n2
    fb = _hy_rows_per_step(nf)
    nwb = HY_WIDTH // wb
    n_blk = k.shape[1] // HY_WIDTH
    k3 = k.reshape(n1h, n2, k.shape[1])
    return pl.pallas_call(
        functools.partial(_hy_spectrum_kernel, nfp=nfp),
        grid=(n_blk, nwb, nf // fb),
        in_specs=[pl.BlockSpec((n1h, n2, wb), lambda o, w, f: (0, 0, o * nwb + w), pipeline_mode=pl.Buffered(1)),
                  pl.BlockSpec(fk.shape, lambda o, w, f: (0, 0)),
                  pl.BlockSpec((fb, 2 * n2, 2 * n2), lambda o, w, f: (f, 0, 0))],
        out_specs=pl.BlockSpec((None, 2, fb, n2, wb), lambda o, w, f: (o, 0, f, 0, w)),
        out_shape=jax.ShapeDtypeStruct((n_blk, 2, nf, n2, HY_WIDTH), f32),
        scratch_shapes=[pltpu.VMEM((2 * nfp, n2, wb), f32)],
        compiler_params=_cparams(("arbitrary",) * 3),
        name="hy_spectrum",
    )(k3, fk, g)


def _hy_conv_kernel(x_ref, gate_ref, kf_ref, kb_ref, g_ref, h_ref, fk_ref, ck_ref, bias_ref, inv_ref, o_ref, a_scr, *,
                    nfp):
    fs = pl.program_id(2)
    n1h, n2, wb = x_ref.shape

    @pl.when(fs == 0)
    def _():
        _dft_stage1(x_ref, fk_ref, a_scr)

    fb = g_ref.shape[0]
    for j in range(fb):
        f1 = fs * fb + j
        xs = _dft_stage2(a_scr, g_ref[j], f1, nfp)
        xr, xi = xs[:n2], xs[n2:]
        kr = kf_ref[0, j] + kb_ref[0, j]
        ki = kf_ref[1, j] - kb_ref[1, j]
        y = jnp.concatenate([xr * kr - xi * ki, xr * ki + xi * kr], axis=0).astype(bf16)
        bb = jnp.dot(h_ref[j], y, preferred_element_type=f32)
        a_scr[f1] = bb[:n2]
        a_scr[nfp + f1] = bb[n2:]

    @pl.when(fs == pl.num_programs(2) - 1)
    def _():
        bias = bias_ref[...].reshape(1, 1, wb)
        inv = inv_ref[...].reshape(1, 1, wb)

        def slab(s, carry):
            rows = pl.ds(pl.multiple_of(s * 8, 8), 8)
            bs = a_scr[:, rows, :].reshape(2 * nfp * 8, wb).astype(bf16)
            y = jnp.dot(ck_ref[...], bs, preferred_element_type=f32).reshape(n1h, 8, wb)
            o_ref[:, rows, :] = (gate_ref[:, rows, :] * (y * inv + x_ref[:, rows, :] * bias)).astype(o_ref.dtype)
            return carry

        lax.fori_loop(0, n2 // 8, slab, 0)


def _hy_conv(x, gate, kf, order, tabs, bias, inv_norm, out_dtype, *, batch, seq):
    fk, ck, g, h, nf, nfp = tabs
    n2, wb = HY_N2, HY_WB
    n1h = seq // n2
    fb = _hy_rows_per_step(nf)
    width = x.shape[1]
    big = pl.BlockSpec((n1h, n2, wb), lambda b, w, f: (b, 0, w), pipeline_mode=pl.Buffered(1))
    out = pl.pallas_call(
        functools.partial(_hy_conv_kernel, nfp=nfp),
        grid=(batch, width // wb, nf // fb),
        in_specs=[big, big,
                  pl.BlockSpec((None, 2, fb, n2, wb), lambda b, w, f: (2 * order, 0, f, 0, w)),
                  pl.BlockSpec((None, 2, fb, n2, wb), lambda b, w, f: (2 * order + 1, 0, f, 0, w)),
                  pl.BlockSpec((fb, 2 * n2, 2 * n2), lambda b, w, f: (f, 0, 0)),
                  pl.BlockSpec((fb, 2 * n2, 2 * n2), lambda b, w, f: (f, 0, 0)),
                  pl.BlockSpec(fk.shape, lambda b, w, f: (0, 0)),
                  pl.BlockSpec(ck.shape, lambda b, w, f: (0, 0)),
                  pl.BlockSpec((1, wb), lambda b, w, f: (0, w)),
                  pl.BlockSpec((1, wb), lambda b, w, f: (0, w))],
        out_specs=big,
        out_shape=jax.ShapeDtypeStruct((batch * n1h, n2, width), out_dtype),
        scratch_shapes=[pltpu.VMEM((2 * nfp, n2, wb), f32)],
        compiler_params=_cparams(("arbitrary",) * 3),
        name="hy_conv",
    )(x.reshape(batch * n1h, n2, width), gate.reshape(batch * n1h, n2, width), kf, kf, g, h, fk, ck,
      bias.reshape(1, width), inv_norm.reshape(1, width))
    return out.reshape(batch * seq, width)


def _hyena(hy, conv_w, conv_b, f_w1, f_b1, f_freq, f_w2, f_b2, f_w3, hy_bias, *, batch, seq):
    v, x1, x2 = _short_conv(hy, conv_w, conv_b, batch=batch, seq=seq)
    k, sums = _hy_filters(seq, f_w1, f_b1, f_freq, f_w2, f_b2, f_w3)
    sums = sums.reshape(HY_ORDER, 2, HY_WIDTH)
    inv_norm = 1.0 / (sums[:, 0] + sums[:, 1])
    tabs = _dft_tables(seq)
    kf = _hy_spectrum(k, tabs[0], tabs[2], tabs[4], tabs[5], seq)
    z = _hy_conv(v, x1, kf, 0, tabs, hy_bias[0], inv_norm[0], f32, batch=batch, seq=seq)
    return _hy_conv(z, x2, kf, 1, tabs, hy_bias[1], inv_norm[1], bf16, batch=batch, seq=seq)


S5_TT = 256
S5_GB = 8


def _gelu_tanh(x):
    return 0.5 * x * (1.0 + jnp.tanh(0.7978845608028654 * (x + 0.044715 * x * x * x)))


def _s5_kernel(u_ref, perm_ref, permt_ref, wb_ref, wc_ref, are_ref, aim_ref, pre_ref, pim_ref, *rest, fuse_glu):
    if fuse_glu:
        yf_ref, d_ref, gw_ref, gb_ref, o_ref, st_scr, bu_scr, y_scr = rest
    else:
        o_ref, st_scr, bu_scr, y_scr = rest
    tt = u_ref.shape[0]
    steps = tt // 8
    half = bu_scr.shape[1] // 2

    @pl.when(pl.program_id(1) == 0)
    def _():
        st_scr[...] = jnp.zeros_like(st_scr)

    up = jnp.dot(perm_ref[...], u_ref[...].astype(bf16), preferred_element_type=f32).astype(bf16)
    for gb in range(wb_ref.shape[0]):
        bu_scr[...] = jnp.dot(up[:, gb * 128:(gb + 1) * 128], wb_ref[gb], preferred_element_type=f32)
        a_re = jnp.broadcast_to(are_ref[gb], (8, half))
        a_im = jnp.broadcast_to(aim_ref[gb], (8, half))

        def scan_step(k, h):
            hr, hi = h
            rows = pl.ds(k * 8, 8)
            nr = a_re * hr - a_im * hi + bu_scr[rows, 0:half]
            ni = a_re * hi + a_im * hr + bu_scr[rows, half:]
            bu_scr[rows, 0:half] = nr
            bu_scr[rows, half:] = ni
            return nr, ni

        zero = jnp.zeros((8, half), f32)
        h = (zero, zero)
        for k in range(steps):
            h = scan_step(k, h)
        hr, hi = h

        pw_r = pre_ref[gb, steps - 1:steps, :]
        pw_i = pim_ref[gb, steps - 1:steps, :]
        cur_r = st_scr[gb, :, 0:half]
        cur_i = st_scr[gb, :, half:]
        car_r, car_i = [], []
        for j in range(8):
            car_r.append(cur_r)
            car_i.append(cur_i)
            cur_r, cur_i = (hr[j:j + 1] + pw_r * cur_r - pw_i * cur_i,
                            hi[j:j + 1] + pw_r * cur_i + pw_i * cur_r)
        st_scr[gb, :, 0:half] = cur_r
        st_scr[gb, :, half:] = cur_i
        cr = jnp.concatenate(car_r, axis=0)
        ci = jnp.concatenate(car_i, axis=0)

        def fix_step(k, carry):
            rows = pl.ds(k * 8, 8)
            pr = pre_ref[gb, pl.ds(k, 1), :]
            pi = pim_ref[gb, pl.ds(k, 1), :]
            bu_scr[rows, 0:half] = bu_scr[rows, 0:half] + (pr * cr - pi * ci)
            bu_scr[rows, half:] = bu_scr[rows, half:] + (pr * ci + pi * cr)
            return carry

        for k in range(steps):
            fix_step(k, 0)
        y_scr[:, gb * 128:(gb + 1) * 128] = jnp.dot(bu_scr[...].astype(bf16), wc_ref[gb],
                                                      preferred_element_type=f32)
    yp = y_scr[...]
    y_hi = yp.astype(bf16)
    y_lo = (yp - y_hi.astype(f32)).astype(bf16)
    y = (jnp.dot(permt_ref[...], y_hi, preferred_element_type=f32)
         + jnp.dot(permt_ref[...], y_lo, preferred_element_type=f32))
    if fuse_glu:
        g = _gelu_tanh(y + yf_ref[...] + u_ref[...] * d_ref[...])
        z = jnp.dot(g.astype(bf16), gw_ref[...], preferred_element_type=f32) + gb_ref[...]
        o_ref[...] = (g * jax.nn.sigmoid(z)).astype(o_ref.dtype)
    else:
        o_ref[...] = y


def _s5_tables(a_re, a_im, log_dt, b_re, b_im, c_re, c_im, steps):
    ng, ns = a_re.shape
    nb = ng // S5_GB
    lam = lax.complex(a_re, a_im)
    dt = jnp.exp(log_dt)[:, None]
    lam_bar = jnp.exp(lam * dt)
    b_bar = ((lam_bar - 1) / lam)[..., None] * lax.complex(b_re, b_im)
    cm = lax.complex(c_re, c_im)
    pw = jnp.exp((lam * dt)[None] * jnp.arange(1, steps + 1, dtype=f32)[:, None, None])
    eye = jnp.eye(S5_GB, dtype=f32)
    bb = b_bar.reshape(nb, S5_GB, ns, S5_GROUP)
    wb = jnp.concatenate([jnp.einsum('bgpc,gh->bgchp', part, eye).reshape(nb, S5_GB * S5_GROUP, S5_GB * ns)
                          for part in (jnp.real(bb), jnp.imag(bb))], axis=-1)
    cc = cm.reshape(nb, S5_GB, S5_GROUP, ns)
    wc = jnp.concatenate([jnp.einsum('bgcp,gh->bgphc', part, eye).reshape(nb, S5_GB * ns, S5_GB * S5_GROUP)
                          for part in (jnp.real(cc), -jnp.imag(cc))], axis=1)
    lb = lam_bar.reshape(nb, 1, S5_GB * ns)
    pwb = pw.reshape(steps, nb, S5_GB * ns).transpose(1, 0, 2)
    return (wb.astype(bf16), wc.astype(bf16), jnp.real(lb), jnp.imag(lb), jnp.real(pwb), jnp.imag(pwb))


def _s5_perm(tt, reverse):
    steps = tt // 8
    dst = jnp.arange(tt)
    t = (dst % 8) * steps + dst // 8
    src = tt - 1 - t if reverse else t
    return jnp.arange(tt)[None, :] == src[:, None]


def _s5(u, a_re, a_im, log_dt, b_re, b_im, c_re, c_im, s5_d, glu_w, glu_b, *, batch, seq, ctx_len):
    r, width = u.shape
    tt = S5_TT
    assert ctx_len == tt and seq % tt == 0
    n_lat = seq // tt
    ctx_blk0 = (batch * seq) // tt
    y = None
    for di in range(2):
        rev = di == 1
        wb, wc, lr, li, pr, pi = _s5_tables(a_re[di], a_im[di], log_dt[di], b_re[di], b_im[di],
                                            c_re[di], c_im[di], tt // 8)
        perm = _s5_perm(tt, rev)

        def tile_map(b, i, rev=rev):
            lat = b * n_lat + (n_lat - i if rev else i - 1)
            return (jnp.where(i == 0, ctx_blk0 + b, lat), 0)

        row_spec = pl.BlockSpec((tt, width), tile_map)
        const = lambda a: pl.BlockSpec(a.shape, lambda b, i: (0,) * a.ndim)
        ins = [u, perm.astype(bf16), perm.T.astype(bf16), wb, wc, lr, li, pr, pi]
        in_specs = [row_spec] + [const(a) for a in ins[1:]]
        if rev:
            extra = [y, s5_d.reshape(1, width), glu_w.astype(bf16), glu_b.reshape(1, width)]
            in_specs += [row_spec] + [const(a) for a in extra[1:]]
            ins += extra
        y = pl.pallas_call(
            functools.partial(_s5_kernel, fuse_glu=rev),
            grid=(batch, n_lat + 1),
            in_specs=in_specs,
            out_specs=row_spec,
            out_shape=jax.ShapeDtypeStruct((r, width), bf16 if rev else f32),
            scratch_shapes=[pltpu.VMEM((wb.shape[0], 1, wb.shape[2]), f32),
                            pltpu.VMEM((tt, wb.shape[2]), f32),
                            pltpu.VMEM((tt, width), f32)],
            compiler_params=_cparams(("arbitrary", "arbitrary")),
            name="s5_bwd_glu" if rev else "s5_fwd",
        )(*ins)
    return y


def _rope_tables(seq):
    rows = seq // GRID_W
    row = jnp.repeat(jnp.arange(rows, dtype=f32), GRID_W)
    col = jnp.tile(jnp.arange(GRID_W, dtype=f32), rows)
    n_freq = DIFF_QK // 4
    inv = ROPE_BASE ** (-jnp.arange(n_freq, dtype=f32) / n_freq)
    ar, ac = row[:, None] * inv, col[:, None] * inv
    cos = jnp.concatenate([jnp.cos(ar), jnp.cos(ar), jnp.cos(ac), jnp.cos(ac)], axis=-1)
    sin = jnp.concatenate([-jnp.sin(ar), jnp.sin(ar), -jnp.sin(ac), jnp.sin(ac)], axis=-1)
    cos = jnp.concatenate([jnp.tile(cos, (1, 2)), jnp.ones((TM, 128), f32)], axis=0)
    sin = jnp.concatenate([jnp.tile(sin, (1, 2)), jnp.zeros((TM, 128), f32)], axis=0)
    return cos, sin


def kernel(x, c, ctx, c_ctx, ada_w, ada_b, norm_mix_g, norm_ffn_g, norm_out_g, ab_w_in, ab_w_out, diff_lam, diff_subln_g, pool_w, pool_scale, cd_w_in, cd_w_out, hy_conv_w, hy_conv_b, hy_f_w1, hy_f_b1, hy_f_freq, hy_f_w2, hy_f_b2, hy_f_w3, hy_bias, s5_a_re, s5_a_im, s5_log_dt, s5_b_re, s5_b_im, s5_c_re, s5_c_im, s5_d, glu_w, glu_b, moe_w_rg, moe_b_rg, moe_w_re, moe_b_re, moe_w1, moe_w3, moe_w2):
    bsz, seq, d = x.shape
    cl = ctx.shape[1]
    depth = ada_w.shape[0]
    n_lat = bsz * seq
    assert seq % TM == 0 and (bsz * cl) % TM == 0 and bsz + 1 <= 8
    xs = jnp.concatenate([x.reshape(n_lat, d), ctx.reshape(bsz * cl, d)], axis=0)
    r = xs.shape[0]
    cond = jnp.zeros((8, d), f32).at[0:bsz].set(c).at[bsz].set(c_ctx)
    mods = _ada_mods(cond, ada_w, ada_b)
    geom = dict(batch=bsz, seq=seq)

    for l in range(depth):
        last = l == depth - 1
        i = l // 2
        n_tiles = (n_lat if last else r) // TM
        g_mix = norm_mix_g[l].reshape(1, d)
        if l % 2 == 0:
            lam_init = 0.8 - 0.6 * math.exp(-0.3 * l)
            cos, sin = _rope_tables(seq)
            q, k, vt, p = _proj_rope(xs, g_mix, mods[l], ab_w_in[i].astype(bf16), cos, sin, **geom)
            mix_a = _diff_attention(q, k, vt, diff_lam[i], diff_subln_g[i], lam_init, ctx_len=cl, **geom)
            mix_b = _pool(p, pool_w[i], pool_scale[i], ctx_len=cl, **geom)
            w_out = ab_w_out[i]
        else:
            hw = (HY_ORDER + 1) * HY_WIDTH
            hy, u = _proj(xs, g_mix, mods[l], cd_w_in[i].astype(bf16), hw, **geom)
            mix_a = _hyena(hy, hy_conv_w[i], hy_conv_b[i], hy_f_w1[i], hy_f_b1[i], hy_f_freq[i],
                           hy_f_w2[i], hy_f_b2[i], hy_f_w3[i], hy_bias[i], **geom)
            mix_b = _s5(u, s5_a_re[i], s5_a_im[i], s5_log_dt[i], s5_b_re[i], s5_b_im[i],
                        s5_c_re[i], s5_c_im[i], s5_d[i], glu_w[i], glu_b[i], ctx_len=cl, **geom)
            w_out = cd_w_out[i]
        w_r = jnp.zeros((d, 128), f32).at[:, :MOE_GROUPS].set(moe_w_rg[l]) \
            .at[:, MOE_GROUPS:MOE_GROUPS + MOE_EXPERTS].set(moe_w_re[l]).astype(bf16)
        b_r = jnp.zeros((1, 128), f32).at[0, :MOE_GROUPS].set(moe_b_rg[l]) \
            .at[0, MOE_GROUPS:MOE_GROUPS + MOE_EXPERTS].set(moe_b_re[l])
        xn, h2, logits = _outproj(mix_a, mix_b, xs, w_out.astype(bf16), mods[l], norm_ffn_g[l].reshape(1, d),
                                  w_r, b_r, n_tiles=n_tiles, **geom)
        xs = _moe(xn, h2, logits, mods[l], moe_w1, moe_w3, moe_w2, l, norm_out_g.reshape(1, d),
                  n_tiles=n_tiles, final_norm=last, **geom)
    return xs[:n_lat].reshape(bsz, seq, d)
```

```python
import functools
import math

import jax
import jax.numpy as jnp
from jax import lax
from jax.experimental import pallas as pl
from jax.experimental.pallas import tpu as pltpu

f32, bf16, i32 = jnp.float32, jnp.bfloat16, jnp.int32

EPS = 1e-6
GRID_W = 64
DIFF_QK = 64
DIFF_V = 128
ROPE_BASE = 10000.0
POOL_WINDOWS = (2, 4, 8, 16)
POOL_GROUP = 256
HY_WIDTH = 1024
HY_ORDER = 2
HY_EMB = 33
HY_FAST_DECAY = 0.3
HY_SLOW_DECAY = 1.5
HY_TARGET = 1e-2
S5_GROUP = 16
S5_GROUPS = 64
S5_STATE = 64
MOE_GROUPS = 8
MOE_PER_GROUP = 8
MOE_EXPERTS = 64
MOE_TOPK = 2
MOE_BLOCK = 256

VMEM_LIMIT = 56 * 1024 * 1024
TM = 512
LOG2E = 1.4426950408889634


def _cparams(sem):
    return pltpu.CompilerParams(dimension_semantics=sem, vmem_limit_bytes=VMEM_LIMIT)


def _norm_mod(x, g, shift, scale):
    ms = jnp.mean(x * x, axis=-1, keepdims=True)
    return (x * lax.rsqrt(ms + EPS) * g) * (1.0 + scale) + shift


def _tile_segment(i, lat_tiles_per_batch, batch):
    seg = jnp.zeros((), i32)
    for b in range(1, batch + 1):
        seg = seg + (i >= b * lat_tiles_per_batch).astype(i32)
    return seg


def _pack_rows(x):
    half = x.shape[1] // 2
    bits = lax.bitcast_convert_type(x.astype(bf16).astype(f32), jnp.uint32)
    return (bits[:, :half] >> 16) | (bits[:, half:] & jnp.uint32(0xFFFF0000))


def _unpack_rows(words):
    return (lax.bitcast_convert_type(words << 16, f32),
            lax.bitcast_convert_type(words & jnp.uint32(0xFFFF0000), f32))


def _ada_kernel(s_ref, w_ref, b_ref, o_ref):
    s = s_ref[...]
    a = (s * jax.nn.sigmoid(s)).astype(bf16)
    o_ref[...] = jnp.dot(a, w_ref[...].astype(bf16), preferred_element_type=f32) + b_ref[...]


def _ada_mods(s, ada_w, ada_b):
    depth, d, n = ada_w.shape
    tn = 1024
    return pl.pallas_call(
        _ada_kernel,
        grid=(depth, n // tn),
        in_specs=[pl.BlockSpec((8, d), lambda l, j: (0, 0)),
                  pl.BlockSpec((None, d, tn), lambda l, j: (l, 0, j)),
                  pl.BlockSpec((None, 1, tn), lambda l, j: (l, 0, j))],
        out_specs=pl.BlockSpec((None, 8, tn), lambda l, j: (l, 0, j)),
        out_shape=jax.ShapeDtypeStruct((depth, 8, n), f32),
        compiler_params=_cparams(("arbitrary", "arbitrary")),
        name="ada_mods",
    )(s, ada_w, ada_b.reshape(depth, 1, n))


def _proj_rope_kernel(x_ref, g_ref, sh_ref, sc_ref, w_ref, cos_ref, sin_ref,
                      q_ref, k_ref, v_ref, p_ref, h_scr, *, tiles_per_batch, batch, qscale):
    seg = _tile_segment(pl.program_id(0), tiles_per_batch, batch)
    h = _norm_mod(x_ref[...], g_ref[...], sh_ref[pl.ds(seg, 1), :], sc_ref[pl.ds(seg, 1), :])
    h_scr[...] = h.astype(bf16)
    cos = cos_ref[...]
    sin = sin_ref[...]
    tm = cos.shape[0]
    lane = lax.broadcasted_iota(i32, (tm, 128), 1)
    first = (lane % 32) < 16

    def rope(r):
        sw = jnp.where(first, pltpu.roll(r, 112, 1), pltpu.roll(r, 16, 1))
        return r * cos + sw * sin

    heads = q_ref.shape[0]
    nc = 512
    for c in range(w_ref.shape[1] // nc):
        res = jnp.dot(h_scr[...], w_ref[:, c * nc:(c + 1) * nc], preferred_element_type=f32)
        for s in range(nc // 128):
            r = res[:, s * 128:(s + 1) * 128]
            col = (c * nc) // 128 + s
            if col < heads:
                q_ref[col] = (rope(r) * qscale).astype(bf16)
            elif col < 2 * heads:
                k_ref[col - heads] = rope(r).astype(bf16)
            elif col < 3 * heads:
                v_ref[col - 2 * heads] = r.T.astype(bf16)
            else:
                c0 = (col - 3 * heads) * 128
                p_ref[:, c0:c0 + 128] = r


def _proj_rope(x, g, mods, w, cos, sin, *, batch, seq):
    r, d = x.shape
    heads = 8
    lat_tiles = seq // TM
    n_tiles = r // TM

    def tab_map(i):
        return (jnp.where(i < batch * lat_tiles, i % lat_tiles, lat_tiles), 0)

    hm = jax.ShapeDtypeStruct((heads, r, 128), bf16)
    hm_spec = pl.BlockSpec((heads, TM, 128), lambda i: (0, i, 0))
    return pl.pallas_call(
        functools.partial(_proj_rope_kernel, tiles_per_batch=lat_tiles, batch=batch,
                          qscale=LOG2E * DIFF_QK ** -0.5),
        grid=(n_tiles,),
        in_specs=[pl.BlockSpec((TM, d), lambda i: (i, 0)),
                  pl.BlockSpec((1, d), lambda i: (0, 0)),
                  pl.BlockSpec((8, d), lambda i: (0, 0)),
                  pl.BlockSpec((8, d), lambda i: (0, 1)),
                  pl.BlockSpec(w.shape, lambda i: (0, 0), pipeline_mode=pl.Buffered(1)),
                  pl.BlockSpec((TM, 128), tab_map),
                  pl.BlockSpec((TM, 128), tab_map)],
        out_specs=[hm_spec, hm_spec, pl.BlockSpec((heads, 128, TM), lambda i: (0, 0, i)),
                   pl.BlockSpec((TM, 1024), lambda i: (i, 0))],
        out_shape=[hm, hm, jax.ShapeDtypeStruct((heads, 128, r), bf16), jax.ShapeDtypeStruct((r, 1024), f32)],
        scratch_shapes=[pltpu.VMEM((TM, d), bf16)],
        compiler_params=_cparams(("arbitrary",)),
        name="proj_rope",
    )(x, g, mods, mods, w, cos, sin)


def _proj_kernel(x_ref, g_ref, sh_ref, sc_ref, w_ref, hy_ref, u_ref, h_scr, *, tiles_per_batch, batch):
    seg = _tile_segment(pl.program_id(0), tiles_per_batch, batch)
    h = _norm_mod(x_ref[...], g_ref[...], sh_ref[pl.ds(seg, 1), :], sc_ref[pl.ds(seg, 1), :])
    h_scr[...] = h.astype(bf16)
    nc = 512
    n_hy = hy_ref.shape[1]
    for c in range(w_ref.shape[1] // nc):
        res = jnp.dot(h_scr[...], w_ref[:, c * nc:(c + 1) * nc], preferred_element_type=f32)
        if c * nc < n_hy:
            hy_ref[:, c * nc:(c + 1) * nc] = res
        else:
            u_ref[:, c * nc - n_hy:(c + 1) * nc - n_hy] = res


def _proj(x, g, mods, w, n_hy, *, batch, seq):
    r, d = x.shape
    n = w.shape[1]
    return pl.pallas_call(
        functools.partial(_proj_kernel, tiles_per_batch=seq // TM, batch=batch),
        grid=(r // TM,),
        in_specs=[pl.BlockSpec((TM, d), lambda i: (i, 0)),
                  pl.BlockSpec((1, d), lambda i: (0, 0)),
                  pl.BlockSpec((8, d), lambda i: (0, 0)),
                  pl.BlockSpec((8, d), lambda i: (0, 1)),
                  pl.BlockSpec(w.shape, lambda i: (0, 0), pipeline_mode=pl.Buffered(1))],
        out_specs=[pl.BlockSpec((TM, n_hy), lambda i: (i, 0)), pl.BlockSpec((TM, n - n_hy), lambda i: (i, 0))],
        out_shape=[jax.ShapeDtypeStruct((r, n_hy), f32), jax.ShapeDtypeStruct((r, n - n_hy), f32)],
        scratch_shapes=[pltpu.VMEM((TM, d), bf16)],
        compiler_params=_cparams(("arbitrary",)),
        name="proj",
    )(x, g, mods, mods, w)


def _attn_kernel(lv_ref, g_ref, q_ref, kc_ref, vc_ref, *rest, n_lat, tk, lam_init):
    if n_lat:
        kl_ref, vl_ref, o_ref, s_scr = rest
    else:
        o_ref, s_scr = rest
    lv = lv_ref[...]
    lam = (jnp.exp(jnp.sum(lv[0:1] * lv[1:2], axis=-1, keepdims=True))
           - jnp.exp(jnp.sum(lv[2:3] * lv[3:4], axis=-1, keepdims=True)) + lam_init)
    q = q_ref[...]
    tq = q.shape[0]
    lane = lax.broadcasted_iota(i32, q.shape, 1)
    zero = jnp.zeros_like(q)
    qs = (jnp.where(lane < DIFF_QK, q, zero), jnp.where(lane >= DIFF_QK, q, zero))
    nt = (((1,), (1,)), ((), ()))

    def scores(slot, k):
        for m in range(2):
            s_scr[slot, m, 0:k.shape[0], :] = lax.dot_general(k, qs[m], nt, preferred_element_type=f32)

    def update(carry, slot, nk, vt):
        new = []
        for m in range(2):
            mx, l, acc = carry[3 * m:3 * m + 3]
            s = s_scr[slot, m, 0:nk, :]
            mn = jnp.maximum(mx, jnp.max(s, axis=0, keepdims=True))
            alpha = jnp.exp2(mx - mn)
            p = jnp.exp2(s - mn)
            l = alpha * l + jnp.sum(p, axis=0, keepdims=True)
            acc = alpha * acc + jnp.dot(vt, p.astype(bf16), preferred_element_type=f32)
            new += [mn, l, acc]
        return tuple(new)

    def k_block(j):
        return kl_ref[pl.ds(pl.multiple_of(j * tk, tk), tk), :]

    def vt_block(j):
        return vl_ref[:, pl.ds(pl.multiple_of(j * tk, tk), tk)]

    init = (jnp.full((1, tq), -jnp.inf, f32), jnp.zeros((1, tq), f32), jnp.zeros((DIFF_V, tq), f32)) * 2
    nc = kc_ref.shape[0]
    scores(1, kc_ref[...])
    if n_lat:
        scores(0, k_block(0))
        carry = update(init, 1, nc, vc_ref[...])

        def body(jj, carry):
            j = 4 * jj
            scores(1, k_block(j + 1))
            carry = update(carry, 0, tk, vt_block(j))
            scores(0, k_block(j + 2))
            carry = update(carry, 1, tk, vt_block(j + 1))
            scores(1, k_block(j + 3))
            carry = update(carry, 0, tk, vt_block(j + 2))
            scores(0, k_block(jnp.minimum(j + 4, n_lat - 1)))
            return update(carry, 1, tk, vt_block(j + 3))

        carry = lax.fori_loop(0, n_lat // 4, body, carry)
    else:
        carry = update(init, 1, nc, vc_ref[...])
    _, l0, a0, _, l1, a1 = carry
    o = a0 / l0 - lam * (a1 / l1)
    ms = jnp.mean(o * o, axis=0, keepdims=True)
    y = (o * lax.rsqrt(ms + EPS) * g_ref[...]) * (1.0 - lam_init)
    o_ref[...] = y.T.astype(o_ref.dtype)


def _diff_attention(q, k, vt, lam_vecs, subln_g, lam_init, *, batch, seq, ctx_len):
    heads, r, _ = q.shape
    tq, tk = 512, 512
    ctx_blk0 = (batch * seq) // ctx_len
    assert (seq // tk) % 4 == 0
    small = [pl.BlockSpec((4, DIFF_QK), lambda *_: (0, 0)), pl.BlockSpec((DIFF_V, 1), lambda *_: (0, 0))]
    kc_spec = pl.BlockSpec((None, ctx_len, 128), lambda b, h, *_: (h, ctx_blk0 + b, 0))
    vc_spec = pl.BlockSpec((None, 128, ctx_len), lambda b, h, *_: (h, 0, ctx_blk0 + b))
    g2 = subln_g.reshape(DIFF_V, 1)
    lat = pl.pallas_call(
        functools.partial(_attn_kernel, n_lat=seq // tk, tk=tk, lam_init=lam_init),
        grid=(batch, heads, seq // tq),
        in_specs=small + [pl.BlockSpec((None, tq, 128), lambda b, h, i: (h, b * (seq // tq) + i, 0)),
                          kc_spec, vc_spec,
                          pl.BlockSpec((None, seq, 128), lambda b, h, i: (h, b, 0)),
                          pl.BlockSpec((None, 128, seq), lambda b, h, i: (h, 0, b))],
        out_specs=pl.BlockSpec((tq, 128), lambda b, h, i: (b * (seq // tq) + i, h)),
        out_shape=jax.ShapeDtypeStruct((batch * seq, heads * DIFF_V), bf16),
        scratch_shapes=[pltpu.VMEM((2, 2, tk, tq), f32)],
        compiler_params=_cparams(("arbitrary",) * 3),
        name="diff_attn_lat",
    )(lam_vecs, g2, q, k, vt, k, vt)
    ctx = pl.pallas_call(
        functools.partial(_attn_kernel, n_lat=0, tk=tk, lam_init=lam_init),
        grid=(batch, heads),
        in_specs=small + [kc_spec, kc_spec, vc_spec],
        out_specs=pl.BlockSpec((ctx_len, 128), lambda b, h: (b, h)),
        out_shape=jax.ShapeDtypeStruct((batch * ctx_len, heads * DIFF_V), bf16),
        scratch_shapes=[pltpu.VMEM((2, 2, ctx_len, ctx_len), f32)],
        compiler_params=_cparams(("arbitrary",) * 2),
        name="diff_attn_ctx",
    )(lam_vecs, g2, q, k, vt)
    return jnp.concatenate([lat, ctx], axis=0)


def _pool_kernel(p_ref, prev_ref, next_ref, w_ref, scale_ref, o_ref, ext_scr, *,
                 tiles_per_batch, batch, seq, ctx_len):
    i = pl.program_id(0)
    tp = p_ref.shape[0]
    is_ctx = i >= batch * tiles_per_batch
    j = jnp.where(is_ctx, 0, i % tiles_per_batch)
    seg_len = jnp.where(is_ctx, ctx_len, seq)
    last = jnp.where(is_ctx, 0, tiles_per_batch - 1)
    ext_scr[0:8, :] = jnp.where(j > 0, prev_ref[...], 0.0)
    ext_scr[8:8 + tp, :] = p_ref[...]
    ext_scr[8 + tp:16 + tp, :] = jnp.where(j < last, next_ref[...], 0.0)
    pos = j * tp + lax.broadcasted_iota(i32, (tp, 1), 0)
    for gi, w in enumerate(POOL_WINDOWS):
        c0 = gi * POOL_GROUP
        half = w // 2
        tot = ext_scr[8 - half:8 - half + tp, c0:c0 + POOL_GROUP]
        for dlt in range(1 - half, half):
            tot = tot + ext_scr[8 + dlt:8 + dlt + tp, c0:c0 + POOL_GROUP]
        cnt = (jnp.minimum(pos - half + w, seg_len) - jnp.maximum(pos - half, 0)).astype(f32)
        dlt_mean = (tot / cnt - ext_scr[8:8 + tp, c0:c0 + POOL_GROUP]).astype(bf16)
        y = jnp.dot(dlt_mean, w_ref[gi].astype(bf16), preferred_element_type=f32)
        o_ref[:, c0:c0 + POOL_GROUP] = (y * scale_ref[:, c0:c0 + POOL_GROUP]).astype(o_ref.dtype)


def _pool(p, pool_w, pool_scale, *, batch, seq, ctx_len):
    r, width = p.shape
    tp = ctx_len
    tpb = seq // tp
    return pl.pallas_call(
        functools.partial(_pool_kernel, tiles_per_batch=tpb, batch=batch, seq=seq, ctx_len=ctx_len),
        grid=(r // tp,),
        in_specs=[pl.BlockSpec((tp, width), lambda i: (i, 0)),
                  pl.BlockSpec((8, width), lambda i: (jnp.maximum(i * (tp // 8) - 1, 0), 0)),
                  pl.BlockSpec((8, width), lambda i: (jnp.minimum((i + 1) * (tp // 8), r // 8 - 1), 0)),
                  pl.BlockSpec(pool_w.shape, lambda i: (0, 0, 0)),
                  pl.BlockSpec((1, width), lambda i: (0, 0))],
        out_specs=pl.BlockSpec((tp, width), lambda i: (i, 0)),
        out_shape=jax.ShapeDtypeStruct((r, width), bf16),
        scratch_shapes=[pltpu.VMEM((tp + 16, width), f32)],
        compiler_params=_cparams(("arbitrary",)),
        name="pool",
    )(p, p, p, pool_w, pool_scale.reshape(1, width))


def _outproj_kernel(a_ref, b_ref, x_ref, w_ref, g1_ref, sh_ref, sc_ref, gn_ref, wr_ref, br_ref,
                    xn_ref, h2_ref, lg_ref, *, tiles_per_batch, batch):
    seg = _tile_segment(pl.program_id(0), tiles_per_batch, batch)
    half = a_ref.shape[1]
    y = (jnp.dot(a_ref[...], w_ref[0:half, :], preferred_element_type=f32)
         + jnp.dot(b_ref[...], w_ref[half:, :], preferred_element_type=f32))
    xn = x_ref[...] + g1_ref[pl.ds(seg, 1), :] * y
    xn_ref[...] = xn
    h2 = _norm_mod(xn, gn_ref[...], sh_ref[pl.ds(seg, 1), :], sc_ref[pl.ds(seg, 1), :])
    h2_ref[...] = _pack_rows(h2)
    lg_ref[...] = jnp.dot(h2.astype(bf16), wr_ref[...], preferred_element_type=f32) + br_ref[...]


def _outproj(a, b, x, w, mods, g_ffn, w_r, b_r, *, n_tiles, batch, seq):
    r, d = n_tiles * TM, x.shape[1]
    half = a.shape[1]
    row = lambda i: (i, 0)
    return pl.pallas_call(
        functools.partial(_outproj_kernel, tiles_per_batch=seq // TM, batch=batch),
        grid=(n_tiles,),
        in_specs=[pl.BlockSpec((TM, half), row), pl.BlockSpec((TM, half), row), pl.BlockSpec((TM, d), row),
                  pl.BlockSpec(w.shape, lambda i: (0, 0), pipeline_mode=pl.Buffered(1)),
                  pl.BlockSpec((8, d), lambda i: (0, 2)),
                  pl.BlockSpec((8, d), lambda i: (0, 3)),
                  pl.BlockSpec((8, d), lambda i: (0, 4)),
                  pl.BlockSpec((1, d), lambda i: (0, 0)),
                  pl.BlockSpec(w_r.shape, lambda i: (0, 0)),
                  pl.BlockSpec((1, 128), lambda i: (0, 0))],
        out_specs=[pl.BlockSpec((TM, d), row), pl.BlockSpec((TM, d // 2), row), pl.BlockSpec((TM, 128), row)],
        out_shape=[jax.ShapeDtypeStruct((r, d), f32), jax.ShapeDtypeStruct((r, d // 2), jnp.uint32),
                   jax.ShapeDtypeStruct((r, 128), f32)],
        compiler_params=_cparams(("arbitrary",)),
        name="outproj",
    )(a, b, x, w, mods, mods, mods, g_ffn, w_r, b_r)


def _moe_kernel(be_ref, nu_ref, nxt_ref, xs_ref, w1_hbm, w3_hbm, w2_hbm, ys_ref,
                w1_f32, w3_f32, w2_f32, w1_scr, w3_scr, w2_scr, sems, *, layer):
    i = pl.program_id(0)
    e = be_ref[i]
    prev = be_ref[jnp.maximum(i - 1, 0)]
    used = i < nu_ref[0]

    def weight_copies(expert):
        return (pltpu.make_async_copy(w1_hbm.at[layer, expert], w1_f32, sems.at[0]),
                pltpu.make_async_copy(w3_hbm.at[layer, expert], w3_f32, sems.at[1]),
                pltpu.make_async_copy(w2_hbm.at[layer, expert], w2_f32, sems.at[2]))

    @pl.when(i == 0)
    def _():
        for cp in weight_copies(e):
            cp.start()

    @pl.when(used & ((i == 0) | (e != prev)))
    def _():
        for cp in weight_copies(e):
            cp.wait()
        w1_scr[...] = w1_f32[...].astype(bf16)
        w3_scr[...] = w3_f32[...].astype(bf16)
        w2_scr[...] = w2_f32[...].astype(bf16)
        nxt = nxt_ref[e]

        @pl.when(nxt >= 0)
        def _():
            for cp in weight_copies(nxt):
                cp.start()

    @pl.when(used)
    def _():
        lo, hi = _unpack_rows(xs_ref[...])
        half = lo.shape[1]
        lo, hi = lo.astype(bf16), hi.astype(bf16)
        a = (jnp.dot(lo, w1_scr[0:half, :], preferred_element_type=f32)
             + jnp.dot(hi, w1_scr[half:, :], preferred_element_type=f32))
        b = (jnp.dot(lo, w3_scr[0:half, :], preferred_element_type=f32)
             + jnp.dot(hi, w3_scr[half:, :], preferred_element_type=f32))
        hid = (a * jax.nn.sigmoid(a) * b).astype(bf16)
        ys_ref[...] = _pack_rows(jnp.dot(hid, w2_scr[...], preferred_element_type=f32))

    @pl.when(jnp.logical_not(used))
    def _():
        ys_ref[...] = jnp.zeros_like(ys_ref)


def _moe_experts(xs, block_e, n_used, next_expert, w1, w3, w2, layer, nb):
    m = MOE_BLOCK
    _, _, d, de = w1.shape
    hbm = pl.BlockSpec(memory_space=pl.ANY)
    grid_spec = pltpu.PrefetchScalarGridSpec(
        num_scalar_prefetch=3,
        grid=(nb,),
        in_specs=[pl.BlockSpec((m, d // 2), lambda i, be, nu, nx: (jnp.minimum(i, nu[0] - 1), 0)), hbm, hbm, hbm],
        out_specs=pl.BlockSpec((m, d // 2), lambda i, be, nu, nx: (i, 0)),
        scratch_shapes=[pltpu.VMEM((d, de), f32), pltpu.VMEM((d, de), f32), pltpu.VMEM((de, d), f32),
                        pltpu.VMEM((d, de), bf16), pltpu.VMEM((d, de), bf16), pltpu.VMEM((de, d), bf16),
                        pltpu.SemaphoreType.DMA((3,))],
    )
    return pl.pallas_call(
        functools.partial(_moe_kernel, layer=layer),
        grid_spec=grid_spec,
        out_shape=jax.ShapeDtypeStruct((nb * m, d // 2), jnp.uint32),
        compiler_params=_cparams(("arbitrary",)),
        name="moe_experts",
    )(block_e, n_used, next_expert, xs, w1, w3, w2)


def _route_kernel(lg_ref, tri_ref, o_ref, cnt_ref, run_scr):
    @pl.when(pl.program_id(0) == 0)
    def _():
        run_scr[...] = jnp.zeros_like(run_scr)

    lg = lg_ref[...]
    lane = lax.broadcasted_iota(i32, lg.shape, 1)
    ninf = -jnp.inf

    def first_lane(mask):
        return jnp.min(jnp.where(mask, lane, lg.shape[1]), axis=-1, keepdims=True)

    gl = jnp.where(lane < MOE_GROUPS, lg, ninf)
    gmax = jnp.max(gl, axis=-1, keepdims=True)
    pg_top = 1.0 / jnp.sum(jnp.exp(gl - gmax), axis=-1, keepdims=True)
    lo = MOE_GROUPS + first_lane(gl == gmax) * MOE_PER_GROUP
    sel = (lane >= lo) & (lane < lo + MOE_PER_GROUP)
    el = jnp.where(sel, lg, ninf)
    ee = jnp.exp(el - jnp.max(el, axis=-1, keepdims=True))
    pe = jnp.where(sel, ee / jnp.sum(ee, axis=-1, keepdims=True), -1.0)
    p1 = jnp.max(pe, axis=-1, keepdims=True)
    i1 = first_lane(pe == p1)
    pe2 = jnp.where(lane == i1, -1.0, pe)
    p2 = jnp.max(pe2, axis=-1, keepdims=True)
    i2 = first_lane(pe2 == p2)
    den = p1 + p2
    e1 = i1 - MOE_GROUPS
    e2 = i2 - MOE_GROUPS
    hit1 = lane == e1
    hit2 = lane == e2
    onehot = jnp.where(hit1, 1.0, jnp.where(hit2, 1.0, 0.0))
    base = jnp.dot(tri_ref[...], onehot.astype(bf16), preferred_element_type=f32) + run_scr[...]
    r1 = jnp.sum(jnp.where(hit1, base, 0.0), axis=-1, keepdims=True)
    r2 = jnp.sum(jnp.where(hit2, base, 0.0), axis=-1, keepdims=True)
    run_scr[...] = run_scr[...] + jnp.sum(onehot, axis=0, keepdims=True)
    cnt_ref[...] = run_scr[...]
    vals = (e1.astype(f32), e2.astype(f32), pg_top * p1 / den, pg_top * p2 / den, r1, r2)
    out = jnp.zeros(lg.shape, f32)
    for j, v in enumerate(vals):
        out = jnp.where(lane == j, v, out)
    o_ref[...] = out


def _route(logits):
    t, width = logits.shape
    tri = jnp.tril(jnp.ones((TM, TM), f32), -1).astype(bf16)
    return pl.pallas_call(
        _route_kernel,
        grid=(t // TM,),
        in_specs=[pl.BlockSpec((TM, width), lambda i: (i, 0)), pl.BlockSpec((TM, TM), lambda i: (0, 0))],
        out_specs=[pl.BlockSpec((TM, width), lambda i: (i, 0)), pl.BlockSpec((1, width), lambda i: (0, 0))],
        out_shape=[jax.ShapeDtypeStruct((t, width), f32), jax.ShapeDtypeStruct((1, width), f32)],
        scratch_shapes=[pltpu.VMEM((1, width), f32)],
        compiler_params=_cparams(("arbitrary",)),
        name="route",
    )(logits, tri)


def _dispatch_kernel(dest_ref, h_ref, init_ref, xs_ref, sem):
    del init_ref
    i = pl.program_id(0)
    tm = h_ref.shape[0]

    def row_copy(r, k):
        slot = dest_ref[(i * tm + r) * MOE_TOPK + k]
        return pltpu.make_async_copy(h_ref.at[pl.ds(r, 1)], xs_ref.at[pl.ds(slot, 1)], sem)

    def issue(r, carry):
        for k in range(MOE_TOPK):
            row_copy(r, k).start(priority=k % 2)
        return carry

    lax.fori_loop(0, tm, issue, 0, unroll=8)
    for k in range(MOE_TOPK):
        pltpu.make_async_copy(h_ref, xs_ref.at[pl.ds(0, tm)], sem).wait()


def _dispatch(h2, dest, nb):
    t, d = h2.shape
    rows = nb * MOE_BLOCK
    grid_spec = pltpu.PrefetchScalarGridSpec(
        num_scalar_prefetch=1,
        grid=(t // TM,),
        in_specs=[pl.BlockSpec((TM, d), lambda i, dst: (i, 0)), pl.BlockSpec(memory_space=pl.ANY)],
        out_specs=pl.BlockSpec(memory_space=pl.ANY),
        scratch_shapes=[pltpu.SemaphoreType.DMA(())],
    )
    return pl.pallas_call(
        _dispatch_kernel,
        grid_spec=grid_spec,
        out_shape=jax.ShapeDtypeStruct((rows, d), h2.dtype),
        input_output_aliases={2: 0},
        compiler_params=_cparams(("arbitrary",)),
        name="moe_dispatch",
    )(dest, h2, jnp.zeros((rows, d), h2.dtype))


def _combine_kernel(dest_ref, x_ref, rt_ref, g2_ref, gn_ref, ys_ref, o_ref, y_scr, sem, *,
                    tiles_per_batch, batch, final_norm):
    i = pl.program_id(0)
    tm = x_ref.shape[0]

    def row_copy(r, k):
        slot = dest_ref[(i * tm + r) * MOE_TOPK + k]
        return pltpu.make_async_copy(ys_ref.at[pl.ds(slot, 1)], y_scr.at[k, pl.ds(r, 1)], sem)

    def issue(r, carry):
        for k in range(MOE_TOPK):
            row_copy(r, k).start(priority=k % 2)
        return carry

    lax.fori_loop(0, tm, issue, 0, unroll=8)
    for k in range(MOE_TOPK):
        pltpu.make_async_copy(ys_ref.at[pl.ds(0, tm)], y_scr.at[k], sem).wait()
    seg = _tile_segment(i, tiles_per_batch, batch)
    rt = rt_ref[...]
    y_lo, y_hi = 0.0, 0.0
    for k in range(MOE_TOPK):
        lo, hi = _unpack_rows(y_scr[k])
        gate = rt[:, MOE_TOPK + k:MOE_TOPK + k + 1]
        y_lo = y_lo + gate * lo
        y_hi = y_hi + gate * hi
    x = x_ref[...] + g2_ref[pl.ds(seg, 1), :] * jnp.concatenate([y_lo, y_hi], axis=1)
    if final_norm:
        ms = jnp.mean(x * x, axis=-1, keepdims=True)
        x = x * lax.rsqrt(ms + EPS) * gn_ref[...]
    o_ref[...] = x


def _combine(x, ys, dest, routed, mods, g_out, *, n_tiles, batch, seq, final_norm):
    d = x.shape[1]
    row = lambda i, dst: (i, 0)
    grid_spec = pltpu.PrefetchScalarGridSpec(
        num_scalar_prefetch=1,
        grid=(n_tiles,),
        in_specs=[pl.BlockSpec((TM, d), row),
                  pl.BlockSpec((TM, routed.shape[1]), row),
                  pl.BlockSpec((8, d), lambda i, dst: (0, 5)),
                  pl.BlockSpec((1, d), lambda i, dst: (0, 0)),
                  pl.BlockSpec(memory_space=pl.ANY)],
        out_specs=pl.BlockSpec((TM, d), row),
        scratch_shapes=[pltpu.VMEM((MOE_TOPK, TM, d // 2), jnp.uint32), pltpu.SemaphoreType.DMA(())],
    )
    return pl.pallas_call(
        functools.partial(_combine_kernel, tiles_per_batch=seq // TM, batch=batch, final_norm=final_norm),
        grid_spec=grid_spec,
        out_shape=jax.ShapeDtypeStruct((n_tiles * TM, d), f32),
        compiler_params=_cparams(("arbitrary",)),
        name="moe_combine",
    )(dest, x, routed, mods, g_out, ys)


def _moe(xn, h2, logits, mods, w1, w3, w2, layer, g_out, *, n_tiles, batch, seq, final_norm):
    t = n_tiles * TM
    m = MOE_BLOCK
    routed, cnt = _route(logits)
    s = t * MOE_TOPK
    nb = s // m + MOE_EXPERTS
    expert = routed[:, 0:MOE_TOPK].astype(i32)
    rank = routed[:, 2 * MOE_TOPK:3 * MOE_TOPK].astype(i32)
    counts = cnt[0, :MOE_EXPERTS].astype(i32)
    padded = ((counts + m - 1) // m) * m
    pad_end = jnp.cumsum(padded)
    onehot = expert[:, :, None] == jnp.arange(MOE_EXPERTS, dtype=i32)[None, None, :]
    dest = (jnp.sum(jnp.where(onehot, (pad_end - padded)[None, None, :], 0), axis=-1) + rank).reshape(s)
    n_used = (pad_end[-1] // m).astype(i32)
    blk = jnp.arange(nb, dtype=i32)
    block_e = jnp.minimum(jnp.searchsorted(pad_end, jnp.minimum(blk, n_used - 1) * m, side='right'),
                          MOE_EXPERTS - 1).astype(i32)
    ids = jnp.arange(MOE_EXPERTS, dtype=i32)
    nonempty_from = lax.cummin(jnp.where(counts > 0, ids, MOE_EXPERTS), reverse=True)
    nxt = jnp.concatenate([nonempty_from[1:], jnp.full((1,), MOE_EXPERTS, i32)])
    next_expert = jnp.where(nxt < MOE_EXPERTS, nxt, -1).astype(i32)
    xs = _dispatch(h2, dest, nb)
    ys = _moe_experts(xs, block_e, n_used.reshape(1), next_expert, w1, w3, w2, layer, nb)
    return _combine(xn, ys, dest, routed, mods, g_out, n_tiles=n_tiles, batch=batch, seq=seq,
                    final_norm=final_norm)


HY_N2 = 128
HY_WB = 256


def _short_conv_kernel(x_ref, prev_ref, next_ref, w_ref, b_ref, v_ref, x1_ref, x2_ref, ext_scr, *, tiles_per_batch):
    j = pl.program_id(0) % tiles_per_batch
    tt = x_ref.shape[0]
    ext_scr[0:8, :] = jnp.where(j > 0, prev_ref[...], 0.0)
    ext_scr[8:8 + tt, :] = x_ref[...]
    ext_scr[8 + tt:16 + tt, :] = jnp.where(j < tiles_per_batch - 1, next_ref[...], 0.0)
    w = w_ref[...]
    width = v_ref.shape[1]
    for part, o_ref in enumerate((v_ref, x1_ref, x2_ref)):
        cols = slice(part * width, (part + 1) * width)
        o_ref[...] = (ext_scr[7:7 + tt, cols] * w[0:1, cols] + ext_scr[8:8 + tt, cols] * w[1:2, cols]
                      + ext_scr[9:9 + tt, cols] * w[2:3, cols] + b_ref[:, cols])


def _short_conv(hy, w, b, *, batch, seq):
    n_lat = batch * seq
    tt = 256
    width = hy.shape[1]
    out = jax.ShapeDtypeStruct((n_lat, width // 3), f32)
    o_spec = pl.BlockSpec((tt, width // 3), lambda i: (i, 0))
    return pl.pallas_call(
        functools.partial(_short_conv_kernel, tiles_per_batch=seq // tt),
        grid=(n_lat // tt,),
        in_specs=[pl.BlockSpec((tt, width), lambda i: (i, 0)),
                  pl.BlockSpec((8, width), lambda i: (jnp.maximum(i * (tt // 8) - 1, 0), 0)),
                  pl.BlockSpec((8, width), lambda i: ((i + 1) * (tt // 8), 0)),
                  pl.BlockSpec((3, width), lambda i: (0, 0)),
                  pl.BlockSpec((1, width), lambda i: (0, 0))],
        out_specs=[o_spec, o_spec, o_spec],
        out_shape=[out, out, out],
        scratch_shapes=[pltpu.VMEM((tt + 16, width), f32)],
        compiler_params=_cparams(("arbitrary",)),
        name="short_conv",
    )(hy, hy, hy, w, b.reshape(1, width))


def _hy_filter_kernel(z_ref, w1_ref, b1_ref, fr_ref, w2_ref, b2_ref, w3_ref, dl_ref, k_ref, nrm_ref):
    i = pl.program_id(0)
    z = z_ref[...]
    fr = fr_ref[...]
    h = jnp.sin(fr[0:1] * (jnp.dot(z.astype(bf16), w1_ref[...], preferred_element_type=f32) + b1_ref[...]))
    h = jnp.sin(fr[1:2] * (jnp.dot(h.astype(bf16), w2_ref[...], preferred_element_type=f32) + b2_ref[...]))
    kk = jnp.dot(h.astype(bf16), w3_ref[...], preferred_element_type=f32)
    width = dl_ref.shape[1]
    decay = jnp.exp(-z[:, 0:1] * jnp.abs(dl_ref[...]))
    tl = z.shape[0]
    row = i * tl + lax.broadcasted_iota(i32, (tl, 1), 0)
    tot = []
    for blk in range(kk.shape[1] // width):
        kb = kk[:, blk * width:(blk + 1) * width] * decay
        if blk % 2 == 1:
            kb = jnp.where(row == 0, 0.0, kb)
        k_ref[:, blk * width:(blk + 1) * width] = kb
        tot.append(jnp.sum(jnp.abs(kb), axis=0, keepdims=True))
    s = jnp.concatenate(tot, axis=1)

    @pl.when(i == 0)
    def _():
        nrm_ref[...] = s

    @pl.when(i > 0)
    def _():
        nrm_ref[...] = nrm_ref[...] + s


def _hy_filters(seq, w1, b1, freq, w2, b2, w3):
    t = jnp.linspace(0.0, 1.0, seq, dtype=f32)[:, None]
    bands = (HY_EMB - 1) // 2
    w = 2 * math.pi * jnp.arange(seq, dtype=f32)[:, None] / seq
    f = jnp.linspace(1e-4, bands - 1, bands, dtype=f32)[None, :]
    z = jnp.concatenate([t, jnp.cos(f * w), -jnp.sin(f * w), jnp.zeros((seq, 128 - HY_EMB), f32)], axis=-1)
    w1p = jnp.concatenate([w1, jnp.zeros((128 - HY_EMB, w1.shape[1]), f32)], axis=0).astype(bf16)
    hid = w1.shape[1]
    n_out = w3.shape[1]
    deltas = jnp.linspace(math.log(HY_TARGET) / HY_SLOW_DECAY, math.log(HY_TARGET) / HY_FAST_DECAY,
                          HY_WIDTH, dtype=f32).reshape(1, HY_WIDTH)
    tl = 512
    const = lambda a: pl.BlockSpec(a.shape, lambda i: (0,) * a.ndim)
    ins = [z, w1p, b1.reshape(1, hid), freq, w2.astype(bf16), b2.reshape(1, hid), w3.astype(bf16), deltas]
    return pl.pallas_call(
        _hy_filter_kernel,
        grid=(seq // tl,),
        in_specs=[pl.BlockSpec((tl, 128), lambda i: (i, 0))] + [const(a) for a in ins[1:]],
        out_specs=[pl.BlockSpec((tl, n_out), lambda i: (i, 0)), pl.BlockSpec((1, n_out), lambda i: (0, 0))],
        out_shape=[jax.ShapeDtypeStruct((seq, n_out), f32), jax.ShapeDtypeStruct((1, n_out), f32)],
        compiler_params=_cparams(("arbitrary",)),
        name="hy_filter",
    )(*ins)


def _dft_tables(seq):
    n2 = HY_N2
    n_tot = 2 * seq
    n1 = n_tot // n2
    nf = n1 // 2 + 1
    nfp = -(-nf // 8) * 8
    f1 = jnp.arange(nfp, dtype=i32)
    k1 = jnp.arange(n1 // 2, dtype=i32)
    th = (2 * math.pi / n1) * ((f1[:, None] * k1[None, :]) % n1).astype(f32)
    valid = (f1 < nf)[:, None]
    fwd = jnp.concatenate([jnp.where(valid, jnp.cos(th), 0.0), jnp.where(valid, -jnp.sin(th), 0.0)], axis=0)
    eye = jnp.eye(8, dtype=f32)
    fk = jnp.einsum('rk,jh->rjkh', fwd, eye).reshape(2 * nfp * 8, (n1 // 2) * 8)
    cw = jnp.where(valid, jnp.where((f1 == 0) | (f1 == n1 // 2), 1.0, 2.0)[:, None], 0.0) / n_tot
    inv = jnp.concatenate([(cw * jnp.cos(th)).T, (-cw * jnp.sin(th)).T], axis=1)
    ck = jnp.einsum('kr,jh->kjrh', inv, eye).reshape((n1 // 2) * 8, 2 * nfp * 8)
    fa = jnp.arange(nf, dtype=i32)[:, None, None]
    f2 = jnp.arange(n2, dtype=i32)[None, :, None]
    m2 = jnp.arange(n2, dtype=i32)[None, None, :]
    ph = (2 * math.pi / n_tot) * ((f2 * m2 * n1 + m2 * fa) % n_tot).astype(f32)
    gr, gi = jnp.cos(ph), -jnp.sin(ph)
    g = jnp.concatenate([jnp.concatenate([gr, -gi], axis=2), jnp.concatenate([gi, gr], axis=2)], axis=1)
    hr, hi = jnp.swapaxes(jnp.cos(ph), 1, 2), jnp.swapaxes(jnp.sin(ph), 1, 2)
    h = jnp.concatenate([jnp.concatenate([hr, -hi], axis=2), jnp.concatenate([hi, hr], axis=2)], axis=1)
    return fk.astype(bf16), ck.astype(bf16), g.astype(bf16), h.astype(bf16), nf, nfp


def _hy_rows_per_step(nf):
    return max(d for d in (5, 4, 3, 2, 1) if nf % d == 0)


def _dft_stage1(x_ref, fk_ref, a_scr):
    n1h, n2, wb = x_ref.shape

    def slab(s, carry):
        rows = pl.ds(pl.multiple_of(s * 8, 8), 8)
        xs = x_ref[:, rows, :].reshape(n1h * 8, wb).astype(bf16)
        a = jnp.dot(fk_ref[...], xs, preferred_element_type=f32)
        a_scr[:, rows, :] = a.reshape(a_scr.shape[0], 8, wb)
        return carry

    lax.fori_loop(0, n2 // 8, slab, 0)


def _dft_stage2(a_scr, g, f1, nfp):
    a = jnp.concatenate([a_scr[f1], a_scr[nfp + f1]], axis=0).astype(bf16)
    return jnp.dot(g, a, preferred_element_type=f32)


def _hy_spectrum_kernel(k_ref, fk_ref, g_ref, o_ref, a_scr, *, nfp):
    fs = pl.program_id(2)

    @pl.when(fs == 0)
    def _():
        _dft_stage1(k_ref, fk_ref, a_scr)

    fb = g_ref.shape[0]
    n2 = k_ref.shape[1]
    for j in range(fb):
        s = _dft_stage2(a_scr, g_ref[j], fs * fb + j, nfp)
        o_ref[0, j] = s[:n2]
        o_ref[1, j] = s[n2:]


def _hy_spectrum(k, fk, g, nf, nfp, seq):
    n2, wb = HY_N2, HY_WB
    n1h = seq // n2
    fb = _hy_rows_per_step(nf)
    nwb = HY_WIDTH // wb
    n_blk = k.shape[1] // HY_WIDTH
    k3 = k.reshape(n1h, n2, k.shape[1])
    return pl.pallas_call(
        functools.partial(_hy_spectrum_kernel, nfp=nfp),
        grid=(n_blk, nwb, nf // fb),
        in_specs=[pl.BlockSpec((n1h, n2, wb), lambda o, w, f: (0, 0, o * nwb + w), pipeline_mode=pl.Buffered(1)),
                  pl.BlockSpec(fk.shape, lambda o, w, f: (0, 0)),
                  pl.BlockSpec((fb, 2 * n2, 2 * n2), lambda o, w, f: (f, 0, 0))],
        out_specs=pl.BlockSpec((None, 2, fb, n2, wb), lambda o, w, f: (o, 0, f, 0, w)),
        out_shape=jax.ShapeDtypeStruct((n_blk, 2, nf, n2, HY_WIDTH), f32),
        scratch_shapes=[pltpu.VMEM((2 * nfp, n2, wb), f32)],
        compiler_params=_cparams(("arbitrary",) * 3),
        name="hy_spectrum",
    )(k3, fk, g)


def _hy_conv_kernel(x_ref, gate_ref, kf_ref, kb_ref, g_ref, h_ref, fk_ref, ck_ref, bias_ref, inv_ref, o_ref, a_scr, *,
                    nfp):
    fs = pl.program_id(2)
    n1h, n2, wb = x_ref.shape

    @pl.when(fs == 0)
    def _():
        _dft_stage1(x_ref, fk_ref, a_scr)

    fb = g_ref.shape[0]
    for j in range(fb):
        f1 = fs * fb + j
        xs = _dft_stage2(a_scr, g_ref[j], f1, nfp)
        xr, xi = xs[:n2], xs[n2:]
        kr = kf_ref[0, j] + kb_ref[0, j]
        ki = kf_ref[1, j] - kb_ref[1, j]
        y = jnp.concatenate([xr * kr - xi * ki, xr * ki + xi * kr], axis=0).astype(bf16)
        bb = jnp.dot(h_ref[j], y, preferred_element_type=f32)
        a_scr[f1] = bb[:n2]
        a_scr[nfp + f1] = bb[n2:]

    @pl.when(fs == pl.num_programs(2) - 1)
    def _():
        bias = bias_ref[...].reshape(1, 1, wb)
        inv = inv_ref[...].reshape(1, 1, wb)

        def slab(s, carry):
            rows = pl.ds(pl.multiple_of(s * 8, 8), 8)
            bs = a_scr[:, rows, :].reshape(2 * nfp * 8, wb).astype(bf16)
            y = jnp.dot(ck_ref[...], bs, preferred_element_type=f32).reshape(n1h, 8, wb)
            o_ref[:, rows, :] = (gate_ref[:, rows, :] * (y * inv + x_ref[:, rows, :] * bias)).astype(o_ref.dtype)
            return carry

        lax.fori_loop(0, n2 // 8, slab, 0)


def _hy_conv(x, gate, kf, order, tabs, bias, inv_norm, out_dtype, *, batch, seq):
    fk, ck, g, h, nf, nfp = tabs
    n2, wb = HY_N2, HY_WB
    n1h = seq // n2
    fb = _hy_rows_per_step(nf)
    width = x.shape[1]
    big = pl.BlockSpec((n1h, n2, wb), lambda b, w, f: (b, 0, w), pipeline_mode=pl.Buffered(1))
    out = pl.pallas_call(
        functools.partial(_hy_conv_kernel, nfp=nfp),
        grid=(batch, width // wb, nf // fb),
        in_specs=[big, big,
                  pl.BlockSpec((None, 2, fb, n2, wb), lambda b, w, f: (2 * order, 0, f, 0, w)),
                  pl.BlockSpec((None, 2, fb, n2, wb), lambda b, w, f: (2 * order + 1, 0, f, 0, w)),
                  pl.BlockSpec((fb, 2 * n2, 2 * n2), lambda b, w, f: (f, 0, 0)),
                  pl.BlockSpec((fb, 2 * n2, 2 * n2), lambda b, w, f: (f, 0, 0)),
                  pl.BlockSpec(fk.shape, lambda b, w, f: (0, 0)),
                  pl.BlockSpec(ck.shape, lambda b, w, f: (0, 0)),
                  pl.BlockSpec((1, wb), lambda b, w, f: (0, w)),
                  pl.BlockSpec((1, wb), lambda b, w, f: (0, w))],
        out_specs=big,
        out_shape=jax.ShapeDtypeStruct((batch * n1h, n2, width), out_dtype),
        scratch_shapes=[pltpu.VMEM((2 * nfp, n2, wb), f32)],
        compiler_params=_cparams(("arbitrary",) * 3),
        name="hy_conv",
    )(x.reshape(batch * n1h, n2, width), gate.reshape(batch * n1h, n2, width), kf, kf, g, h, fk, ck,
      bias.reshape(1, width), inv_norm.reshape(1, width))
    return out.reshape(batch * seq, width)


def _hyena(hy, conv_w, conv_b, f_w1, f_b1, f_freq, f_w2, f_b2, f_w3, hy_bias, *, batch, seq):
    v, x1, x2 = _short_conv(hy, conv_w, conv_b, batch=batch, seq=seq)
    k, sums = _hy_filters(seq, f_w1, f_b1, f_freq, f_w2, f_b2, f_w3)
    sums = sums.reshape(HY_ORDER, 2, HY_WIDTH)
    inv_norm = 1.0 / (sums[:, 0] + sums[:, 1])
    tabs = _dft_tables(seq)
    kf = _hy_spectrum(k, tabs[0], tabs[2], tabs[4], tabs[5], seq)
    z = _hy_conv(v, x1, kf, 0, tabs, hy_bias[0], inv_norm[0], f32, batch=batch, seq=seq)
    return _hy_conv(z, x2, kf, 1, tabs, hy_bias[1], inv_norm[1], bf16, batch=batch, seq=seq)


S5_TT = 256
S5_GB = 8


def _gelu_tanh(x):
    return 0.5 * x * (1.0 + jnp.tanh(0.7978845608028654 * (x + 0.044715 * x * x * x)))


def _s5_kernel(u_ref, perm_ref, permt_ref, wb_ref, wc_ref, are_ref, aim_ref, pre_ref, pim_ref, *rest, fuse_glu):
    if fuse_glu:
        yf_ref, d_ref, gw_ref, gb_ref, o_ref, st_scr, bu_scr, y_scr = rest
    else:
        o_ref, st_scr, bu_scr, y_scr = rest
    tt = u_ref.shape[0]
    steps = tt // 8
    half = bu_scr.shape[1] // 2

    @pl.when(pl.program_id(1) == 0)
    def _():
        st_scr[...] = jnp.zeros_like(st_scr)

    up = jnp.dot(perm_ref[...], u_ref[...].astype(bf16), preferred_element_type=f32).astype(bf16)
    for gb in range(wb_ref.shape[0]):
        bu_scr[...] = jnp.dot(up[:, gb * 128:(gb + 1) * 128], wb_ref[gb], preferred_element_type=f32)
        a_re = jnp.broadcast_to(are_ref[gb], (8, half))
        a_im = jnp.broadcast_to(aim_ref[gb], (8, half))

        def scan_step(k, h):
            hr, hi = h
            rows = pl.ds(k * 8, 8)
            nr = a_re * hr - a_im * hi + bu_scr[rows, 0:half]
            ni = a_re * hi + a_im * hr + bu_scr[rows, half:]
            bu_scr[rows, 0:half] = nr
            bu_scr[rows, half:] = ni
            return nr, ni

        zero = jnp.zeros((8, half), f32)
        h = (zero, zero)
        for k in range(steps):
            h = scan_step(k, h)
        hr, hi = h

        pw_r = pre_ref[gb, steps - 1:steps, :]
        pw_i = pim_ref[gb, steps - 1:steps, :]
        cur_r = st_scr[gb, :, 0:half]
        cur_i = st_scr[gb, :, half:]
        car_r, car_i = [], []
        for j in range(8):
            car_r.append(cur_r)
            car_i.append(cur_i)
            cur_r, cur_i = (hr[j:j + 1] + pw_r * cur_r - pw_i * cur_i,
                            hi[j:j + 1] + pw_r * cur_i + pw_i * cur_r)
        st_scr[gb, :, 0:half] = cur_r
        st_scr[gb, :, half:] = cur_i
        cr = jnp.concatenate(car_r, axis=0)
        ci = jnp.concatenate(car_i, axis=0)

        def fix_step(k, carry):
            rows = pl.ds(k * 8, 8)
            pr = pre_ref[gb, pl.ds(k, 1), :]
            pi = pim_ref[gb, pl.ds(k, 1), :]
            bu_scr[rows, 0:half] = bu_scr[rows, 0:half] + (pr * cr - pi * ci)
            bu_scr[rows, half:] = bu_scr[rows, half:] + (pr * ci + pi * cr)
            return carry

        for k in range(steps):
            fix_step(k, 0)
        y_scr[:, gb * 128:(gb + 1) * 128] = jnp.dot(bu_scr[...].astype(bf16), wc_ref[gb],
                                                      preferred_element_type=f32)
    yp = y_scr[...]
    y_hi = yp.astype(bf16)
    y_lo = (yp - y_hi.astype(f32)).astype(bf16)
    y = (jnp.dot(permt_ref[...], y_hi, preferred_element_type=f32)
         + jnp.dot(permt_ref[...], y_lo, preferred_element_type=f32))
    if fuse_glu:
        g = _gelu_tanh(y + yf_ref[...] + u_ref[...] * d_ref[...])
        z = jnp.dot(g.astype(bf16), gw_ref[...], preferred_element_type=f32) + gb_ref[...]
        o_ref[...] = (g * jax.nn.sigmoid(z)).astype(o_ref.dtype)
    else:
        o_ref[...] = y


def _s5_tables(a_re, a_im, log_dt, b_re, b_im, c_re, c_im, steps):
    ng, ns = a_re.shape
    nb = ng // S5_GB
    lam = lax.complex(a_re, a_im)
    dt = jnp.exp(log_dt)[:, None]
    lam_bar = jnp.exp(lam * dt)
    b_bar = ((lam_bar - 1) / lam)[..., None] * lax.complex(b_re, b_im)
    cm = lax.complex(c_re, c_im)
    pw = jnp.exp((lam * dt)[None] * jnp.arange(1, steps + 1, dtype=f32)[:, None, None])
    eye = jnp.eye(S5_GB, dtype=f32)
    bb = b_bar.reshape(nb, S5_GB, ns, S5_GROUP)
    wb = jnp.concatenate([jnp.einsum('bgpc,gh->bgchp', part, eye).reshape(nb, S5_GB * S5_GROUP, S5_GB * ns)
                          for part in (jnp.real(bb), jnp.imag(bb))], axis=-1)
    cc = cm.reshape(nb, S5_GB, S5_GROUP, ns)
    wc = jnp.concatenate([jnp.einsum('bgcp,gh->bgphc', part, eye).reshape(nb, S5_GB * ns, S5_GB * S5_GROUP)
                          for part in (jnp.real(cc), -jnp.imag(cc))], axis=1)
    lb = lam_bar.reshape(nb, 1, S5_GB * ns)
    pwb = pw.reshape(steps, nb, S5_GB * ns).transpose(1, 0, 2)
    return (wb.astype(bf16), wc.astype(bf16), jnp.real(lb), jnp.imag(lb), jnp.real(pwb), jnp.imag(pwb))


def _s5_perm(tt, reverse):
    steps = tt // 8
    dst = jnp.arange(tt)
    t = (dst % 8) * steps + dst // 8
    src = tt - 1 - t if reverse else t
    return jnp.arange(tt)[None, :] == src[:, None]


def _s5(u, a_re, a_im, log_dt, b_re, b_im, c_re, c_im, s5_d, glu_w, glu_b, *, batch, seq, ctx_len):
    r, width = u.shape
    tt = S5_TT
    assert ctx_len == tt and seq % tt == 0
    n_lat = seq // tt
    ctx_blk0 = (batch * seq) // tt
    y = None
    for di in range(2):
        rev = di == 1
        wb, wc, lr, li, pr, pi = _s5_tables(a_re[di], a_im[di], log_dt[di], b_re[di], b_im[di],
                                            c_re[di], c_im[di], tt // 8)
        perm = _s5_perm(tt, rev)

        def tile_map(b, i, rev=rev):
            lat = b * n_lat + (n_lat - i if rev else i - 1)
            return (jnp.where(i == 0, ctx_blk0 + b, lat), 0)

        row_spec = pl.BlockSpec((tt, width), tile_map)
        const = lambda a: pl.BlockSpec(a.shape, lambda b, i: (0,) * a.ndim)
        ins = [u, perm.astype(bf16), perm.T.astype(bf16), wb, wc, lr, li, pr, pi]
        in_specs = [row_spec] + [const(a) for a in ins[1:]]
        if rev:
            extra = [y, s5_d.reshape(1, width), glu_w.astype(bf16), glu_b.reshape(1, width)]
            in_specs += [row_spec] + [const(a) for a in extra[1:]]
            ins += extra
        y = pl.pallas_call(
            functools.partial(_s5_kernel, fuse_glu=rev),
            grid=(batch, n_lat + 1),
            in_specs=in_specs,
            out_specs=row_spec,
            out_shape=jax.ShapeDtypeStruct((r, width), bf16 if rev else f32),
            scratch_shapes=[pltpu.VMEM((wb.shape[0], 1, wb.shape[2]), f32),
                            pltpu.VMEM((tt, wb.shape[2]), f32),
                            pltpu.VMEM((tt, width), f32)],
            compiler_params=_cparams(("arbitrary", "arbitrary")),
            name="s5_bwd_glu" if rev else "s5_fwd",
        )(*ins)
    return y


def _rope_tables(seq):
    rows = seq // GRID_W
    row = jnp.repeat(jnp.arange(rows, dtype=f32), GRID_W)
    col = jnp.tile(jnp.arange(GRID_W, dtype=f32), rows)
    n_freq = DIFF_QK // 4
    inv = ROPE_BASE ** (-jnp.arange(n_freq, dtype=f32) / n_freq)
    ar, ac = row[:, None] * inv, col[:, None] * inv
    cos = jnp.concatenate([jnp.cos(ar), jnp.cos(ar), jnp.cos(ac), jnp.cos(ac)], axis=-1)
    sin = jnp.concatenate([-jnp.sin(ar), jnp.sin(ar), -jnp.sin(ac), jnp.sin(ac)], axis=-1)
    cos = jnp.concatenate([jnp.tile(cos, (1, 2)), jnp.ones((TM, 128), f32)], axis=0)
    sin = jnp.concatenate([jnp.tile(sin, (1, 2)), jnp.zeros((TM, 128), f32)], axis=0)
    return cos, sin


def kernel(x, c, ctx, c_ctx, ada_w, ada_b, norm_mix_g, norm_ffn_g, norm_out_g, ab_w_in, ab_w_out, diff_lam, diff_subln_g, pool_w, pool_scale, cd_w_in, cd_w_out, hy_conv_w, hy_conv_b, hy_f_w1, hy_f_b1, hy_f_freq, hy_f_w2, hy_f_b2, hy_f_w3, hy_bias, s5_a_re, s5_a_im, s5_log_dt, s5_b_re, s5_b_im, s5_c_re, s5_c_im, s5_d, glu_w, glu_b, moe_w_rg, moe_b_rg, moe_w_re, moe_b_re, moe_w1, moe_w3, moe_w2):
    bsz, seq, d = x.shape
    cl = ctx.shape[1]
    depth = ada_w.shape[0]
    n_lat = bsz * seq
    assert seq % TM == 0 and (bsz * cl) % TM == 0 and bsz + 1 <= 8
    xs = jnp.concatenate([x.reshape(n_lat, d), ctx.reshape(bsz * cl, d)], axis=0)
    r = xs.shape[0]
    cond = jnp.zeros((8, d), f32).at[0:bsz].set(c).at[bsz].set(c_ctx)
    mods = _ada_mods(cond, ada_w, ada_b)
    geom = dict(batch=bsz, seq=seq)

    for l in range(depth):
        last = l == depth - 1
        i = l // 2
        n_tiles = (n_lat if last else r) // TM
        g_mix = norm_mix_g[l].reshape(1, d)
        if l % 2 == 0:
            lam_init = 0.8 - 0.6 * math.exp(-0.3 * l)
            cos, sin = _rope_tables(seq)
            q, k, vt, p = _proj_rope(xs, g_mix, mods[l], ab_w_in[i].astype(bf16), cos, sin, **geom)
            mix_a = _diff_attention(q, k, vt, diff_lam[i], diff_subln_g[i], lam_init, ctx_len=cl, **geom)
            mix_b = _pool(p, pool_w[i], pool_scale[i], ctx_len=cl, **geom)
            w_out = ab_w_out[i]
        else:
            hw = (HY_ORDER + 1) * HY_WIDTH
            hy, u = _proj(xs, g_mix, mods[l], cd_w_in[i].astype(bf16), hw, **geom)
            mix_a = _hyena(hy, hy_conv_w[i], hy_conv_b[i], hy_f_w1[i], hy_f_b1[i], hy_f_freq[i],
                           hy_f_w2[i], hy_f_b2[i], hy_f_w3[i], hy_bias[i], **geom)
            mix_b = _s5(u, s5_a_re[i], s5_a_im[i], s5_log_dt[i], s5_b_re[i], s5_b_im[i],
                        s5_c_re[i], s5_c_im[i], s5_d[i], glu_w[i], glu_b[i], ctx_len=cl, **geom)
            w_out = cd_w_out[i]
        w_r = jnp.zeros((d, 128), f32).at[:, :MOE_GROUPS].set(moe_w_rg[l]) \
            .at[:, MOE_GROUPS:MOE_GROUPS + MOE_EXPERTS].set(moe_w_re[l]).astype(bf16)
        b_r = jnp.zeros((1, 128), f32).at[0, :MOE_GROUPS].set(moe_b_rg[l]) \
            .at[0, MOE_GROUPS:MOE_GROUPS + MOE_EXPERTS].set(moe_b_re[l])
        xn, h2, logits = _outproj(mix_a, mix_b, xs, w_out.astype(bf16), mods[l], norm_ffn_g[l].reshape(1, d),
                                  w_r, b_r, n_tiles=n_tiles, **geom)
        xs = _moe(xn, h2, logits, mods[l], moe_w1, moe_w3, moe_w2, l, norm_out_g.reshape(1, d),
                  n_tiles=n_tiles, final_norm=last, **geom)
    return xs[:n_lat].reshape(bsz, seq, d)
```

```python
import functools
import math

import jax
import jax.numpy as jnp
from jax import lax
from jax.experimental import pallas as pl
from jax.experimental.pallas import tpu as pltpu

f32, bf16, i32 = jnp.float32, jnp.bfloat16, jnp.int32

EPS = 1e-6
GRID_W = 64
DIFF_QK = 64
DIFF_V = 128
ROPE_BASE = 10000.0
POOL_WINDOWS = (2, 4, 8, 16)
POOL_GROUP = 256
HY_WIDTH = 1024
HY_ORDER = 2
HY_EMB = 33
HY_FAST_DECAY = 0.3
HY_SLOW_DECAY = 1.5
HY_TARGET = 1e-2
S5_GROUP = 16
S5_GROUPS = 64
S5_STATE = 64
MOE_GROUPS = 8
MOE_PER_GROUP = 8
MOE_EXPERTS = 64
MOE_TOPK = 2
MOE_BLOCK = 256

VMEM_LIMIT = 56 * 1024 * 1024
TM = 512
LOG2E = 1.4426950408889634


def _cparams(sem):
    return pltpu.CompilerParams(dimension_semantics=sem, vmem_limit_bytes=VMEM_LIMIT)


def _norm_mod(x, g, shift, scale):
    ms = jnp.mean(x * x, axis=-1, keepdims=True)
    return (x * lax.rsqrt(ms + EPS) * g) * (1.0 + scale) + shift


def _tile_segment(i, lat_tiles_per_batch, batch):
    seg = jnp.zeros((), i32)
    for b in range(1, batch + 1):
        seg = seg + (i >= b * lat_tiles_per_batch).astype(i32)
    return seg


def _pack_rows(x):
    half = x.shape[1] // 2
    bits = lax.bitcast_convert_type(x.astype(bf16).astype(f32), jnp.uint32)
    return (bits[:, :half] >> 16) | (bits[:, half:] & jnp.uint32(0xFFFF0000))


def _unpack_rows(words):
    return (lax.bitcast_convert_type(words << 16, f32),
            lax.bitcast_convert_type(words & jnp.uint32(0xFFFF0000), f32))


def _ada_kernel(s_ref, w_ref, b_ref, o_ref):
    s = s_ref[...]
    a = (s * jax.nn.sigmoid(s)).astype(bf16)
    o_ref[...] = jnp.dot(a, w_ref[...].astype(bf16), preferred_element_type=f32) + b_ref[...]


def _ada_mods(s, ada_w, ada_b):
    depth, d, n = ada_w.shape
    tn = 1024
    return pl.pallas_call(
        _ada_kernel,
        grid=(depth, n // tn),
        in_specs=[pl.BlockSpec((8, d), lambda l, j: (0, 0)),
                  pl.BlockSpec((None, d, tn), lambda l, j: (l, 0, j)),
                  pl.BlockSpec((None, 1, tn), lambda l, j: (l, 0, j))],
        out_specs=pl.BlockSpec((None, 8, tn), lambda l, j: (l, 0, j)),
        out_shape=jax.ShapeDtypeStruct((depth, 8, n), f32),
        compiler_params=_cparams(("arbitrary", "arbitrary")),
        name="ada_mods",
    )(s, ada_w, ada_b.reshape(depth, 1, n))


def _proj_rope_kernel(x_ref, g_ref, sh_ref, sc_ref, w_ref, cos_ref, sin_ref,
                      q_ref, k_ref, v_ref, p_ref, h_scr, *, tiles_per_batch, batch, qscale):
    seg = _tile_segment(pl.program_id(0), tiles_per_batch, batch)
    h = _norm_mod(x_ref[...], g_ref[...], sh_ref[pl.ds(seg, 1), :], sc_ref[pl.ds(seg, 1), :])
    h_scr[...] = h.astype(bf16)
    cos = cos_ref[...]
    sin = sin_ref[...]
    tm = cos.shape[0]
    lane = lax.broadcasted_iota(i32, (tm, 128), 1)
    first = (lane % 32) < 16

    def rope(r):
        sw = jnp.where(first, pltpu.roll(r, 112, 1), pltpu.roll(r, 16, 1))
        return r * cos + sw * sin

    heads = q_ref.shape[0]
    nc = 512
    for c in range(w_ref.shape[1] // nc):
        res = jnp.dot(h_scr[...], w_ref[:, c * nc:(c + 1) * nc], preferred_element_type=f32)
        for s in range(nc // 128):
            r = res[:, s * 128:(s + 1) * 128]
            col = (c * nc) // 128 + s
            if col < heads:
                q_ref[col] = (rope(r) * qscale).astype(bf16)
            elif col < 2 * heads:
                k_ref[col - heads] = rope(r).astype(bf16)
            elif col < 3 * heads:
                v_ref[col - 2 * heads] = r.T.astype(bf16)
            else:
                c0 = (col - 3 * heads) * 128
                p_ref[:, c0:c0 + 128] = r


def _proj_rope(x, g, mods, w, cos, sin, *, batch, seq):
    r, d = x.shape
    heads = 8
    lat_tiles = seq // TM
    n_tiles = r // TM

    def tab_map(i):
        return (jnp.where(i < batch * lat_tiles, i % lat_tiles, lat_tiles), 0)

    hm = jax.ShapeDtypeStruct((heads, r, 128), bf16)
    hm_spec = pl.BlockSpec((heads, TM, 128), lambda i: (0, i, 0))
    return pl.pallas_call(
        functools.partial(_proj_rope_kernel, tiles_per_batch=lat_tiles, batch=batch,
                          qscale=LOG2E * DIFF_QK ** -0.5),
        grid=(n_tiles,),
        in_specs=[pl.BlockSpec((TM, d), lambda i: (i, 0)),
                  pl.BlockSpec((1, d), lambda i: (0, 0)),
                  pl.BlockSpec((8, d), lambda i: (0, 0)),
                  pl.BlockSpec((8, d), lambda i: (0, 1)),
                  pl.BlockSpec(w.shape, lambda i: (0, 0), pipeline_mode=pl.Buffered(1)),
                  pl.BlockSpec((TM, 128), tab_map),
                  pl.BlockSpec((TM, 128), tab_map)],
        out_specs=[hm_spec, hm_spec, pl.BlockSpec((heads, 128, TM), lambda i: (0, 0, i)),
                   pl.BlockSpec((TM, 1024), lambda i: (i, 0))],
        out_shape=[hm, hm, jax.ShapeDtypeStruct((heads, 128, r), bf16), jax.ShapeDtypeStruct((r, 1024), f32)],
        scratch_shapes=[pltpu.VMEM((TM, d), bf16)],
        compiler_params=_cparams(("arbitrary",)),
        name="proj_rope",
    )(x, g, mods, mods, w, cos, sin)


def _proj_kernel(x_ref, g_ref, sh_ref, sc_ref, w_ref, hy_ref, u_ref, h_scr, *, tiles_per_batch, batch):
    seg = _tile_segment(pl.program_id(0), tiles_per_batch, batch)
    h = _norm_mod(x_ref[...], g_ref[...], sh_ref[pl.ds(seg, 1), :], sc_ref[pl.ds(seg, 1), :])
    h_scr[...] = h.astype(bf16)
    nc = 512
    n_hy = hy_ref.shape[1]
    for c in range(w_ref.shape[1] // nc):
        res = jnp.dot(h_scr[...], w_ref[:, c * nc:(c + 1) * nc], preferred_element_type=f32)
        if c * nc < n_hy:
            hy_ref[:, c * nc:(c + 1) * nc] = res
        else:
            u_ref[:, c * nc - n_hy:(c + 1) * nc - n_hy] = res


def _proj(x, g, mods, w, n_hy, *, batch, seq):
    r, d = x.shape
    n = w.shape[1]
    return pl.pallas_call(
        functools.partial(_proj_kernel, tiles_per_batch=seq // TM, batch=batch),
        grid=(r // TM,),
        in_specs=[pl.BlockSpec((TM, d), lambda i: (i, 0)),
                  pl.BlockSpec((1, d), lambda i: (0, 0)),
                  pl.BlockSpec((8, d), lambda i: (0, 0)),
                  pl.BlockSpec((8, d), lambda i: (0, 1)),
                  pl.BlockSpec(w.shape, lambda i: (0, 0), pipeline_mode=pl.Buffered(1))],
        out_specs=[pl.BlockSpec((TM, n_hy), lambda i: (i, 0)), pl.BlockSpec((TM, n - n_hy), lambda i: (i, 0))],
        out_shape=[jax.ShapeDtypeStruct((r, n_hy), f32), jax.ShapeDtypeStruct((r, n - n_hy), f32)],
        scratch_shapes=[pltpu.VMEM((TM, d), bf16)],
        compiler_params=_cparams(("arbitrary",)),
        name="proj",
    )(x, g, mods, mods, w)


def _attn_kernel(lv_ref, g_ref, q_ref, kc_ref, vc_ref, *rest, n_lat, tk, lam_init):
    if n_lat:
        kl_ref, vl_ref, o_ref, s_scr = rest
    else:
        o_ref, s_scr = rest
    lv = lv_ref[...]
    lam = (jnp.exp(jnp.sum(lv[0:1] * lv[1:2], axis=-1, keepdims=True))
           - jnp.exp(jnp.sum(lv[2:3] * lv[3:4], axis=-1, keepdims=True)) + lam_init)
    q = q_ref[...]
    tq = q.shape[0]
    lane = lax.broadcasted_iota(i32, q.shape, 1)
    zero = jnp.zeros_like(q)
    qs = (jnp.where(lane < DIFF_QK, q, zero), jnp.where(lane >= DIFF_QK, q, zero))
    nt = (((1,), (1,)), ((), ()))

    def scores(slot, k):
        for m in range(2):
            s_scr[slot, m, 0:k.shape[0], :] = lax.dot_general(k, qs[m], nt, preferred_element_type=f32)

    def update(carry, slot, nk, vt):
        new = []
        for m in range(2):
            mx, l, acc = carry[3 * m:3 * m + 3]
            s = s_scr[slot, m, 0:nk, :]
            mn = jnp.maximum(mx, jnp.max(s, axis=0, keepdims=True))
            alpha = jnp.exp2(mx - mn)
            p = jnp.exp2(s - mn)
            l = alpha * l + jnp.sum(p, axis=0, keepdims=True)
            acc = alpha * acc + jnp.dot(vt, p.astype(bf16), preferred_element_type=f32)
            new += [mn, l, acc]
        return tuple(new)

    def k_block(j):
        return kl_ref[pl.ds(pl.multiple_of(j * tk, tk), tk), :]

    def vt_block(j):
        return vl_ref[:, pl.ds(pl.multiple_of(j * tk, tk), tk)]

    init = (jnp.full((1, tq), -jnp.inf, f32), jnp.zeros((1, tq), f32), jnp.zeros((DIFF_V, tq), f32)) * 2
    nc = kc_ref.shape[0]
    scores(1, kc_ref[...])
    if n_lat:
        scores(0, k_block(0))
        carry = update(init, 1, nc, vc_ref[...])

        def body(jj, carry):
            j = 4 * jj
            scores(1, k_block(j + 1))
            carry = update(carry, 0, tk, vt_block(j))
            scores(0, k_block(j + 2))
            carry = update(carry, 1, tk, vt_block(j + 1))
            scores(1, k_block(j + 3))
            carry = update(carry, 0, tk, vt_block(j + 2))
            scores(0, k_block(jnp.minimum(j + 4, n_lat - 1)))
            return update(carry, 1, tk, vt_block(j + 3))

        carry = lax.fori_loop(0, n_lat // 4, body, carry)
    else:
        carry = update(init, 1, nc, vc_ref[...])
    _, l0, a0, _, l1, a1 = carry
    o = a0 / l0 - lam * (a1 / l1)
    ms = jnp.mean(o * o, axis=0, keepdims=True)
    y = (o * lax.rsqrt(ms + EPS) * g_ref[...]) * (1.0 - lam_init)
    o_ref[...] = y.T.astype(o_ref.dtype)


def _diff_attention(q, k, vt, lam_vecs, subln_g, lam_init, *, batch, seq, ctx_len):
    heads, r, _ = q.shape
    tq, tk = 512, 512
    ctx_blk0 = (batch * seq) // ctx_len
    assert (seq // tk) % 4 == 0
    small = [pl.BlockSpec((4, DIFF_QK), lambda *_: (0, 0)), pl.BlockSpec((DIFF_V, 1), lambda *_: (0, 0))]
    kc_spec = pl.BlockSpec((None, ctx_len, 128), lambda b, h, *_: (h, ctx_blk0 + b, 0))
    vc_spec = pl.BlockSpec((None, 128, ctx_len), lambda b, h, *_: (h, 0, ctx_blk0 + b))
    g2 = subln_g.reshape(DIFF_V, 1)
    lat = pl.pallas_call(
        functools.partial(_attn_kernel, n_lat=seq // tk, tk=tk, lam_init=lam_init),
        grid=(batch, heads, seq // tq),
        in_specs=small + [pl.BlockSpec((None, tq, 128), lambda b, h, i: (h, b * (seq // tq) + i, 0)),
                          kc_spec, vc_spec,
                          pl.BlockSpec((None, seq, 128), lambda b, h, i: (h, b, 0)),
                          pl.BlockSpec((None, 128, seq), lambda b, h, i: (h, 0, b))],
        out_specs=pl.BlockSpec((tq, 128), lambda b, h, i: (b * (seq // tq) + i, h)),
        out_shape=jax.ShapeDtypeStruct((batch * seq, heads * DIFF_V), bf16),
        scratch_shapes=[pltpu.VMEM((2, 2, tk, tq), f32)],
        compiler_params=_cparams(("arbitrary",) * 3),
        name="diff_attn_lat",
    )(lam_vecs, g2, q, k, vt, k, vt)
    ctx = pl.pallas_call(
        functools.partial(_attn_kernel, n_lat=0, tk=tk, lam_init=lam_init),
        grid=(batch, heads),
        in_specs=small + [kc_spec, kc_spec, vc_spec],
        out_specs=pl.BlockSpec((ctx_len, 128), lambda b, h: (b, h)),
        out_shape=jax.ShapeDtypeStruct((batch * ctx_len, heads * DIFF_V), bf16),
        scratch_shapes=[pltpu.VMEM((2, 2, ctx_len, ctx_len), f32)],
        compiler_params=_cparams(("arbitrary",) * 2),
        name="diff_attn_ctx",
    )(lam_vecs, g2, q, k, vt)
    return jnp.concatenate([lat, ctx], axis=0)


def _pool_kernel(p_ref, prev_ref, next_ref, w_ref, scale_ref, o_ref, ext_scr, *,
                 tiles_per_batch, batch, seq, ctx_len):
    i = pl.program_id(0)
    tp = p_ref.shape[0]
    is_ctx = i >= batch * tiles_per_batch
    j = jnp.where(is_ctx, 0, i % tiles_per_batch)
    seg_len = jnp.where(is_ctx, ctx_len, seq)
    last = jnp.where(is_ctx, 0, tiles_per_batch - 1)
    ext_scr[0:8, :] = jnp.where(j > 0, prev_ref[...], 0.0)
    ext_scr[8:8 + tp, :] = p_ref[...]
    ext_scr[8 + tp:16 + tp, :] = jnp.where(j < last, next_ref[...], 0.0)
    pos = j * tp + lax.broadcasted_iota(i32, (tp, 1), 0)
    for gi, w in enumerate(POOL_WINDOWS):
        c0 = gi * POOL_GROUP
        half = w // 2
        tot = ext_scr[8 - half:8 - half + tp, c0:c0 + POOL_GROUP]
        for dlt in range(1 - half, half):
            tot = tot + ext_scr[8 + dlt:8 + dlt + tp, c0:c0 + POOL_GROUP]
        cnt = (jnp.minimum(pos - half + w, seg_len) - jnp.maximum(pos - half, 0)).astype(f32)
        dlt_mean = (tot / cnt - ext_scr[8:8 + tp, c0:c0 + POOL_GROUP]).astype(bf16)
        y = jnp.dot(dlt_mean, w_ref[gi].astype(bf16), preferred_element_type=f32)
        o_ref[:, c0:c0 + POOL_GROUP] = (y * scale_ref[:, c0:c0 + POOL_GROUP]).astype(o_ref.dtype)


def _pool(p, pool_w, pool_scale, *, batch, seq, ctx_len):
    r, width = p.shape
    tp = ctx_len
    tpb = seq // tp
    return pl.pallas_call(
        functools.partial(_pool_kernel, tiles_per_batch=tpb, batch=batch, seq=seq, ctx_len=ctx_len),
        grid=(r // tp,),
        in_specs=[pl.BlockSpec((tp, width), lambda i: (i, 0)),
                  pl.BlockSpec((8, width), lambda i: (jnp.maximum(i * (tp // 8) - 1, 0), 0)),
                  pl.BlockSpec((8, width), lambda i: (jnp.minimum((i + 1) * (tp // 8), r // 8 - 1), 0)),
                  pl.BlockSpec(pool_w.shape, lambda i: (0, 0, 0)),
                  pl.BlockSpec((1, width), lambda i: (0, 0))],
        out_specs=pl.BlockSpec((tp, width), lambda i: (i, 0)),
        out_shape=jax.ShapeDtypeStruct((r, width), bf16),
        scratch_shapes=[pltpu.VMEM((tp + 16, width), f32)],
        compiler_params=_cparams(("arbitrary",)),
        name="pool",
    )(p, p, p, pool_w, pool_scale.reshape(1, width))


def _outproj_kernel(a_ref, b_ref, x_ref, w_ref, g1_ref, sh_ref, sc_ref, gn_ref, wr_ref, br_ref,
                    xn_ref, h2_ref, lg_ref, *, tiles_per_batch, batch):
    seg = _tile_segment(pl.program_id(0), tiles_per_batch, batch)
    half = a_ref.shape[1]
    y = (jnp.dot(a_ref[...], w_ref[0:half, :], preferred_element_type=f32)
         + jnp.dot(b_ref[...], w_ref[half:, :], preferred_element_type=f32))
    xn = x_ref[...] + g1_ref[pl.ds(seg, 1), :] * y
    xn_ref[...] = xn
    h2 = _norm_mod(xn, gn_ref[...], sh_ref[pl.ds(seg, 1), :], sc_ref[pl.ds(seg, 1), :])
    h2_ref[...] = _pack_rows(h2)
    lg_ref[...] = jnp.dot(h2.astype(bf16), wr_ref[...], preferred_element_type=f32) + br_ref[...]


def _outproj(a, b, x, w, mods, g_ffn, w_r, b_r, *, n_tiles, batch, seq):
    r, d = n_tiles * TM, x.shape[1]
    half = a.shape[1]
    row = lambda i: (i, 0)
    return pl.pallas_call(
        functools.partial(_outproj_kernel, tiles_per_batch=seq // TM, batch=batch),
        grid=(n_tiles,),
        in_specs=[pl.BlockSpec((TM, half), row), pl.BlockSpec((TM, half), row), pl.BlockSpec((TM, d), row),
                  pl.BlockSpec(w.shape, lambda i: (0, 0), pipeline_mode=pl.Buffered(1)),
                  pl.BlockSpec((8, d), lambda i: (0, 2)),
                  pl.BlockSpec((8, d), lambda i: (0, 3)),
                  pl.BlockSpec((8, d), lambda i: (0, 4)),
                  pl.BlockSpec((1, d), lambda i: (0, 0)),
                  pl.BlockSpec(w_r.shape, lambda i: (0, 0)),
                  pl.BlockSpec((1, 128), lambda i: (0, 0))],
        out_specs=[pl.BlockSpec((TM, d), row), pl.BlockSpec((TM, d // 2), row), pl.BlockSpec((TM, 128), row)],
        out_shape=[jax.ShapeDtypeStruct((r, d), f32), jax.ShapeDtypeStruct((r, d // 2), jnp.uint32),
                   jax.ShapeDtypeStruct((r, 128), f32)],
        compiler_params=_cparams(("arbitrary",)),
        name="outproj",
    )(a, b, x, w, mods, mods, mods, g_ffn, w_r, b_r)


def _moe_kernel(be_ref, nu_ref, nxt_ref, xs_ref, w1_hbm, w3_hbm, w2_hbm, ys_ref,
                w1_f32, w3_f32, w2_f32, w1_scr, w3_scr, w2_scr, sems, *, layer):
    i = pl.program_id(0)
    e = be_ref[i]
    prev = be_ref[jnp.maximum(i - 1, 0)]
    used = i < nu_ref[0]

    def weight_copies(expert):
        return (pltpu.make_async_copy(w1_hbm.at[layer, expert], w1_f32, sems.at[0]),
                pltpu.make_async_copy(w3_hbm.at[layer, expert], w3_f32, sems.at[1]),
                pltpu.make_async_copy(w2_hbm.at[layer, expert], w2_f32, sems.at[2]))

    @pl.when(i == 0)
    def _():
        for cp in weight_copies(e):
            cp.start()

    @pl.when(used & ((i == 0) | (e != prev)))
    def _():
        for cp in weight_copies(e):
            cp.wait()
        w1_scr[...] = w1_f32[...].astype(bf16)
        w3_scr[...] = w3_f32[...].astype(bf16)
        w2_scr[...] = w2_f32[...].astype(bf16)
        nxt = nxt_ref[e]

        @pl.when(nxt >= 0)
        def _():
            for cp in weight_copies(nxt):
                cp.start()

    @pl.when(used)
    def _():
        lo, hi = _unpack_rows(xs_ref[...])
        half = lo.shape[1]
        lo, hi = lo.astype(bf16), hi.astype(bf16)
        a = (jnp.dot(lo, w1_scr[0:half, :], preferred_element_type=f32)
             + jnp.dot(hi, w1_scr[half:, :], preferred_element_type=f32))
        b = (jnp.dot(lo, w3_scr[0:half, :], preferred_element_type=f32)
             + jnp.dot(hi, w3_scr[half:, :], preferred_element_type=f32))
        hid = (a * jax.nn.sigmoid(a) * b).astype(bf16)
        ys_ref[...] = _pack_rows(jnp.dot(hid, w2_scr[...], preferred_element_type=f32))

    @pl.when(jnp.logical_not(used))
    def _():
        ys_ref[...] = jnp.zeros_like(ys_ref)


def _moe_experts(xs, block_e, n_used, next_expert, w1, w3, w2, layer, nb):
    m = MOE_BLOCK
    _, _, d, de = w1.shape
    hbm = pl.BlockSpec(memory_space=pl.ANY)
    grid_spec = pltpu.PrefetchScalarGridSpec(
        num_scalar_prefetch=3,
        grid=(nb,),
        in_specs=[pl.BlockSpec((m, d // 2), lambda i, be, nu, nx: (jnp.minimum(i, nu[0] - 1), 0)), hbm, hbm, hbm],
        out_specs=pl.BlockSpec((m, d // 2), lambda i, be, nu, nx: (i, 0)),
        scratch_shapes=[pltpu.VMEM((d, de), f32), pltpu.VMEM((d, de), f32), pltpu.VMEM((de, d), f32),
                        pltpu.VMEM((d, de), bf16), pltpu.VMEM((d, de), bf16), pltpu.VMEM((de, d), bf16),
                        pltpu.SemaphoreType.DMA((3,))],
    )
    return pl.pallas_call(
        functools.partial(_moe_kernel, layer=layer),
        grid_spec=grid_spec,
        out_shape=jax.ShapeDtypeStruct((nb * m, d // 2), jnp.uint32),
        compiler_params=_cparams(("arbitrary",)),
        name="moe_experts",
    )(block_e, n_used, next_expert, xs, w1, w3, w2)


def _route_kernel(lg_ref, tri_ref, o_ref, cnt_ref, run_scr):
    @pl.when(pl.program_id(0) == 0)
    def _():
        run_scr[...] = jnp.zeros_like(run_scr)

    lg = lg_ref[...]
    lane = lax.broadcasted_iota(i32, lg.shape, 1)
    ninf = -jnp.inf

    def first_lane(mask):
        return jnp.min(jnp.where(mask, lane, lg.shape[1]), axis=-1, keepdims=True)

    gl = jnp.where(lane < MOE_GROUPS, lg, ninf)
    gmax = jnp.max(gl, axis=-1, keepdims=True)
    pg_top = 1.0 / jnp.sum(jnp.exp(gl - gmax), axis=-1, keepdims=True)
    lo = MOE_GROUPS + first_lane(gl == gmax) * MOE_PER_GROUP
    sel = (lane >= lo) & (lane < lo + MOE_PER_GROUP)
    el = jnp.where(sel, lg, ninf)
    ee = jnp.exp(el - jnp.max(el, axis=-1, keepdims=True))
    pe = jnp.where(sel, ee / jnp.sum(ee, axis=-1, keepdims=True), -1.0)
    p1 = jnp.max(pe, axis=-1, keepdims=True)
    i1 = first_lane(pe == p1)
    pe2 = jnp.where(lane == i1, -1.0, pe)
    p2 = jnp.max(pe2, axis=-1, keepdims=True)
    i2 = first_lane(pe2 == p2)
    den = p1 + p2
    e1 = i1 - MOE_GROUPS
    e2 = i2 - MOE_GROUPS
    hit1 = lane == e1
    hit2 = lane == e2
    onehot = jnp.where(hit1, 1.0, jnp.where(hit2, 1.0, 0.0))
    base = jnp.dot(tri_ref[...], onehot.astype(bf16), preferred_element_type=f32) + run_scr[...]
    r1 = jnp.sum(jnp.where(hit1, base, 0.0), axis=-1, keepdims=True)
    r2 = jnp.sum(jnp.where(hit2, base, 0.0), axis=-1, keepdims=True)
    run_scr[...] = run_scr[...] + jnp.sum(onehot, axis=0, keepdims=True)
    cnt_ref[...] = run_scr[...]
    vals = (e1.astype(f32), e2.astype(f32), pg_top * p1 / den, pg_top * p2 / den, r1, r2)
    out = jnp.zeros(lg.shape, f32)
    for j, v in enumerate(vals):
        out = jnp.where(lane == j, v, out)
    o_ref[...] = out


def _route(logits):
    t, width = logits.shape
    tri = jnp.tril(jnp.ones((TM, TM), f32), -1).astype(bf16)
    return pl.pallas_call(
        _route_kernel,
        grid=(t // TM,),
        in_specs=[pl.BlockSpec((TM, width), lambda i: (i, 0)), pl.BlockSpec((TM, TM), lambda i: (0, 0))],
        out_specs=[pl.BlockSpec((TM, width), lambda i: (i, 0)), pl.BlockSpec((1, width), lambda i: (0, 0))],
        out_shape=[jax.ShapeDtypeStruct((t, width), f32), jax.ShapeDtypeStruct((1, width), f32)],
        scratch_shapes=[pltpu.VMEM((1, width), f32)],
        compiler_params=_cparams(("arbitrary",)),
        name="route",
    )(logits, tri)


def _dispatch_kernel(dest_ref, h_ref, init_ref, xs_ref, sem):
    del init_ref
    i = pl.program_id(0)
    tm = h_ref.shape[0]

    def row_copy(r, k):
        slot = dest_ref[(i * tm + r) * MOE_TOPK + k]
        return pltpu.make_async_copy(h_ref.at[pl.ds(r, 1)], xs_ref.at[pl.ds(slot, 1)], sem)

    def issue(r, carry):
        for k in range(MOE_TOPK):
            row_copy(r, k).start(priority=k % 2)
        return carry

    lax.fori_loop(0, tm, issue, 0, unroll=8)
    for k in range(MOE_TOPK):
        pltpu.make_async_copy(h_ref, xs_ref.at[pl.ds(0, tm)], sem).wait()


def _dispatch(h2, dest, nb):
    t, d = h2.shape
    rows = nb * MOE_BLOCK
    grid_spec = pltpu.PrefetchScalarGridSpec(
        num_scalar_prefetch=1,
        grid=(t // TM,),
        in_specs=[pl.BlockSpec((TM, d), lambda i, dst: (i, 0)), pl.BlockSpec(memory_space=pl.ANY)],
        out_specs=pl.BlockSpec(memory_space=pl.ANY),
        scratch_shapes=[pltpu.SemaphoreType.DMA(())],
    )
    return pl.pallas_call(
        _dispatch_kernel,
        grid_spec=grid_spec,
        out_shape=jax.ShapeDtypeStruct((rows, d), h2.dtype),
        input_output_aliases={2: 0},
        compiler_params=_cparams(("arbitrary",)),
        name="moe_dispatch",
    )(dest, h2, jnp.zeros((rows, d), h2.dtype))


def _combine_kernel(dest_ref, x_ref, rt_ref, g2_ref, gn_ref, ys_ref, o_ref, y_scr, sem, *,
                    tiles_per_batch, batch, final_norm):
    i = pl.program_id(0)
    tm = x_ref.shape[0]

    def row_copy(r, k):
        slot = dest_ref[(i * tm + r) * MOE_TOPK + k]
        return pltpu.make_async_copy(ys_ref.at[pl.ds(slot, 1)], y_scr.at[k, pl.ds(r, 1)], sem)

    def issue(r, carry):
        for k in range(MOE_TOPK):
            row_copy(r, k).start(priority=k % 2)
        return carry

    lax.fori_loop(0, tm, issue, 0, unroll=8)
    for k in range(MOE_TOPK):
        pltpu.make_async_copy(ys_ref.at[pl.ds(0, tm)], y_scr.at[k], sem).wait()
    seg = _tile_segment(i, tiles_per_batch, batch)
    rt = rt_ref[...]
    y_lo, y_hi = 0.0, 0.0
    for k in range(MOE_TOPK):
        lo, hi = _unpack_rows(y_scr[k])
        gate = rt[:, MOE_TOPK + k:MOE_TOPK + k + 1]
        y_lo = y_lo + gate * lo
        y_hi = y_hi + gate * hi
    x = x_ref[...] + g2_ref[pl.ds(seg, 1), :] * jnp.concatenate([y_lo, y_hi], axis=1)
    if final_norm:
        ms = jnp.mean(x * x, axis=-1, keepdims=True)
        x = x * lax.rsqrt(ms + EPS) * gn_ref[...]
    o_ref[...] = x


def _combine(x, ys, dest, routed, mods, g_out, *, n_tiles, batch, seq, final_norm):
    d = x.shape[1]
    row = lambda i, dst: (i, 0)
    grid_spec = pltpu.PrefetchScalarGridSpec(
        num_scalar_prefetch=1,
        grid=(n_tiles,),
        in_specs=[pl.BlockSpec((TM, d), row),
                  pl.BlockSpec((TM, routed.shape[1]), row),
                  pl.BlockSpec((8, d), lambda i, dst: (0, 5)),
                  pl.BlockSpec((1, d), lambda i, dst: (0, 0)),
                  pl.BlockSpec(memory_space=pl.ANY)],
        out_specs=pl.BlockSpec((TM, d), row),
        scratch_shapes=[pltpu.VMEM((MOE_TOPK, TM, d // 2), jnp.uint32), pltpu.SemaphoreType.DMA(())],
    )
    return pl.pallas_call(
        functools.partial(_combine_kernel, tiles_per_batch=seq // TM, batch=batch, final_norm=final_norm),
        grid_spec=grid_spec,
        out_shape=jax.ShapeDtypeStruct((n_tiles * TM, d), f32),
        compiler_params=_cparams(("arbitrary",)),
        name="moe_combine",
    )(dest, x, routed, mods, g_out, ys)


def _moe(xn, h2, logits, mods, w1, w3, w2, layer, g_out, *, n_tiles, batch, seq, final_norm):
    t = n_tiles * TM
    m = MOE_BLOCK
    routed, cnt = _route(logits)
    s = t * MOE_TOPK
    nb = s // m + MOE_EXPERTS
    expert = routed[:, 0:MOE_TOPK].astype(i32)
    rank = routed[:, 2 * MOE_TOPK:3 * MOE_TOPK].astype(i32)
    counts = cnt[0, :MOE_EXPERTS].astype(i32)
    padded = ((counts + m - 1) // m) * m
    pad_end = jnp.cumsum(padded)
    onehot = expert[:, :, None] == jnp.arange(MOE_EXPERTS, dtype=i32)[None, None, :]
    dest = (jnp.sum(jnp.where(onehot, (pad_end - padded)[None, None, :], 0), axis=-1) + rank).reshape(s)
    n_used = (pad_end[-1] // m).astype(i32)
    blk = jnp.arange(nb, dtype=i32)
    block_e = jnp.minimum(jnp.searchsorted(pad_end, jnp.minimum(blk, n_used - 1) * m, side='right'),
                          MOE_EXPERTS - 1).astype(i32)
    ids = jnp.arange(MOE_EXPERTS, dtype=i32)
    nonempty_from = lax.cummin(jnp.where(counts > 0, ids, MOE_EXPERTS), reverse=True)
    nxt = jnp.concatenate([nonempty_from[1:], jnp.full((1,), MOE_EXPERTS, i32)])
    next_expert = jnp.where(nxt < MOE_EXPERTS, nxt, -1).astype(i32)
    xs = _dispatch(h2, dest, nb)
    ys = _moe_experts(xs, block_e, n_used.reshape(1), next_expert, w1, w3, w2, layer, nb)
    return _combine(xn, ys, dest, routed, mods, g_out, n_tiles=n_tiles, batch=batch, seq=seq,
                    final_norm=final_norm)


HY_N2 = 128
HY_WB = 256


def _short_conv_kernel(x_ref, prev_ref, next_ref, w_ref, b_ref, v_ref, x1_ref, x2_ref, ext_scr, *, tiles_per_batch):
    j = pl.program_id(0) % tiles_per_batch
    tt = x_ref.shape[0]
    ext_scr[0:8, :] = jnp.where(j > 0, prev_ref[...], 0.0)
    ext_scr[8:8 + tt, :] = x_ref[...]
    ext_scr[8 + tt:16 + tt, :] = jnp.where(j < tiles_per_batch - 1, next_ref[...], 0.0)
    w = w_ref[...]
    width = v_ref.shape[1]
    for part, o_ref in enumerate((v_ref, x1_ref, x2_ref)):
        cols = slice(part * width, (part + 1) * width)
        o_ref[...] = (ext_scr[7:7 + tt, cols] * w[0:1, cols] + ext_scr[8:8 + tt, cols] * w[1:2, cols]
                      + ext_scr[9:9 + tt, cols] * w[2:3, cols] + b_ref[:, cols])


def _short_conv(hy, w, b, *, batch, seq):
    n_lat = batch * seq
    tt = 256
    width = hy.shape[1]
    out = jax.ShapeDtypeStruct((n_lat, width // 3), f32)
    o_spec = pl.BlockSpec((tt, width // 3), lambda i: (i, 0))
    return pl.pallas_call(
        functools.partial(_short_conv_kernel, tiles_per_batch=seq // tt),
        grid=(n_lat // tt,),
        in_specs=[pl.BlockSpec((tt, width), lambda i: (i, 0)),
                  pl.BlockSpec((8, width), lambda i: (jnp.maximum(i * (tt // 8) - 1, 0), 0)),
                  pl.BlockSpec((8, width), lambda i: ((i + 1) * (tt // 8), 0)),
                  pl.BlockSpec((3, width), lambda i: (0, 0)),
                  pl.BlockSpec((1, width), lambda i: (0, 0))],
        out_specs=[o_spec, o_spec, o_spec],
        out_shape=[out, out, out],
        scratch_shapes=[pltpu.VMEM((tt + 16, width), f32)],
        compiler_params=_cparams(("arbitrary",)),
        name="short_conv",
    )(hy, hy, hy, w, b.reshape(1, width))


def _hy_filter_kernel(z_ref, w1_ref, b1_ref, fr_ref, w2_ref, b2_ref, w3_ref, dl_ref, k_ref, nrm_ref):
    i = pl.program_id(0)
    z = z_ref[...]
    fr = fr_ref[...]
    h = jnp.sin(fr[0:1] * (jnp.dot(z.astype(bf16), w1_ref[...], preferred_element_type=f32) + b1_ref[...]))
    h = jnp.sin(fr[1:2] * (jnp.dot(h.astype(bf16), w2_ref[...], preferred_element_type=f32) + b2_ref[...]))
    kk = jnp.dot(h.astype(bf16), w3_ref[...], preferred_element_type=f32)
    width = dl_ref.shape[1]
    decay = jnp.exp(-z[:, 0:1] * jnp.abs(dl_ref[...]))
    tl = z.shape[0]
    row = i * tl + lax.broadcasted_iota(i32, (tl, 1), 0)
    tot = []
    for blk in range(kk.shape[1] // width):
        kb = kk[:, blk * width:(blk + 1) * width] * decay
        if blk % 2 == 1:
            kb = jnp.where(row == 0, 0.0, kb)
        k_ref[:, blk * width:(blk + 1) * width] = kb
        tot.append(jnp.sum(jnp.abs(kb), axis=0, keepdims=True))
    s = jnp.concatenate(tot, axis=1)

    @pl.when(i == 0)
    def _():
        nrm_ref[...] = s

    @pl.when(i > 0)
    def _():
        nrm_ref[...] = nrm_ref[...] + s


def _hy_filters(seq, w1, b1, freq, w2, b2, w3):
    t = jnp.linspace(0.0, 1.0, seq, dtype=f32)[:, None]
    bands = (HY_EMB - 1) // 2
    w = 2 * math.pi * jnp.arange(seq, dtype=f32)[:, None] / seq
    f = jnp.linspace(1e-4, bands - 1, bands, dtype=f32)[None, :]
    z = jnp.concatenate([t, jnp.cos(f * w), -jnp.sin(f * w), jnp.zeros((seq, 128 - HY_EMB), f32)], axis=-1)
    w1p = jnp.concatenate([w1, jnp.zeros((128 - HY_EMB, w1.shape[1]), f32)], axis=0).astype(bf16)
    hid = w1.shape[1]
    n_out = w3.shape[1]
    deltas = jnp.linspace(math.log(HY_TARGET) / HY_SLOW_DECAY, math.log(HY_TARGET) / HY_FAST_DECAY,
                          HY_WIDTH, dtype=f32).reshape(1, HY_WIDTH)
    tl = 512
    const = lambda a: pl.BlockSpec(a.shape, lambda i: (0,) * a.ndim)
    ins = [z, w1p, b1.reshape(1, hid), freq, w2.astype(bf16), b2.reshape(1, hid), w3.astype(bf16), deltas]
    return pl.pallas_call(
        _hy_filter_kernel,
        grid=(seq // tl,),
        in_specs=[pl.BlockSpec((tl, 128), lambda i: (i, 0))] + [const(a) for a in ins[1:]],
        out_specs=[pl.BlockSpec((tl, n_out), lambda i: (i, 0)), pl.BlockSpec((1, n_out), lambda i: (0, 0))],
        out_shape=[jax.ShapeDtypeStruct((seq, n_out), f32), jax.ShapeDtypeStruct((1, n_out), f32)],
        compiler_params=_cparams(("arbitrary",)),
        name="hy_filter",
    )(*ins)


def _dft_tables(seq):
    n2 = HY_N2
    n_tot = 2 * seq
    n1 = n_tot // n2
    nf = n1 // 2 + 1
    nfp = -(-nf // 8) * 8
    f1 = jnp.arange(nfp, dtype=i32)
    k1 = jnp.arange(n1 // 2, dtype=i32)
    th = (2 * math.pi / n1) * ((f1[:, None] * k1[None, :]) % n1).astype(f32)
    valid = (f1 < nf)[:, None]
    fwd = jnp.concatenate([jnp.where(valid, jnp.cos(th), 0.0), jnp.where(valid, -jnp.sin(th), 0.0)], axis=0)
    eye = jnp.eye(8, dtype=f32)
    fk = jnp.einsum('rk,jh->rjkh', fwd, eye).reshape(2 * nfp * 8, (n1 // 2) * 8)
    cw = jnp.where(valid, jnp.where((f1 == 0) | (f1 == n1 // 2), 1.0, 2.0)[:, None], 0.0) / n_tot
    inv = jnp.concatenate([(cw * jnp.cos(th)).T, (-cw * jnp.sin(th)).T], axis=1)
    ck = jnp.einsum('kr,jh->kjrh', inv, eye).reshape((n1 // 2) * 8, 2 * nfp * 8)
    fa = jnp.arange(nf, dtype=i32)[:, None, None]
    f2 = jnp.arange(n2, dtype=i32)[None, :, None]
    m2 = jnp.arange(n2, dtype=i32)[None, None, :]
    ph = (2 * math.pi / n_tot) * ((f2 * m2 * n1 + m2 * fa) % n_tot).astype(f32)
    gr, gi = jnp.cos(ph), -jnp.sin(ph)
    g = jnp.concatenate([jnp.concatenate([gr, -gi], axis=2), jnp.concatenate([gi, gr], axis=2)], axis=1)
    hr, hi = jnp.swapaxes(jnp.cos(ph), 1, 2), jnp.swapaxes(jnp.sin(ph), 1, 2)
    h = jnp.concatenate([jnp.concatenate([hr, -hi], axis=2), jnp.concatenate([hi, hr], axis=2)], axis=1)
    return fk.astype(bf16), ck.astype(bf16), g.astype(bf16), h.astype(bf16), nf, nfp


def _hy_rows_per_step(nf):
    return max(d for d in (5, 4, 3, 2, 1) if nf % d == 0)


def _dft_stage1(x_ref, fk_ref, a_scr):
    n1h, n2, wb = x_ref.shape

    def slab(s, carry):
        rows = pl.ds(pl.multiple_of(s * 8, 8), 8)
        xs = x_ref[:, rows, :].reshape(n1h * 8, wb).astype(bf16)
        a = jnp.dot(fk_ref[...], xs, preferred_element_type=f32)
        a_scr[:, rows, :] = a.reshape(a_scr.shape[0], 8, wb)
        return carry

    lax.fori_loop(0, n2 // 8, slab, 0)


def _dft_stage2(a_scr, g, f1, nfp):
    a = jnp.concatenate([a_scr[f1], a_scr[nfp + f1]], axis=0).astype(bf16)
    return jnp.dot(g, a, preferred_element_type=f32)


def _hy_spectrum_kernel(k_ref, fk_ref, g_ref, o_ref, a_scr, *, nfp):
    fs = pl.program_id(2)

    @pl.when(fs == 0)
    def _():
        _dft_stage1(k_ref, fk_ref, a_scr)

    fb = g_ref.shape[0]
    n2 = k_ref.shape[1]
    for j in range(fb):
        s = _dft_stage2(a_scr, g_ref[j], fs * fb + j, nfp)
        o_ref[0, j] = s[:n2]
        o_ref[1, j] = s[n2:]


def _hy_spectrum(k, fk, g, nf, nfp, seq):
    n2, wb = HY_N2, HY_WB
    n1h = seq // n2
    fb = _hy_rows_per_step(nf)
    nwb = HY_WIDTH // wb
    n_blk = k.shape[1] // HY_WIDTH
    k3 = k.reshape(n1h, n2, k.shape[1])
    return pl.pallas_call(
        functools.partial(_hy_spectrum_kernel, nfp=nfp),
        grid=(n_blk, nwb, nf // fb),
        in_specs=[pl.BlockSpec((n1h, n2, wb), lambda o, w, f: (0, 0, o * nwb + w), pipeline_mode=pl.Buffered(1)),
                  pl.BlockSpec(fk.shape, lambda o, w, f: (0, 0)),
                  pl.BlockSpec((fb, 2 * n2, 2 * n2), lambda o, w, f: (f, 0, 0))],
        out_specs=pl.BlockSpec((None, 2, fb, n2, wb), lambda o, w, f: (o, 0, f, 0, w)),
        out_shape=jax.ShapeDtypeStruct((n_blk, 2, nf, n2, HY_WIDTH), f32),
        scratch_shapes=[pltpu.VMEM((2 * nfp, n2, wb), f32)],
        compiler_params=_cparams(("arbitrary",) * 3),
        name="hy_spectrum",
    )(k3, fk, g)


def _hy_conv_kernel(x_ref, gate_ref, kf_ref, kb_ref, g_ref, h_ref, fk_ref, ck_ref, bias_ref, inv_ref, o_ref, a_scr, *,
                    nfp):
    fs = pl.program_id(2)
    n1h, n2, wb = x_ref.shape

    @pl.when(fs == 0)
    def _():
        _dft_stage1(x_ref, fk_ref, a_scr)

    fb = g_ref.shape[0]
    for j in range(fb):
        f1 = fs * fb + j
        xs = _dft_stage2(a_scr, g_ref[j], f1, nfp)
        xr, xi = xs[:n2], xs[n2:]
        kr = kf_ref[0, j] + kb_ref[0, j]
        ki = kf_ref[1, j] - kb_ref[1, j]
        y = jnp.concatenate([xr * kr - xi * ki, xr * ki + xi * kr], axis=0).astype(bf16)
        bb = jnp.dot(h_ref[j], y, preferred_element_type=f32)
        a_scr[f1] = bb[:n2]
        a_scr[nfp + f1] = bb[n2:]

    @pl.when(fs == pl.num_programs(2) - 1)
    def _():
        bias = bias_ref[...].reshape(1, 1, wb)
        inv = inv_ref[...].reshape(1, 1, wb)

        def slab(s, carry):
            rows = pl.ds(pl.multiple_of(s * 8, 8), 8)
            bs = a_scr[:, rows, :].reshape(2 * nfp * 8, wb).astype(bf16)
            y = jnp.dot(ck_ref[...], bs, preferred_element_type=f32).reshape(n1h, 8, wb)
            o_ref[:, rows, :] = (gate_ref[:, rows, :] * (y * inv + x_ref[:, rows, :] * bias)).astype(o_ref.dtype)
            return carry

        lax.fori_loop(0, n2 // 8, slab, 0)


def _hy_conv(x, gate, kf, order, tabs, bias, inv_norm, out_dtype, *, batch, seq):
    fk, ck, g, h, nf, nfp = tabs
    n2, wb = HY_N2, HY_WB
    n1h = seq // n2
    fb = _hy_rows_per_step(nf)
    width = x.shape[1]
    big = pl.BlockSpec((n1h, n2, wb), lambda b, w, f: (b, 0, w), pipeline_mode=pl.Buffered(1))
    out = pl.pallas_call(
        functools.partial(_hy_conv_kernel, nfp=nfp),
        grid=(batch, width // wb, nf // fb),
        in_specs=[big, big,
                  pl.BlockSpec((None, 2, fb, n2, wb), lambda b, w, f: (2 * order, 0, f, 0, w)),
                  pl.BlockSpec((None, 2, fb, n2, wb), lambda b, w, f: (2 * order + 1, 0, f, 0, w)),
                  pl.BlockSpec((fb, 2 * n2, 2 * n2), lambda b, w, f: (f, 0, 0)),
                  pl.BlockSpec((fb, 2 * n2, 2 * n2), lambda b, w, f: (f, 0, 0)),
                  pl.BlockSpec(fk.shape, lambda b, w, f: (0, 0)),
                  pl.BlockSpec(ck.shape, lambda b, w, f: (0, 0)),
                  pl.BlockSpec((1, wb), lambda b, w, f: (0, w)),
                  pl.BlockSpec((1, wb), lambda b, w, f: (0, w))],
        out_specs=big,
        out_shape=jax.ShapeDtypeStruct((batch * n1h, n2, width), out_dtype),
        scratch_shapes=[pltpu.VMEM((2 * nfp, n2, wb), f32)],
        compiler_params=_cparams(("arbitrary",) * 3),
        name="hy_conv",
    )(x.reshape(batch * n1h, n2, width), gate.reshape(batch * n1h, n2, width), kf, kf, g, h, fk, ck,
      bias.reshape(1, width), inv_norm.reshape(1, width))
    return out.reshape(batch * seq, width)


def _hyena(hy, conv_w, conv_b, f_w1, f_b1, f_freq, f_w2, f_b2, f_w3, hy_bias, *, batch, seq):
    v, x1, x2 = _short_conv(hy, conv_w, conv_b, batch=batch, seq=seq)
    k, sums = _hy_filters(seq, f_w1, f_b1, f_freq, f_w2, f_b2, f_w3)
    sums = sums.reshape(HY_ORDER, 2, HY_WIDTH)
    inv_norm = 1.0 / (sums[:, 0] + sums[:, 1])
    tabs = _dft_tables(seq)
    kf = _hy_spectrum(k, tabs[0], tabs[2], tabs[4], tabs[5], seq)
    z = _hy_conv(v, x1, kf, 0, tabs, hy_bias[0], inv_norm[0], f32, batch=batch, seq=seq)
    return _hy_conv(z, x2, kf, 1, tabs, hy_bias[1], inv_norm[1], bf16, batch=batch, seq=seq)


S5_TT = 256
S5_GB = 8


def _gelu_tanh(x):
    return 0.5 * x * (1.0 + jnp.tanh(0.7978845608028654 * (x + 0.044715 * x * x * x)))


def _s5_kernel(u_ref, perm_ref, permt_ref, wb_ref, wc_ref, are_ref, aim_ref, pre_ref, pim_ref, *rest, fuse_glu):
    if fuse_glu:
        yf_ref, d_ref, gw_ref, gb_ref, o_ref, st_scr, bu_scr, y_scr = rest
    else:
        o_ref, st_scr, bu_scr, y_scr = rest
    tt = u_ref.shape[0]
    steps = tt // 8
    half = bu_scr.shape[2] // 2

    @pl.when(pl.program_id(1) == 0)
    def _():
        st_scr[...] = jnp.zeros_like(st_scr)

    up = jnp.dot(perm_ref[...], u_ref[...].astype(bf16), preferred_element_type=f32).astype(bf16)
    lock = 4
    zero = jnp.zeros((8, half), f32)
    for g0 in range(0, wb_ref.shape[0], lock):
        gbs = range(g0, g0 + lock)
        for gb in gbs:
            bu_scr[gb] = jnp.dot(up[:, gb * 128:(gb + 1) * 128], wb_ref[gb], preferred_element_type=f32)
        lam = {gb: (jnp.broadcast_to(are_ref[gb], (8, half)), jnp.broadcast_to(aim_ref[gb], (8, half)))
               for gb in gbs}

        def scan_step(gb, k, h):
            hr, hi = h
            a_re, a_im = lam[gb]
            rows = pl.ds(k * 8, 8)
            nr = a_re * hr - a_im * hi + bu_scr[gb, rows, 0:half]
            ni = a_re * hi + a_im * hr + bu_scr[gb, rows, half:]
            bu_scr[gb, rows, 0:half] = nr
            bu_scr[gb, rows, half:] = ni
            return nr, ni

        h = {gb: (zero, zero) for gb in gbs}
        for k in range(steps):
            for gb in gbs:
                h[gb] = scan_step(gb, k, h[gb])

        carry = {}
        for gb in gbs:
            hr, hi = h[gb]
            pw_r = pre_ref[gb, steps - 1:steps, :]
            pw_i = pim_ref[gb, steps - 1:steps, :]
            cur_r = st_scr[gb, :, 0:half]
            cur_i = st_scr[gb, :, half:]
            car_r, car_i = [], []
            for j in range(8):
                car_r.append(cur_r)
                car_i.append(cur_i)
                cur_r, cur_i = (hr[j:j + 1] + pw_r * cur_r - pw_i * cur_i,
                                hi[j:j + 1] + pw_r * cur_i + pw_i * cur_r)
            st_scr[gb, :, 0:half] = cur_r
            st_scr[gb, :, half:] = cur_i
            carry[gb] = (jnp.concatenate(car_r, axis=0), jnp.concatenate(car_i, axis=0))

        for k in range(steps):
            for gb in gbs:
                cr, ci = carry[gb]
                rows = pl.ds(k * 8, 8)
                pr = pre_ref[gb, pl.ds(k, 1), :]
                pi = pim_ref[gb, pl.ds(k, 1), :]
                bu_scr[gb, rows, 0:half] = bu_scr[gb, rows, 0:half] + (pr * cr - pi * ci)
                bu_scr[gb, rows, half:] = bu_scr[gb, rows, half:] + (pr * ci + pi * cr)
        for gb in gbs:
            y_scr[:, gb * 128:(gb + 1) * 128] = jnp.dot(bu_scr[gb].astype(bf16), wc_ref[gb],
                                                          preferred_element_type=f32)
    yp = y_scr[...]
    y_hi = yp.astype(bf16)
    y_lo = (yp - y_hi.astype(f32)).astype(bf16)
    y = (jnp.dot(permt_ref[...], y_hi, preferred_element_type=f32)
         + jnp.dot(permt_ref[...], y_lo, preferred_element_type=f32))
    if fuse_glu:
        g = _gelu_tanh(y + yf_ref[...] + u_ref[...] * d_ref[...])
        z = jnp.dot(g.astype(bf16), gw_ref[...], preferred_element_type=f32) + gb_ref[...]
        o_ref[...] = (g * jax.nn.sigmoid(z)).astype(o_ref.dtype)
    else:
        o_ref[...] = y


def _s5_tables(a_re, a_im, log_dt, b_re, b_im, c_re, c_im, steps):
    ng, ns = a_re.shape
    nb = ng // S5_GB
    lam = lax.complex(a_re, a_im)
    dt = jnp.exp(log_dt)[:, None]
    lam_bar = jnp.exp(lam * dt)
    b_bar = ((lam_bar - 1) / lam)[..., None] * lax.complex(b_re, b_im)
    cm = lax.complex(c_re, c_im)
    pw = jnp.exp((lam * dt)[None] * jnp.arange(1, steps + 1, dtype=f32)[:, None, None])
    eye = jnp.eye(S5_GB, dtype=f32)
    bb = b_bar.reshape(nb, S5_GB, ns, S5_GROUP)
    wb = jnp.concatenate([jnp.einsum('bgpc,gh->bgchp', part, eye).reshape(nb, S5_GB * S5_GROUP, S5_GB * ns)
                          for part in (jnp.real(bb), jnp.imag(bb))], axis=-1)
    cc = cm.reshape(nb, S5_GB, S5_GROUP, ns)
    wc = jnp.concatenate([jnp.einsum('bgcp,gh->bgphc', part, eye).reshape(nb, S5_GB * ns, S5_GB * S5_GROUP)
                          for part in (jnp.real(cc), -jnp.imag(cc))], axis=1)
    lb = lam_bar.reshape(nb, 1, S5_GB * ns)
    pwb = pw.reshape(steps, nb, S5_GB * ns).transpose(1, 0, 2)
    return (wb.astype(bf16), wc.astype(bf16), jnp.real(lb), jnp.imag(lb), jnp.real(pwb), jnp.imag(pwb))


def _s5_perm(tt, reverse):
    steps = tt // 8
    dst = jnp.arange(tt)
    t = (dst % 8) * steps + dst // 8
    src = tt - 1 - t if reverse else t
    return jnp.arange(tt)[None, :] == src[:, None]


def _s5(u, a_re, a_im, log_dt, b_re, b_im, c_re, c_im, s5_d, glu_w, glu_b, *, batch, seq, ctx_len):
    r, width = u.shape
    tt = S5_TT
    assert ctx_len == tt and seq % tt == 0
    n_lat = seq // tt
    ctx_blk0 = (batch * seq) // tt
    y = None
    for di in range(2):
        rev = di == 1
        wb, wc, lr, li, pr, pi = _s5_tables(a_re[di], a_im[di], log_dt[di], b_re[di], b_im[di],
                                            c_re[di], c_im[di], tt // 8)
        perm = _s5_perm(tt, rev)

        def tile_map(b, i, rev=rev):
            lat = b * n_lat + (n_lat - i if rev else i - 1)
            return (jnp.where(i == 0, ctx_blk0 + b, lat), 0)

        row_spec = pl.BlockSpec((tt, width), tile_map)
        const = lambda a: pl.BlockSpec(a.shape, lambda b, i: (0,) * a.ndim)
        ins = [u, perm.astype(bf16), perm.T.astype(bf16), wb, wc, lr, li, pr, pi]
        in_specs = [row_spec] + [const(a) for a in ins[1:]]
        if rev:
            extra = [y, s5_d.reshape(1, width), glu_w.astype(bf16), glu_b.reshape(1, width)]
            in_specs += [row_spec] + [const(a) for a in extra[1:]]
            ins += extra
        y = pl.pallas_call(
            functools.partial(_s5_kernel, fuse_glu=rev),
            grid=(batch, n_lat + 1),
            in_specs=in_specs,
            out_specs=row_spec,
            out_shape=jax.ShapeDtypeStruct((r, width), bf16 if rev else f32),
            scratch_shapes=[pltpu.VMEM((wb.shape[0], 1, wb.shape[2]), f32),
                            pltpu.VMEM((wb.shape[0], tt, wb.shape[2]), f32),
                            pltpu.VMEM((tt, width), f32)],
            compiler_params=_cparams(("arbitrary", "arbitrary")),
            name="s5_bwd_glu" if rev else "s5_fwd",
        )(*ins)
    return y


def _rope_tables(seq):
    rows = seq // GRID_W
    row = jnp.repeat(jnp.arange(rows, dtype=f32), GRID_W)
    col = jnp.tile(jnp.arange(GRID_W, dtype=f32), rows)
    n_freq = DIFF_QK // 4
    inv = ROPE_BASE ** (-jnp.arange(n_freq, dtype=f32) / n_freq)
    ar, ac = row[:, None] * inv, col[:, None] * inv
    cos = jnp.concatenate([jnp.cos(ar), jnp.cos(ar), jnp.cos(ac), jnp.cos(ac)], axis=-1)
    sin = jnp.concatenate([-jnp.sin(ar), jnp.sin(ar), -jnp.sin(ac), jnp.sin(ac)], axis=-1)
    cos = jnp.concatenate([jnp.tile(cos, (1, 2)), jnp.ones((TM, 128), f32)], axis=0)
    sin = jnp.concatenate([jnp.tile(sin, (1, 2)), jnp.zeros((TM, 128), f32)], axis=0)
    return cos, sin


def kernel(x, c, ctx, c_ctx, ada_w, ada_b, norm_mix_g, norm_ffn_g, norm_out_g, ab_w_in, ab_w_out, diff_lam, diff_subln_g, pool_w, pool_scale, cd_w_in, cd_w_out, hy_conv_w, hy_conv_b, hy_f_w1, hy_f_b1, hy_f_freq, hy_f_w2, hy_f_b2, hy_f_w3, hy_bias, s5_a_re, s5_a_im, s5_log_dt, s5_b_re, s5_b_im, s5_c_re, s5_c_im, s5_d, glu_w, glu_b, moe_w_rg, moe_b_rg, moe_w_re, moe_b_re, moe_w1, moe_w3, moe_w2):
    bsz, seq, d = x.shape
    cl = ctx.shape[1]
    depth = ada_w.shape[0]
    n_lat = bsz * seq
    assert seq % TM == 0 and (bsz * cl) % TM == 0 and bsz + 1 <= 8
    xs = jnp.concatenate([x.reshape(n_lat, d), ctx.reshape(bsz * cl, d)], axis=0)
    r = xs.shape[0]
    cond = jnp.zeros((8, d), f32).at[0:bsz].set(c).at[bsz].set(c_ctx)
    mods = _ada_mods(cond, ada_w, ada_b)
    geom = dict(batch=bsz, seq=seq)

    for l in range(depth):
        last = l == depth - 1
        i = l // 2
        n_tiles = (n_lat if last else r) // TM
        g_mix = norm_mix_g[l].reshape(1, d)
        if l % 2 == 0:
            lam_init = 0.8 - 0.6 * math.exp(-0.3 * l)
            cos, sin = _rope_tables(seq)
            q, k, vt, p = _proj_rope(xs, g_mix, mods[l], ab_w_in[i].astype(bf16), cos, sin, **geom)
            mix_a = _diff_attention(q, k, vt, diff_lam[i], diff_subln_g[i], lam_init, ctx_len=cl, **geom)
            mix_b = _pool(p, pool_w[i], pool_scale[i], ctx_len=cl, **geom)
            w_out = ab_w_out[i]
        else:
            hw = (HY_ORDER + 1) * HY_WIDTH
            hy, u = _proj(xs, g_mix, mods[l], cd_w_in[i].astype(bf16), hw, **geom)
            mix_a = _hyena(hy, hy_conv_w[i], hy_conv_b[i], hy_f_w1[i], hy_f_b1[i], hy_f_freq[i],
                           hy_f_w2[i], hy_f_b2[i], hy_f_w3[i], hy_bias[i], **geom)
            mix_b = _s5(u, s5_a_re[i], s5_a_im[i], s5_log_dt[i], s5_b_re[i], s5_b_im[i],
                        s5_c_re[i], s5_c_im[i], s5_d[i], glu_w[i], glu_b[i], ctx_len=cl, **geom)
            w_out = cd_w_out[i]
        w_r = jnp.zeros((d, 128), f32).at[:, :MOE_GROUPS].set(moe_w_rg[l]) \
            .at[:, MOE_GROUPS:MOE_GROUPS + MOE_EXPERTS].set(moe_w_re[l]).astype(bf16)
        b_r = jnp.zeros((1, 128), f32).at[0, :MOE_GROUPS].set(moe_b_rg[l]) \
            .at[0, MOE_GROUPS:MOE_GROUPS + MOE_EXPERTS].set(moe_b_re[l])
        xn, h2, logits = _outproj(mix_a, mix_b, xs, w_out.astype(bf16), mods[l], norm_ffn_g[l].reshape(1, d),
                                  w_r, b_r, n_tiles=n_tiles, **geom)
        xs = _moe(xn, h2, logits, mods[l], moe_w1, moe_w3, moe_w2, l, norm_out_g.reshape(1, d),
                  n_tiles=n_tiles, final_norm=last, **geom)
    return xs[:n_lat].reshape(bsz, seq, d)
```

```python
import functools
import math

import jax
import jax.numpy as jnp
from jax import lax
from jax.experimental import pallas as pl
from jax.experimental.pallas import tpu as pltpu

f32, bf16, i32 = jnp.float32, jnp.bfloat16, jnp.int32

EPS = 1e-6
GRID_W = 64
DIFF_QK = 64
DIFF_V = 128
ROPE_BASE = 10000.0
POOL_WINDOWS = (2, 4, 8, 16)
POOL_GROUP = 256
HY_WIDTH = 1024
HY_ORDER = 2
HY_EMB = 33
HY_FAST_DECAY = 0.3
HY_SLOW_DECAY = 1.5
HY_TARGET = 1e-2
S5_GROUP = 16
S5_GROUPS = 64
S5_STATE = 64
MOE_GROUPS = 8
MOE_PER_GROUP = 8
MOE_EXPERTS = 64
MOE_TOPK = 2
MOE_BLOCK = 256

VMEM_LIMIT = 56 * 1024 * 1024
TM = 512
LOG2E = 1.4426950408889634


def _cparams(sem):
    return pltpu.CompilerParams(dimension_semantics=sem, vmem_limit_bytes=VMEM_LIMIT)


def _norm_mod(x, g, shift, scale):
    ms = jnp.mean(x * x, axis=-1, keepdims=True)
    return (x * lax.rsqrt(ms + EPS) * g) * (1.0 + scale) + shift


def _tile_segment(i, lat_tiles_per_batch, batch):
    seg = jnp.zeros((), i32)
    for b in range(1, batch + 1):
        seg = seg + (i >= b * lat_tiles_per_batch).astype(i32)
    return seg


def _pack_rows(x):
    half = x.shape[1] // 2
    bits = lax.bitcast_convert_type(x.astype(bf16).astype(f32), jnp.uint32)
    return (bits[:, :half] >> 16) | (bits[:, half:] & jnp.uint32(0xFFFF0000))


def _unpack_rows(words):
    return (lax.bitcast_convert_type(words << 16, f32),
            lax.bitcast_convert_type(words & jnp.uint32(0xFFFF0000), f32))


def _ada_kernel(s_ref, w_ref, b_ref, o_ref):
    s = s_ref[...]
    a = (s * jax.nn.sigmoid(s)).astype(bf16)
    o_ref[...] = jnp.dot(a, w_ref[...].astype(bf16), preferred_element_type=f32) + b_ref[...]


def _ada_mods(s, ada_w, ada_b):
    depth, d, n = ada_w.shape
    tn = 1024
    return pl.pallas_call(
        _ada_kernel,
        grid=(depth, n // tn),
        in_specs=[pl.BlockSpec((8, d), lambda l, j: (0, 0)),
                  pl.BlockSpec((None, d, tn), lambda l, j: (l, 0, j)),
                  pl.BlockSpec((None, 1, tn), lambda l, j: (l, 0, j))],
        out_specs=pl.BlockSpec((None, 8, tn), lambda l, j: (l, 0, j)),
        out_shape=jax.ShapeDtypeStruct((depth, 8, n), f32),
        compiler_params=_cparams(("arbitrary", "arbitrary")),
        name="ada_mods",
    )(s, ada_w, ada_b.reshape(depth, 1, n))


def _proj_rope_kernel(x_ref, c_ref, g_ref, sh_ref, sc_ref, w_ref, cos_ref, sin_ref,
                      q_ref, k_ref, v_ref, p_ref, h_scr, *, tiles_per_batch, batch, qscale):
    i = pl.program_id(0)
    seg = _tile_segment(i, tiles_per_batch, batch)
    xin = jnp.where(i < batch * tiles_per_batch, x_ref[...], c_ref[...])
    h = _norm_mod(xin, g_ref[...], sh_ref[pl.ds(seg, 1), :], sc_ref[pl.ds(seg, 1), :])
    h_scr[...] = h.astype(bf16)
    cos = cos_ref[...]
    sin = sin_ref[...]
    tm = cos.shape[0]
    lane = lax.broadcasted_iota(i32, (tm, 128), 1)
    first = (lane % 32) < 16

    def rope(r):
        sw = jnp.where(first, pltpu.roll(r, 112, 1), pltpu.roll(r, 16, 1))
        return r * cos + sw * sin

    heads = q_ref.shape[0]
    nc = 512
    for c in range(w_ref.shape[1] // nc):
        res = jnp.dot(h_scr[...], w_ref[:, c * nc:(c + 1) * nc], preferred_element_type=f32)
        for s in range(nc // 128):
            r = res[:, s * 128:(s + 1) * 128]
            col = (c * nc) // 128 + s
            if col < heads:
                q_ref[col] = (rope(r) * qscale).astype(bf16)
            elif col < 2 * heads:
                k_ref[col - heads] = rope(r).astype(bf16)
            elif col < 3 * heads:
                v_ref[col - 2 * heads] = r.T.astype(bf16)
            else:
                c0 = (col - 3 * heads) * 128
                p_ref[:, c0:c0 + 128] = r


def _stream_specs(d, n_lat_tiles, ctx_off):
    return [pl.BlockSpec((TM, d), lambda i: (jnp.minimum(i, n_lat_tiles - 1), 0)),
            pl.BlockSpec((TM, d), lambda i: (jnp.maximum(i - n_lat_tiles, 0) + ctx_off, 0))]


def _proj_rope(x, xc, ctx_off, g, mods, w, cos, sin, *, batch, seq):
    d = x.shape[1]
    heads = 8
    lat_tiles = seq // TM
    n_tiles = batch * lat_tiles + (xc.shape[0] // TM - ctx_off)
    r = n_tiles * TM

    def tab_map(i):
        return (jnp.where(i < batch * lat_tiles, i % lat_tiles, lat_tiles), 0)

    hm = jax.ShapeDtypeStruct((heads, r, 128), bf16)
    hm_spec = pl.BlockSpec((heads, TM, 128), lambda i: (0, i, 0))
    return pl.pallas_call(
        functools.partial(_proj_rope_kernel, tiles_per_batch=lat_tiles, batch=batch,
                          qscale=LOG2E * DIFF_QK ** -0.5),
        grid=(n_tiles,),
        in_specs=_stream_specs(d, batch * lat_tiles, ctx_off) + [
                  pl.BlockSpec((1, d), lambda i: (0, 0)),
                  pl.BlockSpec((8, d), lambda i: (0, 0)),
                  pl.BlockSpec((8, d), lambda i: (0, 1)),
                  pl.BlockSpec(w.shape, lambda i: (0, 0), pipeline_mode=pl.Buffered(1)),
                  pl.BlockSpec((TM, 128), tab_map),
                  pl.BlockSpec((TM, 128), tab_map)],
        out_specs=[hm_spec, hm_spec, pl.BlockSpec((heads, 128, TM), lambda i: (0, 0, i)),
                   pl.BlockSpec((TM, 1024), lambda i: (i, 0))],
        out_shape=[hm, hm, jax.ShapeDtypeStruct((heads, 128, r), bf16), jax.ShapeDtypeStruct((r, 1024), f32)],
        scratch_shapes=[pltpu.VMEM((TM, d), bf16)],
        compiler_params=_cparams(("arbitrary",)),
        name="proj_rope",
    )(x, xc, g, mods, mods, w, cos, sin)


def _proj_conv_kernel(x_ref, prev_ref, next_ref, g_ref, sh_ref, sc_ref, w_ref, cw_ref, cb_ref,
                      v_ref, x1_ref, x2_ref, u_ref, h_scr, halo_scr, ext_scr, *, tiles_per_batch, batch):
    i = pl.program_id(0)
    seg = _tile_segment(i, tiles_per_batch, batch)
    g, sh, sc = g_ref[...], sh_ref[pl.ds(seg, 1), :], sc_ref[pl.ds(seg, 1), :]
    h_scr[...] = _norm_mod(x_ref[...], g, sh, sc).astype(bf16)
    halo = jnp.concatenate([prev_ref[...], next_ref[...]], axis=0)
    halo_scr[...] = _norm_mod(halo, g, sh, sc).astype(bf16)
    j = i % tiles_per_batch
    is_lat = i < batch * tiles_per_batch
    has_prev = jnp.logical_and(is_lat, j > 0)
    has_next = jnp.logical_and(is_lat, j < tiles_per_batch - 1)
    tm = x_ref.shape[0]
    nc = ext_scr.shape[2]
    width = v_ref.shape[1]
    outs = (v_ref, x1_ref, x2_ref)
    for c in range(w_ref.shape[1] // nc):
        w_c = w_ref[:, c * nc:(c + 1) * nc]
        res = jnp.dot(h_scr[...], w_c, preferred_element_type=f32)
        part, off = divmod(c * nc, width)
        if part < len(outs):
            edge = jnp.dot(halo_scr[...], w_c, preferred_element_type=f32)
            ext = ext_scr.at[c % 2]
            ext[0:8, :] = jnp.where(has_prev, edge[0:8], 0.0)
            ext[8:8 + tm, :] = res
            ext[8 + tm:16 + tm, :] = jnp.where(has_next, edge[8:16], 0.0)
            cols = slice(c * nc, (c + 1) * nc)
            cw = cw_ref[:, cols]
            outs[part][:, off:off + nc] = (ext[7:7 + tm, :] * cw[0:1] + ext[8:8 + tm, :] * cw[1:2]
                                           + ext[9:9 + tm, :] * cw[2:3] + cb_ref[:, cols])
        else:
            u_ref[:, off:off + nc] = res


def _proj_conv(x, g, mods, w, conv_w, conv_b, *, batch, seq):
    r, d = x.shape
    n = w.shape[1]
    width = n // 4
    nc = 512
    row = pl.BlockSpec((TM, width), lambda i: (i, 0))
    out = jax.ShapeDtypeStruct((r, width), f32)
    return pl.pallas_call(
        functools.partial(_proj_conv_kernel, tiles_per_batch=seq // TM, batch=batch),
        grid=(r // TM,),
        in_specs=[pl.BlockSpec((TM, d), lambda i: (i, 0)),
                  pl.BlockSpec((8, d), lambda i: (jnp.maximum(i * (TM // 8) - 1, 0), 0)),
                  pl.BlockSpec((8, d), lambda i: (jnp.minimum((i + 1) * (TM // 8), r // 8 - 1), 0)),
                  pl.BlockSpec((1, d), lambda i: (0, 0)),
                  pl.BlockSpec((8, d), lambda i: (0, 0)),
                  pl.BlockSpec((8, d), lambda i: (0, 1)),
                  pl.BlockSpec(w.shape, lambda i: (0, 0), pipeline_mode=pl.Buffered(1)),
                  pl.BlockSpec(conv_w.shape, lambda i: (0, 0)),
                  pl.BlockSpec((1, conv_w.shape[1]), lambda i: (0, 0))],
        out_specs=[row, row, row, row],
        out_shape=[out, out, out, out],
        scratch_shapes=[pltpu.VMEM((TM, d), bf16), pltpu.VMEM((16, d), bf16), pltpu.VMEM((2, TM + 16, nc), f32)],
        compiler_params=_cparams(("arbitrary",)),
        name="proj_conv",
    )(x, x, x, g, mods, mods, w, conv_w, conv_b.reshape(1, -1))


def _attn_kernel(lv_ref, g_ref, q_ref, kc_ref, vc_ref, *rest, n_lat, tk, lam_init):
    if n_lat:
        kl_ref, vl_ref, o_ref, s_scr = rest
    else:
        o_ref, s_scr = rest
    lv = lv_ref[...]
    lam = (jnp.exp(jnp.sum(lv[0:1] * lv[1:2], axis=-1, keepdims=True))
           - jnp.exp(jnp.sum(lv[2:3] * lv[3:4], axis=-1, keepdims=True)) + lam_init)
    q = q_ref[...]
    tq = q.shape[0]
    lane = lax.broadcasted_iota(i32, q.shape, 1)
    zero = jnp.zeros_like(q)
    qs = (jnp.where(lane < DIFF_QK, q, zero), jnp.where(lane >= DIFF_QK, q, zero))
    nt = (((1,), (1,)), ((), ()))

    def scores(slot, k):
        for m in range(2):
            s_scr[slot, m, 0:k.shape[0], :] = lax.dot_general(k, qs[m], nt, preferred_element_type=f32)

    def update(carry, slot, nk, vt):
        new = []
        for m in range(2):
            mx, l, acc = carry[3 * m:3 * m + 3]
            s = s_scr[slot, m, 0:nk, :]
            mn = jnp.maximum(mx, jnp.max(s, axis=0, keepdims=True))
            alpha = jnp.exp2(mx - mn)
            p = jnp.exp2(s - mn)
            l = alpha * l + jnp.sum(p, axis=0, keepdims=True)
            acc = alpha * acc + jnp.dot(vt, p.astype(bf16), preferred_element_type=f32)
            new += [mn, l, acc]
        return tuple(new)

    def k_block(j):
        return kl_ref[pl.ds(pl.multiple_of(j * tk, tk), tk), :]

    def vt_block(j):
        return vl_ref[:, pl.ds(pl.multiple_of(j * tk, tk), tk)]

    init = (jnp.full((1, tq), -jnp.inf, f32), jnp.zeros((1, tq), f32), jnp.zeros((DIFF_V, tq), f32)) * 2
    nc = kc_ref.shape[0]
    scores(1, kc_ref[...])
    if n_lat:
        scores(0, k_block(0))
        carry = update(init, 1, nc, vc_ref[...])

        def body(jj, carry):
            j = 4 * jj
            scores(1, k_block(j + 1))
            carry = update(carry, 0, tk, vt_block(j))
            scores(0, k_block(j + 2))
            carry = update(carry, 1, tk, vt_block(j + 1))
            scores(1, k_block(j + 3))
            carry = update(carry, 0, tk, vt_block(j + 2))
            scores(0, k_block(jnp.minimum(j + 4, n_lat - 1)))
            return update(carry, 1, tk, vt_block(j + 3))

        carry = lax.fori_loop(0, n_lat // 4, body, carry)
    else:
        carry = update(init, 1, nc, vc_ref[...])
    _, l0, a0, _, l1, a1 = carry
    o = a0 / l0 - lam * (a1 / l1)
    ms = jnp.mean(o * o, axis=0, keepdims=True)
    y = (o * lax.rsqrt(ms + EPS) * g_ref[...]) * (1.0 - lam_init)
    o_ref[...] = y.T.astype(o_ref.dtype)


def _diff_attention(q, k, vt, lam_vecs, subln_g, lam_init, *, batch, seq, ctx_len):
    heads, r, _ = q.shape
    tq, tk = 512, 512
    ctx_blk0 = (batch * seq) // ctx_len
    assert (seq // tk) % 4 == 0
    small = [pl.BlockSpec((4, DIFF_QK), lambda *_: (0, 0)), pl.BlockSpec((DIFF_V, 1), lambda *_: (0, 0))]
    kc_spec = pl.BlockSpec((None, ctx_len, 128), lambda b, h, *_: (h, ctx_blk0 + b, 0))
    vc_spec = pl.BlockSpec((None, 128, ctx_len), lambda b, h, *_: (h, 0, ctx_blk0 + b))
    g2 = subln_g.reshape(DIFF_V, 1)
    lat = pl.pallas_call(
        functools.partial(_attn_kernel, n_lat=seq // tk, tk=tk, lam_init=lam_init),
        grid=(batch, heads, seq // tq),
        in_specs=small + [pl.BlockSpec((None, tq, 128), lambda b, h, i: (h, b * (seq // tq) + i, 0)),
                          kc_spec, vc_spec,
                          pl.BlockSpec((None, seq, 128), lambda b, h, i: (h, b, 0)),
                          pl.BlockSpec((None, 128, seq), lambda b, h, i: (h, 0, b))],
        out_specs=pl.BlockSpec((tq, 128), lambda b, h, i: (b * (seq // tq) + i, h)),
        out_shape=jax.ShapeDtypeStruct((batch * seq, heads * DIFF_V), bf16),
        scratch_shapes=[pltpu.VMEM((2, 2, tk, tq), f32)],
        compiler_params=_cparams(("arbitrary",) * 3),
        name="diff_attn_lat",
    )(lam_vecs, g2, q, k, vt, k, vt)
    ctx = pl.pallas_call(
        functools.partial(_attn_kernel, n_lat=0, tk=tk, lam_init=lam_init),
        grid=(batch, heads),
        in_specs=small + [kc_spec, kc_spec, vc_spec],
        out_specs=pl.BlockSpec((ctx_len, 128), lambda b, h: (b, h)),
        out_shape=jax.ShapeDtypeStruct((batch * ctx_len, heads * DIFF_V), bf16),
        scratch_shapes=[pltpu.VMEM((2, 2, ctx_len, ctx_len), f32)],
        compiler_params=_cparams(("arbitrary",) * 2),
        name="diff_attn_ctx",
    )(lam_vecs, g2, q, k, vt)
    return jnp.concatenate([lat, ctx], axis=0)


def _pool_kernel(p_ref, prev_ref, next_ref, w_ref, scale_ref, o_ref, ext_scr, *,
                 tiles_per_batch, batch, seq, ctx_len):
    i = pl.program_id(0)
    tp = p_ref.shape[0]
    is_ctx = i >= batch * tiles_per_batch
    j = jnp.where(is_ctx, 0, i % tiles_per_batch)
    seg_len = jnp.where(is_ctx, ctx_len, seq)
    last = jnp.where(is_ctx, 0, tiles_per_batch - 1)
    ext_scr[0:8, :] = jnp.where(j > 0, prev_ref[...], 0.0)
    ext_scr[8:8 + tp, :] = p_ref[...]
    ext_scr[8 + tp:16 + tp, :] = jnp.where(j < last, next_ref[...], 0.0)
    pos = j * tp + lax.broadcasted_iota(i32, (tp, 1), 0)
    for gi, w in enumerate(POOL_WINDOWS):
        c0 = gi * POOL_GROUP
        half = w // 2
        tot = ext_scr[8 - half:8 - half + tp, c0:c0 + POOL_GROUP]
        for dlt in range(1 - half, half):
            tot = tot + ext_scr[8 + dlt:8 + dlt + tp, c0:c0 + POOL_GROUP]
        cnt = (jnp.minimum(pos - half + w, seg_len) - jnp.maximum(pos - half, 0)).astype(f32)
        dlt_mean = (tot / cnt - ext_scr[8:8 + tp, c0:c0 + POOL_GROUP]).astype(bf16)
        y = jnp.dot(dlt_mean, w_ref[gi].astype(bf16), preferred_element_type=f32)
        o_ref[:, c0:c0 + POOL_GROUP] = (y * scale_ref[:, c0:c0 + POOL_GROUP]).astype(o_ref.dtype)


def _pool(p, pool_w, pool_scale, *, batch, seq, ctx_len):
    r, width = p.shape
    tp = ctx_len
    tpb = seq // tp
    return pl.pallas_call(
        functools.partial(_pool_kernel, tiles_per_batch=tpb, batch=batch, seq=seq, ctx_len=ctx_len),
        grid=(r // tp,),
        in_specs=[pl.BlockSpec((tp, width), lambda i: (i, 0)),
                  pl.BlockSpec((8, width), lambda i: (jnp.maximum(i * (tp // 8) - 1, 0), 0)),
                  pl.BlockSpec((8, width), lambda i: (jnp.minimum((i + 1) * (tp // 8), r // 8 - 1), 0)),
                  pl.BlockSpec(pool_w.shape, lambda i: (0, 0, 0)),
                  pl.BlockSpec((1, width), lambda i: (0, 0))],
        out_specs=pl.BlockSpec((tp, width), lambda i: (i, 0)),
        out_shape=jax.ShapeDtypeStruct((r, width), bf16),
        scratch_shapes=[pltpu.VMEM((tp + 16, width), f32)],
        compiler_params=_cparams(("arbitrary",)),
        name="pool",
    )(p, p, p, pool_w, pool_scale.reshape(1, width))


def _outproj_kernel(a_ref, b_ref, x_ref, c_ref, w_ref, g1_ref, sh_ref, sc_ref, gn_ref, wr_ref, br_ref,
                    xn_ref, h2_ref, lg_ref, *, tiles_per_batch, batch):
    i = pl.program_id(0)
    seg = _tile_segment(i, tiles_per_batch, batch)
    xin = jnp.where(i < batch * tiles_per_batch, x_ref[...], c_ref[...])
    half = a_ref.shape[1]
    y = (jnp.dot(a_ref[...], w_ref[0:half, :], preferred_element_type=f32)
         + jnp.dot(b_ref[...], w_ref[half:, :], preferred_element_type=f32))
    xn = xin + g1_ref[pl.ds(seg, 1), :] * y
    xn_ref[...] = xn
    h2 = _norm_mod(xn, gn_ref[...], sh_ref[pl.ds(seg, 1), :], sc_ref[pl.ds(seg, 1), :])
    h2_ref[...] = _pack_rows(h2)
    lg_ref[...] = jnp.dot(h2.astype(bf16), wr_ref[...], preferred_element_type=f32) + br_ref[...]


def _outproj(a, b, x, xc, ctx_off, w, mods, g_ffn, w_r, b_r, *, n_tiles, batch, seq):
    r, d = n_tiles * TM, x.shape[1]
    half = a.shape[1]
    row = lambda i: (i, 0)
    return pl.pallas_call(
        functools.partial(_outproj_kernel, tiles_per_batch=seq // TM, batch=batch),
        grid=(n_tiles,),
        in_specs=[pl.BlockSpec((TM, half), row), pl.BlockSpec((TM, half), row)]
        + _stream_specs(d, batch * (seq // TM), ctx_off) + [
                  pl.BlockSpec(w.shape, lambda i: (0, 0), pipeline_mode=pl.Buffered(1)),
                  pl.BlockSpec((8, d), lambda i: (0, 2)),
                  pl.BlockSpec((8, d), lambda i: (0, 3)),
                  pl.BlockSpec((8, d), lambda i: (0, 4)),
                  pl.BlockSpec((1, d), lambda i: (0, 0)),
                  pl.BlockSpec(w_r.shape, lambda i: (0, 0)),
                  pl.BlockSpec((1, 128), lambda i: (0, 0))],
        out_specs=[pl.BlockSpec((TM, d), row), pl.BlockSpec((TM, d // 2), row), pl.BlockSpec((TM, 128), row)],
        out_shape=[jax.ShapeDtypeStruct((r, d), f32), jax.ShapeDtypeStruct((r, d // 2), jnp.uint32),
                   jax.ShapeDtypeStruct((r, 128), f32)],
        compiler_params=_cparams(("arbitrary",)),
        name="outproj",
    )(a, b, x, xc, w, mods, mods, mods, g_ffn, w_r, b_r)


def _moe_kernel(be_ref, nu_ref, nxt_ref, xs_ref, w1_hbm, w3_hbm, w2_hbm, ys_ref,
                w1_f32, w3_f32, w2_f32, w1_scr, w3_scr, w2_scr, sems, *, layer):
    i = pl.program_id(0)
    e = be_ref[i]
    prev = be_ref[jnp.maximum(i - 1, 0)]
    used = i < nu_ref[0]

    def weight_copies(expert):
        return (pltpu.make_async_copy(w1_hbm.at[layer, expert], w1_f32, sems.at[0]),
                pltpu.make_async_copy(w3_hbm.at[layer, expert], w3_f32, sems.at[1]),
                pltpu.make_async_copy(w2_hbm.at[layer, expert], w2_f32, sems.at[2]))

    @pl.when(i == 0)
    def _():
        for cp in weight_copies(e):
            cp.start()

    @pl.when(used & ((i == 0) | (e != prev)))
    def _():
        for cp in weight_copies(e):
            cp.wait()
        w1_scr[...] = w1_f32[...].astype(bf16)
        w3_scr[...] = w3_f32[...].astype(bf16)
        w2_scr[...] = w2_f32[...].astype(bf16)
        nxt = nxt_ref[e]

        @pl.when(nxt >= 0)
        def _():
            for cp in weight_copies(nxt):
                cp.start()

    @pl.when(used)
    def _():
        lo, hi = _unpack_rows(xs_ref[...])
        half = lo.shape[1]
        lo, hi = lo.astype(bf16), hi.astype(bf16)
        a = (jnp.dot(lo, w1_scr[0:half, :], preferred_element_type=f32)
             + jnp.dot(hi, w1_scr[half:, :], preferred_element_type=f32))
        b = (jnp.dot(lo, w3_scr[0:half, :], preferred_element_type=f32)
             + jnp.dot(hi, w3_scr[half:, :], preferred_element_type=f32))
        hid = (a * jax.nn.sigmoid(a) * b).astype(bf16)
        ys_ref[...] = _pack_rows(jnp.dot(hid, w2_scr[...], preferred_element_type=f32))

    @pl.when(jnp.logical_not(used))
    def _():
        ys_ref[...] = jnp.zeros_like(ys_ref)


def _moe_experts(xs, block_e, n_used, next_expert, w1, w3, w2, layer, nb):
    m = MOE_BLOCK
    _, _, d, de = w1.shape
    hbm = pl.BlockSpec(memory_space=pl.ANY)
    grid_spec = pltpu.PrefetchScalarGridSpec(
        num_scalar_prefetch=3,
        grid=(nb,),
        in_specs=[pl.BlockSpec((m, d // 2), lambda i, be, nu, nx: (jnp.minimum(i, nu[0] - 1), 0)), hbm, hbm, hbm],
        out_specs=pl.BlockSpec((m, d // 2), lambda i, be, nu, nx: (i, 0)),
        scratch_shapes=[pltpu.VMEM((d, de), f32), pltpu.VMEM((d, de), f32), pltpu.VMEM((de, d), f32),
                        pltpu.VMEM((d, de), bf16), pltpu.VMEM((d, de), bf16), pltpu.VMEM((de, d), bf16),
                        pltpu.SemaphoreType.DMA((3,))],
    )
    return pl.pallas_call(
        functools.partial(_moe_kernel, layer=layer),
        grid_spec=grid_spec,
        out_shape=jax.ShapeDtypeStruct((nb * m, d // 2), jnp.uint32),
        compiler_params=_cparams(("arbitrary",)),
        name="moe_experts",
    )(block_e, n_used, next_expert, xs, w1, w3, w2)


def _route_kernel(lg_ref, tri_ref, o_ref, cnt_ref, run_scr):
    @pl.when(pl.program_id(0) == 0)
    def _():
        run_scr[...] = jnp.zeros_like(run_scr)

    lg = lg_ref[...]
    lane = lax.broadcasted_iota(i32, lg.shape, 1)
    ninf = -jnp.inf

    def first_lane(mask):
        return jnp.min(jnp.where(mask, lane, lg.shape[1]), axis=-1, keepdims=True)

    gl = jnp.where(lane < MOE_GROUPS, lg, ninf)
    gmax = jnp.max(gl, axis=-1, keepdims=True)
    pg_top = 1.0 / jnp.sum(jnp.exp(gl - gmax), axis=-1, keepdims=True)
    lo = MOE_GROUPS + first_lane(gl == gmax) * MOE_PER_GROUP
    sel = (lane >= lo) & (lane < lo + MOE_PER_GROUP)
    el = jnp.where(sel, lg, ninf)
    ee = jnp.exp(el - jnp.max(el, axis=-1, keepdims=True))
    pe = jnp.where(sel, ee / jnp.sum(ee, axis=-1, keepdims=True), -1.0)
    p1 = jnp.max(pe, axis=-1, keepdims=True)
    i1 = first_lane(pe == p1)
    pe2 = jnp.where(lane == i1, -1.0, pe)
    p2 = jnp.max(pe2, axis=-1, keepdims=True)
    i2 = first_lane(pe2 == p2)
    den = p1 + p2
    e1 = i1 - MOE_GROUPS
    e2 = i2 - MOE_GROUPS
    hit1 = lane == e1
    hit2 = lane == e2
    onehot = jnp.where(hit1, 1.0, jnp.where(hit2, 1.0, 0.0))
    base = jnp.dot(tri_ref[...], onehot.astype(bf16), preferred_element_type=f32) + run_scr[...]
    r1 = jnp.sum(jnp.where(hit1, base, 0.0), axis=-1, keepdims=True)
    r2 = jnp.sum(jnp.where(hit2, base, 0.0), axis=-1, keepdims=True)
    run_scr[...] = run_scr[...] + jnp.sum(onehot, axis=0, keepdims=True)
    cnt_ref[...] = run_scr[...]
    vals = (e1.astype(f32), e2.astype(f32), pg_top * p1 / den, pg_top * p2 / den, r1, r2)
    out = jnp.zeros(lg.shape, f32)
    for j, v in enumerate(vals):
        out = jnp.where(lane == j, v, out)
    o_ref[...] = out


def _route(logits):
    t, width = logits.shape
    tri = jnp.tril(jnp.ones((TM, TM), f32), -1).astype(bf16)
    return pl.pallas_call(
        _route_kernel,
        grid=(t // TM,),
        in_specs=[pl.BlockSpec((TM, width), lambda i: (i, 0)), pl.BlockSpec((TM, TM), lambda i: (0, 0))],
        out_specs=[pl.BlockSpec((TM, width), lambda i: (i, 0)), pl.BlockSpec((1, width), lambda i: (0, 0))],
        out_shape=[jax.ShapeDtypeStruct((t, width), f32), jax.ShapeDtypeStruct((1, width), f32)],
        scratch_shapes=[pltpu.VMEM((1, width), f32)],
        compiler_params=_cparams(("arbitrary",)),
        name="route",
    )(logits, tri)


def _dispatch_kernel(dest_ref, h_ref, init_ref, xs_ref, sem):
    del init_ref
    i = pl.program_id(0)
    tm = h_ref.shape[0]

    def row_copy(r, k):
        slot = dest_ref[(i * tm + r) * MOE_TOPK + k]
        return pltpu.make_async_copy(h_ref.at[pl.ds(r, 1)], xs_ref.at[pl.ds(slot, 1)], sem)

    def issue(r, carry):
        for k in range(MOE_TOPK):
            row_copy(r, k).start(priority=k % 2)
        return carry

    lax.fori_loop(0, tm, issue, 0, unroll=8)
    for k in range(MOE_TOPK):
        pltpu.make_async_copy(h_ref, xs_ref.at[pl.ds(0, tm)], sem).wait()


def _dispatch(h2, dest, nb):
    t, d = h2.shape
    rows = nb * MOE_BLOCK
    grid_spec = pltpu.PrefetchScalarGridSpec(
        num_scalar_prefetch=1,
        grid=(t // TM,),
        in_specs=[pl.BlockSpec((TM, d), lambda i, dst: (i, 0)), pl.BlockSpec(memory_space=pl.ANY)],
        out_specs=pl.BlockSpec(memory_space=pl.ANY),
        scratch_shapes=[pltpu.SemaphoreType.DMA(())],
    )
    return pl.pallas_call(
        _dispatch_kernel,
        grid_spec=grid_spec,
        out_shape=jax.ShapeDtypeStruct((rows, d), h2.dtype),
        input_output_aliases={2: 0},
        compiler_params=_cparams(("arbitrary",)),
        name="moe_dispatch",
    )(dest, h2, jnp.zeros((rows, d), h2.dtype))


def _combine_kernel(dest_ref, x_ref, rt_ref, g2_ref, gn_ref, ys_ref, o_ref, y_scr, sem, *,
                    tiles_per_batch, batch, final_norm):
    i = pl.program_id(0)
    tm = x_ref.shape[0]

    def row_copy(r, k):
        slot = dest_ref[(i * tm + r) * MOE_TOPK + k]
        return pltpu.make_async_copy(ys_ref.at[pl.ds(slot, 1)], y_scr.at[k, pl.ds(r, 1)], sem)

    def issue(r, carry):
        for k in range(MOE_TOPK):
            row_copy(r, k).start(priority=k % 2)
        return carry

    lax.fori_loop(0, tm, issue, 0, unroll=8)
    for k in range(MOE_TOPK):
        pltpu.make_async_copy(ys_ref.at[pl.ds(0, tm)], y_scr.at[k], sem).wait()
    seg = _tile_segment(i, tiles_per_batch, batch)
    rt = rt_ref[...]
    y_lo, y_hi = 0.0, 0.0
    for k in range(MOE_TOPK):
        lo, hi = _unpack_rows(y_scr[k])
        gate = rt[:, MOE_TOPK + k:MOE_TOPK + k + 1]
        y_lo = y_lo + gate * lo
        y_hi = y_hi + gate * hi
    x = x_ref[...] + g2_ref[pl.ds(seg, 1), :] * jnp.concatenate([y_lo, y_hi], axis=1)
    if final_norm:
        ms = jnp.mean(x * x, axis=-1, keepdims=True)
        x = x * lax.rsqrt(ms + EPS) * gn_ref[...]
    o_ref[...] = x


def _combine(x, ys, dest, routed, mods, g_out, *, n_tiles, batch, seq, final_norm):
    d = x.shape[1]
    row = lambda i, dst: (i, 0)
    grid_spec = pltpu.PrefetchScalarGridSpec(
        num_scalar_prefetch=1,
        grid=(n_tiles,),
        in_specs=[pl.BlockSpec((TM, d), row),
                  pl.BlockSpec((TM, routed.shape[1]), row),
                  pl.BlockSpec((8, d), lambda i, dst: (0, 5)),
                  pl.BlockSpec((1, d), lambda i, dst: (0, 0)),
                  pl.BlockSpec(memory_space=pl.ANY)],
        out_specs=pl.BlockSpec((TM, d), row),
        scratch_shapes=[pltpu.VMEM((MOE_TOPK, TM, d // 2), jnp.uint32), pltpu.SemaphoreType.DMA(())],
    )
    return pl.pallas_call(
        functools.partial(_combine_kernel, tiles_per_batch=seq // TM, batch=batch, final_norm=final_norm),
        grid_spec=grid_spec,
        out_shape=jax.ShapeDtypeStruct((n_tiles * TM, d), f32),
        compiler_params=_cparams(("arbitrary",)),
        name="moe_combine",
    )(dest, x, routed, mods, g_out, ys)


def _moe(xn, h2, logits, mods, w1, w3, w2, layer, g_out, *, n_tiles, batch, seq, final_norm):
    t = n_tiles * TM
    m = MOE_BLOCK
    routed, cnt = _route(logits)
    s = t * MOE_TOPK
    nb = s // m + MOE_EXPERTS
    expert = routed[:, 0:MOE_TOPK].astype(i32)
    rank = routed[:, 2 * MOE_TOPK:3 * MOE_TOPK].astype(i32)
    counts = cnt[0, :MOE_EXPERTS].astype(i32)
    padded = ((counts + m - 1) // m) * m
    pad_end = jnp.cumsum(padded)
    onehot = expert[:, :, None] == jnp.arange(MOE_EXPERTS, dtype=i32)[None, None, :]
    dest = (jnp.sum(jnp.where(onehot, (pad_end - padded)[None, None, :], 0), axis=-1) + rank).reshape(s)
    n_used = (pad_end[-1] // m).astype(i32)
    blk = jnp.arange(nb, dtype=i32)
    block_e = jnp.minimum(jnp.searchsorted(pad_end, jnp.minimum(blk, n_used - 1) * m, side='right'),
                          MOE_EXPERTS - 1).astype(i32)
    ids = jnp.arange(MOE_EXPERTS, dtype=i32)
    nonempty_from = lax.cummin(jnp.where(counts > 0, ids, MOE_EXPERTS), reverse=True)
    nxt = jnp.concatenate([nonempty_from[1:], jnp.full((1,), MOE_EXPERTS, i32)])
    next_expert = jnp.where(nxt < MOE_EXPERTS, nxt, -1).astype(i32)
    xs = _dispatch(h2, dest, nb)
    ys = _moe_experts(xs, block_e, n_used.reshape(1), next_expert, w1, w3, w2, layer, nb)
    return _combine(xn, ys, dest, routed, mods, g_out, n_tiles=n_tiles, batch=batch, seq=seq,
                    final_norm=final_norm)


HY_N2 = 128
HY_WB = 256


def _hy_filter_kernel(z_ref, w1_ref, b1_ref, fr_ref, w2_ref, b2_ref, w3_ref, dl_ref, k_ref, nrm_ref):
    i = pl.program_id(0)
    z = z_ref[...]
    fr = fr_ref[...]
    h = jnp.sin(fr[0:1] * (jnp.dot(z.astype(bf16), w1_ref[...], preferred_element_type=f32) + b1_ref[...]))
    h = jnp.sin(fr[1:2] * (jnp.dot(h.astype(bf16), w2_ref[...], preferred_element_type=f32) + b2_ref[...]))
    kk = jnp.dot(h.astype(bf16), w3_ref[...], preferred_element_type=f32)
    width = dl_ref.shape[1]
    decay = jnp.exp(-z[:, 0:1] * jnp.abs(dl_ref[...]))
    tl = z.shape[0]
    row = i * tl + lax.broadcasted_iota(i32, (tl, 1), 0)
    tot = []
    for blk in range(kk.shape[1] // width):
        kb = kk[:, blk * width:(blk + 1) * width] * decay
        if blk % 2 == 1:
            kb = jnp.where(row == 0, 0.0, kb)
        k_ref[:, blk * width:(blk + 1) * width] = kb
        tot.append(jnp.sum(jnp.abs(kb), axis=0, keepdims=True))
    s = jnp.concatenate(tot, axis=1)

    @pl.when(i == 0)
    def _():
        nrm_ref[...] = s

    @pl.when(i > 0)
    def _():
        nrm_ref[...] = nrm_ref[...] + s


def _hy_filters(seq, w1, b1, freq, w2, b2, w3):
    t = jnp.linspace(0.0, 1.0, seq, dtype=f32)[:, None]
    bands = (HY_EMB - 1) // 2
    w = 2 * math.pi * jnp.arange(seq, dtype=f32)[:, None] / seq
    f = jnp.linspace(1e-4, bands - 1, bands, dtype=f32)[None, :]
    z = jnp.concatenate([t, jnp.cos(f * w), -jnp.sin(f * w), jnp.zeros((seq, 128 - HY_EMB), f32)], axis=-1)
    w1p = jnp.concatenate([w1, jnp.zeros((128 - HY_EMB, w1.shape[1]), f32)], axis=0).astype(bf16)
    hid = w1.shape[1]
    n_out = w3.shape[1]
    deltas = jnp.linspace(math.log(HY_TARGET) / HY_SLOW_DECAY, math.log(HY_TARGET) / HY_FAST_DECAY,
                          HY_WIDTH, dtype=f32).reshape(1, HY_WIDTH)
    tl = 512
    const = lambda a: pl.BlockSpec(a.shape, lambda i: (0,) * a.ndim)
    ins = [z, w1p, b1.reshape(1, hid), freq, w2.astype(bf16), b2.reshape(1, hid), w3.astype(bf16), deltas]
    return pl.pallas_call(
        _hy_filter_kernel,
        grid=(seq // tl,),
        in_specs=[pl.BlockSpec((tl, 128), lambda i: (i, 0))] + [const(a) for a in ins[1:]],
        out_specs=[pl.BlockSpec((tl, n_out), lambda i: (i, 0)), pl.BlockSpec((1, n_out), lambda i: (0, 0))],
        out_shape=[jax.ShapeDtypeStruct((seq, n_out), f32), jax.ShapeDtypeStruct((1, n_out), f32)],
        compiler_params=_cparams(("arbitrary",)),
        name="hy_filter",
    )(*ins)


def _dft_tables(seq):
    n2 = HY_N2
    n_tot = 2 * seq
    n1 = n_tot // n2
    nf = n1 // 2 + 1
    nfp = -(-nf // 8) * 8
    f1 = jnp.arange(nfp, dtype=i32)
    k1 = jnp.arange(n1 // 2, dtype=i32)
    th = (2 * math.pi / n1) * ((f1[:, None] * k1[None, :]) % n1).astype(f32)
    valid = (f1 < nf)[:, None]
    fwd = jnp.concatenate([jnp.where(valid, jnp.cos(th), 0.0), jnp.where(valid, -jnp.sin(th), 0.0)], axis=0)
    eye = jnp.eye(8, dtype=f32)
    fk = jnp.einsum('rk,jh->rjkh', fwd, eye).reshape(2 * nfp * 8, (n1 // 2) * 8)
    cw = jnp.where(valid, jnp.where((f1 == 0) | (f1 == n1 // 2), 1.0, 2.0)[:, None], 0.0) / n_tot
    inv = jnp.concatenate([(cw * jnp.cos(th)).T, (-cw * jnp.sin(th)).T], axis=1)
    ck = jnp.einsum('kr,jh->kjrh', inv, eye).reshape((n1 // 2) * 8, 2 * nfp * 8)
    fa = jnp.arange(nf, dtype=i32)[:, None, None]
    f2 = jnp.arange(n2, dtype=i32)[None, :, None]
    m2 = jnp.arange(n2, dtype=i32)[None, None, :]
    ph = (2 * math.pi / n_tot) * ((f2 * m2 * n1 + m2 * fa) % n_tot).astype(f32)
    gr, gi = jnp.cos(ph), -jnp.sin(ph)
    g = jnp.concatenate([jnp.concatenate([gr, -gi], axis=2), jnp.concatenate([gi, gr], axis=2)], axis=1)
    hr, hi = jnp.swapaxes(jnp.cos(ph), 1, 2), jnp.swapaxes(jnp.sin(ph), 1, 2)
    h = jnp.concatenate([jnp.concatenate([hr, -hi], axis=2), jnp.concatenate([hi, hr], axis=2)], axis=1)
    return fk.astype(bf16), ck.astype(bf16), g.astype(bf16), h.astype(bf16), nf, nfp


def _hy_rows_per_step(nf):
    return max(d for d in (5, 4, 3, 2, 1) if nf % d == 0)


def _dft_stage1(x_ref, fk_ref, a_scr):
    n1h, n2, wb = x_ref.shape

    def slab(s, carry):
        rows = pl.ds(pl.multiple_of(s * 8, 8), 8)
        xs = x_ref[:, rows, :].reshape(n1h * 8, wb).astype(bf16)
        a = jnp.dot(fk_ref[...], xs, preferred_element_type=f32)
        a_scr[:, rows, :] = a.reshape(a_scr.shape[0], 8, wb)
        return carry

    lax.fori_loop(0, n2 // 8, slab, 0)


def _dft_stage2(a_scr, g, f1, nfp):
    a = jnp.concatenate([a_scr[f1], a_scr[nfp + f1]], axis=0).astype(bf16)
    return jnp.dot(g, a, preferred_element_type=f32)


def _hy_spectrum_kernel(k_ref, fk_ref, g_ref, o_ref, a_scr, *, nfp):
    fs = pl.program_id(2)

    @pl.when(fs == 0)
    def _():
        _dft_stage1(k_ref, fk_ref, a_scr)

    fb = g_ref.shape[0]
    n2 = k_ref.shape[1]
    for j in range(fb):
        s = _dft_stage2(a_scr, g_ref[j], fs * fb + j, nfp)
        o_ref[0, j] = s[:n2]
        o_ref[1, j] = s[n2:]


def _hy_spectrum(k, fk, g, nf, nfp, seq):
    n2, wb = HY_N2, HY_WB
    n1h = seq // n2
    fb = _hy_rows_per_step(nf)
    nwb = HY_WIDTH // wb
    n_blk = k.shape[1] // HY_WIDTH
    k3 = k.reshape(n1h, n2, k.shape[1])
    return pl.pallas_call(
        functools.partial(_hy_spectrum_kernel, nfp=nfp),
        grid=(n_blk, nwb, nf // fb),
        in_specs=[pl.BlockSpec((n1h, n2, wb), lambda o, w, f: (0, 0, o * nwb + w), pipeline_mode=pl.Buffered(1)),
                  pl.BlockSpec(fk.shape, lambda o, w, f: (0, 0)),
                  pl.BlockSpec((fb, 2 * n2, 2 * n2), lambda o, w, f: (f, 0, 0))],
        out_specs=pl.BlockSpec((None, 2, fb, n2, wb), lambda o, w, f: (o, 0, f, 0, w)),
        out_shape=jax.ShapeDtypeStruct((n_blk, 2, nf, n2, HY_WIDTH), f32),
        scratch_shapes=[pltpu.VMEM((2 * nfp, n2, wb), f32)],
        compiler_params=_cparams(("arbitrary",) * 3),
        name="hy_spectrum",
    )(k3, fk, g)


def _hy_conv_kernel(x_ref, gate_ref, kf_ref, kb_ref, g_ref, h_ref, fk_ref, ck_ref, bias_ref, inv_ref, o_ref, a_scr, *,
                    nfp):
    fs = pl.program_id(2)
    n1h, n2, wb = x_ref.shape

    @pl.when(fs == 0)
    def _():
        _dft_stage1(x_ref, fk_ref, a_scr)

    fb = g_ref.shape[0]
    for j in range(fb):
        f1 = fs * fb + j
        xs = _dft_stage2(a_scr, g_ref[j], f1, nfp)
        xr, xi = xs[:n2], xs[n2:]
        kr = kf_ref[0, j] + kb_ref[0, j]
        ki = kf_ref[1, j] - kb_ref[1, j]
        y = jnp.concatenate([xr * kr - xi * ki, xr * ki + xi * kr], axis=0).astype(bf16)
        bb = jnp.dot(h_ref[j], y, preferred_element_type=f32)
        a_scr[f1] = bb[:n2]
        a_scr[nfp + f1] = bb[n2:]

    @pl.when(fs == pl.num_programs(2) - 1)
    def _():
        bias = bias_ref[...].reshape(1, 1, wb)
        inv = inv_ref[...].reshape(1, 1, wb)

        def slab(s, carry):
            rows = pl.ds(pl.multiple_of(s * 8, 8), 8)
            bs = a_scr[:, rows, :].reshape(2 * nfp * 8, wb).astype(bf16)
            y = jnp.dot(ck_ref[...], bs, preferred_element_type=f32).reshape(n1h, 8, wb)
            o_ref[:, rows, :] = (gate_ref[:, rows, :] * (y * inv + x_ref[:, rows, :] * bias)).astype(o_ref.dtype)
            return carry

        lax.fori_loop(0, n2 // 8, slab, 0)


def _hy_conv(x, gate, kf, order, tabs, bias, inv_norm, out_dtype, *, batch, seq):
    fk, ck, g, h, nf, nfp = tabs
    n2, wb = HY_N2, HY_WB
    n1h = seq // n2
    fb = _hy_rows_per_step(nf)
    width = x.shape[1]
    big = pl.BlockSpec((n1h, n2, wb), lambda b, w, f: (b, 0, w), pipeline_mode=pl.Buffered(1))
    out = pl.pallas_call(
        functools.partial(_hy_conv_kernel, nfp=nfp),
        grid=(batch, width // wb, nf // fb),
        in_specs=[big, big,
                  pl.BlockSpec((None, 2, fb, n2, wb), lambda b, w, f: (2 * order, 0, f, 0, w)),
                  pl.BlockSpec((None, 2, fb, n2, wb), lambda b, w, f: (2 * order + 1, 0, f, 0, w)),
                  pl.BlockSpec((fb, 2 * n2, 2 * n2), lambda b, w, f: (f, 0, 0)),
                  pl.BlockSpec((fb, 2 * n2, 2 * n2), lambda b, w, f: (f, 0, 0)),
                  pl.BlockSpec(fk.shape, lambda b, w, f: (0, 0)),
                  pl.BlockSpec(ck.shape, lambda b, w, f: (0, 0)),
                  pl.BlockSpec((1, wb), lambda b, w, f: (0, w)),
                  pl.BlockSpec((1, wb), lambda b, w, f: (0, w))],
        out_specs=big,
        out_shape=jax.ShapeDtypeStruct((batch * n1h, n2, width), out_dtype),
        scratch_shapes=[pltpu.VMEM((2 * nfp, n2, wb), f32)],
        compiler_params=_cparams(("arbitrary",) * 3),
        name="hy_conv",
    )(x.reshape(-1, n2, width), gate.reshape(-1, n2, width), kf, kf, g, h, fk, ck,
      bias.reshape(1, width), inv_norm.reshape(1, width))
    return out.reshape(batch * seq, width)


def _hyena(v, x1, x2, f_w1, f_b1, f_freq, f_w2, f_b2, f_w3, hy_bias, *, batch, seq):
    k, sums = _hy_filters(seq, f_w1, f_b1, f_freq, f_w2, f_b2, f_w3)
    sums = sums.reshape(HY_ORDER, 2, HY_WIDTH)
    inv_norm = 1.0 / (sums[:, 0] + sums[:, 1])
    tabs = _dft_tables(seq)
    kf = _hy_spectrum(k, tabs[0], tabs[2], tabs[4], tabs[5], seq)
    z = _hy_conv(v, x1, kf, 0, tabs, hy_bias[0], inv_norm[0], f32, batch=batch, seq=seq)
    return _hy_conv(z, x2, kf, 1, tabs, hy_bias[1], inv_norm[1], bf16, batch=batch, seq=seq)


S5_TT = 256
S5_GB = 8


def _gelu_tanh(x):
    return 0.5 * x * (1.0 + jnp.tanh(0.7978845608028654 * (x + 0.044715 * x * x * x)))


def _s5_kernel(u_ref, perm_ref, permt_ref, wb_ref, wc_ref, are_ref, aim_ref, pre_ref, pim_ref, *rest, fuse_glu):
    if fuse_glu:
        yf_ref, d_ref, gw_ref, gb_ref, o_ref, st_scr, bu_scr, y_scr = rest
    else:
        o_ref, st_scr, bu_scr, y_scr = rest
    tt = u_ref.shape[0]
    steps = tt // 8
    half = bu_scr.shape[2] // 2

    @pl.when(pl.program_id(1) == 0)
    def _():
        st_scr[...] = jnp.zeros_like(st_scr)

    up = jnp.dot(perm_ref[...], u_ref[...].astype(bf16), preferred_element_type=f32).astype(bf16)
    lock = 4
    zero = jnp.zeros((8, half), f32)
    for g0 in range(0, wb_ref.shape[0], lock):
        gbs = range(g0, g0 + lock)
        for gb in gbs:
            bu_scr[gb] = jnp.dot(up[:, gb * 128:(gb + 1) * 128], wb_ref[gb], preferred_element_type=f32)
        lam = {gb: (jnp.broadcast_to(are_ref[gb], (8, half)), jnp.broadcast_to(aim_ref[gb], (8, half)))
               for gb in gbs}

        def scan_step(gb, k, h):
            hr, hi = h
            a_re, a_im = lam[gb]
            rows = pl.ds(k * 8, 8)
            nr = a_re * hr - a_im * hi + bu_scr[gb, rows, 0:half]
            ni = a_re * hi + a_im * hr + bu_scr[gb, rows, half:]
            bu_scr[gb, rows, 0:half] = nr
            bu_scr[gb, rows, half:] = ni
            return nr, ni

        h = {gb: (zero, zero) for gb in gbs}
        for k in range(steps):
            for gb in gbs:
                h[gb] = scan_step(gb, k, h[gb])

        carry = {}
        for gb in gbs:
            hr, hi = h[gb]
            pw_r = pre_ref[gb, steps - 1:steps, :]
            pw_i = pim_ref[gb, steps - 1:steps, :]
            cur_r = st_scr[gb, :, 0:half]
            cur_i = st_scr[gb, :, half:]
            car_r, car_i = [], []
            for j in range(8):
                car_r.append(cur_r)
                car_i.append(cur_i)
                cur_r, cur_i = (hr[j:j + 1] + pw_r * cur_r - pw_i * cur_i,
                                hi[j:j + 1] + pw_r * cur_i + pw_i * cur_r)
            st_scr[gb, :, 0:half] = cur_r
            st_scr[gb, :, half:] = cur_i
            carry[gb] = (jnp.concatenate(car_r, axis=0), jnp.concatenate(car_i, axis=0))

        for k in range(steps):
            for gb in gbs:
                cr, ci = carry[gb]
                rows = pl.ds(k * 8, 8)
                pr = pre_ref[gb, pl.ds(k, 1), :]
                pi = pim_ref[gb, pl.ds(k, 1), :]
                bu_scr[gb, rows, 0:half] = bu_scr[gb, rows, 0:half] + (pr * cr - pi * ci)
                bu_scr[gb, rows, half:] = bu_scr[gb, rows, half:] + (pr * ci + pi * cr)
        for gb in gbs:
            y_scr[:, gb * 128:(gb + 1) * 128] = jnp.dot(bu_scr[gb].astype(bf16), wc_ref[gb],
                                                          preferred_element_type=f32)
    yp = y_scr[...]
    y_hi = yp.astype(bf16)
    y_lo = (yp - y_hi.astype(f32)).astype(bf16)
    y = (jnp.dot(permt_ref[...], y_hi, preferred_element_type=f32)
         + jnp.dot(permt_ref[...], y_lo, preferred_element_type=f32))
    if fuse_glu:
        g = _gelu_tanh(y + yf_ref[...] + u_ref[...] * d_ref[...])
        z = jnp.dot(g.astype(bf16), gw_ref[...], preferred_element_type=f32) + gb_ref[...]
        o_ref[...] = (g * jax.nn.sigmoid(z)).astype(o_ref.dtype)
    else:
        o_ref[...] = y


def _s5_tables(a_re, a_im, log_dt, b_re, b_im, c_re, c_im, steps):
    ng, ns = a_re.shape
    nb = ng // S5_GB
    lam = lax.complex(a_re, a_im)
    dt = jnp.exp(log_dt)[:, None]
    lam_bar = jnp.exp(lam * dt)
    b_bar = ((lam_bar - 1) / lam)[..., None] * lax.complex(b_re, b_im)
    cm = lax.complex(c_re, c_im)
    pw = jnp.exp((lam * dt)[None] * jnp.arange(1, steps + 1, dtype=f32)[:, None, None])
    eye = jnp.eye(S5_GB, dtype=f32)
    bb = b_bar.reshape(nb, S5_GB, ns, S5_GROUP)
    wb = jnp.concatenate([jnp.einsum('bgpc,gh->bgchp', part, eye).reshape(nb, S5_GB * S5_GROUP, S5_GB * ns)
                          for part in (jnp.real(bb), jnp.imag(bb))], axis=-1)
    cc = cm.reshape(nb, S5_GB, S5_GROUP, ns)
    wc = jnp.concatenate([jnp.einsum('bgcp,gh->bgphc', part, eye).reshape(nb, S5_GB * ns, S5_GB * S5_GROUP)
                          for part in (jnp.real(cc), -jnp.imag(cc))], axis=1)
    lb = lam_bar.reshape(nb, 1, S5_GB * ns)
    pwb = pw.reshape(steps, nb, S5_GB * ns).transpose(1, 0, 2)
    return (wb.astype(bf16), wc.astype(bf16), jnp.real(lb), jnp.imag(lb), jnp.real(pwb), jnp.imag(pwb))


def _s5_perm(tt, reverse):
    steps = tt // 8
    dst = jnp.arange(tt)
    t = (dst % 8) * steps + dst // 8
    src = tt - 1 - t if reverse else t
    return jnp.arange(tt)[None, :] == src[:, None]


def _s5(u, a_re, a_im, log_dt, b_re, b_im, c_re, c_im, s5_d, glu_w, glu_b, *, batch, seq, ctx_len):
    r, width = u.shape
    tt = S5_TT
    assert ctx_len == tt and seq % tt == 0
    n_lat = seq // tt
    ctx_blk0 = (batch * seq) // tt
    y = None
    for di in range(2):
        rev = di == 1
        wb, wc, lr, li, pr, pi = _s5_tables(a_re[di], a_im[di], log_dt[di], b_re[di], b_im[di],
                                            c_re[di], c_im[di], tt // 8)
        perm = _s5_perm(tt, rev)

        def tile_map(b, i, rev=rev):
            lat = b * n_lat + (n_lat - i if rev else i - 1)
            return (jnp.where(i == 0, ctx_blk0 + b, lat), 0)

        row_spec = pl.BlockSpec((tt, width), tile_map)
        const = lambda a: pl.BlockSpec(a.shape, lambda b, i: (0,) * a.ndim)
        ins = [u, perm.astype(bf16), perm.T.astype(bf16), wb, wc, lr, li, pr, pi]
        in_specs = [row_spec] + [const(a) for a in ins[1:]]
        if rev:
            extra = [y, s5_d.reshape(1, width), glu_w.astype(bf16), glu_b.reshape(1, width)]
            in_specs += [row_spec] + [const(a) for a in extra[1:]]
            ins += extra
        y = pl.pallas_call(
            functools.partial(_s5_kernel, fuse_glu=rev),
            grid=(batch, n_lat + 1),
            in_specs=in_specs,
            out_specs=row_spec,
            out_shape=jax.ShapeDtypeStruct((r, width), bf16 if rev else f32),
            scratch_shapes=[pltpu.VMEM((wb.shape[0], 1, wb.shape[2]), f32),
                            pltpu.VMEM((wb.shape[0], tt, wb.shape[2]), f32),
                            pltpu.VMEM((tt, width), f32)],
            compiler_params=_cparams(("arbitrary", "arbitrary")),
            name="s5_bwd_glu" if rev else "s5_fwd",
        )(*ins)
    return y


def _rope_tables(seq):
    rows = seq // GRID_W
    row = jnp.repeat(jnp.arange(rows, dtype=f32), GRID_W)
    col = jnp.tile(jnp.arange(GRID_W, dtype=f32), rows)
    n_freq = DIFF_QK // 4
    inv = ROPE_BASE ** (-jnp.arange(n_freq, dtype=f32) / n_freq)
    ar, ac = row[:, None] * inv, col[:, None] * inv
    cos = jnp.concatenate([jnp.cos(ar), jnp.cos(ar), jnp.cos(ac), jnp.cos(ac)], axis=-1)
    sin = jnp.concatenate([-jnp.sin(ar), jnp.sin(ar), -jnp.sin(ac), jnp.sin(ac)], axis=-1)
    cos = jnp.concatenate([jnp.tile(cos, (1, 2)), jnp.ones((TM, 128), f32)], axis=0)
    sin = jnp.concatenate([jnp.tile(sin, (1, 2)), jnp.zeros((TM, 128), f32)], axis=0)
    return cos, sin


def kernel(x, c, ctx, c_ctx, ada_w, ada_b, norm_mix_g, norm_ffn_g, norm_out_g, ab_w_in, ab_w_out, diff_lam, diff_subln_g, pool_w, pool_scale, cd_w_in, cd_w_out, hy_conv_w, hy_conv_b, hy_f_w1, hy_f_b1, hy_f_freq, hy_f_w2, hy_f_b2, hy_f_w3, hy_bias, s5_a_re, s5_a_im, s5_log_dt, s5_b_re, s5_b_im, s5_c_re, s5_c_im, s5_d, glu_w, glu_b, moe_w_rg, moe_b_rg, moe_w_re, moe_b_re, moe_w1, moe_w3, moe_w2):
    bsz, seq, d = x.shape
    cl = ctx.shape[1]
    depth = ada_w.shape[0]
    n_lat = bsz * seq
    assert seq % TM == 0 and (bsz * cl) % TM == 0 and bsz + 1 <= 8
    stream = (x.reshape(n_lat, d), ctx.reshape(bsz * cl, d), 0)
    r = n_lat + bsz * cl
    cond = jnp.zeros((8, d), f32).at[0:bsz].set(c).at[bsz].set(c_ctx)
    mods = _ada_mods(cond, ada_w, ada_b)
    geom = dict(batch=bsz, seq=seq)

    for l in range(depth):
        last = l == depth - 1
        i = l // 2
        n_tiles = (n_lat if last else r) // TM
        g_mix = norm_mix_g[l].reshape(1, d)
        if l % 2 == 0:
            lam_init = 0.8 - 0.6 * math.exp(-0.3 * l)
            cos, sin = _rope_tables(seq)
            q, k, vt, p = _proj_rope(*stream, g_mix, mods[l], ab_w_in[i].astype(bf16), cos, sin, **geom)
            mix_a = _diff_attention(q, k, vt, diff_lam[i], diff_subln_g[i], lam_init, ctx_len=cl, **geom)
            mix_b = _pool(p, pool_w[i], pool_scale[i], ctx_len=cl, **geom)
            w_out = ab_w_out[i]
        else:
            assert stream[0] is stream[1]
            v, x1, x2, u = _proj_conv(stream[0], g_mix, mods[l], cd_w_in[i].astype(bf16), hy_conv_w[i], hy_conv_b[i],
                                      **geom)
            mix_a = _hyena(v, x1, x2, hy_f_w1[i], hy_f_b1[i], hy_f_freq[i],
                           hy_f_w2[i], hy_f_b2[i], hy_f_w3[i], hy_bias[i], **geom)
            mix_b = _s5(u, s5_a_re[i], s5_a_im[i], s5_log_dt[i], s5_b_re[i], s5_b_im[i],
                        s5_c_re[i], s5_c_im[i], s5_d[i], glu_w[i], glu_b[i], ctx_len=cl, **geom)
            w_out = cd_w_out[i]
        w_r = jnp.zeros((d, 128), f32).at[:, :MOE_GROUPS].set(moe_w_rg[l]) \
            .at[:, MOE_GROUPS:MOE_GROUPS + MOE_EXPERTS].set(moe_w_re[l]).astype(bf16)
        b_r = jnp.zeros((1, 128), f32).at[0, :MOE_GROUPS].set(moe_b_rg[l]) \
            .at[0, MOE_GROUPS:MOE_GROUPS + MOE_EXPERTS].set(moe_b_re[l])
        xn, h2, logits = _outproj(mix_a, mix_b, *stream, w_out.astype(bf16), mods[l], norm_ffn_g[l].reshape(1, d),
                                  w_r, b_r, n_tiles=n_tiles, **geom)
        xs = _moe(xn, h2, logits, mods[l], moe_w1, moe_w3, moe_w2, l, norm_out_g.reshape(1, d),
                  n_tiles=n_tiles, final_norm=last, **geom)
        stream = (xs, xs, n_lat // TM)
    return xs[:n_lat].reshape(bsz, seq, d)
```

```python
import functools
import math

import jax
import jax.numpy as jnp
from jax import lax
from jax.experimental import pallas as pl
from jax.experimental.pallas import tpu as pltpu

f32, bf16, i32 = jnp.float32, jnp.bfloat16, jnp.int32

EPS = 1e-6
GRID_W = 64
DIFF_QK = 64
DIFF_V = 128
ROPE_BASE = 10000.0
POOL_WINDOWS = (2, 4, 8, 16)
POOL_GROUP = 256
HY_WIDTH = 1024
HY_ORDER = 2
HY_EMB = 33
HY_FAST_DECAY = 0.3
HY_SLOW_DECAY = 1.5
HY_TARGET = 1e-2
S5_GROUP = 16
S5_GROUPS = 64
S5_STATE = 64
MOE_GROUPS = 8
MOE_PER_GROUP = 8
MOE_EXPERTS = 64
MOE_TOPK = 2
MOE_BLOCK = 256

VMEM_LIMIT = 56 * 1024 * 1024
TM = 512
LOG2E = 1.4426950408889634


def _cparams(sem):
    return pltpu.CompilerParams(dimension_semantics=sem, vmem_limit_bytes=VMEM_LIMIT)


def _norm_mod(x, g, shift, scale):
    ms = jnp.mean(x * x, axis=-1, keepdims=True)
    return (x * lax.rsqrt(ms + EPS) * g) * (1.0 + scale) + shift


def _tile_segment(i, lat_tiles_per_batch, batch):
    seg = jnp.zeros((), i32)
    for b in range(1, batch + 1):
        seg = seg + (i >= b * lat_tiles_per_batch).astype(i32)
    return seg


def _pack_rows(x):
    half = x.shape[1] // 2
    bits = lax.bitcast_convert_type(x.astype(bf16).astype(f32), jnp.uint32)
    return (bits[:, :half] >> 16) | (bits[:, half:] & jnp.uint32(0xFFFF0000))


def _unpack_rows(words):
    return (lax.bitcast_convert_type(words << 16, f32),
            lax.bitcast_convert_type(words & jnp.uint32(0xFFFF0000), f32))


def _ada_kernel(s_ref, w_ref, b_ref, o_ref):
    s = s_ref[...]
    a = (s * jax.nn.sigmoid(s)).astype(bf16)
    o_ref[...] = jnp.dot(a, w_ref[...].astype(bf16), preferred_element_type=f32) + b_ref[...]


def _ada_mods(s, ada_w, ada_b):
    depth, d, n = ada_w.shape
    tn = 1024
    return pl.pallas_call(
        _ada_kernel,
        grid=(depth, n // tn),
        in_specs=[pl.BlockSpec((8, d), lambda l, j: (0, 0)),
                  pl.BlockSpec((None, d, tn), lambda l, j: (l, 0, j)),
                  pl.BlockSpec((None, 1, tn), lambda l, j: (l, 0, j))],
        out_specs=pl.BlockSpec((None, 8, tn), lambda l, j: (l, 0, j)),
        out_shape=jax.ShapeDtypeStruct((depth, 8, n), f32),
        compiler_params=_cparams(("arbitrary", "arbitrary")),
        name="ada_mods",
    )(s, ada_w, ada_b.reshape(depth, 1, n))


def _proj_rope_kernel(x_ref, c_ref, g_ref, sh_ref, sc_ref, w_ref, cos_ref, sin_ref,
                      q_ref, k_ref, v_ref, p_ref, h_scr, *, tiles_per_batch, batch, qscale):
    i = pl.program_id(0)
    seg = _tile_segment(i, tiles_per_batch, batch)
    xin = jnp.where(i < batch * tiles_per_batch, x_ref[...], c_ref[...])
    h = _norm_mod(xin, g_ref[...], sh_ref[pl.ds(seg, 1), :], sc_ref[pl.ds(seg, 1), :])
    h_scr[...] = h.astype(bf16)
    cos = cos_ref[...]
    sin = sin_ref[...]
    tm = cos.shape[0]
    lane = lax.broadcasted_iota(i32, (tm, 128), 1)
    first = (lane % 32) < 16

    def rope(r):
        sw = jnp.where(first, pltpu.roll(r, 112, 1), pltpu.roll(r, 16, 1))
        return r * cos + sw * sin

    heads = q_ref.shape[0]
    nc = 512
    for c in range(w_ref.shape[1] // nc):
        res = jnp.dot(h_scr[...], w_ref[:, c * nc:(c + 1) * nc], preferred_element_type=f32)
        for s in range(nc // 128):
            r = res[:, s * 128:(s + 1) * 128]
            col = (c * nc) // 128 + s
            if col < heads:
                q_ref[col] = (rope(r) * qscale).astype(bf16)
            elif col < 2 * heads:
                k_ref[col - heads] = rope(r).astype(bf16)
            elif col < 3 * heads:
                v_ref[col - 2 * heads] = r.T.astype(bf16)
            else:
                c0 = (col - 3 * heads) * 128
                p_ref[:, c0:c0 + 128] = r


def _stream_specs(d, n_lat_tiles, ctx_off):
    return [pl.BlockSpec((TM, d), lambda i: (jnp.minimum(i, n_lat_tiles - 1), 0)),
            pl.BlockSpec((TM, d), lambda i: (jnp.maximum(i - n_lat_tiles, 0) + ctx_off, 0))]


def _proj_rope(x, xc, ctx_off, g, mods, w, cos, sin, *, batch, seq):
    d = x.shape[1]
    heads = 8
    lat_tiles = seq // TM
    n_tiles = batch * lat_tiles + (xc.shape[0] // TM - ctx_off)
    r = n_tiles * TM

    def tab_map(i):
        return (jnp.where(i < batch * lat_tiles, i % lat_tiles, lat_tiles), 0)

    hm = jax.ShapeDtypeStruct((heads, r, 128), bf16)
    hm_spec = pl.BlockSpec((heads, TM, 128), lambda i: (0, i, 0))
    return pl.pallas_call(
        functools.partial(_proj_rope_kernel, tiles_per_batch=lat_tiles, batch=batch,
                          qscale=LOG2E * DIFF_QK ** -0.5),
        grid=(n_tiles,),
        in_specs=_stream_specs(d, batch * lat_tiles, ctx_off) + [
                  pl.BlockSpec((1, d), lambda i: (0, 0)),
                  pl.BlockSpec((8, d), lambda i: (0, 0)),
                  pl.BlockSpec((8, d), lambda i: (0, 1)),
                  pl.BlockSpec(w.shape, lambda i: (0, 0), pipeline_mode=pl.Buffered(1)),
                  pl.BlockSpec((TM, 128), tab_map),
                  pl.BlockSpec((TM, 128), tab_map)],
        out_specs=[hm_spec, hm_spec, pl.BlockSpec((heads, 128, TM), lambda i: (0, 0, i)),
                   pl.BlockSpec((TM, 1024), lambda i: (i, 0))],
        out_shape=[hm, hm, jax.ShapeDtypeStruct((heads, 128, r), bf16), jax.ShapeDtypeStruct((r, 1024), f32)],
        scratch_shapes=[pltpu.VMEM((TM, d), bf16)],
        compiler_params=_cparams(("arbitrary",)),
        name="proj_rope",
    )(x, xc, g, mods, mods, w, cos, sin)


def _proj_conv_kernel(x_ref, prev_ref, next_ref, g_ref, sh_ref, sc_ref, w_ref, cw_ref, cb_ref,
                      v_ref, x1_ref, x2_ref, u_ref, h_scr, halo_scr, ext_scr, *, tiles_per_batch, batch):
    i = pl.program_id(0)
    seg = _tile_segment(i, tiles_per_batch, batch)
    g, sh, sc = g_ref[...], sh_ref[pl.ds(seg, 1), :], sc_ref[pl.ds(seg, 1), :]
    h_scr[...] = _norm_mod(x_ref[...], g, sh, sc).astype(bf16)
    halo = jnp.concatenate([prev_ref[...], next_ref[...]], axis=0)
    halo_scr[...] = _norm_mod(halo, g, sh, sc).astype(bf16)
    j = i % tiles_per_batch
    is_lat = i < batch * tiles_per_batch
    has_prev = jnp.logical_and(is_lat, j > 0)
    has_next = jnp.logical_and(is_lat, j < tiles_per_batch - 1)
    tm = x_ref.shape[0]
    nc = ext_scr.shape[2]
    width = v_ref.shape[1]
    outs = (v_ref, x1_ref, x2_ref)
    for c in range(w_ref.shape[1] // nc):
        w_c = w_ref[:, c * nc:(c + 1) * nc]
        res = jnp.dot(h_scr[...], w_c, preferred_element_type=f32)
        part, off = divmod(c * nc, width)
        if part < len(outs):
            edge = jnp.dot(halo_scr[...], w_c, preferred_element_type=f32)
            ext = ext_scr.at[c % 2]
            ext[0:8, :] = jnp.where(has_prev, edge[0:8], 0.0)
            ext[8:8 + tm, :] = res
            ext[8 + tm:16 + tm, :] = jnp.where(has_next, edge[8:16], 0.0)
            cols = slice(c * nc, (c + 1) * nc)
            cw = cw_ref[:, cols]
            outs[part][:, off:off + nc] = (ext[7:7 + tm, :] * cw[0:1] + ext[8:8 + tm, :] * cw[1:2]
                                           + ext[9:9 + tm, :] * cw[2:3] + cb_ref[:, cols])
        else:
            u_ref[:, off:off + nc] = res


def _proj_conv(x, g, mods, w, conv_w, conv_b, *, batch, seq):
    r, d = x.shape
    n = w.shape[1]
    width = n // 4
    nc = 512
    row = pl.BlockSpec((TM, width), lambda i: (i, 0))
    out = jax.ShapeDtypeStruct((r, width), f32)
    return pl.pallas_call(
        functools.partial(_proj_conv_kernel, tiles_per_batch=seq // TM, batch=batch),
        grid=(r // TM,),
        in_specs=[pl.BlockSpec((TM, d), lambda i: (i, 0)),
                  pl.BlockSpec((8, d), lambda i: (jnp.maximum(i * (TM // 8) - 1, 0), 0)),
                  pl.BlockSpec((8, d), lambda i: (jnp.minimum((i + 1) * (TM // 8), r // 8 - 1), 0)),
                  pl.BlockSpec((1, d), lambda i: (0, 0)),
                  pl.BlockSpec((8, d), lambda i: (0, 0)),
                  pl.BlockSpec((8, d), lambda i: (0, 1)),
                  pl.BlockSpec(w.shape, lambda i: (0, 0), pipeline_mode=pl.Buffered(1)),
                  pl.BlockSpec(conv_w.shape, lambda i: (0, 0)),
                  pl.BlockSpec((1, conv_w.shape[1]), lambda i: (0, 0))],
        out_specs=[row, row, row, row],
        out_shape=[out, out, out, out],
        scratch_shapes=[pltpu.VMEM((TM, d), bf16), pltpu.VMEM((16, d), bf16), pltpu.VMEM((2, TM + 16, nc), f32)],
        compiler_params=_cparams(("arbitrary",)),
        name="proj_conv",
    )(x, x, x, g, mods, mods, w, conv_w, conv_b.reshape(1, -1))


ATTN_ONES = 16


def _attn_kernel(lv_ref, g_ref, q_ref, kc_ref, vc_ref, *rest, n_lat, tk, lam_init):
    if n_lat:
        kl_ref, vl_ref, o_ref, s_scr = rest
    else:
        o_ref, s_scr = rest
    lv = lv_ref[...]
    lam = (jnp.exp(jnp.sum(lv[0:1] * lv[1:2], axis=-1, keepdims=True))
           - jnp.exp(jnp.sum(lv[2:3] * lv[3:4], axis=-1, keepdims=True)) + lam_init)
    q = q_ref[...]
    tq = q.shape[0]
    lane = lax.broadcasted_iota(i32, q.shape, 1)
    zero = jnp.zeros_like(q)
    qs = (jnp.where(lane < DIFF_QK, q, zero), jnp.where(lane >= DIFF_QK, q, zero))
    nt = (((1,), (1,)), ((), ()))

    def scores(slot, k):
        for m in range(2):
            s_scr[slot, m, 0:k.shape[0], :] = lax.dot_general(k, qs[m], nt, preferred_element_type=f32)

    def update(carry, slot, nk, vt):
        vt_ext = jnp.concatenate([vt, jnp.ones((ATTN_ONES, vt.shape[1]), bf16)], axis=0)
        new = []
        for m in range(2):
            mx, acc = carry[2 * m:2 * m + 2]
            s = s_scr[slot, m, 0:nk, :]
            mn = jnp.maximum(mx, jnp.max(s, axis=0, keepdims=True))
            p = jnp.exp2(s - mn).astype(bf16)
            acc = jnp.exp2(mx - mn) * acc + jnp.dot(vt_ext, p, preferred_element_type=f32)
            new += [mn, acc]
        return tuple(new)

    def k_block(j):
        return kl_ref[pl.ds(pl.multiple_of(j * tk, tk), tk), :]

    def vt_block(j):
        return vl_ref[:, pl.ds(pl.multiple_of(j * tk, tk), tk)]

    init = (jnp.full((1, tq), -jnp.inf, f32), jnp.zeros((DIFF_V + ATTN_ONES, tq), f32)) * 2
    nc = kc_ref.shape[0]
    scores(1, kc_ref[...])
    if n_lat:
        scores(0, k_block(0))
        carry = update(init, 1, nc, vc_ref[...])

        def body(jj, carry):
            j = 4 * jj
            scores(1, k_block(j + 1))
            carry = update(carry, 0, tk, vt_block(j))
            scores(0, k_block(j + 2))
            carry = update(carry, 1, tk, vt_block(j + 1))
            scores(1, k_block(j + 3))
            carry = update(carry, 0, tk, vt_block(j + 2))
            scores(0, k_block(jnp.minimum(j + 4, n_lat - 1)))
            return update(carry, 1, tk, vt_block(j + 3))

        carry = lax.fori_loop(0, n_lat // 4, body, carry)
    else:
        carry = update(init, 1, nc, vc_ref[...])
    _, a0, _, a1 = carry
    o = (a0[:DIFF_V] / a0[DIFF_V:DIFF_V + 1]
         - lam * (a1[:DIFF_V] / a1[DIFF_V:DIFF_V + 1]))
    ms = jnp.mean(o * o, axis=0, keepdims=True)
    y = (o * lax.rsqrt(ms + EPS) * g_ref[...]) * (1.0 - lam_init)
    o_ref[...] = y.T.astype(o_ref.dtype)


def _diff_attention(q, k, vt, lam_vecs, subln_g, lam_init, *, batch, seq, ctx_len):
    heads, r, _ = q.shape
    tq, tk = 512, 512
    ctx_blk0 = (batch * seq) // ctx_len
    assert (seq // tk) % 4 == 0
    small = [pl.BlockSpec((4, DIFF_QK), lambda *_: (0, 0)), pl.BlockSpec((DIFF_V, 1), lambda *_: (0, 0))]
    kc_spec = pl.BlockSpec((None, ctx_len, 128), lambda b, h, *_: (h, ctx_blk0 + b, 0))
    vc_spec = pl.BlockSpec((None, 128, ctx_len), lambda b, h, *_: (h, 0, ctx_blk0 + b))
    g2 = subln_g.reshape(DIFF_V, 1)
    lat = pl.pallas_call(
        functools.partial(_attn_kernel, n_lat=seq // tk, tk=tk, lam_init=lam_init),
        grid=(batch, heads, seq // tq),
        in_specs=small + [pl.BlockSpec((None, tq, 128), lambda b, h, i: (h, b * (seq // tq) + i, 0)),
                          kc_spec, vc_spec,
                          pl.BlockSpec((None, seq, 128), lambda b, h, i: (h, b, 0)),
                          pl.BlockSpec((None, 128, seq), lambda b, h, i: (h, 0, b))],
        out_specs=pl.BlockSpec((tq, 128), lambda b, h, i: (b * (seq // tq) + i, h)),
        out_shape=jax.ShapeDtypeStruct((batch * seq, heads * DIFF_V), bf16),
        scratch_shapes=[pltpu.VMEM((2, 2, tk, tq), f32)],
        compiler_params=_cparams(("arbitrary",) * 3),
        name="diff_attn_lat",
    )(lam_vecs, g2, q, k, vt, k, vt)
    ctx = pl.pallas_call(
        functools.partial(_attn_kernel, n_lat=0, tk=tk, lam_init=lam_init),
        grid=(batch, heads),
        in_specs=small + [kc_spec, kc_spec, vc_spec],
        out_specs=pl.BlockSpec((ctx_len, 128), lambda b, h: (b, h)),
        out_shape=jax.ShapeDtypeStruct((batch * ctx_len, heads * DIFF_V), bf16),
        scratch_shapes=[pltpu.VMEM((2, 2, ctx_len, ctx_len), f32)],
        compiler_params=_cparams(("arbitrary",) * 2),
        name="diff_attn_ctx",
    )(lam_vecs, g2, q, k, vt)
    return jnp.concatenate([lat, ctx], axis=0)


def _pool_kernel(p_ref, prev_ref, next_ref, w_ref, scale_ref, o_ref, ext_scr, *,
                 tiles_per_batch, batch, seq, ctx_len):
    i = pl.program_id(0)
    tp = p_ref.shape[0]
    is_ctx = i >= batch * tiles_per_batch
    j = jnp.where(is_ctx, 0, i % tiles_per_batch)
    seg_len = jnp.where(is_ctx, ctx_len, seq)
    last = jnp.where(is_ctx, 0, tiles_per_batch - 1)
    ext_scr[0:8, :] = jnp.where(j > 0, prev_ref[...], 0.0)
    ext_scr[8:8 + tp, :] = p_ref[...]
    ext_scr[8 + tp:16 + tp, :] = jnp.where(j < last, next_ref[...], 0.0)
    pos = j * tp + lax.broadcasted_iota(i32, (tp, 1), 0)
    for gi, w in enumerate(POOL_WINDOWS):
        c0 = gi * POOL_GROUP
        half = w // 2
        tot = ext_scr[8 - half:8 - half + tp, c0:c0 + POOL_GROUP]
        for dlt in range(1 - half, half):
            tot = tot + ext_scr[8 + dlt:8 + dlt + tp, c0:c0 + POOL_GROUP]
        cnt = (jnp.minimum(pos - half + w, seg_len) - jnp.maximum(pos - half, 0)).astype(f32)
        dlt_mean = (tot / cnt - ext_scr[8:8 + tp, c0:c0 + POOL_GROUP]).astype(bf16)
        y = jnp.dot(dlt_mean, w_ref[gi].astype(bf16), preferred_element_type=f32)
        o_ref[:, c0:c0 + POOL_GROUP] = (y * scale_ref[:, c0:c0 + POOL_GROUP]).astype(o_ref.dtype)


def _pool(p, pool_w, pool_scale, *, batch, seq, ctx_len):
    r, width = p.shape
    tp = ctx_len
    tpb = seq // tp
    return pl.pallas_call(
        functools.partial(_pool_kernel, tiles_per_batch=tpb, batch=batch, seq=seq, ctx_len=ctx_len),
        grid=(r // tp,),
        in_specs=[pl.BlockSpec((tp, width), lambda i: (i, 0)),
                  pl.BlockSpec((8, width), lambda i: (jnp.maximum(i * (tp // 8) - 1, 0), 0)),
                  pl.BlockSpec((8, width), lambda i: (jnp.minimum((i + 1) * (tp // 8), r // 8 - 1), 0)),
                  pl.BlockSpec(pool_w.shape, lambda i: (0, 0, 0)),
                  pl.BlockSpec((1, width), lambda i: (0, 0))],
        out_specs=pl.BlockSpec((tp, width), lambda i: (i, 0)),
        out_shape=jax.ShapeDtypeStruct((r, width), bf16),
        scratch_shapes=[pltpu.VMEM((tp + 16, width), f32)],
        compiler_params=_cparams(("arbitrary",)),
        name="pool",
    )(p, p, p, pool_w, pool_scale.reshape(1, width))


def _outproj_kernel(a_ref, b_ref, x_ref, c_ref, w_ref, g1_ref, sh_ref, sc_ref, gn_ref, wr_ref, br_ref,
                    xn_ref, h2_ref, lg_ref, *, tiles_per_batch, batch):
    i = pl.program_id(0)
    seg = _tile_segment(i, tiles_per_batch, batch)
    xin = jnp.where(i < batch * tiles_per_batch, x_ref[...], c_ref[...])
    half = a_ref.shape[1]
    y = (jnp.dot(a_ref[...], w_ref[0:half, :], preferred_element_type=f32)
         + jnp.dot(b_ref[...], w_ref[half:, :], preferred_element_type=f32))
    xn = xin + g1_ref[pl.ds(seg, 1), :] * y
    xn_ref[...] = xn
    h2 = _norm_mod(xn, gn_ref[...], sh_ref[pl.ds(seg, 1), :], sc_ref[pl.ds(seg, 1), :])
    h2_ref[...] = _pack_rows(h2)
    lg_ref[...] = jnp.dot(h2.astype(bf16), wr_ref[...], preferred_element_type=f32) + br_ref[...]


def _outproj(a, b, x, xc, ctx_off, w, mods, g_ffn, w_r, b_r, *, n_tiles, batch, seq):
    r, d = n_tiles * TM, x.shape[1]
    half = a.shape[1]
    row = lambda i: (i, 0)
    return pl.pallas_call(
        functools.partial(_outproj_kernel, tiles_per_batch=seq // TM, batch=batch),
        grid=(n_tiles,),
        in_specs=[pl.BlockSpec((TM, half), row), pl.BlockSpec((TM, half), row)]
        + _stream_specs(d, batch * (seq // TM), ctx_off) + [
                  pl.BlockSpec(w.shape, lambda i: (0, 0), pipeline_mode=pl.Buffered(1)),
                  pl.BlockSpec((8, d), lambda i: (0, 2)),
                  pl.BlockSpec((8, d), lambda i: (0, 3)),
                  pl.BlockSpec((8, d), lambda i: (0, 4)),
                  pl.BlockSpec((1, d), lambda i: (0, 0)),
                  pl.BlockSpec(w_r.shape, lambda i: (0, 0)),
                  pl.BlockSpec((1, 128), lambda i: (0, 0))],
        out_specs=[pl.BlockSpec((TM, d), row), pl.BlockSpec((TM, d // 2), row), pl.BlockSpec((TM, 128), row)],
        out_shape=[jax.ShapeDtypeStruct((r, d), f32), jax.ShapeDtypeStruct((r, d // 2), jnp.uint32),
                   jax.ShapeDtypeStruct((r, 128), f32)],
        compiler_params=_cparams(("arbitrary",)),
        name="outproj",
    )(a, b, x, xc, w, mods, mods, mods, g_ffn, w_r, b_r)


def _moe_kernel(be_ref, nu_ref, nxt_ref, par_ref, xs_ref, w1_hbm, w3_hbm, w2_hbm, ys_ref,
                w1_f32, w3_f32, w2_f32, w1_scr, w3_scr, w2_scr, sems, *, layer):
    i = pl.program_id(0)
    e = be_ref[i]
    prev = be_ref[jnp.maximum(i - 1, 0)]
    used = i < nu_ref[0]

    def weight_copies(expert, slot):
        return (pltpu.make_async_copy(w1_hbm.at[layer, expert], w1_f32.at[slot], sems.at[slot, 0]),
                pltpu.make_async_copy(w3_hbm.at[layer, expert], w3_f32.at[slot], sems.at[slot, 1]),
                pltpu.make_async_copy(w2_hbm.at[layer, expert], w2_f32.at[slot], sems.at[slot, 2]))

    @pl.when(i == 0)
    def _():
        for cp in weight_copies(e, par_ref[e]):
            cp.start()

    @pl.when(used & ((i == 0) | (e != prev)))
    def _():
        slot = par_ref[e]
        nxt = nxt_ref[e]

        @pl.when(nxt >= 0)
        def _():
            for cp in weight_copies(nxt, 1 - slot):
                cp.start()

        for cp in weight_copies(e, slot):
            cp.wait()
        w1_scr[...] = w1_f32[slot].astype(bf16)
        w3_scr[...] = w3_f32[slot].astype(bf16)
        w2_scr[...] = w2_f32[slot].astype(bf16)

    @pl.when(used)
    def _():
        lo, hi = _unpack_rows(xs_ref[...])
        half = lo.shape[1]
        lo, hi = lo.astype(bf16), hi.astype(bf16)
        a = (jnp.dot(lo, w1_scr[0:half, :], preferred_element_type=f32)
             + jnp.dot(hi, w1_scr[half:, :], preferred_element_type=f32))
        b = (jnp.dot(lo, w3_scr[0:half, :], preferred_element_type=f32)
             + jnp.dot(hi, w3_scr[half:, :], preferred_element_type=f32))
        hid = (a * jax.nn.sigmoid(a) * b).astype(bf16)
        ys_ref[...] = _pack_rows(jnp.dot(hid, w2_scr[...], preferred_element_type=f32))

    @pl.when(jnp.logical_not(used))
    def _():
        ys_ref[...] = jnp.zeros_like(ys_ref)


def _moe_experts(xs, block_e, n_used, next_expert, parity, w1, w3, w2, layer, nb):
    m = MOE_BLOCK
    _, _, d, de = w1.shape
    hbm = pl.BlockSpec(memory_space=pl.ANY)
    grid_spec = pltpu.PrefetchScalarGridSpec(
        num_scalar_prefetch=4,
        grid=(nb,),
        in_specs=[pl.BlockSpec((m, d // 2), lambda i, be, nu, nx, pa: (jnp.minimum(i, nu[0] - 1), 0)), hbm, hbm, hbm],
        out_specs=pl.BlockSpec((m, d // 2), lambda i, be, nu, nx, pa: (i, 0)),
        scratch_shapes=[pltpu.VMEM((2, d, de), f32), pltpu.VMEM((2, d, de), f32), pltpu.VMEM((2, de, d), f32),
                        pltpu.VMEM((d, de), bf16), pltpu.VMEM((d, de), bf16), pltpu.VMEM((de, d), bf16),
                        pltpu.SemaphoreType.DMA((2, 3))],
    )
    return pl.pallas_call(
        functools.partial(_moe_kernel, layer=layer),
        grid_spec=grid_spec,
        out_shape=jax.ShapeDtypeStruct((nb * m, d // 2), jnp.uint32),
        compiler_params=_cparams(("arbitrary",)),
        name="moe_experts",
    )(block_e, n_used, next_expert, parity, xs, w1, w3, w2)


def _route_kernel(lg_ref, tri_ref, o_ref, cnt_ref, run_scr):
    @pl.when(pl.program_id(0) == 0)
    def _():
        run_scr[...] = jnp.zeros_like(run_scr)

    lg = lg_ref[...]
    lane = lax.broadcasted_iota(i32, lg.shape, 1)
    ninf = -jnp.inf

    def first_lane(mask):
        return jnp.min(jnp.where(mask, lane, lg.shape[1]), axis=-1, keepdims=True)

    gl = jnp.where(lane < MOE_GROUPS, lg, ninf)
    gmax = jnp.max(gl, axis=-1, keepdims=True)
    pg_top = 1.0 / jnp.sum(jnp.exp(gl - gmax), axis=-1, keepdims=True)
    lo = MOE_GROUPS + first_lane(gl == gmax) * MOE_PER_GROUP
    sel = (lane >= lo) & (lane < lo + MOE_PER_GROUP)
    el = jnp.where(sel, lg, ninf)
    ee = jnp.exp(el - jnp.max(el, axis=-1, keepdims=True))
    pe = jnp.where(sel, ee / jnp.sum(ee, axis=-1, keepdims=True), -1.0)
    p1 = jnp.max(pe, axis=-1, keepdims=True)
    i1 = first_lane(pe == p1)
    pe2 = jnp.where(lane == i1, -1.0, pe)
    p2 = jnp.max(pe2, axis=-1, keepdims=True)
    i2 = first_lane(pe2 == p2)
    den = p1 + p2
    e1 = i1 - MOE_GROUPS
    e2 = i2 - MOE_GROUPS
    hit1 = lane == e1
    hit2 = lane == e2
    onehot = jnp.where(hit1, 1.0, jnp.where(hit2, 1.0, 0.0))
    base = jnp.dot(tri_ref[...], onehot.astype(bf16), preferred_element_type=f32) + run_scr[...]
    r1 = jnp.sum(jnp.where(hit1, base, 0.0), axis=-1, keepdims=True)
    r2 = jnp.sum(jnp.where(hit2, base, 0.0), axis=-1, keepdims=True)
    run_scr[...] = run_scr[...] + jnp.sum(onehot, axis=0, keepdims=True)
    cnt_ref[...] = run_scr[...]
    vals = (e1.astype(f32), e2.astype(f32), pg_top * p1 / den, pg_top * p2 / den, r1, r2)
    out = jnp.zeros(lg.shape, f32)
    for j, v in enumerate(vals):
        out = jnp.where(lane == j, v, out)
    o_ref[...] = out


def _route(logits):
    t, width = logits.shape
    tri = jnp.tril(jnp.ones((TM, TM), f32), -1).astype(bf16)
    return pl.pallas_call(
        _route_kernel,
        grid=(t // TM,),
        in_specs=[pl.BlockSpec((TM, width), lambda i: (i, 0)), pl.BlockSpec((TM, TM), lambda i: (0, 0))],
        out_specs=[pl.BlockSpec((TM, width), lambda i: (i, 0)), pl.BlockSpec((1, width), lambda i: (0, 0))],
        out_shape=[jax.ShapeDtypeStruct((t, width), f32), jax.ShapeDtypeStruct((1, width), f32)],
        scratch_shapes=[pltpu.VMEM((1, width), f32)],
        compiler_params=_cparams(("arbitrary",)),
        name="route",
    )(logits, tri)


def _dispatch_kernel(dest_ref, h_ref, init_ref, xs_ref, sem):
    del init_ref
    i = pl.program_id(0)
    tm = h_ref.shape[0]

    def row_copy(r, k):
        slot = dest_ref[(i * tm + r) * MOE_TOPK + k]
        return pltpu.make_async_copy(h_ref.at[pl.ds(r, 1)], xs_ref.at[pl.ds(slot, 1)], sem)

    def issue(r, carry):
        for k in range(MOE_TOPK):
            row_copy(r, k).start(priority=k % 2)
        return carry

    lax.fori_loop(0, tm, issue, 0, unroll=8)
    for k in range(MOE_TOPK):
        pltpu.make_async_copy(h_ref, xs_ref.at[pl.ds(0, tm)], sem).wait()


def _dispatch(h2, dest, nb):
    t, d = h2.shape
    rows = nb * MOE_BLOCK
    grid_spec = pltpu.PrefetchScalarGridSpec(
        num_scalar_prefetch=1,
        grid=(t // TM,),
        in_specs=[pl.BlockSpec((TM, d), lambda i, dst: (i, 0)), pl.BlockSpec(memory_space=pl.ANY)],
        out_specs=pl.BlockSpec(memory_space=pl.ANY),
        scratch_shapes=[pltpu.SemaphoreType.DMA(())],
    )
    return pl.pallas_call(
        _dispatch_kernel,
        grid_spec=grid_spec,
        out_shape=jax.ShapeDtypeStruct((rows, d), h2.dtype),
        input_output_aliases={2: 0},
        compiler_params=_cparams(("arbitrary",)),
        name="moe_dispatch",
    )(dest, h2, jnp.zeros((rows, d), h2.dtype))


def _combine_kernel(dest_ref, x_ref, rt_ref, g2_ref, gn_ref, ys_ref, o_ref, y_scr, sem, *,
                    tiles_per_batch, batch, final_norm):
    i = pl.program_id(0)
    tm = x_ref.shape[0]

    def row_copy(r, k):
        slot = dest_ref[(i * tm + r) * MOE_TOPK + k]
        return pltpu.make_async_copy(ys_ref.at[pl.ds(slot, 1)], y_scr.at[k, pl.ds(r, 1)], sem)

    def issue(r, carry):
        for k in range(MOE_TOPK):
            row_copy(r, k).start(priority=k % 2)
        return carry

    lax.fori_loop(0, tm, issue, 0, unroll=8)
    for k in range(MOE_TOPK):
        pltpu.make_async_copy(ys_ref.at[pl.ds(0, tm)], y_scr.at[k], sem).wait()
    seg = _tile_segment(i, tiles_per_batch, batch)
    rt = rt_ref[...]
    y_lo, y_hi = 0.0, 0.0
    for k in range(MOE_TOPK):
        lo, hi = _unpack_rows(y_scr[k])
        gate = rt[:, MOE_TOPK + k:MOE_TOPK + k + 1]
        y_lo = y_lo + gate * lo
        y_hi = y_hi + gate * hi
    x = x_ref[...] + g2_ref[pl.ds(seg, 1), :] * jnp.concatenate([y_lo, y_hi], axis=1)
    if final_norm:
        ms = jnp.mean(x * x, axis=-1, keepdims=True)
        x = x * lax.rsqrt(ms + EPS) * gn_ref[...]
    o_ref[...] = x


def _combine(x, ys, dest, routed, mods, g_out, *, n_tiles, batch, seq, final_norm):
    d = x.shape[1]
    row = lambda i, dst: (i, 0)
    grid_spec = pltpu.PrefetchScalarGridSpec(
        num_scalar_prefetch=1,
        grid=(n_tiles,),
        in_specs=[pl.BlockSpec((TM, d), row),
                  pl.BlockSpec((TM, routed.shape[1]), row),
                  pl.BlockSpec((8, d), lambda i, dst: (0, 5)),
                  pl.BlockSpec((1, d), lambda i, dst: (0, 0)),
                  pl.BlockSpec(memory_space=pl.ANY)],
        out_specs=pl.BlockSpec((TM, d), row),
        scratch_shapes=[pltpu.VMEM((MOE_TOPK, TM, d // 2), jnp.uint32), pltpu.SemaphoreType.DMA(())],
    )
    return pl.pallas_call(
        functools.partial(_combine_kernel, tiles_per_batch=seq // TM, batch=batch, final_norm=final_norm),
        grid_spec=grid_spec,
        out_shape=jax.ShapeDtypeStruct((n_tiles * TM, d), f32),
        compiler_params=_cparams(("arbitrary",)),
        name="moe_combine",
    )(dest, x, routed, mods, g_out, ys)


def _moe(xn, h2, logits, mods, w1, w3, w2, layer, g_out, *, n_tiles, batch, seq, final_norm):
    t = n_tiles * TM
    m = MOE_BLOCK
    routed, cnt = _route(logits)
    s = t * MOE_TOPK
    nb = s // m + MOE_EXPERTS
    expert = routed[:, 0:MOE_TOPK].astype(i32)
    rank = routed[:, 2 * MOE_TOPK:3 * MOE_TOPK].astype(i32)
    counts = cnt[0, :MOE_EXPERTS].astype(i32)
    padded = ((counts + m - 1) // m) * m
    pad_end = jnp.cumsum(padded)
    onehot = expert[:, :, None] == jnp.arange(MOE_EXPERTS, dtype=i32)[None, None, :]
    dest = (jnp.sum(jnp.where(onehot, (pad_end - padded)[None, None, :], 0), axis=-1) + rank).reshape(s)
    n_used = (pad_end[-1] // m).astype(i32)
    blk = jnp.arange(nb, dtype=i32)
    block_e = jnp.minimum(jnp.searchsorted(pad_end, jnp.minimum(blk, n_used - 1) * m, side='right'),
                          MOE_EXPERTS - 1).astype(i32)
    ids = jnp.arange(MOE_EXPERTS, dtype=i32)
    nonempty_from = lax.cummin(jnp.where(counts > 0, ids, MOE_EXPERTS), reverse=True)
    nxt = jnp.concatenate([nonempty_from[1:], jnp.full((1,), MOE_EXPERTS, i32)])
    next_expert = jnp.where(nxt < MOE_EXPERTS, nxt, -1).astype(i32)
    parity = ((jnp.cumsum((counts > 0).astype(i32)) - 1) % 2).astype(i32)
    xs = _dispatch(h2, dest, nb)
    ys = _moe_experts(xs, block_e, n_used.reshape(1), next_expert, parity, w1, w3, w2, layer, nb)
    return _combine(xn, ys, dest, routed, mods, g_out, n_tiles=n_tiles, batch=batch, seq=seq,
                    final_norm=final_norm)


HY_N2 = 128
HY_WB = 256


def _hy_filter_kernel(z_ref, w1_ref, b1_ref, fr_ref, w2_ref, b2_ref, w3_ref, dl_ref, k_ref, nrm_ref):
    i = pl.program_id(0)
    z = z_ref[...]
    fr = fr_ref[...]
    h = jnp.sin(fr[0:1] * (jnp.dot(z.astype(bf16), w1_ref[...], preferred_element_type=f32) + b1_ref[...]))
    h = jnp.sin(fr[1:2] * (jnp.dot(h.astype(bf16), w2_ref[...], preferred_element_type=f32) + b2_ref[...]))
    kk = jnp.dot(h.astype(bf16), w3_ref[...], preferred_element_type=f32)
    width = dl_ref.shape[1]
    decay = jnp.exp(-z[:, 0:1] * jnp.abs(dl_ref[...]))
    tl = z.shape[0]
    row = i * tl + lax.broadcasted_iota(i32, (tl, 1), 0)
    tot = []
    for blk in range(kk.shape[1] // width):
        kb = kk[:, blk * width:(blk + 1) * width] * decay
        if blk % 2 == 1:
            kb = jnp.where(row == 0, 0.0, kb)
        k_ref[:, blk * width:(blk + 1) * width] = kb
        tot.append(jnp.sum(jnp.abs(kb), axis=0, keepdims=True))
    s = jnp.concatenate(tot, axis=1)

    @pl.when(i == 0)
    def _():
        nrm_ref[...] = s

    @pl.when(i > 0)
    def _():
        nrm_ref[...] = nrm_ref[...] + s


def _hy_filters(seq, w1, b1, freq, w2, b2, w3):
    t = jnp.linspace(0.0, 1.0, seq, dtype=f32)[:, None]
    bands = (HY_EMB - 1) // 2
    w = 2 * math.pi * jnp.arange(seq, dtype=f32)[:, None] / seq
    f = jnp.linspace(1e-4, bands - 1, bands, dtype=f32)[None, :]
    z = jnp.concatenate([t, jnp.cos(f * w), -jnp.sin(f * w), jnp.zeros((seq, 128 - HY_EMB), f32)], axis=-1)
    w1p = jnp.concatenate([w1, jnp.zeros((128 - HY_EMB, w1.shape[1]), f32)], axis=0).astype(bf16)
    hid = w1.shape[1]
    n_out = w3.shape[1]
    deltas = jnp.linspace(math.log(HY_TARGET) / HY_SLOW_DECAY, math.log(HY_TARGET) / HY_FAST_DECAY,
                          HY_WIDTH, dtype=f32).reshape(1, HY_WIDTH)
    tl = 512
    const = lambda a: pl.BlockSpec(a.shape, lambda i: (0,) * a.ndim)
    ins = [z, w1p, b1.reshape(1, hid), freq, w2.astype(bf16), b2.reshape(1, hid), w3.astype(bf16), deltas]
    return pl.pallas_call(
        _hy_filter_kernel,
        grid=(seq // tl,),
        in_specs=[pl.BlockSpec((tl, 128), lambda i: (i, 0))] + [const(a) for a in ins[1:]],
        out_specs=[pl.BlockSpec((tl, n_out), lambda i: (i, 0)), pl.BlockSpec((1, n_out), lambda i: (0, 0))],
        out_shape=[jax.ShapeDtypeStruct((seq, n_out), f32), jax.ShapeDtypeStruct((1, n_out), f32)],
        compiler_params=_cparams(("arbitrary",)),
        name="hy_filter",
    )(*ins)


def _dft_tables(seq):
    n2 = HY_N2
    n_tot = 2 * seq
    n1 = n_tot // n2
    nf = n1 // 2 + 1
    nfp = -(-nf // 8) * 8
    f1 = jnp.arange(nfp, dtype=i32)
    k1 = jnp.arange(n1 // 2, dtype=i32)
    th = (2 * math.pi / n1) * ((f1[:, None] * k1[None, :]) % n1).astype(f32)
    valid = (f1 < nf)[:, None]
    fwd = jnp.concatenate([jnp.where(valid, jnp.cos(th), 0.0), jnp.where(valid, -jnp.sin(th), 0.0)], axis=0)
    eye = jnp.eye(8, dtype=f32)
    fk = jnp.einsum('rk,jh->rjkh', fwd, eye).reshape(2 * nfp * 8, (n1 // 2) * 8)
    cw = jnp.where(valid, jnp.where((f1 == 0) | (f1 == n1 // 2), 1.0, 2.0)[:, None], 0.0) / n_tot
    inv = jnp.concatenate([(cw * jnp.cos(th)).T, (-cw * jnp.sin(th)).T], axis=1)
    ck = jnp.einsum('kr,jh->kjrh', inv, eye).reshape((n1 // 2) * 8, 2 * nfp * 8)
    fa = jnp.arange(nf, dtype=i32)[:, None, None]
    f2 = jnp.arange(n2, dtype=i32)[None, :, None]
    m2 = jnp.arange(n2, dtype=i32)[None, None, :]
    ph = (2 * math.pi / n_tot) * ((f2 * m2 * n1 + m2 * fa) % n_tot).astype(f32)
    gr, gi = jnp.cos(ph), -jnp.sin(ph)
    g = jnp.concatenate([jnp.concatenate([gr, -gi], axis=2), jnp.concatenate([gi, gr], axis=2)], axis=1)
    hr, hi = jnp.swapaxes(jnp.cos(ph), 1, 2), jnp.swapaxes(jnp.sin(ph), 1, 2)
    h = jnp.concatenate([jnp.concatenate([hr, -hi], axis=2), jnp.concatenate([hi, hr], axis=2)], axis=1)
    return fk.astype(bf16), ck.astype(bf16), g.astype(bf16), h.astype(bf16), nf, nfp


def _hy_rows_per_step(nf):
    return max(d for d in (5, 4, 3, 2, 1) if nf % d == 0)


def _dft_stage1(x_ref, fk_ref, a_scr):
    n1h, n2, wb = x_ref.shape

    def slab(s, carry):
        rows = pl.ds(pl.multiple_of(s * 8, 8), 8)
        xs = x_ref[:, rows, :].reshape(n1h * 8, wb).astype(bf16)
        a = jnp.dot(fk_ref[...], xs, preferred_element_type=f32)
        a_scr[:, rows, :] = a.reshape(a_scr.shape[0], 8, wb)
        return carry

    lax.fori_loop(0, n2 // 8, slab, 0)


def _dft_stage2(a_scr, g, f1, nfp):
    a = jnp.concatenate([a_scr[f1], a_scr[nfp + f1]], axis=0).astype(bf16)
    return jnp.dot(g, a, preferred_element_type=f32)


def _hy_spectrum_kernel(k_ref, fk_ref, g_ref, o_ref, a_scr, *, nfp):
    fs = pl.program_id(2)

    @pl.when(fs == 0)
    def _():
        _dft_stage1(k_ref, fk_ref, a_scr)

    fb = g_ref.shape[0]
    n2 = k_ref.shape[1]
    for j in range(fb):
        s = _dft_stage2(a_scr, g_ref[j], fs * fb + j, nfp)
        o_ref[0, j] = s[:n2]
        o_ref[1, j] = s[n2:]


def _hy_spectrum(k, fk, g, nf, nfp, seq):
    n2, wb = HY_N2, HY_WB
    n1h = seq // n2
    fb = _hy_rows_per_step(nf)
    nwb = HY_WIDTH // wb
    n_blk = k.shape[1] // HY_WIDTH
    k3 = k.reshape(n1h, n2, k.shape[1])
    return pl.pallas_call(
        functools.partial(_hy_spectrum_kernel, nfp=nfp),
        grid=(n_blk, nwb, nf // fb),
        in_specs=[pl.BlockSpec((n1h, n2, wb), lambda o, w, f: (0, 0, o * nwb + w), pipeline_mode=pl.Buffered(1)),
                  pl.BlockSpec(fk.shape, lambda o, w, f: (0, 0)),
                  pl.BlockSpec((fb, 2 * n2, 2 * n2), lambda o, w, f: (f, 0, 0))],
        out_specs=pl.BlockSpec((None, 2, fb, n2, wb), lambda o, w, f: (o, 0, f, 0, w)),
        out_shape=jax.ShapeDtypeStruct((n_blk, 2, nf, n2, HY_WIDTH), f32),
        scratch_shapes=[pltpu.VMEM((2 * nfp, n2, wb), f32)],
        compiler_params=_cparams(("arbitrary",) * 3),
        name="hy_spectrum",
    )(k3, fk, g)


def _hy_conv_kernel(x_ref, gate_ref, kf_ref, kb_ref, g_ref, h_ref, fk_ref, ck_ref, bias_ref, inv_ref, o_ref, a_scr, *,
                    nfp):
    fs = pl.program_id(2)
    n1h, n2, wb = x_ref.shape

    @pl.when(fs == 0)
    def _():
        _dft_stage1(x_ref, fk_ref, a_scr)

    fb = g_ref.shape[0]
    for j in range(fb):
        f1 = fs * fb + j
        xs = _dft_stage2(a_scr, g_ref[j], f1, nfp)
        xr, xi = xs[:n2], xs[n2:]
        kr = kf_ref[0, j] + kb_ref[0, j]
        ki = kf_ref[1, j] - kb_ref[1, j]
        y = jnp.concatenate([xr * kr - xi * ki, xr * ki + xi * kr], axis=0).astype(bf16)
        bb = jnp.dot(h_ref[j], y, preferred_element_type=f32)
        a_scr[f1] = bb[:n2]
        a_scr[nfp + f1] = bb[n2:]

    @pl.when(fs == pl.num_programs(2) - 1)
    def _():
        bias = bias_ref[...].reshape(1, 1, wb)
        inv = inv_ref[...].reshape(1, 1, wb)

        def slab(s, carry):
            rows = pl.ds(pl.multiple_of(s * 8, 8), 8)
            bs = a_scr[:, rows, :].reshape(2 * nfp * 8, wb).astype(bf16)
            y = jnp.dot(ck_ref[...], bs, preferred_element_type=f32).reshape(n1h, 8, wb)
            o_ref[:, rows, :] = (gate_ref[:, rows, :] * (y * inv + x_ref[:, rows, :] * bias)).astype(o_ref.dtype)
            return carry

        lax.fori_loop(0, n2 // 8, slab, 0)


def _hy_conv(x, gate, kf, order, tabs, bias, inv_norm, out_dtype, *, batch, seq):
    fk, ck, g, h, nf, nfp = tabs
    n2, wb = HY_N2, HY_WB
    n1h = seq // n2
    fb = _hy_rows_per_step(nf)
    width = x.shape[1]
    big = pl.BlockSpec((n1h, n2, wb), lambda b, w, f: (b, 0, w), pipeline_mode=pl.Buffered(1))
    out = pl.pallas_call(
        functools.partial(_hy_conv_kernel, nfp=nfp),
        grid=(batch, width // wb, nf // fb),
        in_specs=[big, big,
                  pl.BlockSpec((None, 2, fb, n2, wb), lambda b, w, f: (2 * order, 0, f, 0, w)),
                  pl.BlockSpec((None, 2, fb, n2, wb), lambda b, w, f: (2 * order + 1, 0, f, 0, w)),
                  pl.BlockSpec((fb, 2 * n2, 2 * n2), lambda b, w, f: (f, 0, 0)),
                  pl.BlockSpec((fb, 2 * n2, 2 * n2), lambda b, w, f: (f, 0, 0)),
                  pl.BlockSpec(fk.shape, lambda b, w, f: (0, 0)),
                  pl.BlockSpec(ck.shape, lambda b, w, f: (0, 0)),
                  pl.BlockSpec((1, wb), lambda b, w, f: (0, w)),
                  pl.BlockSpec((1, wb), lambda b, w, f: (0, w))],
        out_specs=big,
        out_shape=jax.ShapeDtypeStruct((batch * n1h, n2, width), out_dtype),
        scratch_shapes=[pltpu.VMEM((2 * nfp, n2, wb), f32)],
        compiler_params=_cparams(("arbitrary",) * 3),
        name="hy_conv",
    )(x.reshape(-1, n2, width), gate.reshape(-1, n2, width), kf, kf, g, h, fk, ck,
      bias.reshape(1, width), inv_norm.reshape(1, width))
    return out.reshape(batch * seq, width)


def _hyena(v, x1, x2, f_w1, f_b1, f_freq, f_w2, f_b2, f_w3, hy_bias, *, batch, seq):
    k, sums = _hy_filters(seq, f_w1, f_b1, f_freq, f_w2, f_b2, f_w3)
    sums = sums.reshape(HY_ORDER, 2, HY_WIDTH)
    inv_norm = 1.0 / (sums[:, 0] + sums[:, 1])
    tabs = _dft_tables(seq)
    kf = _hy_spectrum(k, tabs[0], tabs[2], tabs[4], tabs[5], seq)
    z = _hy_conv(v, x1, kf, 0, tabs, hy_bias[0], inv_norm[0], f32, batch=batch, seq=seq)
    return _hy_conv(z, x2, kf, 1, tabs, hy_bias[1], inv_norm[1], bf16, batch=batch, seq=seq)


S5_TT = 256
S5_GB = 8


def _gelu_tanh(x):
    return 0.5 * x * (1.0 + jnp.tanh(0.7978845608028654 * (x + 0.044715 * x * x * x)))


def _s5_kernel(u_ref, perm_ref, permt_ref, wb_ref, wc_ref, are_ref, aim_ref, pre_ref, pim_ref, *rest, fuse_glu):
    if fuse_glu:
        yf_ref, d_ref, gw_ref, gb_ref, o_ref, st_scr, bu_scr, y_scr = rest
    else:
        o_ref, st_scr, bu_scr, y_scr = rest
    tt = u_ref.shape[0]
    steps = tt // 8
    half = bu_scr.shape[2] // 2

    @pl.when(pl.program_id(1) == 0)
    def _():
        st_scr[...] = jnp.zeros_like(st_scr)

    up = jnp.dot(perm_ref[...], u_ref[...].astype(bf16), preferred_element_type=f32).astype(bf16)
    lock = 4
    zero = jnp.zeros((8, half), f32)
    for g0 in range(0, wb_ref.shape[0], lock):
        gbs = range(g0, g0 + lock)
        for gb in gbs:
            bu_scr[gb] = jnp.dot(up[:, gb * 128:(gb + 1) * 128], wb_ref[gb], preferred_element_type=f32)
        lam = {gb: (jnp.broadcast_to(are_ref[gb], (8, half)), jnp.broadcast_to(aim_ref[gb], (8, half)))
               for gb in gbs}

        def scan_step(gb, k, h):
            hr, hi = h
            a_re, a_im = lam[gb]
            rows = pl.ds(k * 8, 8)
            nr = a_re * hr - a_im * hi + bu_scr[gb, rows, 0:half]
            ni = a_re * hi + a_im * hr + bu_scr[gb, rows, half:]
            bu_scr[gb, rows, 0:half] = nr
            bu_scr[gb, rows, half:] = ni
            return nr, ni

        h = {gb: (zero, zero) for gb in gbs}
        for k in range(steps):
            for gb in gbs:
                h[gb] = scan_step(gb, k, h[gb])

        carry = {}
        for gb in gbs:
            hr, hi = h[gb]
            pw_r = pre_ref[gb, steps - 1:steps, :]
            pw_i = pim_ref[gb, steps - 1:steps, :]
            cur_r = st_scr[gb, :, 0:half]
            cur_i = st_scr[gb, :, half:]
            car_r, car_i = [], []
            for j in range(8):
                car_r.append(cur_r)
                car_i.append(cur_i)
                cur_r, cur_i = (hr[j:j + 1] + pw_r * cur_r - pw_i * cur_i,
                                hi[j:j + 1] + pw_r * cur_i + pw_i * cur_r)
            st_scr[gb, :, 0:half] = cur_r
            st_scr[gb, :, half:] = cur_i
            carry[gb] = (jnp.concatenate(car_r, axis=0), jnp.concatenate(car_i, axis=0))

        for k in range(steps):
            for gb in gbs:
                cr, ci = carry[gb]
                rows = pl.ds(k * 8, 8)
                pr = pre_ref[gb, pl.ds(k, 1), :]
                pi = pim_ref[gb, pl.ds(k, 1), :]
                bu_scr[gb, rows, 0:half] = bu_scr[gb, rows, 0:half] + (pr * cr - pi * ci)
                bu_scr[gb, rows, half:] = bu_scr[gb, rows, half:] + (pr * ci + pi * cr)
        for gb in gbs:
            y_scr[:, gb * 128:(gb + 1) * 128] = jnp.dot(bu_scr[gb].astype(bf16), wc_ref[gb],
                                                          preferred_element_type=f32)
    yp = y_scr[...]
    y_hi = yp.astype(bf16)
    y_lo = (yp - y_hi.astype(f32)).astype(bf16)
    y = (jnp.dot(permt_ref[...], y_hi, preferred_element_type=f32)
         + jnp.dot(permt_ref[...], y_lo, preferred_element_type=f32))
    if fuse_glu:
        g = _gelu_tanh(y + yf_ref[...] + u_ref[...] * d_ref[...])
        z = jnp.dot(g.astype(bf16), gw_ref[...], preferred_element_type=f32) + gb_ref[...]
        o_ref[...] = (g * jax.nn.sigmoid(z)).astype(o_ref.dtype)
    else:
        o_ref[...] = y


def _s5_tables(a_re, a_im, log_dt, b_re, b_im, c_re, c_im, steps):
    ng, ns = a_re.shape
    nb = ng // S5_GB
    lam = lax.complex(a_re, a_im)
    dt = jnp.exp(log_dt)[:, None]
    lam_bar = jnp.exp(lam * dt)
    b_bar = ((lam_bar - 1) / lam)[..., None] * lax.complex(b_re, b_im)
    cm = lax.complex(c_re, c_im)
    pw = jnp.exp((lam * dt)[None] * jnp.arange(1, steps + 1, dtype=f32)[:, None, None])
    eye = jnp.eye(S5_GB, dtype=f32)
    bb = b_bar.reshape(nb, S5_GB, ns, S5_GROUP)
    wb = jnp.concatenate([jnp.einsum('bgpc,gh->bgchp', part, eye).reshape(nb, S5_GB * S5_GROUP, S5_GB * ns)
                          for part in (jnp.real(bb), jnp.imag(bb))], axis=-1)
    cc = cm.reshape(nb, S5_GB, S5_GROUP, ns)
    wc = jnp.concatenate([jnp.einsum('bgcp,gh->bgphc', part, eye).reshape(nb, S5_GB * ns, S5_GB * S5_GROUP)
                          for part in (jnp.real(cc), -jnp.imag(cc))], axis=1)
    lb = lam_bar.reshape(nb, 1, S5_GB * ns)
    pwb = pw.reshape(steps, nb, S5_GB * ns).transpose(1, 0, 2)
    return (wb.astype(bf16), wc.astype(bf16), jnp.real(lb), jnp.imag(lb), jnp.real(pwb), jnp.imag(pwb))


def _s5_perm(tt, reverse):
    steps = tt // 8
    dst = jnp.arange(tt)
    t = (dst % 8) * steps + dst // 8
    src = tt - 1 - t if reverse else t
    return jnp.arange(tt)[None, :] == src[:, None]


def _s5(u, a_re, a_im, log_dt, b_re, b_im, c_re, c_im, s5_d, glu_w, glu_b, *, batch, seq, ctx_len):
    r, width = u.shape
    tt = S5_TT
    assert ctx_len == tt and seq % tt == 0
    n_lat = seq // tt
    ctx_blk0 = (batch * seq) // tt
    y = None
    for di in range(2):
        rev = di == 1
        wb, wc, lr, li, pr, pi = _s5_tables(a_re[di], a_im[di], log_dt[di], b_re[di], b_im[di],
                                            c_re[di], c_im[di], tt // 8)
        perm = _s5_perm(tt, rev)

        def tile_map(b, i, rev=rev):
            lat = b * n_lat + (n_lat - i if rev else i - 1)
            return (jnp.where(i == 0, ctx_blk0 + b, lat), 0)

        row_spec = pl.BlockSpec((tt, width), tile_map)
        const = lambda a: pl.BlockSpec(a.shape, lambda b, i: (0,) * a.ndim)
        ins = [u, perm.astype(bf16), perm.T.astype(bf16), wb, wc, lr, li, pr, pi]
        in_specs = [row_spec] + [const(a) for a in ins[1:]]
        if rev:
            extra = [y, s5_d.reshape(1, width), glu_w.astype(bf16), glu_b.reshape(1, width)]
            in_specs += [row_spec] + [const(a) for a in extra[1:]]
            ins += extra
        y = pl.pallas_call(
            functools.partial(_s5_kernel, fuse_glu=rev),
            grid=(batch, n_lat + 1),
            in_specs=in_specs,
            out_specs=row_spec,
            out_shape=jax.ShapeDtypeStruct((r, width), bf16 if rev else f32),
            scratch_shapes=[pltpu.VMEM((wb.shape[0], 1, wb.shape[2]), f32),
                            pltpu.VMEM((wb.shape[0], tt, wb.shape[2]), f32),
                            pltpu.VMEM((tt, width), f32)],
            compiler_params=_cparams(("arbitrary", "arbitrary")),
            name="s5_bwd_glu" if rev else "s5_fwd",
        )(*ins)
    return y


def _rope_tables(seq):
    rows = seq // GRID_W
    row = jnp.repeat(jnp.arange(rows, dtype=f32), GRID_W)
    col = jnp.tile(jnp.arange(GRID_W, dtype=f32), rows)
    n_freq = DIFF_QK // 4
    inv = ROPE_BASE ** (-jnp.arange(n_freq, dtype=f32) / n_freq)
    ar, ac = row[:, None] * inv, col[:, None] * inv
    cos = jnp.concatenate([jnp.cos(ar), jnp.cos(ar), jnp.cos(ac), jnp.cos(ac)], axis=-1)
    sin = jnp.concatenate([-jnp.sin(ar), jnp.sin(ar), -jnp.sin(ac), jnp.sin(ac)], axis=-1)
    cos = jnp.concatenate([jnp.tile(cos, (1, 2)), jnp.ones((TM, 128), f32)], axis=0)
    sin = jnp.concatenate([jnp.tile(sin, (1, 2)), jnp.zeros((TM, 128), f32)], axis=0)
    return cos, sin


def kernel(x, c, ctx, c_ctx, ada_w, ada_b, norm_mix_g, norm_ffn_g, norm_out_g, ab_w_in, ab_w_out, diff_lam, diff_subln_g, pool_w, pool_scale, cd_w_in, cd_w_out, hy_conv_w, hy_conv_b, hy_f_w1, hy_f_b1, hy_f_freq, hy_f_w2, hy_f_b2, hy_f_w3, hy_bias, s5_a_re, s5_a_im, s5_log_dt, s5_b_re, s5_b_im, s5_c_re, s5_c_im, s5_d, glu_w, glu_b, moe_w_rg, moe_b_rg, moe_w_re, moe_b_re, moe_w1, moe_w3, moe_w2):
    bsz, seq, d = x.shape
    cl = ctx.shape[1]
    depth = ada_w.shape[0]
    n_lat = bsz * seq
    assert seq % TM == 0 and (bsz * cl) % TM == 0 and bsz + 1 <= 8
    stream = (x.reshape(n_lat, d), ctx.reshape(bsz * cl, d), 0)
    r = n_lat + bsz * cl
    cond = jnp.zeros((8, d), f32).at[0:bsz].set(c).at[bsz].set(c_ctx)
    mods = _ada_mods(cond, ada_w, ada_b)
    geom = dict(batch=bsz, seq=seq)

    for l in range(depth):
        last = l == depth - 1
        i = l // 2
        n_tiles = (n_lat if last else r) // TM
        g_mix = norm_mix_g[l].reshape(1, d)
        if l % 2 == 0:
            lam_init = 0.8 - 0.6 * math.exp(-0.3 * l)
            cos, sin = _rope_tables(seq)
            q, k, vt, p = _proj_rope(*stream, g_mix, mods[l], ab_w_in[i].astype(bf16), cos, sin, **geom)
            mix_a = _diff_attention(q, k, vt, diff_lam[i], diff_subln_g[i], lam_init, ctx_len=cl, **geom)
            mix_b = _pool(p, pool_w[i], pool_scale[i], ctx_len=cl, **geom)
            w_out = ab_w_out[i]
        else:
            assert stream[0] is stream[1]
            v, x1, x2, u = _proj_conv(stream[0], g_mix, mods[l], cd_w_in[i].astype(bf16), hy_conv_w[i], hy_conv_b[i],
                                      **geom)
            mix_a = _hyena(v, x1, x2, hy_f_w1[i], hy_f_b1[i], hy_f_freq[i],
                           hy_f_w2[i], hy_f_b2[i], hy_f_w3[i], hy_bias[i], **geom)
            mix_b = _s5(u, s5_a_re[i], s5_a_im[i], s5_log_dt[i], s5_b_re[i], s5_b_im[i],
                        s5_c_re[i], s5_c_im[i], s5_d[i], glu_w[i], glu_b[i], ctx_len=cl, **geom)
            w_out = cd_w_out[i]
        w_r = jnp.zeros((d, 128), f32).at[:, :MOE_GROUPS].set(moe_w_rg[l]) \
            .at[:, MOE_GROUPS:MOE_GROUPS + MOE_EXPERTS].set(moe_w_re[l]).astype(bf16)
        b_r = jnp.zeros((1, 128), f32).at[0, :MOE_GROUPS].set(moe_b_rg[l]) \
            .at[0, MOE_GROUPS:MOE_GROUPS + MOE_EXPERTS].set(moe_b_re[l])
        xn, h2, logits = _outproj(mix_a, mix_b, *stream, w_out.astype(bf16), mods[l], norm_ffn_g[l].reshape(1, d),
                                  w_r, b_r, n_tiles=n_tiles, **geom)
        xs = _moe(xn, h2, logits, mods[l], moe_w1, moe_w3, moe_w2, l, norm_out_g.reshape(1, d),
                  n_tiles=n_tiles, final_norm=last, **geom)
        stream = (xs, xs, n_lat // TM)
    return xs[:n_lat].reshape(bsz, seq, d)
```

```python
import functools
import math

import jax
import jax.numpy as jnp
from jax import lax
from jax.experimental import pallas as pl
from jax.experimental.pallas import tpu as pltpu

f32, bf16, i32 = jnp.float32, jnp.bfloat16, jnp.int32

EPS = 1e-6
GRID_W = 64
DIFF_QK = 64
DIFF_V = 128
ROPE_BASE = 10000.0
POOL_WINDOWS = (2, 4, 8, 16)
POOL_GROUP = 256
HY_WIDTH = 1024
HY_ORDER = 2
HY_EMB = 33
HY_FAST_DECAY = 0.3
HY_SLOW_DECAY = 1.5
HY_TARGET = 1e-2
S5_GROUP = 16
MOE_GROUPS = 8
MOE_PER_GROUP = 8
MOE_EXPERTS = 64
MOE_TOPK = 2
MOE_BLOCK = 256

VMEM_LIMIT = 56 * 1024 * 1024
TM = 512
LOG2E = 1.4426950408889634


def _cparams(sem):
    return pltpu.CompilerParams(dimension_semantics=sem, vmem_limit_bytes=VMEM_LIMIT)


def _norm_mod(x, g, shift, scale):
    ms = jnp.mean(x * x, axis=-1, keepdims=True)
    return (x * lax.rsqrt(ms + EPS) * g) * (1.0 + scale) + shift


def _tile_segment(i, lat_tiles_per_batch, batch):
    seg = jnp.zeros((), i32)
    for b in range(1, batch + 1):
        seg = seg + (i >= b * lat_tiles_per_batch).astype(i32)
    return seg


def _pack_rows(x):
    half = x.shape[1] // 2
    bits = lax.bitcast_convert_type(x.astype(bf16).astype(f32), jnp.uint32)
    return (bits[:, :half] >> 16) | (bits[:, half:] & jnp.uint32(0xFFFF0000))


def _unpack_rows(words):
    return (lax.bitcast_convert_type(words << 16, f32),
            lax.bitcast_convert_type(words & jnp.uint32(0xFFFF0000), f32))


def _ada_kernel(s_ref, w_ref, b_ref, o_ref):
    s = s_ref[...]
    a = (s * jax.nn.sigmoid(s)).astype(bf16)
    o_ref[...] = jnp.dot(a, w_ref[...].astype(bf16), preferred_element_type=f32) + b_ref[...]


def _ada_mods(s, ada_w, ada_b):
    depth, d, n = ada_w.shape
    tn = 1024
    return pl.pallas_call(
        _ada_kernel,
        grid=(depth, n // tn),
        in_specs=[pl.BlockSpec((8, d), lambda l, j: (0, 0)),
                  pl.BlockSpec((None, d, tn), lambda l, j: (l, 0, j)),
                  pl.BlockSpec((None, 1, tn), lambda l, j: (l, 0, j))],
        out_specs=pl.BlockSpec((None, 8, tn), lambda l, j: (l, 0, j)),
        out_shape=jax.ShapeDtypeStruct((depth, 8, n), f32),
        compiler_params=_cparams(("arbitrary", "arbitrary")),
        name="ada_mods",
    )(s, ada_w, ada_b.reshape(depth, 1, n))


def _proj_rope_kernel(x_ref, c_ref, g_ref, sh_ref, sc_ref, w_ref, cos_ref, sin_ref,
                      q_ref, k_ref, v_ref, p_ref, h_scr, *, tiles_per_batch, batch, qscale):
    i = pl.program_id(0)
    seg = _tile_segment(i, tiles_per_batch, batch)
    xin = jnp.where(i < batch * tiles_per_batch, x_ref[...], c_ref[...])
    h = _norm_mod(xin, g_ref[...], sh_ref[pl.ds(seg, 1), :], sc_ref[pl.ds(seg, 1), :])
    h_scr[...] = h.astype(bf16)
    cos = cos_ref[...]
    sin = sin_ref[...]
    tm = cos.shape[0]
    lane = lax.broadcasted_iota(i32, (tm, 128), 1)
    first = (lane % 32) < 16

    def rope(r):
        sw = jnp.where(first, pltpu.roll(r, 112, 1), pltpu.roll(r, 16, 1))
        return r * cos + sw * sin

    heads = q_ref.shape[0]
    nc = 512
    for c in range(w_ref.shape[1] // nc):
        res = jnp.dot(h_scr[...], w_ref[:, c * nc:(c + 1) * nc], preferred_element_type=f32)
        for s in range(nc // 128):
            r = res[:, s * 128:(s + 1) * 128]
            col = (c * nc) // 128 + s
            if col < heads:
                q_ref[col] = (rope(r) * qscale).astype(bf16)
            elif col < 2 * heads:
                k_ref[col - heads] = rope(r).astype(bf16)
            elif col < 3 * heads:
                v_ref[col - 2 * heads] = r.T.astype(bf16)
            else:
                c0 = (col - 3 * heads) * 128
                p_ref[:, c0:c0 + 128] = r


def _stream_specs(d, n_lat_tiles, ctx_off):
    return [pl.BlockSpec((TM, d), lambda i: (jnp.minimum(i, n_lat_tiles - 1), 0)),
            pl.BlockSpec((TM, d), lambda i: (jnp.maximum(i - n_lat_tiles, 0) + ctx_off, 0))]


def _proj_rope(x, xc, ctx_off, g, mods, w, cos, sin, *, batch, seq):
    d = x.shape[1]
    heads = 8
    lat_tiles = seq // TM
    n_tiles = batch * lat_tiles + (xc.shape[0] // TM - ctx_off)
    r = n_tiles * TM

    def tab_map(i):
        return (jnp.where(i < batch * lat_tiles, i % lat_tiles, lat_tiles), 0)

    hm = jax.ShapeDtypeStruct((heads, r, 128), bf16)
    hm_spec = pl.BlockSpec((heads, TM, 128), lambda i: (0, i, 0))
    return pl.pallas_call(
        functools.partial(_proj_rope_kernel, tiles_per_batch=lat_tiles, batch=batch,
                          qscale=LOG2E * DIFF_QK ** -0.5),
        grid=(n_tiles,),
        in_specs=_stream_specs(d, batch * lat_tiles, ctx_off) + [
                  pl.BlockSpec((1, d), lambda i: (0, 0)),
                  pl.BlockSpec((8, d), lambda i: (0, 0)),
                  pl.BlockSpec((8, d), lambda i: (0, 1)),
                  pl.BlockSpec(w.shape, lambda i: (0, 0), pipeline_mode=pl.Buffered(1)),
                  pl.BlockSpec((TM, 128), tab_map),
                  pl.BlockSpec((TM, 128), tab_map)],
        out_specs=[hm_spec, hm_spec, pl.BlockSpec((heads, 128, TM), lambda i: (0, 0, i)),
                   pl.BlockSpec((TM, 1024), lambda i: (i, 0))],
        out_shape=[hm, hm, jax.ShapeDtypeStruct((heads, 128, r), bf16), jax.ShapeDtypeStruct((r, 1024), f32)],
        scratch_shapes=[pltpu.VMEM((TM, d), bf16)],
        compiler_params=_cparams(("arbitrary",)),
        name="proj_rope",
    )(x, xc, g, mods, mods, w, cos, sin)


def _proj_conv_kernel(x_ref, prev_ref, next_ref, g_ref, sh_ref, sc_ref, w_ref, cw_ref, cb_ref,
                      v_ref, x1_ref, x2_ref, u_ref, h_scr, halo_scr, ext_scr, *, tiles_per_batch, batch):
    i = pl.program_id(0)
    seg = _tile_segment(i, tiles_per_batch, batch)
    g, sh, sc = g_ref[...], sh_ref[pl.ds(seg, 1), :], sc_ref[pl.ds(seg, 1), :]
    h_scr[...] = _norm_mod(x_ref[...], g, sh, sc).astype(bf16)
    halo = jnp.concatenate([prev_ref[...], next_ref[...]], axis=0)
    halo_scr[...] = _norm_mod(halo, g, sh, sc).astype(bf16)
    j = i % tiles_per_batch
    is_lat = i < batch * tiles_per_batch
    has_prev = jnp.logical_and(is_lat, j > 0)
    has_next = jnp.logical_and(is_lat, j < tiles_per_batch - 1)
    tm = x_ref.shape[0]
    nc = ext_scr.shape[2]
    width = v_ref.shape[1]
    outs = (v_ref, x1_ref, x2_ref)
    for c in range(w_ref.shape[1] // nc):
        w_c = w_ref[:, c * nc:(c + 1) * nc]
        res = jnp.dot(h_scr[...], w_c, preferred_element_type=f32)
        part, off = divmod(c * nc, width)
        if part < len(outs):
            edge = jnp.dot(halo_scr[...], w_c, preferred_element_type=f32)
            ext = ext_scr.at[c % 2]
            ext[0:8, :] = jnp.where(has_prev, edge[0:8], 0.0)
            ext[8:8 + tm, :] = res
            ext[8 + tm:16 + tm, :] = jnp.where(has_next, edge[8:16], 0.0)
            cols = slice(c * nc, (c + 1) * nc)
            cw = cw_ref[:, cols]
            outs[part][:, off:off + nc] = (ext[7:7 + tm, :] * cw[0:1] + ext[8:8 + tm, :] * cw[1:2]
                                           + ext[9:9 + tm, :] * cw[2:3] + cb_ref[:, cols])
        else:
            u_ref[:, off:off + nc] = res


def _proj_conv(x, g, mods, w, conv_w, conv_b, *, batch, seq):
    r, d = x.shape
    n = w.shape[1]
    width = n // 4
    nc = 512
    row = pl.BlockSpec((TM, width), lambda i: (i, 0))
    out = jax.ShapeDtypeStruct((r, width), f32)
    return pl.pallas_call(
        functools.partial(_proj_conv_kernel, tiles_per_batch=seq // TM, batch=batch),
        grid=(r // TM,),
        in_specs=[pl.BlockSpec((TM, d), lambda i: (i, 0)),
                  pl.BlockSpec((8, d), lambda i: (jnp.maximum(i * (TM // 8) - 1, 0), 0)),
                  pl.BlockSpec((8, d), lambda i: (jnp.minimum((i + 1) * (TM // 8), r // 8 - 1), 0)),
                  pl.BlockSpec((1, d), lambda i: (0, 0)),
                  pl.BlockSpec((8, d), lambda i: (0, 0)),
                  pl.BlockSpec((8, d), lambda i: (0, 1)),
                  pl.BlockSpec(w.shape, lambda i: (0, 0), pipeline_mode=pl.Buffered(1)),
                  pl.BlockSpec(conv_w.shape, lambda i: (0, 0)),
                  pl.BlockSpec((1, conv_w.shape[1]), lambda i: (0, 0))],
        out_specs=[row, row, row, row],
        out_shape=[out, out, out, out],
        scratch_shapes=[pltpu.VMEM((TM, d), bf16), pltpu.VMEM((16, d), bf16), pltpu.VMEM((2, TM + 16, nc), f32)],
        compiler_params=_cparams(("arbitrary",)),
        name="proj_conv",
    )(x, x, x, g, mods, mods, w, conv_w, conv_b.reshape(1, -1))


ATTN_ONES = 16
ATTN_UNROLL = 8


def _attn_kernel(lv_ref, g_ref, q_ref, kc_ref, vc_ref, *rest, n_lat, tk, lam_init):
    if n_lat:
        kl_ref, vl_ref, o_ref, s_scr = rest
    else:
        o_ref, s_scr = rest
    lv = lv_ref[...]
    lam = (jnp.exp(jnp.sum(lv[0:1] * lv[1:2], axis=-1, keepdims=True))
           - jnp.exp(jnp.sum(lv[2:3] * lv[3:4], axis=-1, keepdims=True)) + lam_init)
    q = q_ref[...]
    tq = q.shape[0]
    lane = lax.broadcasted_iota(i32, q.shape, 1)
    zero = jnp.zeros_like(q)
    qs = (jnp.where(lane < DIFF_QK, q, zero), jnp.where(lane >= DIFF_QK, q, zero))
    nt = (((1,), (1,)), ((), ()))

    def scores(slot, k):
        for m in range(2):
            s_scr[slot, m, 0:k.shape[0], :] = lax.dot_general(k, qs[m], nt, preferred_element_type=f32)

    def update(carry, slot, nk, vt):
        vt_ext = jnp.concatenate([vt, jnp.ones((ATTN_ONES, vt.shape[1]), bf16)], axis=0)
        new = []
        for m in range(2):
            mx, acc = carry[2 * m:2 * m + 2]
            s = s_scr[slot, m, 0:nk, :]
            mn = jnp.maximum(mx, jnp.max(s, axis=0, keepdims=True))
            p = jnp.exp2(s - mn).astype(bf16)
            acc = jnp.exp2(mx - mn) * acc + jnp.dot(vt_ext, p, preferred_element_type=f32)
            new += [mn, acc]
        return tuple(new)

    def k_block(j):
        return kl_ref[pl.ds(pl.multiple_of(j * tk, tk), tk), :]

    def vt_block(j):
        return vl_ref[:, pl.ds(pl.multiple_of(j * tk, tk), tk)]

    init = (jnp.full((1, tq), -jnp.inf, f32), jnp.zeros((DIFF_V + ATTN_ONES, tq), f32)) * 2
    nc = kc_ref.shape[0]
    scores(1, kc_ref[...])
    if n_lat:
        scores(0, k_block(0))
        carry = update(init, 1, nc, vc_ref[...])

        def body(jj, carry):
            j = ATTN_UNROLL * jj
            for t in range(ATTN_UNROLL):
                nxt = j + t + 1 if t + 1 < ATTN_UNROLL else jnp.minimum(j + ATTN_UNROLL, n_lat - 1)
                scores((t + 1) % 2, k_block(nxt))
                carry = update(carry, t % 2, tk, vt_block(j + t))
            return carry

        carry = lax.fori_loop(0, n_lat // ATTN_UNROLL, body, carry)
    else:
        carry = update(init, 1, nc, vc_ref[...])
    _, a0, _, a1 = carry
    o = (a0[:DIFF_V] / a0[DIFF_V:DIFF_V + 1]
         - lam * (a1[:DIFF_V] / a1[DIFF_V:DIFF_V + 1]))
    ms = jnp.mean(o * o, axis=0, keepdims=True)
    y = (o * lax.rsqrt(ms + EPS) * g_ref[...]) * (1.0 - lam_init)
    o_ref[...] = y.T.astype(o_ref.dtype)


def _diff_attention(q, k, vt, lam_vecs, subln_g, lam_init, *, batch, seq, ctx_len):
    heads, r, _ = q.shape
    tq, tk = 512, 512
    ctx_blk0 = (batch * seq) // ctx_len
    assert (seq // tk) % ATTN_UNROLL == 0
    small = [pl.BlockSpec((4, DIFF_QK), lambda *_: (0, 0)), pl.BlockSpec((DIFF_V, 1), lambda *_: (0, 0))]
    kc_spec = pl.BlockSpec((None, ctx_len, 128), lambda b, h, *_: (h, ctx_blk0 + b, 0))
    vc_spec = pl.BlockSpec((None, 128, ctx_len), lambda b, h, *_: (h, 0, ctx_blk0 + b))
    g2 = subln_g.reshape(DIFF_V, 1)
    lat = pl.pallas_call(
        functools.partial(_attn_kernel, n_lat=seq // tk, tk=tk, lam_init=lam_init),
        grid=(batch, heads, seq // tq),
        in_specs=small + [pl.BlockSpec((None, tq, 128), lambda b, h, i: (h, b * (seq // tq) + i, 0)),
                          kc_spec, vc_spec,
                          pl.BlockSpec((None, seq, 128), lambda b, h, i: (h, b, 0)),
                          pl.BlockSpec((None, 128, seq), lambda b, h, i: (h, 0, b))],
        out_specs=pl.BlockSpec((tq, 128), lambda b, h, i: (b * (seq // tq) + i, h)),
        out_shape=jax.ShapeDtypeStruct((batch * seq, heads * DIFF_V), bf16),
        scratch_shapes=[pltpu.VMEM((2, 2, tk, tq), f32)],
        compiler_params=_cparams(("arbitrary",) * 3),
        name="diff_attn_lat",
    )(lam_vecs, g2, q, k, vt, k, vt)
    ctx = pl.pallas_call(
        functools.partial(_attn_kernel, n_lat=0, tk=tk, lam_init=lam_init),
        grid=(batch, heads),
        in_specs=small + [kc_spec, kc_spec, vc_spec],
        out_specs=pl.BlockSpec((ctx_len, 128), lambda b, h: (b, h)),
        out_shape=jax.ShapeDtypeStruct((batch * ctx_len, heads * DIFF_V), bf16),
        scratch_shapes=[pltpu.VMEM((2, 2, ctx_len, ctx_len), f32)],
        compiler_params=_cparams(("arbitrary",) * 2),
        name="diff_attn_ctx",
    )(lam_vecs, g2, q, k, vt)
    return jnp.concatenate([lat, ctx], axis=0)


def _pool_kernel(p_ref, prev_ref, next_ref, w_ref, scale_ref, o_ref, ext_scr, *,
                 tiles_per_batch, batch, seq, ctx_len):
    i = pl.program_id(0)
    tp = p_ref.shape[0]
    is_ctx = i >= batch * tiles_per_batch
    j = jnp.where(is_ctx, 0, i % tiles_per_batch)
    seg_len = jnp.where(is_ctx, ctx_len, seq)
    last = jnp.where(is_ctx, 0, tiles_per_batch - 1)
    ext_scr[0:8, :] = jnp.where(j > 0, prev_ref[...], 0.0)
    ext_scr[8:8 + tp, :] = p_ref[...]
    ext_scr[8 + tp:16 + tp, :] = jnp.where(j < last, next_ref[...], 0.0)
    pos = j * tp + lax.broadcasted_iota(i32, (tp, 1), 0)
    for gi, w in enumerate(POOL_WINDOWS):
        c0 = gi * POOL_GROUP
        half = w // 2
        tot = ext_scr[8 - half:8 - half + tp, c0:c0 + POOL_GROUP]
        for dlt in range(1 - half, half):
            tot = tot + ext_scr[8 + dlt:8 + dlt + tp, c0:c0 + POOL_GROUP]
        cnt = (jnp.minimum(pos - half + w, seg_len) - jnp.maximum(pos - half, 0)).astype(f32)
        dlt_mean = (tot / cnt - ext_scr[8:8 + tp, c0:c0 + POOL_GROUP]).astype(bf16)
        y = jnp.dot(dlt_mean, w_ref[gi].astype(bf16), preferred_element_type=f32)
        o_ref[:, c0:c0 + POOL_GROUP] = (y * scale_ref[:, c0:c0 + POOL_GROUP]).astype(o_ref.dtype)


def _pool(p, pool_w, pool_scale, *, batch, seq, ctx_len):
    r, width = p.shape
    tp = ctx_len
    tpb = seq // tp
    return pl.pallas_call(
        functools.partial(_pool_kernel, tiles_per_batch=tpb, batch=batch, seq=seq, ctx_len=ctx_len),
        grid=(r // tp,),
        in_specs=[pl.BlockSpec((tp, width), lambda i: (i, 0)),
                  pl.BlockSpec((8, width), lambda i: (jnp.maximum(i * (tp // 8) - 1, 0), 0)),
                  pl.BlockSpec((8, width), lambda i: (jnp.minimum((i + 1) * (tp // 8), r // 8 - 1), 0)),
                  pl.BlockSpec(pool_w.shape, lambda i: (0, 0, 0)),
                  pl.BlockSpec((1, width), lambda i: (0, 0))],
        out_specs=pl.BlockSpec((tp, width), lambda i: (i, 0)),
        out_shape=jax.ShapeDtypeStruct((r, width), bf16),
        scratch_shapes=[pltpu.VMEM((tp + 16, width), f32)],
        compiler_params=_cparams(("arbitrary",)),
        name="pool",
    )(p, p, p, pool_w, pool_scale.reshape(1, width))


def _outproj_kernel(a_ref, b_ref, x_ref, c_ref, w_ref, g1_ref, sh_ref, sc_ref, gn_ref, wr_ref, br_ref,
                    xn_ref, h2_ref, lg_ref, *, tiles_per_batch, batch):
    i = pl.program_id(0)
    seg = _tile_segment(i, tiles_per_batch, batch)
    xin = jnp.where(i < batch * tiles_per_batch, x_ref[...], c_ref[...])
    half = a_ref.shape[1]
    y = (jnp.dot(a_ref[...], w_ref[0:half, :], preferred_element_type=f32)
         + jnp.dot(b_ref[...], w_ref[half:, :], preferred_element_type=f32))
    xn = xin + g1_ref[pl.ds(seg, 1), :] * y
    xn_ref[...] = xn
    h2 = _norm_mod(xn, gn_ref[...], sh_ref[pl.ds(seg, 1), :], sc_ref[pl.ds(seg, 1), :])
    h2_ref[...] = _pack_rows(h2)
    lg_ref[...] = jnp.dot(h2.astype(bf16), wr_ref[...], preferred_element_type=f32) + br_ref[...]


def _outproj(a, b, x, xc, ctx_off, w, mods, g_ffn, w_r, b_r, *, n_tiles, batch, seq):
    r, d = n_tiles * TM, x.shape[1]
    half = a.shape[1]
    row = lambda i: (i, 0)
    return pl.pallas_call(
        functools.partial(_outproj_kernel, tiles_per_batch=seq // TM, batch=batch),
        grid=(n_tiles,),
        in_specs=[pl.BlockSpec((TM, half), row), pl.BlockSpec((TM, half), row)]
        + _stream_specs(d, batch * (seq // TM), ctx_off) + [
                  pl.BlockSpec(w.shape, lambda i: (0, 0), pipeline_mode=pl.Buffered(1)),
                  pl.BlockSpec((8, d), lambda i: (0, 2)),
                  pl.BlockSpec((8, d), lambda i: (0, 3)),
                  pl.BlockSpec((8, d), lambda i: (0, 4)),
                  pl.BlockSpec((1, d), lambda i: (0, 0)),
                  pl.BlockSpec(w_r.shape, lambda i: (0, 0)),
                  pl.BlockSpec((1, 128), lambda i: (0, 0))],
        out_specs=[pl.BlockSpec((TM, d), row), pl.BlockSpec((TM, d // 2), row), pl.BlockSpec((TM, 128), row)],
        out_shape=[jax.ShapeDtypeStruct((r, d), f32), jax.ShapeDtypeStruct((r, d // 2), jnp.uint32),
                   jax.ShapeDtypeStruct((r, 128), f32)],
        compiler_params=_cparams(("arbitrary",)),
        name="outproj",
    )(a, b, x, xc, w, mods, mods, mods, g_ffn, w_r, b_r)


def _moe_kernel(be_ref, nu_ref, nxt_ref, par_ref, xs_ref, w1_hbm, w3_hbm, w2_hbm, ys_ref,
                w1_f32, w3_f32, w2_f32, w1_scr, w3_scr, w2_scr, sems, *, layer):
    i = pl.program_id(0)
    e = be_ref[i]
    prev = be_ref[jnp.maximum(i - 1, 0)]
    used = i < nu_ref[0]

    def weight_copies(expert, slot):
        return (pltpu.make_async_copy(w1_hbm.at[layer, expert], w1_f32.at[slot], sems.at[slot, 0]),
                pltpu.make_async_copy(w3_hbm.at[layer, expert], w3_f32.at[slot], sems.at[slot, 1]),
                pltpu.make_async_copy(w2_hbm.at[layer, expert], w2_f32.at[slot], sems.at[slot, 2]))

    @pl.when(i == 0)
    def _():
        for cp in weight_copies(e, par_ref[e]):
            cp.start()

    @pl.when(used & ((i == 0) | (e != prev)))
    def _():
        slot = par_ref[e]
        nxt = nxt_ref[e]

        @pl.when(nxt >= 0)
        def _():
            for cp in weight_copies(nxt, 1 - slot):
                cp.start()

        for cp in weight_copies(e, slot):
            cp.wait()
        w1_scr[...] = w1_f32[slot].astype(bf16)
        w3_scr[...] = w3_f32[slot].astype(bf16)
        w2_scr[...] = w2_f32[slot].astype(bf16)

    @pl.when(used)
    def _():
        lo, hi = _unpack_rows(xs_ref[...])
        half = lo.shape[1]
        lo, hi = lo.astype(bf16), hi.astype(bf16)
        a = (jnp.dot(lo, w1_scr[0:half, :], preferred_element_type=f32)
             + jnp.dot(hi, w1_scr[half:, :], preferred_element_type=f32))
        b = (jnp.dot(lo, w3_scr[0:half, :], preferred_element_type=f32)
             + jnp.dot(hi, w3_scr[half:, :], preferred_element_type=f32))
        hid = (a * jax.nn.sigmoid(a) * b).astype(bf16)
        ys_ref[...] = _pack_rows(jnp.dot(hid, w2_scr[...], preferred_element_type=f32))

    @pl.when(jnp.logical_not(used))
    def _():
        ys_ref[...] = jnp.zeros_like(ys_ref)


def _moe_experts(xs, block_e, n_used, next_expert, parity, w1, w3, w2, layer, nb):
    m = MOE_BLOCK
    _, _, d, de = w1.shape
    hbm = pl.BlockSpec(memory_space=pl.ANY)
    grid_spec = pltpu.PrefetchScalarGridSpec(
        num_scalar_prefetch=4,
        grid=(nb,),
        in_specs=[pl.BlockSpec((m, d // 2), lambda i, be, nu, nx, pa: (jnp.minimum(i, nu[0] - 1), 0)), hbm, hbm, hbm],
        out_specs=pl.BlockSpec((m, d // 2), lambda i, be, nu, nx, pa: (i, 0)),
        scratch_shapes=[pltpu.VMEM((2, d, de), f32), pltpu.VMEM((2, d, de), f32), pltpu.VMEM((2, de, d), f32),
                        pltpu.VMEM((d, de), bf16), pltpu.VMEM((d, de), bf16), pltpu.VMEM((de, d), bf16),
                        pltpu.SemaphoreType.DMA((2, 3))],
    )
    return pl.pallas_call(
        functools.partial(_moe_kernel, layer=layer),
        grid_spec=grid_spec,
        out_shape=jax.ShapeDtypeStruct((nb * m, d // 2), jnp.uint32),
        compiler_params=_cparams(("arbitrary",)),
        name="moe_experts",
    )(block_e, n_used, next_expert, parity, xs, w1, w3, w2)


def _route_kernel(lg_ref, tri_ref, o_ref, cnt_ref, run_scr):
    @pl.when(pl.program_id(0) == 0)
    def _():
        run_scr[...] = jnp.zeros_like(run_scr)

    lg = lg_ref[...]
    lane = lax.broadcasted_iota(i32, lg.shape, 1)
    ninf = -jnp.inf

    def first_lane(mask):
        return jnp.min(jnp.where(mask, lane, lg.shape[1]), axis=-1, keepdims=True)

    gl = jnp.where(lane < MOE_GROUPS, lg, ninf)
    gmax = jnp.max(gl, axis=-1, keepdims=True)
    pg_top = 1.0 / jnp.sum(jnp.exp(gl - gmax), axis=-1, keepdims=True)
    lo = MOE_GROUPS + first_lane(gl == gmax) * MOE_PER_GROUP
    sel = (lane >= lo) & (lane < lo + MOE_PER_GROUP)
    el = jnp.where(sel, lg, ninf)
    ee = jnp.exp(el - jnp.max(el, axis=-1, keepdims=True))
    pe = jnp.where(sel, ee / jnp.sum(ee, axis=-1, keepdims=True), -1.0)
    p1 = jnp.max(pe, axis=-1, keepdims=True)
    i1 = first_lane(pe == p1)
    pe2 = jnp.where(lane == i1, -1.0, pe)
    p2 = jnp.max(pe2, axis=-1, keepdims=True)
    i2 = first_lane(pe2 == p2)
    den = p1 + p2
    e1 = i1 - MOE_GROUPS
    e2 = i2 - MOE_GROUPS
    hit1 = lane == e1
    hit2 = lane == e2
    onehot = jnp.where(hit1, 1.0, jnp.where(hit2, 1.0, 0.0))
    base = jnp.dot(tri_ref[...], onehot.astype(bf16), preferred_element_type=f32) + run_scr[...]
    r1 = jnp.sum(jnp.where(hit1, base, 0.0), axis=-1, keepdims=True)
    r2 = jnp.sum(jnp.where(hit2, base, 0.0), axis=-1, keepdims=True)
    run_scr[...] = run_scr[...] + jnp.sum(onehot, axis=0, keepdims=True)
    cnt_ref[...] = run_scr[...]
    vals = (e1.astype(f32), e2.astype(f32), pg_top * p1 / den, pg_top * p2 / den, r1, r2)
    out = jnp.zeros(lg.shape, f32)
    for j, v in enumerate(vals):
        out = jnp.where(lane == j, v, out)
    o_ref[...] = out


def _route(logits):
    t, width = logits.shape
    tri = jnp.tril(jnp.ones((TM, TM), f32), -1).astype(bf16)
    return pl.pallas_call(
        _route_kernel,
        grid=(t // TM,),
        in_specs=[pl.BlockSpec((TM, width), lambda i: (i, 0)), pl.BlockSpec((TM, TM), lambda i: (0, 0))],
        out_specs=[pl.BlockSpec((TM, width), lambda i: (i, 0)), pl.BlockSpec((1, width), lambda i: (0, 0))],
        out_shape=[jax.ShapeDtypeStruct((t, width), f32), jax.ShapeDtypeStruct((1, width), f32)],
        scratch_shapes=[pltpu.VMEM((1, width), f32)],
        compiler_params=_cparams(("arbitrary",)),
        name="route",
    )(logits, tri)


def _dispatch_kernel(dest_ref, h_ref, init_ref, xs_ref, sem):
    del init_ref
    i = pl.program_id(0)
    tm = h_ref.shape[0]

    def row_copy(r, k):
        slot = dest_ref[(i * tm + r) * MOE_TOPK + k]
        return pltpu.make_async_copy(h_ref.at[pl.ds(r, 1)], xs_ref.at[pl.ds(slot, 1)], sem)

    def issue(r, carry):
        for k in range(MOE_TOPK):
            row_copy(r, k).start(priority=k % 2)
        return carry

    lax.fori_loop(0, tm, issue, 0, unroll=8)
    for k in range(MOE_TOPK):
        pltpu.make_async_copy(h_ref, xs_ref.at[pl.ds(0, tm)], sem).wait()


def _dispatch(h2, dest, nb):
    t, d = h2.shape
    rows = nb * MOE_BLOCK
    grid_spec = pltpu.PrefetchScalarGridSpec(
        num_scalar_prefetch=1,
        grid=(t // TM,),
        in_specs=[pl.BlockSpec((TM, d), lambda i, dst: (i, 0)), pl.BlockSpec(memory_space=pl.ANY)],
        out_specs=pl.BlockSpec(memory_space=pl.ANY),
        scratch_shapes=[pltpu.SemaphoreType.DMA(())],
    )
    return pl.pallas_call(
        _dispatch_kernel,
        grid_spec=grid_spec,
        out_shape=jax.ShapeDtypeStruct((rows, d), h2.dtype),
        input_output_aliases={2: 0},
        compiler_params=_cparams(("arbitrary",)),
        name="moe_dispatch",
    )(dest, h2, jnp.zeros((rows, d), h2.dtype))


def _combine_kernel(dest_ref, x_ref, rt_ref, g2_ref, gn_ref, ys_ref, o_ref, y_scr, sem, *,
                    tiles_per_batch, batch, final_norm):
    i = pl.program_id(0)
    tm = x_ref.shape[0]

    def row_copy(r, k):
        slot = dest_ref[(i * tm + r) * MOE_TOPK + k]
        return pltpu.make_async_copy(ys_ref.at[pl.ds(slot, 1)], y_scr.at[k, pl.ds(r, 1)], sem)

    def issue(r, carry):
        for k in range(MOE_TOPK):
            row_copy(r, k).start(priority=k % 2)
        return carry

    lax.fori_loop(0, tm, issue, 0, unroll=8)
    for k in range(MOE_TOPK):
        pltpu.make_async_copy(ys_ref.at[pl.ds(0, tm)], y_scr.at[k], sem).wait()
    seg = _tile_segment(i, tiles_per_batch, batch)
    rt = rt_ref[...]
    y_lo, y_hi = 0.0, 0.0
    for k in range(MOE_TOPK):
        lo, hi = _unpack_rows(y_scr[k])
        gate = rt[:, MOE_TOPK + k:MOE_TOPK + k + 1]
        y_lo = y_lo + gate * lo
        y_hi = y_hi + gate * hi
    x = x_ref[...] + g2_ref[pl.ds(seg, 1), :] * jnp.concatenate([y_lo, y_hi], axis=1)
    if final_norm:
        ms = jnp.mean(x * x, axis=-1, keepdims=True)
        x = x * lax.rsqrt(ms + EPS) * gn_ref[...]
    o_ref[...] = x


def _combine(x, ys, dest, routed, mods, g_out, *, n_tiles, batch, seq, final_norm):
    d = x.shape[1]
    row = lambda i, dst: (i, 0)
    grid_spec = pltpu.PrefetchScalarGridSpec(
        num_scalar_prefetch=1,
        grid=(n_tiles,),
        in_specs=[pl.BlockSpec((TM, d), row),
                  pl.BlockSpec((TM, routed.shape[1]), row),
                  pl.BlockSpec((8, d), lambda i, dst: (0, 5)),
                  pl.BlockSpec((1, d), lambda i, dst: (0, 0)),
                  pl.BlockSpec(memory_space=pl.ANY)],
        out_specs=pl.BlockSpec((TM, d), row),
        scratch_shapes=[pltpu.VMEM((MOE_TOPK, TM, d // 2), jnp.uint32), pltpu.SemaphoreType.DMA(())],
    )
    return pl.pallas_call(
        functools.partial(_combine_kernel, tiles_per_batch=seq // TM, batch=batch, final_norm=final_norm),
        grid_spec=grid_spec,
        out_shape=jax.ShapeDtypeStruct((n_tiles * TM, d), f32),
        compiler_params=_cparams(("arbitrary",)),
        name="moe_combine",
    )(dest, x, routed, mods, g_out, ys)


def _moe(xn, h2, logits, mods, w1, w3, w2, layer, g_out, *, n_tiles, batch, seq, final_norm):
    t = n_tiles * TM
    m = MOE_BLOCK
    routed, cnt = _route(logits)
    s = t * MOE_TOPK
    nb = s // m + MOE_EXPERTS
    expert = routed[:, 0:MOE_TOPK].astype(i32)
    rank = routed[:, 2 * MOE_TOPK:3 * MOE_TOPK].astype(i32)
    counts = cnt[0, :MOE_EXPERTS].astype(i32)
    padded = ((counts + m - 1) // m) * m
    pad_end = jnp.cumsum(padded)
    onehot = expert[:, :, None] == jnp.arange(MOE_EXPERTS, dtype=i32)[None, None, :]
    dest = (jnp.sum(jnp.where(onehot, (pad_end - padded)[None, None, :], 0), axis=-1) + rank).reshape(s)
    n_used = (pad_end[-1] // m).astype(i32)
    blk = jnp.arange(nb, dtype=i32)
    block_e = jnp.minimum(jnp.searchsorted(pad_end, jnp.minimum(blk, n_used - 1) * m, side='right'),
                          MOE_EXPERTS - 1).astype(i32)
    ids = jnp.arange(MOE_EXPERTS, dtype=i32)
    nonempty_from = lax.cummin(jnp.where(counts > 0, ids, MOE_EXPERTS), reverse=True)
    nxt = jnp.concatenate([nonempty_from[1:], jnp.full((1,), MOE_EXPERTS, i32)])
    next_expert = jnp.where(nxt < MOE_EXPERTS, nxt, -1).astype(i32)
    parity = ((jnp.cumsum((counts > 0).astype(i32)) - 1) % 2).astype(i32)
    xs = _dispatch(h2, dest, nb)
    ys = _moe_experts(xs, block_e, n_used.reshape(1), next_expert, parity, w1, w3, w2, layer, nb)
    return _combine(xn, ys, dest, routed, mods, g_out, n_tiles=n_tiles, batch=batch, seq=seq,
                    final_norm=final_norm)


HY_N2 = 128
HY_WB = 256


def _hy_filter_kernel(z_ref, w1_ref, b1_ref, fr_ref, w2_ref, b2_ref, w3_ref, dl_ref, k_ref, nrm_ref):
    i = pl.program_id(0)
    z = z_ref[...]
    fr = fr_ref[...]
    h = jnp.sin(fr[0:1] * (jnp.dot(z.astype(bf16), w1_ref[...], preferred_element_type=f32) + b1_ref[...]))
    h = jnp.sin(fr[1:2] * (jnp.dot(h.astype(bf16), w2_ref[...], preferred_element_type=f32) + b2_ref[...]))
    kk = jnp.dot(h.astype(bf16), w3_ref[...], preferred_element_type=f32)
    width = dl_ref.shape[1]
    decay = jnp.exp(-z[:, 0:1] * jnp.abs(dl_ref[...]))
    tl = z.shape[0]
    row = i * tl + lax.broadcasted_iota(i32, (tl, 1), 0)
    tot = []
    for blk in range(kk.shape[1] // width):
        kb = kk[:, blk * width:(blk + 1) * width] * decay
        if blk % 2 == 1:
            kb = jnp.where(row == 0, 0.0, kb)
        k_ref[:, blk * width:(blk + 1) * width] = kb
        tot.append(jnp.sum(jnp.abs(kb), axis=0, keepdims=True))
    s = jnp.concatenate(tot, axis=1)

    @pl.when(i == 0)
    def _():
        nrm_ref[...] = s

    @pl.when(i > 0)
    def _():
        nrm_ref[...] = nrm_ref[...] + s


def _hy_filters(seq, w1, b1, freq, w2, b2, w3):
    t = jnp.linspace(0.0, 1.0, seq, dtype=f32)[:, None]
    bands = (HY_EMB - 1) // 2
    w = 2 * math.pi * jnp.arange(seq, dtype=f32)[:, None] / seq
    f = jnp.linspace(1e-4, bands - 1, bands, dtype=f32)[None, :]
    z = jnp.concatenate([t, jnp.cos(f * w), -jnp.sin(f * w), jnp.zeros((seq, 128 - HY_EMB), f32)], axis=-1)
    w1p = jnp.concatenate([w1, jnp.zeros((128 - HY_EMB, w1.shape[1]), f32)], axis=0).astype(bf16)
    hid = w1.shape[1]
    n_out = w3.shape[1]
    deltas = jnp.linspace(math.log(HY_TARGET) / HY_SLOW_DECAY, math.log(HY_TARGET) / HY_FAST_DECAY,
                          HY_WIDTH, dtype=f32).reshape(1, HY_WIDTH)
    tl = 512
    const = lambda a: pl.BlockSpec(a.shape, lambda i: (0,) * a.ndim)
    ins = [z, w1p, b1.reshape(1, hid), freq, w2.astype(bf16), b2.reshape(1, hid), w3.astype(bf16), deltas]
    return pl.pallas_call(
        _hy_filter_kernel,
        grid=(seq // tl,),
        in_specs=[pl.BlockSpec((tl, 128), lambda i: (i, 0))] + [const(a) for a in ins[1:]],
        out_specs=[pl.BlockSpec((tl, n_out), lambda i: (i, 0)), pl.BlockSpec((1, n_out), lambda i: (0, 0))],
        out_shape=[jax.ShapeDtypeStruct((seq, n_out), f32), jax.ShapeDtypeStruct((1, n_out), f32)],
        compiler_params=_cparams(("arbitrary",)),
        name="hy_filter",
    )(*ins)


def _dft_tables(seq):
    n2 = HY_N2
    n_tot = 2 * seq
    n1 = n_tot // n2
    nf = n1 // 2 + 1
    nfp = -(-nf // 8) * 8
    f1 = jnp.arange(nfp, dtype=i32)
    k1 = jnp.arange(n1 // 2, dtype=i32)
    th = (2 * math.pi / n1) * ((f1[:, None] * k1[None, :]) % n1).astype(f32)
    valid = (f1 < nf)[:, None]
    fwd = jnp.concatenate([jnp.where(valid, jnp.cos(th), 0.0), jnp.where(valid, -jnp.sin(th), 0.0)], axis=0)
    eye = jnp.eye(8, dtype=f32)
    fk = jnp.einsum('rk,jh->rjkh', fwd, eye).reshape(2 * nfp * 8, (n1 // 2) * 8)
    cw = jnp.where(valid, jnp.where((f1 == 0) | (f1 == n1 // 2), 1.0, 2.0)[:, None], 0.0) / n_tot
    inv = jnp.concatenate([(cw * jnp.cos(th)).T, (-cw * jnp.sin(th)).T], axis=1)
    ck = jnp.einsum('kr,jh->kjrh', inv, eye).reshape((n1 // 2) * 8, 2 * nfp * 8)
    fa = jnp.arange(nf, dtype=i32)[:, None, None]
    f2 = jnp.arange(n2, dtype=i32)[None, :, None]
    m2 = jnp.arange(n2, dtype=i32)[None, None, :]
    ph = (2 * math.pi / n_tot) * ((f2 * m2 * n1 + m2 * fa) % n_tot).astype(f32)
    gr, gi = jnp.cos(ph), -jnp.sin(ph)
    g = jnp.concatenate([jnp.concatenate([gr, -gi], axis=2), jnp.concatenate([gi, gr], axis=2)], axis=1)
    hr, hi = jnp.swapaxes(jnp.cos(ph), 1, 2), jnp.swapaxes(jnp.sin(ph), 1, 2)
    h = jnp.concatenate([jnp.concatenate([hr, -hi], axis=2), jnp.concatenate([hi, hr], axis=2)], axis=1)
    return fk.astype(bf16), ck.astype(bf16), g.astype(bf16), h.astype(bf16), nf, nfp


def _hy_rows_per_step(nf):
    return max(d for d in (5, 4, 3, 2, 1) if nf % d == 0)


def _dft_stage1(x_ref, fk_ref, a_scr):
    n1h, n2, wb = x_ref.shape

    def slab(s, carry):
        rows = pl.ds(pl.multiple_of(s * 8, 8), 8)
        xs = x_ref[:, rows, :].reshape(n1h * 8, wb).astype(bf16)
        a = jnp.dot(fk_ref[...], xs, preferred_element_type=f32)
        a_scr[:, rows, :] = a.reshape(a_scr.shape[0], 8, wb)
        return carry

    lax.fori_loop(0, n2 // 8, slab, 0)


def _dft_stage2(a_scr, g, f1, nfp):
    a = jnp.concatenate([a_scr[f1], a_scr[nfp + f1]], axis=0).astype(bf16)
    return jnp.dot(g, a, preferred_element_type=f32)


def _hy_spectrum_kernel(k_ref, fk_ref, g_ref, o_ref, a_scr, *, nfp):
    fs = pl.program_id(2)

    @pl.when(fs == 0)
    def _():
        _dft_stage1(k_ref, fk_ref, a_scr)

    fb = g_ref.shape[0]
    n2 = k_ref.shape[1]
    for j in range(fb):
        s = _dft_stage2(a_scr, g_ref[j], fs * fb + j, nfp)
        o_ref[0, j] = s[:n2]
        o_ref[1, j] = s[n2:]


def _hy_spectrum(k, fk, g, nf, nfp, seq):
    n2, wb = HY_N2, HY_WB
    n1h = seq // n2
    fb = _hy_rows_per_step(nf)
    nwb = HY_WIDTH // wb
    n_blk = k.shape[1] // HY_WIDTH
    k3 = k.reshape(n1h, n2, k.shape[1])
    return pl.pallas_call(
        functools.partial(_hy_spectrum_kernel, nfp=nfp),
        grid=(n_blk, nwb, nf // fb),
        in_specs=[pl.BlockSpec((n1h, n2, wb), lambda o, w, f: (0, 0, o * nwb + w), pipeline_mode=pl.Buffered(1)),
                  pl.BlockSpec(fk.shape, lambda o, w, f: (0, 0)),
                  pl.BlockSpec((fb, 2 * n2, 2 * n2), lambda o, w, f: (f, 0, 0))],
        out_specs=pl.BlockSpec((None, 2, fb, n2, wb), lambda o, w, f: (o, 0, f, 0, w)),
        out_shape=jax.ShapeDtypeStruct((n_blk, 2, nf, n2, HY_WIDTH), f32),
        scratch_shapes=[pltpu.VMEM((2 * nfp, n2, wb), f32)],
        compiler_params=_cparams(("arbitrary",) * 3),
        name="hy_spectrum",
    )(k3, fk, g)


def _hy_conv_kernel(x_ref, gate_ref, kf_ref, kb_ref, g_ref, h_ref, fk_ref, ck_ref, bias_ref, inv_ref, o_ref, a_scr, *,
                    nfp):
    fs = pl.program_id(2)
    n1h, n2, wb = x_ref.shape

    @pl.when(fs == 0)
    def _():
        _dft_stage1(x_ref, fk_ref, a_scr)

    fb = g_ref.shape[0]
    for j in range(fb):
        f1 = fs * fb + j
        xs = _dft_stage2(a_scr, g_ref[j], f1, nfp)
        xr, xi = xs[:n2], xs[n2:]
        kr = kf_ref[0, j] + kb_ref[0, j]
        ki = kf_ref[1, j] - kb_ref[1, j]
        y = jnp.concatenate([xr * kr - xi * ki, xr * ki + xi * kr], axis=0).astype(bf16)
        bb = jnp.dot(h_ref[j], y, preferred_element_type=f32)
        a_scr[f1] = bb[:n2]
        a_scr[nfp + f1] = bb[n2:]

    @pl.when(fs == pl.num_programs(2) - 1)
    def _():
        bias = bias_ref[...].reshape(1, 1, wb)
        inv = inv_ref[...].reshape(1, 1, wb)

        def slab(s, carry):
            rows = pl.ds(pl.multiple_of(s * 8, 8), 8)
            bs = a_scr[:, rows, :].reshape(2 * nfp * 8, wb).astype(bf16)
            y = jnp.dot(ck_ref[...], bs, preferred_element_type=f32).reshape(n1h, 8, wb)
            o_ref[:, rows, :] = (gate_ref[:, rows, :] * (y * inv + x_ref[:, rows, :] * bias)).astype(o_ref.dtype)
            return carry

        lax.fori_loop(0, n2 // 8, slab, 0)


def _hy_conv(x, gate, kf, order, tabs, bias, inv_norm, out_dtype, *, batch, seq):
    fk, ck, g, h, nf, nfp = tabs
    n2, wb = HY_N2, HY_WB
    n1h = seq // n2
    fb = _hy_rows_per_step(nf)
    width = x.shape[1]
    big = pl.BlockSpec((n1h, n2, wb), lambda b, w, f: (b, 0, w), pipeline_mode=pl.Buffered(1))
    out = pl.pallas_call(
        functools.partial(_hy_conv_kernel, nfp=nfp),
        grid=(batch, width // wb, nf // fb),
        in_specs=[big, big,
                  pl.BlockSpec((None, 2, fb, n2, wb), lambda b, w, f: (2 * order, 0, f, 0, w)),
                  pl.BlockSpec((None, 2, fb, n2, wb), lambda b, w, f: (2 * order + 1, 0, f, 0, w)),
                  pl.BlockSpec((fb, 2 * n2, 2 * n2), lambda b, w, f: (f, 0, 0)),
                  pl.BlockSpec((fb, 2 * n2, 2 * n2), lambda b, w, f: (f, 0, 0)),
                  pl.BlockSpec(fk.shape, lambda b, w, f: (0, 0)),
                  pl.BlockSpec(ck.shape, lambda b, w, f: (0, 0)),
                  pl.BlockSpec((1, wb), lambda b, w, f: (0, w)),
                  pl.BlockSpec((1, wb), lambda b, w, f: (0, w))],
        out_specs=big,
        out_shape=jax.ShapeDtypeStruct((batch * n1h, n2, width), out_dtype),
        scratch_shapes=[pltpu.VMEM((2 * nfp, n2, wb), f32)],
        compiler_params=_cparams(("arbitrary",) * 3),
        name="hy_conv",
    )(x.reshape(-1, n2, width), gate.reshape(-1, n2, width), kf, kf, g, h, fk, ck,
      bias.reshape(1, width), inv_norm.reshape(1, width))
    return out.reshape(batch * seq, width)


def _hyena(v, x1, x2, f_w1, f_b1, f_freq, f_w2, f_b2, f_w3, hy_bias, *, batch, seq):
    k, sums = _hy_filters(seq, f_w1, f_b1, f_freq, f_w2, f_b2, f_w3)
    sums = sums.reshape(HY_ORDER, 2, HY_WIDTH)
    inv_norm = 1.0 / (sums[:, 0] + sums[:, 1])
    tabs = _dft_tables(seq)
    kf = _hy_spectrum(k, tabs[0], tabs[2], tabs[4], tabs[5], seq)
    z = _hy_conv(v, x1, kf, 0, tabs, hy_bias[0], inv_norm[0], f32, batch=batch, seq=seq)
    return _hy_conv(z, x2, kf, 1, tabs, hy_bias[1], inv_norm[1], bf16, batch=batch, seq=seq)


S5_TT = 256
S5_GB = 8


def _gelu_tanh(x):
    return 0.5 * x * (1.0 + jnp.tanh(0.7978845608028654 * (x + 0.044715 * x * x * x)))


def _s5_kernel(u_ref, perm_ref, permt_ref, wb_ref, wc_ref, are_ref, aim_ref, pre_ref, pim_ref, *rest, fuse_glu):
    if fuse_glu:
        yf_ref, d_ref, gw_ref, gb_ref, o_ref, st_scr, bu_scr, y_scr = rest
    else:
        o_ref, st_scr, bu_scr, y_scr = rest
    tt = u_ref.shape[0]
    steps = tt // 8
    half = bu_scr.shape[2] // 2

    @pl.when(pl.program_id(1) == 0)
    def _():
        st_scr[...] = jnp.zeros_like(st_scr)

    up = jnp.dot(perm_ref[...], u_ref[...].astype(bf16), preferred_element_type=f32).astype(bf16)
    lock = 8
    zero = jnp.zeros((8, half), f32)
    for g0 in range(0, wb_ref.shape[0], lock):
        gbs = range(g0, g0 + lock)
        for gb in gbs:
            bu_scr[gb] = jnp.dot(up[:, gb * 128:(gb + 1) * 128], wb_ref[gb], preferred_element_type=f32)
        lam = {gb: (jnp.broadcast_to(are_ref[gb], (8, half)), jnp.broadcast_to(aim_ref[gb], (8, half)))
               for gb in gbs}

        def scan_step(gb, k, h):
            hr, hi = h
            a_re, a_im = lam[gb]
            rows = pl.ds(k * 8, 8)
            nr = a_re * hr - a_im * hi + bu_scr[gb, rows, 0:half]
            ni = a_re * hi + a_im * hr + bu_scr[gb, rows, half:]
            bu_scr[gb, rows, 0:half] = nr
            bu_scr[gb, rows, half:] = ni
            return nr, ni

        h = {gb: (zero, zero) for gb in gbs}
        for k in range(steps):
            for gb in gbs:
                h[gb] = scan_step(gb, k, h[gb])

        carry = {}
        for gb in gbs:
            hr, hi = h[gb]
            pw_r = pre_ref[gb, steps - 1:steps, :]
            pw_i = pim_ref[gb, steps - 1:steps, :]
            cur_r = st_scr[gb, :, 0:half]
            cur_i = st_scr[gb, :, half:]
            car_r, car_i = [], []
            for j in range(8):
                car_r.append(cur_r)
                car_i.append(cur_i)
                cur_r, cur_i = (hr[j:j + 1] + pw_r * cur_r - pw_i * cur_i,
                                hi[j:j + 1] + pw_r * cur_i + pw_i * cur_r)
            st_scr[gb, :, 0:half] = cur_r
            st_scr[gb, :, half:] = cur_i
            carry[gb] = (jnp.concatenate(car_r, axis=0), jnp.concatenate(car_i, axis=0))

        for k in range(steps):
            for gb in gbs:
                cr, ci = carry[gb]
                rows = pl.ds(k * 8, 8)
                pr = pre_ref[gb, pl.ds(k, 1), :]
                pi = pim_ref[gb, pl.ds(k, 1), :]
                bu_scr[gb, rows, 0:half] = bu_scr[gb, rows, 0:half] + (pr * cr - pi * ci)
                bu_scr[gb, rows, half:] = bu_scr[gb, rows, half:] + (pr * ci + pi * cr)
        for gb in gbs:
            y_scr[:, gb * 128:(gb + 1) * 128] = jnp.dot(bu_scr[gb].astype(bf16), wc_ref[gb],
                                                          preferred_element_type=f32)
    yp = y_scr[...]
    y_hi = yp.astype(bf16)
    y_lo = (yp - y_hi.astype(f32)).astype(bf16)
    y = (jnp.dot(permt_ref[...], y_hi, preferred_element_type=f32)
         + jnp.dot(permt_ref[...], y_lo, preferred_element_type=f32))
    if fuse_glu:
        g = _gelu_tanh(y + yf_ref[...] + u_ref[...] * d_ref[...])
        z = jnp.dot(g.astype(bf16), gw_ref[...], preferred_element_type=f32) + gb_ref[...]
        o_ref[...] = (g * jax.nn.sigmoid(z)).astype(o_ref.dtype)
    else:
        o_ref[...] = y


def _s5_tables(a_re, a_im, log_dt, b_re, b_im, c_re, c_im, steps):
    ng, ns = a_re.shape
    nb = ng // S5_GB
    lam = lax.complex(a_re, a_im)
    dt = jnp.exp(log_dt)[:, None]
    lam_bar = jnp.exp(lam * dt)
    b_bar = ((lam_bar - 1) / lam)[..., None] * lax.complex(b_re, b_im)
    cm = lax.complex(c_re, c_im)
    pw = jnp.exp((lam * dt)[None] * jnp.arange(1, steps + 1, dtype=f32)[:, None, None])
    eye = jnp.eye(S5_GB, dtype=f32)
    bb = b_bar.reshape(nb, S5_GB, ns, S5_GROUP)
    wb = jnp.concatenate([jnp.einsum('bgpc,gh->bgchp', part, eye).reshape(nb, S5_GB * S5_GROUP, S5_GB * ns)
                          for part in (jnp.real(bb), jnp.imag(bb))], axis=-1)
    cc = cm.reshape(nb, S5_GB, S5_GROUP, ns)
    wc = jnp.concatenate([jnp.einsum('bgcp,gh->bgphc', part, eye).reshape(nb, S5_GB * ns, S5_GB * S5_GROUP)
                          for part in (jnp.real(cc), -jnp.imag(cc))], axis=1)
    lb = lam_bar.reshape(nb, 1, S5_GB * ns)
    pwb = pw.reshape(steps, nb, S5_GB * ns).transpose(1, 0, 2)
    return (wb.astype(bf16), wc.astype(bf16), jnp.real(lb), jnp.imag(lb), jnp.real(pwb), jnp.imag(pwb))


def _s5_perm(tt, reverse):
    steps = tt // 8
    dst = jnp.arange(tt)
    t = (dst % 8) * steps + dst // 8
    src = tt - 1 - t if reverse else t
    return jnp.arange(tt)[None, :] == src[:, None]


def _s5(u, a_re, a_im, log_dt, b_re, b_im, c_re, c_im, s5_d, glu_w, glu_b, *, batch, seq, ctx_len):
    r, width = u.shape
    tt = S5_TT
    assert ctx_len == tt and seq % tt == 0
    n_lat = seq // tt
    ctx_blk0 = (batch * seq) // tt
    y = None
    for di in range(2):
        rev = di == 1
        wb, wc, lr, li, pr, pi = _s5_tables(a_re[di], a_im[di], log_dt[di], b_re[di], b_im[di],
                                            c_re[di], c_im[di], tt // 8)
        perm = _s5_perm(tt, rev)

        def tile_map(b, i, rev=rev):
            lat = b * n_lat + (n_lat - i if rev else i - 1)
            return (jnp.where(i == 0, ctx_blk0 + b, lat), 0)

        row_spec = pl.BlockSpec((tt, width), tile_map)
        const = lambda a: pl.BlockSpec(a.shape, lambda b, i: (0,) * a.ndim)
        ins = [u, perm.astype(bf16), perm.T.astype(bf16), wb, wc, lr, li, pr, pi]
        in_specs = [row_spec] + [const(a) for a in ins[1:]]
        if rev:
            extra = [y, s5_d.reshape(1, width), glu_w.astype(bf16), glu_b.reshape(1, width)]
            in_specs += [row_spec] + [const(a) for a in extra[1:]]
            ins += extra
        y = pl.pallas_call(
            functools.partial(_s5_kernel, fuse_glu=rev),
            grid=(batch, n_lat + 1),
            in_specs=in_specs,
            out_specs=row_spec,
            out_shape=jax.ShapeDtypeStruct((r, width), bf16 if rev else f32),
            scratch_shapes=[pltpu.VMEM((wb.shape[0], 1, wb.shape[2]), f32),
                            pltpu.VMEM((wb.shape[0], tt, wb.shape[2]), f32),
                            pltpu.VMEM((tt, width), f32)],
            compiler_params=_cparams(("arbitrary", "arbitrary")),
            name="s5_bwd_glu" if rev else "s5_fwd",
        )(*ins)
    return y


def _rope_tables(seq):
    rows = seq // GRID_W
    row = jnp.repeat(jnp.arange(rows, dtype=f32), GRID_W)
    col = jnp.tile(jnp.arange(GRID_W, dtype=f32), rows)
    n_freq = DIFF_QK // 4
    inv = ROPE_BASE ** (-jnp.arange(n_freq, dtype=f32) / n_freq)
    ar, ac = row[:, None] * inv, col[:, None] * inv
    cos = jnp.concatenate([jnp.cos(ar), jnp.cos(ar), jnp.cos(ac), jnp.cos(ac)], axis=-1)
    sin = jnp.concatenate([-jnp.sin(ar), jnp.sin(ar), -jnp.sin(ac), jnp.sin(ac)], axis=-1)
    cos = jnp.concatenate([jnp.tile(cos, (1, 2)), jnp.ones((TM, 128), f32)], axis=0)
    sin = jnp.concatenate([jnp.tile(sin, (1, 2)), jnp.zeros((TM, 128), f32)], axis=0)
    return cos, sin


def kernel(x, c, ctx, c_ctx, ada_w, ada_b, norm_mix_g, norm_ffn_g, norm_out_g, ab_w_in, ab_w_out, diff_lam, diff_subln_g, pool_w, pool_scale, cd_w_in, cd_w_out, hy_conv_w, hy_conv_b, hy_f_w1, hy_f_b1, hy_f_freq, hy_f_w2, hy_f_b2, hy_f_w3, hy_bias, s5_a_re, s5_a_im, s5_log_dt, s5_b_re, s5_b_im, s5_c_re, s5_c_im, s5_d, glu_w, glu_b, moe_w_rg, moe_b_rg, moe_w_re, moe_b_re, moe_w1, moe_w3, moe_w2):
    bsz, seq, d = x.shape
    cl = ctx.shape[1]
    depth = ada_w.shape[0]
    n_lat = bsz * seq
    assert seq % TM == 0 and (bsz * cl) % TM == 0 and bsz + 1 <= 8
    stream = (x.reshape(n_lat, d), ctx.reshape(bsz * cl, d), 0)
    r = n_lat + bsz * cl
    cond = jnp.zeros((8, d), f32).at[0:bsz].set(c).at[bsz].set(c_ctx)
    mods = _ada_mods(cond, ada_w, ada_b)
    geom = dict(batch=bsz, seq=seq)

    for l in range(depth):
        last = l == depth - 1
        i = l // 2
        n_tiles = (n_lat if last else r) // TM
        g_mix = norm_mix_g[l].reshape(1, d)
        if l % 2 == 0:
            lam_init = 0.8 - 0.6 * math.exp(-0.3 * l)
            cos, sin = _rope_tables(seq)
            q, k, vt, p = _proj_rope(*stream, g_mix, mods[l], ab_w_in[i].astype(bf16), cos, sin, **geom)
            mix_a = _diff_attention(q, k, vt, diff_lam[i], diff_subln_g[i], lam_init, ctx_len=cl, **geom)
            mix_b = _pool(p, pool_w[i], pool_scale[i], ctx_len=cl, **geom)
            w_out = ab_w_out[i]
        else:
            assert stream[0] is stream[1]
            v, x1, x2, u = _proj_conv(stream[0], g_mix, mods[l], cd_w_in[i].astype(bf16), hy_conv_w[i], hy_conv_b[i],
                                      **geom)
            mix_a = _hyena(v, x1, x2, hy_f_w1[i], hy_f_b1[i], hy_f_freq[i],
                           hy_f_w2[i], hy_f_b2[i], hy_f_w3[i], hy_bias[i], **geom)
            mix_b = _s5(u, s5_a_re[i], s5_a_im[i], s5_log_dt[i], s5_b_re[i], s5_b_im[i],
                        s5_c_re[i], s5_c_im[i], s5_d[i], glu_w[i], glu_b[i], ctx_len=cl, **geom)
            w_out = cd_w_out[i]
        w_r = jnp.zeros((d, 128), f32).at[:, :MOE_GROUPS].set(moe_w_rg[l]) \
            .at[:, MOE_GROUPS:MOE_GROUPS + MOE_EXPERTS].set(moe_w_re[l]).astype(bf16)
        b_r = jnp.zeros((1, 128), f32).at[0, :MOE_GROUPS].set(moe_b_rg[l]) \
            .at[0, MOE_GROUPS:MOE_GROUPS + MOE_EXPERTS].set(moe_b_re[l])
        xn, h2, logits = _outproj(mix_a, mix_b, *stream, w_out.astype(bf16), mods[l], norm_ffn_g[l].reshape(1, d),
                                  w_r, b_r, n_tiles=n_tiles, **geom)
        xs = _moe(xn, h2, logits, mods[l], moe_w1, moe_w3, moe_w2, l, norm_out_g.reshape(1, d),
                  n_tiles=n_tiles, final_norm=last, **geom)
        stream = (xs, xs, n_lat // TM)
    return xs[:n_lat].reshape(bsz, seq, d)
```

```python
import functools
import math

import jax
import jax.numpy as jnp
from jax import lax
from jax.experimental import pallas as pl
from jax.experimental.pallas import tpu as pltpu

f32, bf16, i32 = jnp.float32, jnp.bfloat16, jnp.int32

EPS = 1e-6
GRID_W = 64
DIFF_QK = 64
DIFF_V = 128
ROPE_BASE = 10000.0
POOL_WINDOWS = (2, 4, 8, 16)
POOL_GROUP = 256
HY_WIDTH = 1024
HY_ORDER = 2
HY_EMB = 33
HY_FAST_DECAY = 0.3
HY_SLOW_DECAY = 1.5
HY_TARGET = 1e-2
S5_GROUP = 16
MOE_GROUPS = 8
MOE_PER_GROUP = 8
MOE_EXPERTS = 64
MOE_TOPK = 2
MOE_BLOCK = 256

VMEM_LIMIT = 56 * 1024 * 1024
TM = 512
LOG2E = 1.4426950408889634


def _cparams(sem):
    return pltpu.CompilerParams(dimension_semantics=sem, vmem_limit_bytes=VMEM_LIMIT)


def _norm_mod(x, g, shift, scale):
    ms = jnp.mean(x * x, axis=-1, keepdims=True)
    return (x * lax.rsqrt(ms + EPS) * g) * (1.0 + scale) + shift


def _tile_segment(i, lat_tiles_per_batch, batch):
    seg = jnp.zeros((), i32)
    for b in range(1, batch + 1):
        seg = seg + (i >= b * lat_tiles_per_batch).astype(i32)
    return seg


def _pack_rows(x):
    half = x.shape[1] // 2
    bits = lax.bitcast_convert_type(x.astype(bf16).astype(f32), jnp.uint32)
    return (bits[:, :half] >> 16) | (bits[:, half:] & jnp.uint32(0xFFFF0000))


def _unpack_rows(words):
    return (lax.bitcast_convert_type(words << 16, f32),
            lax.bitcast_convert_type(words & jnp.uint32(0xFFFF0000), f32))


def _ada_kernel(s_ref, w_ref, b_ref, o_ref):
    s = s_ref[...]
    a = (s * jax.nn.sigmoid(s)).astype(bf16)
    o_ref[...] = jnp.dot(a, w_ref[...].astype(bf16), preferred_element_type=f32) + b_ref[...]


def _ada_mods(s, ada_w, ada_b):
    depth, d, n = ada_w.shape
    tn = 1024
    return pl.pallas_call(
        _ada_kernel,
        grid=(depth, n // tn),
        in_specs=[pl.BlockSpec((8, d), lambda l, j: (0, 0)),
                  pl.BlockSpec((None, d, tn), lambda l, j: (l, 0, j)),
                  pl.BlockSpec((None, 1, tn), lambda l, j: (l, 0, j))],
        out_specs=pl.BlockSpec((None, 8, tn), lambda l, j: (l, 0, j)),
        out_shape=jax.ShapeDtypeStruct((depth, 8, n), f32),
        compiler_params=_cparams(("arbitrary", "arbitrary")),
        name="ada_mods",
    )(s, ada_w, ada_b.reshape(depth, 1, n))


def _proj_rope_kernel(x_ref, c_ref, g_ref, sh_ref, sc_ref, w_ref, cos_ref, sin_ref,
                      q_ref, k_ref, v_ref, p_ref, h_scr, *, tiles_per_batch, batch, qscale):
    i = pl.program_id(0)
    seg = _tile_segment(i, tiles_per_batch, batch)
    xin = jnp.where(i < batch * tiles_per_batch, x_ref[...], c_ref[...])
    h = _norm_mod(xin, g_ref[...], sh_ref[pl.ds(seg, 1), :], sc_ref[pl.ds(seg, 1), :])
    h_scr[...] = h.astype(bf16)
    cos = cos_ref[...]
    sin = sin_ref[...]
    tm = cos.shape[0]
    lane = lax.broadcasted_iota(i32, (tm, 128), 1)
    first = (lane % 32) < 16

    def rope(r):
        sw = jnp.where(first, pltpu.roll(r, 112, 1), pltpu.roll(r, 16, 1))
        return r * cos + sw * sin

    heads = q_ref.shape[0]
    nc = 512
    for c in range(w_ref.shape[1] // nc):
        res = jnp.dot(h_scr[...], w_ref[:, c * nc:(c + 1) * nc], preferred_element_type=f32)
        for s in range(nc // 128):
            r = res[:, s * 128:(s + 1) * 128]
            col = (c * nc) // 128 + s
            if col < heads:
                q_ref[col] = (rope(r) * qscale).astype(bf16)
            elif col < 2 * heads:
                k_ref[col - heads] = rope(r).astype(bf16)
            elif col < 3 * heads:
                v_ref[col - 2 * heads] = r.T.astype(bf16)
            else:
                c0 = (col - 3 * heads) * 128
                p_ref[:, c0:c0 + 128] = r


def _stream_specs(d, n_lat_tiles, ctx_off):
    return [pl.BlockSpec((TM, d), lambda i: (jnp.minimum(i, n_lat_tiles - 1), 0)),
            pl.BlockSpec((TM, d), lambda i: (jnp.maximum(i - n_lat_tiles, 0) + ctx_off, 0))]


def _proj_rope(x, xc, ctx_off, g, mods, w, cos, sin, *, batch, seq):
    d = x.shape[1]
    heads = 8
    lat_tiles = seq // TM
    n_tiles = batch * lat_tiles + (xc.shape[0] // TM - ctx_off)
    r = n_tiles * TM

    def tab_map(i):
        return (jnp.where(i < batch * lat_tiles, i % lat_tiles, lat_tiles), 0)

    hm = jax.ShapeDtypeStruct((heads, r, 128), bf16)
    hm_spec = pl.BlockSpec((heads, TM, 128), lambda i: (0, i, 0))
    return pl.pallas_call(
        functools.partial(_proj_rope_kernel, tiles_per_batch=lat_tiles, batch=batch,
                          qscale=LOG2E * DIFF_QK ** -0.5),
        grid=(n_tiles,),
        in_specs=_stream_specs(d, batch * lat_tiles, ctx_off) + [
                  pl.BlockSpec((1, d), lambda i: (0, 0)),
                  pl.BlockSpec((8, d), lambda i: (0, 0)),
                  pl.BlockSpec((8, d), lambda i: (0, 1)),
                  pl.BlockSpec(w.shape, lambda i: (0, 0), pipeline_mode=pl.Buffered(1)),
                  pl.BlockSpec((TM, 128), tab_map),
                  pl.BlockSpec((TM, 128), tab_map)],
        out_specs=[hm_spec, hm_spec, pl.BlockSpec((heads, 128, TM), lambda i: (0, 0, i)),
                   pl.BlockSpec((TM, 1024), lambda i: (i, 0))],
        out_shape=[hm, hm, jax.ShapeDtypeStruct((heads, 128, r), bf16), jax.ShapeDtypeStruct((r, 1024), f32)],
        scratch_shapes=[pltpu.VMEM((TM, d), bf16)],
        compiler_params=_cparams(("arbitrary",)),
        name="proj_rope",
    )(x, xc, g, mods, mods, w, cos, sin)


def _proj_conv_kernel(x_ref, prev_ref, next_ref, g_ref, sh_ref, sc_ref, w_ref, cw_ref, cb_ref,
                      v_ref, x1_ref, x2_ref, u_ref, h_scr, halo_scr, ext_scr, *, tiles_per_batch, batch):
    i = pl.program_id(0)
    seg = _tile_segment(i, tiles_per_batch, batch)
    g, sh, sc = g_ref[...], sh_ref[pl.ds(seg, 1), :], sc_ref[pl.ds(seg, 1), :]
    h_scr[...] = _norm_mod(x_ref[...], g, sh, sc).astype(bf16)
    halo = jnp.concatenate([prev_ref[...], next_ref[...]], axis=0)
    halo_scr[...] = _norm_mod(halo, g, sh, sc).astype(bf16)
    j = i % tiles_per_batch
    is_lat = i < batch * tiles_per_batch
    has_prev = jnp.logical_and(is_lat, j > 0)
    has_next = jnp.logical_and(is_lat, j < tiles_per_batch - 1)
    tm = x_ref.shape[0]
    nc = ext_scr.shape[2]
    width = v_ref.shape[1]
    outs = (v_ref, x1_ref, x2_ref)
    for c in range(w_ref.shape[1] // nc):
        w_c = w_ref[:, c * nc:(c + 1) * nc]
        res = jnp.dot(h_scr[...], w_c, preferred_element_type=f32)
        part, off = divmod(c * nc, width)
        if part < len(outs):
            edge = jnp.dot(halo_scr[...], w_c, preferred_element_type=f32)
            ext = ext_scr.at[c % 2]
            ext[0:8, :] = jnp.where(has_prev, edge[0:8], 0.0)
            ext[8:8 + tm, :] = res
            ext[8 + tm:16 + tm, :] = jnp.where(has_next, edge[8:16], 0.0)
            cols = slice(c * nc, (c + 1) * nc)
            cw = cw_ref[:, cols]
            outs[part][:, off:off + nc] = (ext[7:7 + tm, :] * cw[0:1] + ext[8:8 + tm, :] * cw[1:2]
                                           + ext[9:9 + tm, :] * cw[2:3] + cb_ref[:, cols])
        else:
            u_ref[:, off:off + nc] = res


def _proj_conv(x, g, mods, w, conv_w, conv_b, *, batch, seq):
    r, d = x.shape
    n = w.shape[1]
    width = n // 4
    nc = 512
    row = pl.BlockSpec((TM, width), lambda i: (i, 0))
    out = jax.ShapeDtypeStruct((r, width), f32)
    return pl.pallas_call(
        functools.partial(_proj_conv_kernel, tiles_per_batch=seq // TM, batch=batch),
        grid=(r // TM,),
        in_specs=[pl.BlockSpec((TM, d), lambda i: (i, 0)),
                  pl.BlockSpec((8, d), lambda i: (jnp.maximum(i * (TM // 8) - 1, 0), 0)),
                  pl.BlockSpec((8, d), lambda i: (jnp.minimum((i + 1) * (TM // 8), r // 8 - 1), 0)),
                  pl.BlockSpec((1, d), lambda i: (0, 0)),
                  pl.BlockSpec((8, d), lambda i: (0, 0)),
                  pl.BlockSpec((8, d), lambda i: (0, 1)),
                  pl.BlockSpec(w.shape, lambda i: (0, 0), pipeline_mode=pl.Buffered(1)),
                  pl.BlockSpec(conv_w.shape, lambda i: (0, 0)),
                  pl.BlockSpec((1, conv_w.shape[1]), lambda i: (0, 0))],
        out_specs=[row, row, row, row],
        out_shape=[out, out, out, out],
        scratch_shapes=[pltpu.VMEM((TM, d), bf16), pltpu.VMEM((16, d), bf16), pltpu.VMEM((2, TM + 16, nc), f32)],
        compiler_params=_cparams(("arbitrary",)),
        name="proj_conv",
    )(x, x, x, g, mods, mods, w, conv_w, conv_b.reshape(1, -1))


ATTN_ONES = 16
ATTN_UNROLL = 8


def _attn_kernel(lv_ref, g_ref, q_ref, kc_ref, vc_ref, *rest, n_lat, tk, lam_init):
    if n_lat:
        kl_ref, vl_ref, o_ref, s_scr = rest
    else:
        o_ref, s_scr = rest
    lv = lv_ref[...]
    lam = (jnp.exp(jnp.sum(lv[0:1] * lv[1:2], axis=-1, keepdims=True))
           - jnp.exp(jnp.sum(lv[2:3] * lv[3:4], axis=-1, keepdims=True)) + lam_init)
    q = q_ref[...]
    tq = q.shape[0]
    lane = lax.broadcasted_iota(i32, q.shape, 1)
    zero = jnp.zeros_like(q)
    qs = (jnp.where(lane < DIFF_QK, q, zero), jnp.where(lane >= DIFF_QK, q, zero))
    nt = (((1,), (1,)), ((), ()))

    def scores(slot, k):
        for m in range(2):
            s_scr[slot, m, 0:k.shape[0], :] = lax.dot_general(k, qs[m], nt, preferred_element_type=f32)

    def update(carry, slot, nk, vt):
        vt_ext = jnp.concatenate([vt, jnp.ones((ATTN_ONES, vt.shape[1]), bf16)], axis=0)
        new = []
        for m in range(2):
            mx, acc = carry[2 * m:2 * m + 2]
            s = s_scr[slot, m, 0:nk, :]
            mn = jnp.maximum(mx, jnp.max(s, axis=0, keepdims=True))
            p = jnp.exp2(s - mn).astype(bf16)
            acc = jnp.exp2(mx - mn) * acc + jnp.dot(vt_ext, p, preferred_element_type=f32)
            new += [mn, acc]
        return tuple(new)

    def k_block(j):
        return kl_ref[pl.ds(pl.multiple_of(j * tk, tk), tk), :]

    def vt_block(j):
        return vl_ref[:, pl.ds(pl.multiple_of(j * tk, tk), tk)]

    init = (jnp.full((1, tq), -jnp.inf, f32), jnp.zeros((DIFF_V + ATTN_ONES, tq), f32)) * 2
    nc = kc_ref.shape[0]
    scores(1, kc_ref[...])
    if n_lat:
        scores(0, k_block(0))
        carry = update(init, 1, nc, vc_ref[...])

        def body(jj, carry):
            j = ATTN_UNROLL * jj
            for t in range(ATTN_UNROLL):
                nxt = j + t + 1 if t + 1 < ATTN_UNROLL else jnp.minimum(j + ATTN_UNROLL, n_lat - 1)
                scores((t + 1) % 2, k_block(nxt))
                carry = update(carry, t % 2, tk, vt_block(j + t))
            return carry

        carry = lax.fori_loop(0, n_lat // ATTN_UNROLL, body, carry)
    else:
        carry = update(init, 1, nc, vc_ref[...])
    _, a0, _, a1 = carry
    o = (a0[:DIFF_V] / a0[DIFF_V:DIFF_V + 1]
         - lam * (a1[:DIFF_V] / a1[DIFF_V:DIFF_V + 1]))
    ms = jnp.mean(o * o, axis=0, keepdims=True)
    y = (o * lax.rsqrt(ms + EPS) * g_ref[...]) * (1.0 - lam_init)
    o_ref[...] = y.T.astype(o_ref.dtype)


def _diff_attention(q, k, vt, lam_vecs, subln_g, lam_init, *, batch, seq, ctx_len):
    heads, r, _ = q.shape
    tq, tk = 512, 512
    ctx_blk0 = (batch * seq) // ctx_len
    assert (seq // tk) % ATTN_UNROLL == 0
    small = [pl.BlockSpec((4, DIFF_QK), lambda *_: (0, 0)), pl.BlockSpec((DIFF_V, 1), lambda *_: (0, 0))]
    kc_spec = pl.BlockSpec((None, ctx_len, 128), lambda b, h, *_: (h, ctx_blk0 + b, 0))
    vc_spec = pl.BlockSpec((None, 128, ctx_len), lambda b, h, *_: (h, 0, ctx_blk0 + b))
    g2 = subln_g.reshape(DIFF_V, 1)
    lat = pl.pallas_call(
        functools.partial(_attn_kernel, n_lat=seq // tk, tk=tk, lam_init=lam_init),
        grid=(batch, heads, seq // tq),
        in_specs=small + [pl.BlockSpec((None, tq, 128), lambda b, h, i: (h, b * (seq // tq) + i, 0)),
                          kc_spec, vc_spec,
                          pl.BlockSpec((None, seq, 128), lambda b, h, i: (h, b, 0)),
                          pl.BlockSpec((None, 128, seq), lambda b, h, i: (h, 0, b))],
        out_specs=pl.BlockSpec((tq, 128), lambda b, h, i: (b * (seq // tq) + i, h)),
        out_shape=jax.ShapeDtypeStruct((batch * seq, heads * DIFF_V), bf16),
        scratch_shapes=[pltpu.VMEM((2, 2, tk, tq), f32)],
        compiler_params=_cparams(("arbitrary",) * 3),
        name="diff_attn_lat",
    )(lam_vecs, g2, q, k, vt, k, vt)
    ctx = pl.pallas_call(
        functools.partial(_attn_kernel, n_lat=0, tk=tk, lam_init=lam_init),
        grid=(batch, heads),
        in_specs=small + [kc_spec, kc_spec, vc_spec],
        out_specs=pl.BlockSpec((ctx_len, 128), lambda b, h: (b, h)),
        out_shape=jax.ShapeDtypeStruct((batch * ctx_len, heads * DIFF_V), bf16),
        scratch_shapes=[pltpu.VMEM((2, 2, ctx_len, ctx_len), f32)],
        compiler_params=_cparams(("arbitrary",) * 2),
        name="diff_attn_ctx",
    )(lam_vecs, g2, q, k, vt)
    return jnp.concatenate([lat, ctx], axis=0)


def _pool_kernel(p_ref, prev_ref, next_ref, w_ref, scale_ref, o_ref, ext_scr, *,
                 tiles_per_batch, batch, seq, ctx_len):
    i = pl.program_id(0)
    tp = p_ref.shape[0]
    is_ctx = i >= batch * tiles_per_batch
    j = jnp.where(is_ctx, 0, i % tiles_per_batch)
    seg_len = jnp.where(is_ctx, ctx_len, seq)
    last = jnp.where(is_ctx, 0, tiles_per_batch - 1)
    ext_scr[0:8, :] = jnp.where(j > 0, prev_ref[...], 0.0)
    ext_scr[8:8 + tp, :] = p_ref[...]
    ext_scr[8 + tp:16 + tp, :] = jnp.where(j < last, next_ref[...], 0.0)
    pos = j * tp + lax.broadcasted_iota(i32, (tp, 1), 0)
    for gi, w in enumerate(POOL_WINDOWS):
        c0 = gi * POOL_GROUP
        half = w // 2
        tot = ext_scr[8 - half:8 - half + tp, c0:c0 + POOL_GROUP]
        for dlt in range(1 - half, half):
            tot = tot + ext_scr[8 + dlt:8 + dlt + tp, c0:c0 + POOL_GROUP]
        cnt = (jnp.minimum(pos - half + w, seg_len) - jnp.maximum(pos - half, 0)).astype(f32)
        dlt_mean = (tot / cnt - ext_scr[8:8 + tp, c0:c0 + POOL_GROUP]).astype(bf16)
        y = jnp.dot(dlt_mean, w_ref[gi].astype(bf16), preferred_element_type=f32)
        o_ref[:, c0:c0 + POOL_GROUP] = (y * scale_ref[:, c0:c0 + POOL_GROUP]).astype(o_ref.dtype)


def _pool(p, pool_w, pool_scale, *, batch, seq, ctx_len):
    r, width = p.shape
    tp = ctx_len
    tpb = seq // tp
    return pl.pallas_call(
        functools.partial(_pool_kernel, tiles_per_batch=tpb, batch=batch, seq=seq, ctx_len=ctx_len),
        grid=(r // tp,),
        in_specs=[pl.BlockSpec((tp, width), lambda i: (i, 0)),
                  pl.BlockSpec((8, width), lambda i: (jnp.maximum(i * (tp // 8) - 1, 0), 0)),
                  pl.BlockSpec((8, width), lambda i: (jnp.minimum((i + 1) * (tp // 8), r // 8 - 1), 0)),
                  pl.BlockSpec(pool_w.shape, lambda i: (0, 0, 0)),
                  pl.BlockSpec((1, width), lambda i: (0, 0))],
        out_specs=pl.BlockSpec((tp, width), lambda i: (i, 0)),
        out_shape=jax.ShapeDtypeStruct((r, width), bf16),
        scratch_shapes=[pltpu.VMEM((tp + 16, width), f32)],
        compiler_params=_cparams(("arbitrary",)),
        name="pool",
    )(p, p, p, pool_w, pool_scale.reshape(1, width))


def _outproj_kernel(a_ref, b_ref, x_ref, c_ref, w_ref, g1_ref, sh_ref, sc_ref, gn_ref, wr_ref, br_ref,
                    xn_ref, h2_ref, lg_ref, *, tiles_per_batch, batch):
    i = pl.program_id(0)
    seg = _tile_segment(i, tiles_per_batch, batch)
    xin = jnp.where(i < batch * tiles_per_batch, x_ref[...], c_ref[...])
    half = a_ref.shape[1]
    y = (jnp.dot(a_ref[...], w_ref[0:half, :], preferred_element_type=f32)
         + jnp.dot(b_ref[...], w_ref[half:, :], preferred_element_type=f32))
    xn = xin + g1_ref[pl.ds(seg, 1), :] * y
    xn_ref[...] = xn
    h2 = _norm_mod(xn, gn_ref[...], sh_ref[pl.ds(seg, 1), :], sc_ref[pl.ds(seg, 1), :])
    h2_ref[...] = _pack_rows(h2)
    lg_ref[...] = jnp.dot(h2.astype(bf16), wr_ref[...], preferred_element_type=f32) + br_ref[...]


def _outproj(a, b, x, xc, ctx_off, w, mods, g_ffn, w_r, b_r, *, n_tiles, batch, seq):
    r, d = n_tiles * TM, x.shape[1]
    half = a.shape[1]
    row = lambda i: (i, 0)
    return pl.pallas_call(
        functools.partial(_outproj_kernel, tiles_per_batch=seq // TM, batch=batch),
        grid=(n_tiles,),
        in_specs=[pl.BlockSpec((TM, half), row), pl.BlockSpec((TM, half), row)]
        + _stream_specs(d, batch * (seq // TM), ctx_off) + [
                  pl.BlockSpec(w.shape, lambda i: (0, 0), pipeline_mode=pl.Buffered(1)),
                  pl.BlockSpec((8, d), lambda i: (0, 2)),
                  pl.BlockSpec((8, d), lambda i: (0, 3)),
                  pl.BlockSpec((8, d), lambda i: (0, 4)),
                  pl.BlockSpec((1, d), lambda i: (0, 0)),
                  pl.BlockSpec(w_r.shape, lambda i: (0, 0)),
                  pl.BlockSpec((1, 128), lambda i: (0, 0))],
        out_specs=[pl.BlockSpec((TM, d), row), pl.BlockSpec((TM, d // 2), row), pl.BlockSpec((TM, 128), row)],
        out_shape=[jax.ShapeDtypeStruct((r, d), f32), jax.ShapeDtypeStruct((r, d // 2), jnp.uint32),
                   jax.ShapeDtypeStruct((r, 128), f32)],
        compiler_params=_cparams(("arbitrary",)),
        name="outproj",
    )(a, b, x, xc, w, mods, mods, mods, g_ffn, w_r, b_r)


def _moe_kernel(be_ref, nu_ref, nxt_ref, par_ref, xs_ref, w1_hbm, w3_hbm, w2_hbm, ys_ref,
                w1_f32, w3_f32, w2_f32, w1_scr, w3_scr, w2_scr, sems, *, layer):
    i = pl.program_id(0)
    e = be_ref[i]
    prev = be_ref[jnp.maximum(i - 1, 0)]
    used = i < nu_ref[0]

    def weight_copies(expert, slot):
        return (pltpu.make_async_copy(w1_hbm.at[layer, expert], w1_f32.at[slot], sems.at[slot, 0]),
                pltpu.make_async_copy(w3_hbm.at[layer, expert], w3_f32.at[slot], sems.at[slot, 1]),
                pltpu.make_async_copy(w2_hbm.at[layer, expert], w2_f32.at[slot], sems.at[slot, 2]))

    @pl.when(i == 0)
    def _():
        for cp in weight_copies(e, par_ref[e]):
            cp.start()

    @pl.when(used & ((i == 0) | (e != prev)))
    def _():
        slot = par_ref[e]
        nxt = nxt_ref[e]

        @pl.when(nxt >= 0)
        def _():
            for cp in weight_copies(nxt, 1 - slot):
                cp.start()

        for cp in weight_copies(e, slot):
            cp.wait()
        w1_scr[...] = w1_f32[slot].astype(bf16)
        w3_scr[...] = w3_f32[slot].astype(bf16)
        w2_scr[...] = w2_f32[slot].astype(bf16)

    @pl.when(used)
    def _():
        lo, hi = _unpack_rows(xs_ref[...])
        half = lo.shape[1]
        lo, hi = lo.astype(bf16), hi.astype(bf16)
        a = (jnp.dot(lo, w1_scr[0:half, :], preferred_element_type=f32)
             + jnp.dot(hi, w1_scr[half:, :], preferred_element_type=f32))
        b = (jnp.dot(lo, w3_scr[0:half, :], preferred_element_type=f32)
             + jnp.dot(hi, w3_scr[half:, :], preferred_element_type=f32))
        hid = (a * jax.nn.sigmoid(a) * b).astype(bf16)
        ys_ref[...] = _pack_rows(jnp.dot(hid, w2_scr[...], preferred_element_type=f32))

    @pl.when(jnp.logical_not(used))
    def _():
        ys_ref[...] = jnp.zeros_like(ys_ref)


def _moe_experts(xs, block_e, n_used, next_expert, parity, w1, w3, w2, layer, nb):
    m = MOE_BLOCK
    _, _, d, de = w1.shape
    hbm = pl.BlockSpec(memory_space=pl.ANY)
    grid_spec = pltpu.PrefetchScalarGridSpec(
        num_scalar_prefetch=4,
        grid=(nb,),
        in_specs=[pl.BlockSpec((m, d // 2), lambda i, be, nu, nx, pa: (jnp.minimum(i, nu[0] - 1), 0)), hbm, hbm, hbm],
        out_specs=pl.BlockSpec((m, d // 2), lambda i, be, nu, nx, pa: (i, 0)),
        scratch_shapes=[pltpu.VMEM((2, d, de), f32), pltpu.VMEM((2, d, de), f32), pltpu.VMEM((2, de, d), f32),
                        pltpu.VMEM((d, de), bf16), pltpu.VMEM((d, de), bf16), pltpu.VMEM((de, d), bf16),
                        pltpu.SemaphoreType.DMA((2, 3))],
    )
    return pl.pallas_call(
        functools.partial(_moe_kernel, layer=layer),
        grid_spec=grid_spec,
        out_shape=jax.ShapeDtypeStruct((nb * m, d // 2), jnp.uint32),
        compiler_params=_cparams(("arbitrary",)),
        name="moe_experts",
    )(block_e, n_used, next_expert, parity, xs, w1, w3, w2)


def _route_kernel(lg_ref, tri_ref, o_ref, cnt_ref, run_scr):
    @pl.when(pl.program_id(0) == 0)
    def _():
        run_scr[...] = jnp.zeros_like(run_scr)

    lg = lg_ref[...]
    lane = lax.broadcasted_iota(i32, lg.shape, 1)
    ninf = -jnp.inf

    def first_lane(mask):
        return jnp.min(jnp.where(mask, lane, lg.shape[1]), axis=-1, keepdims=True)

    gl = jnp.where(lane < MOE_GROUPS, lg, ninf)
    gmax = jnp.max(gl, axis=-1, keepdims=True)
    pg_top = 1.0 / jnp.sum(jnp.exp(gl - gmax), axis=-1, keepdims=True)
    lo = MOE_GROUPS + first_lane(gl == gmax) * MOE_PER_GROUP
    sel = (lane >= lo) & (lane < lo + MOE_PER_GROUP)
    el = jnp.where(sel, lg, ninf)
    ee = jnp.exp(el - jnp.max(el, axis=-1, keepdims=True))
    pe = jnp.where(sel, ee / jnp.sum(ee, axis=-1, keepdims=True), -1.0)
    p1 = jnp.max(pe, axis=-1, keepdims=True)
    i1 = first_lane(pe == p1)
    pe2 = jnp.where(lane == i1, -1.0, pe)
    p2 = jnp.max(pe2, axis=-1, keepdims=True)
    i2 = first_lane(pe2 == p2)
    den = p1 + p2
    e1 = i1 - MOE_GROUPS
    e2 = i2 - MOE_GROUPS
    hit1 = lane == e1
    hit2 = lane == e2
    onehot = jnp.where(hit1, 1.0, jnp.where(hit2, 1.0, 0.0))
    base = jnp.dot(tri_ref[...], onehot.astype(bf16), preferred_element_type=f32) + run_scr[...]
    r1 = jnp.sum(jnp.where(hit1, base, 0.0), axis=-1, keepdims=True)
    r2 = jnp.sum(jnp.where(hit2, base, 0.0), axis=-1, keepdims=True)
    run_scr[...] = run_scr[...] + jnp.sum(onehot, axis=0, keepdims=True)
    cnt_ref[...] = run_scr[...]
    vals = (e1.astype(f32), e2.astype(f32), pg_top * p1 / den, pg_top * p2 / den, r1, r2)
    out = jnp.zeros(lg.shape, f32)
    for j, v in enumerate(vals):
        out = jnp.where(lane == j, v, out)
    o_ref[...] = out


def _route(logits):
    t, width = logits.shape
    tri = jnp.tril(jnp.ones((TM, TM), f32), -1).astype(bf16)
    return pl.pallas_call(
        _route_kernel,
        grid=(t // TM,),
        in_specs=[pl.BlockSpec((TM, width), lambda i: (i, 0)), pl.BlockSpec((TM, TM), lambda i: (0, 0))],
        out_specs=[pl.BlockSpec((TM, width), lambda i: (i, 0)), pl.BlockSpec((1, width), lambda i: (0, 0))],
        out_shape=[jax.ShapeDtypeStruct((t, width), f32), jax.ShapeDtypeStruct((1, width), f32)],
        scratch_shapes=[pltpu.VMEM((1, width), f32)],
        compiler_params=_cparams(("arbitrary",)),
        name="route",
    )(logits, tri)


def _dispatch_kernel(dest_ref, h_ref, init_ref, xs_ref, sem):
    del init_ref
    i = pl.program_id(0)
    tm = h_ref.shape[0]

    def row_copy(r, k):
        slot = dest_ref[(i * tm + r) * MOE_TOPK + k]
        return pltpu.make_async_copy(h_ref.at[pl.ds(r, 1)], xs_ref.at[pl.ds(slot, 1)], sem)

    def issue(r, carry):
        for k in range(MOE_TOPK):
            row_copy(r, k).start(priority=k % 2)
        return carry

    lax.fori_loop(0, tm, issue, 0, unroll=8)
    for k in range(MOE_TOPK):
        pltpu.make_async_copy(h_ref, xs_ref.at[pl.ds(0, tm)], sem).wait()


def _dispatch(h2, dest, nb):
    t, d = h2.shape
    rows = nb * MOE_BLOCK
    grid_spec = pltpu.PrefetchScalarGridSpec(
        num_scalar_prefetch=1,
        grid=(t // TM,),
        in_specs=[pl.BlockSpec((TM, d), lambda i, dst: (i, 0)), pl.BlockSpec(memory_space=pl.ANY)],
        out_specs=pl.BlockSpec(memory_space=pl.ANY),
        scratch_shapes=[pltpu.SemaphoreType.DMA(())],
    )
    return pl.pallas_call(
        _dispatch_kernel,
        grid_spec=grid_spec,
        out_shape=jax.ShapeDtypeStruct((rows, d), h2.dtype),
        input_output_aliases={2: 0},
        compiler_params=_cparams(("arbitrary",)),
        name="moe_dispatch",
    )(dest, h2, jnp.zeros((rows, d), h2.dtype))


def _combine_kernel(dest_ref, x_ref, rt_ref, g2_ref, gn_ref, ys_ref, o_ref, y_scr, sem, *,
                    tiles_per_batch, batch, final_norm):
    i = pl.program_id(0)
    tm = x_ref.shape[0]

    def row_copy(r, k):
        slot = dest_ref[(i * tm + r) * MOE_TOPK + k]
        return pltpu.make_async_copy(ys_ref.at[pl.ds(slot, 1)], y_scr.at[k, pl.ds(r, 1)], sem)

    def issue(r, carry):
        for k in range(MOE_TOPK):
            row_copy(r, k).start(priority=k % 2)
        return carry

    lax.fori_loop(0, tm, issue, 0, unroll=8)
    for k in range(MOE_TOPK):
        pltpu.make_async_copy(ys_ref.at[pl.ds(0, tm)], y_scr.at[k], sem).wait()
    seg = _tile_segment(i, tiles_per_batch, batch)
    rt = rt_ref[...]
    y_lo, y_hi = 0.0, 0.0
    for k in range(MOE_TOPK):
        lo, hi = _unpack_rows(y_scr[k])
        gate = rt[:, MOE_TOPK + k:MOE_TOPK + k + 1]
        y_lo = y_lo + gate * lo
        y_hi = y_hi + gate * hi
    x = x_ref[...] + g2_ref[pl.ds(seg, 1), :] * jnp.concatenate([y_lo, y_hi], axis=1)
    if final_norm:
        ms = jnp.mean(x * x, axis=-1, keepdims=True)
        x = x * lax.rsqrt(ms + EPS) * gn_ref[...]
    o_ref[...] = x


def _combine(x, ys, dest, routed, mods, g_out, *, n_tiles, batch, seq, final_norm):
    d = x.shape[1]
    row = lambda i, dst: (i, 0)
    grid_spec = pltpu.PrefetchScalarGridSpec(
        num_scalar_prefetch=1,
        grid=(n_tiles,),
        in_specs=[pl.BlockSpec((TM, d), row),
                  pl.BlockSpec((TM, routed.shape[1]), row),
                  pl.BlockSpec((8, d), lambda i, dst: (0, 5)),
                  pl.BlockSpec((1, d), lambda i, dst: (0, 0)),
                  pl.BlockSpec(memory_space=pl.ANY)],
        out_specs=pl.BlockSpec((TM, d), row),
        scratch_shapes=[pltpu.VMEM((MOE_TOPK, TM, d // 2), jnp.uint32), pltpu.SemaphoreType.DMA(())],
    )
    return pl.pallas_call(
        functools.partial(_combine_kernel, tiles_per_batch=seq // TM, batch=batch, final_norm=final_norm),
        grid_spec=grid_spec,
        out_shape=jax.ShapeDtypeStruct((n_tiles * TM, d), f32),
        compiler_params=_cparams(("arbitrary",)),
        name="moe_combine",
    )(dest, x, routed, mods, g_out, ys)


def _moe(xn, h2, logits, mods, w1, w3, w2, layer, g_out, *, n_tiles, batch, seq, final_norm):
    t = n_tiles * TM
    m = MOE_BLOCK
    routed, cnt = _route(logits)
    s = t * MOE_TOPK
    nb = s // m + MOE_EXPERTS
    expert = routed[:, 0:MOE_TOPK].astype(i32)
    rank = routed[:, 2 * MOE_TOPK:3 * MOE_TOPK].astype(i32)
    counts = cnt[0, :MOE_EXPERTS].astype(i32)
    padded = ((counts + m - 1) // m) * m
    pad_end = jnp.cumsum(padded)
    onehot = expert[:, :, None] == jnp.arange(MOE_EXPERTS, dtype=i32)[None, None, :]
    dest = (jnp.sum(jnp.where(onehot, (pad_end - padded)[None, None, :], 0), axis=-1) + rank).reshape(s)
    n_used = (pad_end[-1] // m).astype(i32)
    blk = jnp.arange(nb, dtype=i32)
    block_e = jnp.minimum(jnp.searchsorted(pad_end, jnp.minimum(blk, n_used - 1) * m, side='right'),
                          MOE_EXPERTS - 1).astype(i32)
    ids = jnp.arange(MOE_EXPERTS, dtype=i32)
    nonempty_from = lax.cummin(jnp.where(counts > 0, ids, MOE_EXPERTS), reverse=True)
    nxt = jnp.concatenate([nonempty_from[1:], jnp.full((1,), MOE_EXPERTS, i32)])
    next_expert = jnp.where(nxt < MOE_EXPERTS, nxt, -1).astype(i32)
    parity = ((jnp.cumsum((counts > 0).astype(i32)) - 1) % 2).astype(i32)
    xs = _dispatch(h2, dest, nb)
    ys = _moe_experts(xs, block_e, n_used.reshape(1), next_expert, parity, w1, w3, w2, layer, nb)
    return _combine(xn, ys, dest, routed, mods, g_out, n_tiles=n_tiles, batch=batch, seq=seq,
                    final_norm=final_norm)


HY_N2 = 128
HY_WB = 256


def _hy_filter_kernel(z_ref, w1_ref, b1_ref, fr_ref, w2_ref, b2_ref, w3_ref, dl_ref, k_ref, nrm_ref):
    i = pl.program_id(0)
    z = z_ref[...]
    fr = fr_ref[...]
    h = jnp.sin(fr[0:1] * (jnp.dot(z.astype(bf16), w1_ref[...], preferred_element_type=f32) + b1_ref[...]))
    h = jnp.sin(fr[1:2] * (jnp.dot(h.astype(bf16), w2_ref[...], preferred_element_type=f32) + b2_ref[...]))
    kk = jnp.dot(h.astype(bf16), w3_ref[...], preferred_element_type=f32)
    width = dl_ref.shape[1]
    decay = jnp.exp(-z[:, 0:1] * jnp.abs(dl_ref[...]))
    tl = z.shape[0]
    row = i * tl + lax.broadcasted_iota(i32, (tl, 1), 0)
    tot = []
    for blk in range(kk.shape[1] // width):
        kb = kk[:, blk * width:(blk + 1) * width] * decay
        if blk % 2 == 1:
            kb = jnp.where(row == 0, 0.0, kb)
        k_ref[:, blk * width:(blk + 1) * width] = kb
        tot.append(jnp.sum(jnp.abs(kb), axis=0, keepdims=True))
    s = jnp.concatenate(tot, axis=1)

    @pl.when(i == 0)
    def _():
        nrm_ref[...] = s

    @pl.when(i > 0)
    def _():
        nrm_ref[...] = nrm_ref[...] + s


def _hy_filters(seq, w1, b1, freq, w2, b2, w3):
    t = jnp.linspace(0.0, 1.0, seq, dtype=f32)[:, None]
    bands = (HY_EMB - 1) // 2
    w = 2 * math.pi * jnp.arange(seq, dtype=f32)[:, None] / seq
    f = jnp.linspace(1e-4, bands - 1, bands, dtype=f32)[None, :]
    z = jnp.concatenate([t, jnp.cos(f * w), -jnp.sin(f * w), jnp.zeros((seq, 128 - HY_EMB), f32)], axis=-1)
    w1p = jnp.concatenate([w1, jnp.zeros((128 - HY_EMB, w1.shape[1]), f32)], axis=0).astype(bf16)
    hid = w1.shape[1]
    n_out = w3.shape[1]
    deltas = jnp.linspace(math.log(HY_TARGET) / HY_SLOW_DECAY, math.log(HY_TARGET) / HY_FAST_DECAY,
                          HY_WIDTH, dtype=f32).reshape(1, HY_WIDTH)
    tl = 512
    const = lambda a: pl.BlockSpec(a.shape, lambda i: (0,) * a.ndim)
    ins = [z, w1p, b1.reshape(1, hid), freq, w2.astype(bf16), b2.reshape(1, hid), w3.astype(bf16), deltas]
    return pl.pallas_call(
        _hy_filter_kernel,
        grid=(seq // tl,),
        in_specs=[pl.BlockSpec((tl, 128), lambda i: (i, 0))] + [const(a) for a in ins[1:]],
        out_specs=[pl.BlockSpec((tl, n_out), lambda i: (i, 0)), pl.BlockSpec((1, n_out), lambda i: (0, 0))],
        out_shape=[jax.ShapeDtypeStruct((seq, n_out), f32), jax.ShapeDtypeStruct((1, n_out), f32)],
        compiler_params=_cparams(("arbitrary",)),
        name="hy_filter",
    )(*ins)


def _dft_tables(seq):
    n2 = HY_N2
    n_tot = 2 * seq
    n1 = n_tot // n2
    nf = n1 // 2 + 1
    nfp = -(-nf // 8) * 8
    f1 = jnp.arange(nfp, dtype=i32)
    k1 = jnp.arange(n1 // 2, dtype=i32)
    th = (2 * math.pi / n1) * ((f1[:, None] * k1[None, :]) % n1).astype(f32)
    valid = (f1 < nf)[:, None]
    fwd = jnp.concatenate([jnp.where(valid, jnp.cos(th), 0.0), jnp.where(valid, -jnp.sin(th), 0.0)], axis=0)
    eye = jnp.eye(8, dtype=f32)
    fk = jnp.einsum('rk,jh->rjkh', fwd, eye).reshape(2 * nfp * 8, (n1 // 2) * 8)
    cw = jnp.where(valid, jnp.where((f1 == 0) | (f1 == n1 // 2), 1.0, 2.0)[:, None], 0.0) / n_tot
    inv = jnp.concatenate([(cw * jnp.cos(th)).T, (-cw * jnp.sin(th)).T], axis=1)
    ck = jnp.einsum('kr,jh->kjrh', inv, eye).reshape((n1 // 2) * 8, 2 * nfp * 8)
    fa = jnp.arange(nf, dtype=i32)[:, None, None]
    f2 = jnp.arange(n2, dtype=i32)[None, :, None]
    m2 = jnp.arange(n2, dtype=i32)[None, None, :]
    ph = (2 * math.pi / n_tot) * ((f2 * m2 * n1 + m2 * fa) % n_tot).astype(f32)
    gr, gi = jnp.cos(ph), -jnp.sin(ph)
    g = jnp.concatenate([jnp.concatenate([gr, -gi], axis=2), jnp.concatenate([gi, gr], axis=2)], axis=1)
    hr, hi = jnp.swapaxes(jnp.cos(ph), 1, 2), jnp.swapaxes(jnp.sin(ph), 1, 2)
    h = jnp.concatenate([jnp.concatenate([hr, -hi], axis=2), jnp.concatenate([hi, hr], axis=2)], axis=1)
    return fk.astype(bf16), ck.astype(bf16), g.astype(bf16), h.astype(bf16), nf, nfp


def _hy_rows_per_step(nf, most=5):
    return max(d for d in range(most, 0, -1) if nf % d == 0)


def _dft_stage1(x_ref, fk_ref, a_scr):
    n1h, n2, wb = x_ref.shape

    def slab(s, carry):
        rows = pl.ds(pl.multiple_of(s * 8, 8), 8)
        xs = x_ref[:, rows, :].reshape(n1h * 8, wb).astype(bf16)
        a = jnp.dot(fk_ref[...], xs, preferred_element_type=f32)
        a_scr[:, rows, :] = a.reshape(a_scr.shape[0], 8, wb)
        return carry

    lax.fori_loop(0, n2 // 8, slab, 0)


def _dft_stage2(a_scr, g, f1, nfp):
    a = jnp.concatenate([a_scr[f1], a_scr[nfp + f1]], axis=0).astype(bf16)
    return jnp.dot(g, a, preferred_element_type=f32)


def _hy_spectrum_kernel(k_ref, fk_ref, g_ref, o_ref, a_scr, *, nfp):
    fs = pl.program_id(2)

    @pl.when(fs == 0)
    def _():
        _dft_stage1(k_ref, fk_ref, a_scr)

    fb = g_ref.shape[0]
    n2 = k_ref.shape[1]
    for j in range(fb):
        s = _dft_stage2(a_scr, g_ref[j], fs * fb + j, nfp)
        o_ref[0, j] = s[:n2]
        o_ref[1, j] = s[n2:]


def _hy_spectrum(k, fk, g, nf, nfp, seq):
    n2, wb = HY_N2, HY_WB
    n1h = seq // n2
    fb = _hy_rows_per_step(nf, 13)
    nwb = HY_WIDTH // wb
    n_blk = k.shape[1] // HY_WIDTH
    k3 = k.reshape(n1h, n2, k.shape[1])
    return pl.pallas_call(
        functools.partial(_hy_spectrum_kernel, nfp=nfp),
        grid=(n_blk, nwb, nf // fb),
        in_specs=[pl.BlockSpec((n1h, n2, wb), lambda o, w, f: (0, 0, o * nwb + w), pipeline_mode=pl.Buffered(1)),
                  pl.BlockSpec(fk.shape, lambda o, w, f: (0, 0)),
                  pl.BlockSpec((fb, 2 * n2, 2 * n2), lambda o, w, f: (f, 0, 0))],
        out_specs=pl.BlockSpec((None, 2, fb, n2, wb), lambda o, w, f: (o, 0, f, 0, w)),
        out_shape=jax.ShapeDtypeStruct((n_blk, 2, nf, n2, HY_WIDTH), f32),
        scratch_shapes=[pltpu.VMEM((2 * nfp, n2, wb), f32)],
        compiler_params=_cparams(("arbitrary",) * 3),
        name="hy_spectrum",
    )(k3, fk, g)


def _hy_conv_kernel(x_ref, gate_ref, kf_ref, kb_ref, g_ref, h_ref, fk_ref, ck_ref, bias_ref, inv_ref, o_ref, a_scr, *,
                    nfp):
    fs = pl.program_id(2)
    n1h, n2, wb = x_ref.shape

    @pl.when(fs == 0)
    def _():
        _dft_stage1(x_ref, fk_ref, a_scr)

    fb = g_ref.shape[0]
    for j in range(fb):
        f1 = fs * fb + j
        xs = _dft_stage2(a_scr, g_ref[j], f1, nfp)
        xr, xi = xs[:n2], xs[n2:]
        kr = kf_ref[0, j] + kb_ref[0, j]
        ki = kf_ref[1, j] - kb_ref[1, j]
        y = jnp.concatenate([xr * kr - xi * ki, xr * ki + xi * kr], axis=0).astype(bf16)
        bb = jnp.dot(h_ref[j], y, preferred_element_type=f32)
        a_scr[f1] = bb[:n2]
        a_scr[nfp + f1] = bb[n2:]

    @pl.when(fs == pl.num_programs(2) - 1)
    def _():
        bias = bias_ref[...].reshape(1, 1, wb)
        inv = inv_ref[...].reshape(1, 1, wb)

        def slab(s, carry):
            rows = pl.ds(pl.multiple_of(s * 8, 8), 8)
            bs = a_scr[:, rows, :].reshape(2 * nfp * 8, wb).astype(bf16)
            y = jnp.dot(ck_ref[...], bs, preferred_element_type=f32).reshape(n1h, 8, wb)
            o_ref[:, rows, :] = (gate_ref[:, rows, :] * (y * inv + x_ref[:, rows, :] * bias)).astype(o_ref.dtype)
            return carry

        lax.fori_loop(0, n2 // 8, slab, 0)


def _hy_conv(x, gate, kf, order, tabs, bias, inv_norm, out_dtype, *, batch, seq):
    fk, ck, g, h, nf, nfp = tabs
    n2, wb = HY_N2, HY_WB
    n1h = seq // n2
    fb = _hy_rows_per_step(nf)
    width = x.shape[1]
    big = pl.BlockSpec((n1h, n2, wb), lambda b, w, f: (b, 0, w), pipeline_mode=pl.Buffered(1))
    out = pl.pallas_call(
        functools.partial(_hy_conv_kernel, nfp=nfp),
        grid=(batch, width // wb, nf // fb),
        in_specs=[big, big,
                  pl.BlockSpec((None, 2, fb, n2, wb), lambda b, w, f: (2 * order, 0, f, 0, w)),
                  pl.BlockSpec((None, 2, fb, n2, wb), lambda b, w, f: (2 * order + 1, 0, f, 0, w)),
                  pl.BlockSpec((fb, 2 * n2, 2 * n2), lambda b, w, f: (f, 0, 0)),
                  pl.BlockSpec((fb, 2 * n2, 2 * n2), lambda b, w, f: (f, 0, 0)),
                  pl.BlockSpec(fk.shape, lambda b, w, f: (0, 0)),
                  pl.BlockSpec(ck.shape, lambda b, w, f: (0, 0)),
                  pl.BlockSpec((1, wb), lambda b, w, f: (0, w)),
                  pl.BlockSpec((1, wb), lambda b, w, f: (0, w))],
        out_specs=big,
        out_shape=jax.ShapeDtypeStruct((batch * n1h, n2, width), out_dtype),
        scratch_shapes=[pltpu.VMEM((2 * nfp, n2, wb), f32)],
        compiler_params=_cparams(("arbitrary",) * 3),
        name="hy_conv",
    )(x.reshape(-1, n2, width), gate.reshape(-1, n2, width), kf, kf, g, h, fk, ck,
      bias.reshape(1, width), inv_norm.reshape(1, width))
    return out.reshape(batch * seq, width)


def _hyena(v, x1, x2, f_w1, f_b1, f_freq, f_w2, f_b2, f_w3, hy_bias, *, batch, seq):
    k, sums = _hy_filters(seq, f_w1, f_b1, f_freq, f_w2, f_b2, f_w3)
    sums = sums.reshape(HY_ORDER, 2, HY_WIDTH)
    inv_norm = 1.0 / (sums[:, 0] + sums[:, 1])
    tabs = _dft_tables(seq)
    kf = _hy_spectrum(k, tabs[0], tabs[2], tabs[4], tabs[5], seq)
    z = _hy_conv(v, x1, kf, 0, tabs, hy_bias[0], inv_norm[0], f32, batch=batch, seq=seq)
    return _hy_conv(z, x2, kf, 1, tabs, hy_bias[1], inv_norm[1], bf16, batch=batch, seq=seq)


S5_TT = 256
S5_GB = 8


def _gelu_tanh(x):
    return 0.5 * x * (1.0 + jnp.tanh(0.7978845608028654 * (x + 0.044715 * x * x * x)))


def _s5_kernel(u_ref, perm_ref, permt_ref, wb_ref, wc_ref, are_ref, aim_ref, pre_ref, pim_ref, *rest, fuse_glu):
    if fuse_glu:
        yf_ref, d_ref, gw_ref, gb_ref, o_ref, st_scr, bu_scr, y_scr = rest
    else:
        o_ref, st_scr, bu_scr, y_scr = rest
    tt = u_ref.shape[0]
    steps = tt // 8
    half = bu_scr.shape[2] // 2

    @pl.when(pl.program_id(1) == 0)
    def _():
        st_scr[...] = jnp.zeros_like(st_scr)

    up = jnp.dot(perm_ref[...], u_ref[...].astype(bf16), preferred_element_type=f32).astype(bf16)
    lock = 8
    zero = jnp.zeros((8, half), f32)
    for g0 in range(0, wb_ref.shape[0], lock):
        gbs = range(g0, g0 + lock)
        for gb in gbs:
            bu_scr[gb] = jnp.dot(up[:, gb * 128:(gb + 1) * 128], wb_ref[gb], preferred_element_type=f32)
        lam = {gb: (jnp.broadcast_to(are_ref[gb], (8, half)), jnp.broadcast_to(aim_ref[gb], (8, half)))
               for gb in gbs}

        def scan_step(gb, k, h):
            hr, hi = h
            a_re, a_im = lam[gb]
            rows = pl.ds(k * 8, 8)
            nr = a_re * hr - a_im * hi + bu_scr[gb, rows, 0:half]
            ni = a_re * hi + a_im * hr + bu_scr[gb, rows, half:]
            bu_scr[gb, rows, 0:half] = nr
            bu_scr[gb, rows, half:] = ni
            return nr, ni

        h = {gb: (zero, zero) for gb in gbs}
        for k in range(steps):
            for gb in gbs:
                h[gb] = scan_step(gb, k, h[gb])

        carry = {}
        for gb in gbs:
            hr, hi = h[gb]
            pw_r = pre_ref[gb, steps - 1:steps, :]
            pw_i = pim_ref[gb, steps - 1:steps, :]
            cur_r = st_scr[gb, :, 0:half]
            cur_i = st_scr[gb, :, half:]
            car_r, car_i = [], []
            for j in range(8):
                car_r.append(cur_r)
                car_i.append(cur_i)
                cur_r, cur_i = (hr[j:j + 1] + pw_r * cur_r - pw_i * cur_i,
                                hi[j:j + 1] + pw_r * cur_i + pw_i * cur_r)
            st_scr[gb, :, 0:half] = cur_r
            st_scr[gb, :, half:] = cur_i
            carry[gb] = (jnp.concatenate(car_r, axis=0), jnp.concatenate(car_i, axis=0))

        for k in range(steps):
            for gb in gbs:
                cr, ci = carry[gb]
                rows = pl.ds(k * 8, 8)
                pr = pre_ref[gb, pl.ds(k, 1), :]
                pi = pim_ref[gb, pl.ds(k, 1), :]
                bu_scr[gb, rows, 0:half] = bu_scr[gb, rows, 0:half] + (pr * cr - pi * ci)
                bu_scr[gb, rows, half:] = bu_scr[gb, rows, half:] + (pr * ci + pi * cr)
        for gb in gbs:
            y_scr[:, gb * 128:(gb + 1) * 128] = jnp.dot(bu_scr[gb].astype(bf16), wc_ref[gb],
                                                          preferred_element_type=f32)
    yp = y_scr[...]
    y_hi = yp.astype(bf16)
    y_lo = (yp - y_hi.astype(f32)).astype(bf16)
    y = (jnp.dot(permt_ref[...], y_hi, preferred_element_type=f32)
         + jnp.dot(permt_ref[...], y_lo, preferred_element_type=f32))
    if fuse_glu:
        g = _gelu_tanh(y + yf_ref[...] + u_ref[...] * d_ref[...])
        z = jnp.dot(g.astype(bf16), gw_ref[...], preferred_element_type=f32) + gb_ref[...]
        o_ref[...] = (g * jax.nn.sigmoid(z)).astype(o_ref.dtype)
    else:
        o_ref[...] = y


def _s5_tables(a_re, a_im, log_dt, b_re, b_im, c_re, c_im, steps):
    ng, ns = a_re.shape
    nb = ng // S5_GB
    lam = lax.complex(a_re, a_im)
    dt = jnp.exp(log_dt)[:, None]
    lam_bar = jnp.exp(lam * dt)
    b_bar = ((lam_bar - 1) / lam)[..., None] * lax.complex(b_re, b_im)
    cm = lax.complex(c_re, c_im)
    pw = jnp.exp((lam * dt)[None] * jnp.arange(1, steps + 1, dtype=f32)[:, None, None])
    eye = jnp.eye(S5_GB, dtype=f32)
    bb = b_bar.reshape(nb, S5_GB, ns, S5_GROUP)
    wb = jnp.concatenate([jnp.einsum('bgpc,gh->bgchp', part, eye).reshape(nb, S5_GB * S5_GROUP, S5_GB * ns)
                          for part in (jnp.real(bb), jnp.imag(bb))], axis=-1)
    cc = cm.reshape(nb, S5_GB, S5_GROUP, ns)
    wc = jnp.concatenate([jnp.einsum('bgcp,gh->bgphc', part, eye).reshape(nb, S5_GB * ns, S5_GB * S5_GROUP)
                          for part in (jnp.real(cc), -jnp.imag(cc))], axis=1)
    lb = lam_bar.reshape(nb, 1, S5_GB * ns)
    pwb = pw.reshape(steps, nb, S5_GB * ns).transpose(1, 0, 2)
    return (wb.astype(bf16), wc.astype(bf16), jnp.real(lb), jnp.imag(lb), jnp.real(pwb), jnp.imag(pwb))


def _s5_perm(tt, reverse):
    steps = tt // 8
    dst = jnp.arange(tt)
    t = (dst % 8) * steps + dst // 8
    src = tt - 1 - t if reverse else t
    return jnp.arange(tt)[None, :] == src[:, None]


def _s5(u, a_re, a_im, log_dt, b_re, b_im, c_re, c_im, s5_d, glu_w, glu_b, *, batch, seq, ctx_len):
    r, width = u.shape
    tt = S5_TT
    assert ctx_len == tt and seq % tt == 0
    n_lat = seq // tt
    ctx_blk0 = (batch * seq) // tt
    y = None
    for di in range(2):
        rev = di == 1
        wb, wc, lr, li, pr, pi = _s5_tables(a_re[di], a_im[di], log_dt[di], b_re[di], b_im[di],
                                            c_re[di], c_im[di], tt // 8)
        perm = _s5_perm(tt, rev)

        def tile_map(b, i, rev=rev):
            lat = b * n_lat + (n_lat - i if rev else i - 1)
            return (jnp.where(i == 0, ctx_blk0 + b, lat), 0)

        row_spec = pl.BlockSpec((tt, width), tile_map)
        const = lambda a: pl.BlockSpec(a.shape, lambda b, i: (0,) * a.ndim)
        ins = [u, perm.astype(bf16), perm.T.astype(bf16), wb, wc, lr, li, pr, pi]
        in_specs = [row_spec] + [const(a) for a in ins[1:]]
        if rev:
            extra = [y, s5_d.reshape(1, width), glu_w.astype(bf16), glu_b.reshape(1, width)]
            in_specs += [row_spec] + [const(a) for a in extra[1:]]
            ins += extra
        y = pl.pallas_call(
            functools.partial(_s5_kernel, fuse_glu=rev),
            grid=(batch, n_lat + 1),
            in_specs=in_specs,
            out_specs=row_spec,
            out_shape=jax.ShapeDtypeStruct((r, width), bf16 if rev else f32),
            scratch_shapes=[pltpu.VMEM((wb.shape[0], 1, wb.shape[2]), f32),
                            pltpu.VMEM((wb.shape[0], tt, wb.shape[2]), f32),
                            pltpu.VMEM((tt, width), f32)],
            compiler_params=_cparams(("arbitrary", "arbitrary")),
            name="s5_bwd_glu" if rev else "s5_fwd",
        )(*ins)
    return y


def _rope_tables(seq):
    rows = seq // GRID_W
    row = jnp.repeat(jnp.arange(rows, dtype=f32), GRID_W)
    col = jnp.tile(jnp.arange(GRID_W, dtype=f32), rows)
    n_freq = DIFF_QK // 4
    inv = ROPE_BASE ** (-jnp.arange(n_freq, dtype=f32) / n_freq)
    ar, ac = row[:, None] * inv, col[:, None] * inv
    cos = jnp.concatenate([jnp.cos(ar), jnp.cos(ar), jnp.cos(ac), jnp.cos(ac)], axis=-1)
    sin = jnp.concatenate([-jnp.sin(ar), jnp.sin(ar), -jnp.sin(ac), jnp.sin(ac)], axis=-1)
    cos = jnp.concatenate([jnp.tile(cos, (1, 2)), jnp.ones((TM, 128), f32)], axis=0)
    sin = jnp.concatenate([jnp.tile(sin, (1, 2)), jnp.zeros((TM, 128), f32)], axis=0)
    return cos, sin


def kernel(x, c, ctx, c_ctx, ada_w, ada_b, norm_mix_g, norm_ffn_g, norm_out_g, ab_w_in, ab_w_out, diff_lam, diff_subln_g, pool_w, pool_scale, cd_w_in, cd_w_out, hy_conv_w, hy_conv_b, hy_f_w1, hy_f_b1, hy_f_freq, hy_f_w2, hy_f_b2, hy_f_w3, hy_bias, s5_a_re, s5_a_im, s5_log_dt, s5_b_re, s5_b_im, s5_c_re, s5_c_im, s5_d, glu_w, glu_b, moe_w_rg, moe_b_rg, moe_w_re, moe_b_re, moe_w1, moe_w3, moe_w2):
    bsz, seq, d = x.shape
    cl = ctx.shape[1]
    depth = ada_w.shape[0]
    n_lat = bsz * seq
    assert seq % TM == 0 and (bsz * cl) % TM == 0 and bsz + 1 <= 8
    stream = (x.reshape(n_lat, d), ctx.reshape(bsz * cl, d), 0)
    r = n_lat + bsz * cl
    cond = jnp.zeros((8, d), f32).at[0:bsz].set(c).at[bsz].set(c_ctx)
    mods = _ada_mods(cond, ada_w, ada_b)
    geom = dict(batch=bsz, seq=seq)

    for l in range(depth):
        last = l == depth - 1
        i = l // 2
        n_tiles = (n_lat if last else r) // TM
        g_mix = norm_mix_g[l].reshape(1, d)
        if l % 2 == 0:
            lam_init = 0.8 - 0.6 * math.exp(-0.3 * l)
            cos, sin = _rope_tables(seq)
            q, k, vt, p = _proj_rope(*stream, g_mix, mods[l], ab_w_in[i].astype(bf16), cos, sin, **geom)
            mix_a = _diff_attention(q, k, vt, diff_lam[i], diff_subln_g[i], lam_init, ctx_len=cl, **geom)
            mix_b = _pool(p, pool_w[i], pool_scale[i], ctx_len=cl, **geom)
            w_out = ab_w_out[i]
        else:
            assert stream[0] is stream[1]
            v, x1, x2, u = _proj_conv(stream[0], g_mix, mods[l], cd_w_in[i].astype(bf16), hy_conv_w[i], hy_conv_b[i],
                                      **geom)
            mix_a = _hyena(v, x1, x2, hy_f_w1[i], hy_f_b1[i], hy_f_freq[i],
                           hy_f_w2[i], hy_f_b2[i], hy_f_w3[i], hy_bias[i], **geom)
            mix_b = _s5(u, s5_a_re[i], s5_a_im[i], s5_log_dt[i], s5_b_re[i], s5_b_im[i],
                        s5_c_re[i], s5_c_im[i], s5_d[i], glu_w[i], glu_b[i], ctx_len=cl, **geom)
            w_out = cd_w_out[i]
        w_r = jnp.zeros((d, 128), f32).at[:, :MOE_GROUPS].set(moe_w_rg[l]) \
            .at[:, MOE_GROUPS:MOE_GROUPS + MOE_EXPERTS].set(moe_w_re[l]).astype(bf16)
        b_r = jnp.zeros((1, 128), f32).at[0, :MOE_GROUPS].set(moe_b_rg[l]) \
            .at[0, MOE_GROUPS:MOE_GROUPS + MOE_EXPERTS].set(moe_b_re[l])
        xn, h2, logits = _outproj(mix_a, mix_b, *stream, w_out.astype(bf16), mods[l], norm_ffn_g[l].reshape(1, d),
                                  w_r, b_r, n_tiles=n_tiles, **geom)
        xs = _moe(xn, h2, logits, mods[l], moe_w1, moe_w3, moe_w2, l, norm_out_g.reshape(1, d),
                  n_tiles=n_tiles, final_norm=last, **geom)
        stream = (xs, xs, n_lat // TM)
    return xs[:n_lat].reshape(bsz, seq, d)
```

```python
import functools
import math

import jax
import jax.numpy as jnp
from jax import lax
from jax.experimental import pallas as pl
from jax.experimental.pallas import tpu as pltpu

f32, bf16, i32 = jnp.float32, jnp.bfloat16, jnp.int32

EPS = 1e-6
GRID_W = 64
DIFF_QK = 64
DIFF_V = 128
ROPE_BASE = 10000.0
POOL_WINDOWS = (2, 4, 8, 16)
POOL_GROUP = 256
HY_WIDTH = 1024
HY_ORDER = 2
HY_EMB = 33
HY_FAST_DECAY = 0.3
HY_SLOW_DECAY = 1.5
HY_TARGET = 1e-2
S5_GROUP = 16
MOE_GROUPS = 8
MOE_PER_GROUP = 8
MOE_EXPERTS = 64
MOE_TOPK = 2
MOE_BLOCK = 256

VMEM_LIMIT = 56 * 1024 * 1024
TM = 512
LOG2E = 1.4426950408889634


def _cparams(sem):
    return pltpu.CompilerParams(dimension_semantics=sem, vmem_limit_bytes=VMEM_LIMIT)


def _norm_mod(x, g, shift, scale):
    ms = jnp.mean(x * x, axis=-1, keepdims=True)
    return (x * lax.rsqrt(ms + EPS) * g) * (1.0 + scale) + shift


def _tile_segment(i, lat_tiles_per_batch, batch):
    seg = jnp.zeros((), i32)
    for b in range(1, batch + 1):
        seg = seg + (i >= b * lat_tiles_per_batch).astype(i32)
    return seg


def _pack_rows(x):
    half = x.shape[1] // 2
    bits = lax.bitcast_convert_type(x.astype(bf16).astype(f32), jnp.uint32)
    return (bits[:, :half] >> 16) | (bits[:, half:] & jnp.uint32(0xFFFF0000))


def _unpack_rows(words):
    return (lax.bitcast_convert_type(words << 16, f32),
            lax.bitcast_convert_type(words & jnp.uint32(0xFFFF0000), f32))


def _ada_kernel(s_ref, w_ref, b_ref, o_ref):
    s = s_ref[...]
    a = (s * jax.nn.sigmoid(s)).astype(bf16)
    o_ref[...] = jnp.dot(a, w_ref[...].astype(bf16), preferred_element_type=f32) + b_ref[...]


def _ada_mods(s, ada_w, ada_b):
    depth, d, n = ada_w.shape
    tn = 1024
    return pl.pallas_call(
        _ada_kernel,
        grid=(depth, n // tn),
        in_specs=[pl.BlockSpec((8, d), lambda l, j: (0, 0)),
                  pl.BlockSpec((None, d, tn), lambda l, j: (l, 0, j)),
                  pl.BlockSpec((None, 1, tn), lambda l, j: (l, 0, j))],
        out_specs=pl.BlockSpec((None, 8, tn), lambda l, j: (l, 0, j)),
        out_shape=jax.ShapeDtypeStruct((depth, 8, n), f32),
        compiler_params=_cparams(("arbitrary", "arbitrary")),
        name="ada_mods",
    )(s, ada_w, ada_b.reshape(depth, 1, n))


def _proj_rope_kernel(x_ref, c_ref, g_ref, sh_ref, sc_ref, w_ref, cos_ref, sin_ref,
                      q_ref, k_ref, v_ref, p_ref, h_scr, *, tiles_per_batch, batch, qscale):
    i = pl.program_id(0)
    seg = _tile_segment(i, tiles_per_batch, batch)
    xin = jnp.where(i < batch * tiles_per_batch, x_ref[...], c_ref[...])
    h = _norm_mod(xin, g_ref[...], sh_ref[pl.ds(seg, 1), :], sc_ref[pl.ds(seg, 1), :])
    h_scr[...] = h.astype(bf16)
    cos = cos_ref[...]
    sin = sin_ref[...]
    tm = cos.shape[0]
    lane = lax.broadcasted_iota(i32, (tm, 128), 1)
    first = (lane % 32) < 16

    def rope(r):
        sw = jnp.where(first, pltpu.roll(r, 112, 1), pltpu.roll(r, 16, 1))
        return r * cos + sw * sin

    heads = q_ref.shape[0]
    nc = 512
    for c in range(w_ref.shape[1] // nc):
        res = jnp.dot(h_scr[...], w_ref[:, c * nc:(c + 1) * nc], preferred_element_type=f32)
        for s in range(nc // 128):
            r = res[:, s * 128:(s + 1) * 128]
            col = (c * nc) // 128 + s
            if col < heads:
                q_ref[col] = (rope(r) * qscale).astype(bf16)
            elif col < 2 * heads:
                k_ref[col - heads] = rope(r).astype(bf16)
            elif col < 3 * heads:
                v_ref[col - 2 * heads] = r.T.astype(bf16)
            else:
                c0 = (col - 3 * heads) * 128
                p_ref[:, c0:c0 + 128] = r


def _stream_specs(d, n_lat_tiles, ctx_off):
    return [pl.BlockSpec((TM, d), lambda i: (jnp.minimum(i, n_lat_tiles - 1), 0)),
            pl.BlockSpec((TM, d), lambda i: (jnp.maximum(i - n_lat_tiles, 0) + ctx_off, 0))]


def _proj_rope(x, xc, ctx_off, g, mods, w, cos, sin, *, batch, seq):
    d = x.shape[1]
    heads = 8
    lat_tiles = seq // TM
    n_tiles = batch * lat_tiles + (xc.shape[0] // TM - ctx_off)
    r = n_tiles * TM

    def tab_map(i):
        return (jnp.where(i < batch * lat_tiles, i % lat_tiles, lat_tiles), 0)

    hm = jax.ShapeDtypeStruct((heads, r, 128), bf16)
    hm_spec = pl.BlockSpec((heads, TM, 128), lambda i: (0, i, 0))
    return pl.pallas_call(
        functools.partial(_proj_rope_kernel, tiles_per_batch=lat_tiles, batch=batch,
                          qscale=LOG2E * DIFF_QK ** -0.5),
        grid=(n_tiles,),
        in_specs=_stream_specs(d, batch * lat_tiles, ctx_off) + [
                  pl.BlockSpec((1, d), lambda i: (0, 0)),
                  pl.BlockSpec((8, d), lambda i: (0, 0)),
                  pl.BlockSpec((8, d), lambda i: (0, 1)),
                  pl.BlockSpec(w.shape, lambda i: (0, 0), pipeline_mode=pl.Buffered(1)),
                  pl.BlockSpec((TM, 128), tab_map),
                  pl.BlockSpec((TM, 128), tab_map)],
        out_specs=[hm_spec, hm_spec, pl.BlockSpec((heads, 128, TM), lambda i: (0, 0, i)),
                   pl.BlockSpec((TM, 1024), lambda i: (i, 0))],
        out_shape=[hm, hm, jax.ShapeDtypeStruct((heads, 128, r), bf16), jax.ShapeDtypeStruct((r, 1024), f32)],
        scratch_shapes=[pltpu.VMEM((TM, d), bf16)],
        compiler_params=_cparams(("arbitrary",)),
        name="proj_rope",
    )(x, xc, g, mods, mods, w, cos, sin)


def _proj_conv_kernel(x_ref, prev_ref, next_ref, g_ref, sh_ref, sc_ref, w_ref, cw_ref, cb_ref,
                      v_ref, x1_ref, x2_ref, u_ref, h_scr, halo_scr, ext_scr, *, tiles_per_batch, batch):
    i = pl.program_id(0)
    seg = _tile_segment(i, tiles_per_batch, batch)
    g, sh, sc = g_ref[...], sh_ref[pl.ds(seg, 1), :], sc_ref[pl.ds(seg, 1), :]
    h_scr[...] = _norm_mod(x_ref[...], g, sh, sc).astype(bf16)
    halo = jnp.concatenate([prev_ref[...], next_ref[...]], axis=0)
    halo_scr[...] = _norm_mod(halo, g, sh, sc).astype(bf16)
    j = i % tiles_per_batch
    is_lat = i < batch * tiles_per_batch
    has_prev = jnp.logical_and(is_lat, j > 0)
    has_next = jnp.logical_and(is_lat, j < tiles_per_batch - 1)
    tm = x_ref.shape[0]
    nc = ext_scr.shape[2]
    width = v_ref.shape[1]
    outs = (v_ref, x1_ref, x2_ref)
    for c in range(w_ref.shape[1] // nc):
        w_c = w_ref[:, c * nc:(c + 1) * nc]
        res = jnp.dot(h_scr[...], w_c, preferred_element_type=f32)
        part, off = divmod(c * nc, width)
        if part < len(outs):
            edge = jnp.dot(halo_scr[...], w_c, preferred_element_type=f32)
            ext = ext_scr.at[c % 2]
            ext[0:8, :] = jnp.where(has_prev, edge[0:8], 0.0)
            ext[8:8 + tm, :] = res
            ext[8 + tm:16 + tm, :] = jnp.where(has_next, edge[8:16], 0.0)
            cols = slice(c * nc, (c + 1) * nc)
            cw = cw_ref[:, cols]
            outs[part][:, off:off + nc] = (ext[7:7 + tm, :] * cw[0:1] + ext[8:8 + tm, :] * cw[1:2]
                                           + ext[9:9 + tm, :] * cw[2:3] + cb_ref[:, cols])
        else:
            u_ref[:, off:off + nc] = res


def _proj_conv(x, g, mods, w, conv_w, conv_b, *, batch, seq):
    r, d = x.shape
    n = w.shape[1]
    width = n // 4
    nc = 512
    row = pl.BlockSpec((TM, width), lambda i: (i, 0))
    out = jax.ShapeDtypeStruct((r, width), f32)
    return pl.pallas_call(
        functools.partial(_proj_conv_kernel, tiles_per_batch=seq // TM, batch=batch),
        grid=(r // TM,),
        in_specs=[pl.BlockSpec((TM, d), lambda i: (i, 0)),
                  pl.BlockSpec((8, d), lambda i: (jnp.maximum(i * (TM // 8) - 1, 0), 0)),
                  pl.BlockSpec((8, d), lambda i: (jnp.minimum((i + 1) * (TM // 8), r // 8 - 1), 0)),
                  pl.BlockSpec((1, d), lambda i: (0, 0)),
                  pl.BlockSpec((8, d), lambda i: (0, 0)),
                  pl.BlockSpec((8, d), lambda i: (0, 1)),
                  pl.BlockSpec(w.shape, lambda i: (0, 0), pipeline_mode=pl.Buffered(1)),
                  pl.BlockSpec(conv_w.shape, lambda i: (0, 0)),
                  pl.BlockSpec((1, conv_w.shape[1]), lambda i: (0, 0))],
        out_specs=[row, row, row, row],
        out_shape=[out, out, out, out],
        scratch_shapes=[pltpu.VMEM((TM, d), bf16), pltpu.VMEM((16, d), bf16), pltpu.VMEM((2, TM + 16, nc), f32)],
        compiler_params=_cparams(("arbitrary",)),
        name="proj_conv",
    )(x, x, x, g, mods, mods, w, conv_w, conv_b.reshape(1, -1))


ATTN_ONES = 16
ATTN_UNROLL = 8


def _attn_kernel(lv_ref, g_ref, q_ref, kc_ref, vc_ref, *rest, n_lat, tk, lam_init):
    if n_lat:
        kl_ref, vl_ref, o_ref, s_scr = rest
    else:
        o_ref, s_scr = rest
    lv = lv_ref[...]
    lam = (jnp.exp(jnp.sum(lv[0:1] * lv[1:2], axis=-1, keepdims=True))
           - jnp.exp(jnp.sum(lv[2:3] * lv[3:4], axis=-1, keepdims=True)) + lam_init)
    q = q_ref[...]
    tq = q.shape[0]
    lane = lax.broadcasted_iota(i32, q.shape, 1)
    zero = jnp.zeros_like(q)
    qs = (jnp.where(lane < DIFF_QK, q, zero), jnp.where(lane >= DIFF_QK, q, zero))
    nt = (((1,), (1,)), ((), ()))

    def scores(slot, k):
        for m in range(2):
            s_scr[slot, m, 0:k.shape[0], :] = lax.dot_general(k, qs[m], nt, preferred_element_type=f32)

    def update(carry, slot, nk, vt):
        vt_ext = jnp.concatenate([vt, jnp.ones((ATTN_ONES, vt.shape[1]), bf16)], axis=0)
        new = []
        for m in range(2):
            mx, acc = carry[2 * m:2 * m + 2]
            s = s_scr[slot, m, 0:nk, :]
            mn = jnp.maximum(mx, jnp.max(s, axis=0, keepdims=True))
            p = jnp.exp2(s - mn).astype(bf16)
            acc = jnp.exp2(mx - mn) * acc + jnp.dot(vt_ext, p, preferred_element_type=f32)
            new += [mn, acc]
        return tuple(new)

    def k_block(j):
        return kl_ref[pl.ds(pl.multiple_of(j * tk, tk), tk), :]

    def vt_block(j):
        return vl_ref[:, pl.ds(pl.multiple_of(j * tk, tk), tk)]

    init = (jnp.full((1, tq), -jnp.inf, f32), jnp.zeros((DIFF_V + ATTN_ONES, tq), f32)) * 2
    nc = kc_ref.shape[0]
    scores(1, kc_ref[...])
    if n_lat:
        scores(0, k_block(0))
        carry = update(init, 1, nc, vc_ref[...])

        def body(jj, carry):
            j = ATTN_UNROLL * jj
            for t in range(ATTN_UNROLL):
                nxt = j + t + 1 if t + 1 < ATTN_UNROLL else jnp.minimum(j + ATTN_UNROLL, n_lat - 1)
                scores((t + 1) % 2, k_block(nxt))
                carry = update(carry, t % 2, tk, vt_block(j + t))
            return carry

        carry = lax.fori_loop(0, n_lat // ATTN_UNROLL, body, carry)
    else:
        carry = update(init, 1, nc, vc_ref[...])
    _, a0, _, a1 = carry
    o = (a0[:DIFF_V] / a0[DIFF_V:DIFF_V + 1]
         - lam * (a1[:DIFF_V] / a1[DIFF_V:DIFF_V + 1]))
    ms = jnp.mean(o * o, axis=0, keepdims=True)
    y = (o * lax.rsqrt(ms + EPS) * g_ref[...]) * (1.0 - lam_init)
    o_ref[...] = y.T.astype(o_ref.dtype)


def _diff_attention(q, k, vt, lam_vecs, subln_g, lam_init, *, batch, seq, ctx_len):
    heads, r, _ = q.shape
    tq, tk = 512, 512
    ctx_blk0 = (batch * seq) // ctx_len
    assert (seq // tk) % ATTN_UNROLL == 0
    small = [pl.BlockSpec((4, DIFF_QK), lambda *_: (0, 0)), pl.BlockSpec((DIFF_V, 1), lambda *_: (0, 0))]
    kc_spec = pl.BlockSpec((None, ctx_len, 128), lambda b, h, *_: (h, ctx_blk0 + b, 0))
    vc_spec = pl.BlockSpec((None, 128, ctx_len), lambda b, h, *_: (h, 0, ctx_blk0 + b))
    g2 = subln_g.reshape(DIFF_V, 1)
    lat = pl.pallas_call(
        functools.partial(_attn_kernel, n_lat=seq // tk, tk=tk, lam_init=lam_init),
        grid=(batch, heads, seq // tq),
        in_specs=small + [pl.BlockSpec((None, tq, 128), lambda b, h, i: (h, b * (seq // tq) + i, 0)),
                          kc_spec, vc_spec,
                          pl.BlockSpec((None, seq, 128), lambda b, h, i: (h, b, 0)),
                          pl.BlockSpec((None, 128, seq), lambda b, h, i: (h, 0, b))],
        out_specs=pl.BlockSpec((tq, 128), lambda b, h, i: (b * (seq // tq) + i, h)),
        out_shape=jax.ShapeDtypeStruct((batch * seq, heads * DIFF_V), bf16),
        scratch_shapes=[pltpu.VMEM((2, 2, tk, tq), f32)],
        compiler_params=_cparams(("arbitrary",) * 3),
        name="diff_attn_lat",
    )(lam_vecs, g2, q, k, vt, k, vt)
    ctx = pl.pallas_call(
        functools.partial(_attn_kernel, n_lat=0, tk=tk, lam_init=lam_init),
        grid=(batch, heads),
        in_specs=small + [kc_spec, kc_spec, vc_spec],
        out_specs=pl.BlockSpec((ctx_len, 128), lambda b, h: (b, h)),
        out_shape=jax.ShapeDtypeStruct((batch * ctx_len, heads * DIFF_V), bf16),
        scratch_shapes=[pltpu.VMEM((2, 2, ctx_len, ctx_len), f32)],
        compiler_params=_cparams(("arbitrary",) * 2),
        name="diff_attn_ctx",
    )(lam_vecs, g2, q, k, vt)
    return jnp.concatenate([lat, ctx], axis=0)


def _pool_kernel(p_ref, prev_ref, next_ref, w_ref, scale_ref, o_ref, ext_scr, *,
                 tiles_per_batch, batch, seq, ctx_len):
    i = pl.program_id(0)
    tp = p_ref.shape[0]
    is_ctx = i >= batch * tiles_per_batch
    j = jnp.where(is_ctx, 0, i % tiles_per_batch)
    seg_len = jnp.where(is_ctx, ctx_len, seq)
    last = jnp.where(is_ctx, 0, tiles_per_batch - 1)
    ext_scr[0:8, :] = jnp.where(j > 0, prev_ref[...], 0.0)
    ext_scr[8:8 + tp, :] = p_ref[...]
    ext_scr[8 + tp:16 + tp, :] = jnp.where(j < last, next_ref[...], 0.0)
    pos = j * tp + lax.broadcasted_iota(i32, (tp, 1), 0)
    for gi, w in enumerate(POOL_WINDOWS):
        c0 = gi * POOL_GROUP
        half = w // 2
        tot = ext_scr[8 - half:8 - half + tp, c0:c0 + POOL_GROUP]
        for dlt in range(1 - half, half):
            tot = tot + ext_scr[8 + dlt:8 + dlt + tp, c0:c0 + POOL_GROUP]
        cnt = (jnp.minimum(pos - half + w, seg_len) - jnp.maximum(pos - half, 0)).astype(f32)
        dlt_mean = (tot / cnt - ext_scr[8:8 + tp, c0:c0 + POOL_GROUP]).astype(bf16)
        y = jnp.dot(dlt_mean, w_ref[gi].astype(bf16), preferred_element_type=f32)
        o_ref[:, c0:c0 + POOL_GROUP] = (y * scale_ref[:, c0:c0 + POOL_GROUP]).astype(o_ref.dtype)


def _pool(p, pool_w, pool_scale, *, batch, seq, ctx_len):
    r, width = p.shape
    tp = ctx_len
    tpb = seq // tp
    return pl.pallas_call(
        functools.partial(_pool_kernel, tiles_per_batch=tpb, batch=batch, seq=seq, ctx_len=ctx_len),
        grid=(r // tp,),
        in_specs=[pl.BlockSpec((tp, width), lambda i: (i, 0)),
                  pl.BlockSpec((8, width), lambda i: (jnp.maximum(i * (tp // 8) - 1, 0), 0)),
                  pl.BlockSpec((8, width), lambda i: (jnp.minimum((i + 1) * (tp // 8), r // 8 - 1), 0)),
                  pl.BlockSpec(pool_w.shape, lambda i: (0, 0, 0)),
                  pl.BlockSpec((1, width), lambda i: (0, 0))],
        out_specs=pl.BlockSpec((tp, width), lambda i: (i, 0)),
        out_shape=jax.ShapeDtypeStruct((r, width), bf16),
        scratch_shapes=[pltpu.VMEM((tp + 16, width), f32)],
        compiler_params=_cparams(("arbitrary",)),
        name="pool",
    )(p, p, p, pool_w, pool_scale.reshape(1, width))


def _outproj_kernel(a_ref, b_ref, x_ref, c_ref, w_ref, g1_ref, sh_ref, sc_ref, gn_ref, wr_ref, br_ref,
                    xn_ref, h2_ref, lg_ref, *, tiles_per_batch, batch):
    i = pl.program_id(0)
    seg = _tile_segment(i, tiles_per_batch, batch)
    xin = jnp.where(i < batch * tiles_per_batch, x_ref[...], c_ref[...])
    half = a_ref.shape[1]
    y = (jnp.dot(a_ref[...], w_ref[0:half, :], preferred_element_type=f32)
         + jnp.dot(b_ref[...], w_ref[half:, :], preferred_element_type=f32))
    xn = xin + g1_ref[pl.ds(seg, 1), :] * y
    xn_ref[...] = xn
    h2 = _norm_mod(xn, gn_ref[...], sh_ref[pl.ds(seg, 1), :], sc_ref[pl.ds(seg, 1), :])
    h2_ref[...] = _pack_rows(h2)
    lg_ref[...] = jnp.dot(h2.astype(bf16), wr_ref[...], preferred_element_type=f32) + br_ref[...]


def _outproj(a, b, x, xc, ctx_off, w, mods, g_ffn, w_r, b_r, *, n_tiles, batch, seq):
    r, d = n_tiles * TM, x.shape[1]
    half = a.shape[1]
    row = lambda i: (i, 0)
    return pl.pallas_call(
        functools.partial(_outproj_kernel, tiles_per_batch=seq // TM, batch=batch),
        grid=(n_tiles,),
        in_specs=[pl.BlockSpec((TM, half), row), pl.BlockSpec((TM, half), row)]
        + _stream_specs(d, batch * (seq // TM), ctx_off) + [
                  pl.BlockSpec(w.shape, lambda i: (0, 0), pipeline_mode=pl.Buffered(1)),
                  pl.BlockSpec((8, d), lambda i: (0, 2)),
                  pl.BlockSpec((8, d), lambda i: (0, 3)),
                  pl.BlockSpec((8, d), lambda i: (0, 4)),
                  pl.BlockSpec((1, d), lambda i: (0, 0)),
                  pl.BlockSpec(w_r.shape, lambda i: (0, 0)),
                  pl.BlockSpec((1, 128), lambda i: (0, 0))],
        out_specs=[pl.BlockSpec((TM, d), row), pl.BlockSpec((TM, d // 2), row), pl.BlockSpec((TM, 128), row)],
        out_shape=[jax.ShapeDtypeStruct((r, d), f32), jax.ShapeDtypeStruct((r, d // 2), jnp.uint32),
                   jax.ShapeDtypeStruct((r, 128), f32)],
        compiler_params=_cparams(("arbitrary",)),
        name="outproj",
    )(a, b, x, xc, w, mods, mods, mods, g_ffn, w_r, b_r)


def _moe_kernel(be_ref, nu_ref, nxt_ref, par_ref, xs_ref, w1_hbm, w3_hbm, w2_hbm, ys_ref,
                w1_f32, w3_f32, w2_f32, w1_scr, w3_scr, w2_scr, sems, *, layer):
    i = pl.program_id(0)
    e = be_ref[i]
    prev = be_ref[jnp.maximum(i - 1, 0)]
    used = i < nu_ref[0]

    def weight_copies(expert, slot):
        return (pltpu.make_async_copy(w1_hbm.at[layer, expert], w1_f32.at[slot], sems.at[slot, 0]),
                pltpu.make_async_copy(w3_hbm.at[layer, expert], w3_f32.at[slot], sems.at[slot, 1]),
                pltpu.make_async_copy(w2_hbm.at[layer, expert], w2_f32.at[slot], sems.at[slot, 2]))

    @pl.when(i == 0)
    def _():
        for cp in weight_copies(e, par_ref[e]):
            cp.start()

    @pl.when(used & ((i == 0) | (e != prev)))
    def _():
        slot = par_ref[e]
        nxt = nxt_ref[e]

        @pl.when(nxt >= 0)
        def _():
            for cp in weight_copies(nxt, 1 - slot):
                cp.start()

        for cp in weight_copies(e, slot):
            cp.wait()
        w1_scr[...] = w1_f32[slot].astype(bf16)
        w3_scr[...] = w3_f32[slot].astype(bf16)
        w2_scr[...] = w2_f32[slot].astype(bf16)

    @pl.when(used)
    def _():
        lo, hi = _unpack_rows(xs_ref[...])
        half = lo.shape[1]
        lo, hi = lo.astype(bf16), hi.astype(bf16)
        a = (jnp.dot(lo, w1_scr[0:half, :], preferred_element_type=f32)
             + jnp.dot(hi, w1_scr[half:, :], preferred_element_type=f32))
        b = (jnp.dot(lo, w3_scr[0:half, :], preferred_element_type=f32)
             + jnp.dot(hi, w3_scr[half:, :], preferred_element_type=f32))
        hid = (a * jax.nn.sigmoid(a) * b).astype(bf16)
        ys_ref[...] = _pack_rows(jnp.dot(hid, w2_scr[...], preferred_element_type=f32))

    @pl.when(jnp.logical_not(used))
    def _():
        ys_ref[...] = jnp.zeros_like(ys_ref)


def _moe_experts(xs, block_e, n_used, next_expert, parity, w1, w3, w2, layer, nb):
    m = MOE_BLOCK
    _, _, d, de = w1.shape
    hbm = pl.BlockSpec(memory_space=pl.ANY)
    grid_spec = pltpu.PrefetchScalarGridSpec(
        num_scalar_prefetch=4,
        grid=(nb,),
        in_specs=[pl.BlockSpec((m, d // 2), lambda i, be, nu, nx, pa: (jnp.minimum(i, nu[0] - 1), 0)), hbm, hbm, hbm],
        out_specs=pl.BlockSpec((m, d // 2), lambda i, be, nu, nx, pa: (i, 0)),
        scratch_shapes=[pltpu.VMEM((2, d, de), f32), pltpu.VMEM((2, d, de), f32), pltpu.VMEM((2, de, d), f32),
                        pltpu.VMEM((d, de), bf16), pltpu.VMEM((d, de), bf16), pltpu.VMEM((de, d), bf16),
                        pltpu.SemaphoreType.DMA((2, 3))],
    )
    return pl.pallas_call(
        functools.partial(_moe_kernel, layer=layer),
        grid_spec=grid_spec,
        out_shape=jax.ShapeDtypeStruct((nb * m, d // 2), jnp.uint32),
        compiler_params=_cparams(("arbitrary",)),
        name="moe_experts",
    )(block_e, n_used, next_expert, parity, xs, w1, w3, w2)


def _route_kernel(lg_ref, tri_ref, o_ref, cnt_ref, run_scr):
    @pl.when(pl.program_id(0) == 0)
    def _():
        run_scr[...] = jnp.zeros_like(run_scr)

    lg = lg_ref[...]
    lane = lax.broadcasted_iota(i32, lg.shape, 1)
    ninf = -jnp.inf

    def first_lane(mask):
        return jnp.min(jnp.where(mask, lane, lg.shape[1]), axis=-1, keepdims=True)

    gl = jnp.where(lane < MOE_GROUPS, lg, ninf)
    gmax = jnp.max(gl, axis=-1, keepdims=True)
    pg_top = 1.0 / jnp.sum(jnp.exp(gl - gmax), axis=-1, keepdims=True)
    lo = MOE_GROUPS + first_lane(gl == gmax) * MOE_PER_GROUP
    sel = (lane >= lo) & (lane < lo + MOE_PER_GROUP)
    el = jnp.where(sel, lg, ninf)
    ee = jnp.exp(el - jnp.max(el, axis=-1, keepdims=True))
    pe = jnp.where(sel, ee / jnp.sum(ee, axis=-1, keepdims=True), -1.0)
    p1 = jnp.max(pe, axis=-1, keepdims=True)
    i1 = first_lane(pe == p1)
    pe2 = jnp.where(lane == i1, -1.0, pe)
    p2 = jnp.max(pe2, axis=-1, keepdims=True)
    i2 = first_lane(pe2 == p2)
    den = p1 + p2
    e1 = i1 - MOE_GROUPS
    e2 = i2 - MOE_GROUPS
    hit1 = lane == e1
    hit2 = lane == e2
    onehot = jnp.where(hit1, 1.0, jnp.where(hit2, 1.0, 0.0))
    base = jnp.dot(tri_ref[...], onehot.astype(bf16), preferred_element_type=f32) + run_scr[...]
    r1 = jnp.sum(jnp.where(hit1, base, 0.0), axis=-1, keepdims=True)
    r2 = jnp.sum(jnp.where(hit2, base, 0.0), axis=-1, keepdims=True)
    run_scr[...] = run_scr[...] + jnp.sum(onehot, axis=0, keepdims=True)
    cnt_ref[...] = run_scr[...]
    vals = (e1.astype(f32), e2.astype(f32), pg_top * p1 / den, pg_top * p2 / den, r1, r2)
    out = jnp.zeros(lg.shape, f32)
    for j, v in enumerate(vals):
        out = jnp.where(lane == j, v, out)
    o_ref[...] = out


def _route(logits):
    t, width = logits.shape
    tri = jnp.tril(jnp.ones((TM, TM), f32), -1).astype(bf16)
    return pl.pallas_call(
        _route_kernel,
        grid=(t // TM,),
        in_specs=[pl.BlockSpec((TM, width), lambda i: (i, 0)), pl.BlockSpec((TM, TM), lambda i: (0, 0))],
        out_specs=[pl.BlockSpec((TM, width), lambda i: (i, 0)), pl.BlockSpec((1, width), lambda i: (0, 0))],
        out_shape=[jax.ShapeDtypeStruct((t, width), f32), jax.ShapeDtypeStruct((1, width), f32)],
        scratch_shapes=[pltpu.VMEM((1, width), f32)],
        compiler_params=_cparams(("arbitrary",)),
        name="route",
    )(logits, tri)


def _dispatch_kernel(dest_ref, h_ref, init_ref, xs_ref, sem):
    del init_ref
    i = pl.program_id(0)
    tm = h_ref.shape[0]

    def row_copy(r, k):
        slot = dest_ref[(i * tm + r) * MOE_TOPK + k]
        return pltpu.make_async_copy(h_ref.at[pl.ds(r, 1)], xs_ref.at[pl.ds(slot, 1)], sem)

    def issue(r, carry):
        for k in range(MOE_TOPK):
            row_copy(r, k).start(priority=k % 2)
        return carry

    lax.fori_loop(0, tm, issue, 0, unroll=8)
    for k in range(MOE_TOPK):
        pltpu.make_async_copy(h_ref, xs_ref.at[pl.ds(0, tm)], sem).wait()


def _dispatch(h2, dest, nb):
    t, d = h2.shape
    rows = nb * MOE_BLOCK
    grid_spec = pltpu.PrefetchScalarGridSpec(
        num_scalar_prefetch=1,
        grid=(t // TM,),
        in_specs=[pl.BlockSpec((TM, d), lambda i, dst: (i, 0)), pl.BlockSpec(memory_space=pl.ANY)],
        out_specs=pl.BlockSpec(memory_space=pl.ANY),
        scratch_shapes=[pltpu.SemaphoreType.DMA(())],
    )
    return pl.pallas_call(
        _dispatch_kernel,
        grid_spec=grid_spec,
        out_shape=jax.ShapeDtypeStruct((rows, d), h2.dtype),
        input_output_aliases={2: 0},
        compiler_params=_cparams(("arbitrary",)),
        name="moe_dispatch",
    )(dest, h2, jnp.zeros((rows, d), h2.dtype))


def _combine_kernel(dest_ref, x_ref, rt_ref, g2_ref, gn_ref, ys_ref, o_ref, y_scr, sem, *,
                    tiles_per_batch, batch, final_norm):
    i = pl.program_id(0)
    tm = x_ref.shape[0]

    def issue_tile(t, buf):
        def issue(r, carry):
            for k in range(MOE_TOPK):
                slot = dest_ref[(t * tm + r) * MOE_TOPK + k]
                pltpu.make_async_copy(ys_ref.at[pl.ds(slot, 1)], y_scr.at[buf, k, pl.ds(r, 1)],
                                      sem.at[buf]).start(priority=k % 2)
            return carry
        lax.fori_loop(0, tm, issue, 0, unroll=8)

    @pl.when(i == 0)
    def _():
        issue_tile(0, 0)

    @pl.when(i + 1 < pl.num_programs(0))
    def _():
        issue_tile(i + 1, (i + 1) % 2)

    buf = i % 2
    for k in range(MOE_TOPK):
        pltpu.make_async_copy(ys_ref.at[pl.ds(0, tm)], y_scr.at[buf, k], sem.at[buf]).wait()
    seg = _tile_segment(i, tiles_per_batch, batch)
    rt = rt_ref[...]
    y_lo, y_hi = 0.0, 0.0
    for k in range(MOE_TOPK):
        lo, hi = _unpack_rows(y_scr[buf, k])
        gate = rt[:, MOE_TOPK + k:MOE_TOPK + k + 1]
        y_lo = y_lo + gate * lo
        y_hi = y_hi + gate * hi
    x = x_ref[...] + g2_ref[pl.ds(seg, 1), :] * jnp.concatenate([y_lo, y_hi], axis=1)
    if final_norm:
        ms = jnp.mean(x * x, axis=-1, keepdims=True)
        x = x * lax.rsqrt(ms + EPS) * gn_ref[...]
    o_ref[...] = x


def _combine(x, ys, dest, routed, mods, g_out, *, n_tiles, batch, seq, final_norm):
    d = x.shape[1]
    row = lambda i, dst: (i, 0)
    grid_spec = pltpu.PrefetchScalarGridSpec(
        num_scalar_prefetch=1,
        grid=(n_tiles,),
        in_specs=[pl.BlockSpec((TM, d), row),
                  pl.BlockSpec((TM, routed.shape[1]), row),
                  pl.BlockSpec((8, d), lambda i, dst: (0, 5)),
                  pl.BlockSpec((1, d), lambda i, dst: (0, 0)),
                  pl.BlockSpec(memory_space=pl.ANY)],
        out_specs=pl.BlockSpec((TM, d), row),
        scratch_shapes=[pltpu.VMEM((2, MOE_TOPK, TM, d // 2), jnp.uint32), pltpu.SemaphoreType.DMA((2,))],
    )
    return pl.pallas_call(
        functools.partial(_combine_kernel, tiles_per_batch=seq // TM, batch=batch, final_norm=final_norm),
        grid_spec=grid_spec,
        out_shape=jax.ShapeDtypeStruct((n_tiles * TM, d), f32),
        compiler_params=_cparams(("arbitrary",)),
        name="moe_combine",
    )(dest, x, routed, mods, g_out, ys)


def _moe(xn, h2, logits, mods, w1, w3, w2, layer, g_out, *, n_tiles, batch, seq, final_norm):
    t = n_tiles * TM
    m = MOE_BLOCK
    routed, cnt = _route(logits)
    s = t * MOE_TOPK
    nb = s // m + MOE_EXPERTS
    expert = routed[:, 0:MOE_TOPK].astype(i32)
    rank = routed[:, 2 * MOE_TOPK:3 * MOE_TOPK].astype(i32)
    counts = cnt[0, :MOE_EXPERTS].astype(i32)
    padded = ((counts + m - 1) // m) * m
    pad_end = jnp.cumsum(padded)
    onehot = expert[:, :, None] == jnp.arange(MOE_EXPERTS, dtype=i32)[None, None, :]
    dest = (jnp.sum(jnp.where(onehot, (pad_end - padded)[None, None, :], 0), axis=-1) + rank).reshape(s)
    n_used = (pad_end[-1] // m).astype(i32)
    blk = jnp.arange(nb, dtype=i32)
    block_e = jnp.minimum(jnp.searchsorted(pad_end, jnp.minimum(blk, n_used - 1) * m, side='right'),
                          MOE_EXPERTS - 1).astype(i32)
    ids = jnp.arange(MOE_EXPERTS, dtype=i32)
    nonempty_from = lax.cummin(jnp.where(counts > 0, ids, MOE_EXPERTS), reverse=True)
    nxt = jnp.concatenate([nonempty_from[1:], jnp.full((1,), MOE_EXPERTS, i32)])
    next_expert = jnp.where(nxt < MOE_EXPERTS, nxt, -1).astype(i32)
    parity = ((jnp.cumsum((counts > 0).astype(i32)) - 1) % 2).astype(i32)
    xs = _dispatch(h2, dest, nb)
    ys = _moe_experts(xs, block_e, n_used.reshape(1), next_expert, parity, w1, w3, w2, layer, nb)
    return _combine(xn, ys, dest, routed, mods, g_out, n_tiles=n_tiles, batch=batch, seq=seq,
                    final_norm=final_norm)


HY_N2 = 128
HY_WB = 256


def _hy_filter_kernel(z_ref, w1_ref, b1_ref, fr_ref, w2_ref, b2_ref, w3_ref, dl_ref, k_ref, nrm_ref):
    i = pl.program_id(0)
    z = z_ref[...]
    fr = fr_ref[...]
    h = jnp.sin(fr[0:1] * (jnp.dot(z.astype(bf16), w1_ref[...], preferred_element_type=f32) + b1_ref[...]))
    h = jnp.sin(fr[1:2] * (jnp.dot(h.astype(bf16), w2_ref[...], preferred_element_type=f32) + b2_ref[...]))
    kk = jnp.dot(h.astype(bf16), w3_ref[...], preferred_element_type=f32)
    width = dl_ref.shape[1]
    decay = jnp.exp(-z[:, 0:1] * jnp.abs(dl_ref[...]))
    tl = z.shape[0]
    row = i * tl + lax.broadcasted_iota(i32, (tl, 1), 0)
    tot = []
    for blk in range(kk.shape[1] // width):
        kb = kk[:, blk * width:(blk + 1) * width] * decay
        if blk % 2 == 1:
            kb = jnp.where(row == 0, 0.0, kb)
        k_ref[:, blk * width:(blk + 1) * width] = kb
        tot.append(jnp.sum(jnp.abs(kb), axis=0, keepdims=True))
    s = jnp.concatenate(tot, axis=1)

    @pl.when(i == 0)
    def _():
        nrm_ref[...] = s

    @pl.when(i > 0)
    def _():
        nrm_ref[...] = nrm_ref[...] + s


def _hy_filters(seq, w1, b1, freq, w2, b2, w3):
    t = jnp.linspace(0.0, 1.0, seq, dtype=f32)[:, None]
    bands = (HY_EMB - 1) // 2
    w = 2 * math.pi * jnp.arange(seq, dtype=f32)[:, None] / seq
    f = jnp.linspace(1e-4, bands - 1, bands, dtype=f32)[None, :]
    z = jnp.concatenate([t, jnp.cos(f * w), -jnp.sin(f * w), jnp.zeros((seq, 128 - HY_EMB), f32)], axis=-1)
    w1p = jnp.concatenate([w1, jnp.zeros((128 - HY_EMB, w1.shape[1]), f32)], axis=0).astype(bf16)
    hid = w1.shape[1]
    n_out = w3.shape[1]
    deltas = jnp.linspace(math.log(HY_TARGET) / HY_SLOW_DECAY, math.log(HY_TARGET) / HY_FAST_DECAY,
                          HY_WIDTH, dtype=f32).reshape(1, HY_WIDTH)
    tl = 512
    const = lambda a: pl.BlockSpec(a.shape, lambda i: (0,) * a.ndim)
    ins = [z, w1p, b1.reshape(1, hid), freq, w2.astype(bf16), b2.reshape(1, hid), w3.astype(bf16), deltas]
    return pl.pallas_call(
        _hy_filter_kernel,
        grid=(seq // tl,),
        in_specs=[pl.BlockSpec((tl, 128), lambda i: (i, 0))] + [const(a) for a in ins[1:]],
        out_specs=[pl.BlockSpec((tl, n_out), lambda i: (i, 0)), pl.BlockSpec((1, n_out), lambda i: (0, 0))],
        out_shape=[jax.ShapeDtypeStruct((seq, n_out), f32), jax.ShapeDtypeStruct((1, n_out), f32)],
        compiler_params=_cparams(("arbitrary",)),
        name="hy_filter",
    )(*ins)


def _dft_tables(seq):
    n2 = HY_N2
    n_tot = 2 * seq
    n1 = n_tot // n2
    nf = n1 // 2 + 1
    nfp = -(-nf // 8) * 8
    f1 = jnp.arange(nfp, dtype=i32)
    k1 = jnp.arange(n1 // 2, dtype=i32)
    th = (2 * math.pi / n1) * ((f1[:, None] * k1[None, :]) % n1).astype(f32)
    valid = (f1 < nf)[:, None]
    fwd = jnp.concatenate([jnp.where(valid, jnp.cos(th), 0.0), jnp.where(valid, -jnp.sin(th), 0.0)], axis=0)
    eye = jnp.eye(8, dtype=f32)
    fk = jnp.einsum('rk,jh->rjkh', fwd, eye).reshape(2 * nfp * 8, (n1 // 2) * 8)
    cw = jnp.where(valid, jnp.where((f1 == 0) | (f1 == n1 // 2), 1.0, 2.0)[:, None], 0.0) / n_tot
    inv = jnp.concatenate([(cw * jnp.cos(th)).T, (-cw * jnp.sin(th)).T], axis=1)
    ck = jnp.einsum('kr,jh->kjrh', inv, eye).reshape((n1 // 2) * 8, 2 * nfp * 8)
    fa = jnp.arange(nf, dtype=i32)[:, None, None]
    f2 = jnp.arange(n2, dtype=i32)[None, :, None]
    m2 = jnp.arange(n2, dtype=i32)[None, None, :]
    ph = (2 * math.pi / n_tot) * ((f2 * m2 * n1 + m2 * fa) % n_tot).astype(f32)
    gr, gi = jnp.cos(ph), -jnp.sin(ph)
    g = jnp.concatenate([jnp.concatenate([gr, -gi], axis=2), jnp.concatenate([gi, gr], axis=2)], axis=1)
    hr, hi = jnp.swapaxes(jnp.cos(ph), 1, 2), jnp.swapaxes(jnp.sin(ph), 1, 2)
    h = jnp.concatenate([jnp.concatenate([hr, -hi], axis=2), jnp.concatenate([hi, hr], axis=2)], axis=1)
    return fk.astype(bf16), ck.astype(bf16), g.astype(bf16), h.astype(bf16), nf, nfp


def _hy_rows_per_step(nf, most=5):
    return max(d for d in range(most, 0, -1) if nf % d == 0)


def _dft_stage1(x_ref, fk_ref, a_scr):
    n1h, n2, wb = x_ref.shape

    def slab(s, carry):
        rows = pl.ds(pl.multiple_of(s * 8, 8), 8)
        xs = x_ref[:, rows, :].reshape(n1h * 8, wb).astype(bf16)
        a = jnp.dot(fk_ref[...], xs, preferred_element_type=f32)
        a_scr[:, rows, :] = a.reshape(a_scr.shape[0], 8, wb)
        return carry

    lax.fori_loop(0, n2 // 8, slab, 0)


def _dft_stage2(a_scr, g, f1, nfp):
    a = jnp.concatenate([a_scr[f1], a_scr[nfp + f1]], axis=0).astype(bf16)
    return jnp.dot(g, a, preferred_element_type=f32)


def _hy_spectrum_kernel(k_ref, fk_ref, g_ref, o_ref, a_scr, *, nfp):
    fs = pl.program_id(2)

    @pl.when(fs == 0)
    def _():
        _dft_stage1(k_ref, fk_ref, a_scr)

    fb = g_ref.shape[0]
    n2 = k_ref.shape[1]
    for j in range(fb):
        s = _dft_stage2(a_scr, g_ref[j], fs * fb + j, nfp)
        o_ref[0, j] = s[:n2]
        o_ref[1, j] = s[n2:]


def _hy_spectrum(k, fk, g, nf, nfp, seq):
    n2, wb = HY_N2, HY_WB
    n1h = seq // n2
    fb = _hy_rows_per_step(nf, 13)
    nwb = HY_WIDTH // wb
    n_blk = k.shape[1] // HY_WIDTH
    k3 = k.reshape(n1h, n2, k.shape[1])
    return pl.pallas_call(
        functools.partial(_hy_spectrum_kernel, nfp=nfp),
        grid=(n_blk, nwb, nf // fb),
        in_specs=[pl.BlockSpec((n1h, n2, wb), lambda o, w, f: (0, 0, o * nwb + w), pipeline_mode=pl.Buffered(1)),
                  pl.BlockSpec(fk.shape, lambda o, w, f: (0, 0)),
                  pl.BlockSpec((fb, 2 * n2, 2 * n2), lambda o, w, f: (f, 0, 0))],
        out_specs=pl.BlockSpec((None, 2, fb, n2, wb), lambda o, w, f: (o, 0, f, 0, w)),
        out_shape=jax.ShapeDtypeStruct((n_blk, 2, nf, n2, HY_WIDTH), f32),
        scratch_shapes=[pltpu.VMEM((2 * nfp, n2, wb), f32)],
        compiler_params=_cparams(("arbitrary",) * 3),
        name="hy_spectrum",
    )(k3, fk, g)


def _hy_conv_kernel(x_ref, gate_ref, kf_ref, kb_ref, g_ref, h_ref, fk_ref, ck_ref, bias_ref, inv_ref, o_ref, a_scr, *,
                    nfp):
    fs = pl.program_id(2)
    n1h, n2, wb = x_ref.shape

    @pl.when(fs == 0)
    def _():
        _dft_stage1(x_ref, fk_ref, a_scr)

    fb = g_ref.shape[0]
    for j in range(fb):
        f1 = fs * fb + j
        xs = _dft_stage2(a_scr, g_ref[j], f1, nfp)
        xr, xi = xs[:n2], xs[n2:]
        kr = kf_ref[0, j] + kb_ref[0, j]
        ki = kf_ref[1, j] - kb_ref[1, j]
        y = jnp.concatenate([xr * kr - xi * ki, xr * ki + xi * kr], axis=0).astype(bf16)
        bb = jnp.dot(h_ref[j], y, preferred_element_type=f32)
        a_scr[f1] = bb[:n2]
        a_scr[nfp + f1] = bb[n2:]

    @pl.when(fs == pl.num_programs(2) - 1)
    def _():
        bias = bias_ref[...].reshape(1, 1, wb)
        inv = inv_ref[...].reshape(1, 1, wb)

        def slab(s, carry):
            rows = pl.ds(pl.multiple_of(s * 8, 8), 8)
            bs = a_scr[:, rows, :].reshape(2 * nfp * 8, wb).astype(bf16)
            y = jnp.dot(ck_ref[...], bs, preferred_element_type=f32).reshape(n1h, 8, wb)
            o_ref[:, rows, :] = (gate_ref[:, rows, :] * (y * inv + x_ref[:, rows, :] * bias)).astype(o_ref.dtype)
            return carry

        lax.fori_loop(0, n2 // 8, slab, 0)


def _hy_conv(x, gate, kf, order, tabs, bias, inv_norm, out_dtype, *, batch, seq):
    fk, ck, g, h, nf, nfp = tabs
    n2, wb = HY_N2, HY_WB
    n1h = seq // n2
    fb = _hy_rows_per_step(nf)
    width = x.shape[1]
    big = pl.BlockSpec((n1h, n2, wb), lambda b, w, f: (b, 0, w), pipeline_mode=pl.Buffered(1))
    out = pl.pallas_call(
        functools.partial(_hy_conv_kernel, nfp=nfp),
        grid=(batch, width // wb, nf // fb),
        in_specs=[big, big,
                  pl.BlockSpec((None, 2, fb, n2, wb), lambda b, w, f: (2 * order, 0, f, 0, w)),
                  pl.BlockSpec((None, 2, fb, n2, wb), lambda b, w, f: (2 * order + 1, 0, f, 0, w)),
                  pl.BlockSpec((fb, 2 * n2, 2 * n2), lambda b, w, f: (f, 0, 0)),
                  pl.BlockSpec((fb, 2 * n2, 2 * n2), lambda b, w, f: (f, 0, 0)),
                  pl.BlockSpec(fk.shape, lambda b, w, f: (0, 0)),
                  pl.BlockSpec(ck.shape, lambda b, w, f: (0, 0)),
                  pl.BlockSpec((1, wb), lambda b, w, f: (0, w)),
                  pl.BlockSpec((1, wb), lambda b, w, f: (0, w))],
        out_specs=big,
        out_shape=jax.ShapeDtypeStruct((batch * n1h, n2, width), out_dtype),
        scratch_shapes=[pltpu.VMEM((2 * nfp, n2, wb), f32)],
        compiler_params=_cparams(("arbitrary",) * 3),
        name="hy_conv",
    )(x.reshape(-1, n2, width), gate.reshape(-1, n2, width), kf, kf, g, h, fk, ck,
      bias.reshape(1, width), inv_norm.reshape(1, width))
    return out.reshape(batch * seq, width)


def _hyena(v, x1, x2, f_w1, f_b1, f_freq, f_w2, f_b2, f_w3, hy_bias, *, batch, seq):
    k, sums = _hy_filters(seq, f_w1, f_b1, f_freq, f_w2, f_b2, f_w3)
    sums = sums.reshape(HY_ORDER, 2, HY_WIDTH)
    inv_norm = 1.0 / (sums[:, 0] + sums[:, 1])
    tabs = _dft_tables(seq)
    kf = _hy_spectrum(k, tabs[0], tabs[2], tabs[4], tabs[5], seq)
    z = _hy_conv(v, x1, kf, 0, tabs, hy_bias[0], inv_norm[0], f32, batch=batch, seq=seq)
    return _hy_conv(z, x2, kf, 1, tabs, hy_bias[1], inv_norm[1], bf16, batch=batch, seq=seq)


S5_TT = 256
S5_GB = 8


def _gelu_tanh(x):
    return 0.5 * x * (1.0 + jnp.tanh(0.7978845608028654 * (x + 0.044715 * x * x * x)))


def _s5_kernel(u_ref, perm_ref, permt_ref, wb_ref, wc_ref, are_ref, aim_ref, pre_ref, pim_ref, *rest, fuse_glu):
    if fuse_glu:
        yf_ref, d_ref, gw_ref, gb_ref, o_ref, st_scr, bu_scr, y_scr = rest
    else:
        o_ref, st_scr, bu_scr, y_scr = rest
    tt = u_ref.shape[0]
    steps = tt // 8
    half = bu_scr.shape[2] // 2

    @pl.when(pl.program_id(1) == 0)
    def _():
        st_scr[...] = jnp.zeros_like(st_scr)

    up = jnp.dot(perm_ref[...], u_ref[...].astype(bf16), preferred_element_type=f32).astype(bf16)
    lock = 8
    zero = jnp.zeros((8, half), f32)
    for g0 in range(0, wb_ref.shape[0], lock):
        gbs = range(g0, g0 + lock)
        for gb in gbs:
            bu_scr[gb] = jnp.dot(up[:, gb * 128:(gb + 1) * 128], wb_ref[gb], preferred_element_type=f32)
        lam = {gb: (jnp.broadcast_to(are_ref[gb], (8, half)), jnp.broadcast_to(aim_ref[gb], (8, half)))
               for gb in gbs}

        def scan_step(gb, k, h):
            hr, hi = h
            a_re, a_im = lam[gb]
            rows = pl.ds(k * 8, 8)
            nr = a_re * hr - a_im * hi + bu_scr[gb, rows, 0:half]
            ni = a_re * hi + a_im * hr + bu_scr[gb, rows, half:]
            bu_scr[gb, rows, 0:half] = nr
            bu_scr[gb, rows, half:] = ni
            return nr, ni

        h = {gb: (zero, zero) for gb in gbs}
        for k in range(steps):
            for gb in gbs:
                h[gb] = scan_step(gb, k, h[gb])

        carry = {}
        for gb in gbs:
            hr, hi = h[gb]
            pw_r = pre_ref[gb, steps - 1:steps, :]
            pw_i = pim_ref[gb, steps - 1:steps, :]
            cur_r = st_scr[gb, :, 0:half]
            cur_i = st_scr[gb, :, half:]
            car_r, car_i = [], []
            for j in range(8):
                car_r.append(cur_r)
                car_i.append(cur_i)
                cur_r, cur_i = (hr[j:j + 1] + pw_r * cur_r - pw_i * cur_i,
                                hi[j:j + 1] + pw_r * cur_i + pw_i * cur_r)
            st_scr[gb, :, 0:half] = cur_r
            st_scr[gb, :, half:] = cur_i
            carry[gb] = (jnp.concatenate(car_r, axis=0), jnp.concatenate(car_i, axis=0))

        for k in range(steps):
            for gb in gbs:
                cr, ci = carry[gb]
                rows = pl.ds(k * 8, 8)
                pr = pre_ref[gb, pl.ds(k, 1), :]
                pi = pim_ref[gb, pl.ds(k, 1), :]
                bu_scr[gb, rows, 0:half] = bu_scr[gb, rows, 0:half] + (pr * cr - pi * ci)
                bu_scr[gb, rows, half:] = bu_scr[gb, rows, half:] + (pr * ci + pi * cr)
        for gb in gbs:
            y_scr[:, gb * 128:(gb + 1) * 128] = jnp.dot(bu_scr[gb].astype(bf16), wc_ref[gb],
                                                          preferred_element_type=f32)
    yp = y_scr[...]
    y_hi = yp.astype(bf16)
    y_lo = (yp - y_hi.astype(f32)).astype(bf16)
    y = (jnp.dot(permt_ref[...], y_hi, preferred_element_type=f32)
         + jnp.dot(permt_ref[...], y_lo, preferred_element_type=f32))
    if fuse_glu:
        g = _gelu_tanh(y + yf_ref[...] + u_ref[...] * d_ref[...])
        z = jnp.dot(g.astype(bf16), gw_ref[...], preferred_element_type=f32) + gb_ref[...]
        o_ref[...] = (g * jax.nn.sigmoid(z)).astype(o_ref.dtype)
    else:
        o_ref[...] = y


def _s5_tables(a_re, a_im, log_dt, b_re, b_im, c_re, c_im, steps):
    ng, ns = a_re.shape
    nb = ng // S5_GB
    lam = lax.complex(a_re, a_im)
    dt = jnp.exp(log_dt)[:, None]
    lam_bar = jnp.exp(lam * dt)
    b_bar = ((lam_bar - 1) / lam)[..., None] * lax.complex(b_re, b_im)
    cm = lax.complex(c_re, c_im)
    pw = jnp.exp((lam * dt)[None] * jnp.arange(1, steps + 1, dtype=f32)[:, None, None])
    eye = jnp.eye(S5_GB, dtype=f32)
    bb = b_bar.reshape(nb, S5_GB, ns, S5_GROUP)
    wb = jnp.concatenate([jnp.einsum('bgpc,gh->bgchp', part, eye).reshape(nb, S5_GB * S5_GROUP, S5_GB * ns)
                          for part in (jnp.real(bb), jnp.imag(bb))], axis=-1)
    cc = cm.reshape(nb, S5_GB, S5_GROUP, ns)
    wc = jnp.concatenate([jnp.einsum('bgcp,gh->bgphc', part, eye).reshape(nb, S5_GB * ns, S5_GB * S5_GROUP)
                          for part in (jnp.real(cc), -jnp.imag(cc))], axis=1)
    lb = lam_bar.reshape(nb, 1, S5_GB * ns)
    pwb = pw.reshape(steps, nb, S5_GB * ns).transpose(1, 0, 2)
    return (wb.astype(bf16), wc.astype(bf16), jnp.real(lb), jnp.imag(lb), jnp.real(pwb), jnp.imag(pwb))


def _s5_perm(tt, reverse):
    steps = tt // 8
    dst = jnp.arange(tt)
    t = (dst % 8) * steps + dst // 8
    src = tt - 1 - t if reverse else t
    return jnp.arange(tt)[None, :] == src[:, None]


def _s5(u, a_re, a_im, log_dt, b_re, b_im, c_re, c_im, s5_d, glu_w, glu_b, *, batch, seq, ctx_len):
    r, width = u.shape
    tt = S5_TT
    assert ctx_len == tt and seq % tt == 0
    n_lat = seq // tt
    ctx_blk0 = (batch * seq) // tt
    y = None
    for di in range(2):
        rev = di == 1
        wb, wc, lr, li, pr, pi = _s5_tables(a_re[di], a_im[di], log_dt[di], b_re[di], b_im[di],
                                            c_re[di], c_im[di], tt // 8)
        perm = _s5_perm(tt, rev)

        def tile_map(b, i, rev=rev):
            lat = b * n_lat + (n_lat - i if rev else i - 1)
            return (jnp.where(i == 0, ctx_blk0 + b, lat), 0)

        row_spec = pl.BlockSpec((tt, width), tile_map)
        const = lambda a: pl.BlockSpec(a.shape, lambda b, i: (0,) * a.ndim)
        ins = [u, perm.astype(bf16), perm.T.astype(bf16), wb, wc, lr, li, pr, pi]
        in_specs = [row_spec] + [const(a) for a in ins[1:]]
        if rev:
            extra = [y, s5_d.reshape(1, width), glu_w.astype(bf16), glu_b.reshape(1, width)]
            in_specs += [row_spec] + [const(a) for a in extra[1:]]
            ins += extra
        y = pl.pallas_call(
            functools.partial(_s5_kernel, fuse_glu=rev),
            grid=(batch, n_lat + 1),
            in_specs=in_specs,
            out_specs=row_spec,
            out_shape=jax.ShapeDtypeStruct((r, width), bf16 if rev else f32),
            scratch_shapes=[pltpu.VMEM((wb.shape[0], 1, wb.shape[2]), f32),
                            pltpu.VMEM((wb.shape[0], tt, wb.shape[2]), f32),
                            pltpu.VMEM((tt, width), f32)],
            compiler_params=_cparams(("arbitrary", "arbitrary")),
            name="s5_bwd_glu" if rev else "s5_fwd",
        )(*ins)
    return y


def _rope_tables(seq):
    rows = seq // GRID_W
    row = jnp.repeat(jnp.arange(rows, dtype=f32), GRID_W)
    col = jnp.tile(jnp.arange(GRID_W, dtype=f32), rows)
    n_freq = DIFF_QK // 4
    inv = ROPE_BASE ** (-jnp.arange(n_freq, dtype=f32) / n_freq)
    ar, ac = row[:, None] * inv, col[:, None] * inv
    cos = jnp.concatenate([jnp.cos(ar), jnp.cos(ar), jnp.cos(ac), jnp.cos(ac)], axis=-1)
    sin = jnp.concatenate([-jnp.sin(ar), jnp.sin(ar), -jnp.sin(ac), jnp.sin(ac)], axis=-1)
    cos = jnp.concatenate([jnp.tile(cos, (1, 2)), jnp.ones((TM, 128), f32)], axis=0)
    sin = jnp.concatenate([jnp.tile(sin, (1, 2)), jnp.zeros((TM, 128), f32)], axis=0)
    return cos, sin


def kernel(x, c, ctx, c_ctx, ada_w, ada_b, norm_mix_g, norm_ffn_g, norm_out_g, ab_w_in, ab_w_out, diff_lam, diff_subln_g, pool_w, pool_scale, cd_w_in, cd_w_out, hy_conv_w, hy_conv_b, hy_f_w1, hy_f_b1, hy_f_freq, hy_f_w2, hy_f_b2, hy_f_w3, hy_bias, s5_a_re, s5_a_im, s5_log_dt, s5_b_re, s5_b_im, s5_c_re, s5_c_im, s5_d, glu_w, glu_b, moe_w_rg, moe_b_rg, moe_w_re, moe_b_re, moe_w1, moe_w3, moe_w2):
    bsz, seq, d = x.shape
    cl = ctx.shape[1]
    depth = ada_w.shape[0]
    n_lat = bsz * seq
    assert seq % TM == 0 and (bsz * cl) % TM == 0 and bsz + 1 <= 8
    stream = (x.reshape(n_lat, d), ctx.reshape(bsz * cl, d), 0)
    r = n_lat + bsz * cl
    cond = jnp.zeros((8, d), f32).at[0:bsz].set(c).at[bsz].set(c_ctx)
    mods = _ada_mods(cond, ada_w, ada_b)
    geom = dict(batch=bsz, seq=seq)

    for l in range(depth):
        last = l == depth - 1
        i = l // 2
        n_tiles = (n_lat if last else r) // TM
        g_mix = norm_mix_g[l].reshape(1, d)
        if l % 2 == 0:
            lam_init = 0.8 - 0.6 * math.exp(-0.3 * l)
            cos, sin = _rope_tables(seq)
            q, k, vt, p = _proj_rope(*stream, g_mix, mods[l], ab_w_in[i].astype(bf16), cos, sin, **geom)
            mix_a = _diff_attention(q, k, vt, diff_lam[i], diff_subln_g[i], lam_init, ctx_len=cl, **geom)
            mix_b = _pool(p, pool_w[i], pool_scale[i], ctx_len=cl, **geom)
            w_out = ab_w_out[i]
        else:
            assert stream[0] is stream[1]
            v, x1, x2, u = _proj_conv(stream[0], g_mix, mods[l], cd_w_in[i].astype(bf16), hy_conv_w[i], hy_conv_b[i],
                                      **geom)
            mix_a = _hyena(v, x1, x2, hy_f_w1[i], hy_f_b1[i], hy_f_freq[i],
                           hy_f_w2[i], hy_f_b2[i], hy_f_w3[i], hy_bias[i], **geom)
            mix_b = _s5(u, s5_a_re[i], s5_a_im[i], s5_log_dt[i], s5_b_re[i], s5_b_im[i],
                        s5_c_re[i], s5_c_im[i], s5_d[i], glu_w[i], glu_b[i], ctx_len=cl, **geom)
            w_out = cd_w_out[i]
        w_r = jnp.zeros((d, 128), f32).at[:, :MOE_GROUPS].set(moe_w_rg[l]) \
            .at[:, MOE_GROUPS:MOE_GROUPS + MOE_EXPERTS].set(moe_w_re[l]).astype(bf16)
        b_r = jnp.zeros((1, 128), f32).at[0, :MOE_GROUPS].set(moe_b_rg[l]) \
            .at[0, MOE_GROUPS:MOE_GROUPS + MOE_EXPERTS].set(moe_b_re[l])
        xn, h2, logits = _outproj(mix_a, mix_b, *stream, w_out.astype(bf16), mods[l], norm_ffn_g[l].reshape(1, d),
                                  w_r, b_r, n_tiles=n_tiles, **geom)
        xs = _moe(xn, h2, logits, mods[l], moe_w1, moe_w3, moe_w2, l, norm_out_g.reshape(1, d),
                  n_tiles=n_tiles, final_norm=last, **geom)
        stream = (xs, xs, n_lat // TM)
    return xs[:n_lat].reshape(bsz, seq, d)
```
